```python
import jax, jax.numpy as jnp
from jax import lax
import numpy as np

D_MODEL = 1024
BATCH = 8
SEQ = 16384
DEPTH = 2

N_EVEN = (DEPTH + 1) // 2
N_ODD = DEPTH // 2

POOL_DIM = D_MODEL // 2
POOL_WINDOWS = (2, 4, 8, 16)
N_POOL_GROUPS = len(POOL_WINDOWS)
POOL_GROUP_DIM = POOL_DIM // N_POOL_GROUPS

MLA_HEADS = 8
QK_NOPE_DIM = 64
QK_ROPE_DIM = 32
QK_DIM = QK_NOPE_DIM + QK_ROPE_DIM
V_HEAD_DIM = 64
Q_LORA_RANK = 256
KV_LORA_RANK = 128
ROPE_BASE = 10000.0
Q_BLOCK = 128

EVEN_IN_DIM = POOL_DIM + Q_LORA_RANK + KV_LORA_RANK + QK_ROPE_DIM
EVEN_MIX_DIM = POOL_DIM + MLA_HEADS * V_HEAD_DIM

LRU_WIDTH = D_MODEL
LRU_HEADS = 4
LRU_HEAD_DIM = LRU_WIDTH // LRU_HEADS
CONV_WIDTH = 4
LRU_C = 8.0

MEM_TOKENS = 256
MEM_HEADS = 4
MEM_HEAD_DIM = D_MODEL // MEM_HEADS

D_FF = -(-8 * D_MODEL // (3 * 256)) * 256

RMS_EPS = 1e-6
NEG_INF = -1e30

kernel_name = "hybrid_pool_mla_rglru_memxattn"


def rms_norm(x, g):
    xf = x.astype(jnp.float32)
    y = xf * lax.rsqrt(jnp.mean(xf * xf, axis=-1, keepdims=True) + RMS_EPS)
    return (y * g).astype(x.dtype)


def rope_tables(positions):
    inv_freq = ROPE_BASE ** (-jnp.arange(0, QK_ROPE_DIM, 2, dtype=jnp.float32) / QK_ROPE_DIM)
    ang = positions.astype(jnp.float32)[..., None] * inv_freq
    return jnp.cos(ang), jnp.sin(ang)


def apply_rope(t, cos, sin):
    t1, t2 = jnp.split(t.astype(jnp.float32), 2, axis=-1)
    out = jnp.concatenate([t1 * cos - t2 * sin, t2 * cos + t1 * sin], axis=-1)
    return out.astype(t.dtype)


def pool_mixer(u, pool_w, pool_scale):
    B, S, _ = u.shape
    ug = u.reshape(B, S, N_POOL_GROUPS, POOL_GROUP_DIM)
    uf = ug.astype(jnp.float32)
    csum = jnp.concatenate([jnp.zeros((B, 1, N_POOL_GROUPS, POOL_GROUP_DIM), jnp.float32),
                            jnp.cumsum(uf, axis=1)], axis=1)
    t = jnp.arange(S)
    means = []
    for g, w in enumerate(POOL_WINDOWS):
        lo = jnp.maximum(t + 1 - w, 0)
        win_sum = csum[:, 1:, g] - csum[:, lo, g]
        cnt = jnp.minimum(t + 1, w).astype(jnp.float32)
        means.append(win_sum / cnt[None, :, None])
    pooled = (jnp.stack(means, axis=2) - uf).astype(u.dtype)
    y = jnp.einsum('bsgc,gcd->bsgd', pooled, pool_w).reshape(B, S, POOL_DIM)
    return y * pool_scale.astype(y.dtype)


def mla_causal_attention(q_nope, q_rope, k_nope, k_rope, v):
    B, S, H, _ = q_nope.shape
    nb = S // Q_BLOCK
    qn = q_nope.reshape(B, nb, Q_BLOCK, H, QK_NOPE_DIM).transpose(1, 0, 2, 3, 4)
    qr = q_rope.reshape(B, nb, Q_BLOCK, H, QK_ROPE_DIM).transpose(1, 0, 2, 3, 4)
    starts = jnp.arange(nb, dtype=jnp.int32) * Q_BLOCK
    kpos = jnp.arange(S, dtype=jnp.int32)
    scale = QK_DIM ** -0.5

    def one_block(args):
        qn_b, qr_b, start = args
        s = (jnp.einsum('bqhd,bkhd->bhqk', qn_b, k_nope).astype(jnp.float32)
             + jnp.einsum('bqhr,bkr->bhqk', qr_b, k_rope).astype(jnp.float32)) * scale
        qpos = start + jnp.arange(Q_BLOCK, dtype=jnp.int32)
        mask = kpos[None, :] <= qpos[:, None]
        s = jnp.where(mask[None, None], s, NEG_INF)
        p = jax.nn.softmax(s, axis=-1).astype(v.dtype)
        return jnp.einsum('bhqk,bkhd->bqhd', p, v)

    out = lax.map(one_block, (qn, qr, starts))
    return out.transpose(1, 0, 2, 3, 4).reshape(B, S, H * V_HEAD_DIM)


def even_mixer(h, cos, sin, w_in, pool_w, pool_scale, q_norm, w_q_up, kv_norm, w_kv_up, w_out):
    B, S, _ = h.shape
    z = h @ w_in
    u, cq, ckv, kr = jnp.split(z, [POOL_DIM, POOL_DIM + Q_LORA_RANK,
                                   POOL_DIM + Q_LORA_RANK + KV_LORA_RANK], axis=-1)
    y_pool = pool_mixer(u, pool_w, pool_scale)
    q = (rms_norm(cq, q_norm) @ w_q_up).reshape(B, S, MLA_HEADS, QK_DIM)
    q_nope, q_rope = jnp.split(q, [QK_NOPE_DIM], axis=-1)
    kv = (rms_norm(ckv, kv_norm) @ w_kv_up).reshape(B, S, MLA_HEADS, QK_NOPE_DIM + V_HEAD_DIM)
    k_nope, v = jnp.split(kv, [QK_NOPE_DIM], axis=-1)
    q_rope = apply_rope(q_rope, cos[:, :, None, :], sin[:, :, None, :])
    k_rope = apply_rope(kr, cos, sin)
    y_att = mla_causal_attention(q_nope, q_rope, k_nope, k_rope, v)
    return jnp.concatenate([y_pool, y_att], axis=-1) @ w_out


def causal_depthwise_conv(xb, conv_w, conv_b):
    y = lax.conv_general_dilated(xb, conv_w[:, None, :].astype(xb.dtype), window_strides=(1,),
                                 padding=((CONV_WIDTH - 1, 0),),
                                 dimension_numbers=('NWC', 'WIO', 'NWC'),
                                 feature_group_count=xb.shape[-1])
    return y + conv_b.astype(y.dtype)


def linear_scan_combine(c1, c2):
    a1, b1 = c1
    a2, b2 = c2
    return a1 * a2, a2 * b1 + b2


def odd_mixer(h, reset, w_in, conv_w, conv_b, w_rgate, b_rgate, w_igate, b_igate, lam, w_out):
    B, S, _ = h.shape
    z = h @ w_in
    gate_branch, xb = jnp.split(z, [LRU_WIDTH], axis=-1)
    xb = causal_depthwise_conv(xb, conv_w, conv_b)
    xg = xb.reshape(B, S, LRU_HEADS, LRU_HEAD_DIM)
    r = jax.nn.sigmoid(jnp.einsum('bshc,hcd->bshd', xg, w_rgate).reshape(B, S, LRU_WIDTH) + b_rgate)
    i = jax.nn.sigmoid(jnp.einsum('bshc,hcd->bshd', xg, w_igate).reshape(B, S, LRU_WIDTH) + b_igate)
    log_a = -LRU_C * r.astype(jnp.float32) * jax.nn.softplus(-lam.astype(jnp.float32))
    a = jnp.exp(log_a)
    mult = jnp.sqrt(jnp.maximum(-jnp.expm1(2.0 * log_a), 0.0))
    a = jnp.where(reset, 0.0, a)
    mult = jnp.where(reset, 1.0, mult)
    b = mult * (i * xb).astype(jnp.float32)
    _, hseq = lax.associative_scan(linear_scan_combine, (a, b), axis=1)
    y = jax.nn.gelu(gate_branch) * hseq.astype(h.dtype)
    return y @ w_out


def mem_cross_attention(h, mem, norm_mem, w_q, w_kv, w_o):
    B, S, _ = h.shape
    m = rms_norm(mem, norm_mem)
    q = (h @ w_q).reshape(B, S, MEM_HEADS, MEM_HEAD_DIM)
    k, v = jnp.split(m @ w_kv, 2, axis=-1)
    k = k.reshape(B, -1, MEM_HEADS, MEM_HEAD_DIM)
    v = v.reshape(B, -1, MEM_HEADS, MEM_HEAD_DIM)
    s = jnp.einsum('bqhd,bkhd->bhqk', q, k).astype(jnp.float32) * (MEM_HEAD_DIM ** -0.5)
    p = jax.nn.softmax(s, axis=-1).astype(v.dtype)
    o = jnp.einsum('bhqk,bkhd->bqhd', p, v).reshape(B, S, D_MODEL)
    return o @ w_o


def swiglu(h, w_gate_up, w_down):
    g, u = jnp.split(h @ w_gate_up, 2, axis=-1)
    return (jax.nn.silu(g) * u) @ w_down


def _fwd_setup_inputs(seed: int = 0) -> dict:
    key = jax.random.key(seed)
    ks = iter(jax.random.split(key, 48))
    f32 = jnp.float32

    def w(shape, fan_in):
        return jax.random.normal(next(ks), shape, f32) * fan_in ** -0.5

    def gain(shape):
        return 1.0 + 0.02 * jax.random.normal(next(ks), shape, f32)

    def bias(shape):
        return 0.02 * jax.random.normal(next(ks), shape, f32)

    E, O, L = N_EVEN, N_ODD, DEPTH
    x = jax.random.normal(next(ks), (BATCH, SEQ, D_MODEL), f32)
    mem = jax.random.normal(next(ks), (BATCH, MEM_TOKENS, D_MODEL), f32)
    positions = jnp.broadcast_to(jnp.arange(SEQ, dtype=jnp.int32), (BATCH, SEQ))
    a_c = jax.random.uniform(next(ks), (O, LRU_WIDTH), f32, 0.9, 0.999)
    s_l = a_c ** (1.0 / LRU_C)
    lam = jnp.log(s_l) - jnp.log1p(-s_l)
    return {
        "x": x,
        "mem": mem,
        "positions": positions,
        "ev_norm": gain((E, D_MODEL)),
        "ev_w_in": w((E, D_MODEL, EVEN_IN_DIM), D_MODEL),
        "ev_pool_w": w((E, N_POOL_GROUPS, POOL_GROUP_DIM, POOL_GROUP_DIM), POOL_GROUP_DIM),
        "ev_pool_scale": gain((E, POOL_DIM)),
        "ev_q_norm": gain((E, Q_LORA_RANK)),
        "ev_w_q_up": w((E, Q_LORA_RANK, MLA_HEADS * QK_DIM), Q_LORA_RANK),
        "ev_kv_norm": gain((E, KV_LORA_RANK)),
        "ev_w_kv_up": w((E, KV_LORA_RANK, MLA_HEADS * (QK_NOPE_DIM + V_HEAD_DIM)), KV_LORA_RANK),
        "ev_w_out": w((E, EVEN_MIX_DIM, D_MODEL), EVEN_MIX_DIM),
        "od_norm": gain((O, D_MODEL)),
        "od_w_in": w((O, D_MODEL, 2 * LRU_WIDTH), D_MODEL),
        "od_conv_w": w((O, CONV_WIDTH, LRU_WIDTH), CONV_WIDTH),
        "od_conv_b": bias((O, LRU_WIDTH)),
        "od_w_rgate": w((O, LRU_HEADS, LRU_HEAD_DIM, LRU_HEAD_DIM), LRU_HEAD_DIM),
        "od_b_rgate": bias((O, LRU_WIDTH)),
        "od_w_igate": w((O, LRU_HEADS, LRU_HEAD_DIM, LRU_HEAD_DIM), LRU_HEAD_DIM),
        "od_b_igate": bias((O, LRU_WIDTH)),
        "od_lambda": lam,
        "od_w_out": w((O, LRU_WIDTH, D_MODEL), LRU_WIDTH),
        "xa_norm_x": gain((L, D_MODEL)),
        "xa_norm_mem": gain((L, D_MODEL)),
        "xa_w_q": w((L, D_MODEL, D_MODEL), D_MODEL),
        "xa_w_kv": w((L, D_MODEL, 2 * D_MODEL), D_MODEL),
        "xa_w_o": w((L, D_MODEL, D_MODEL), D_MODEL),
        "ffn_norm": gain((L, D_MODEL)),
        "ffn_w_gate_up": w((L, D_MODEL, 2 * D_FF), D_MODEL),
        "ffn_w_down": w((L, D_FF, D_MODEL), D_FF),
        "final_norm": gain((D_MODEL,)),
    }


def _fwd_reference(x, mem, positions,
              ev_norm, ev_w_in, ev_pool_w, ev_pool_scale, ev_q_norm, ev_w_q_up,
              ev_kv_norm, ev_w_kv_up, ev_w_out,
              od_norm, od_w_in, od_conv_w, od_conv_b, od_w_rgate, od_b_rgate,
              od_w_igate, od_b_igate, od_lambda, od_w_out,
              xa_norm_x, xa_norm_mem, xa_w_q, xa_w_kv, xa_w_o,
              ffn_norm, ffn_w_gate_up, ffn_w_down, final_norm):
    cos, sin = rope_tables(positions)
    reset = (positions == 0)[..., None]
    for layer in range(DEPTH):
        j = layer // 2
        if layer % 2 == 0:
            h = rms_norm(x, ev_norm[j])
            x = x + even_mixer(h, cos, sin, ev_w_in[j], ev_pool_w[j], ev_pool_scale[j],
                               ev_q_norm[j], ev_w_q_up[j], ev_kv_norm[j], ev_w_kv_up[j],
                               ev_w_out[j])
        else:
            h = rms_norm(x, od_norm[j])
            x = x + odd_mixer(h, reset, od_w_in[j], od_conv_w[j], od_conv_b[j],
                              od_w_rgate[j], od_b_rgate[j], od_w_igate[j], od_b_igate[j],
                              od_lambda[j], od_w_out[j])
        x = x + mem_cross_attention(rms_norm(x, xa_norm_x[layer]), mem, xa_norm_mem[layer],
                                    xa_w_q[layer], xa_w_kv[layer], xa_w_o[layer])
        x = x + swiglu(rms_norm(x, ffn_norm[layer]), ffn_w_gate_up[layer], ffn_w_down[layer])
    return rms_norm(x, final_norm)


import jax as _jax
import jax.numpy as _jnp

TWIN_FORMAT = 'train_step'
FWD_PARAMS = ['x', 'mem', 'positions', 'ev_norm', 'ev_w_in', 'ev_pool_w', 'ev_pool_scale', 'ev_q_norm', 'ev_w_q_up', 'ev_kv_norm', 'ev_w_kv_up', 'ev_w_out', 'od_norm', 'od_w_in', 'od_conv_w', 'od_conv_b', 'od_w_rgate', 'od_b_rgate', 'od_w_igate', 'od_b_igate', 'od_lambda', 'od_w_out', 'xa_norm_x', 'xa_norm_mem', 'xa_w_q', 'xa_w_kv', 'xa_w_o', 'ffn_norm', 'ffn_w_gate_up', 'ffn_w_down', 'final_norm']
TWIN_WEIGHTS = ['ev_norm', 'ev_w_in', 'ev_pool_w', 'ev_pool_scale', 'ev_q_norm', 'ev_w_q_up', 'ev_kv_norm', 'ev_w_kv_up', 'ev_w_out', 'od_norm', 'od_w_in', 'od_conv_w', 'od_conv_b', 'od_w_rgate', 'od_b_rgate', 'od_w_igate', 'od_b_igate', 'od_lambda', 'od_w_out', 'xa_norm_x', 'xa_norm_mem', 'xa_w_q', 'xa_w_kv', 'xa_w_o', 'ffn_norm', 'ffn_w_gate_up', 'ffn_w_down', 'final_norm']
TWIN_DIFF_INPUT = 'x'
TWIN_INPUTS = ['x', 'mem', 'positions', 'ev_norm', 'ev_w_in', 'ev_pool_w', 'ev_pool_scale', 'ev_q_norm', 'ev_w_q_up', 'ev_kv_norm', 'ev_w_kv_up', 'ev_w_out', 'od_norm', 'od_w_in', 'od_conv_w', 'od_conv_b', 'od_w_rgate', 'od_b_rgate', 'od_w_igate', 'od_b_igate', 'od_lambda', 'od_w_out', 'xa_norm_x', 'xa_norm_mem', 'xa_w_q', 'xa_w_kv', 'xa_w_o', 'ffn_norm', 'ffn_w_gate_up', 'ffn_w_down', 'final_norm', 'loss_target', 'm_ev_norm', 'm_ev_w_in', 'm_ev_pool_w', 'm_ev_pool_scale', 'm_ev_q_norm', 'm_ev_w_q_up', 'm_ev_kv_norm', 'm_ev_w_kv_up', 'm_ev_w_out', 'm_od_norm', 'm_od_w_in', 'm_od_conv_w', 'm_od_conv_b', 'm_od_w_rgate', 'm_od_b_rgate', 'm_od_w_igate', 'm_od_b_igate', 'm_od_lambda', 'm_od_w_out', 'm_xa_norm_x', 'm_xa_norm_mem', 'm_xa_w_q', 'm_xa_w_kv', 'm_xa_w_o', 'm_ffn_norm', 'm_ffn_w_gate_up', 'm_ffn_w_down', 'm_final_norm', 'v_ev_norm', 'v_ev_w_in', 'v_ev_pool_w', 'v_ev_pool_scale', 'v_ev_q_norm', 'v_ev_w_q_up', 'v_ev_kv_norm', 'v_ev_w_kv_up', 'v_ev_w_out', 'v_od_norm', 'v_od_w_in', 'v_od_conv_w', 'v_od_conv_b', 'v_od_w_rgate', 'v_od_b_rgate', 'v_od_w_igate', 'v_od_b_igate', 'v_od_lambda', 'v_od_w_out', 'v_xa_norm_x', 'v_xa_norm_mem', 'v_xa_w_q', 'v_xa_w_kv', 'v_xa_w_o', 'v_ffn_norm', 'v_ffn_w_gate_up', 'v_ffn_w_down', 'v_final_norm']
TWIN_OUTPUTS = ['loss', 'grad_x', 'grad_ev_norm', 'grad_ev_w_in', 'grad_ev_pool_w', 'grad_ev_pool_scale', 'grad_ev_q_norm', 'grad_ev_w_q_up', 'grad_ev_kv_norm', 'grad_ev_w_kv_up', 'grad_ev_w_out', 'grad_od_norm', 'grad_od_w_in', 'grad_od_conv_w', 'grad_od_conv_b', 'grad_od_w_rgate', 'grad_od_b_rgate', 'grad_od_w_igate', 'grad_od_b_igate', 'grad_od_lambda', 'grad_od_w_out', 'grad_xa_norm_x', 'grad_xa_norm_mem', 'grad_xa_w_q', 'grad_xa_w_kv', 'grad_xa_w_o', 'grad_ffn_norm', 'grad_ffn_w_gate_up', 'grad_ffn_w_down', 'grad_final_norm', 'delta_ev_norm', 'delta_ev_w_in', 'delta_ev_pool_w', 'delta_ev_pool_scale', 'delta_ev_q_norm', 'delta_ev_w_q_up', 'delta_ev_kv_norm', 'delta_ev_w_kv_up', 'delta_ev_w_out', 'delta_od_norm', 'delta_od_w_in', 'delta_od_conv_w', 'delta_od_conv_b', 'delta_od_w_rgate', 'delta_od_b_rgate', 'delta_od_w_igate', 'delta_od_b_igate', 'delta_od_lambda', 'delta_od_w_out', 'delta_xa_norm_x', 'delta_xa_norm_mem', 'delta_xa_w_q', 'delta_xa_w_kv', 'delta_xa_w_o', 'delta_ffn_norm', 'delta_ffn_w_gate_up', 'delta_ffn_w_down', 'delta_final_norm', 'new_m_ev_norm', 'new_m_ev_w_in', 'new_m_ev_pool_w', 'new_m_ev_pool_scale', 'new_m_ev_q_norm', 'new_m_ev_w_q_up', 'new_m_ev_kv_norm', 'new_m_ev_w_kv_up', 'new_m_ev_w_out', 'new_m_od_norm', 'new_m_od_w_in', 'new_m_od_conv_w', 'new_m_od_conv_b', 'new_m_od_w_rgate', 'new_m_od_b_rgate', 'new_m_od_w_igate', 'new_m_od_b_igate', 'new_m_od_lambda', 'new_m_od_w_out', 'new_m_xa_norm_x', 'new_m_xa_norm_mem', 'new_m_xa_w_q', 'new_m_xa_w_kv', 'new_m_xa_w_o', 'new_m_ffn_norm', 'new_m_ffn_w_gate_up', 'new_m_ffn_w_down', 'new_m_final_norm', 'new_v_ev_norm', 'new_v_ev_w_in', 'new_v_ev_pool_w', 'new_v_ev_pool_scale', 'new_v_ev_q_norm', 'new_v_ev_w_q_up', 'new_v_ev_kv_norm', 'new_v_ev_w_kv_up', 'new_v_ev_w_out', 'new_v_od_norm', 'new_v_od_w_in', 'new_v_od_conv_w', 'new_v_od_conv_b', 'new_v_od_w_rgate', 'new_v_od_b_rgate', 'new_v_od_w_igate', 'new_v_od_b_igate', 'new_v_od_lambda', 'new_v_od_w_out', 'new_v_xa_norm_x', 'new_v_xa_norm_mem', 'new_v_xa_w_q', 'new_v_xa_w_kv', 'new_v_xa_w_o', 'new_v_ffn_norm', 'new_v_ffn_w_gate_up', 'new_v_ffn_w_down', 'new_v_final_norm']
TWIN_LEAF_KINDS = {'loss': 'loss', 'grad_x': 'grad_x', 'grad_ev_norm': 'grad_w', 'grad_ev_w_in': 'grad_w', 'grad_ev_pool_w': 'grad_w', 'grad_ev_pool_scale': 'grad_w', 'grad_ev_q_norm': 'grad_w', 'grad_ev_w_q_up': 'grad_w', 'grad_ev_kv_norm': 'grad_w', 'grad_ev_w_kv_up': 'grad_w', 'grad_ev_w_out': 'grad_w', 'grad_od_norm': 'grad_w', 'grad_od_w_in': 'grad_w', 'grad_od_conv_w': 'grad_w', 'grad_od_conv_b': 'grad_w', 'grad_od_w_rgate': 'grad_w', 'grad_od_b_rgate': 'grad_w', 'grad_od_w_igate': 'grad_w', 'grad_od_b_igate': 'grad_w', 'grad_od_lambda': 'grad_w', 'grad_od_w_out': 'grad_w', 'grad_xa_norm_x': 'grad_w', 'grad_xa_norm_mem': 'grad_w', 'grad_xa_w_q': 'grad_w', 'grad_xa_w_kv': 'grad_w', 'grad_xa_w_o': 'grad_w', 'grad_ffn_norm': 'grad_w', 'grad_ffn_w_gate_up': 'grad_w', 'grad_ffn_w_down': 'grad_w', 'grad_final_norm': 'grad_w', 'delta_ev_norm': 'delta_w', 'delta_ev_w_in': 'delta_w', 'delta_ev_pool_w': 'delta_w', 'delta_ev_pool_scale': 'delta_w', 'delta_ev_q_norm': 'delta_w', 'delta_ev_w_q_up': 'delta_w', 'delta_ev_kv_norm': 'delta_w', 'delta_ev_w_kv_up': 'delta_w', 'delta_ev_w_out': 'delta_w', 'delta_od_norm': 'delta_w', 'delta_od_w_in': 'delta_w', 'delta_od_conv_w': 'delta_w', 'delta_od_conv_b': 'delta_w', 'delta_od_w_rgate': 'delta_w', 'delta_od_b_rgate': 'delta_w', 'delta_od_w_igate': 'delta_w', 'delta_od_b_igate': 'delta_w', 'delta_od_lambda': 'delta_w', 'delta_od_w_out': 'delta_w', 'delta_xa_norm_x': 'delta_w', 'delta_xa_norm_mem': 'delta_w', 'delta_xa_w_q': 'delta_w', 'delta_xa_w_kv': 'delta_w', 'delta_xa_w_o': 'delta_w', 'delta_ffn_norm': 'delta_w', 'delta_ffn_w_gate_up': 'delta_w', 'delta_ffn_w_down': 'delta_w', 'delta_final_norm': 'delta_w', 'new_m_ev_norm': 'new_m', 'new_m_ev_w_in': 'new_m', 'new_m_ev_pool_w': 'new_m', 'new_m_ev_pool_scale': 'new_m', 'new_m_ev_q_norm': 'new_m', 'new_m_ev_w_q_up': 'new_m', 'new_m_ev_kv_norm': 'new_m', 'new_m_ev_w_kv_up': 'new_m', 'new_m_ev_w_out': 'new_m', 'new_m_od_norm': 'new_m', 'new_m_od_w_in': 'new_m', 'new_m_od_conv_w': 'new_m', 'new_m_od_conv_b': 'new_m', 'new_m_od_w_rgate': 'new_m', 'new_m_od_b_rgate': 'new_m', 'new_m_od_w_igate': 'new_m', 'new_m_od_b_igate': 'new_m', 'new_m_od_lambda': 'new_m', 'new_m_od_w_out': 'new_m', 'new_m_xa_norm_x': 'new_m', 'new_m_xa_norm_mem': 'new_m', 'new_m_xa_w_q': 'new_m', 'new_m_xa_w_kv': 'new_m', 'new_m_xa_w_o': 'new_m', 'new_m_ffn_norm': 'new_m', 'new_m_ffn_w_gate_up': 'new_m', 'new_m_ffn_w_down': 'new_m', 'new_m_final_norm': 'new_m', 'new_v_ev_norm': 'new_v', 'new_v_ev_w_in': 'new_v', 'new_v_ev_pool_w': 'new_v', 'new_v_ev_pool_scale': 'new_v', 'new_v_ev_q_norm': 'new_v', 'new_v_ev_w_q_up': 'new_v', 'new_v_ev_kv_norm': 'new_v', 'new_v_ev_w_kv_up': 'new_v', 'new_v_ev_w_out': 'new_v', 'new_v_od_norm': 'new_v', 'new_v_od_w_in': 'new_v', 'new_v_od_conv_w': 'new_v', 'new_v_od_conv_b': 'new_v', 'new_v_od_w_rgate': 'new_v', 'new_v_od_b_rgate': 'new_v', 'new_v_od_w_igate': 'new_v', 'new_v_od_b_igate': 'new_v', 'new_v_od_lambda': 'new_v', 'new_v_od_w_out': 'new_v', 'new_v_xa_norm_x': 'new_v', 'new_v_xa_norm_mem': 'new_v', 'new_v_xa_w_q': 'new_v', 'new_v_xa_w_kv': 'new_v', 'new_v_xa_w_o': 'new_v', 'new_v_ffn_norm': 'new_v', 'new_v_ffn_w_gate_up': 'new_v', 'new_v_ffn_w_down': 'new_v', 'new_v_final_norm': 'new_v'}


def _forward(args):
    return _fwd_reference(*[args[k] for k in FWD_PARAMS])


def _output_shape():
    def fwd():
        inp = _fwd_setup_inputs(0)
        return _fwd_reference(*[inp[k] for k in FWD_PARAMS])
    out = _jax.eval_shape(fwd)
    return out.shape, out.dtype

N_MICROBATCH = 1
ADAM_LR = 0.001
ADAM_B1 = 0.9
ADAM_B2 = 0.999
ADAM_EPS = 1e-08
ADAM_WD = 0.01
ADAM_STEP = 10
PER_EXAMPLE_BATCH_AXIS = {'x': 0, 'mem': 0, 'positions': 0, 'loss_target': 0}
SHARED_INPUTS = []
_WEIGHT_DTYPES = {'ev_norm': _jnp.float32, 'ev_w_in': _jnp.float32, 'ev_pool_w': _jnp.float32, 'ev_pool_scale': _jnp.float32, 'ev_q_norm': _jnp.float32, 'ev_w_q_up': _jnp.float32, 'ev_kv_norm': _jnp.float32, 'ev_w_kv_up': _jnp.float32, 'ev_w_out': _jnp.float32, 'od_norm': _jnp.float32, 'od_w_in': _jnp.float32, 'od_conv_w': _jnp.float32, 'od_conv_b': _jnp.float32, 'od_w_rgate': _jnp.float32, 'od_b_rgate': _jnp.float32, 'od_w_igate': _jnp.float32, 'od_b_igate': _jnp.float32, 'od_lambda': _jnp.float32, 'od_w_out': _jnp.float32, 'xa_norm_x': _jnp.float32, 'xa_norm_mem': _jnp.float32, 'xa_w_q': _jnp.float32, 'xa_w_kv': _jnp.float32, 'xa_w_o': _jnp.float32, 'ffn_norm': _jnp.float32, 'ffn_w_gate_up': _jnp.float32, 'ffn_w_down': _jnp.float32, 'final_norm': _jnp.float32}
MOMENT_SCALE = {'ev_norm': 2.474111e-01, 'ev_w_in': 2.574917e-01, 'ev_pool_w': 3.370228e-01, 'ev_pool_scale': 3.274308e-01, 'ev_q_norm': 9.646495e-02, 'ev_w_q_up': 5.646459e-02, 'ev_kv_norm': 2.276296e-01, 'ev_w_kv_up': 8.230692e-02, 'ev_w_out': 2.354986e-01, 'od_norm': 1.630753e-01, 'od_w_in': 1.265220e-01, 'od_conv_w': 1.376395e-01, 'od_conv_b': 1.606848e+00, 'od_w_rgate': 3.649868e-02, 'od_b_rgate': 3.160090e-02, 'od_w_igate': 6.518776e-02, 'od_b_igate': 4.995977e-02, 'od_lambda': 6.404220e-02, 'od_w_out': 1.243706e-01, 'xa_norm_x': 3.457255e-02, 'xa_norm_mem': 6.003965e-02, 'xa_w_q': 3.438712e-02, 'xa_w_kv': 4.671083e-02, 'xa_w_o': 5.852006e-02, 'ffn_norm': 2.480826e-01, 'ffn_w_gate_up': 9.967218e-02, 'ffn_w_down': 1.625412e-01, 'final_norm': 1.280719e+02}


def _to_microbatches(a, axis):
    t = _jnp.moveaxis(a, axis, 0)
    t = t.reshape((N_MICROBATCH, t.shape[0] // N_MICROBATCH) + t.shape[1:])
    return _jnp.moveaxis(t, 1, axis + 1)


def setup_inputs(seed: int = 0) -> dict:
    inp = _fwd_setup_inputs(seed)
    key = _jax.random.fold_in(_jax.random.key(seed), 7919)
    shape, _ = _output_shape()
    out = dict(inp)
    out["loss_target"] = _jax.random.normal(_jax.random.fold_in(key, 0), shape, _jnp.float32)
    for i, name in enumerate(TWIN_WEIGHTS):
        w = inp[name].astype(_jnp.float32)
        if MOMENT_SCALE is None:
            s = _jnp.sqrt(_jnp.mean(_jnp.square(w)) + 1e-30)
        else:
            s = MOMENT_SCALE[name]
        km, kv = _jax.random.split(_jax.random.fold_in(key, i + 1))
        out[name] = w
        out["m_" + name] = s * _jax.random.normal(km, w.shape, _jnp.float32)
        out["v_" + name] = (s * s) * _jax.random.uniform(kv, w.shape, _jnp.float32, 0.5, 1.5)
    if N_MICROBATCH > 1:
        for name, axis in PER_EXAMPLE_BATCH_AXIS.items():
            out[name] = _to_microbatches(out[name], axis)
    return {'x': out['x'], 'mem': out['mem'], 'positions': out['positions'], 'ev_norm': out['ev_norm'], 'ev_w_in': out['ev_w_in'], 'ev_pool_w': out['ev_pool_w'], 'ev_pool_scale': out['ev_pool_scale'], 'ev_q_norm': out['ev_q_norm'], 'ev_w_q_up': out['ev_w_q_up'], 'ev_kv_norm': out['ev_kv_norm'], 'ev_w_kv_up': out['ev_w_kv_up'], 'ev_w_out': out['ev_w_out'], 'od_norm': out['od_norm'], 'od_w_in': out['od_w_in'], 'od_conv_w': out['od_conv_w'], 'od_conv_b': out['od_conv_b'], 'od_w_rgate': out['od_w_rgate'], 'od_b_rgate': out['od_b_rgate'], 'od_w_igate': out['od_w_igate'], 'od_b_igate': out['od_b_igate'], 'od_lambda': out['od_lambda'], 'od_w_out': out['od_w_out'], 'xa_norm_x': out['xa_norm_x'], 'xa_norm_mem': out['xa_norm_mem'], 'xa_w_q': out['xa_w_q'], 'xa_w_kv': out['xa_w_kv'], 'xa_w_o': out['xa_w_o'], 'ffn_norm': out['ffn_norm'], 'ffn_w_gate_up': out['ffn_w_gate_up'], 'ffn_w_down': out['ffn_w_down'], 'final_norm': out['final_norm'], 'loss_target': out['loss_target'], 'm_ev_norm': out['m_ev_norm'], 'm_ev_w_in': out['m_ev_w_in'], 'm_ev_pool_w': out['m_ev_pool_w'], 'm_ev_pool_scale': out['m_ev_pool_scale'], 'm_ev_q_norm': out['m_ev_q_norm'], 'm_ev_w_q_up': out['m_ev_w_q_up'], 'm_ev_kv_norm': out['m_ev_kv_norm'], 'm_ev_w_kv_up': out['m_ev_w_kv_up'], 'm_ev_w_out': out['m_ev_w_out'], 'm_od_norm': out['m_od_norm'], 'm_od_w_in': out['m_od_w_in'], 'm_od_conv_w': out['m_od_conv_w'], 'm_od_conv_b': out['m_od_conv_b'], 'm_od_w_rgate': out['m_od_w_rgate'], 'm_od_b_rgate': out['m_od_b_rgate'], 'm_od_w_igate': out['m_od_w_igate'], 'm_od_b_igate': out['m_od_b_igate'], 'm_od_lambda': out['m_od_lambda'], 'm_od_w_out': out['m_od_w_out'], 'm_xa_norm_x': out['m_xa_norm_x'], 'm_xa_norm_mem': out['m_xa_norm_mem'], 'm_xa_w_q': out['m_xa_w_q'], 'm_xa_w_kv': out['m_xa_w_kv'], 'm_xa_w_o': out['m_xa_w_o'], 'm_ffn_norm': out['m_ffn_norm'], 'm_ffn_w_gate_up': out['m_ffn_w_gate_up'], 'm_ffn_w_down': out['m_ffn_w_down'], 'm_final_norm': out['m_final_norm'], 'v_ev_norm': out['v_ev_norm'], 'v_ev_w_in': out['v_ev_w_in'], 'v_ev_pool_w': out['v_ev_pool_w'], 'v_ev_pool_scale': out['v_ev_pool_scale'], 'v_ev_q_norm': out['v_ev_q_norm'], 'v_ev_w_q_up': out['v_ev_w_q_up'], 'v_ev_kv_norm': out['v_ev_kv_norm'], 'v_ev_w_kv_up': out['v_ev_w_kv_up'], 'v_ev_w_out': out['v_ev_w_out'], 'v_od_norm': out['v_od_norm'], 'v_od_w_in': out['v_od_w_in'], 'v_od_conv_w': out['v_od_conv_w'], 'v_od_conv_b': out['v_od_conv_b'], 'v_od_w_rgate': out['v_od_w_rgate'], 'v_od_b_rgate': out['v_od_b_rgate'], 'v_od_w_igate': out['v_od_w_igate'], 'v_od_b_igate': out['v_od_b_igate'], 'v_od_lambda': out['v_od_lambda'], 'v_od_w_out': out['v_od_w_out'], 'v_xa_norm_x': out['v_xa_norm_x'], 'v_xa_norm_mem': out['v_xa_norm_mem'], 'v_xa_w_q': out['v_xa_w_q'], 'v_xa_w_kv': out['v_xa_w_kv'], 'v_xa_w_o': out['v_xa_w_o'], 'v_ffn_norm': out['v_ffn_norm'], 'v_ffn_w_gate_up': out['v_ffn_w_gate_up'], 'v_ffn_w_down': out['v_ffn_w_down'], 'v_final_norm': out['v_final_norm']}


def _loss(weights, diff, rest, loss_target):
    with _jax.named_scope("forward"):
        args = {**rest, TWIN_DIFF_INPUT: diff, **{k: w.astype(_WEIGHT_DTYPES[k]) for k, w in weights.items()}}
        y = _forward(args)
    with _jax.named_scope("loss_head"):
        err = _jnp.square(y.astype(_jnp.float32) - loss_target)
        return 0.5 * _jnp.sum(_jnp.mean(err, axis=-1)) if err.ndim else 0.5 * err


def _adamw(w, g, m, v):
    m = ADAM_B1 * m + (1.0 - ADAM_B1) * g
    v = ADAM_B2 * v + (1.0 - ADAM_B2) * _jnp.square(g)
    m_hat = m / (1.0 - ADAM_B1 ** ADAM_STEP)
    v_hat = v / (1.0 - ADAM_B2 ** ADAM_STEP)
    delta = -ADAM_LR * (m_hat / (_jnp.sqrt(v_hat) + ADAM_EPS) + ADAM_WD * w)
    return delta, m, v


def reference(x, mem, positions, ev_norm, ev_w_in, ev_pool_w, ev_pool_scale, ev_q_norm, ev_w_q_up, ev_kv_norm, ev_w_kv_up, ev_w_out, od_norm, od_w_in, od_conv_w, od_conv_b, od_w_rgate, od_b_rgate, od_w_igate, od_b_igate, od_lambda, od_w_out, xa_norm_x, xa_norm_mem, xa_w_q, xa_w_kv, xa_w_o, ffn_norm, ffn_w_gate_up, ffn_w_down, final_norm, loss_target, m_ev_norm, m_ev_w_in, m_ev_pool_w, m_ev_pool_scale, m_ev_q_norm, m_ev_w_q_up, m_ev_kv_norm, m_ev_w_kv_up, m_ev_w_out, m_od_norm, m_od_w_in, m_od_conv_w, m_od_conv_b, m_od_w_rgate, m_od_b_rgate, m_od_w_igate, m_od_b_igate, m_od_lambda, m_od_w_out, m_xa_norm_x, m_xa_norm_mem, m_xa_w_q, m_xa_w_kv, m_xa_w_o, m_ffn_norm, m_ffn_w_gate_up, m_ffn_w_down, m_final_norm, v_ev_norm, v_ev_w_in, v_ev_pool_w, v_ev_pool_scale, v_ev_q_norm, v_ev_w_q_up, v_ev_kv_norm, v_ev_w_kv_up, v_ev_w_out, v_od_norm, v_od_w_in, v_od_conv_w, v_od_conv_b, v_od_w_rgate, v_od_b_rgate, v_od_w_igate, v_od_b_igate, v_od_lambda, v_od_w_out, v_xa_norm_x, v_xa_norm_mem, v_xa_w_q, v_xa_w_kv, v_xa_w_o, v_ffn_norm, v_ffn_w_gate_up, v_ffn_w_down, v_final_norm):
    given = dict(x=x, mem=mem, positions=positions, ev_norm=ev_norm, ev_w_in=ev_w_in, ev_pool_w=ev_pool_w, ev_pool_scale=ev_pool_scale, ev_q_norm=ev_q_norm, ev_w_q_up=ev_w_q_up, ev_kv_norm=ev_kv_norm, ev_w_kv_up=ev_w_kv_up, ev_w_out=ev_w_out, od_norm=od_norm, od_w_in=od_w_in, od_conv_w=od_conv_w, od_conv_b=od_conv_b, od_w_rgate=od_w_rgate, od_b_rgate=od_b_rgate, od_w_igate=od_w_igate, od_b_igate=od_b_igate, od_lambda=od_lambda, od_w_out=od_w_out, xa_norm_x=xa_norm_x, xa_norm_mem=xa_norm_mem, xa_w_q=xa_w_q, xa_w_kv=xa_w_kv, xa_w_o=xa_w_o, ffn_norm=ffn_norm, ffn_w_gate_up=ffn_w_gate_up, ffn_w_down=ffn_w_down, final_norm=final_norm, loss_target=loss_target, m_ev_norm=m_ev_norm, m_ev_w_in=m_ev_w_in, m_ev_pool_w=m_ev_pool_w, m_ev_pool_scale=m_ev_pool_scale, m_ev_q_norm=m_ev_q_norm, m_ev_w_q_up=m_ev_w_q_up, m_ev_kv_norm=m_ev_kv_norm, m_ev_w_kv_up=m_ev_w_kv_up, m_ev_w_out=m_ev_w_out, m_od_norm=m_od_norm, m_od_w_in=m_od_w_in, m_od_conv_w=m_od_conv_w, m_od_conv_b=m_od_conv_b, m_od_w_rgate=m_od_w_rgate, m_od_b_rgate=m_od_b_rgate, m_od_w_igate=m_od_w_igate, m_od_b_igate=m_od_b_igate, m_od_lambda=m_od_lambda, m_od_w_out=m_od_w_out, m_xa_norm_x=m_xa_norm_x, m_xa_norm_mem=m_xa_norm_mem, m_xa_w_q=m_xa_w_q, m_xa_w_kv=m_xa_w_kv, m_xa_w_o=m_xa_w_o, m_ffn_norm=m_ffn_norm, m_ffn_w_gate_up=m_ffn_w_gate_up, m_ffn_w_down=m_ffn_w_down, m_final_norm=m_final_norm, v_ev_norm=v_ev_norm, v_ev_w_in=v_ev_w_in, v_ev_pool_w=v_ev_pool_w, v_ev_pool_scale=v_ev_pool_scale, v_ev_q_norm=v_ev_q_norm, v_ev_w_q_up=v_ev_w_q_up, v_ev_kv_norm=v_ev_kv_norm, v_ev_w_kv_up=v_ev_w_kv_up, v_ev_w_out=v_ev_w_out, v_od_norm=v_od_norm, v_od_w_in=v_od_w_in, v_od_conv_w=v_od_conv_w, v_od_conv_b=v_od_conv_b, v_od_w_rgate=v_od_w_rgate, v_od_b_rgate=v_od_b_rgate, v_od_w_igate=v_od_w_igate, v_od_b_igate=v_od_b_igate, v_od_lambda=v_od_lambda, v_od_w_out=v_od_w_out, v_xa_norm_x=v_xa_norm_x, v_xa_norm_mem=v_xa_norm_mem, v_xa_w_q=v_xa_w_q, v_xa_w_kv=v_xa_w_kv, v_xa_w_o=v_xa_w_o, v_ffn_norm=v_ffn_norm, v_ffn_w_gate_up=v_ffn_w_gate_up, v_ffn_w_down=v_ffn_w_down, v_final_norm=v_final_norm)
    weights = {n: given[n] for n in TWIN_WEIGHTS}
    shared = {n: given[n] for n in SHARED_INPUTS}
    per_example = {n: given[n] for n in ['x', 'mem', 'positions']}
    grad_fn = _jax.value_and_grad(_loss, argnums=(0, 1))

    def one_microbatch(ex, loss_target):
        ex = dict(ex)
        diff = ex.pop(TWIN_DIFF_INPUT)
        return grad_fn(weights, diff, {**shared, **ex}, loss_target)

    if N_MICROBATCH == 1:
        loss, (grad_w, grad_x) = one_microbatch(per_example, given["loss_target"])
    else:
        def body(carry, xs):
            loss_sum, grad_sum = carry
            l_k, (gw_k, gx_k) = one_microbatch(xs[0], xs[1])
            with _jax.named_scope("update"):
                return (loss_sum + l_k, _jax.tree.map(_jnp.add, grad_sum, gw_k)), gx_k

        init = (_jnp.zeros((), _jnp.float32), _jax.tree.map(_jnp.zeros_like, weights))
        (loss, grad_w), grad_x = _jax.lax.scan(body, init, (per_example, given["loss_target"]))
    with _jax.named_scope("update"):
        delta_w, new_m, new_v = {}, {}, {}
        for n in TWIN_WEIGHTS:
            delta_w[n], new_m[n], new_v[n] = _adamw(weights[n], grad_w[n], given["m_" + n], given["v_" + n])
    return (loss, grad_x, *[grad_w[n] for n in TWIN_WEIGHTS], *[delta_w[n] for n in TWIN_WEIGHTS],
            *[new_m[n] for n in TWIN_WEIGHTS], *[new_v[n] for n in TWIN_WEIGHTS])
```

```python
import functools

import jax
import jax.numpy as jnp
from jax import lax
from jax.experimental import pallas as pl
from jax.experimental.pallas import tpu as pltpu

F32, BF16 = jnp.float32, jnp.bfloat16
N_DEV = 8
D = 1024
POOL_DIM = 512
POOL_WINDOWS = (2, 4, 8, 16)
MLA_HEADS = 8
QK_DIM = 96
Q_LORA, KV_LORA = 256, 128
LRU_HEADS, LRU_HEAD_DIM = 4, 256
LRU_C = 8.0
MEM_HEADS, MEM_HEAD_DIM = 4, 256
D_FF = 2816
RMS_EPS = 1e-6
ADAM_LR, ADAM_B1, ADAM_B2, ADAM_EPS, ADAM_WD, ADAM_STEP = 0.001, 0.9, 0.999, 1e-08, 0.01, 10
LANES = 128
POOL_HALO = 16
CONV_HALO = 8
VMEM_LIMIT = 60000 * 1024


def _cp():
    return pltpu.CompilerParams(dimension_semantics=("arbitrary",), vmem_limit_bytes=VMEM_LIMIT)


def _cp2():
    return pltpu.CompilerParams(dimension_semantics=("arbitrary", "arbitrary"), vmem_limit_bytes=VMEM_LIMIT)


def _row(ts, c, col=0):
    return pl.BlockSpec((ts, c), lambda i: (i, col))


def _prev(hr, c, ts, col=0):
    r = ts // hr
    return pl.BlockSpec((hr, c), lambda i: (jnp.maximum(i * r - 1, 0), col))


def _next(hr, c, ts, n, col=0):
    r = ts // hr
    return pl.BlockSpec((hr, c), lambda i: (jnp.minimum((i + 1) * r, n * r - 1), col))


def _const(shape):
    nd = len(shape)
    return pl.BlockSpec(tuple(shape), lambda i: (0,) * nd, pipeline_mode=pl.Buffered(1))


def _acc(shape):
    nd = len(shape)
    return pl.BlockSpec(tuple(shape), lambda i: (0,) * nd)


def _sds(shape, dt):
    return jax.ShapeDtypeStruct(tuple(shape), dt)


def _dot(a, b):
    return jnp.dot(a.astype(BF16), b.astype(BF16), preferred_element_type=F32)


def _dot_nt(a, b):
    return lax.dot_general(a.astype(BF16), b.astype(BF16), (((1,), (1,)), ((), ())), preferred_element_type=F32)


def _dot_tn(a, b):
    return lax.dot_general(a.astype(BF16), b.astype(BF16), (((0,), (0,)), ((), ())), preferred_element_type=F32)


def _rms(x, g):
    rstd = lax.rsqrt(jnp.mean(x * x, axis=-1, keepdims=True) + RMS_EPS)
    return x * rstd * g, rstd


def _rms_bwd(x, g, rstd, dy):
    xn = x * rstd
    dyg = dy * g
    dx = rstd * (dyg - xn * jnp.mean(dyg * xn, axis=-1, keepdims=True))
    return dx, dy * xn


def _rowsum(v):
    return jnp.sum(v, axis=0, keepdims=True)


def _roll(v, s, axis):
    n = v.shape[axis]
    return pltpu.roll(v, s % n, axis)


def _rope(t, c, a, b):
    k = t.shape[1] // LANES
    if k > 1:
        c, a, b = (jnp.tile(v, (1, k)) for v in (c, a, b))
    return t * c + _roll(t, 16, 1) * a + _roll(t, -16, 1) * b


def _rope_bwd(d, c, a, b):
    k = d.shape[1] // LANES
    if k > 1:
        c, a, b = (jnp.tile(v, (1, k)) for v in (c, a, b))
    return d * c + _roll(d * a, -16, 1) + _roll(d * b, 16, 1)


def _gelu(x):
    c = 0.7978845608028654
    t = jnp.tanh(c * (x + 0.044715 * x * x * x))
    return 0.5 * x * (1.0 + t), t


def _gelu_grad(x, t):
    c = 0.7978845608028654
    return 0.5 * (1.0 + t) + 0.5 * x * (1.0 - t * t) * c * (1.0 + 3.0 * 0.044715 * x * x)


def _blockdot(v, w_ref, nblk, width):
    return jnp.concatenate(
        [_dot(v[:, j * width:(j + 1) * width], w_ref[j]) for j in range(nblk)], axis=1)


def _pool_cnt(row0, rows):
    t = row0 + lax.broadcasted_iota(jnp.int32, (rows, POOL_DIM), 0)
    w = jnp.left_shift(2, lax.broadcasted_iota(jnp.int32, (rows, POOL_DIM), 1) // LANES)
    return jnp.minimum(t + 1, w).astype(F32)


def _pool_windows(ext, sign):
    s2 = ext + _roll(ext, sign * 1, 0)
    t = s2[:, LANES:]
    s4 = t + _roll(t, sign * 2, 0)
    t = s4[:, LANES:]
    s8 = t + _roll(t, sign * 4, 0)
    t = s8[:, LANES:]
    s16 = t + _roll(t, sign * 8, 0)
    return jnp.concatenate([s2[:, :LANES], s4[:, :LANES], s8[:, :LANES], s16], axis=1)


def _pooled(uprev, u, row0):
    ts = u.shape[0]
    ext = jnp.concatenate([uprev, u], axis=0)
    sums = _pool_windows(ext, 1)[POOL_HALO:]
    return sums / _pool_cnt(row0, ts) - u


def _expm1(x):
    return jnp.where(jnp.abs(x) < 0.01, x * (1.0 + 0.5 * x * (1.0 + x * (1.0 / 3.0))), jnp.exp(x) - 1.0)


def _softplus(z):
    return jnp.maximum(z, 0.0) + jnp.log1p(jnp.exp(-jnp.abs(z)))


def _tile_rows(s, want):
    return min(want, s)


def even_pre(x, tabs, g, win, pw, pscale, qg, wq, kvg, wk, wv):
    S = x.shape[0]
    ts = _tile_rows(S, 512)

    def body(x_ref, xp_ref, c_ref, a_ref, b_ref, g_ref, win_ref, pw_ref, ps_ref, qg_ref, wq_ref, kvg_ref,
             wk_ref, wv_ref, z_ref, q_ref, k_ref, v_ref, yp_ref):
        i = pl.program_id(0)
        h, _ = _rms(x_ref[...], g_ref[...])
        z = _dot(h, win_ref[...])
        z_ref[...] = z
        hp, _ = _rms(xp_ref[...], g_ref[...])
        uprev = _dot(hp, win_ref[:, :POOL_DIM]) * (i > 0).astype(F32)
        u = z[:, :POOL_DIM]
        pooled = _pooled(uprev, u, i * ts)
        yp_ref[...] = (_blockdot(pooled, pw_ref, 4, LANES) * ps_ref[...]).astype(BF16)
        c, a, b = c_ref[...], a_ref[...], b_ref[...]
        cqn, _ = _rms(z[:, 512:768], qg_ref[...])
        q_ref[...] = _rope(_dot(cqn, wq_ref[...]), c, a, b).astype(BF16)
        ckvn, _ = _rms(z[:, 768:896], kvg_ref[...])
        krr = _rope(z[:, 896:1024], c, a, b)
        k_ref[...] = (_dot(ckvn, wk_ref[...]) + jnp.tile(krr, (1, MLA_HEADS))).astype(BF16)
        v_ref[...] = _dot(ckvn, wv_ref[...]).astype(BF16)

    ins = [x, x, *tabs, g, win, pw, pscale, qg, wq, kvg, wk, wv]
    in_specs = [_row(ts, D), _prev(POOL_HALO, D, ts), _row(ts, LANES), _row(ts, LANES), _row(ts, LANES)]
    in_specs += [_const(v.shape) for v in ins[5:]]
    return pl.pallas_call(
        body, name="even_pre", grid=(S // ts,), in_specs=in_specs,
        out_specs=[_row(ts, D)] * 4 + [_row(ts, POOL_DIM)],
        out_shape=[_sds((S, D), F32)] + [_sds((S, D), BF16)] * 3 + [_sds((S, POOL_DIM), BF16)],
        compiler_params=_cp())(*ins)


def attn_fwd(qp, kp, vp):
    S = qp.shape[0]
    tq = _tile_rows(S, 512)
    scale = QK_DIM ** -0.5

    def body(q_ref, k_ref, v_ref, o_ref, lse_ref):
        qi = pl.program_id(1)
        q = q_ref[...]

        def block(ki, carry, masked):
            m, l, acc = carry
            off = pl.multiple_of(ki * tq, tq)
            s = _dot_nt(q, k_ref[pl.ds(off, tq), :]) * scale
            if masked:
                row = lax.broadcasted_iota(jnp.int32, (tq, tq), 0)
                col = lax.broadcasted_iota(jnp.int32, (tq, tq), 1)
                s = jnp.where(col <= row, s, -1e30)
            m_new = jnp.maximum(m, jnp.max(s, axis=1, keepdims=True))
            p = jnp.exp(s - m_new)
            alpha = jnp.exp(m - m_new)
            l = alpha * l + jnp.sum(p, axis=1, keepdims=True)
            acc = alpha * acc + _dot(p, v_ref[pl.ds(off, tq), :])
            return m_new, l, acc

        init = (jnp.full((tq, 1), -1e30, F32), jnp.zeros((tq, 1), F32), jnp.zeros((tq, LANES), F32))
        carry = lax.fori_loop(0, qi, lambda ki, c: block(ki, c, False), init)
        m, l, acc = block(qi, carry, True)
        o_ref[...] = acc / l
        lse_ref[...] = jnp.broadcast_to(m + jnp.log(l), (tq, LANES))

    blk = pl.BlockSpec((tq, LANES), lambda h, i: (i, h))
    full = pl.BlockSpec((S, LANES), lambda h, i: (0, h))
    return pl.pallas_call(
        body, name="attn_fwd", grid=(MLA_HEADS, S // tq), in_specs=[blk, full, full], out_specs=[blk, blk],
        out_shape=[_sds((S, D), F32), _sds((S, D), F32)], compiler_params=_cp2())(qp, kp, vp)


def even_post(x, ypool, o, wo_pool, wo_att):
    S = x.shape[0]
    ts = _tile_rows(S, 512)

    def body(x_ref, yp_ref, o_ref, wp_ref, wa_ref, out_ref):
        out_ref[...] = x_ref[...] + _dot(yp_ref[...], wp_ref[...]) + _dot(o_ref[...], wa_ref[...])

    return pl.pallas_call(
        body, name="even_post", grid=(S // ts,),
        in_specs=[_row(ts, D), _row(ts, POOL_DIM), _row(ts, D), _const(wo_pool.shape), _const(wo_att.shape)],
        out_specs=_row(ts, D), out_shape=_sds((S, D), F32), compiler_params=_cp())(x, ypool, o, wo_pool, wo_att)


def mem_kv(mem, g, wkv):
    M = mem.shape[0]

    def body(mem_ref, g_ref, w_ref, mn_ref, k_ref, v_ref):
        mn, _ = _rms(mem_ref[...], g_ref[...])
        mn_ref[...] = mn.astype(BF16)
        k_ref[...] = _dot(mn, w_ref[:, :D]).astype(BF16)
        v_ref[...] = _dot(mn, w_ref[:, D:]).astype(BF16)

    return pl.pallas_call(
        body, name="mem_kv", grid=(1,), in_specs=[_acc(mem.shape), _acc(g.shape), _acc(wkv.shape)],
        out_specs=[_acc((M, D))] * 3, out_shape=[_sds((M, D), BF16)] * 3, compiler_params=_cp())(mem, g, wkv)


def _xattn_heads(hx, wq_ref, k_ref, v_ref):
    q = _dot(hx, wq_ref[...])
    scale = MEM_HEAD_DIM ** -0.5
    ps, os_ = [], []
    for h in range(MEM_HEADS):
        sl = slice(h * MEM_HEAD_DIM, (h + 1) * MEM_HEAD_DIM)
        s = _dot_nt(q[:, sl], k_ref[:, sl]) * scale
        e = jnp.exp(s - jnp.max(s, axis=1, keepdims=True))
        p = e / jnp.sum(e, axis=1, keepdims=True)
        ps.append(p)
        os_.append(_dot(p, v_ref[:, sl]))
    return q, ps, jnp.concatenate(os_, axis=1)


def xattn_fwd(x, g, wq, kmem, vmem, wo):
    S = x.shape[0]
    ts = _tile_rows(S, 512)

    def body(x_ref, g_ref, wq_ref, k_ref, v_ref, wo_ref, out_ref):
        x_ = x_ref[...]
        hx, _ = _rms(x_, g_ref[...])
        _, _, o = _xattn_heads(hx, wq_ref, k_ref, v_ref)
        out_ref[...] = x_ + _dot(o, wo_ref[...])

    ins = [x, g, wq, kmem, vmem, wo]
    return pl.pallas_call(
        body, name="xattn_fwd", grid=(S // ts,), in_specs=[_row(ts, D)] + [_const(v.shape) for v in ins[1:]],
        out_specs=_row(ts, D), out_shape=_sds((S, D), F32), compiler_params=_cp())(*ins)


def xattn_bwd(x, dy, g, wq, wqT, kmem, vmem, woT):
    S = x.shape[0]
    M = kmem.shape[0]
    ts = _tile_rows(S, 512)
    scale = MEM_HEAD_DIM ** -0.5

    def body(x_ref, dy_ref, g_ref, wq_ref, wqT_ref, k_ref, v_ref, woT_ref,
             dx_ref, o_ref, dq_ref, hx_ref, dg_ref, dk_ref, dv_ref):
        i = pl.program_id(0)

        @pl.when(i == 0)
        def _():
            dg_ref[...] = jnp.zeros_like(dg_ref)
            dk_ref[...] = jnp.zeros_like(dk_ref)
            dv_ref[...] = jnp.zeros_like(dv_ref)

        x_, dy_ = x_ref[...], dy_ref[...]
        hx, rstd = _rms(x_, g_ref[...])
        q, ps, o = _xattn_heads(hx, wq_ref, k_ref, v_ref)
        hx_ref[...] = hx.astype(BF16)
        o_ref[...] = o.astype(BF16)
        do = _dot(dy_, woT_ref[...])
        dqs = []
        for h in range(MEM_HEADS):
            sl = slice(h * MEM_HEAD_DIM, (h + 1) * MEM_HEAD_DIM)
            p, do_h = ps[h], do[:, sl]
            dp = _dot_nt(do_h, v_ref[:, sl])
            ds = p * (dp - jnp.sum(p * dp, axis=1, keepdims=True)) * scale
            dqs.append(_dot(ds, k_ref[:, sl]))
            dk_ref[:, sl] += _dot_tn(ds, q[:, sl])
            dv_ref[:, sl] += _dot_tn(p, do_h)
        dq = jnp.concatenate(dqs, axis=1)
        dq_ref[...] = dq.astype(BF16)
        dxn, dgr = _rms_bwd(x_, g_ref[...], rstd, _dot(dq, wqT_ref[...]))
        dx_ref[...] = dy_ + dxn
        dg_ref[...] += _rowsum(dgr)

    ins = [x, dy, g, wq, wqT, kmem, vmem, woT]
    return pl.pallas_call(
        body, name="xattn_bwd", grid=(S // ts,),
        in_specs=[_row(ts, D), _row(ts, D)] + [_const(v.shape) for v in ins[2:]],
        out_specs=[_row(ts, D)] * 4 + [_acc((1, D)), _acc((M, D)), _acc((M, D))],
        out_shape=[_sds((S, D), F32)] + [_sds((S, D), BF16)] * 3 + [_sds((1, D), F32), _sds((M, D), F32),
                                                                    _sds((M, D), F32)],
        compiler_params=_cp())(*ins)


def mem_bwd(mem, g, dk, dv, wkvT):
    M = mem.shape[0]

    def body(mem_ref, g_ref, dk_ref, dv_ref, w_ref, dkv_ref, dg_ref):
        dkv = jnp.concatenate([dk_ref[...], dv_ref[...]], axis=1)
        dkv_ref[...] = dkv.astype(BF16)
        _, rstd = _rms(mem_ref[...], g_ref[...])
        dg_ref[...] = _rowsum(_dot(dkv, w_ref[...]) * (mem_ref[...] * rstd))

    ins = [mem, g, dk, dv, wkvT]
    return pl.pallas_call(
        body, name="mem_bwd", grid=(1,), in_specs=[_acc(v.shape) for v in ins],
        out_specs=[_acc((M, 2 * D)), _acc((1, D))], out_shape=[_sds((M, 2 * D), BF16), _sds((1, D), F32)],
        compiler_params=_cp())(*ins)


def ffn_fwd(x, g, wgu, wd):
    S = x.shape[0]
    ts = _tile_rows(S, 256)

    def body(x_ref, g_ref, wgu_ref, wd_ref, out_ref):
        x_ = x_ref[...]
        hf, _ = _rms(x_, g_ref[...])
        gu = _dot(hf, wgu_ref[...])
        gg, uu = gu[:, :D_FF], gu[:, D_FF:]
        out_ref[...] = x_ + _dot(gg * jax.nn.sigmoid(gg) * uu, wd_ref[...])

    ins = [x, g, wgu, wd]
    return pl.pallas_call(
        body, name="ffn_fwd", grid=(S // ts,), in_specs=[_row(ts, D)] + [_const(v.shape) for v in ins[1:]],
        out_specs=_row(ts, D), out_shape=_sds((S, D), F32), compiler_params=_cp())(*ins)


def ffn_bwd_a(x, dy, g, wgu, wdT):
    S = x.shape[0]
    ts = _tile_rows(S, 256)

    def body(x_ref, dy_ref, g_ref, wgu_ref, wdT_ref, hf_ref, act_ref, dgu_ref):
        hf, _ = _rms(x_ref[...], g_ref[...])
        hf_ref[...] = hf.astype(BF16)
        gu = _dot(hf, wgu_ref[...])
        gg, uu = gu[:, :D_FF], gu[:, D_FF:]
        sg = jax.nn.sigmoid(gg)
        silu = gg * sg
        act_ref[...] = (silu * uu).astype(BF16)
        dact = _dot(dy_ref[...], wdT_ref[...])
        dgu_ref[:, :D_FF] = (dact * uu * (sg * (1.0 + gg * (1.0 - sg)))).astype(BF16)
        dgu_ref[:, D_FF:] = (dact * silu).astype(BF16)

    ins = [x, dy, g, wgu, wdT]
    return pl.pallas_call(
        body, name="ffn_bwd_a", grid=(S // ts,),
        in_specs=[_row(ts, D), _row(ts, D)] + [_const(v.shape) for v in ins[2:]],
        out_specs=[_row(ts, D), _row(ts, D_FF), _row(ts, 2 * D_FF)],
        out_shape=[_sds((S, D), BF16), _sds((S, D_FF), BF16), _sds((S, 2 * D_FF), BF16)],
        compiler_params=_cp())(*ins)


def matmul_rms_bwd(name, x, dy, dz, g, wT):
    S = x.shape[0]
    ts = _tile_rows(S, 512)
    kin = dz.shape[1]

    def body(x_ref, dy_ref, dz_ref, g_ref, wT_ref, dx_ref, dg_ref):
        @pl.when(pl.program_id(0) == 0)
        def _():
            dg_ref[...] = jnp.zeros_like(dg_ref)

        x_ = x_ref[...]
        _, rstd = _rms(x_, g_ref[...])
        dxn, dgr = _rms_bwd(x_, g_ref[...], rstd, _dot(dz_ref[...], wT_ref[...]))
        dx_ref[...] = dy_ref[...] + dxn
        dg_ref[...] += _rowsum(dgr)

    return pl.pallas_call(
        body, name=name, grid=(S // ts,),
        in_specs=[_row(ts, D), _row(ts, D), _row(ts, kin), _const(g.shape), _const(wT.shape)],
        out_specs=[_row(ts, D), _acc((1, D))], out_shape=[_sds((S, D), F32), _sds((1, D), F32)],
        compiler_params=_cp())(x, dy, dz, g, wT)


def _conv_fwd(xprev, xbp, cw_ref, cb):
    ext = jnp.concatenate([xprev, xbp], axis=0)
    acc = cb + cw_ref[3:4, :] * xbp
    for k in range(3):
        acc = acc + cw_ref[k:k + 1, :] * _roll(ext, 3 - k, 0)[CONV_HALO:]
    return acc


def _gates(xb, keep, wr_ref, br, wi_ref, bi, lam):
    r = jax.nn.sigmoid(_blockdot(xb, wr_ref, LRU_HEADS, LRU_HEAD_DIM) + br)
    ig = jax.nn.sigmoid(_blockdot(xb, wi_ref, LRU_HEADS, LRU_HEAD_DIM) + bi)
    sp = _softplus(-lam)
    log_a = -LRU_C * r * sp
    a = jnp.exp(log_a)
    mult = jnp.sqrt(jnp.maximum(-_expm1(2.0 * log_a), 0.0))
    return r, ig, sp, a, mult


def odd_pre(x, keep, g, win, cw, cb, wr, br, wi, bi, lam):
    S = x.shape[0]
    ts = _tile_rows(S, 512)

    def body(x_ref, xp_ref, keep_ref, g_ref, win_ref, cw_ref, cb_ref, wr_ref, br_ref, wi_ref, bi_ref, lam_ref,
             z_ref, a_ref, b_ref):
        i = pl.program_id(0)
        h, _ = _rms(x_ref[...], g_ref[...])
        z = _dot(h, win_ref[...])
        z_ref[...] = z
        hp, _ = _rms(xp_ref[...], g_ref[...])
        xprev = _dot(hp, win_ref[:, D:]) * (i > 0).astype(F32)
        xb = _conv_fwd(xprev, z[:, D:], cw_ref, cb_ref[...])
        keep_ = keep_ref[...]
        _, ig, _, a, mult = _gates(xb, keep_, wr_ref, br_ref[...], wi_ref, bi_ref[...], lam_ref[...])
        a_ref[...] = a * keep_
        b_ref[...] = jnp.where(keep_ > 0.0, mult, 1.0) * (ig * xb)

    ins = [x, x, keep, g, win, cw, cb, wr, br, wi, bi, lam]
    return pl.pallas_call(
        body, name="odd_pre", grid=(S // ts,),
        in_specs=[_row(ts, D), _prev(CONV_HALO, D, ts), _row(ts, 1)] + [_const(v.shape) for v in ins[3:]],
        out_specs=[_row(ts, 2 * D), _row(ts, D), _row(ts, D)],
        out_shape=[_sds((S, 2 * D), F32), _sds((S, D), F32), _sds((S, D), F32)], compiler_params=_cp())(*ins)


def lru_scan(a, b):
    S = a.shape[0]
    ts = _tile_rows(S, 512)

    def body(a_ref, b_ref, h_ref, carry_ref):
        @pl.when(pl.program_id(0) == 0)
        def _():
            carry_ref[...] = jnp.zeros_like(carry_ref)

        rid = lax.broadcasted_iota(jnp.int32, (8, D), 0)

        def group(j, carry):
            off = pl.multiple_of(j * 8, 8)
            a8, b8 = a_ref[pl.ds(off, 8), :], b_ref[pl.ds(off, 8), :]
            for k in (1, 2, 4):
                a_sh = jnp.where(rid >= k, _roll(a8, k, 0), 1.0)
                b_sh = jnp.where(rid >= k, _roll(b8, k, 0), 0.0)
                b8 = a8 * b_sh + b8
                a8 = a8 * a_sh
            h8 = a8 * carry + b8
            h_ref[pl.ds(off, 8), :] = h8
            return h8[7:8, :]

        carry_ref[...] = lax.fori_loop(0, ts // 8, group, carry_ref[...])

    return pl.pallas_call(
        body, name="lru_scan", grid=(S // ts,), in_specs=[_row(ts, D), _row(ts, D)], out_specs=_row(ts, D),
        out_shape=_sds((S, D), F32), scratch_shapes=[pltpu.VMEM((1, D), F32)], compiler_params=_cp())(a, b)


def odd_post(x, z, hseq, wout):
    S = x.shape[0]
    ts = _tile_rows(S, 512)

    def body(x_ref, gate_ref, h_ref, w_ref, out_ref):
        gl, _ = _gelu(gate_ref[...])
        out_ref[...] = x_ref[...] + _dot(gl * h_ref[...], w_ref[...])

    return pl.pallas_call(
        body, name="odd_post", grid=(S // ts,),
        in_specs=[_row(ts, D), _row(ts, D), _row(ts, D), _const(wout.shape)],
        out_specs=_row(ts, D), out_shape=_sds((S, D), F32), compiler_params=_cp())(x, z, hseq, wout)


def odd_post_bwd(dy, z, hseq, woutT):
    S = dy.shape[0]
    ts = _tile_rows(S, 512)

    def body(dy_ref, gate_ref, h_ref, w_ref, y_ref, dgate_ref, dh_ref):
        gate, hs = gate_ref[...], h_ref[...]
        gl, t = _gelu(gate)
        y_ref[...] = (gl * hs).astype(BF16)
        dyy = _dot(dy_ref[...], w_ref[...])
        dgate_ref[...] = dyy * hs * _gelu_grad(gate, t)
        dh_ref[...] = dyy * gl

    return pl.pallas_call(
        body, name="odd_post_bwd", grid=(S // ts,),
        in_specs=[_row(ts, D), _row(ts, D), _row(ts, D), _const(woutT.shape)],
        out_specs=[_row(ts, D)] * 3, out_shape=[_sds((S, D), BF16), _sds((S, D), F32), _sds((S, D), F32)],
        compiler_params=_cp())(dy, z, hseq, woutT)


def odd_gates_bwd(z, lam_grad, hseq, keep, cw, cb, wr, wrT, br, wi, wiT, bi, lam):
    S = z.shape[0]
    ts = _tile_rows(S, 512)

    def body(xbp_ref, xbpp_ref, lg_ref, h_ref, hp_ref, keep_ref, cw_ref, cb_ref, wr_ref, wrT_ref, br_ref, wi_ref,
             wiT_ref, bi_ref, lam_ref, dxb_ref, dcb_ref, dbr_ref, dbi_ref, dlam_ref, dwr_ref, dwi_ref):
        i = pl.program_id(0)

        @pl.when(i == 0)
        def _():
            for ref in (dcb_ref, dbr_ref, dbi_ref, dlam_ref, dwr_ref, dwi_ref):
                ref[...] = jnp.zeros_like(ref)

        first = (i > 0).astype(F32)
        xb = _conv_fwd(xbpp_ref[...] * first, xbp_ref[...], cw_ref, cb_ref[...])
        keep_ = keep_ref[...]
        lam_ = lam_ref[...]
        r, ig, sp, a, mult = _gates(xb, keep_, wr_ref, br_ref[...], wi_ref, bi_ref[...], lam_)
        hs = h_ref[...]
        hprev = _roll(jnp.concatenate([hp_ref[...] * first, hs], axis=0), 1, 0)[CONV_HALO:]
        lg = lg_ref[...]
        da = lg * hprev * keep_
        ixb = ig * xb
        dmult = lg * ixb * keep_
        dixb = lg * jnp.where(keep_ > 0.0, mult, 1.0)
        dlog_a = da * a - dmult * jnp.where(mult > 0.0, a * a / mult, 0.0)
        dr = dlog_a * (-LRU_C * sp)
        dlam_ref[...] += _rowsum(dlog_a * (-LRU_C * r)) * (-jax.nn.sigmoid(-lam_))
        dpr = dr * r * (1.0 - r)
        dpi = dixb * xb * ig * (1.0 - ig)
        dbr_ref[...] += _rowsum(dpr)
        dbi_ref[...] += _rowsum(dpi)
        dxb = dixb * ig
        parts = []
        for h in range(LRU_HEADS):
            sl = slice(h * LRU_HEAD_DIM, (h + 1) * LRU_HEAD_DIM)
            dwr_ref[h] += _dot_tn(xb[:, sl], dpr[:, sl])
            dwi_ref[h] += _dot_tn(xb[:, sl], dpi[:, sl])
            parts.append(_dot(dpr[:, sl], wrT_ref[h]) + _dot(dpi[:, sl], wiT_ref[h]))
        dxb = dxb + jnp.concatenate(parts, axis=1)
        dxb_ref[...] = dxb
        dcb_ref[...] += _rowsum(dxb)

    ins = [z, z, lam_grad, hseq, hseq, keep, cw, cb, wr, wrT, br, wi, wiT, bi, lam]
    in_specs = [_row(ts, D, 1), _prev(CONV_HALO, D, ts, 1), _row(ts, D), _row(ts, D), _prev(CONV_HALO, D, ts),
                _row(ts, 1)] + [_const(v.shape) for v in ins[6:]]
    gshape = (LRU_HEADS, LRU_HEAD_DIM, LRU_HEAD_DIM)
    return pl.pallas_call(
        body, name="odd_gates_bwd", grid=(S // ts,), in_specs=in_specs,
        out_specs=[_row(ts, D)] + [_acc((1, D))] * 4 + [_acc(gshape)] * 2,
        out_shape=[_sds((S, D), F32)] + [_sds((1, D), F32)] * 4 + [_sds(gshape, F32)] * 2,
        compiler_params=_cp())(*ins)


def odd_pre_bwd(x, dy, z, dxb, dgate, g, cw, winT):
    S = x.shape[0]
    ts = _tile_rows(S, 512)
    n = S // ts

    def body(x_ref, dy_ref, xbp_ref, xbpp_ref, dxb_ref, dxbn_ref, dgate_ref, g_ref, cw_ref, winT_ref,
             dx_ref, h_ref, dz_ref, dcw_ref, dg_ref):
        i = pl.program_id(0)

        @pl.when(i == 0)
        def _():
            dcw_ref[...] = jnp.zeros_like(dcw_ref)
            dg_ref[...] = jnp.zeros_like(dg_ref)

        dxb = dxb_ref[...]
        extd = jnp.concatenate([dxb, dxbn_ref[...] * (i < n - 1).astype(F32)], axis=0)
        extx = jnp.concatenate([xbpp_ref[...] * (i > 0).astype(F32), xbp_ref[...]], axis=0)
        dxbp = cw_ref[3:4, :] * dxb
        dcw_ref[3:4, :] += _rowsum(dxb * xbp_ref[...])
        for k in range(3):
            dxbp = dxbp + cw_ref[k:k + 1, :] * _roll(extd, -(3 - k), 0)[:ts]
            dcw_ref[k:k + 1, :] += _rowsum(dxb * _roll(extx, 3 - k, 0)[CONV_HALO:])
        dz = jnp.concatenate([dgate_ref[...], dxbp], axis=1)
        dz_ref[...] = dz.astype(BF16)
        x_ = x_ref[...]
        h, rstd = _rms(x_, g_ref[...])
        h_ref[...] = h.astype(BF16)
        dxn, dgr = _rms_bwd(x_, g_ref[...], rstd, _dot(dz, winT_ref[...]))
        dx_ref[...] = dy_ref[...] + dxn
        dg_ref[...] += _rowsum(dgr)

    ins = [x, dy, z, z, dxb, dxb, dgate, g, cw, winT]
    in_specs = [_row(ts, D), _row(ts, D), _row(ts, D, 1), _prev(CONV_HALO, D, ts, 1), _row(ts, D),
                _next(CONV_HALO, D, ts, n), _row(ts, D)] + [_const(v.shape) for v in ins[7:]]
    return pl.pallas_call(
        body, name="odd_pre_bwd", grid=(n,), in_specs=in_specs,
        out_specs=[_row(ts, D), _row(ts, D), _row(ts, 2 * D), _acc((4, D)), _acc((1, D))],
        out_shape=[_sds((S, D), F32), _sds((S, D), BF16), _sds((S, 2 * D), BF16), _sds((4, D), F32),
                   _sds((1, D), F32)],
        compiler_params=_cp())(*ins)


def loss_head(x, target, g):
    S = x.shape[0]
    ts = _tile_rows(S, 512)

    def body(x_ref, t_ref, g_ref, dx_ref, dg_ref, loss_ref):
        @pl.when(pl.program_id(0) == 0)
        def _():
            dg_ref[...] = jnp.zeros_like(dg_ref)
            loss_ref[...] = jnp.zeros_like(loss_ref)

        x_ = x_ref[...]
        y, rstd = _rms(x_, g_ref[...])
        err = y - t_ref[...]
        loss_ref[...] += 0.5 * _rowsum(jnp.mean(err * err, axis=1, keepdims=True))
        dxn, dgr = _rms_bwd(x_, g_ref[...], rstd, err * (1.0 / D))
        dx_ref[...] = dxn
        dg_ref[...] += _rowsum(dgr)

    return pl.pallas_call(
        body, name="loss_head", grid=(S // ts,), in_specs=[_row(ts, D), _row(ts, D), _const(g.shape)],
        out_specs=[_row(ts, D), _acc((1, D)), _acc((1, 1))],
        out_shape=[_sds((S, D), F32), _sds((1, D), F32), _sds((1, 1), F32)], compiler_params=_cp())(x, target, g)


def even_post_bwd(dy, o, woT_pool, woT_att):
    S = dy.shape[0]
    ts = _tile_rows(S, 512)

    def body(dy_ref, o_ref, wp_ref, wa_ref, dyp_ref, do_ref, delta_ref):
        dy_ = dy_ref[...]
        dyp_ref[...] = _dot(dy_, wp_ref[...])
        do = _dot(dy_, wa_ref[...])
        do_ref[...] = do.astype(BF16)
        prod = do * o_ref[...]
        delta_ref[...] = jnp.concatenate(
            [jnp.broadcast_to(jnp.sum(prod[:, h * LANES:(h + 1) * LANES], axis=1, keepdims=True), (ts, LANES))
             for h in range(MLA_HEADS)], axis=1)

    return pl.pallas_call(
        body, name="even_post_bwd", grid=(S // ts,),
        in_specs=[_row(ts, D), _row(ts, D), _const(woT_pool.shape), _const(woT_att.shape)],
        out_specs=[_row(ts, POOL_DIM), _row(ts, D), _row(ts, D)],
        out_shape=[_sds((S, POOL_DIM), F32), _sds((S, D), BF16), _sds((S, D), F32)],
        compiler_params=_cp())(dy, o, woT_pool, woT_att)


def attn_dq(qp, kp, vp, do, lse, delta):
    S = qp.shape[0]
    tq = _tile_rows(S, 512)
    scale = QK_DIM ** -0.5

    def body(q_ref, k_ref, v_ref, do_ref, lse_ref, delta_ref, dq_ref):
        qi = pl.program_id(1)
        q, do_ = q_ref[...], do_ref[...]
        lse_, delta_ = lse_ref[:, :1], delta_ref[:, :1]

        def block(ki, dq, masked):
            off = pl.multiple_of(ki * tq, tq)
            k = k_ref[pl.ds(off, tq), :]
            s = _dot_nt(q, k) * scale
            if masked:
                row = lax.broadcasted_iota(jnp.int32, (tq, tq), 0)
                col = lax.broadcasted_iota(jnp.int32, (tq, tq), 1)
                s = jnp.where(col <= row, s, -1e30)
            p = jnp.exp(s - lse_)
            dp = _dot_nt(do_, v_ref[pl.ds(off, tq), :])
            return dq + _dot(p * (dp - delta_) * scale, k)

        dq = lax.fori_loop(0, qi, lambda ki, c: block(ki, c, False), jnp.zeros((tq, LANES), F32))
        dq_ref[...] = block(qi, dq, True)

    blk = pl.BlockSpec((tq, LANES), lambda h, i: (i, h))
    full = pl.BlockSpec((S, LANES), lambda h, i: (0, h))
    return pl.pallas_call(
        body, name="attn_dq", grid=(MLA_HEADS, S // tq), in_specs=[blk, full, full, blk, blk, blk], out_specs=blk,
        out_shape=_sds((S, D), F32), compiler_params=_cp2())(qp, kp, vp, do, lse, delta)


def attn_dkv(qp, kp, vp, do, lse_row, delta_row):
    S = qp.shape[0]
    tk = _tile_rows(S, 512)
    nq = S // tk
    scale = QK_DIM ** -0.5

    def body(q_ref, k_ref, v_ref, do_ref, lse_ref, delta_ref, dk_ref, dv_ref):
        kj = pl.program_id(1)
        k, v = k_ref[...], v_ref[...]

        def block(qi, carry, masked):
            dk, dv = carry
            off = pl.multiple_of(qi * tk, tk)
            q = q_ref[pl.ds(off, tk), :]
            do_ = do_ref[pl.ds(off, tk), :]
            st = _dot_nt(k, q) * scale
            if masked:
                row = lax.broadcasted_iota(jnp.int32, (tk, tk), 0)
                col = lax.broadcasted_iota(jnp.int32, (tk, tk), 1)
                st = jnp.where(col >= row, st, -1e30)
            pt = jnp.exp(st - lse_ref[qi])
            dv = dv + _dot(pt, do_)
            dpt = _dot_nt(v, do_)
            dk = dk + _dot(pt * (dpt - delta_ref[qi]) * scale, q)
            return dk, dv

        zero = jnp.zeros((tk, LANES), F32)
        carry = block(kj, (zero, zero), True)
        dk, dv = lax.fori_loop(kj + 1, nq, lambda qi, c: block(qi, c, False), carry)
        dk_ref[...] = dk
        dv_ref[...] = dv

    blk = pl.BlockSpec((tk, LANES), lambda h, j: (j, h))
    full = pl.BlockSpec((S, LANES), lambda h, j: (0, h))
    rowv = pl.BlockSpec((None, nq, 1, tk), lambda h, j: (h, 0, 0, 0))
    return pl.pallas_call(
        body, name="attn_dkv", grid=(MLA_HEADS, nq), in_specs=[full, blk, blk, full, rowv, rowv],
        out_specs=[blk, blk], out_shape=[_sds((S, D), F32)] * 2, compiler_params=_cp2())(
            qp, kp, vp, do, lse_row, delta_row)


def even_pre_bwd(x, dy, z, dq, dk, dv, dyp, tabs, g, winT, pw, pwT, pscale, qg, wqT, kvg, wkT, wvT):
    S = x.shape[0]
    ts = _tile_rows(S, 512)
    n = S // ts

    def body(x_ref, dy_ref, z_ref, up_ref, dq_ref, dk_ref, dv_ref, dyp_ref, dypn_ref, c_ref, a_ref, b_ref,
             g_ref, winT_ref, pw_ref, pwT_ref, ps_ref, qg_ref, wqT_ref, kvg_ref, wkT_ref, wvT_ref,
             dx_ref, h_ref, dz_ref, dg_ref, dpw_ref, dps_ref, dqg_ref, dwq_ref, dkvg_ref, dwk_ref, dwv_ref):
        i = pl.program_id(0)

        @pl.when(i == 0)
        def _():
            for ref in (dg_ref, dpw_ref, dps_ref, dqg_ref, dwq_ref, dkvg_ref, dwk_ref, dwv_ref):
                ref[...] = jnp.zeros_like(ref)

        z = z_ref[...]
        c, a, b = c_ref[...], a_ref[...], b_ref[...]
        ps = ps_ref[...]
        u = z[:, :POOL_DIM]
        pooled = _pooled(up_ref[...] * (i > 0).astype(F32), u, i * ts)
        dyp_ = dyp_ref[...]
        dps_ref[...] += _rowsum(dyp_ * _blockdot(pooled, pw_ref, 4, LANES))
        ext = jnp.concatenate([dyp_, dypn_ref[...] * (i < n - 1).astype(F32)], axis=0) * ps
        for gidx in range(4):
            sl = slice(gidx * LANES, (gidx + 1) * LANES)
            dpw_ref[gidx] += _dot_tn(pooled[:, sl], ext[:ts, sl])
        dpooled = _blockdot(ext, pwT_ref, 4, LANES)
        dm = dpooled / _pool_cnt(i * ts, ts + POOL_HALO)
        du = _pool_windows(dm, -1)[:ts] - dpooled[:ts]
        cq = z[:, 512:768]
        cqn, rstd_q = _rms(cq, qg_ref[...])
        dqf = _rope_bwd(dq_ref[...], c, a, b)
        dwq_ref[...] += _dot_tn(cqn, dqf)
        dcq, dqg_rows = _rms_bwd(cq, qg_ref[...], rstd_q, _dot(dqf, wqT_ref[...]))
        dqg_ref[...] += _rowsum(dqg_rows)
        ckv = z[:, 768:896]
        ckvn, rstd_kv = _rms(ckv, kvg_ref[...])
        dk_, dv_ = dk_ref[...], dv_ref[...]
        dwk_ref[...] += _dot_tn(ckvn, dk_)
        dwv_ref[...] += _dot_tn(ckvn, dv_)
        dckv, dkvg_rows = _rms_bwd(ckv, kvg_ref[...], rstd_kv, _dot(dk_, wkT_ref[...]) + _dot(dv_, wvT_ref[...]))
        dkvg_ref[...] += _rowsum(dkvg_rows)
        dkr = dk_[:, :LANES]
        for h in range(1, MLA_HEADS):
            dkr = dkr + dk_[:, h * LANES:(h + 1) * LANES]
        lane = lax.broadcasted_iota(jnp.int32, (ts, LANES), 1)
        dkr = jnp.where((lane >= 64) & (lane < 96), _rope_bwd(dkr, c, a, b), 0.0)
        dz = jnp.concatenate([du, dcq, dckv, dkr], axis=1)
        dz_ref[...] = dz.astype(BF16)
        x_ = x_ref[...]
        h, rstd = _rms(x_, g_ref[...])
        h_ref[...] = h.astype(BF16)
        dxn, dgr = _rms_bwd(x_, g_ref[...], rstd, _dot(dz, winT_ref[...]))
        dx_ref[...] = dy_ref[...] + dxn
        dg_ref[...] += _rowsum(dgr)

    ins = [x, dy, z, z, dq, dk, dv, dyp, dyp, *tabs, g, winT, pw, pwT, pscale, qg, wqT, kvg, wkT, wvT]
    in_specs = [_row(ts, D), _row(ts, D), _row(ts, D), _prev(POOL_HALO, POOL_DIM, ts), _row(ts, D), _row(ts, D),
                _row(ts, D), _row(ts, POOL_DIM), _next(POOL_HALO, POOL_DIM, ts, n), _row(ts, LANES),
                _row(ts, LANES), _row(ts, LANES)] + [_const(v.shape) for v in ins[12:]]
    acc_shapes = [(1, D), (4, LANES, LANES), (1, POOL_DIM), (1, Q_LORA), (Q_LORA, D), (1, KV_LORA), (KV_LORA, D),
                  (KV_LORA, D)]
    return pl.pallas_call(
        body, name="even_pre_bwd", grid=(n,), in_specs=in_specs,
        out_specs=[_row(ts, D)] * 3 + [_acc(s) for s in acc_shapes],
        out_shape=[_sds((S, D), F32), _sds((S, D), BF16), _sds((S, D), BF16)] + [_sds(s, F32) for s in acc_shapes],
        compiler_params=_cp())(*ins)


def _pick(n, options):
    for o in options:
        if n % o == 0:
            return o
    return n


def matmul_tn(name, a, b):
    S, K = a.shape
    N = b.shape[1]
    ts = _tile_rows(S, 1024)
    tk = _pick(K, (512, 256, 128))
    tn = _pick(N, (512, 256, 128))

    def body(a_ref, b_ref, o_ref):
        @pl.when(pl.program_id(2) == 0)
        def _():
            o_ref[...] = jnp.zeros_like(o_ref)

        o_ref[...] += _dot_tn(a_ref[...], b_ref[...])

    return pl.pallas_call(
        body, name=name, grid=(K // tk, N // tn, S // ts),
        in_specs=[pl.BlockSpec((ts, tk), lambda i, j, s: (s, i)), pl.BlockSpec((ts, tn), lambda i, j, s: (s, j))],
        out_specs=pl.BlockSpec((tk, tn), lambda i, j, s: (i, j)), out_shape=_sds((K, N), F32),
        compiler_params=pltpu.CompilerParams(dimension_semantics=("arbitrary",) * 3, vmem_limit_bytes=VMEM_LIMIT))(
            a, b)


def _my_id():
    return lax.axis_index("x") * 4 + lax.axis_index("y") * 2 + lax.axis_index("c")


def _peer(j):
    x, y, c = lax.axis_index("x"), lax.axis_index("y"), lax.axis_index("c")
    px = 1 - x if j & 4 else x
    py = 1 - y if j & 2 else y
    pc = 1 - c if j & 1 else c
    return (px, py, pc), px * 4 + py * 2 + pc


def all_gather(name, arrays):
    n = len(arrays)

    def body(*refs):
        ins, outs = refs[:n], refs[n:2 * n]
        send_sems, recv_sems, local_sems = refs[2 * n:]
        me = _my_id()
        local = [pltpu.make_async_copy(ins[k], outs[k].at[me], local_sems.at[k]) for k in range(n)]
        for cp in local:
            cp.start()
        sends = []
        for j in range(1, N_DEV):
            peer, _ = _peer(j)
            for k in range(n):
                cp = pltpu.make_async_remote_copy(
                    src_ref=ins[k], dst_ref=outs[k].at[me], send_sem=send_sems.at[k, j - 1],
                    recv_sem=recv_sems.at[k, j - 1], device_id=peer, device_id_type=pl.DeviceIdType.MESH)
                cp.start()
                sends.append(cp)
        for j in range(1, N_DEV):
            peer, pid = _peer(j)
            for k in range(n):
                pltpu.make_async_remote_copy(
                    src_ref=ins[k], dst_ref=outs[k].at[pid], send_sem=send_sems.at[k, j - 1],
                    recv_sem=recv_sems.at[k, j - 1], device_id=peer, device_id_type=pl.DeviceIdType.MESH).wait_recv()
        for cp in sends:
            cp.wait_send()
        for cp in local:
            cp.wait()

    any_spec = pl.BlockSpec(memory_space=pl.ANY)
    return pl.pallas_call(
        body, name=name, in_specs=[any_spec] * n, out_specs=[any_spec] * n,
        out_shape=[_sds((N_DEV,) + a.shape, a.dtype) for a in arrays],
        scratch_shapes=[pltpu.SemaphoreType.DMA((n, N_DEV - 1)), pltpu.SemaphoreType.DMA((n, N_DEV - 1)),
                        pltpu.SemaphoreType.DMA((n,))],
        compiler_params=pltpu.CompilerParams(has_side_effects=True))(*arrays)


def exchange(name, g):
    def body(g_ref, out_ref, send_sems, recv_sems, local_sem):
        me = _my_id()
        local = pltpu.make_async_copy(g_ref.at[me], out_ref.at[me], local_sem)
        local.start()
        sends = []
        for j in range(1, N_DEV):
            peer, pid = _peer(j)
            cp = pltpu.make_async_remote_copy(
                src_ref=g_ref.at[pid], dst_ref=out_ref.at[me], send_sem=send_sems.at[j - 1],
                recv_sem=recv_sems.at[j - 1], device_id=peer, device_id_type=pl.DeviceIdType.MESH)
            cp.start()
            sends.append(cp)
        for j in range(1, N_DEV):
            peer, pid = _peer(j)
            pltpu.make_async_remote_copy(
                src_ref=g_ref.at[me], dst_ref=out_ref.at[pid], send_sem=send_sems.at[j - 1],
                recv_sem=recv_sems.at[j - 1], device_id=peer, device_id_type=pl.DeviceIdType.MESH).wait_recv()
        for cp in sends:
            cp.wait_send()
        local.wait()

    any_spec = pl.BlockSpec(memory_space=pl.ANY)
    return pl.pallas_call(
        body, name=name, in_specs=[any_spec], out_specs=any_spec, out_shape=_sds(g.shape, g.dtype),
        scratch_shapes=[pltpu.SemaphoreType.DMA((N_DEV - 1,)), pltpu.SemaphoreType.DMA((N_DEV - 1,)),
                        pltpu.SemaphoreType.DMA],
        compiler_params=pltpu.CompilerParams(has_side_effects=True))(g)


def adamw(name, parts, w, m, v):
    R = w.shape[0]
    tr = _pick(R, (512, 256, 128, 64, 32, 16, 8))
    c1 = 1.0 - ADAM_B1 ** ADAM_STEP
    c2 = 1.0 - ADAM_B2 ** ADAM_STEP

    def body(p_ref, w_ref, m_ref, v_ref, g_ref, d_ref, nm_ref, nv_ref):
        g = p_ref[0]
        for s in range(1, N_DEV):
            g = g + p_ref[s]
        g_ref[...] = g
        m_ = ADAM_B1 * m_ref[...] + (1.0 - ADAM_B1) * g
        v_ = ADAM_B2 * v_ref[...] + (1.0 - ADAM_B2) * (g * g)
        nm_ref[...] = m_
        nv_ref[...] = v_
        d_ref[...] = -ADAM_LR * ((m_ / c1) / (jnp.sqrt(v_ / c2) + ADAM_EPS) + ADAM_WD * w_ref[...])

    row = pl.BlockSpec((tr, LANES), lambda i: (i, 0))
    return pl.pallas_call(
        body, name=name, grid=(R // tr,),
        in_specs=[pl.BlockSpec((N_DEV, tr, LANES), lambda i: (0, i, 0)), row, row, row], out_specs=[row] * 4,
        out_shape=[_sds((R, LANES), F32)] * 4, compiler_params=_cp())(parts, w, m, v)


WEIGHTS = ['ev_norm', 'ev_w_in', 'ev_pool_w', 'ev_pool_scale', 'ev_q_norm', 'ev_w_q_up', 'ev_kv_norm', 'ev_w_kv_up',
           'ev_w_out', 'od_norm', 'od_w_in', 'od_conv_w', 'od_conv_b', 'od_w_rgate', 'od_b_rgate', 'od_w_igate',
           'od_b_igate', 'od_lambda', 'od_w_out', 'xa_norm_x', 'xa_norm_mem', 'xa_w_q', 'xa_w_kv', 'xa_w_o',
           'ffn_norm', 'ffn_w_gate_up', 'ffn_w_down', 'final_norm']
SHARD_AXIS = {'ev_w_in': 1, 'ev_w_q_up': 2, 'ev_w_kv_up': 2, 'ev_w_out': 1, 'od_norm': 1, 'od_w_in': 2,
              'od_conv_w': 2, 'od_conv_b': 1, 'od_w_rgate': 2, 'od_b_rgate': 1, 'od_w_igate': 2, 'od_b_igate': 1,
              'od_lambda': 1, 'od_w_out': 1, 'xa_w_q': 1, 'xa_w_kv': 2, 'xa_w_o': 1, 'ffn_w_gate_up': 2,
              'ffn_w_down': 1}
SMALL_F32 = ('od_norm', 'od_conv_w', 'od_conv_b', 'od_b_rgate', 'od_b_igate', 'od_lambda')
SHARDED = [n for n in WEIGHTS if n in SHARD_AXIS]
REPLICATED = [n for n in WEIGHTS if n not in SHARD_AXIS]
ROW_ALIGN = 512


def _pack(flats, dtype):
    v = jnp.concatenate([f.reshape(-1).astype(dtype) for f in flats])
    pad = (-v.shape[0]) % (ROW_ALIGN * LANES)
    return jnp.pad(v, (0, pad)).reshape(-1, LANES)


def _pack_lead(flats, dtype):
    v = jnp.concatenate([f.reshape(N_DEV, -1).astype(dtype) for f in flats], axis=1)
    pad = (-v.shape[1]) % (ROW_ALIGN * LANES)
    return jnp.pad(v, ((0, 0), (0, pad))).reshape(N_DEV, -1, LANES)


def _unpack(flat, shapes):
    out, off = [], 0
    v = flat.reshape(-1)
    for s in shapes:
        n = 1
        for d in s:
            n *= d
        out.append(v[off:off + n].reshape(s))
        off += n
    return out


def _unpack_lead(flat, shapes):
    out, off = [], 0
    v = flat.reshape(N_DEV, -1)
    for s in shapes:
        n = 1
        for d in s:
            n *= d
        out.append(v[:, off:off + n].reshape((N_DEV,) + tuple(s)))
        off += n
    return out


def _to_full(stacked, axis):
    v = jnp.moveaxis(stacked, 0, axis)
    s = v.shape
    return v.reshape(s[:axis] + (s[axis] * s[axis + 1],) + s[axis + 2:])


def _to_shards(full, axis):
    s = full.shape
    v = full.reshape(s[:axis] + (N_DEV, s[axis] // N_DEV) + s[axis + 1:])
    return jnp.moveaxis(v, axis, 0)


def _pad_heads(w, nh, dh, lead):
    s = w.shape
    v = w.reshape(s[:-1] + (nh, dh))
    v = jnp.pad(v, [(0, 0)] * (len(s) - 1) + [(0, 0), (lead, LANES - dh - lead)])
    return v.reshape(s[:-1] + (nh * LANES,))


def _unpad_heads(w, nh, dh, lead):
    s = w.shape
    return w.reshape(s[:-1] + (nh, LANES))[..., lead:lead + dh].reshape(s[:-1] + (nh * dh,))


def _rope_tables(positions):
    inv_freq = 10000.0 ** (-jnp.arange(0, 32, 2, dtype=F32) / 32)
    ang = positions.astype(F32)[:, None] * inv_freq
    cos, sin = jnp.cos(ang), jnp.sin(ang)
    S = positions.shape[0]
    one, zero = jnp.ones((S, 64), F32), jnp.zeros((S, 64), F32)
    z16, z32 = jnp.zeros((S, 16), F32), jnp.zeros((S, 32), F32)
    c = jnp.concatenate([one, cos, cos, jnp.ones((S, 32), F32)], axis=1)
    a = jnp.concatenate([zero, z16, sin, z32], axis=1)
    b = jnp.concatenate([zero, -sin, z16, z32], axis=1)
    return c, a, b


def _t(w):
    return jnp.swapaxes(w, -1, -2)


def _col_to_row(v, tq):
    S = v.shape[0]
    return v[:, ::LANES].T.reshape(MLA_HEADS, S // tq, 1, tq)


def device_step(x, mem, positions, target, W):
    S = x.shape[0]
    G = {}
    tabs = _rope_tables(positions)
    keep = (positions != 0).astype(F32)[:, None]
    row = lambda v: v.reshape(1, -1)

    w_in = W['ev_w_in'][0]
    ev_win = jnp.concatenate([w_in[:, :896], _pad_heads(w_in[:, 896:], 1, 32, 64)], axis=1)
    ev_wq = _pad_heads(W['ev_w_q_up'][0], MLA_HEADS, QK_DIM, 0)
    kvw = W['ev_w_kv_up'][0].reshape(KV_LORA, MLA_HEADS, 128)
    ev_wk = _pad_heads(kvw[:, :, :64].reshape(KV_LORA, 512), MLA_HEADS, 64, 0)
    ev_wv = _pad_heads(kvw[:, :, 64:].reshape(KV_LORA, 512), MLA_HEADS, 64, 0)
    ev_wo_pool = W['ev_w_out'][0][:POOL_DIM]
    ev_wo_att = _t(_pad_heads(_t(W['ev_w_out'][0][POOL_DIM:]), MLA_HEADS, 64, 0))
    pw = W['ev_pool_w'][0].astype(BF16)
    ev_g, ps, qg, kvg = row(W['ev_norm'][0]), row(W['ev_pool_scale'][0]), row(W['ev_q_norm'][0]), row(W['ev_kv_norm'][0])

    z0, qp, kp, vp, ypool = even_pre(x, tabs, ev_g, ev_win, pw, ps, qg, ev_wq, kvg, ev_wk, ev_wv)
    o_att, lse = attn_fwd(qp, kp, vp)
    x1 = even_post(x, ypool, o_att, ev_wo_pool, ev_wo_att)

    def xa_ffn_fwd(xin, l):
        mn, km, vm = mem_kv(mem, row(W['xa_norm_mem'][l]), W['xa_w_kv'][l])
        xm = xattn_fwd(xin, row(W['xa_norm_x'][l]), W['xa_w_q'][l], km, vm, W['xa_w_o'][l])
        xo = ffn_fwd(xm, row(W['ffn_norm'][l]), W['ffn_w_gate_up'][l], W['ffn_w_down'][l])
        return xm, xo, (mn, km, vm)

    x2, x3, memkv0 = xa_ffn_fwd(x1, 0)

    od_g, lam = row(W['od_norm'][0]), row(W['od_lambda'][0])
    cw, cb = W['od_conv_w'][0], row(W['od_conv_b'][0])
    wr, wi = W['od_w_rgate'][0], W['od_w_igate'][0]
    br, bi = row(W['od_b_rgate'][0]), row(W['od_b_igate'][0])
    z1, a_t, b_t = odd_pre(x3, keep, od_g, W['od_w_in'][0], cw, cb, wr, br, wi, bi, lam)
    hseq = lru_scan(a_t, b_t)
    x4 = odd_post(x3, z1, hseq, W['od_w_out'][0])
    x5, x6, memkv1 = xa_ffn_fwd(x4, 1)

    dx, G['final_norm'], loss = loss_head(x6, target, row(W['final_norm']))
    G['final_norm'] = G['final_norm'].reshape(D)

    gnx, gnm, gwq, gwkv, gwo, gfn, gwgu, gwd = ([None, None] for _ in range(8))

    def xa_ffn_bwd(dy, xin, xm, memkv, l):
        mn, km, vm = memkv
        fg = row(W['ffn_norm'][l])
        hf, act, dgu = ffn_bwd_a(xm, dy, fg, W['ffn_w_gate_up'][l], _t(W['ffn_w_down'][l]))
        gwd[l] = matmul_tn("ffn_dwd", act, dy)
        gwgu[l] = matmul_tn("ffn_dwgu", hf, dgu)
        dxm, dfg = matmul_rms_bwd("ffn_bwd_b", xm, dy, dgu, fg, _t(W['ffn_w_gate_up'][l]))
        gfn[l] = dfg[0]
        dxin, o, dq, hx, dgx, dk, dv = xattn_bwd(xin, dxm, row(W['xa_norm_x'][l]), W['xa_w_q'][l],
                                                  _t(W['xa_w_q'][l]), km, vm, _t(W['xa_w_o'][l]))
        gnx[l] = dgx[0]
        gwo[l] = matmul_tn("xa_dwo", o, dxm)
        gwq[l] = matmul_tn("xa_dwq", hx, dq)
        dkv, dgm = mem_bwd(mem, row(W['xa_norm_mem'][l]), dk, dv, _t(W['xa_w_kv'][l]))
        gnm[l] = dgm[0]
        gwkv[l] = matmul_tn("xa_dwkv", mn, dkv)
        return dxin

    dx4 = xa_ffn_bwd(dx, x4, x5, memkv1, 1)

    y_od, dgate, dhs = odd_post_bwd(dx4, z1, hseq, _t(W['od_w_out'][0]))
    G['od_w_out'] = matmul_tn("od_dwout", y_od, dx4)[None]
    a_rev = jnp.concatenate([jnp.zeros((1, D), F32), jnp.flip(a_t, 0)[:-1]], axis=0)
    lam_grad = jnp.flip(lru_scan(a_rev, jnp.flip(dhs, 0)), 0)
    dxb, dcb, dbr, dbi, dlam, dwr, dwi = odd_gates_bwd(z1, lam_grad, hseq, keep, cw, cb, wr, _t(wr), br, wi, _t(wi),
                                                        bi, lam)
    dx3, h_od, dz1, dcw, dg_od = odd_pre_bwd(x3, dx4, z1, dxb, dgate, od_g, cw, _t(W['od_w_in'][0]))
    G['od_w_in'] = matmul_tn("od_dwin", h_od, dz1)[None]
    G['od_norm'], G['od_conv_w'], G['od_conv_b'] = dg_od, dcw[None], dcb
    G['od_w_rgate'], G['od_b_rgate'], G['od_w_igate'], G['od_b_igate'], G['od_lambda'] = (
        dwr[None], dbr, dwi[None], dbi, dlam)

    dx1 = xa_ffn_bwd(dx3, x1, x2, memkv0, 0)

    dyp, do_att, delta = even_post_bwd(dx1, o_att, _t(ev_wo_pool), _t(ev_wo_att))
    g_wo_pool = matmul_tn("ev_dwo_pool", ypool, dx1)
    g_wo_att = matmul_tn("ev_dwo_att", o_att, dx1)
    G['ev_w_out'] = jnp.concatenate([g_wo_pool, _t(_unpad_heads(_t(g_wo_att), MLA_HEADS, 64, 0))], axis=0)[None]
    tq = _tile_rows(S, 512)
    dq = attn_dq(qp, kp, vp, do_att, lse, delta)
    dk, dv = attn_dkv(qp, kp, vp, do_att, _col_to_row(lse, tq), _col_to_row(delta, tq))
    (grad_x, h_ev, dz0, dg_ev, dpw, dps, dqg, dwq, dkvg, dwk, dwv) = even_pre_bwd(
        x, dx1, z0, dq, dk, dv, dyp, tabs, ev_g, _t(ev_win), pw, _t(pw), ps, qg, _t(ev_wq), kvg, _t(ev_wk),
        _t(ev_wv))
    g_win = matmul_tn("ev_dwin", h_ev, dz0)
    G['ev_w_in'] = jnp.concatenate([g_win[:, :896], _unpad_heads(g_win[:, 896:], 1, 32, 64)], axis=1)[None]
    G['ev_norm'], G['ev_pool_w'], G['ev_pool_scale'], G['ev_q_norm'], G['ev_kv_norm'] = (
        dg_ev, dpw[None], dps, dqg, dkvg)
    G['ev_w_q_up'] = _unpad_heads(dwq, MLA_HEADS, QK_DIM, 0)[None]
    gk = _unpad_heads(dwk, MLA_HEADS, 64, 0).reshape(KV_LORA, MLA_HEADS, 64)
    gv = _unpad_heads(dwv, MLA_HEADS, 64, 0).reshape(KV_LORA, MLA_HEADS, 64)
    G['ev_w_kv_up'] = jnp.concatenate([gk, gv], axis=2).reshape(1, KV_LORA, MLA_HEADS * 128)

    G['xa_norm_x'], G['xa_norm_mem'], G['ffn_norm'] = jnp.stack(gnx), jnp.stack(gnm), jnp.stack(gfn)
    G['xa_w_q'], G['xa_w_kv'], G['xa_w_o'] = jnp.stack(gwq), jnp.stack(gwkv), jnp.stack(gwo)
    G['ffn_w_gate_up'], G['ffn_w_down'] = jnp.stack(gwgu), jnp.stack(gwd)
    return loss[0, 0], grad_x, G


def kernel(x, mem, positions, ev_norm, ev_w_in, ev_pool_w, ev_pool_scale, ev_q_norm, ev_w_q_up, ev_kv_norm, ev_w_kv_up, ev_w_out, od_norm, od_w_in, od_conv_w, od_conv_b, od_w_rgate, od_b_rgate, od_w_igate, od_b_igate, od_lambda, od_w_out, xa_norm_x, xa_norm_mem, xa_w_q, xa_w_kv, xa_w_o, ffn_norm, ffn_w_gate_up, ffn_w_down, final_norm, loss_target, m_ev_norm, m_ev_w_in, m_ev_pool_w, m_ev_pool_scale, m_ev_q_norm, m_ev_w_q_up, m_ev_kv_norm, m_ev_w_kv_up, m_ev_w_out, m_od_norm, m_od_w_in, m_od_conv_w, m_od_conv_b, m_od_w_rgate, m_od_b_rgate, m_od_w_igate, m_od_b_igate, m_od_lambda, m_od_w_out, m_xa_norm_x, m_xa_norm_mem, m_xa_w_q, m_xa_w_kv, m_xa_w_o, m_ffn_norm, m_ffn_w_gate_up, m_ffn_w_down, m_final_norm, v_ev_norm, v_ev_w_in, v_ev_pool_w, v_ev_pool_scale, v_ev_q_norm, v_ev_w_q_up, v_ev_kv_norm, v_ev_w_kv_up, v_ev_w_out, v_od_norm, v_od_w_in, v_od_conv_w, v_od_conv_b, v_od_w_rgate, v_od_b_rgate, v_od_w_igate, v_od_b_igate, v_od_lambda, v_od_w_out, v_xa_norm_x, v_xa_norm_mem, v_xa_w_q, v_xa_w_kv, v_xa_w_o, v_ffn_norm, v_ffn_w_gate_up, v_ffn_w_down, v_final_norm):
    args = dict(locals())
    w = {n: args[n] for n in WEIGHTS}
    m = {n: args['m_' + n] for n in WEIGHTS}
    v = {n: args['v_' + n] for n in WEIGHTS}
    big = [n for n in SHARDED if n not in SMALL_F32]
    small = [n for n in SHARDED if n in SMALL_F32]

    g_big, g_small = all_gather("gather_weights", [_pack([w[n] for n in big], BF16), _pack([w[n] for n in small], F32)])
    W = {n: w[n] for n in REPLICATED}
    for names, buf in ((big, g_big), (small, g_small)):
        for n, st in zip(names, _unpack_lead(buf, [w[n].shape for n in names])):
            W[n] = _to_full(st, SHARD_AXIS[n])

    loss, grad_x, G = device_step(x[0], mem[0], positions[0], loss_target[0], W)
    loss = lax.psum(loss, ("x", "y", "c"))

    rep_shapes = [w[n].shape for n in REPLICATED]
    rep_parts, = all_gather("gather_rep_grads", [_pack([G[n] for n in REPLICATED], F32)])
    rep = adamw("adamw_rep", rep_parts, *[_pack([d[n] for n in REPLICATED], F32) for d in (w, m, v)])
    rep = [dict(zip(REPLICATED, _unpack(r, rep_shapes))) for r in rep]

    sh_shapes = [w[n].shape for n in SHARDED]
    parts = exchange("exchange_grads", _pack_lead([_to_shards(G[n], SHARD_AXIS[n]) for n in SHARDED], F32))
    sh = adamw("adamw_shard", parts, *[_pack([d[n] for n in SHARDED], F32) for d in (w, m, v)])
    sh = [dict(zip(SHARDED, _unpack(r, sh_shapes))) for r in sh]

    outs = [{**rep[k], **sh[k]} for k in range(4)]
    return (loss, grad_x[None], *[outs[0][n] for n in WEIGHTS], *[outs[1][n] for n in WEIGHTS],
            *[outs[2][n] for n in WEIGHTS], *[outs[3][n] for n in WEIGHTS])
```

```python
import functools

import jax
import jax.numpy as jnp
from jax import lax
from jax.experimental import pallas as pl
from jax.experimental.pallas import tpu as pltpu

F32, BF16 = jnp.float32, jnp.bfloat16
N_DEV = 8
D = 1024
POOL_DIM = 512
POOL_WINDOWS = (2, 4, 8, 16)
MLA_HEADS = 8
QK_DIM = 96
Q_LORA, KV_LORA = 256, 128
LRU_HEADS, LRU_HEAD_DIM = 4, 256
LRU_C = 8.0
MEM_HEADS, MEM_HEAD_DIM = 4, 256
D_FF = 2816
RMS_EPS = 1e-6
ADAM_LR, ADAM_B1, ADAM_B2, ADAM_EPS, ADAM_WD, ADAM_STEP = 0.001, 0.9, 0.999, 1e-08, 0.01, 10
LANES = 128
POOL_HALO = 16
CONV_HALO = 8
VMEM_LIMIT = 60000 * 1024


def _cp():
    return pltpu.CompilerParams(dimension_semantics=("arbitrary",), vmem_limit_bytes=VMEM_LIMIT)


def _cp2():
    return pltpu.CompilerParams(dimension_semantics=("arbitrary", "arbitrary"), vmem_limit_bytes=VMEM_LIMIT)


def _row(ts, c, col=0):
    return pl.BlockSpec((ts, c), lambda i: (i, col))


def _prev(hr, c, ts, col=0):
    r = ts // hr
    return pl.BlockSpec((hr, c), lambda i: (jnp.maximum(i * r - 1, 0), col))


def _next(hr, c, ts, n, col=0):
    r = ts // hr
    return pl.BlockSpec((hr, c), lambda i: (jnp.minimum((i + 1) * r, n * r - 1), col))


def _const(shape):
    nd = len(shape)
    return pl.BlockSpec(tuple(shape), lambda i: (0,) * nd, pipeline_mode=pl.Buffered(1))


def _acc(shape):
    nd = len(shape)
    return pl.BlockSpec(tuple(shape), lambda i: (0,) * nd)


def _sds(shape, dt):
    return jax.ShapeDtypeStruct(tuple(shape), dt)


def _dot(a, b):
    return jnp.dot(a.astype(BF16), b.astype(BF16), preferred_element_type=F32)


def _dot_nt(a, b):
    return lax.dot_general(a.astype(BF16), b.astype(BF16), (((1,), (1,)), ((), ())), preferred_element_type=F32)


def _dot_tn(a, b):
    return lax.dot_general(a.astype(BF16), b.astype(BF16), (((0,), (0,)), ((), ())), preferred_element_type=F32)


def _rms(x, g):
    rstd = lax.rsqrt(jnp.mean(x * x, axis=-1, keepdims=True) + RMS_EPS)
    return x * rstd * g, rstd


def _rms_bwd(x, g, rstd, dy):
    xn = x * rstd
    dyg = dy * g
    dx = rstd * (dyg - xn * jnp.mean(dyg * xn, axis=-1, keepdims=True))
    return dx, dy * xn


def _rowsum(v):
    return jnp.sum(v, axis=0, keepdims=True)


def _roll(v, s, axis):
    n = v.shape[axis]
    return pltpu.roll(v, s % n, axis)


def _rope(t, c, a, b):
    k = t.shape[1] // LANES
    if k > 1:
        c, a, b = (jnp.tile(v, (1, k)) for v in (c, a, b))
    return t * c + _roll(t, 16, 1) * a + _roll(t, -16, 1) * b


def _rope_bwd(d, c, a, b):
    k = d.shape[1] // LANES
    if k > 1:
        c, a, b = (jnp.tile(v, (1, k)) for v in (c, a, b))
    return d * c + _roll(d * a, -16, 1) + _roll(d * b, 16, 1)


def _gelu(x):
    c = 0.7978845608028654
    t = jnp.tanh(c * (x + 0.044715 * x * x * x))
    return 0.5 * x * (1.0 + t), t


def _gelu_grad(x, t):
    c = 0.7978845608028654
    return 0.5 * (1.0 + t) + 0.5 * x * (1.0 - t * t) * c * (1.0 + 3.0 * 0.044715 * x * x)


def _blockdot(v, w_ref, nblk, width):
    return jnp.concatenate(
        [_dot(v[:, j * width:(j + 1) * width], w_ref[j]) for j in range(nblk)], axis=1)


def _pool_cnt(row0, rows):
    t = row0 + lax.broadcasted_iota(jnp.int32, (rows, POOL_DIM), 0)
    w = jnp.left_shift(2, lax.broadcasted_iota(jnp.int32, (rows, POOL_DIM), 1) // LANES)
    return jnp.minimum(t + 1, w).astype(F32)


def _pool_windows(ext, sign):
    s2 = ext + _roll(ext, sign * 1, 0)
    t = s2[:, LANES:]
    s4 = t + _roll(t, sign * 2, 0)
    t = s4[:, LANES:]
    s8 = t + _roll(t, sign * 4, 0)
    t = s8[:, LANES:]
    s16 = t + _roll(t, sign * 8, 0)
    return jnp.concatenate([s2[:, :LANES], s4[:, :LANES], s8[:, :LANES], s16], axis=1)


def _pooled(uprev, u, row0):
    ts = u.shape[0]
    ext = jnp.concatenate([uprev, u], axis=0)
    sums = _pool_windows(ext, 1)[POOL_HALO:]
    return sums / _pool_cnt(row0, ts) - u


def _expm1(x):
    return jnp.where(jnp.abs(x) < 0.01, x * (1.0 + 0.5 * x * (1.0 + x * (1.0 / 3.0))), jnp.exp(x) - 1.0)


def _softplus(z):
    return jnp.maximum(z, 0.0) + jnp.log1p(jnp.exp(-jnp.abs(z)))


def _tile_rows(s, want):
    while s % want:
        want //= 2
    return want


def even_pre(x, tabs, g, win, pw, pscale, qg, wq, kvg, wk, wv):
    S = x.shape[0]
    ts = _tile_rows(S, 512)

    def body(x_ref, xp_ref, c_ref, a_ref, b_ref, g_ref, win_ref, pw_ref, ps_ref, qg_ref, wq_ref, kvg_ref,
             wk_ref, wv_ref, z_ref, q_ref, k_ref, v_ref, yp_ref):
        i = pl.program_id(0)
        h, _ = _rms(x_ref[...], g_ref[...])
        z = _dot(h, win_ref[...])
        z_ref[...] = z
        hp, _ = _rms(xp_ref[...], g_ref[...])
        uprev = _dot(hp, win_ref[:, :POOL_DIM]) * (i > 0).astype(F32)
        u = z[:, :POOL_DIM]
        pooled = _pooled(uprev, u, i * ts)
        yp_ref[...] = (_blockdot(pooled, pw_ref, 4, LANES) * ps_ref[...]).astype(BF16)
        c, a, b = c_ref[...], a_ref[...], b_ref[...]
        cqn, _ = _rms(z[:, 512:768], qg_ref[...])
        q_ref[...] = (_rope(_dot(cqn, wq_ref[...]), c, a, b) * ATTN_SCALE).astype(BF16)
        ckvn, _ = _rms(z[:, 768:896], kvg_ref[...])
        krr = _rope(z[:, 896:1024], c, a, b)
        k_ref[...] = (_dot(ckvn, wk_ref[...]) + jnp.tile(krr, (1, MLA_HEADS))).astype(BF16)
        v_ref[...] = _dot(ckvn, wv_ref[...]).astype(BF16)

    ins = [x, x, *tabs, g, win, pw, pscale, qg, wq, kvg, wk, wv]
    in_specs = [_row(ts, D), _prev(POOL_HALO, D, ts), _row(ts, LANES), _row(ts, LANES), _row(ts, LANES)]
    in_specs += [_const(v.shape) for v in ins[5:]]
    return pl.pallas_call(
        body, name="even_pre", grid=(S // ts,), in_specs=in_specs,
        out_specs=[_row(ts, D)] * 4 + [_row(ts, POOL_DIM)],
        out_shape=[_sds((S, D), F32)] + [_sds((S, D), BF16)] * 3 + [_sds((S, POOL_DIM), BF16)],
        compiler_params=_cp())(*ins)


ATTN_SCALE = QK_DIM ** -0.5


def _pair_loop(lo, hi, step, init):
    pairs = (hi - lo) // 2
    carry = lax.fori_loop(0, pairs, lambda j, c: step(lo + 2 * j + 1, step(lo + 2 * j, c)), init)
    return lax.fori_loop(lo + 2 * pairs, hi, step, carry)


def attn_fwd(qp, kp, vp):
    S = qp.shape[0]
    tq = _tile_rows(S, 512)

    def body(q_ref, k_ref, v_ref, o_ref, lse_ref):
        qi = pl.program_id(1)
        q = q_ref[...]

        def block(ki, carry, masked):
            m, l, acc = carry
            off = pl.multiple_of(ki * tq, tq)
            s = _dot_nt(q, k_ref[pl.ds(off, tq), :])
            if masked:
                row = lax.broadcasted_iota(jnp.int32, (tq, tq), 0)
                col = lax.broadcasted_iota(jnp.int32, (tq, tq), 1)
                s = jnp.where(col <= row, s, -1e30)
            m_new = jnp.maximum(m, jnp.max(s, axis=1, keepdims=True))
            p = jnp.exp(s - m_new)
            alpha = jnp.exp(m - m_new)
            l = alpha * l + jnp.sum(p, axis=1, keepdims=True)
            acc = alpha * acc + _dot(p, v_ref[pl.ds(off, tq), :])
            return m_new, l, acc

        init = (jnp.full((tq, 1), -1e30, F32), jnp.zeros((tq, 1), F32), jnp.zeros((tq, LANES), F32))
        carry = _pair_loop(0, qi, lambda ki, c: block(ki, c, False), init)
        m, l, acc = block(qi, carry, True)
        o_ref[...] = acc / l
        lse_ref[...] = jnp.broadcast_to(m + jnp.log(l), (tq, LANES))

    blk = pl.BlockSpec((tq, LANES), lambda h, i: (i, h))
    full = pl.BlockSpec((S, LANES), lambda h, i: (0, h))
    return pl.pallas_call(
        body, name="attn_fwd", grid=(MLA_HEADS, S // tq), in_specs=[blk, full, full], out_specs=[blk, blk],
        out_shape=[_sds((S, D), F32), _sds((S, D), F32)], compiler_params=_cp2())(qp, kp, vp)


def even_post(x, ypool, o, wo_pool, wo_att):
    S = x.shape[0]
    ts = _tile_rows(S, 512)

    def body(x_ref, yp_ref, o_ref, wp_ref, wa_ref, out_ref):
        out_ref[...] = x_ref[...] + _dot(yp_ref[...], wp_ref[...]) + _dot(o_ref[...], wa_ref[...])

    return pl.pallas_call(
        body, name="even_post", grid=(S // ts,),
        in_specs=[_row(ts, D), _row(ts, POOL_DIM), _row(ts, D), _const(wo_pool.shape), _const(wo_att.shape)],
        out_specs=_row(ts, D), out_shape=_sds((S, D), F32), compiler_params=_cp())(x, ypool, o, wo_pool, wo_att)


def mem_kv(mem, g, wkv):
    M = mem.shape[0]

    def body(mem_ref, g_ref, w_ref, mn_ref, k_ref, v_ref):
        mn, _ = _rms(mem_ref[...], g_ref[...])
        mn_ref[...] = mn.astype(BF16)
        k_ref[...] = _dot(mn, w_ref[:, :D]).astype(BF16)
        v_ref[...] = _dot(mn, w_ref[:, D:]).astype(BF16)

    return pl.pallas_call(
        body, name="mem_kv", grid=(1,), in_specs=[_acc(mem.shape), _acc(g.shape), _acc(wkv.shape)],
        out_specs=[_acc((M, D))] * 3, out_shape=[_sds((M, D), BF16)] * 3, compiler_params=_cp())(mem, g, wkv)


def _xattn_heads(hx, wq_ref, k_ref, v_ref):
    q = _dot(hx, wq_ref[...])
    scale = MEM_HEAD_DIM ** -0.5
    ps, os_ = [], []
    for h in range(MEM_HEADS):
        sl = slice(h * MEM_HEAD_DIM, (h + 1) * MEM_HEAD_DIM)
        s = _dot_nt(q[:, sl], k_ref[:, sl]) * scale
        e = jnp.exp(s - jnp.max(s, axis=1, keepdims=True))
        p = e / jnp.sum(e, axis=1, keepdims=True)
        ps.append(p)
        os_.append(_dot(p, v_ref[:, sl]))
    return q, ps, jnp.concatenate(os_, axis=1)


def xattn_fwd(x, g, wq, kmem, vmem, wo):
    S = x.shape[0]
    ts = _tile_rows(S, 512)

    def body(x_ref, g_ref, wq_ref, k_ref, v_ref, wo_ref, out_ref):
        x_ = x_ref[...]
        hx, _ = _rms(x_, g_ref[...])
        _, _, o = _xattn_heads(hx, wq_ref, k_ref, v_ref)
        out_ref[...] = x_ + _dot(o, wo_ref[...])

    ins = [x, g, wq, kmem, vmem, wo]
    return pl.pallas_call(
        body, name="xattn_fwd", grid=(S // ts,), in_specs=[_row(ts, D)] + [_const(v.shape) for v in ins[1:]],
        out_specs=_row(ts, D), out_shape=_sds((S, D), F32), compiler_params=_cp())(*ins)


def xattn_bwd(x, dy, g, wq, wqT, kmem, vmem, woT):
    S = x.shape[0]
    M = kmem.shape[0]
    ts = _tile_rows(S, 512)
    scale = MEM_HEAD_DIM ** -0.5

    def body(x_ref, dy_ref, g_ref, wq_ref, wqT_ref, k_ref, v_ref, woT_ref,
             dx_ref, o_ref, dq_ref, hx_ref, dg_ref, dk_ref, dv_ref):
        i = pl.program_id(0)

        @pl.when(i == 0)
        def _():
            dg_ref[...] = jnp.zeros_like(dg_ref)
            dk_ref[...] = jnp.zeros_like(dk_ref)
            dv_ref[...] = jnp.zeros_like(dv_ref)

        x_, dy_ = x_ref[...], dy_ref[...]
        hx, rstd = _rms(x_, g_ref[...])
        q, ps, o = _xattn_heads(hx, wq_ref, k_ref, v_ref)
        hx_ref[...] = hx.astype(BF16)
        o_ref[...] = o.astype(BF16)
        do = _dot(dy_, woT_ref[...])
        dqs = []
        for h in range(MEM_HEADS):
            sl = slice(h * MEM_HEAD_DIM, (h + 1) * MEM_HEAD_DIM)
            p, do_h = ps[h], do[:, sl]
            dp = _dot_nt(do_h, v_ref[:, sl])
            ds = p * (dp - jnp.sum(p * dp, axis=1, keepdims=True)) * scale
            dqs.append(_dot(ds, k_ref[:, sl]))
            dk_ref[:, sl] += _dot_tn(ds, q[:, sl])
            dv_ref[:, sl] += _dot_tn(p, do_h)
        dq = jnp.concatenate(dqs, axis=1)
        dq_ref[...] = dq.astype(BF16)
        dxn, dgr = _rms_bwd(x_, g_ref[...], rstd, _dot(dq, wqT_ref[...]))
        dx_ref[...] = dy_ + dxn
        dg_ref[...] += _rowsum(dgr)

    ins = [x, dy, g, wq, wqT, kmem, vmem, woT]
    return pl.pallas_call(
        body, name="xattn_bwd", grid=(S // ts,),
        in_specs=[_row(ts, D), _row(ts, D)] + [_const(v.shape) for v in ins[2:]],
        out_specs=[_row(ts, D)] * 4 + [_acc((1, D)), _acc((M, D)), _acc((M, D))],
        out_shape=[_sds((S, D), F32)] + [_sds((S, D), BF16)] * 3 + [_sds((1, D), F32), _sds((M, D), F32),
                                                                    _sds((M, D), F32)],
        compiler_params=_cp())(*ins)


def mem_bwd(mem, g, dk, dv, wkvT):
    M = mem.shape[0]

    def body(mem_ref, g_ref, dk_ref, dv_ref, w_ref, dkv_ref, dg_ref):
        dkv = jnp.concatenate([dk_ref[...], dv_ref[...]], axis=1)
        dkv_ref[...] = dkv.astype(BF16)
        _, rstd = _rms(mem_ref[...], g_ref[...])
        dg_ref[...] = _rowsum(_dot(dkv, w_ref[...]) * (mem_ref[...] * rstd))

    ins = [mem, g, dk, dv, wkvT]
    return pl.pallas_call(
        body, name="mem_bwd", grid=(1,), in_specs=[_acc(v.shape) for v in ins],
        out_specs=[_acc((M, 2 * D)), _acc((1, D))], out_shape=[_sds((M, 2 * D), BF16), _sds((1, D), F32)],
        compiler_params=_cp())(*ins)


FF_CHUNK = 2 * D_FF // N_DEV
FF_HALF = N_DEV // 2


def ffn_fwd(x, g, wgu, wd):
    S = x.shape[0]
    ts = _tile_rows(S, 256)

    def body(x_ref, g_ref, wgu_ref, wd_ref, out_ref):
        x_ = x_ref[...]
        hf = _rms(x_, g_ref[...])[0].astype(BF16)
        out = x_
        for j in range(FF_HALF):
            gg = _dot(hf, wgu_ref[j])
            out = out + _dot(gg * jax.nn.sigmoid(gg) * _dot(hf, wgu_ref[j + FF_HALF]), wd_ref[j])
        out_ref[...] = out

    ins = [x, g, wgu, wd]
    return pl.pallas_call(
        body, name="ffn_fwd", grid=(S // ts,), in_specs=[_row(ts, D)] + [_const(v.shape) for v in ins[1:]],
        out_specs=_row(ts, D), out_shape=_sds((S, D), F32), compiler_params=_cp())(*ins)


def ffn_bwd_a(x, dy, g, wgu, wdT):
    S = x.shape[0]
    ts = _tile_rows(S, 256)

    def body(x_ref, dy_ref, g_ref, wgu_ref, wdT_ref, hf_ref, act_ref, dgu_ref):
        hf = _rms(x_ref[...], g_ref[...])[0].astype(BF16)
        hf_ref[...] = hf
        dy_ = dy_ref[...].astype(BF16)
        for j in range(FF_HALF):
            gg, uu = _dot(hf, wgu_ref[j]), _dot(hf, wgu_ref[j + FF_HALF])
            sg = jax.nn.sigmoid(gg)
            silu = gg * sg
            act_ref[j] = (silu * uu).astype(BF16)
            dact = _dot(dy_, wdT_ref[j])
            dgu_ref[j] = (dact * uu * (sg * (1.0 + gg * (1.0 - sg)))).astype(BF16)
            dgu_ref[j + FF_HALF] = (dact * silu).astype(BF16)

    ins = [x, dy, g, wgu, wdT]
    chunked = lambda c: pl.BlockSpec((c, ts, FF_CHUNK), lambda i: (0, i, 0))
    return pl.pallas_call(
        body, name="ffn_bwd_a", grid=(S // ts,),
        in_specs=[_row(ts, D), _row(ts, D)] + [_const(v.shape) for v in ins[2:]],
        out_specs=[_row(ts, D), chunked(FF_HALF), chunked(N_DEV)],
        out_shape=[_sds((S, D), BF16), _sds((FF_HALF, S, FF_CHUNK), BF16), _sds((N_DEV, S, FF_CHUNK), BF16)],
        compiler_params=_cp())(*ins)


def ffn_bwd_b(x, dy, dgu, g, wguT):
    S = x.shape[0]
    ts = _tile_rows(S, 512)

    def body(x_ref, dy_ref, dgu_ref, g_ref, wT_ref, dx_ref, dg_ref):
        @pl.when(pl.program_id(0) == 0)
        def _():
            dg_ref[...] = jnp.zeros_like(dg_ref)

        dh = _dot(dgu_ref[0], wT_ref[0])
        for j in range(1, N_DEV):
            dh = dh + _dot(dgu_ref[j], wT_ref[j])
        x_ = x_ref[...]
        _, rstd = _rms(x_, g_ref[...])
        dxn, dgr = _rms_bwd(x_, g_ref[...], rstd, dh)
        dx_ref[...] = dy_ref[...] + dxn
        dg_ref[...] += _rowsum(dgr)

    return pl.pallas_call(
        body, name="ffn_bwd_b", grid=(S // ts,),
        in_specs=[_row(ts, D), _row(ts, D), pl.BlockSpec((N_DEV, ts, FF_CHUNK), lambda i: (0, i, 0)),
                  _const(g.shape), _const(wguT.shape)],
        out_specs=[_row(ts, D), _acc((1, D))], out_shape=[_sds((S, D), F32), _sds((1, D), F32)],
        compiler_params=_cp())(x, dy, dgu, g, wguT)


def _conv_fwd(xprev, xbp, cw_ref, cb):
    ext = jnp.concatenate([xprev, xbp], axis=0)
    acc = cb + cw_ref[3:4, :] * xbp
    for k in range(3):
        acc = acc + cw_ref[k:k + 1, :] * _roll(ext, 3 - k, 0)[CONV_HALO:]
    return acc


def _gates(xb, keep, wr_ref, br, wi_ref, bi, lam):
    r = jax.nn.sigmoid(_blockdot(xb, wr_ref, LRU_HEADS, LRU_HEAD_DIM) + br)
    ig = jax.nn.sigmoid(_blockdot(xb, wi_ref, LRU_HEADS, LRU_HEAD_DIM) + bi)
    sp = _softplus(-lam)
    log_a = -LRU_C * r * sp
    a = jnp.exp(log_a)
    mult = jnp.sqrt(jnp.maximum(-_expm1(2.0 * log_a), 0.0))
    return r, ig, sp, a, mult


def odd_pre(x, keep, g, win, cw, cb, wr, br, wi, bi, lam):
    S = x.shape[0]
    ts = _tile_rows(S, 512)

    def body(x_ref, xp_ref, keep_ref, g_ref, win_ref, cw_ref, cb_ref, wr_ref, br_ref, wi_ref, bi_ref, lam_ref,
             z_ref, a_ref, b_ref):
        i = pl.program_id(0)
        h, _ = _rms(x_ref[...], g_ref[...])
        z = _dot(h, win_ref[...])
        z_ref[...] = z
        hp, _ = _rms(xp_ref[...], g_ref[...])
        xprev = _dot(hp, win_ref[:, D:]) * (i > 0).astype(F32)
        xb = _conv_fwd(xprev, z[:, D:], cw_ref, cb_ref[...])
        keep_ = keep_ref[...]
        _, ig, _, a, mult = _gates(xb, keep_, wr_ref, br_ref[...], wi_ref, bi_ref[...], lam_ref[...])
        a_ref[...] = a * keep_
        b_ref[...] = jnp.where(keep_ > 0.0, mult, 1.0) * (ig * xb)

    ins = [x, x, keep, g, win, cw, cb, wr, br, wi, bi, lam]
    return pl.pallas_call(
        body, name="odd_pre", grid=(S // ts,),
        in_specs=[_row(ts, D), _prev(CONV_HALO, D, ts), _row(ts, 1)] + [_const(v.shape) for v in ins[3:]],
        out_specs=[_row(ts, 2 * D), _row(ts, D), _row(ts, D)],
        out_shape=[_sds((S, 2 * D), F32), _sds((S, D), F32), _sds((S, D), F32)], compiler_params=_cp())(*ins)


def lru_scan(a, b, reverse=False):
    S = a.shape[0]
    ts = _tile_rows(S, 512)
    n = S // ts
    groups = ts // 8

    def body(a_ref, an_ref, b_ref, h_ref, carry_ref, ash_ref):
        i = pl.program_id(0)

        @pl.when(i == 0)
        def _():
            carry_ref[...] = jnp.zeros_like(carry_ref)

        rid = lax.broadcasted_iota(jnp.int32, (8, D), 0)
        if reverse:
            ext = jnp.concatenate([a_ref[...], an_ref[...] * (i > 0).astype(F32)], axis=0)
            ash_ref[...] = _roll(ext, -1, 0)[:ts]
        src = ash_ref if reverse else a_ref

        def group(j, carry):
            off = pl.multiple_of((groups - 1 - j if reverse else j) * 8, 8)
            a8, b8 = src[pl.ds(off, 8), :], b_ref[pl.ds(off, 8), :]
            for k in (1, 2, 4):
                inside = (rid < 8 - k) if reverse else (rid >= k)
                sh = -k if reverse else k
                a_sh = jnp.where(inside, _roll(a8, sh, 0), 1.0)
                b_sh = jnp.where(inside, _roll(b8, sh, 0), 0.0)
                b8 = a8 * b_sh + b8
                a8 = a8 * a_sh
            h8 = a8 * carry + b8
            h_ref[pl.ds(off, 8), :] = h8
            return h8[0:1, :] if reverse else h8[7:8, :]

        carry_ref[...] = lax.fori_loop(0, groups, group, carry_ref[...])

    if reverse:
        r = ts // 8
        tile = pl.BlockSpec((ts, D), lambda i: (n - 1 - i, 0))
        halo = pl.BlockSpec((8, D), lambda i: (jnp.minimum((n - i) * r, n * r - 1), 0))
    else:
        tile, halo = _row(ts, D), _prev(8, D, ts)
    return pl.pallas_call(
        body, name="lru_scan_rev" if reverse else "lru_scan", grid=(n,), in_specs=[tile, halo, tile],
        out_specs=tile, out_shape=_sds((S, D), F32),
        scratch_shapes=[pltpu.VMEM((1, D), F32), pltpu.VMEM((ts, D), F32)], compiler_params=_cp())(a, a, b)


def odd_post(x, z, hseq, wout):
    S = x.shape[0]
    ts = _tile_rows(S, 512)

    def body(x_ref, gate_ref, h_ref, w_ref, out_ref):
        gl, _ = _gelu(gate_ref[...])
        out_ref[...] = x_ref[...] + _dot(gl * h_ref[...], w_ref[...])

    return pl.pallas_call(
        body, name="odd_post", grid=(S // ts,),
        in_specs=[_row(ts, D), _row(ts, D), _row(ts, D), _const(wout.shape)],
        out_specs=_row(ts, D), out_shape=_sds((S, D), F32), compiler_params=_cp())(x, z, hseq, wout)


def odd_post_bwd(dy, z, hseq, woutT):
    S = dy.shape[0]
    ts = _tile_rows(S, 512)

    def body(dy_ref, gate_ref, h_ref, w_ref, y_ref, dgate_ref, dh_ref):
        gate, hs = gate_ref[...], h_ref[...]
        gl, t = _gelu(gate)
        y_ref[...] = (gl * hs).astype(BF16)
        dyy = _dot(dy_ref[...], w_ref[...])
        dgate_ref[...] = dyy * hs * _gelu_grad(gate, t)
        dh_ref[...] = dyy * gl

    return pl.pallas_call(
        body, name="odd_post_bwd", grid=(S // ts,),
        in_specs=[_row(ts, D), _row(ts, D), _row(ts, D), _const(woutT.shape)],
        out_specs=[_row(ts, D)] * 3, out_shape=[_sds((S, D), BF16), _sds((S, D), F32), _sds((S, D), F32)],
        compiler_params=_cp())(dy, z, hseq, woutT)


def odd_gates_bwd(z, lam_grad, hseq, keep, cw, cb, wr, wrT, br, wi, wiT, bi, lam):
    S = z.shape[0]
    ts = _tile_rows(S, 512)

    def body(xbp_ref, xbpp_ref, lg_ref, h_ref, hp_ref, keep_ref, cw_ref, cb_ref, wr_ref, wrT_ref, br_ref, wi_ref,
             wiT_ref, bi_ref, lam_ref, dxb_ref, dcb_ref, dbr_ref, dbi_ref, dlam_ref, dwr_ref, dwi_ref):
        i = pl.program_id(0)

        @pl.when(i == 0)
        def _():
            for ref in (dcb_ref, dbr_ref, dbi_ref, dlam_ref, dwr_ref, dwi_ref):
                ref[...] = jnp.zeros_like(ref)

        first = (i > 0).astype(F32)
        xb = _conv_fwd(xbpp_ref[...] * first, xbp_ref[...], cw_ref, cb_ref[...])
        keep_ = keep_ref[...]
        lam_ = lam_ref[...]
        r, ig, sp, a, mult = _gates(xb, keep_, wr_ref, br_ref[...], wi_ref, bi_ref[...], lam_)
        hs = h_ref[...]
        hprev = _roll(jnp.concatenate([hp_ref[...] * first, hs], axis=0), 1, 0)[CONV_HALO:]
        lg = lg_ref[...]
        da = lg * hprev * keep_
        ixb = ig * xb
        dmult = lg * ixb * keep_
        dixb = lg * jnp.where(keep_ > 0.0, mult, 1.0)
        dlog_a = da * a - dmult * jnp.where(mult > 0.0, a * a / mult, 0.0)
        dr = dlog_a * (-LRU_C * sp)
        dlam_ref[...] += _rowsum(dlog_a * (-LRU_C * r)) * (-jax.nn.sigmoid(-lam_))
        dpr = dr * r * (1.0 - r)
        dpi = dixb * xb * ig * (1.0 - ig)
        dbr_ref[...] += _rowsum(dpr)
        dbi_ref[...] += _rowsum(dpi)
        dxb = dixb * ig
        parts = []
        for h in range(LRU_HEADS):
            sl = slice(h * LRU_HEAD_DIM, (h + 1) * LRU_HEAD_DIM)
            dwr_ref[h] += _dot_tn(xb[:, sl], dpr[:, sl])
            dwi_ref[h] += _dot_tn(xb[:, sl], dpi[:, sl])
            parts.append(_dot(dpr[:, sl], wrT_ref[h]) + _dot(dpi[:, sl], wiT_ref[h]))
        dxb = dxb + jnp.concatenate(parts, axis=1)
        dxb_ref[...] = dxb
        dcb_ref[...] += _rowsum(dxb)

    ins = [z, z, lam_grad, hseq, hseq, keep, cw, cb, wr, wrT, br, wi, wiT, bi, lam]
    in_specs = [_row(ts, D, 1), _prev(CONV_HALO, D, ts, 1), _row(ts, D), _row(ts, D), _prev(CONV_HALO, D, ts),
                _row(ts, 1)] + [_const(v.shape) for v in ins[6:]]
    gshape = (LRU_HEADS, LRU_HEAD_DIM, LRU_HEAD_DIM)
    return pl.pallas_call(
        body, name="odd_gates_bwd", grid=(S // ts,), in_specs=in_specs,
        out_specs=[_row(ts, D)] + [_acc((1, D))] * 4 + [_acc(gshape)] * 2,
        out_shape=[_sds((S, D), F32)] + [_sds((1, D), F32)] * 4 + [_sds(gshape, F32)] * 2,
        compiler_params=_cp())(*ins)


def odd_pre_bwd(x, dy, z, dxb, dgate, g, cw, winT):
    S = x.shape[0]
    ts = _tile_rows(S, 512)
    n = S // ts

    def body(x_ref, dy_ref, xbp_ref, xbpp_ref, dxb_ref, dxbn_ref, dgate_ref, g_ref, cw_ref, winT_ref,
             dx_ref, h_ref, dz_ref, dcw_ref, dg_ref):
        i = pl.program_id(0)

        @pl.when(i == 0)
        def _():
            dcw_ref[...] = jnp.zeros_like(dcw_ref)
            dg_ref[...] = jnp.zeros_like(dg_ref)

        dxb = dxb_ref[...]
        extd = jnp.concatenate([dxb, dxbn_ref[...] * (i < n - 1).astype(F32)], axis=0)
        extx = jnp.concatenate([xbpp_ref[...] * (i > 0).astype(F32), xbp_ref[...]], axis=0)
        dxbp = cw_ref[3:4, :] * dxb
        dcw_ref[3:4, :] += _rowsum(dxb * xbp_ref[...])
        for k in range(3):
            dxbp = dxbp + cw_ref[k:k + 1, :] * _roll(extd, -(3 - k), 0)[:ts]
            dcw_ref[k:k + 1, :] += _rowsum(dxb * _roll(extx, 3 - k, 0)[CONV_HALO:])
        dz = jnp.concatenate([dgate_ref[...], dxbp], axis=1)
        dz_ref[...] = dz.astype(BF16)
        x_ = x_ref[...]
        h, rstd = _rms(x_, g_ref[...])
        h_ref[...] = h.astype(BF16)
        dxn, dgr = _rms_bwd(x_, g_ref[...], rstd, _dot(dz, winT_ref[...]))
        dx_ref[...] = dy_ref[...] + dxn
        dg_ref[...] += _rowsum(dgr)

    ins = [x, dy, z, z, dxb, dxb, dgate, g, cw, winT]
    in_specs = [_row(ts, D), _row(ts, D), _row(ts, D, 1), _prev(CONV_HALO, D, ts, 1), _row(ts, D),
                _next(CONV_HALO, D, ts, n), _row(ts, D)] + [_const(v.shape) for v in ins[7:]]
    return pl.pallas_call(
        body, name="odd_pre_bwd", grid=(n,), in_specs=in_specs,
        out_specs=[_row(ts, D), _row(ts, D), _row(ts, 2 * D), _acc((4, D)), _acc((1, D))],
        out_shape=[_sds((S, D), F32), _sds((S, D), BF16), _sds((S, 2 * D), BF16), _sds((4, D), F32),
                   _sds((1, D), F32)],
        compiler_params=_cp())(*ins)


def loss_head(x, target, g):
    S = x.shape[0]
    ts = _tile_rows(S, 512)

    def body(x_ref, t_ref, g_ref, dx_ref, dg_ref, loss_ref):
        @pl.when(pl.program_id(0) == 0)
        def _():
            dg_ref[...] = jnp.zeros_like(dg_ref)
            loss_ref[...] = jnp.zeros_like(loss_ref)

        x_ = x_ref[...]
        y, rstd = _rms(x_, g_ref[...])
        err = y - t_ref[...]
        loss_ref[...] += 0.5 * _rowsum(jnp.mean(err * err, axis=1, keepdims=True))
        dxn, dgr = _rms_bwd(x_, g_ref[...], rstd, err * (1.0 / D))
        dx_ref[...] = dxn
        dg_ref[...] += _rowsum(dgr)

    return pl.pallas_call(
        body, name="loss_head", grid=(S // ts,), in_specs=[_row(ts, D), _row(ts, D), _const(g.shape)],
        out_specs=[_row(ts, D), _acc((1, D)), _acc((1, 1))],
        out_shape=[_sds((S, D), F32), _sds((1, D), F32), _sds((1, 1), F32)], compiler_params=_cp())(x, target, g)


def even_post_bwd(dy, o, woT_pool, woT_att):
    S = dy.shape[0]
    ts = _tile_rows(S, 512)

    def body(dy_ref, o_ref, wp_ref, wa_ref, dyp_ref, do_ref, delta_ref):
        dy_ = dy_ref[...]
        dyp_ref[...] = _dot(dy_, wp_ref[...])
        do = _dot(dy_, wa_ref[...])
        do_ref[...] = do.astype(BF16)
        prod = do * o_ref[...]
        delta_ref[...] = jnp.concatenate(
            [jnp.broadcast_to(jnp.sum(prod[:, h * LANES:(h + 1) * LANES], axis=1, keepdims=True), (ts, LANES))
             for h in range(MLA_HEADS)], axis=1)

    return pl.pallas_call(
        body, name="even_post_bwd", grid=(S // ts,),
        in_specs=[_row(ts, D), _row(ts, D), _const(woT_pool.shape), _const(woT_att.shape)],
        out_specs=[_row(ts, POOL_DIM), _row(ts, D), _row(ts, D)],
        out_shape=[_sds((S, POOL_DIM), F32), _sds((S, D), BF16), _sds((S, D), F32)],
        compiler_params=_cp())(dy, o, woT_pool, woT_att)


def attn_bwd(qp, kp, vp, do, lse_row, delta_row):
    S = qp.shape[0]
    tk = _tile_rows(S, 512)
    nq = S // tk

    def body(q_ref, k_ref, v_ref, do_ref, lse_ref, delta_ref, dq_ref, dk_ref, dv_ref):
        kj = pl.program_id(1)

        @pl.when(kj == 0)
        def _():
            dq_ref[...] = jnp.zeros_like(dq_ref)

        k, v = k_ref[...], v_ref[...]

        def block(qi, carry, masked):
            dk, dv = carry
            off = pl.multiple_of(qi * tk, tk)
            q = q_ref[pl.ds(off, tk), :]
            do_ = do_ref[pl.ds(off, tk), :]
            st = _dot_nt(k, q)
            if masked:
                row = lax.broadcasted_iota(jnp.int32, (tk, tk), 0)
                col = lax.broadcasted_iota(jnp.int32, (tk, tk), 1)
                st = jnp.where(col >= row, st, -1e30)
            pt = jnp.exp(st - lse_ref[qi])
            dv = dv + _dot(pt, do_)
            dst = (pt * (_dot_nt(v, do_) - delta_ref[qi])).astype(BF16)
            dk = dk + _dot(dst, q)
            dq_ref[pl.ds(off, tk), :] += _dot_tn(dst, k)
            return dk, dv

        zero = jnp.zeros((tk, LANES), F32)
        carry = block(kj, (zero, zero), True)
        dk, dv = _pair_loop(kj + 1, nq, lambda qi, c: block(qi, c, False), carry)
        dk_ref[...] = dk
        dv_ref[...] = dv

    blk = pl.BlockSpec((tk, LANES), lambda h, j: (j, h))
    full = pl.BlockSpec((S, LANES), lambda h, j: (0, h))
    rowv = pl.BlockSpec((None, nq, 1, tk), lambda h, j: (h, 0, 0, 0))
    return pl.pallas_call(
        body, name="attn_bwd", grid=(MLA_HEADS, nq), in_specs=[full, blk, blk, full, rowv, rowv],
        out_specs=[full, blk, blk], out_shape=[_sds((S, D), F32)] * 3, compiler_params=_cp2())(
            qp, kp, vp, do, lse_row, delta_row)


def even_pre_bwd(x, dy, z, dq, dk, dv, dyp, tabs, g, winT, pw, pwT, pscale, qg, wqT, kvg, wkT, wvT):
    S = x.shape[0]
    ts = _tile_rows(S, 512)
    n = S // ts

    def body(x_ref, dy_ref, z_ref, up_ref, dq_ref, dk_ref, dv_ref, dyp_ref, dypn_ref, c_ref, a_ref, b_ref,
             g_ref, winT_ref, pw_ref, pwT_ref, ps_ref, qg_ref, wqT_ref, kvg_ref, wkT_ref, wvT_ref,
             dx_ref, h_ref, dz_ref, dg_ref, dpw_ref, dps_ref, dqg_ref, dwq_ref, dkvg_ref, dwk_ref, dwv_ref):
        i = pl.program_id(0)

        @pl.when(i == 0)
        def _():
            for ref in (dg_ref, dpw_ref, dps_ref, dqg_ref, dwq_ref, dkvg_ref, dwk_ref, dwv_ref):
                ref[...] = jnp.zeros_like(ref)

        z = z_ref[...]
        c, a, b = c_ref[...], a_ref[...], b_ref[...]
        ps = ps_ref[...]
        u = z[:, :POOL_DIM]
        pooled = _pooled(up_ref[...] * (i > 0).astype(F32), u, i * ts)
        dyp_ = dyp_ref[...]
        dps_ref[...] += _rowsum(dyp_ * _blockdot(pooled, pw_ref, 4, LANES))
        ext = jnp.concatenate([dyp_, dypn_ref[...] * (i < n - 1).astype(F32)], axis=0) * ps
        for gidx in range(4):
            sl = slice(gidx * LANES, (gidx + 1) * LANES)
            dpw_ref[gidx] += _dot_tn(pooled[:, sl], ext[:ts, sl])
        dpooled = _blockdot(ext, pwT_ref, 4, LANES)
        dm = dpooled / _pool_cnt(i * ts, ts + POOL_HALO)
        du = _pool_windows(dm, -1)[:ts] - dpooled[:ts]
        cq = z[:, 512:768]
        cqn, rstd_q = _rms(cq, qg_ref[...])
        dqf = _rope_bwd(dq_ref[...] * ATTN_SCALE, c, a, b)
        dwq_ref[...] += _dot_tn(cqn, dqf)
        dcq, dqg_rows = _rms_bwd(cq, qg_ref[...], rstd_q, _dot(dqf, wqT_ref[...]))
        dqg_ref[...] += _rowsum(dqg_rows)
        ckv = z[:, 768:896]
        ckvn, rstd_kv = _rms(ckv, kvg_ref[...])
        dk_, dv_ = dk_ref[...], dv_ref[...]
        dwk_ref[...] += _dot_tn(ckvn, dk_)
        dwv_ref[...] += _dot_tn(ckvn, dv_)
        dckv, dkvg_rows = _rms_bwd(ckv, kvg_ref[...], rstd_kv, _dot(dk_, wkT_ref[...]) + _dot(dv_, wvT_ref[...]))
        dkvg_ref[...] += _rowsum(dkvg_rows)
        dkr = dk_[:, :LANES]
        for h in range(1, MLA_HEADS):
            dkr = dkr + dk_[:, h * LANES:(h + 1) * LANES]
        lane = lax.broadcasted_iota(jnp.int32, (ts, LANES), 1)
        dkr = jnp.where((lane >= 64) & (lane < 96), _rope_bwd(dkr, c, a, b), 0.0)
        dz = jnp.concatenate([du, dcq, dckv, dkr], axis=1)
        dz_ref[...] = dz.astype(BF16)
        x_ = x_ref[...]
        h, rstd = _rms(x_, g_ref[...])
        h_ref[...] = h.astype(BF16)
        dxn, dgr = _rms_bwd(x_, g_ref[...], rstd, _dot(dz, winT_ref[...]))
        dx_ref[...] = dy_ref[...] + dxn
        dg_ref[...] += _rowsum(dgr)

    ins = [x, dy, z, z, dq, dk, dv, dyp, dyp, *tabs, g, winT, pw, pwT, pscale, qg, wqT, kvg, wkT, wvT]
    in_specs = [_row(ts, D), _row(ts, D), _row(ts, D), _prev(POOL_HALO, POOL_DIM, ts), _row(ts, D), _row(ts, D),
                _row(ts, D), _row(ts, POOL_DIM), _next(POOL_HALO, POOL_DIM, ts, n), _row(ts, LANES),
                _row(ts, LANES), _row(ts, LANES)] + [_const(v.shape) for v in ins[12:]]
    acc_shapes = [(1, D), (4, LANES, LANES), (1, POOL_DIM), (1, Q_LORA), (Q_LORA, D), (1, KV_LORA), (KV_LORA, D),
                  (KV_LORA, D)]
    return pl.pallas_call(
        body, name="even_pre_bwd", grid=(n,), in_specs=in_specs,
        out_specs=[_row(ts, D)] * 3 + [_acc(s) for s in acc_shapes],
        out_shape=[_sds((S, D), F32), _sds((S, D), BF16), _sds((S, D), BF16)] + [_sds(s, F32) for s in acc_shapes],
        compiler_params=_cp())(*ins)


def _pick(n, options):
    for o in options:
        if n % o == 0:
            return o
    return n


def matmul_tn(name, a, b):
    S = a.shape[-2]
    ts = _tile_rows(S, 1024)

    def body(a_ref, b_ref, o_ref):
        @pl.when(pl.program_id(2) == 0)
        def _():
            o_ref[...] = jnp.zeros_like(o_ref)

        o_ref[...] += _dot_tn(a_ref[...], b_ref[...])

    if a.ndim == 3:
        C, _, K = a.shape
        N = b.shape[1]
        tn = _pick(N, (512, 256, 128))
        grid = (C, N // tn, S // ts)
        in_specs = [pl.BlockSpec((None, ts, K), lambda c, j, s: (c, s, 0)),
                    pl.BlockSpec((ts, tn), lambda c, j, s: (s, j))]
        out_spec, out_shape = pl.BlockSpec((None, K, tn), lambda c, j, s: (c, 0, j)), (C, K, N)
    elif b.ndim == 3:
        C, _, N = b.shape
        K = a.shape[1]
        tk = _pick(K, (512, 256, 128))
        grid = (C, K // tk, S // ts)
        in_specs = [pl.BlockSpec((ts, tk), lambda c, i, s: (s, i)),
                    pl.BlockSpec((None, ts, N), lambda c, i, s: (c, s, 0))]
        out_spec, out_shape = pl.BlockSpec((None, tk, N), lambda c, i, s: (c, i, 0)), (C, K, N)
    else:
        K, N = a.shape[1], b.shape[1]
        tk = _pick(K, (512, 256, 128))
        tn = _pick(N, (512, 256, 128))
        grid = (K // tk, N // tn, S // ts)
        in_specs = [pl.BlockSpec((ts, tk), lambda i, j, s: (s, i)), pl.BlockSpec((ts, tn), lambda i, j, s: (s, j))]
        out_spec, out_shape = pl.BlockSpec((tk, tn), lambda i, j, s: (i, j)), (K, N)
    return pl.pallas_call(
        body, name=name, grid=grid, in_specs=in_specs, out_specs=out_spec, out_shape=_sds(out_shape, F32),
        compiler_params=pltpu.CompilerParams(dimension_semantics=("arbitrary",) * 3, vmem_limit_bytes=VMEM_LIMIT))(
            a, b)


def _my_id():
    return lax.axis_index("x") * 4 + lax.axis_index("y") * 2 + lax.axis_index("c")


def _peer(j):
    x, y, c = lax.axis_index("x"), lax.axis_index("y"), lax.axis_index("c")
    px = 1 - x if j & 4 else x
    py = 1 - y if j & 2 else y
    pc = 1 - c if j & 1 else c
    return (px, py, pc), px * 4 + py * 2 + pc


def all_gather(name, arrays):
    n = len(arrays)

    def body(*refs):
        ins, outs = refs[:n], refs[n:2 * n]
        send_sems, recv_sems, local_sems = refs[2 * n:]
        me = _my_id()
        local = [pltpu.make_async_copy(ins[k], outs[k].at[me], local_sems.at[k]) for k in range(n)]
        for cp in local:
            cp.start()
        sends = []
        for j in range(1, N_DEV):
            peer, _ = _peer(j)
            for k in range(n):
                cp = pltpu.make_async_remote_copy(
                    src_ref=ins[k], dst_ref=outs[k].at[me], send_sem=send_sems.at[k, j - 1],
                    recv_sem=recv_sems.at[k, j - 1], device_id=peer, device_id_type=pl.DeviceIdType.MESH)
                cp.start()
                sends.append(cp)
        for j in range(1, N_DEV):
            peer, pid = _peer(j)
            for k in range(n):
                pltpu.make_async_remote_copy(
                    src_ref=ins[k], dst_ref=outs[k].at[pid], send_sem=send_sems.at[k, j - 1],
                    recv_sem=recv_sems.at[k, j - 1], device_id=peer, device_id_type=pl.DeviceIdType.MESH).wait_recv()
        for cp in sends:
            cp.wait_send()
        for cp in local:
            cp.wait()

    any_spec = pl.BlockSpec(memory_space=pl.ANY)
    return pl.pallas_call(
        body, name=name, in_specs=[any_spec] * n, out_specs=[any_spec] * n,
        out_shape=[_sds((N_DEV,) + a.shape, a.dtype) for a in arrays],
        scratch_shapes=[pltpu.SemaphoreType.DMA((n, N_DEV - 1)), pltpu.SemaphoreType.DMA((n, N_DEV - 1)),
                        pltpu.SemaphoreType.DMA((n,))],
        compiler_params=pltpu.CompilerParams(has_side_effects=True))(*arrays)


def exchange(name, g):
    def body(g_ref, out_ref, send_sems, recv_sems, local_sem):
        me = _my_id()
        local = pltpu.make_async_copy(g_ref.at[me], out_ref.at[me], local_sem)
        local.start()
        sends = []
        for j in range(1, N_DEV):
            peer, pid = _peer(j)
            cp = pltpu.make_async_remote_copy(
                src_ref=g_ref.at[pid], dst_ref=out_ref.at[me], send_sem=send_sems.at[j - 1],
                recv_sem=recv_sems.at[j - 1], device_id=peer, device_id_type=pl.DeviceIdType.MESH)
            cp.start()
            sends.append(cp)
        for j in range(1, N_DEV):
            peer, pid = _peer(j)
            pltpu.make_async_remote_copy(
                src_ref=g_ref.at[me], dst_ref=out_ref.at[pid], send_sem=send_sems.at[j - 1],
                recv_sem=recv_sems.at[j - 1], device_id=peer, device_id_type=pl.DeviceIdType.MESH).wait_recv()
        for cp in sends:
            cp.wait_send()
        local.wait()

    any_spec = pl.BlockSpec(memory_space=pl.ANY)
    return pl.pallas_call(
        body, name=name, in_specs=[any_spec], out_specs=any_spec, out_shape=_sds(g.shape, g.dtype),
        scratch_shapes=[pltpu.SemaphoreType.DMA((N_DEV - 1,)), pltpu.SemaphoreType.DMA((N_DEV - 1,)),
                        pltpu.SemaphoreType.DMA],
        compiler_params=pltpu.CompilerParams(has_side_effects=True))(g)


def adamw(name, parts, w, m, v):
    R = w.shape[0]
    tr = _pick(R, (512, 256, 128, 64, 32, 16, 8))
    c1 = 1.0 - ADAM_B1 ** ADAM_STEP
    c2 = 1.0 - ADAM_B2 ** ADAM_STEP

    def body(p_ref, w_ref, m_ref, v_ref, g_ref, d_ref, nm_ref, nv_ref):
        g = p_ref[0]
        for s in range(1, N_DEV):
            g = g + p_ref[s]
        g_ref[...] = g
        m_ = ADAM_B1 * m_ref[...] + (1.0 - ADAM_B1) * g
        v_ = ADAM_B2 * v_ref[...] + (1.0 - ADAM_B2) * (g * g)
        nm_ref[...] = m_
        nv_ref[...] = v_
        d_ref[...] = -ADAM_LR * ((m_ / c1) / (jnp.sqrt(v_ / c2) + ADAM_EPS) + ADAM_WD * w_ref[...])

    row = pl.BlockSpec((tr, LANES), lambda i: (i, 0))
    return pl.pallas_call(
        body, name=name, grid=(R // tr,),
        in_specs=[pl.BlockSpec((N_DEV, tr, LANES), lambda i: (0, i, 0)), row, row, row], out_specs=[row] * 4,
        out_shape=[_sds((R, LANES), F32)] * 4, compiler_params=_cp())(parts, w, m, v)


WEIGHTS = ['ev_norm', 'ev_w_in', 'ev_pool_w', 'ev_pool_scale', 'ev_q_norm', 'ev_w_q_up', 'ev_kv_norm', 'ev_w_kv_up',
           'ev_w_out', 'od_norm', 'od_w_in', 'od_conv_w', 'od_conv_b', 'od_w_rgate', 'od_b_rgate', 'od_w_igate',
           'od_b_igate', 'od_lambda', 'od_w_out', 'xa_norm_x', 'xa_norm_mem', 'xa_w_q', 'xa_w_kv', 'xa_w_o',
           'ffn_norm', 'ffn_w_gate_up', 'ffn_w_down', 'final_norm']
SHARD_AXIS = {'ev_w_in': 1, 'ev_w_q_up': 2, 'ev_w_kv_up': 2, 'ev_w_out': 1, 'od_norm': 1, 'od_w_in': 2,
              'od_conv_w': 2, 'od_conv_b': 1, 'od_w_rgate': 2, 'od_b_rgate': 1, 'od_w_igate': 2, 'od_b_igate': 1,
              'od_lambda': 1, 'od_w_out': 1, 'xa_w_q': 1, 'xa_w_kv': 2, 'xa_w_o': 1, 'ffn_w_gate_up': 2,
              'ffn_w_down': 1}
SMALL_F32 = ('od_norm', 'od_conv_w', 'od_conv_b', 'od_b_rgate', 'od_b_igate', 'od_lambda')
STACKED = ('ffn_w_gate_up', 'ffn_w_down')
SHARDED = [n for n in WEIGHTS if n in SHARD_AXIS]
REPLICATED = [n for n in WEIGHTS if n not in SHARD_AXIS]
ROW_ALIGN = 512


def _pack(flats, dtype):
    v = jnp.concatenate([f.reshape(-1).astype(dtype) for f in flats])
    pad = (-v.shape[0]) % (ROW_ALIGN * LANES)
    return jnp.pad(v, (0, pad)).reshape(-1, LANES)


def _pack_lead(flats, dtype):
    v = jnp.concatenate([f.reshape(N_DEV, -1).astype(dtype) for f in flats], axis=1)
    pad = (-v.shape[1]) % (ROW_ALIGN * LANES)
    return jnp.pad(v, ((0, 0), (0, pad))).reshape(N_DEV, -1, LANES)


def _unpack(flat, shapes):
    out, off = [], 0
    v = flat.reshape(-1)
    for s in shapes:
        n = 1
        for d in s:
            n *= d
        out.append(v[off:off + n].reshape(s))
        off += n
    return out


def _unpack_lead(flat, shapes):
    out, off = [], 0
    v = flat.reshape(N_DEV, -1)
    for s in shapes:
        n = 1
        for d in s:
            n *= d
        out.append(v[:, off:off + n].reshape((N_DEV,) + tuple(s)))
        off += n
    return out


def _to_full(stacked, axis):
    v = jnp.moveaxis(stacked, 0, axis)
    s = v.shape
    return v.reshape(s[:axis] + (s[axis] * s[axis + 1],) + s[axis + 2:])


def _to_shards(full, axis):
    s = full.shape
    v = full.reshape(s[:axis] + (N_DEV, s[axis] // N_DEV) + s[axis + 1:])
    return jnp.moveaxis(v, axis, 0)


def _pad_heads(w, nh, dh, lead):
    s = w.shape
    v = w.reshape(s[:-1] + (nh, dh))
    v = jnp.pad(v, [(0, 0)] * (len(s) - 1) + [(0, 0), (lead, LANES - dh - lead)])
    return v.reshape(s[:-1] + (nh * LANES,))


def _unpad_heads(w, nh, dh, lead):
    s = w.shape
    return w.reshape(s[:-1] + (nh, LANES))[..., lead:lead + dh].reshape(s[:-1] + (nh * dh,))


def _rope_tables(positions):
    inv_freq = 10000.0 ** (-jnp.arange(0, 32, 2, dtype=F32) / 32)
    ang = positions.astype(F32)[:, None] * inv_freq
    cos, sin = jnp.cos(ang), jnp.sin(ang)
    S = positions.shape[0]
    one, zero = jnp.ones((S, 64), F32), jnp.zeros((S, 64), F32)
    z16, z32 = jnp.zeros((S, 16), F32), jnp.zeros((S, 32), F32)
    c = jnp.concatenate([one, cos, cos, jnp.ones((S, 32), F32)], axis=1)
    a = jnp.concatenate([zero, z16, sin, z32], axis=1)
    b = jnp.concatenate([zero, -sin, z16, z32], axis=1)
    return c, a, b


def _t(w):
    return jnp.swapaxes(w, -1, -2)


def _col_to_row(v, tq):
    S = v.shape[0]
    return v[:, ::LANES].T.reshape(MLA_HEADS, S // tq, 1, tq)


def device_step(x, mem, positions, target, W):
    S = x.shape[0]
    G = {}
    tabs = _rope_tables(positions)
    keep = (positions != 0).astype(F32)[:, None]
    row = lambda v: v.reshape(1, -1)

    w_in = W['ev_w_in'][0]
    ev_win = jnp.concatenate([w_in[:, :896], _pad_heads(w_in[:, 896:], 1, 32, 64)], axis=1)
    ev_wq = _pad_heads(W['ev_w_q_up'][0], MLA_HEADS, QK_DIM, 0)
    kvw = W['ev_w_kv_up'][0].reshape(KV_LORA, MLA_HEADS, 128)
    ev_wk = _pad_heads(kvw[:, :, :64].reshape(KV_LORA, 512), MLA_HEADS, 64, 0)
    ev_wv = _pad_heads(kvw[:, :, 64:].reshape(KV_LORA, 512), MLA_HEADS, 64, 0)
    ev_wo_pool = W['ev_w_out'][0][:POOL_DIM]
    ev_wo_att = _t(_pad_heads(_t(W['ev_w_out'][0][POOL_DIM:]), MLA_HEADS, 64, 0))
    pw = W['ev_pool_w'][0].astype(BF16)
    ev_g, ps, qg, kvg = row(W['ev_norm'][0]), row(W['ev_pool_scale'][0]), row(W['ev_q_norm'][0]), row(W['ev_kv_norm'][0])

    z0, qp, kp, vp, ypool = even_pre(x, tabs, ev_g, ev_win, pw, ps, qg, ev_wq, kvg, ev_wk, ev_wv)
    o_att, lse = attn_fwd(qp, kp, vp)
    x1 = even_post(x, ypool, o_att, ev_wo_pool, ev_wo_att)

    def xa_ffn_fwd(xin, l):
        mn, km, vm = mem_kv(mem, row(W['xa_norm_mem'][l]), W['xa_w_kv'][l])
        xm = xattn_fwd(xin, row(W['xa_norm_x'][l]), W['xa_w_q'][l], km, vm, W['xa_w_o'][l])
        xo = ffn_fwd(xm, row(W['ffn_norm'][l]), W['ffn_w_gate_up'][:, l],
                     W['ffn_w_down'][:, l].reshape(FF_HALF, FF_CHUNK, D))
        return xm, xo, (mn, km, vm)

    x2, x3, memkv0 = xa_ffn_fwd(x1, 0)

    od_g, lam = row(W['od_norm'][0]), row(W['od_lambda'][0])
    cw, cb = W['od_conv_w'][0], row(W['od_conv_b'][0])
    wr, wi = W['od_w_rgate'][0], W['od_w_igate'][0]
    br, bi = row(W['od_b_rgate'][0]), row(W['od_b_igate'][0])
    z1, a_t, b_t = odd_pre(x3, keep, od_g, W['od_w_in'][0], cw, cb, wr, br, wi, bi, lam)
    hseq = lru_scan(a_t, b_t)
    x4 = odd_post(x3, z1, hseq, W['od_w_out'][0])
    x5, x6, memkv1 = xa_ffn_fwd(x4, 1)

    dx, G['final_norm'], loss = loss_head(x6, target, row(W['final_norm']))
    G['final_norm'] = G['final_norm'].reshape(D)

    gnx, gnm, gwq, gwkv, gwo, gfn, gwgu, gwd = ([None, None] for _ in range(8))

    def xa_ffn_bwd(dy, xin, xm, memkv, l):
        mn, km, vm = memkv
        fg = row(W['ffn_norm'][l])
        wgu = W['ffn_w_gate_up'][:, l]
        hf, act, dgu = ffn_bwd_a(xm, dy, fg, wgu, _t(W['ffn_w_down'][:, l].reshape(FF_HALF, FF_CHUNK, D)))
        gwd[l] = matmul_tn("ffn_dwd", act, dy).reshape(N_DEV, D_FF // N_DEV, D)
        gwgu[l] = matmul_tn("ffn_dwgu", hf, dgu)
        dxm, dfg = ffn_bwd_b(xm, dy, dgu, fg, _t(wgu))
        gfn[l] = dfg[0]
        dxin, o, dq, hx, dgx, dk, dv = xattn_bwd(xin, dxm, row(W['xa_norm_x'][l]), W['xa_w_q'][l],
                                                  _t(W['xa_w_q'][l]), km, vm, _t(W['xa_w_o'][l]))
        gnx[l] = dgx[0]
        gwo[l] = matmul_tn("xa_dwo", o, dxm)
        gwq[l] = matmul_tn("xa_dwq", hx, dq)
        dkv, dgm = mem_bwd(mem, row(W['xa_norm_mem'][l]), dk, dv, _t(W['xa_w_kv'][l]))
        gnm[l] = dgm[0]
        gwkv[l] = matmul_tn("xa_dwkv", mn, dkv)
        return dxin

    dx4 = xa_ffn_bwd(dx, x4, x5, memkv1, 1)

    y_od, dgate, dhs = odd_post_bwd(dx4, z1, hseq, _t(W['od_w_out'][0]))
    G['od_w_out'] = matmul_tn("od_dwout", y_od, dx4)[None]
    lam_grad = lru_scan(a_t, dhs, reverse=True)
    dxb, dcb, dbr, dbi, dlam, dwr, dwi = odd_gates_bwd(z1, lam_grad, hseq, keep, cw, cb, wr, _t(wr), br, wi, _t(wi),
                                                        bi, lam)
    dx3, h_od, dz1, dcw, dg_od = odd_pre_bwd(x3, dx4, z1, dxb, dgate, od_g, cw, _t(W['od_w_in'][0]))
    G['od_w_in'] = matmul_tn("od_dwin", h_od, dz1)[None]
    G['od_norm'], G['od_conv_w'], G['od_conv_b'] = dg_od, dcw[None], dcb
    G['od_w_rgate'], G['od_b_rgate'], G['od_w_igate'], G['od_b_igate'], G['od_lambda'] = (
        dwr[None], dbr, dwi[None], dbi, dlam)

    dx1 = xa_ffn_bwd(dx3, x1, x2, memkv0, 0)

    dyp, do_att, delta = even_post_bwd(dx1, o_att, _t(ev_wo_pool), _t(ev_wo_att))
    g_wo_pool = matmul_tn("ev_dwo_pool", ypool, dx1)
    g_wo_att = matmul_tn("ev_dwo_att", o_att, dx1)
    G['ev_w_out'] = jnp.concatenate([g_wo_pool, _t(_unpad_heads(_t(g_wo_att), MLA_HEADS, 64, 0))], axis=0)[None]
    tq = _tile_rows(S, 512)
    dq, dk, dv = attn_bwd(qp, kp, vp, do_att, _col_to_row(lse, tq), _col_to_row(delta, tq))
    (grad_x, h_ev, dz0, dg_ev, dpw, dps, dqg, dwq, dkvg, dwk, dwv) = even_pre_bwd(
        x, dx1, z0, dq, dk, dv, dyp, tabs, ev_g, _t(ev_win), pw, _t(pw), ps, qg, _t(ev_wq), kvg, _t(ev_wk),
        _t(ev_wv))
    g_win = matmul_tn("ev_dwin", h_ev, dz0)
    G['ev_w_in'] = jnp.concatenate([g_win[:, :896], _unpad_heads(g_win[:, 896:], 1, 32, 64)], axis=1)[None]
    G['ev_norm'], G['ev_pool_w'], G['ev_pool_scale'], G['ev_q_norm'], G['ev_kv_norm'] = (
        dg_ev, dpw[None], dps, dqg, dkvg)
    G['ev_w_q_up'] = _unpad_heads(dwq, MLA_HEADS, QK_DIM, 0)[None]
    gk = _unpad_heads(dwk, MLA_HEADS, 64, 0).reshape(KV_LORA, MLA_HEADS, 64)
    gv = _unpad_heads(dwv, MLA_HEADS, 64, 0).reshape(KV_LORA, MLA_HEADS, 64)
    G['ev_w_kv_up'] = jnp.concatenate([gk, gv], axis=2).reshape(1, KV_LORA, MLA_HEADS * 128)

    G['xa_norm_x'], G['xa_norm_mem'], G['ffn_norm'] = jnp.stack(gnx), jnp.stack(gnm), jnp.stack(gfn)
    G['xa_w_q'], G['xa_w_kv'], G['xa_w_o'] = jnp.stack(gwq), jnp.stack(gwkv), jnp.stack(gwo)
    G['ffn_w_gate_up'], G['ffn_w_down'] = jnp.stack(gwgu, axis=1), jnp.stack(gwd, axis=1)
    return loss[0, 0], grad_x, G


def kernel(x, mem, positions, ev_norm, ev_w_in, ev_pool_w, ev_pool_scale, ev_q_norm, ev_w_q_up, ev_kv_norm, ev_w_kv_up, ev_w_out, od_norm, od_w_in, od_conv_w, od_conv_b, od_w_rgate, od_b_rgate, od_w_igate, od_b_igate, od_lambda, od_w_out, xa_norm_x, xa_norm_mem, xa_w_q, xa_w_kv, xa_w_o, ffn_norm, ffn_w_gate_up, ffn_w_down, final_norm, loss_target, m_ev_norm, m_ev_w_in, m_ev_pool_w, m_ev_pool_scale, m_ev_q_norm, m_ev_w_q_up, m_ev_kv_norm, m_ev_w_kv_up, m_ev_w_out, m_od_norm, m_od_w_in, m_od_conv_w, m_od_conv_b, m_od_w_rgate, m_od_b_rgate, m_od_w_igate, m_od_b_igate, m_od_lambda, m_od_w_out, m_xa_norm_x, m_xa_norm_mem, m_xa_w_q, m_xa_w_kv, m_xa_w_o, m_ffn_norm, m_ffn_w_gate_up, m_ffn_w_down, m_final_norm, v_ev_norm, v_ev_w_in, v_ev_pool_w, v_ev_pool_scale, v_ev_q_norm, v_ev_w_q_up, v_ev_kv_norm, v_ev_w_kv_up, v_ev_w_out, v_od_norm, v_od_w_in, v_od_conv_w, v_od_conv_b, v_od_w_rgate, v_od_b_rgate, v_od_w_igate, v_od_b_igate, v_od_lambda, v_od_w_out, v_xa_norm_x, v_xa_norm_mem, v_xa_w_q, v_xa_w_kv, v_xa_w_o, v_ffn_norm, v_ffn_w_gate_up, v_ffn_w_down, v_final_norm):
    args = dict(locals())
    w = {n: args[n] for n in WEIGHTS}
    m = {n: args['m_' + n] for n in WEIGHTS}
    v = {n: args['v_' + n] for n in WEIGHTS}
    big = [n for n in SHARDED if n not in SMALL_F32]
    small = [n for n in SHARDED if n in SMALL_F32]

    g_big, g_small = all_gather("gather_weights", [_pack([w[n] for n in big], BF16), _pack([w[n] for n in small], F32)])
    W = {n: w[n] for n in REPLICATED}
    for names, buf in ((big, g_big), (small, g_small)):
        for n, st in zip(names, _unpack_lead(buf, [w[n].shape for n in names])):
            W[n] = st if n in STACKED else _to_full(st, SHARD_AXIS[n])

    loss, grad_x, G = device_step(x[0], mem[0], positions[0], loss_target[0], W)
    loss = lax.psum(loss, ("x", "y", "c"))

    rep_shapes = [w[n].shape for n in REPLICATED]
    rep_parts, = all_gather("gather_rep_grads", [_pack([G[n] for n in REPLICATED], F32)])
    rep = adamw("adamw_rep", rep_parts, *[_pack([d[n] for n in REPLICATED], F32) for d in (w, m, v)])
    rep = [dict(zip(REPLICATED, _unpack(r, rep_shapes))) for r in rep]

    sh_shapes = [w[n].shape for n in SHARDED]
    parts = exchange("exchange_grads", _pack_lead(
        [G[n] if n in STACKED else _to_shards(G[n], SHARD_AXIS[n]) for n in SHARDED], F32))
    sh = adamw("adamw_shard", parts, *[_pack([d[n] for n in SHARDED], F32) for d in (w, m, v)])
    sh = [dict(zip(SHARDED, _unpack(r, sh_shapes))) for r in sh]

    outs = [{**rep[k], **sh[k]} for k in range(4)]
    return (loss, grad_x[None], *[outs[0][n] for n in WEIGHTS], *[outs[1][n] for n in WEIGHTS],
            *[outs[2][n] for n in WEIGHTS], *[outs[3][n] for n in WEIGHTS])
```

```python
import functools

import jax
import jax.numpy as jnp
from jax import lax
from jax.experimental import pallas as pl
from jax.experimental.pallas import tpu as pltpu

F32, BF16 = jnp.float32, jnp.bfloat16
N_DEV = 8
D = 1024
POOL_DIM = 512
POOL_WINDOWS = (2, 4, 8, 16)
MLA_HEADS = 8
QK_DIM = 96
Q_LORA, KV_LORA = 256, 128
LRU_HEADS, LRU_HEAD_DIM = 4, 256
LRU_C = 8.0
MEM_HEADS, MEM_HEAD_DIM = 4, 256
D_FF = 2816
RMS_EPS = 1e-6
ADAM_LR, ADAM_B1, ADAM_B2, ADAM_EPS, ADAM_WD, ADAM_STEP = 0.001, 0.9, 0.999, 1e-08, 0.01, 10
LANES = 128
POOL_HALO = 16
CONV_HALO = 8
VMEM_LIMIT = 60000 * 1024


def _cp():
    return pltpu.CompilerParams(dimension_semantics=("arbitrary",), vmem_limit_bytes=VMEM_LIMIT)


def _cp2():
    return pltpu.CompilerParams(dimension_semantics=("arbitrary", "arbitrary"), vmem_limit_bytes=VMEM_LIMIT)


def _row(ts, c, col=0):
    return pl.BlockSpec((ts, c), lambda i: (i, col))


def _prev(hr, c, ts, col=0):
    r = ts // hr
    return pl.BlockSpec((hr, c), lambda i: (jnp.maximum(i * r - 1, 0), col))


def _next(hr, c, ts, n, col=0):
    r = ts // hr
    return pl.BlockSpec((hr, c), lambda i: (jnp.minimum((i + 1) * r, n * r - 1), col))


def _const(shape):
    nd = len(shape)
    return pl.BlockSpec(tuple(shape), lambda i: (0,) * nd, pipeline_mode=pl.Buffered(1))


def _acc(shape):
    nd = len(shape)
    return pl.BlockSpec(tuple(shape), lambda i: (0,) * nd)


def _sds(shape, dt):
    return jax.ShapeDtypeStruct(tuple(shape), dt)


def _dot(a, b):
    return jnp.dot(a.astype(BF16), b.astype(BF16), preferred_element_type=F32)


def _dot_nt(a, b):
    return lax.dot_general(a.astype(BF16), b.astype(BF16), (((1,), (1,)), ((), ())), preferred_element_type=F32)


def _dot_tn(a, b):
    return lax.dot_general(a.astype(BF16), b.astype(BF16), (((0,), (0,)), ((), ())), preferred_element_type=F32)


def _rms(x, g):
    rstd = lax.rsqrt(jnp.mean(x * x, axis=-1, keepdims=True) + RMS_EPS)
    return x * rstd * g, rstd


def _rms_bwd(x, g, rstd, dy):
    xn = x * rstd
    dyg = dy * g
    dx = rstd * (dyg - xn * jnp.mean(dyg * xn, axis=-1, keepdims=True))
    return dx, dy * xn


def _rowsum(v):
    return jnp.sum(v, axis=0, keepdims=True)


def _roll(v, s, axis):
    n = v.shape[axis]
    return pltpu.roll(v, s % n, axis)


def _rope(t, c, a, b):
    k = t.shape[1] // LANES
    if k > 1:
        c, a, b = (jnp.tile(v, (1, k)) for v in (c, a, b))
    return t * c + _roll(t, 16, 1) * a + _roll(t, -16, 1) * b


def _rope_bwd(d, c, a, b):
    k = d.shape[1] // LANES
    if k > 1:
        c, a, b = (jnp.tile(v, (1, k)) for v in (c, a, b))
    return d * c + _roll(d * a, -16, 1) + _roll(d * b, 16, 1)


def _gelu(x):
    c = 0.7978845608028654
    t = jnp.tanh(c * (x + 0.044715 * x * x * x))
    return 0.5 * x * (1.0 + t), t


def _gelu_grad(x, t):
    c = 0.7978845608028654
    return 0.5 * (1.0 + t) + 0.5 * x * (1.0 - t * t) * c * (1.0 + 3.0 * 0.044715 * x * x)


def _blockdot(v, w_ref, nblk, width):
    return jnp.concatenate(
        [_dot(v[:, j * width:(j + 1) * width], w_ref[j]) for j in range(nblk)], axis=1)


def _pool_cnt(row0, rows):
    t = row0 + lax.broadcasted_iota(jnp.int32, (rows, POOL_DIM), 0)
    w = jnp.left_shift(2, lax.broadcasted_iota(jnp.int32, (rows, POOL_DIM), 1) // LANES)
    return jnp.minimum(t + 1, w).astype(F32)


def _pool_windows(ext, sign):
    s2 = ext + _roll(ext, sign * 1, 0)
    t = s2[:, LANES:]
    s4 = t + _roll(t, sign * 2, 0)
    t = s4[:, LANES:]
    s8 = t + _roll(t, sign * 4, 0)
    t = s8[:, LANES:]
    s16 = t + _roll(t, sign * 8, 0)
    return jnp.concatenate([s2[:, :LANES], s4[:, :LANES], s8[:, :LANES], s16], axis=1)


def _pooled(uprev, u, row0):
    ts = u.shape[0]
    ext = jnp.concatenate([uprev, u], axis=0)
    sums = _pool_windows(ext, 1)[POOL_HALO:]
    return sums / _pool_cnt(row0, ts) - u


def _expm1(x):
    return jnp.where(jnp.abs(x) < 0.01, x * (1.0 + 0.5 * x * (1.0 + x * (1.0 / 3.0))), jnp.exp(x) - 1.0)


def _softplus(z):
    return jnp.maximum(z, 0.0) + jnp.log1p(jnp.exp(-jnp.abs(z)))


def _tile_rows(s, want):
    while s % want:
        want //= 2
    return want


def even_pre(x, tabs, g, win, pw, pscale, qg, wq, kvg, wk, wv):
    S = x.shape[0]
    ts = _tile_rows(S, 512)

    def body(x_ref, xp_ref, c_ref, a_ref, b_ref, g_ref, win_ref, pw_ref, ps_ref, qg_ref, wq_ref, kvg_ref,
             wk_ref, wv_ref, z_ref, q_ref, k_ref, v_ref, yp_ref):
        i = pl.program_id(0)
        h, _ = _rms(x_ref[...], g_ref[...])
        z = _dot(h, win_ref[...])
        z_ref[...] = z
        hp, _ = _rms(xp_ref[...], g_ref[...])
        uprev = _dot(hp, win_ref[:, :POOL_DIM]) * (i > 0).astype(F32)
        u = z[:, :POOL_DIM]
        pooled = _pooled(uprev, u, i * ts)
        yp_ref[...] = (_blockdot(pooled, pw_ref, 4, LANES) * ps_ref[...]).astype(BF16)
        c, a, b = c_ref[...], a_ref[...], b_ref[...]
        cqn, _ = _rms(z[:, 512:768], qg_ref[...])
        q_ref[...] = (_rope(_dot(cqn, wq_ref[...]), c, a, b) * ATTN_SCALE).astype(BF16)
        ckvn, _ = _rms(z[:, 768:896], kvg_ref[...])
        krr = _rope(z[:, 896:1024], c, a, b)
        k_ref[...] = (_dot(ckvn, wk_ref[...]) + jnp.tile(krr, (1, MLA_HEADS))).astype(BF16)
        v_ref[...] = _dot(ckvn, wv_ref[...]).astype(BF16)

    ins = [x, x, *tabs, g, win, pw, pscale, qg, wq, kvg, wk, wv]
    in_specs = [_row(ts, D), _prev(POOL_HALO, D, ts), _row(ts, LANES), _row(ts, LANES), _row(ts, LANES)]
    in_specs += [_const(v.shape) for v in ins[5:]]
    return pl.pallas_call(
        body, name="even_pre", grid=(S // ts,), in_specs=in_specs,
        out_specs=[_row(ts, D)] * 4 + [_row(ts, POOL_DIM)],
        out_shape=[_sds((S, D), F32)] + [_sds((S, D), BF16)] * 3 + [_sds((S, POOL_DIM), BF16)],
        compiler_params=_cp())(*ins)


ATTN_SCALE = QK_DIM ** -0.5


def _pair_loop(lo, hi, step, init):
    pairs = (hi - lo) // 2
    carry = lax.fori_loop(0, pairs, lambda j, c: step(lo + 2 * j + 1, step(lo + 2 * j, c)), init)
    return lax.fori_loop(lo + 2 * pairs, hi, step, carry)


def attn_fwd(qp, kp, vp):
    S = qp.shape[0]
    tq = _tile_rows(S, 512)

    def body(q_ref, k_ref, v_ref, o_ref, lse_ref):
        qi = pl.program_id(1)
        q = q_ref[...]

        def block(ki, carry, masked):
            m, l, acc = carry
            off = pl.multiple_of(ki * tq, tq)
            s = _dot_nt(q, k_ref[pl.ds(off, tq), :])
            if masked:
                row = lax.broadcasted_iota(jnp.int32, (tq, tq), 0)
                col = lax.broadcasted_iota(jnp.int32, (tq, tq), 1)
                s = jnp.where(col <= row, s, -1e30)
            m_new = jnp.maximum(m, jnp.max(s, axis=1, keepdims=True))
            p = jnp.exp(s - m_new)
            alpha = jnp.exp(m - m_new)
            l = alpha * l + jnp.sum(p, axis=1, keepdims=True)
            acc = alpha * acc + _dot(p, v_ref[pl.ds(off, tq), :])
            return m_new, l, acc

        init = (jnp.full((tq, 1), -1e30, F32), jnp.zeros((tq, 1), F32), jnp.zeros((tq, LANES), F32))
        carry = _pair_loop(0, qi, lambda ki, c: block(ki, c, False), init)
        m, l, acc = block(qi, carry, True)
        o_ref[...] = acc / l
        lse_ref[...] = jnp.broadcast_to(m + jnp.log(l), (tq, LANES))

    blk = pl.BlockSpec((tq, LANES), lambda h, i: (i, h))
    full = pl.BlockSpec((S, LANES), lambda h, i: (0, h))
    return pl.pallas_call(
        body, name="attn_fwd", grid=(MLA_HEADS, S // tq), in_specs=[blk, full, full], out_specs=[blk, blk],
        out_shape=[_sds((S, D), F32), _sds((S, D), F32)], compiler_params=_cp2())(qp, kp, vp)


def even_post(x, ypool, o, wo_pool, wo_att):
    S = x.shape[0]
    ts = _tile_rows(S, 512)

    def body(x_ref, yp_ref, o_ref, wp_ref, wa_ref, out_ref):
        out_ref[...] = x_ref[...] + _dot(yp_ref[...], wp_ref[...]) + _dot(o_ref[...], wa_ref[...])

    return pl.pallas_call(
        body, name="even_post", grid=(S // ts,),
        in_specs=[_row(ts, D), _row(ts, POOL_DIM), _row(ts, D), _const(wo_pool.shape), _const(wo_att.shape)],
        out_specs=_row(ts, D), out_shape=_sds((S, D), F32), compiler_params=_cp())(x, ypool, o, wo_pool, wo_att)


def mem_kv(mem, g, wkv):
    M = mem.shape[0]

    def body(mem_ref, g_ref, w_ref, mn_ref, k_ref, v_ref):
        mn, _ = _rms(mem_ref[...], g_ref[...])
        mn_ref[...] = mn.astype(BF16)
        k_ref[...] = _dot(mn, w_ref[:, :D]).astype(BF16)
        v_ref[...] = _dot(mn, w_ref[:, D:]).astype(BF16)

    return pl.pallas_call(
        body, name="mem_kv", grid=(1,), in_specs=[_acc(mem.shape), _acc(g.shape), _acc(wkv.shape)],
        out_specs=[_acc((M, D))] * 3, out_shape=[_sds((M, D), BF16)] * 3, compiler_params=_cp())(mem, g, wkv)


def _xattn_heads(hx, wq_ref, k_ref, v_ref):
    q = _dot(hx, wq_ref[...])
    scale = MEM_HEAD_DIM ** -0.5
    ps, os_ = [], []
    for h in range(MEM_HEADS):
        sl = slice(h * MEM_HEAD_DIM, (h + 1) * MEM_HEAD_DIM)
        s = _dot_nt(q[:, sl], k_ref[:, sl]) * scale
        e = jnp.exp(s - jnp.max(s, axis=1, keepdims=True))
        p = e / jnp.sum(e, axis=1, keepdims=True)
        ps.append(p)
        os_.append(_dot(p, v_ref[:, sl]))
    return q, ps, jnp.concatenate(os_, axis=1)


def xattn_fwd(x, g, wq, kmem, vmem, wo):
    S = x.shape[0]
    ts = _tile_rows(S, 512)

    def body(x_ref, g_ref, wq_ref, k_ref, v_ref, wo_ref, out_ref):
        x_ = x_ref[...]
        hx, _ = _rms(x_, g_ref[...])
        _, _, o = _xattn_heads(hx, wq_ref, k_ref, v_ref)
        out_ref[...] = x_ + _dot(o, wo_ref[...])

    ins = [x, g, wq, kmem, vmem, wo]
    return pl.pallas_call(
        body, name="xattn_fwd", grid=(S // ts,), in_specs=[_row(ts, D)] + [_const(v.shape) for v in ins[1:]],
        out_specs=_row(ts, D), out_shape=_sds((S, D), F32), compiler_params=_cp())(*ins)


def xattn_bwd(x, dy, g, wq, wqT, kmem, vmem, woT):
    S = x.shape[0]
    M = kmem.shape[0]
    ts = _tile_rows(S, 512)
    scale = MEM_HEAD_DIM ** -0.5

    def body(x_ref, dy_ref, g_ref, wq_ref, wqT_ref, k_ref, v_ref, woT_ref,
             dx_ref, o_ref, dq_ref, hx_ref, dg_ref, dk_ref, dv_ref):
        i = pl.program_id(0)

        @pl.when(i == 0)
        def _():
            dg_ref[...] = jnp.zeros_like(dg_ref)
            dk_ref[...] = jnp.zeros_like(dk_ref)
            dv_ref[...] = jnp.zeros_like(dv_ref)

        x_, dy_ = x_ref[...], dy_ref[...]
        hx, rstd = _rms(x_, g_ref[...])
        q, ps, o = _xattn_heads(hx, wq_ref, k_ref, v_ref)
        hx_ref[...] = hx.astype(BF16)
        o_ref[...] = o.astype(BF16)
        do = _dot(dy_, woT_ref[...])
        dqs = []
        for h in range(MEM_HEADS):
            sl = slice(h * MEM_HEAD_DIM, (h + 1) * MEM_HEAD_DIM)
            p, do_h = ps[h], do[:, sl]
            dp = _dot_nt(do_h, v_ref[:, sl])
            ds = p * (dp - jnp.sum(p * dp, axis=1, keepdims=True)) * scale
            dqs.append(_dot(ds, k_ref[:, sl]))
            dk_ref[:, sl] += _dot_tn(ds, q[:, sl])
            dv_ref[:, sl] += _dot_tn(p, do_h)
        dq = jnp.concatenate(dqs, axis=1)
        dq_ref[...] = dq.astype(BF16)
        dxn, dgr = _rms_bwd(x_, g_ref[...], rstd, _dot(dq, wqT_ref[...]))
        dx_ref[...] = dy_ + dxn
        dg_ref[...] += _rowsum(dgr)

    ins = [x, dy, g, wq, wqT, kmem, vmem, woT]
    return pl.pallas_call(
        body, name="xattn_bwd", grid=(S // ts,),
        in_specs=[_row(ts, D), _row(ts, D)] + [_const(v.shape) for v in ins[2:]],
        out_specs=[_row(ts, D)] * 4 + [_acc((1, D)), _acc((M, D)), _acc((M, D))],
        out_shape=[_sds((S, D), F32)] + [_sds((S, D), BF16)] * 3 + [_sds((1, D), F32), _sds((M, D), F32),
                                                                    _sds((M, D), F32)],
        compiler_params=_cp())(*ins)


def mem_bwd(mem, g, dk, dv, wkvT):
    M = mem.shape[0]

    def body(mem_ref, g_ref, dk_ref, dv_ref, w_ref, dkv_ref, dg_ref):
        dkv = jnp.concatenate([dk_ref[...], dv_ref[...]], axis=1)
        dkv_ref[...] = dkv.astype(BF16)
        _, rstd = _rms(mem_ref[...], g_ref[...])
        dg_ref[...] = _rowsum(_dot(dkv, w_ref[...]) * (mem_ref[...] * rstd))

    ins = [mem, g, dk, dv, wkvT]
    return pl.pallas_call(
        body, name="mem_bwd", grid=(1,), in_specs=[_acc(v.shape) for v in ins],
        out_specs=[_acc((M, 2 * D)), _acc((1, D))], out_shape=[_sds((M, 2 * D), BF16), _sds((1, D), F32)],
        compiler_params=_cp())(*ins)


FF_CHUNK = 2 * D_FF // N_DEV
FF_HALF = N_DEV // 2


def ffn_fwd(x, g, wgu, wd):
    S = x.shape[0]
    ts = _tile_rows(S, 256)

    def body(x_ref, g_ref, wgu_ref, wd_ref, out_ref):
        x_ = x_ref[...]
        hf = _rms(x_, g_ref[...])[0].astype(BF16)
        out = x_
        for j in range(FF_HALF):
            gg = _dot(hf, wgu_ref[j])
            out = out + _dot(gg * jax.nn.sigmoid(gg) * _dot(hf, wgu_ref[j + FF_HALF]), wd_ref[j])
        out_ref[...] = out

    ins = [x, g, wgu, wd]
    return pl.pallas_call(
        body, name="ffn_fwd", grid=(S // ts,), in_specs=[_row(ts, D)] + [_const(v.shape) for v in ins[1:]],
        out_specs=_row(ts, D), out_shape=_sds((S, D), F32), compiler_params=_cp())(*ins)


def ffn_bwd_a(x, dy, g, wgu, wdT):
    S = x.shape[0]
    ts = _tile_rows(S, 256)

    def body(x_ref, dy_ref, g_ref, wgu_ref, wdT_ref, hf_ref, act_ref, dgu_ref):
        hf = _rms(x_ref[...], g_ref[...])[0].astype(BF16)
        hf_ref[...] = hf
        dy_ = dy_ref[...].astype(BF16)
        for j in range(FF_HALF):
            gg, uu = _dot(hf, wgu_ref[j]), _dot(hf, wgu_ref[j + FF_HALF])
            sg = jax.nn.sigmoid(gg)
            silu = gg * sg
            act_ref[j] = (silu * uu).astype(BF16)
            dact = _dot(dy_, wdT_ref[j])
            dgu_ref[j] = (dact * uu * (sg * (1.0 + gg * (1.0 - sg)))).astype(BF16)
            dgu_ref[j + FF_HALF] = (dact * silu).astype(BF16)

    ins = [x, dy, g, wgu, wdT]
    chunked = lambda c: pl.BlockSpec((c, ts, FF_CHUNK), lambda i: (0, i, 0))
    return pl.pallas_call(
        body, name="ffn_bwd_a", grid=(S // ts,),
        in_specs=[_row(ts, D), _row(ts, D)] + [_const(v.shape) for v in ins[2:]],
        out_specs=[_row(ts, D), chunked(FF_HALF), chunked(N_DEV)],
        out_shape=[_sds((S, D), BF16), _sds((FF_HALF, S, FF_CHUNK), BF16), _sds((N_DEV, S, FF_CHUNK), BF16)],
        compiler_params=_cp())(*ins)


def ffn_bwd_b(x, dy, dgu, g, wguT):
    S = x.shape[0]
    ts = _tile_rows(S, 512)

    def body(x_ref, dy_ref, dgu_ref, g_ref, wT_ref, dx_ref, dg_ref):
        @pl.when(pl.program_id(0) == 0)
        def _():
            dg_ref[...] = jnp.zeros_like(dg_ref)

        dh = _dot(dgu_ref[0], wT_ref[0])
        for j in range(1, N_DEV):
            dh = dh + _dot(dgu_ref[j], wT_ref[j])
        x_ = x_ref[...]
        _, rstd = _rms(x_, g_ref[...])
        dxn, dgr = _rms_bwd(x_, g_ref[...], rstd, dh)
        dx_ref[...] = dy_ref[...] + dxn
        dg_ref[...] += _rowsum(dgr)

    return pl.pallas_call(
        body, name="ffn_bwd_b", grid=(S // ts,),
        in_specs=[_row(ts, D), _row(ts, D), pl.BlockSpec((N_DEV, ts, FF_CHUNK), lambda i: (0, i, 0)),
                  _const(g.shape), _const(wguT.shape)],
        out_specs=[_row(ts, D), _acc((1, D))], out_shape=[_sds((S, D), F32), _sds((1, D), F32)],
        compiler_params=_cp())(x, dy, dgu, g, wguT)


def _conv_fwd(xprev, xbp, cw_ref, cb):
    ext = jnp.concatenate([xprev, xbp], axis=0)
    acc = cb + cw_ref[3:4, :] * xbp
    for k in range(3):
        acc = acc + cw_ref[k:k + 1, :] * _roll(ext, 3 - k, 0)[CONV_HALO:]
    return acc


def _gates(xb, keep, wr_ref, br, wi_ref, bi, lam):
    r = jax.nn.sigmoid(_blockdot(xb, wr_ref, LRU_HEADS, LRU_HEAD_DIM) + br)
    ig = jax.nn.sigmoid(_blockdot(xb, wi_ref, LRU_HEADS, LRU_HEAD_DIM) + bi)
    sp = _softplus(-lam)
    log_a = -LRU_C * r * sp
    a = jnp.exp(log_a)
    mult = jnp.sqrt(jnp.maximum(-_expm1(2.0 * log_a), 0.0))
    return r, ig, sp, a, mult


def odd_pre(x, keep, g, win, cw, cb, wr, br, wi, bi, lam):
    S = x.shape[0]
    ts = _tile_rows(S, 512)

    def body(x_ref, xp_ref, keep_ref, g_ref, win_ref, cw_ref, cb_ref, wr_ref, br_ref, wi_ref, bi_ref, lam_ref,
             z_ref, a_ref, b_ref):
        i = pl.program_id(0)
        h, _ = _rms(x_ref[...], g_ref[...])
        z = _dot(h, win_ref[...])
        z_ref[...] = z
        hp, _ = _rms(xp_ref[...], g_ref[...])
        xprev = _dot(hp, win_ref[:, D:]) * (i > 0).astype(F32)
        xb = _conv_fwd(xprev, z[:, D:], cw_ref, cb_ref[...])
        keep_ = keep_ref[...]
        _, ig, _, a, mult = _gates(xb, keep_, wr_ref, br_ref[...], wi_ref, bi_ref[...], lam_ref[...])
        a_ref[...] = a * keep_
        b_ref[...] = jnp.where(keep_ > 0.0, mult, 1.0) * (ig * xb)

    ins = [x, x, keep, g, win, cw, cb, wr, br, wi, bi, lam]
    return pl.pallas_call(
        body, name="odd_pre", grid=(S // ts,),
        in_specs=[_row(ts, D), _prev(CONV_HALO, D, ts), _row(ts, 1)] + [_const(v.shape) for v in ins[3:]],
        out_specs=[_row(ts, 2 * D), _row(ts, D), _row(ts, D)],
        out_shape=[_sds((S, 2 * D), F32), _sds((S, D), F32), _sds((S, D), F32)], compiler_params=_cp())(*ins)


def lru_scan(a, b, reverse=False):
    S = a.shape[0]
    ts = _tile_rows(S, 512)
    n = S // ts
    groups = ts // 8

    def body(a_ref, an_ref, b_ref, h_ref, carry_ref, ash_ref):
        i = pl.program_id(0)

        @pl.when(i == 0)
        def _():
            carry_ref[...] = jnp.zeros_like(carry_ref)

        rid = lax.broadcasted_iota(jnp.int32, (8, D), 0)
        if reverse:
            ext = jnp.concatenate([a_ref[...], an_ref[...] * (i > 0).astype(F32)], axis=0)
            ash_ref[...] = _roll(ext, -1, 0)[:ts]
        src = ash_ref if reverse else a_ref

        def group(j, carry):
            off = pl.multiple_of((groups - 1 - j if reverse else j) * 8, 8)
            a8, b8 = src[pl.ds(off, 8), :], b_ref[pl.ds(off, 8), :]
            for k in (1, 2, 4):
                inside = (rid < 8 - k) if reverse else (rid >= k)
                sh = -k if reverse else k
                a_sh = jnp.where(inside, _roll(a8, sh, 0), 1.0)
                b_sh = jnp.where(inside, _roll(b8, sh, 0), 0.0)
                b8 = a8 * b_sh + b8
                a8 = a8 * a_sh
            h8 = a8 * carry + b8
            h_ref[pl.ds(off, 8), :] = h8
            return h8[0:1, :] if reverse else h8[7:8, :]

        carry_ref[...] = lax.fori_loop(0, groups, group, carry_ref[...])

    if reverse:
        r = ts // 8
        tile = pl.BlockSpec((ts, D), lambda i: (n - 1 - i, 0))
        halo = pl.BlockSpec((8, D), lambda i: (jnp.minimum((n - i) * r, n * r - 1), 0))
    else:
        tile, halo = _row(ts, D), _prev(8, D, ts)
    return pl.pallas_call(
        body, name="lru_scan_rev" if reverse else "lru_scan", grid=(n,), in_specs=[tile, halo, tile],
        out_specs=tile, out_shape=_sds((S, D), F32),
        scratch_shapes=[pltpu.VMEM((1, D), F32), pltpu.VMEM((ts, D), F32)], compiler_params=_cp())(a, a, b)


def odd_post(x, z, hseq, wout):
    S = x.shape[0]
    ts = _tile_rows(S, 512)

    def body(x_ref, gate_ref, h_ref, w_ref, out_ref):
        gl, _ = _gelu(gate_ref[...])
        out_ref[...] = x_ref[...] + _dot(gl * h_ref[...], w_ref[...])

    return pl.pallas_call(
        body, name="odd_post", grid=(S // ts,),
        in_specs=[_row(ts, D), _row(ts, D), _row(ts, D), _const(wout.shape)],
        out_specs=_row(ts, D), out_shape=_sds((S, D), F32), compiler_params=_cp())(x, z, hseq, wout)


def odd_post_bwd(dy, z, hseq, woutT):
    S = dy.shape[0]
    ts = _tile_rows(S, 512)

    def body(dy_ref, gate_ref, h_ref, w_ref, y_ref, dgate_ref, dh_ref):
        gate, hs = gate_ref[...], h_ref[...]
        gl, t = _gelu(gate)
        y_ref[...] = (gl * hs).astype(BF16)
        dyy = _dot(dy_ref[...], w_ref[...])
        dgate_ref[...] = dyy * hs * _gelu_grad(gate, t)
        dh_ref[...] = dyy * gl

    return pl.pallas_call(
        body, name="odd_post_bwd", grid=(S // ts,),
        in_specs=[_row(ts, D), _row(ts, D), _row(ts, D), _const(woutT.shape)],
        out_specs=[_row(ts, D)] * 3, out_shape=[_sds((S, D), BF16), _sds((S, D), F32), _sds((S, D), F32)],
        compiler_params=_cp())(dy, z, hseq, woutT)


def odd_gates_bwd(z, lam_grad, hseq, keep, cw, cb, wr, wrT, br, wi, wiT, bi, lam):
    S = z.shape[0]
    ts = _tile_rows(S, 512)

    def body(xbp_ref, xbpp_ref, lg_ref, h_ref, hp_ref, keep_ref, cw_ref, cb_ref, wr_ref, wrT_ref, br_ref, wi_ref,
             wiT_ref, bi_ref, lam_ref, dxb_ref, dcb_ref, dbr_ref, dbi_ref, dlam_ref, dwr_ref, dwi_ref):
        i = pl.program_id(0)

        @pl.when(i == 0)
        def _():
            for ref in (dcb_ref, dbr_ref, dbi_ref, dlam_ref, dwr_ref, dwi_ref):
                ref[...] = jnp.zeros_like(ref)

        first = (i > 0).astype(F32)
        xb = _conv_fwd(xbpp_ref[...] * first, xbp_ref[...], cw_ref, cb_ref[...])
        keep_ = keep_ref[...]
        lam_ = lam_ref[...]
        r, ig, sp, a, mult = _gates(xb, keep_, wr_ref, br_ref[...], wi_ref, bi_ref[...], lam_)
        hs = h_ref[...]
        hprev = _roll(jnp.concatenate([hp_ref[...] * first, hs], axis=0), 1, 0)[CONV_HALO:]
        lg = lg_ref[...]
        da = lg * hprev * keep_
        ixb = ig * xb
        dmult = lg * ixb * keep_
        dixb = lg * jnp.where(keep_ > 0.0, mult, 1.0)
        dlog_a = da * a - dmult * jnp.where(mult > 0.0, a * a / mult, 0.0)
        dr = dlog_a * (-LRU_C * sp)
        dlam_ref[...] += _rowsum(dlog_a * (-LRU_C * r)) * (-jax.nn.sigmoid(-lam_))
        dpr = dr * r * (1.0 - r)
        dpi = dixb * xb * ig * (1.0 - ig)
        dbr_ref[...] += _rowsum(dpr)
        dbi_ref[...] += _rowsum(dpi)
        dxb = dixb * ig
        parts = []
        for h in range(LRU_HEADS):
            sl = slice(h * LRU_HEAD_DIM, (h + 1) * LRU_HEAD_DIM)
            dwr_ref[h] += _dot_tn(xb[:, sl], dpr[:, sl])
            dwi_ref[h] += _dot_tn(xb[:, sl], dpi[:, sl])
            parts.append(_dot(dpr[:, sl], wrT_ref[h]) + _dot(dpi[:, sl], wiT_ref[h]))
        dxb = dxb + jnp.concatenate(parts, axis=1)
        dxb_ref[...] = dxb
        dcb_ref[...] += _rowsum(dxb)

    ins = [z, z, lam_grad, hseq, hseq, keep, cw, cb, wr, wrT, br, wi, wiT, bi, lam]
    in_specs = [_row(ts, D, 1), _prev(CONV_HALO, D, ts, 1), _row(ts, D), _row(ts, D), _prev(CONV_HALO, D, ts),
                _row(ts, 1)] + [_const(v.shape) for v in ins[6:]]
    gshape = (LRU_HEADS, LRU_HEAD_DIM, LRU_HEAD_DIM)
    return pl.pallas_call(
        body, name="odd_gates_bwd", grid=(S // ts,), in_specs=in_specs,
        out_specs=[_row(ts, D)] + [_acc((1, D))] * 4 + [_acc(gshape)] * 2,
        out_shape=[_sds((S, D), F32)] + [_sds((1, D), F32)] * 4 + [_sds(gshape, F32)] * 2,
        compiler_params=_cp())(*ins)


def odd_pre_bwd(x, dy, z, dxb, dgate, g, cw, winT):
    S = x.shape[0]
    ts = _tile_rows(S, 512)
    n = S // ts

    def body(x_ref, dy_ref, xbp_ref, xbpp_ref, dxb_ref, dxbn_ref, dgate_ref, g_ref, cw_ref, winT_ref,
             dx_ref, h_ref, dz_ref, dcw_ref, dg_ref):
        i = pl.program_id(0)

        @pl.when(i == 0)
        def _():
            dcw_ref[...] = jnp.zeros_like(dcw_ref)
            dg_ref[...] = jnp.zeros_like(dg_ref)

        dxb = dxb_ref[...]
        extd = jnp.concatenate([dxb, dxbn_ref[...] * (i < n - 1).astype(F32)], axis=0)
        extx = jnp.concatenate([xbpp_ref[...] * (i > 0).astype(F32), xbp_ref[...]], axis=0)
        dxbp = cw_ref[3:4, :] * dxb
        dcw_ref[3:4, :] += _rowsum(dxb * xbp_ref[...])
        for k in range(3):
            dxbp = dxbp + cw_ref[k:k + 1, :] * _roll(extd, -(3 - k), 0)[:ts]
            dcw_ref[k:k + 1, :] += _rowsum(dxb * _roll(extx, 3 - k, 0)[CONV_HALO:])
        dz = jnp.concatenate([dgate_ref[...], dxbp], axis=1)
        dz_ref[...] = dz.astype(BF16)
        x_ = x_ref[...]
        h, rstd = _rms(x_, g_ref[...])
        h_ref[...] = h.astype(BF16)
        dxn, dgr = _rms_bwd(x_, g_ref[...], rstd, _dot(dz, winT_ref[...]))
        dx_ref[...] = dy_ref[...] + dxn
        dg_ref[...] += _rowsum(dgr)

    ins = [x, dy, z, z, dxb, dxb, dgate, g, cw, winT]
    in_specs = [_row(ts, D), _row(ts, D), _row(ts, D, 1), _prev(CONV_HALO, D, ts, 1), _row(ts, D),
                _next(CONV_HALO, D, ts, n), _row(ts, D)] + [_const(v.shape) for v in ins[7:]]
    return pl.pallas_call(
        body, name="odd_pre_bwd", grid=(n,), in_specs=in_specs,
        out_specs=[_row(ts, D), _row(ts, D), _row(ts, 2 * D), _acc((4, D)), _acc((1, D))],
        out_shape=[_sds((S, D), F32), _sds((S, D), BF16), _sds((S, 2 * D), BF16), _sds((4, D), F32),
                   _sds((1, D), F32)],
        compiler_params=_cp())(*ins)


def loss_head(x, target, g):
    S = x.shape[0]
    ts = _tile_rows(S, 512)

    def body(x_ref, t_ref, g_ref, dx_ref, dg_ref, loss_ref):
        @pl.when(pl.program_id(0) == 0)
        def _():
            dg_ref[...] = jnp.zeros_like(dg_ref)
            loss_ref[...] = jnp.zeros_like(loss_ref)

        x_ = x_ref[...]
        y, rstd = _rms(x_, g_ref[...])
        err = y - t_ref[...]
        loss_ref[...] += 0.5 * _rowsum(jnp.mean(err * err, axis=1, keepdims=True))
        dxn, dgr = _rms_bwd(x_, g_ref[...], rstd, err * (1.0 / D))
        dx_ref[...] = dxn
        dg_ref[...] += _rowsum(dgr)

    return pl.pallas_call(
        body, name="loss_head", grid=(S // ts,), in_specs=[_row(ts, D), _row(ts, D), _const(g.shape)],
        out_specs=[_row(ts, D), _acc((1, D)), _acc((1, 1))],
        out_shape=[_sds((S, D), F32), _sds((1, D), F32), _sds((1, 1), F32)], compiler_params=_cp())(x, target, g)


def even_post_bwd(dy, o, woT_pool, woT_att):
    S = dy.shape[0]
    ts = _tile_rows(S, 512)

    def body(dy_ref, o_ref, wp_ref, wa_ref, dyp_ref, do_ref, delta_ref):
        dy_ = dy_ref[...]
        dyp_ref[...] = _dot(dy_, wp_ref[...])
        do = _dot(dy_, wa_ref[...])
        do_ref[...] = do.astype(BF16)
        prod = do * o_ref[...]
        delta_ref[...] = jnp.concatenate(
            [jnp.broadcast_to(jnp.sum(prod[:, h * LANES:(h + 1) * LANES], axis=1, keepdims=True), (ts, LANES))
             for h in range(MLA_HEADS)], axis=1)

    return pl.pallas_call(
        body, name="even_post_bwd", grid=(S // ts,),
        in_specs=[_row(ts, D), _row(ts, D), _const(woT_pool.shape), _const(woT_att.shape)],
        out_specs=[_row(ts, POOL_DIM), _row(ts, D), _row(ts, D)],
        out_shape=[_sds((S, POOL_DIM), F32), _sds((S, D), BF16), _sds((S, D), F32)],
        compiler_params=_cp())(dy, o, woT_pool, woT_att)


def attn_bwd(qp, kp, vp, do, lse_row, delta_row):
    S = qp.shape[0]
    tk = _tile_rows(S, 512)
    nq = S // tk

    def body(q_ref, k_ref, v_ref, do_ref, lse_ref, delta_ref, dq_ref, dk_ref, dv_ref):
        kj = pl.program_id(1)

        @pl.when(kj == 0)
        def _():
            dq_ref[...] = jnp.zeros_like(dq_ref)

        k, v = k_ref[...], v_ref[...]

        def block(qi, carry, masked):
            dk, dv = carry
            off = pl.multiple_of(qi * tk, tk)
            q = q_ref[pl.ds(off, tk), :]
            do_ = do_ref[pl.ds(off, tk), :]
            st = _dot_nt(k, q)
            if masked:
                row = lax.broadcasted_iota(jnp.int32, (tk, tk), 0)
                col = lax.broadcasted_iota(jnp.int32, (tk, tk), 1)
                st = jnp.where(col >= row, st, -1e30)
            pt = jnp.exp(st - lse_ref[qi])
            dv = dv + _dot(pt, do_)
            dst = (pt * (_dot_nt(v, do_) - delta_ref[qi])).astype(BF16)
            dk = dk + _dot(dst, q)
            dq_ref[pl.ds(off, tk), :] += _dot_tn(dst, k)
            return dk, dv

        zero = jnp.zeros((tk, LANES), F32)
        carry = block(kj, (zero, zero), True)
        dk, dv = _pair_loop(kj + 1, nq, lambda qi, c: block(qi, c, False), carry)
        dk_ref[...] = dk
        dv_ref[...] = dv

    blk = pl.BlockSpec((tk, LANES), lambda h, j: (j, h))
    full = pl.BlockSpec((S, LANES), lambda h, j: (0, h))
    rowv = pl.BlockSpec((None, nq, 1, tk), lambda h, j: (h, 0, 0, 0))
    return pl.pallas_call(
        body, name="attn_bwd", grid=(MLA_HEADS, nq), in_specs=[full, blk, blk, full, rowv, rowv],
        out_specs=[full, blk, blk], out_shape=[_sds((S, D), F32)] * 3, compiler_params=_cp2())(
            qp, kp, vp, do, lse_row, delta_row)


def even_pre_bwd(x, dy, z, dq, dk, dv, dyp, tabs, g, winT, pw, pwT, pscale, qg, wqT, kvg, wkT, wvT):
    S = x.shape[0]
    ts = _tile_rows(S, 512)
    n = S // ts

    def body(x_ref, dy_ref, z_ref, up_ref, dq_ref, dk_ref, dv_ref, dyp_ref, dypn_ref, c_ref, a_ref, b_ref,
             g_ref, winT_ref, pw_ref, pwT_ref, ps_ref, qg_ref, wqT_ref, kvg_ref, wkT_ref, wvT_ref,
             dx_ref, h_ref, dz_ref, dg_ref, dpw_ref, dps_ref, dqg_ref, dwq_ref, dkvg_ref, dwk_ref, dwv_ref):
        i = pl.program_id(0)

        @pl.when(i == 0)
        def _():
            for ref in (dg_ref, dpw_ref, dps_ref, dqg_ref, dwq_ref, dkvg_ref, dwk_ref, dwv_ref):
                ref[...] = jnp.zeros_like(ref)

        z = z_ref[...]
        c, a, b = c_ref[...], a_ref[...], b_ref[...]
        ps = ps_ref[...]
        u = z[:, :POOL_DIM]
        pooled = _pooled(up_ref[...] * (i > 0).astype(F32), u, i * ts)
        dyp_ = dyp_ref[...]
        dps_ref[...] += _rowsum(dyp_ * _blockdot(pooled, pw_ref, 4, LANES))
        ext = jnp.concatenate([dyp_, dypn_ref[...] * (i < n - 1).astype(F32)], axis=0) * ps
        for gidx in range(4):
            sl = slice(gidx * LANES, (gidx + 1) * LANES)
            dpw_ref[gidx] += _dot_tn(pooled[:, sl], ext[:ts, sl])
        dpooled = _blockdot(ext, pwT_ref, 4, LANES)
        dm = dpooled / _pool_cnt(i * ts, ts + POOL_HALO)
        du = _pool_windows(dm, -1)[:ts] - dpooled[:ts]
        cq = z[:, 512:768]
        cqn, rstd_q = _rms(cq, qg_ref[...])
        dqf = _rope_bwd(dq_ref[...] * ATTN_SCALE, c, a, b)
        dwq_ref[...] += _dot_tn(cqn, dqf)
        dcq, dqg_rows = _rms_bwd(cq, qg_ref[...], rstd_q, _dot(dqf, wqT_ref[...]))
        dqg_ref[...] += _rowsum(dqg_rows)
        ckv = z[:, 768:896]
        ckvn, rstd_kv = _rms(ckv, kvg_ref[...])
        dk_, dv_ = dk_ref[...], dv_ref[...]
        dwk_ref[...] += _dot_tn(ckvn, dk_)
        dwv_ref[...] += _dot_tn(ckvn, dv_)
        dckv, dkvg_rows = _rms_bwd(ckv, kvg_ref[...], rstd_kv, _dot(dk_, wkT_ref[...]) + _dot(dv_, wvT_ref[...]))
        dkvg_ref[...] += _rowsum(dkvg_rows)
        dkr = dk_[:, :LANES]
        for h in range(1, MLA_HEADS):
            dkr = dkr + dk_[:, h * LANES:(h + 1) * LANES]
        lane = lax.broadcasted_iota(jnp.int32, (ts, LANES), 1)
        dkr = jnp.where((lane >= 64) & (lane < 96), _rope_bwd(dkr, c, a, b), 0.0)
        dz = jnp.concatenate([du, dcq, dckv, dkr], axis=1)
        dz_ref[...] = dz.astype(BF16)
        x_ = x_ref[...]
        h, rstd = _rms(x_, g_ref[...])
        h_ref[...] = h.astype(BF16)
        dxn, dgr = _rms_bwd(x_, g_ref[...], rstd, _dot(dz, winT_ref[...]))
        dx_ref[...] = dy_ref[...] + dxn
        dg_ref[...] += _rowsum(dgr)

    ins = [x, dy, z, z, dq, dk, dv, dyp, dyp, *tabs, g, winT, pw, pwT, pscale, qg, wqT, kvg, wkT, wvT]
    in_specs = [_row(ts, D), _row(ts, D), _row(ts, D), _prev(POOL_HALO, POOL_DIM, ts), _row(ts, D), _row(ts, D),
                _row(ts, D), _row(ts, POOL_DIM), _next(POOL_HALO, POOL_DIM, ts, n), _row(ts, LANES),
                _row(ts, LANES), _row(ts, LANES)] + [_const(v.shape) for v in ins[12:]]
    acc_shapes = [(1, D), (4, LANES, LANES), (1, POOL_DIM), (1, Q_LORA), (Q_LORA, D), (1, KV_LORA), (KV_LORA, D),
                  (KV_LORA, D)]
    return pl.pallas_call(
        body, name="even_pre_bwd", grid=(n,), in_specs=in_specs,
        out_specs=[_row(ts, D)] * 3 + [_acc(s) for s in acc_shapes],
        out_shape=[_sds((S, D), F32), _sds((S, D), BF16), _sds((S, D), BF16)] + [_sds(s, F32) for s in acc_shapes],
        compiler_params=_cp())(*ins)


def _pick(n, options):
    for o in options:
        if n % o == 0:
            return o
    return n


def matmul_tn(name, a, b):
    out_dtype = BF16
    S = a.shape[-2]
    ts = _tile_rows(S, 1024)
    steps = S // ts

    def body(a_ref, b_ref, o_ref, acc_ref):
        s = pl.program_id(2)

        @pl.when(s == 0)
        def _():
            acc_ref[...] = jnp.zeros_like(acc_ref)

        acc_ref[...] += _dot_tn(a_ref[...], b_ref[...])

        @pl.when(s == steps - 1)
        def _():
            o_ref[...] = acc_ref[...].astype(o_ref.dtype)

    if a.ndim == 3:
        C, _, K = a.shape
        N = b.shape[1]
        tn = _pick(N, (512, 256, 128))
        grid = (C, N // tn, S // ts)
        in_specs = [pl.BlockSpec((None, ts, K), lambda c, j, s: (c, s, 0)),
                    pl.BlockSpec((ts, tn), lambda c, j, s: (s, j))]
        out_spec, out_shape, tile = pl.BlockSpec((None, K, tn), lambda c, j, s: (c, 0, j)), (C, K, N), (K, tn)
    elif b.ndim == 3:
        C, _, N = b.shape
        K = a.shape[1]
        tk = _pick(K, (512, 256, 128))
        grid = (C, K // tk, S // ts)
        in_specs = [pl.BlockSpec((ts, tk), lambda c, i, s: (s, i)),
                    pl.BlockSpec((None, ts, N), lambda c, i, s: (c, s, 0))]
        out_spec, out_shape, tile = pl.BlockSpec((None, tk, N), lambda c, i, s: (c, i, 0)), (C, K, N), (tk, N)
    else:
        K, N = a.shape[1], b.shape[1]
        tk = _pick(K, (512, 256, 128))
        tn = _pick(N, (512, 256, 128))
        grid = (K // tk, N // tn, S // ts)
        in_specs = [pl.BlockSpec((ts, tk), lambda i, j, s: (s, i)), pl.BlockSpec((ts, tn), lambda i, j, s: (s, j))]
        out_spec, out_shape, tile = pl.BlockSpec((tk, tn), lambda i, j, s: (i, j)), (K, N), (tk, tn)
    return pl.pallas_call(
        body, name=name, grid=grid, in_specs=in_specs, out_specs=out_spec, out_shape=_sds(out_shape, out_dtype),
        scratch_shapes=[pltpu.VMEM(tile, F32)], compiler_params=pltpu.CompilerParams(dimension_semantics=("arbitrary",) * 3, vmem_limit_bytes=VMEM_LIMIT))(
            a, b)


def _my_id():
    return lax.axis_index("x") * 4 + lax.axis_index("y") * 2 + lax.axis_index("c")


def _peer(j):
    x, y, c = lax.axis_index("x"), lax.axis_index("y"), lax.axis_index("c")
    px = 1 - x if j & 4 else x
    py = 1 - y if j & 2 else y
    pc = 1 - c if j & 1 else c
    return (px, py, pc), px * 4 + py * 2 + pc


def all_gather(name, arrays):
    n = len(arrays)

    def body(*refs):
        ins, outs = refs[:n], refs[n:2 * n]
        send_sems, recv_sems, local_sems = refs[2 * n:]
        me = _my_id()
        local = [pltpu.make_async_copy(ins[k], outs[k].at[me], local_sems.at[k]) for k in range(n)]
        for cp in local:
            cp.start()
        sends = []
        for j in range(1, N_DEV):
            peer, _ = _peer(j)
            for k in range(n):
                cp = pltpu.make_async_remote_copy(
                    src_ref=ins[k], dst_ref=outs[k].at[me], send_sem=send_sems.at[k, j - 1],
                    recv_sem=recv_sems.at[k, j - 1], device_id=peer, device_id_type=pl.DeviceIdType.MESH)
                cp.start()
                sends.append(cp)
        for j in range(1, N_DEV):
            peer, pid = _peer(j)
            for k in range(n):
                pltpu.make_async_remote_copy(
                    src_ref=ins[k], dst_ref=outs[k].at[pid], send_sem=send_sems.at[k, j - 1],
                    recv_sem=recv_sems.at[k, j - 1], device_id=peer, device_id_type=pl.DeviceIdType.MESH).wait_recv()
        for cp in sends:
            cp.wait_send()
        for cp in local:
            cp.wait()

    any_spec = pl.BlockSpec(memory_space=pl.ANY)
    return pl.pallas_call(
        body, name=name, in_specs=[any_spec] * n, out_specs=[any_spec] * n,
        out_shape=[_sds((N_DEV,) + a.shape, a.dtype) for a in arrays],
        scratch_shapes=[pltpu.SemaphoreType.DMA((n, N_DEV - 1)), pltpu.SemaphoreType.DMA((n, N_DEV - 1)),
                        pltpu.SemaphoreType.DMA((n,))],
        compiler_params=pltpu.CompilerParams(has_side_effects=True))(*arrays)


def exchange(name, arrays):
    n = len(arrays)

    def body(*refs):
        ins, outs = refs[:n], refs[n:2 * n]
        send_sems, recv_sems, local_sems = refs[2 * n:]
        me = _my_id()
        local = [pltpu.make_async_copy(ins[k].at[me], outs[k].at[me], local_sems.at[k]) for k in range(n)]
        for cp in local:
            cp.start()
        sends = []
        for j in range(1, N_DEV):
            peer, pid = _peer(j)
            for k in range(n):
                cp = pltpu.make_async_remote_copy(
                    src_ref=ins[k].at[pid], dst_ref=outs[k].at[me], send_sem=send_sems.at[k, j - 1],
                    recv_sem=recv_sems.at[k, j - 1], device_id=peer, device_id_type=pl.DeviceIdType.MESH)
                cp.start()
                sends.append(cp)
        for j in range(1, N_DEV):
            peer, pid = _peer(j)
            for k in range(n):
                pltpu.make_async_remote_copy(
                    src_ref=ins[k].at[me], dst_ref=outs[k].at[pid], send_sem=send_sems.at[k, j - 1],
                    recv_sem=recv_sems.at[k, j - 1], device_id=peer, device_id_type=pl.DeviceIdType.MESH).wait_recv()
        for cp in sends:
            cp.wait_send()
        for cp in local:
            cp.wait()

    any_spec = pl.BlockSpec(memory_space=pl.ANY)
    return pl.pallas_call(
        body, name=name, in_specs=[any_spec] * n, out_specs=[any_spec] * n,
        out_shape=[_sds(a.shape, a.dtype) for a in arrays],
        scratch_shapes=[pltpu.SemaphoreType.DMA((n, N_DEV - 1)), pltpu.SemaphoreType.DMA((n, N_DEV - 1)),
                        pltpu.SemaphoreType.DMA((n,))],
        compiler_params=pltpu.CompilerParams(has_side_effects=True))(*arrays)


ADAMW_BLOCK_ELEMS = 128 * 1024


def adamw(name, parts, w, m, v):
    R, C = w.shape
    tr = _pick(R, [t for t in (512, 256, 128, 64, 32, 16, 8) if t * C <= ADAMW_BLOCK_ELEMS])
    c1 = 1.0 - ADAM_B1 ** ADAM_STEP
    c2 = 1.0 - ADAM_B2 ** ADAM_STEP

    def body(p_ref, w_ref, m_ref, v_ref, g_ref, d_ref, nm_ref, nv_ref):
        g = p_ref[0].astype(F32)
        for s in range(1, N_DEV):
            g = g + p_ref[s].astype(F32)
        g_ref[...] = g
        m_ = ADAM_B1 * m_ref[...] + (1.0 - ADAM_B1) * g
        v_ = ADAM_B2 * v_ref[...] + (1.0 - ADAM_B2) * (g * g)
        nm_ref[...] = m_
        nv_ref[...] = v_
        d_ref[...] = -ADAM_LR * ((m_ / c1) / (jnp.sqrt(v_ / c2) + ADAM_EPS) + ADAM_WD * w_ref[...])

    row = pl.BlockSpec((tr, C), lambda i: (i, 0))
    return pl.pallas_call(
        body, name=name, grid=(R // tr,),
        in_specs=[pl.BlockSpec((N_DEV, tr, C), lambda i: (0, i, 0)), row, row, row], out_specs=[row] * 4,
        out_shape=[_sds((R, C), F32)] * 4, compiler_params=_cp())(parts, w, m, v)


WEIGHTS = ['ev_norm', 'ev_w_in', 'ev_pool_w', 'ev_pool_scale', 'ev_q_norm', 'ev_w_q_up', 'ev_kv_norm', 'ev_w_kv_up',
           'ev_w_out', 'od_norm', 'od_w_in', 'od_conv_w', 'od_conv_b', 'od_w_rgate', 'od_b_rgate', 'od_w_igate',
           'od_b_igate', 'od_lambda', 'od_w_out', 'xa_norm_x', 'xa_norm_mem', 'xa_w_q', 'xa_w_kv', 'xa_w_o',
           'ffn_norm', 'ffn_w_gate_up', 'ffn_w_down', 'final_norm']
SHARD_AXIS = {'ev_w_in': 1, 'ev_w_q_up': 2, 'ev_w_kv_up': 2, 'ev_w_out': 1, 'od_norm': 1, 'od_w_in': 2,
              'od_conv_w': 2, 'od_conv_b': 1, 'od_w_rgate': 2, 'od_b_rgate': 1, 'od_w_igate': 2, 'od_b_igate': 1,
              'od_lambda': 1, 'od_w_out': 1, 'xa_w_q': 1, 'xa_w_kv': 2, 'xa_w_o': 1, 'ffn_w_gate_up': 2,
              'ffn_w_down': 1}
SMALL_F32 = ('od_norm', 'od_conv_w', 'od_conv_b', 'od_b_rgate', 'od_b_igate', 'od_lambda')
STACKED = ('ffn_w_gate_up', 'ffn_w_down')
SHARDED = [n for n in WEIGHTS if n in SHARD_AXIS]
REPLICATED = [n for n in WEIGHTS if n not in SHARD_AXIS]
ROW_ALIGN = 512


def _pack(flats, dtype):
    v = jnp.concatenate([f.reshape(-1).astype(dtype) for f in flats])
    pad = (-v.shape[0]) % (ROW_ALIGN * LANES)
    return jnp.pad(v, (0, pad)).reshape(-1, LANES)


def _rows8(n_elems):
    return -(-n_elems // (8 * LANES)) * 8


def _pack_rows(arrays, lead=False):
    out = []
    for a in arrays:
        r = a.reshape((N_DEV, -1, LANES) if lead else (-1, LANES))
        pad = _rows8(r.shape[-2] * LANES) - r.shape[-2]
        out.append(jnp.pad(r, [(0, 0)] * (r.ndim - 2) + [(0, pad), (0, 0)]))
    return jnp.concatenate(out, axis=-2)


def _unpack_rows(buf, shapes, lead=False):
    out, off = [], 0
    for s in shapes:
        n = 1
        for d in s:
            n *= d
        rows = buf[..., off:off + n // LANES, :]
        out.append(rows.reshape(((N_DEV,) if lead else ()) + tuple(s)))
        off += _rows8(n)
    return out


def _unpack(flat, shapes):
    out, off = [], 0
    v = flat.reshape(-1)
    for s in shapes:
        n = 1
        for d in s:
            n *= d
        out.append(v[off:off + n].reshape(s))
        off += n
    return out


def _to_full(stacked, axis):
    v = jnp.moveaxis(stacked, 0, axis)
    s = v.shape
    return v.reshape(s[:axis] + (s[axis] * s[axis + 1],) + s[axis + 2:])


def _to_shards(full, axis):
    s = full.shape
    v = full.reshape(s[:axis] + (N_DEV, s[axis] // N_DEV) + s[axis + 1:])
    return jnp.moveaxis(v, axis, 0)


def _pad_heads(w, nh, dh, lead):
    s = w.shape
    v = w.reshape(s[:-1] + (nh, dh))
    v = jnp.pad(v, [(0, 0)] * (len(s) - 1) + [(0, 0), (lead, LANES - dh - lead)])
    return v.reshape(s[:-1] + (nh * LANES,))


def _unpad_heads(w, nh, dh, lead):
    s = w.shape
    return w.reshape(s[:-1] + (nh, LANES))[..., lead:lead + dh].reshape(s[:-1] + (nh * dh,))


def _rope_tables(positions):
    inv_freq = 10000.0 ** (-jnp.arange(0, 32, 2, dtype=F32) / 32)
    ang = positions.astype(F32)[:, None] * inv_freq
    cos, sin = jnp.cos(ang), jnp.sin(ang)
    S = positions.shape[0]
    one, zero = jnp.ones((S, 64), F32), jnp.zeros((S, 64), F32)
    z16, z32 = jnp.zeros((S, 16), F32), jnp.zeros((S, 32), F32)
    c = jnp.concatenate([one, cos, cos, jnp.ones((S, 32), F32)], axis=1)
    a = jnp.concatenate([zero, z16, sin, z32], axis=1)
    b = jnp.concatenate([zero, -sin, z16, z32], axis=1)
    return c, a, b


def _t(w):
    return jnp.swapaxes(w, -1, -2)


def _col_to_row(v, tq):
    S = v.shape[0]
    return v[:, ::LANES].T.reshape(MLA_HEADS, S // tq, 1, tq)


def device_step(x, mem, positions, target, W):
    S = x.shape[0]
    G = {}
    tabs = _rope_tables(positions)
    keep = (positions != 0).astype(F32)[:, None]
    row = lambda v: v.reshape(1, -1)

    w_in = W['ev_w_in'][0]
    ev_win = jnp.concatenate([w_in[:, :896], _pad_heads(w_in[:, 896:], 1, 32, 64)], axis=1)
    ev_wq = _pad_heads(W['ev_w_q_up'][0], MLA_HEADS, QK_DIM, 0)
    kvw = W['ev_w_kv_up'][0].reshape(KV_LORA, MLA_HEADS, 128)
    ev_wk = _pad_heads(kvw[:, :, :64].reshape(KV_LORA, 512), MLA_HEADS, 64, 0)
    ev_wv = _pad_heads(kvw[:, :, 64:].reshape(KV_LORA, 512), MLA_HEADS, 64, 0)
    ev_wo_pool = W['ev_w_out'][0][:POOL_DIM]
    ev_wo_att = _t(_pad_heads(_t(W['ev_w_out'][0][POOL_DIM:]), MLA_HEADS, 64, 0))
    pw = W['ev_pool_w'][0].astype(BF16)
    ev_g, ps, qg, kvg = row(W['ev_norm'][0]), row(W['ev_pool_scale'][0]), row(W['ev_q_norm'][0]), row(W['ev_kv_norm'][0])

    z0, qp, kp, vp, ypool = even_pre(x, tabs, ev_g, ev_win, pw, ps, qg, ev_wq, kvg, ev_wk, ev_wv)
    o_att, lse = attn_fwd(qp, kp, vp)
    x1 = even_post(x, ypool, o_att, ev_wo_pool, ev_wo_att)

    def xa_ffn_fwd(xin, l):
        mn, km, vm = mem_kv(mem, row(W['xa_norm_mem'][l]), W['xa_w_kv'][l])
        xm = xattn_fwd(xin, row(W['xa_norm_x'][l]), W['xa_w_q'][l], km, vm, W['xa_w_o'][l])
        xo = ffn_fwd(xm, row(W['ffn_norm'][l]), W['ffn_w_gate_up'][:, l],
                     W['ffn_w_down'][:, l].reshape(FF_HALF, FF_CHUNK, D))
        return xm, xo, (mn, km, vm)

    x2, x3, memkv0 = xa_ffn_fwd(x1, 0)

    od_g, lam = row(W['od_norm'][0]), row(W['od_lambda'][0])
    cw, cb = W['od_conv_w'][0], row(W['od_conv_b'][0])
    wr, wi = W['od_w_rgate'][0], W['od_w_igate'][0]
    br, bi = row(W['od_b_rgate'][0]), row(W['od_b_igate'][0])
    z1, a_t, b_t = odd_pre(x3, keep, od_g, W['od_w_in'][0], cw, cb, wr, br, wi, bi, lam)
    hseq = lru_scan(a_t, b_t)
    x4 = odd_post(x3, z1, hseq, W['od_w_out'][0])
    x5, x6, memkv1 = xa_ffn_fwd(x4, 1)

    dx, G['final_norm'], loss = loss_head(x6, target, row(W['final_norm']))
    G['final_norm'] = G['final_norm'].reshape(D)

    gnx, gnm, gwq, gwkv, gwo, gfn, gwgu, gwd = ([None, None] for _ in range(8))

    def xa_ffn_bwd(dy, xin, xm, memkv, l):
        mn, km, vm = memkv
        fg = row(W['ffn_norm'][l])
        wgu = W['ffn_w_gate_up'][:, l]
        hf, act, dgu = ffn_bwd_a(xm, dy, fg, wgu, _t(W['ffn_w_down'][:, l].reshape(FF_HALF, FF_CHUNK, D)))
        gwd[l] = matmul_tn("ffn_dwd", act, dy).reshape(N_DEV, D_FF // N_DEV, D)
        gwgu[l] = matmul_tn("ffn_dwgu", hf, dgu)
        dxm, dfg = ffn_bwd_b(xm, dy, dgu, fg, _t(wgu))
        gfn[l] = dfg[0]
        dxin, o, dq, hx, dgx, dk, dv = xattn_bwd(xin, dxm, row(W['xa_norm_x'][l]), W['xa_w_q'][l],
                                                  _t(W['xa_w_q'][l]), km, vm, _t(W['xa_w_o'][l]))
        gnx[l] = dgx[0]
        gwo[l] = matmul_tn("xa_dwo", o, dxm)
        gwq[l] = matmul_tn("xa_dwq", hx, dq)
        dkv, dgm = mem_bwd(mem, row(W['xa_norm_mem'][l]), dk, dv, _t(W['xa_w_kv'][l]))
        gnm[l] = dgm[0]
        gwkv[l] = matmul_tn("xa_dwkv", mn, dkv)
        return dxin

    dx4 = xa_ffn_bwd(dx, x4, x5, memkv1, 1)

    y_od, dgate, dhs = odd_post_bwd(dx4, z1, hseq, _t(W['od_w_out'][0]))
    G['od_w_out'] = matmul_tn("od_dwout", y_od, dx4)[None]
    lam_grad = lru_scan(a_t, dhs, reverse=True)
    dxb, dcb, dbr, dbi, dlam, dwr, dwi = odd_gates_bwd(z1, lam_grad, hseq, keep, cw, cb, wr, _t(wr), br, wi, _t(wi),
                                                        bi, lam)
    dx3, h_od, dz1, dcw, dg_od = odd_pre_bwd(x3, dx4, z1, dxb, dgate, od_g, cw, _t(W['od_w_in'][0]))
    G['od_w_in'] = matmul_tn("od_dwin", h_od, dz1)[None]
    G['od_norm'], G['od_conv_w'], G['od_conv_b'] = dg_od, dcw[None], dcb
    G['od_w_rgate'], G['od_b_rgate'], G['od_w_igate'], G['od_b_igate'], G['od_lambda'] = (
        dwr[None], dbr, dwi[None], dbi, dlam)

    dx1 = xa_ffn_bwd(dx3, x1, x2, memkv0, 0)

    dyp, do_att, delta = even_post_bwd(dx1, o_att, _t(ev_wo_pool), _t(ev_wo_att))
    g_wo_pool = matmul_tn("ev_dwo_pool", ypool, dx1)
    g_wo_att = matmul_tn("ev_dwo_att", o_att, dx1)
    G['ev_w_out'] = jnp.concatenate([g_wo_pool, _t(_unpad_heads(_t(g_wo_att), MLA_HEADS, 64, 0))], axis=0)[None]
    tq = _tile_rows(S, 512)
    dq, dk, dv = attn_bwd(qp, kp, vp, do_att, _col_to_row(lse, tq), _col_to_row(delta, tq))
    (grad_x, h_ev, dz0, dg_ev, dpw, dps, dqg, dwq, dkvg, dwk, dwv) = even_pre_bwd(
        x, dx1, z0, dq, dk, dv, dyp, tabs, ev_g, _t(ev_win), pw, _t(pw), ps, qg, _t(ev_wq), kvg, _t(ev_wk),
        _t(ev_wv))
    g_win = matmul_tn("ev_dwin", h_ev, dz0)
    G['ev_w_in'] = jnp.concatenate([g_win[:, :896], _unpad_heads(g_win[:, 896:], 1, 32, 64)], axis=1)[None]
    G['ev_norm'], G['ev_pool_w'], G['ev_pool_scale'], G['ev_q_norm'], G['ev_kv_norm'] = (
        dg_ev, dpw[None], dps, dqg, dkvg)
    G['ev_w_q_up'] = _unpad_heads(dwq, MLA_HEADS, QK_DIM, 0)[None]
    gk = _unpad_heads(dwk, MLA_HEADS, 64, 0).reshape(KV_LORA, MLA_HEADS, 64)
    gv = _unpad_heads(dwv, MLA_HEADS, 64, 0).reshape(KV_LORA, MLA_HEADS, 64)
    G['ev_w_kv_up'] = jnp.concatenate([gk, gv], axis=2).reshape(1, KV_LORA, MLA_HEADS * 128)

    G['xa_norm_x'], G['xa_norm_mem'], G['ffn_norm'] = jnp.stack(gnx), jnp.stack(gnm), jnp.stack(gfn)
    G['xa_w_q'], G['xa_w_kv'], G['xa_w_o'] = jnp.stack(gwq), jnp.stack(gwkv), jnp.stack(gwo)
    G['ffn_w_gate_up'], G['ffn_w_down'] = jnp.stack(gwgu, axis=1), jnp.stack(gwd, axis=1)
    return loss[0, 0], grad_x, G


def kernel(x, mem, positions, ev_norm, ev_w_in, ev_pool_w, ev_pool_scale, ev_q_norm, ev_w_q_up, ev_kv_norm, ev_w_kv_up, ev_w_out, od_norm, od_w_in, od_conv_w, od_conv_b, od_w_rgate, od_b_rgate, od_w_igate, od_b_igate, od_lambda, od_w_out, xa_norm_x, xa_norm_mem, xa_w_q, xa_w_kv, xa_w_o, ffn_norm, ffn_w_gate_up, ffn_w_down, final_norm, loss_target, m_ev_norm, m_ev_w_in, m_ev_pool_w, m_ev_pool_scale, m_ev_q_norm, m_ev_w_q_up, m_ev_kv_norm, m_ev_w_kv_up, m_ev_w_out, m_od_norm, m_od_w_in, m_od_conv_w, m_od_conv_b, m_od_w_rgate, m_od_b_rgate, m_od_w_igate, m_od_b_igate, m_od_lambda, m_od_w_out, m_xa_norm_x, m_xa_norm_mem, m_xa_w_q, m_xa_w_kv, m_xa_w_o, m_ffn_norm, m_ffn_w_gate_up, m_ffn_w_down, m_final_norm, v_ev_norm, v_ev_w_in, v_ev_pool_w, v_ev_pool_scale, v_ev_q_norm, v_ev_w_q_up, v_ev_kv_norm, v_ev_w_kv_up, v_ev_w_out, v_od_norm, v_od_w_in, v_od_conv_w, v_od_conv_b, v_od_w_rgate, v_od_b_rgate, v_od_w_igate, v_od_b_igate, v_od_lambda, v_od_w_out, v_xa_norm_x, v_xa_norm_mem, v_xa_w_q, v_xa_w_kv, v_xa_w_o, v_ffn_norm, v_ffn_w_gate_up, v_ffn_w_down, v_final_norm):
    args = dict(locals())
    w = {n: args[n] for n in WEIGHTS}
    m = {n: args['m_' + n] for n in WEIGHTS}
    v = {n: args['v_' + n] for n in WEIGHTS}
    big = [n for n in SHARDED if n not in SMALL_F32]
    small = [n for n in SHARDED if n in SMALL_F32]

    small_shapes = [w[n].shape for n in small]
    gathered = all_gather("gather_weights", [w[n].astype(BF16) for n in big] + [_pack_rows([w[n] for n in small])])
    W = {n: w[n] for n in REPLICATED}
    for n, st in zip(big, gathered):
        W[n] = st if n in STACKED else _to_full(st, SHARD_AXIS[n])
    for n, st in zip(small, _unpack_rows(gathered[-1], small_shapes, lead=True)):
        W[n] = _to_full(st, SHARD_AXIS[n])

    loss, grad_x, G = device_step(x[0], mem[0], positions[0], loss_target[0], W)
    outs = [{}, {}, {}, {}]

    rep_shapes = [w[n].shape for n in REPLICATED] + [(LANES,)]
    zero = jnp.zeros((LANES,), F32)
    rep_parts, = all_gather("gather_rep_grads", [_pack(
        [G[n] for n in REPLICATED] + [jnp.broadcast_to(loss, (LANES,))], F32)])
    rep = adamw("adamw_rep", rep_parts, *[_pack([d[n] for n in REPLICATED] + [zero], F32) for d in (w, m, v)])
    for k in range(4):
        outs[k].update(zip(REPLICATED + ['loss'], _unpack(rep[k], rep_shapes)))
    loss = outs[0]['loss'][0]

    def shards(n):
        return G[n] if n in STACKED else _to_shards(G[n], SHARD_AXIS[n])

    parts = exchange("exchange_grads", [shards(n).astype(BF16) for n in big] +
                     [_pack_rows([shards(n) for n in small], lead=True)])
    two_d = lambda a: a.reshape(-1, a.shape[-1])
    for n, p in zip(big, parts):
        res = adamw("adamw_" + n, p.reshape((N_DEV,) + two_d(w[n]).shape), two_d(w[n]), two_d(m[n]), two_d(v[n]))
        for k in range(4):
            outs[k][n] = res[k].reshape(w[n].shape)
    res = adamw("adamw_small", parts[-1], *[_pack_rows([d[n] for n in small]) for d in (w, m, v)])
    for k in range(4):
        outs[k].update(zip(small, _unpack_rows(res[k], small_shapes)))

    return (loss, grad_x[None], *[outs[0][n] for n in WEIGHTS], *[outs[1][n] for n in WEIGHTS],
            *[outs[2][n] for n in WEIGHTS], *[outs[3][n] for n in WEIGHTS])
```

```python
import functools

import jax
import jax.numpy as jnp
from jax import lax
from jax.experimental import pallas as pl
from jax.experimental.pallas import tpu as pltpu

F32, BF16 = jnp.float32, jnp.bfloat16
N_DEV = 8
D = 1024
POOL_DIM = 512
POOL_WINDOWS = (2, 4, 8, 16)
MLA_HEADS = 8
QK_DIM = 96
Q_LORA, KV_LORA = 256, 128
LRU_HEADS, LRU_HEAD_DIM = 4, 256
LRU_C = 8.0
MEM_HEADS, MEM_HEAD_DIM = 4, 256
D_FF = 2816
RMS_EPS = 1e-6
ADAM_LR, ADAM_B1, ADAM_B2, ADAM_EPS, ADAM_WD, ADAM_STEP = 0.001, 0.9, 0.999, 1e-08, 0.01, 10
LANES = 128
POOL_HALO = 16
CONV_HALO = 8
VMEM_LIMIT = 60000 * 1024


def _cp():
    return pltpu.CompilerParams(dimension_semantics=("arbitrary",), vmem_limit_bytes=VMEM_LIMIT)


def _cp2():
    return pltpu.CompilerParams(dimension_semantics=("arbitrary", "arbitrary"), vmem_limit_bytes=VMEM_LIMIT)


def _row(ts, c, col=0):
    return pl.BlockSpec((ts, c), lambda i: (i, col))


def _prev(hr, c, ts, col=0):
    r = ts // hr
    return pl.BlockSpec((hr, c), lambda i: (jnp.maximum(i * r - 1, 0), col))


def _next(hr, c, ts, n, col=0):
    r = ts // hr
    return pl.BlockSpec((hr, c), lambda i: (jnp.minimum((i + 1) * r, n * r - 1), col))


def _const(shape):
    nd = len(shape)
    return pl.BlockSpec(tuple(shape), lambda i: (0,) * nd, pipeline_mode=pl.Buffered(1))


def _acc(shape):
    nd = len(shape)
    return pl.BlockSpec(tuple(shape), lambda i: (0,) * nd)


def _sds(shape, dt):
    return jax.ShapeDtypeStruct(tuple(shape), dt)


def _dot(a, b):
    return jnp.dot(a.astype(BF16), b.astype(BF16), preferred_element_type=F32)


def _dot_nt(a, b):
    return lax.dot_general(a.astype(BF16), b.astype(BF16), (((1,), (1,)), ((), ())), preferred_element_type=F32)


def _dot_tn(a, b):
    return lax.dot_general(a.astype(BF16), b.astype(BF16), (((0,), (0,)), ((), ())), preferred_element_type=F32)


def _rms(x, g):
    rstd = lax.rsqrt(jnp.mean(x * x, axis=-1, keepdims=True) + RMS_EPS)
    return x * rstd * g, rstd


def _rms_bwd(x, g, rstd, dy):
    xn = x * rstd
    dyg = dy * g
    dx = rstd * (dyg - xn * jnp.mean(dyg * xn, axis=-1, keepdims=True))
    return dx, dy * xn


def _rowsum(v):
    return jnp.sum(v, axis=0, keepdims=True)


def _roll(v, s, axis):
    n = v.shape[axis]
    return pltpu.roll(v, s % n, axis)


def _rope(t, c, a, b):
    k = t.shape[1] // LANES
    if k > 1:
        c, a, b = (jnp.tile(v, (1, k)) for v in (c, a, b))
    return t * c + _roll(t, 16, 1) * a + _roll(t, -16, 1) * b


def _rope_bwd(d, c, a, b):
    k = d.shape[1] // LANES
    if k > 1:
        c, a, b = (jnp.tile(v, (1, k)) for v in (c, a, b))
    return d * c + _roll(d * a, -16, 1) + _roll(d * b, 16, 1)


def _gelu(x):
    c = 0.7978845608028654
    t = jnp.tanh(c * (x + 0.044715 * x * x * x))
    return 0.5 * x * (1.0 + t), t


def _gelu_grad(x, t):
    c = 0.7978845608028654
    return 0.5 * (1.0 + t) + 0.5 * x * (1.0 - t * t) * c * (1.0 + 3.0 * 0.044715 * x * x)


def _blockdot(v, w_ref, nblk, width):
    return jnp.concatenate(
        [_dot(v[:, j * width:(j + 1) * width], w_ref[j]) for j in range(nblk)], axis=1)


def _pool_cnt(row0, rows):
    t = row0 + lax.broadcasted_iota(jnp.int32, (rows, POOL_DIM), 0)
    w = jnp.left_shift(2, lax.broadcasted_iota(jnp.int32, (rows, POOL_DIM), 1) // LANES)
    return jnp.minimum(t + 1, w).astype(F32)


def _pool_windows(ext, sign):
    s2 = ext + _roll(ext, sign * 1, 0)
    t = s2[:, LANES:]
    s4 = t + _roll(t, sign * 2, 0)
    t = s4[:, LANES:]
    s8 = t + _roll(t, sign * 4, 0)
    t = s8[:, LANES:]
    s16 = t + _roll(t, sign * 8, 0)
    return jnp.concatenate([s2[:, :LANES], s4[:, :LANES], s8[:, :LANES], s16], axis=1)


def _pooled(uprev, u, row0):
    ts = u.shape[0]
    ext = jnp.concatenate([uprev, u], axis=0)
    sums = _pool_windows(ext, 1)[POOL_HALO:]
    return sums / _pool_cnt(row0, ts) - u


def _expm1(x):
    return jnp.where(jnp.abs(x) < 0.01, x * (1.0 + 0.5 * x * (1.0 + x * (1.0 / 3.0))), jnp.exp(x) - 1.0)


def _softplus(z):
    return jnp.maximum(z, 0.0) + jnp.log1p(jnp.exp(-jnp.abs(z)))


def _tile_rows(s, want):
    while s % want:
        want //= 2
    return want


def even_pre(x, tabs, g, win, pw, pscale, qg, wq, kvg, wk, wv):
    S = x.shape[0]
    ts = _tile_rows(S, 512)

    def body(x_ref, xp_ref, c_ref, a_ref, b_ref, g_ref, win_ref, pw_ref, ps_ref, qg_ref, wq_ref, kvg_ref,
             wk_ref, wv_ref, z_ref, q_ref, k_ref, v_ref, yp_ref):
        i = pl.program_id(0)
        h, _ = _rms(x_ref[...], g_ref[...])
        z = _dot(h, win_ref[...])
        z_ref[...] = z
        hp, _ = _rms(xp_ref[...], g_ref[...])
        uprev = _dot(hp, win_ref[:, :POOL_DIM]) * (i > 0).astype(F32)
        u = z[:, :POOL_DIM]
        pooled = _pooled(uprev, u, i * ts)
        yp_ref[...] = (_blockdot(pooled, pw_ref, 4, LANES) * ps_ref[...]).astype(BF16)
        c, a, b = c_ref[...], a_ref[...], b_ref[...]
        cqn, _ = _rms(z[:, 512:768], qg_ref[...])
        q_ref[...] = (_rope(_dot(cqn, wq_ref[...]), c, a, b) * (ATTN_SCALE * LOG2_E)).astype(BF16)
        ckvn, _ = _rms(z[:, 768:896], kvg_ref[...])
        krr = _rope(z[:, 896:1024], c, a, b)
        k_ref[...] = (_dot(ckvn, wk_ref[...]) + jnp.tile(krr, (1, MLA_HEADS))).astype(BF16)
        v_ref[...] = _dot(ckvn, wv_ref[...]).astype(BF16)

    ins = [x, x, *tabs, g, win, pw, pscale, qg, wq, kvg, wk, wv]
    in_specs = [_row(ts, D), _prev(POOL_HALO, D, ts), _row(ts, LANES), _row(ts, LANES), _row(ts, LANES)]
    in_specs += [_const(v.shape) for v in ins[5:]]
    return pl.pallas_call(
        body, name="even_pre", grid=(S // ts,), in_specs=in_specs,
        out_specs=[_row(ts, D)] * 4 + [_row(ts, POOL_DIM)],
        out_shape=[_sds((S, D), F32)] + [_sds((S, D), BF16)] * 3 + [_sds((S, POOL_DIM), BF16)],
        compiler_params=_cp())(*ins)


ATTN_SCALE = QK_DIM ** -0.5
LOG2_E = 1.4426950408889634
LN_2 = 0.6931471805599453


def _exp2(x):
    return jnp.exp2(x)


def _pair_loop(lo, hi, step, init):
    pairs = (hi - lo) // 2
    carry = lax.fori_loop(0, pairs, lambda j, c: step(lo + 2 * j + 1, step(lo + 2 * j, c)), init)
    return lax.fori_loop(lo + 2 * pairs, hi, step, carry)


def _after(token):
    return ([], []) if token is None else ([token], [pl.BlockSpec(memory_space=pl.ANY)])


def attn_fwd(qp, kp, vp, token=None):
    S = qp.shape[0]
    tq = _tile_rows(S, 512)
    extra, extra_specs = _after(token)

    def body(q_ref, k_ref, v_ref, *rest):
        o_ref, lse_ref = rest[-2:]
        qi = pl.program_id(1)
        q = q_ref[...]

        def block(ki, carry, masked):
            m, l, acc = carry
            off = pl.multiple_of(ki * tq, tq)
            s = _dot_nt(q, k_ref[pl.ds(off, tq), :])
            if masked:
                row = lax.broadcasted_iota(jnp.int32, (tq, tq), 0)
                col = lax.broadcasted_iota(jnp.int32, (tq, tq), 1)
                s = jnp.where(col <= row, s, -1e30)
            m_new = jnp.maximum(m, jnp.max(s, axis=1, keepdims=True))
            p = _exp2(s - m_new)
            alpha = _exp2(m - m_new)
            l = alpha * l + jnp.sum(p, axis=1, keepdims=True)
            acc = alpha * acc + _dot(p, v_ref[pl.ds(off, tq), :])
            return m_new, l, acc

        init = (jnp.full((tq, 1), -1e30, F32), jnp.zeros((tq, 1), F32), jnp.zeros((tq, LANES), F32))
        carry = _pair_loop(0, qi, lambda ki, c: block(ki, c, False), init)
        m, l, acc = block(qi, carry, True)
        o_ref[...] = acc / l
        lse_ref[...] = jnp.broadcast_to(m + jnp.log(l) * LOG2_E, (tq, LANES))

    blk = pl.BlockSpec((tq, LANES), lambda h, i: (i, h))
    full = pl.BlockSpec((S, LANES), lambda h, i: (0, h))
    return pl.pallas_call(
        body, name="attn_fwd", grid=(MLA_HEADS, S // tq), in_specs=[blk, full, full] + extra_specs,
        out_specs=[blk, blk], out_shape=[_sds((S, D), F32), _sds((S, D), F32)], compiler_params=_cp2())(
            qp, kp, vp, *extra)


def even_post(x, ypool, o, wo_pool, wo_att):
    S = x.shape[0]
    ts = _tile_rows(S, 512)

    def body(x_ref, yp_ref, o_ref, wp_ref, wa_ref, out_ref):
        out_ref[...] = x_ref[...] + _dot(yp_ref[...], wp_ref[...]) + _dot(o_ref[...], wa_ref[...])

    return pl.pallas_call(
        body, name="even_post", grid=(S // ts,),
        in_specs=[_row(ts, D), _row(ts, POOL_DIM), _row(ts, D), _const(wo_pool.shape), _const(wo_att.shape)],
        out_specs=_row(ts, D), out_shape=_sds((S, D), F32), compiler_params=_cp())(x, ypool, o, wo_pool, wo_att)


def mem_kv(mem, g, wkv):
    M = mem.shape[0]

    def body(mem_ref, g_ref, w_ref, mn_ref, k_ref, v_ref):
        mn, _ = _rms(mem_ref[...], g_ref[...])
        mn_ref[...] = mn.astype(BF16)
        k_ref[...] = _dot(mn, w_ref[:, :D]).astype(BF16)
        v_ref[...] = _dot(mn, w_ref[:, D:]).astype(BF16)

    return pl.pallas_call(
        body, name="mem_kv", grid=(1,), in_specs=[_acc(mem.shape), _acc(g.shape), _acc(wkv.shape)],
        out_specs=[_acc((M, D))] * 3, out_shape=[_sds((M, D), BF16)] * 3, compiler_params=_cp())(mem, g, wkv)


def _xattn_heads(hx, wq_ref, k_ref, v_ref):
    q = _dot(hx, wq_ref[...])
    scale = MEM_HEAD_DIM ** -0.5
    ps, os_ = [], []
    for h in range(MEM_HEADS):
        sl = slice(h * MEM_HEAD_DIM, (h + 1) * MEM_HEAD_DIM)
        s = _dot_nt(q[:, sl], k_ref[:, sl]) * scale
        e = jnp.exp(s - jnp.max(s, axis=1, keepdims=True))
        p = e / jnp.sum(e, axis=1, keepdims=True)
        ps.append(p)
        os_.append(_dot(p, v_ref[:, sl]))
    return q, ps, jnp.concatenate(os_, axis=1)


def xattn_fwd(x, g, wq, kmem, vmem, wo):
    S = x.shape[0]
    ts = _tile_rows(S, 512)

    def body(x_ref, g_ref, wq_ref, k_ref, v_ref, wo_ref, out_ref):
        x_ = x_ref[...]
        hx, _ = _rms(x_, g_ref[...])
        _, _, o = _xattn_heads(hx, wq_ref, k_ref, v_ref)
        out_ref[...] = x_ + _dot(o, wo_ref[...])

    ins = [x, g, wq, kmem, vmem, wo]
    return pl.pallas_call(
        body, name="xattn_fwd", grid=(S // ts,), in_specs=[_row(ts, D)] + [_const(v.shape) for v in ins[1:]],
        out_specs=_row(ts, D), out_shape=_sds((S, D), F32), compiler_params=_cp())(*ins)


def xattn_bwd(x, dy, g, wq, wqT, kmem, vmem, woT):
    S = x.shape[0]
    M = kmem.shape[0]
    ts = _tile_rows(S, 512)
    scale = MEM_HEAD_DIM ** -0.5

    def body(x_ref, dy_ref, g_ref, wq_ref, wqT_ref, k_ref, v_ref, woT_ref,
             dx_ref, o_ref, dq_ref, hx_ref, dg_ref, dk_ref, dv_ref):
        i = pl.program_id(0)

        @pl.when(i == 0)
        def _():
            dg_ref[...] = jnp.zeros_like(dg_ref)
            dk_ref[...] = jnp.zeros_like(dk_ref)
            dv_ref[...] = jnp.zeros_like(dv_ref)

        x_, dy_ = x_ref[...], dy_ref[...]
        hx, rstd = _rms(x_, g_ref[...])
        q, ps, o = _xattn_heads(hx, wq_ref, k_ref, v_ref)
        hx_ref[...] = hx.astype(BF16)
        o_ref[...] = o.astype(BF16)
        do = _dot(dy_, woT_ref[...])
        dqs = []
        for h in range(MEM_HEADS):
            sl = slice(h * MEM_HEAD_DIM, (h + 1) * MEM_HEAD_DIM)
            p, do_h = ps[h], do[:, sl]
            dp = _dot_nt(do_h, v_ref[:, sl])
            ds = p * (dp - jnp.sum(p * dp, axis=1, keepdims=True)) * scale
            dqs.append(_dot(ds, k_ref[:, sl]))
            dk_ref[:, sl] += _dot_tn(ds, q[:, sl])
            dv_ref[:, sl] += _dot_tn(p, do_h)
        dq = jnp.concatenate(dqs, axis=1)
        dq_ref[...] = dq.astype(BF16)
        dxn, dgr = _rms_bwd(x_, g_ref[...], rstd, _dot(dq, wqT_ref[...]))
        dx_ref[...] = dy_ + dxn
        dg_ref[...] += _rowsum(dgr)

    ins = [x, dy, g, wq, wqT, kmem, vmem, woT]
    return pl.pallas_call(
        body, name="xattn_bwd", grid=(S // ts,),
        in_specs=[_row(ts, D), _row(ts, D)] + [_const(v.shape) for v in ins[2:]],
        out_specs=[_row(ts, D)] * 4 + [_acc((1, D)), _acc((M, D)), _acc((M, D))],
        out_shape=[_sds((S, D), F32)] + [_sds((S, D), BF16)] * 3 + [_sds((1, D), F32), _sds((M, D), F32),
                                                                    _sds((M, D), F32)],
        compiler_params=_cp())(*ins)


def mem_bwd(mem, g, dk, dv, wkvT):
    M = mem.shape[0]

    def body(mem_ref, g_ref, dk_ref, dv_ref, w_ref, dkv_ref, dg_ref):
        dkv = jnp.concatenate([dk_ref[...], dv_ref[...]], axis=1)
        dkv_ref[...] = dkv.astype(BF16)
        _, rstd = _rms(mem_ref[...], g_ref[...])
        dg_ref[...] = _rowsum(_dot(dkv, w_ref[...]) * (mem_ref[...] * rstd))

    ins = [mem, g, dk, dv, wkvT]
    return pl.pallas_call(
        body, name="mem_bwd", grid=(1,), in_specs=[_acc(v.shape) for v in ins],
        out_specs=[_acc((M, 2 * D)), _acc((1, D))], out_shape=[_sds((M, 2 * D), BF16), _sds((1, D), F32)],
        compiler_params=_cp())(*ins)


FF_CHUNK = 2 * D_FF // N_DEV
FF_HALF = N_DEV // 2


def ffn_fwd(x, g, wgu, wd):
    S = x.shape[0]
    ts = _tile_rows(S, 256)

    def body(x_ref, g_ref, wgu_ref, wd_ref, out_ref):
        x_ = x_ref[...]
        hf = _rms(x_, g_ref[...])[0].astype(BF16)
        out = x_
        for j in range(FF_HALF):
            gg = _dot(hf, wgu_ref[j])
            out = out + _dot(gg * jax.nn.sigmoid(gg) * _dot(hf, wgu_ref[j + FF_HALF]), wd_ref[j])
        out_ref[...] = out

    ins = [x, g, wgu, wd]
    return pl.pallas_call(
        body, name="ffn_fwd", grid=(S // ts,), in_specs=[_row(ts, D)] + [_const(v.shape) for v in ins[1:]],
        out_specs=_row(ts, D), out_shape=_sds((S, D), F32), compiler_params=_cp())(*ins)


def ffn_bwd_a(x, dy, g, wgu, wdT):
    S = x.shape[0]
    ts = _tile_rows(S, 256)

    def body(x_ref, dy_ref, g_ref, wgu_ref, wdT_ref, hf_ref, act_ref, dgu_ref):
        hf = _rms(x_ref[...], g_ref[...])[0].astype(BF16)
        hf_ref[...] = hf
        dy_ = dy_ref[...].astype(BF16)
        for j in range(FF_HALF):
            gg, uu = _dot(hf, wgu_ref[j]), _dot(hf, wgu_ref[j + FF_HALF])
            sg = jax.nn.sigmoid(gg)
            silu = gg * sg
            act_ref[j] = (silu * uu).astype(BF16)
            dact = _dot(dy_, wdT_ref[j])
            dgu_ref[j] = (dact * uu * (sg * (1.0 + gg * (1.0 - sg)))).astype(BF16)
            dgu_ref[j + FF_HALF] = (dact * silu).astype(BF16)

    ins = [x, dy, g, wgu, wdT]
    chunked = lambda c: pl.BlockSpec((c, ts, FF_CHUNK), lambda i: (0, i, 0))
    return pl.pallas_call(
        body, name="ffn_bwd_a", grid=(S // ts,),
        in_specs=[_row(ts, D), _row(ts, D)] + [_const(v.shape) for v in ins[2:]],
        out_specs=[_row(ts, D), chunked(FF_HALF), chunked(N_DEV)],
        out_shape=[_sds((S, D), BF16), _sds((FF_HALF, S, FF_CHUNK), BF16), _sds((N_DEV, S, FF_CHUNK), BF16)],
        compiler_params=_cp())(*ins)


def ffn_bwd_b(x, dy, dgu, g, wguT):
    S = x.shape[0]
    ts = _tile_rows(S, 512)

    def body(x_ref, dy_ref, dgu_ref, g_ref, wT_ref, dx_ref, dg_ref):
        @pl.when(pl.program_id(0) == 0)
        def _():
            dg_ref[...] = jnp.zeros_like(dg_ref)

        dh = _dot(dgu_ref[0], wT_ref[0])
        for j in range(1, N_DEV):
            dh = dh + _dot(dgu_ref[j], wT_ref[j])
        x_ = x_ref[...]
        _, rstd = _rms(x_, g_ref[...])
        dxn, dgr = _rms_bwd(x_, g_ref[...], rstd, dh)
        dx_ref[...] = dy_ref[...] + dxn
        dg_ref[...] += _rowsum(dgr)

    return pl.pallas_call(
        body, name="ffn_bwd_b", grid=(S // ts,),
        in_specs=[_row(ts, D), _row(ts, D), pl.BlockSpec((N_DEV, ts, FF_CHUNK), lambda i: (0, i, 0)),
                  _const(g.shape), _const(wguT.shape)],
        out_specs=[_row(ts, D), _acc((1, D))], out_shape=[_sds((S, D), F32), _sds((1, D), F32)],
        compiler_params=_cp())(x, dy, dgu, g, wguT)


def _conv_fwd(xprev, xbp, cw_ref, cb):
    ext = jnp.concatenate([xprev, xbp], axis=0)
    acc = cb + cw_ref[3:4, :] * xbp
    for k in range(3):
        acc = acc + cw_ref[k:k + 1, :] * _roll(ext, 3 - k, 0)[CONV_HALO:]
    return acc


def _gates(xb, keep, wr_ref, br, wi_ref, bi, lam):
    r = jax.nn.sigmoid(_blockdot(xb, wr_ref, LRU_HEADS, LRU_HEAD_DIM) + br)
    ig = jax.nn.sigmoid(_blockdot(xb, wi_ref, LRU_HEADS, LRU_HEAD_DIM) + bi)
    sp = _softplus(-lam)
    log_a = -LRU_C * r * sp
    a = jnp.exp(log_a)
    mult = jnp.sqrt(jnp.maximum(-_expm1(2.0 * log_a), 0.0))
    return r, ig, sp, a, mult


def odd_pre(x, keep, g, win, cw, cb, wr, br, wi, bi, lam):
    S = x.shape[0]
    ts = _tile_rows(S, 512)

    def body(x_ref, xp_ref, keep_ref, g_ref, win_ref, cw_ref, cb_ref, wr_ref, br_ref, wi_ref, bi_ref, lam_ref,
             z_ref, a_ref, b_ref):
        i = pl.program_id(0)
        h, _ = _rms(x_ref[...], g_ref[...])
        z = _dot(h, win_ref[...])
        z_ref[...] = z
        hp, _ = _rms(xp_ref[...], g_ref[...])
        xprev = _dot(hp, win_ref[:, D:]) * (i > 0).astype(F32)
        xb = _conv_fwd(xprev, z[:, D:], cw_ref, cb_ref[...])
        keep_ = keep_ref[...]
        _, ig, _, a, mult = _gates(xb, keep_, wr_ref, br_ref[...], wi_ref, bi_ref[...], lam_ref[...])
        a_ref[...] = a * keep_
        b_ref[...] = jnp.where(keep_ > 0.0, mult, 1.0) * (ig * xb)

    ins = [x, x, keep, g, win, cw, cb, wr, br, wi, bi, lam]
    return pl.pallas_call(
        body, name="odd_pre", grid=(S // ts,),
        in_specs=[_row(ts, D), _prev(CONV_HALO, D, ts), _row(ts, 1)] + [_const(v.shape) for v in ins[3:]],
        out_specs=[_row(ts, 2 * D), _row(ts, D), _row(ts, D)],
        out_shape=[_sds((S, 2 * D), F32), _sds((S, D), F32), _sds((S, D), F32)], compiler_params=_cp())(*ins)


def lru_scan(a, b, reverse=False):
    S = a.shape[0]
    ts = _tile_rows(S, 512)
    n = S // ts
    groups = ts // 8

    def body(a_ref, an_ref, b_ref, h_ref, carry_ref, ash_ref):
        i = pl.program_id(0)

        @pl.when(i == 0)
        def _():
            carry_ref[...] = jnp.zeros_like(carry_ref)

        rid = lax.broadcasted_iota(jnp.int32, (8, D), 0)
        if reverse:
            ext = jnp.concatenate([a_ref[...], an_ref[...] * (i > 0).astype(F32)], axis=0)
            ash_ref[...] = _roll(ext, -1, 0)[:ts]
        src = ash_ref if reverse else a_ref

        def group(j, carry):
            off = pl.multiple_of((groups - 1 - j if reverse else j) * 8, 8)
            a8, b8 = src[pl.ds(off, 8), :], b_ref[pl.ds(off, 8), :]
            for k in (1, 2, 4):
                inside = (rid < 8 - k) if reverse else (rid >= k)
                sh = -k if reverse else k
                a_sh = jnp.where(inside, _roll(a8, sh, 0), 1.0)
                b_sh = jnp.where(inside, _roll(b8, sh, 0), 0.0)
                b8 = a8 * b_sh + b8
                a8 = a8 * a_sh
            h8 = a8 * carry + b8
            h_ref[pl.ds(off, 8), :] = h8
            return h8[0:1, :] if reverse else h8[7:8, :]

        carry_ref[...] = lax.fori_loop(0, groups, group, carry_ref[...])

    if reverse:
        r = ts // 8
        tile = pl.BlockSpec((ts, D), lambda i: (n - 1 - i, 0))
        halo = pl.BlockSpec((8, D), lambda i: (jnp.minimum((n - i) * r, n * r - 1), 0))
    else:
        tile, halo = _row(ts, D), _prev(8, D, ts)
    return pl.pallas_call(
        body, name="lru_scan_rev" if reverse else "lru_scan", grid=(n,), in_specs=[tile, halo, tile],
        out_specs=tile, out_shape=_sds((S, D), F32),
        scratch_shapes=[pltpu.VMEM((1, D), F32), pltpu.VMEM((ts, D), F32)], compiler_params=_cp())(a, a, b)


def odd_post(x, z, hseq, wout):
    S = x.shape[0]
    ts = _tile_rows(S, 512)

    def body(x_ref, gate_ref, h_ref, w_ref, out_ref):
        gl, _ = _gelu(gate_ref[...])
        out_ref[...] = x_ref[...] + _dot(gl * h_ref[...], w_ref[...])

    return pl.pallas_call(
        body, name="odd_post", grid=(S // ts,),
        in_specs=[_row(ts, D), _row(ts, D), _row(ts, D), _const(wout.shape)],
        out_specs=_row(ts, D), out_shape=_sds((S, D), F32), compiler_params=_cp())(x, z, hseq, wout)


def odd_post_bwd(dy, z, hseq, woutT):
    S = dy.shape[0]
    ts = _tile_rows(S, 512)

    def body(dy_ref, gate_ref, h_ref, w_ref, y_ref, dgate_ref, dh_ref):
        gate, hs = gate_ref[...], h_ref[...]
        gl, t = _gelu(gate)
        y_ref[...] = (gl * hs).astype(BF16)
        dyy = _dot(dy_ref[...], w_ref[...])
        dgate_ref[...] = dyy * hs * _gelu_grad(gate, t)
        dh_ref[...] = dyy * gl

    return pl.pallas_call(
        body, name="odd_post_bwd", grid=(S // ts,),
        in_specs=[_row(ts, D), _row(ts, D), _row(ts, D), _const(woutT.shape)],
        out_specs=[_row(ts, D)] * 3, out_shape=[_sds((S, D), BF16), _sds((S, D), F32), _sds((S, D), F32)],
        compiler_params=_cp())(dy, z, hseq, woutT)


def odd_gates_bwd(z, lam_grad, hseq, keep, cw, cb, wr, wrT, br, wi, wiT, bi, lam):
    S = z.shape[0]
    ts = _tile_rows(S, 512)

    def body(xbp_ref, xbpp_ref, lg_ref, h_ref, hp_ref, keep_ref, cw_ref, cb_ref, wr_ref, wrT_ref, br_ref, wi_ref,
             wiT_ref, bi_ref, lam_ref, dxb_ref, dcb_ref, dbr_ref, dbi_ref, dlam_ref, dwr_ref, dwi_ref):
        i = pl.program_id(0)

        @pl.when(i == 0)
        def _():
            for ref in (dcb_ref, dbr_ref, dbi_ref, dlam_ref, dwr_ref, dwi_ref):
                ref[...] = jnp.zeros_like(ref)

        first = (i > 0).astype(F32)
        xb = _conv_fwd(xbpp_ref[...] * first, xbp_ref[...], cw_ref, cb_ref[...])
        keep_ = keep_ref[...]
        lam_ = lam_ref[...]
        r, ig, sp, a, mult = _gates(xb, keep_, wr_ref, br_ref[...], wi_ref, bi_ref[...], lam_)
        hs = h_ref[...]
        hprev = _roll(jnp.concatenate([hp_ref[...] * first, hs], axis=0), 1, 0)[CONV_HALO:]
        lg = lg_ref[...]
        da = lg * hprev * keep_
        ixb = ig * xb
        dmult = lg * ixb * keep_
        dixb = lg * jnp.where(keep_ > 0.0, mult, 1.0)
        dlog_a = da * a - dmult * jnp.where(mult > 0.0, a * a / mult, 0.0)
        dr = dlog_a * (-LRU_C * sp)
        dlam_ref[...] += _rowsum(dlog_a * (-LRU_C * r)) * (-jax.nn.sigmoid(-lam_))
        dpr = dr * r * (1.0 - r)
        dpi = dixb * xb * ig * (1.0 - ig)
        dbr_ref[...] += _rowsum(dpr)
        dbi_ref[...] += _rowsum(dpi)
        dxb = dixb * ig
        parts = []
        for h in range(LRU_HEADS):
            sl = slice(h * LRU_HEAD_DIM, (h + 1) * LRU_HEAD_DIM)
            dwr_ref[h] += _dot_tn(xb[:, sl], dpr[:, sl])
            dwi_ref[h] += _dot_tn(xb[:, sl], dpi[:, sl])
            parts.append(_dot(dpr[:, sl], wrT_ref[h]) + _dot(dpi[:, sl], wiT_ref[h]))
        dxb = dxb + jnp.concatenate(parts, axis=1)
        dxb_ref[...] = dxb
        dcb_ref[...] += _rowsum(dxb)

    ins = [z, z, lam_grad, hseq, hseq, keep, cw, cb, wr, wrT, br, wi, wiT, bi, lam]
    in_specs = [_row(ts, D, 1), _prev(CONV_HALO, D, ts, 1), _row(ts, D), _row(ts, D), _prev(CONV_HALO, D, ts),
                _row(ts, 1)] + [_const(v.shape) for v in ins[6:]]
    gshape = (LRU_HEADS, LRU_HEAD_DIM, LRU_HEAD_DIM)
    return pl.pallas_call(
        body, name="odd_gates_bwd", grid=(S // ts,), in_specs=in_specs,
        out_specs=[_row(ts, D)] + [_acc((1, D))] * 4 + [_acc(gshape)] * 2,
        out_shape=[_sds((S, D), F32)] + [_sds((1, D), F32)] * 4 + [_sds(gshape, F32)] * 2,
        compiler_params=_cp())(*ins)


def odd_pre_bwd(x, dy, z, dxb, dgate, g, cw, winT):
    S = x.shape[0]
    ts = _tile_rows(S, 512)
    n = S // ts

    def body(x_ref, dy_ref, xbp_ref, xbpp_ref, dxb_ref, dxbn_ref, dgate_ref, g_ref, cw_ref, winT_ref,
             dx_ref, h_ref, dz_ref, dcw_ref, dg_ref):
        i = pl.program_id(0)

        @pl.when(i == 0)
        def _():
            dcw_ref[...] = jnp.zeros_like(dcw_ref)
            dg_ref[...] = jnp.zeros_like(dg_ref)

        dxb = dxb_ref[...]
        extd = jnp.concatenate([dxb, dxbn_ref[...] * (i < n - 1).astype(F32)], axis=0)
        extx = jnp.concatenate([xbpp_ref[...] * (i > 0).astype(F32), xbp_ref[...]], axis=0)
        dxbp = cw_ref[3:4, :] * dxb
        dcw_ref[3:4, :] += _rowsum(dxb * xbp_ref[...])
        for k in range(3):
            dxbp = dxbp + cw_ref[k:k + 1, :] * _roll(extd, -(3 - k), 0)[:ts]
            dcw_ref[k:k + 1, :] += _rowsum(dxb * _roll(extx, 3 - k, 0)[CONV_HALO:])
        dz = jnp.concatenate([dgate_ref[...], dxbp], axis=1)
        dz_ref[...] = dz.astype(BF16)
        x_ = x_ref[...]
        h, rstd = _rms(x_, g_ref[...])
        h_ref[...] = h.astype(BF16)
        dxn, dgr = _rms_bwd(x_, g_ref[...], rstd, _dot(dz, winT_ref[...]))
        dx_ref[...] = dy_ref[...] + dxn
        dg_ref[...] += _rowsum(dgr)

    ins = [x, dy, z, z, dxb, dxb, dgate, g, cw, winT]
    in_specs = [_row(ts, D), _row(ts, D), _row(ts, D, 1), _prev(CONV_HALO, D, ts, 1), _row(ts, D),
                _next(CONV_HALO, D, ts, n), _row(ts, D)] + [_const(v.shape) for v in ins[7:]]
    return pl.pallas_call(
        body, name="odd_pre_bwd", grid=(n,), in_specs=in_specs,
        out_specs=[_row(ts, D), _row(ts, D), _row(ts, 2 * D), _acc((4, D)), _acc((1, D))],
        out_shape=[_sds((S, D), F32), _sds((S, D), BF16), _sds((S, 2 * D), BF16), _sds((4, D), F32),
                   _sds((1, D), F32)],
        compiler_params=_cp())(*ins)


def loss_head(x, target, g):
    S = x.shape[0]
    ts = _tile_rows(S, 512)

    def body(x_ref, t_ref, g_ref, dx_ref, dg_ref, loss_ref):
        @pl.when(pl.program_id(0) == 0)
        def _():
            dg_ref[...] = jnp.zeros_like(dg_ref)
            loss_ref[...] = jnp.zeros_like(loss_ref)

        x_ = x_ref[...]
        y, rstd = _rms(x_, g_ref[...])
        err = y - t_ref[...]
        loss_ref[...] += 0.5 * _rowsum(jnp.mean(err * err, axis=1, keepdims=True))
        dxn, dgr = _rms_bwd(x_, g_ref[...], rstd, err * (1.0 / D))
        dx_ref[...] = dxn
        dg_ref[...] += _rowsum(dgr)

    return pl.pallas_call(
        body, name="loss_head", grid=(S // ts,), in_specs=[_row(ts, D), _row(ts, D), _const(g.shape)],
        out_specs=[_row(ts, D), _acc((1, D)), _acc((1, 1))],
        out_shape=[_sds((S, D), F32), _sds((1, D), F32), _sds((1, 1), F32)], compiler_params=_cp())(x, target, g)


def even_post_bwd(dy, o, woT_pool, woT_att):
    S = dy.shape[0]
    ts = _tile_rows(S, 512)

    def body(dy_ref, o_ref, wp_ref, wa_ref, dyp_ref, do_ref, delta_ref):
        dy_ = dy_ref[...]
        dyp_ref[...] = _dot(dy_, wp_ref[...])
        do = _dot(dy_, wa_ref[...])
        do_ref[...] = do.astype(BF16)
        prod = do * o_ref[...]
        delta_ref[...] = jnp.concatenate(
            [jnp.broadcast_to(jnp.sum(prod[:, h * LANES:(h + 1) * LANES], axis=1, keepdims=True), (ts, LANES))
             for h in range(MLA_HEADS)], axis=1)

    return pl.pallas_call(
        body, name="even_post_bwd", grid=(S // ts,),
        in_specs=[_row(ts, D), _row(ts, D), _const(woT_pool.shape), _const(woT_att.shape)],
        out_specs=[_row(ts, POOL_DIM), _row(ts, D), _row(ts, D)],
        out_shape=[_sds((S, POOL_DIM), F32), _sds((S, D), BF16), _sds((S, D), F32)],
        compiler_params=_cp())(dy, o, woT_pool, woT_att)


def attn_bwd(qp, kp, vp, do, lse_row, delta_row, token=None):
    S = qp.shape[0]
    tk = _tile_rows(S, 512)
    nq = S // tk
    extra, extra_specs = _after(token)

    def body(q_ref, k_ref, v_ref, do_ref, lse_ref, delta_ref, *rest):
        dq_ref, dk_ref, dv_ref = rest[-3:]
        kj = pl.program_id(1)

        @pl.when(kj == 0)
        def _():
            dq_ref[...] = jnp.zeros_like(dq_ref)

        k, v = k_ref[...], v_ref[...]

        def block(qi, carry, masked):
            dk, dv = carry
            off = pl.multiple_of(qi * tk, tk)
            q = q_ref[pl.ds(off, tk), :]
            do_ = do_ref[pl.ds(off, tk), :]
            st = _dot_nt(k, q)
            if masked:
                row = lax.broadcasted_iota(jnp.int32, (tk, tk), 0)
                col = lax.broadcasted_iota(jnp.int32, (tk, tk), 1)
                st = jnp.where(col >= row, st, -1e30)
            pt = _exp2(st - lse_ref[qi])
            dv = dv + _dot(pt, do_)
            dst = (pt * (_dot_nt(v, do_) - delta_ref[qi])).astype(BF16)
            dk = dk + _dot(dst, q)
            dq_ref[pl.ds(off, tk), :] += _dot_tn(dst, k)
            return dk, dv

        zero = jnp.zeros((tk, LANES), F32)
        carry = block(kj, (zero, zero), True)
        dk, dv = _pair_loop(kj + 1, nq, lambda qi, c: block(qi, c, False), carry)
        dk_ref[...] = dk * LN_2
        dv_ref[...] = dv

    blk = pl.BlockSpec((tk, LANES), lambda h, j: (j, h))
    full = pl.BlockSpec((S, LANES), lambda h, j: (0, h))
    rowv = pl.BlockSpec((None, nq, 1, tk), lambda h, j: (h, 0, 0, 0))
    return pl.pallas_call(
        body, name="attn_bwd", grid=(MLA_HEADS, nq), in_specs=[full, blk, blk, full, rowv, rowv] + extra_specs,
        out_specs=[full, blk, blk], out_shape=[_sds((S, D), F32)] * 3, compiler_params=_cp2())(
            qp, kp, vp, do, lse_row, delta_row, *extra)


def even_pre_bwd(x, dy, z, dq, dk, dv, dyp, tabs, g, winT, pw, pwT, pscale, qg, wqT, kvg, wkT, wvT):
    S = x.shape[0]
    ts = _tile_rows(S, 512)
    n = S // ts

    def body(x_ref, dy_ref, z_ref, up_ref, dq_ref, dk_ref, dv_ref, dyp_ref, dypn_ref, c_ref, a_ref, b_ref,
             g_ref, winT_ref, pw_ref, pwT_ref, ps_ref, qg_ref, wqT_ref, kvg_ref, wkT_ref, wvT_ref,
             dx_ref, h_ref, dz_ref, dg_ref, dpw_ref, dps_ref, dqg_ref, dwq_ref, dkvg_ref, dwk_ref, dwv_ref):
        i = pl.program_id(0)

        @pl.when(i == 0)
        def _():
            for ref in (dg_ref, dpw_ref, dps_ref, dqg_ref, dwq_ref, dkvg_ref, dwk_ref, dwv_ref):
                ref[...] = jnp.zeros_like(ref)

        z = z_ref[...]
        c, a, b = c_ref[...], a_ref[...], b_ref[...]
        ps = ps_ref[...]
        u = z[:, :POOL_DIM]
        pooled = _pooled(up_ref[...] * (i > 0).astype(F32), u, i * ts)
        dyp_ = dyp_ref[...]
        dps_ref[...] += _rowsum(dyp_ * _blockdot(pooled, pw_ref, 4, LANES))
        ext = jnp.concatenate([dyp_, dypn_ref[...] * (i < n - 1).astype(F32)], axis=0) * ps
        for gidx in range(4):
            sl = slice(gidx * LANES, (gidx + 1) * LANES)
            dpw_ref[gidx] += _dot_tn(pooled[:, sl], ext[:ts, sl])
        dpooled = _blockdot(ext, pwT_ref, 4, LANES)
        dm = dpooled / _pool_cnt(i * ts, ts + POOL_HALO)
        du = _pool_windows(dm, -1)[:ts] - dpooled[:ts]
        cq = z[:, 512:768]
        cqn, rstd_q = _rms(cq, qg_ref[...])
        dqf = _rope_bwd(dq_ref[...] * ATTN_SCALE, c, a, b)
        dwq_ref[...] += _dot_tn(cqn, dqf)
        dcq, dqg_rows = _rms_bwd(cq, qg_ref[...], rstd_q, _dot(dqf, wqT_ref[...]))
        dqg_ref[...] += _rowsum(dqg_rows)
        ckv = z[:, 768:896]
        ckvn, rstd_kv = _rms(ckv, kvg_ref[...])
        dk_, dv_ = dk_ref[...], dv_ref[...]
        dwk_ref[...] += _dot_tn(ckvn, dk_)
        dwv_ref[...] += _dot_tn(ckvn, dv_)
        dckv, dkvg_rows = _rms_bwd(ckv, kvg_ref[...], rstd_kv, _dot(dk_, wkT_ref[...]) + _dot(dv_, wvT_ref[...]))
        dkvg_ref[...] += _rowsum(dkvg_rows)
        dkr = dk_[:, :LANES]
        for h in range(1, MLA_HEADS):
            dkr = dkr + dk_[:, h * LANES:(h + 1) * LANES]
        lane = lax.broadcasted_iota(jnp.int32, (ts, LANES), 1)
        dkr = jnp.where((lane >= 64) & (lane < 96), _rope_bwd(dkr, c, a, b), 0.0)
        dz = jnp.concatenate([du, dcq, dckv, dkr], axis=1)
        dz_ref[...] = dz.astype(BF16)
        x_ = x_ref[...]
        h, rstd = _rms(x_, g_ref[...])
        h_ref[...] = h.astype(BF16)
        dxn, dgr = _rms_bwd(x_, g_ref[...], rstd, _dot(dz, winT_ref[...]))
        dx_ref[...] = dy_ref[...] + dxn
        dg_ref[...] += _rowsum(dgr)

    ins = [x, dy, z, z, dq, dk, dv, dyp, dyp, *tabs, g, winT, pw, pwT, pscale, qg, wqT, kvg, wkT, wvT]
    in_specs = [_row(ts, D), _row(ts, D), _row(ts, D), _prev(POOL_HALO, POOL_DIM, ts), _row(ts, D), _row(ts, D),
                _row(ts, D), _row(ts, POOL_DIM), _next(POOL_HALO, POOL_DIM, ts, n), _row(ts, LANES),
                _row(ts, LANES), _row(ts, LANES)] + [_const(v.shape) for v in ins[12:]]
    acc_shapes = [(1, D), (4, LANES, LANES), (1, POOL_DIM), (1, Q_LORA), (Q_LORA, D), (1, KV_LORA), (KV_LORA, D),
                  (KV_LORA, D)]
    return pl.pallas_call(
        body, name="even_pre_bwd", grid=(n,), in_specs=in_specs,
        out_specs=[_row(ts, D)] * 3 + [_acc(s) for s in acc_shapes],
        out_shape=[_sds((S, D), F32), _sds((S, D), BF16), _sds((S, D), BF16)] + [_sds(s, F32) for s in acc_shapes],
        compiler_params=_cp())(*ins)


def _pick(n, options):
    for o in options:
        if n % o == 0:
            return o
    return n


def matmul_tn(name, a, b):
    out_dtype = BF16
    S = a.shape[-2]
    ts = _tile_rows(S, 2048)
    steps = S // ts

    def body(a_ref, b_ref, o_ref, acc_ref):
        s = pl.program_id(2)

        @pl.when(s == 0)
        def _():
            acc_ref[...] = jnp.zeros_like(acc_ref)

        acc_ref[...] += _dot_tn(a_ref[...], b_ref[...])

        @pl.when(s == steps - 1)
        def _():
            o_ref[...] = acc_ref[...].astype(o_ref.dtype)

    if a.ndim == 3:
        C, _, K = a.shape
        N = b.shape[1]
        tn = _pick(N, (512, 256, 128))
        grid = (C, N // tn, S // ts)
        in_specs = [pl.BlockSpec((None, ts, K), lambda c, j, s: (c, s, 0)),
                    pl.BlockSpec((ts, tn), lambda c, j, s: (s, j))]
        out_spec, out_shape, tile = pl.BlockSpec((None, K, tn), lambda c, j, s: (c, 0, j)), (C, K, N), (K, tn)
    elif b.ndim == 3:
        C, _, N = b.shape
        K = a.shape[1]
        tk = _pick(K, (512, 256, 128))
        grid = (C, K // tk, S // ts)
        in_specs = [pl.BlockSpec((ts, tk), lambda c, i, s: (s, i)),
                    pl.BlockSpec((None, ts, N), lambda c, i, s: (c, s, 0))]
        out_spec, out_shape, tile = pl.BlockSpec((None, tk, N), lambda c, i, s: (c, i, 0)), (C, K, N), (tk, N)
    else:
        K, N = a.shape[1], b.shape[1]
        tk = _pick(K, (512, 256, 128))
        tn = _pick(N, (512, 256, 128))
        grid = (K // tk, N // tn, S // ts)
        in_specs = [pl.BlockSpec((ts, tk), lambda i, j, s: (s, i)), pl.BlockSpec((ts, tn), lambda i, j, s: (s, j))]
        out_spec, out_shape, tile = pl.BlockSpec((tk, tn), lambda i, j, s: (i, j)), (K, N), (tk, tn)
    return pl.pallas_call(
        body, name=name, grid=grid, in_specs=in_specs, out_specs=out_spec, out_shape=_sds(out_shape, out_dtype),
        scratch_shapes=[pltpu.VMEM(tile, F32)], compiler_params=pltpu.CompilerParams(dimension_semantics=("arbitrary",) * 3, vmem_limit_bytes=VMEM_LIMIT))(
            a, b)


def _my_id():
    return lax.axis_index("x") * 4 + lax.axis_index("y") * 2 + lax.axis_index("c")


def _peer(j):
    x, y, c = lax.axis_index("x"), lax.axis_index("y"), lax.axis_index("c")
    px = 1 - x if j & 4 else x
    py = 1 - y if j & 2 else y
    pc = 1 - c if j & 1 else c
    return (px, py, pc), px * 4 + py * 2 + pc


def all_gather(name, arrays):
    n = len(arrays)

    def body(*refs):
        ins, outs = refs[:n], refs[n:2 * n]
        send_sems, recv_sems, local_sems = refs[2 * n:]
        me = _my_id()
        local = [pltpu.make_async_copy(ins[k], outs[k].at[me], local_sems.at[k]) for k in range(n)]
        for cp in local:
            cp.start()
        sends = []
        for j in range(1, N_DEV):
            peer, _ = _peer(j)
            for k in range(n):
                cp = pltpu.make_async_remote_copy(
                    src_ref=ins[k], dst_ref=outs[k].at[me], send_sem=send_sems.at[k, j - 1],
                    recv_sem=recv_sems.at[k, j - 1], device_id=peer, device_id_type=pl.DeviceIdType.MESH)
                cp.start()
                sends.append(cp)
        for j in range(1, N_DEV):
            peer, pid = _peer(j)
            for k in range(n):
                pltpu.make_async_remote_copy(
                    src_ref=ins[k], dst_ref=outs[k].at[pid], send_sem=send_sems.at[k, j - 1],
                    recv_sem=recv_sems.at[k, j - 1], device_id=peer, device_id_type=pl.DeviceIdType.MESH).wait_recv()
        for cp in sends:
            cp.wait_send()
        for cp in local:
            cp.wait()

    any_spec = pl.BlockSpec(memory_space=pl.ANY)
    return pl.pallas_call(
        body, name=name, in_specs=[any_spec] * n, out_specs=[any_spec] * n,
        out_shape=[_sds((N_DEV,) + a.shape, a.dtype) for a in arrays],
        scratch_shapes=[pltpu.SemaphoreType.DMA((n, N_DEV - 1)), pltpu.SemaphoreType.DMA((n, N_DEV - 1)),
                        pltpu.SemaphoreType.DMA((n,))],
        compiler_params=pltpu.CompilerParams(has_side_effects=True))(*arrays)


def exchange(name, arrays):
    n = len(arrays)

    def body(*refs):
        ins, outs = refs[:n], refs[n:2 * n]
        send_sems, recv_sems, local_sems = refs[2 * n:]
        me = _my_id()
        local = [pltpu.make_async_copy(ins[k].at[me], outs[k].at[me], local_sems.at[k]) for k in range(n)]
        for cp in local:
            cp.start()
        sends = []
        for j in range(1, N_DEV):
            peer, pid = _peer(j)
            for k in range(n):
                cp = pltpu.make_async_remote_copy(
                    src_ref=ins[k].at[pid], dst_ref=outs[k].at[me], send_sem=send_sems.at[k, j - 1],
                    recv_sem=recv_sems.at[k, j - 1], device_id=peer, device_id_type=pl.DeviceIdType.MESH)
                cp.start()
                sends.append(cp)
        for j in range(1, N_DEV):
            peer, pid = _peer(j)
            for k in range(n):
                pltpu.make_async_remote_copy(
                    src_ref=ins[k].at[me], dst_ref=outs[k].at[pid], send_sem=send_sems.at[k, j - 1],
                    recv_sem=recv_sems.at[k, j - 1], device_id=peer, device_id_type=pl.DeviceIdType.MESH).wait_recv()
        for cp in sends:
            cp.wait_send()
        for cp in local:
            cp.wait()

    any_spec = pl.BlockSpec(memory_space=pl.ANY)
    return pl.pallas_call(
        body, name=name, in_specs=[any_spec] * n, out_specs=[any_spec] * n,
        out_shape=[_sds(a.shape, a.dtype) for a in arrays],
        scratch_shapes=[pltpu.SemaphoreType.DMA((n, N_DEV - 1)), pltpu.SemaphoreType.DMA((n, N_DEV - 1)),
                        pltpu.SemaphoreType.DMA((n,))],
        compiler_params=pltpu.CompilerParams(has_side_effects=True))(*arrays)


_HBM = pl.BlockSpec(memory_space=pltpu.HBM)
_SEM = pl.BlockSpec(memory_space=pltpu.SEMAPHORE)
_DATAFLOW = pltpu.SideEffectType.DATAFLOW_SIDE_EFFECTING


def _in_hbm(v):
    return pltpu.with_memory_space_constraint(v, pltpu.HBM)


N_PEERS = N_DEV - 1


def _split_copy(k, j, srcs, lands, send_sems, recv_sems, gather, slot):
    peer, pid = _peer(j)
    return pltpu.make_async_remote_copy(
        src_ref=srcs[k] if gather else srcs[k].at[pid], dst_ref=lands[k].at[_my_id() if slot == "mine" else pid],
        send_sem=send_sems[j - 1], recv_sem=recv_sems[j - 1], device_id=peer, device_id_type=pl.DeviceIdType.MESH)


def split_start(name, arrays, gather):
    n = len(arrays)
    lands = [lax.empty((N_DEV,) + a.shape if gather else a.shape, a.dtype) for a in arrays]

    def body(*refs):
        srcs, lnds = refs[:n], refs[n:2 * n]
        sems = refs[4 * n:4 * n + 2 * N_PEERS]
        token = refs[-1]
        for j in range(1, N_DEV):
            for k in range(n):
                _split_copy(k, j, srcs, lnds, sems[:N_PEERS], sems[N_PEERS:], gather, "mine").start()
        token[...] = jnp.zeros_like(token)

    out = pl.pallas_call(
        body, name=name,
        out_shape=(*[pltpu.HBM(a.shape, a.dtype) for a in arrays], *[pltpu.HBM(l.shape, l.dtype) for l in lands],
                   *[pltpu.SemaphoreType.DMA(())] * (2 * N_PEERS), _sds((8, LANES), F32)),
        in_specs=[_HBM] * (2 * n),
        out_specs=(*[_HBM] * (2 * n), *[_SEM] * (2 * N_PEERS), pl.BlockSpec(memory_space=pltpu.VMEM)),
        input_output_aliases={k: k for k in range(2 * n)},
        compiler_params=pltpu.CompilerParams(has_side_effects=_DATAFLOW))(
            *[_in_hbm(a) for a in arrays], *[_in_hbm(l) for l in lands])
    sems = list(out[2 * n:2 * n + 2 * N_PEERS])
    return sems[:N_PEERS], sems[N_PEERS:], list(out[:n]), list(out[n:2 * n]), out[-1]


def split_wait(name, handle, after, gather):
    send_sems, recv_sems, srcs, lands, _ = handle
    n = len(srcs)

    def body(*refs):
        srcs_r, lnds_r = refs[:n], refs[n:2 * n]
        sems = refs[2 * n:2 * n + 2 * N_PEERS]
        for j in range(1, N_DEV):
            for k in range(n):
                cp = _split_copy(k, j, srcs_r, lnds_r, sems[:N_PEERS], sems[N_PEERS:], gather, "peer")
                cp.wait_send()
                cp.wait_recv()

    out = pl.pallas_call(
        body, name=name, out_shape=tuple(pltpu.HBM(a.shape, a.dtype) for a in srcs + lands),
        in_specs=[_HBM] * (2 * n) + [_SEM] * (2 * N_PEERS) + [pl.BlockSpec(memory_space=pl.ANY)],
        out_specs=tuple([_HBM] * (2 * n)), input_output_aliases={k: k for k in range(2 * n)},
        compiler_params=pltpu.CompilerParams(has_side_effects=_DATAFLOW))(
            *srcs, *lands, *send_sems, *recv_sems, after)
    return list(out[:n]), list(out[n:])


def _fill_own_slot(src, land, gather):
    me = _my_id()
    own = src[None] if gather else lax.dynamic_index_in_dim(src, me, 0, keepdims=True)
    return lax.dynamic_update_slice_in_dim(land, own, me, 0)


ADAMW_BLOCK_ELEMS = 128 * 1024


def adamw(name, parts, w, m, v):
    R, C = w.shape
    tr = _pick(R, [t for t in (512, 256, 128, 64, 32, 16, 8) if t * C <= ADAMW_BLOCK_ELEMS])
    c1 = 1.0 - ADAM_B1 ** ADAM_STEP
    c2 = 1.0 - ADAM_B2 ** ADAM_STEP

    def body(p_ref, w_ref, m_ref, v_ref, g_ref, d_ref, nm_ref, nv_ref):
        g = p_ref[0].astype(F32)
        for s in range(1, N_DEV):
            g = g + p_ref[s].astype(F32)
        g_ref[...] = g
        m_ = ADAM_B1 * m_ref[...] + (1.0 - ADAM_B1) * g
        v_ = ADAM_B2 * v_ref[...] + (1.0 - ADAM_B2) * (g * g)
        nm_ref[...] = m_
        nv_ref[...] = v_
        d_ref[...] = -ADAM_LR * ((m_ / c1) / (jnp.sqrt(v_ / c2) + ADAM_EPS) + ADAM_WD * w_ref[...])

    row = pl.BlockSpec((tr, C), lambda i: (i, 0))
    return pl.pallas_call(
        body, name=name, grid=(R // tr,),
        in_specs=[pl.BlockSpec((N_DEV, tr, C), lambda i: (0, i, 0)), row, row, row], out_specs=[row] * 4,
        out_shape=[_sds((R, C), F32)] * 4, compiler_params=_cp())(parts, w, m, v)


WEIGHTS = ['ev_norm', 'ev_w_in', 'ev_pool_w', 'ev_pool_scale', 'ev_q_norm', 'ev_w_q_up', 'ev_kv_norm', 'ev_w_kv_up',
           'ev_w_out', 'od_norm', 'od_w_in', 'od_conv_w', 'od_conv_b', 'od_w_rgate', 'od_b_rgate', 'od_w_igate',
           'od_b_igate', 'od_lambda', 'od_w_out', 'xa_norm_x', 'xa_norm_mem', 'xa_w_q', 'xa_w_kv', 'xa_w_o',
           'ffn_norm', 'ffn_w_gate_up', 'ffn_w_down', 'final_norm']
SHARD_AXIS = {'ev_w_in': 1, 'ev_w_q_up': 2, 'ev_w_kv_up': 2, 'ev_w_out': 1, 'od_norm': 1, 'od_w_in': 2,
              'od_conv_w': 2, 'od_conv_b': 1, 'od_w_rgate': 2, 'od_b_rgate': 1, 'od_w_igate': 2, 'od_b_igate': 1,
              'od_lambda': 1, 'od_w_out': 1, 'xa_w_q': 1, 'xa_w_kv': 2, 'xa_w_o': 1, 'ffn_w_gate_up': 2,
              'ffn_w_down': 1}
SMALL_F32 = ('od_norm', 'od_conv_w', 'od_conv_b', 'od_b_rgate', 'od_b_igate', 'od_lambda')
STACKED = ('ffn_w_gate_up', 'ffn_w_down')
SHARDED = [n for n in WEIGHTS if n in SHARD_AXIS]
REPLICATED = [n for n in WEIGHTS if n not in SHARD_AXIS]
ROW_ALIGN = 512


def _pack(flats, dtype):
    v = jnp.concatenate([f.reshape(-1).astype(dtype) for f in flats])
    pad = (-v.shape[0]) % (ROW_ALIGN * LANES)
    return jnp.pad(v, (0, pad)).reshape(-1, LANES)


def _rows8(n_elems):
    return -(-n_elems // (8 * LANES)) * 8


def _pack_rows(arrays, lead=False):
    out = []
    for a in arrays:
        r = a.reshape((N_DEV, -1, LANES) if lead else (-1, LANES))
        pad = _rows8(r.shape[-2] * LANES) - r.shape[-2]
        out.append(jnp.pad(r, [(0, 0)] * (r.ndim - 2) + [(0, pad), (0, 0)]))
    return jnp.concatenate(out, axis=-2)


def _unpack_rows(buf, shapes, lead=False):
    out, off = [], 0
    for s in shapes:
        n = 1
        for d in s:
            n *= d
        rows = buf[..., off:off + n // LANES, :]
        out.append(rows.reshape(((N_DEV,) if lead else ()) + tuple(s)))
        off += _rows8(n)
    return out


def _unpack(flat, shapes):
    out, off = [], 0
    v = flat.reshape(-1)
    for s in shapes:
        n = 1
        for d in s:
            n *= d
        out.append(v[off:off + n].reshape(s))
        off += n
    return out


def _to_full(stacked, axis):
    v = jnp.moveaxis(stacked, 0, axis)
    s = v.shape
    return v.reshape(s[:axis] + (s[axis] * s[axis + 1],) + s[axis + 2:])


def _to_shards(full, axis):
    s = full.shape
    v = full.reshape(s[:axis] + (N_DEV, s[axis] // N_DEV) + s[axis + 1:])
    return jnp.moveaxis(v, axis, 0)


def _pad_heads(w, nh, dh, lead):
    s = w.shape
    v = w.reshape(s[:-1] + (nh, dh))
    v = jnp.pad(v, [(0, 0)] * (len(s) - 1) + [(0, 0), (lead, LANES - dh - lead)])
    return v.reshape(s[:-1] + (nh * LANES,))


def _unpad_heads(w, nh, dh, lead):
    s = w.shape
    return w.reshape(s[:-1] + (nh, LANES))[..., lead:lead + dh].reshape(s[:-1] + (nh * dh,))


def _rope_tables(positions):
    inv_freq = 10000.0 ** (-jnp.arange(0, 32, 2, dtype=F32) / 32)
    ang = positions.astype(F32)[:, None] * inv_freq
    cos, sin = jnp.cos(ang), jnp.sin(ang)
    S = positions.shape[0]
    one, zero = jnp.ones((S, 64), F32), jnp.zeros((S, 64), F32)
    z16, z32 = jnp.zeros((S, 16), F32), jnp.zeros((S, 32), F32)
    c = jnp.concatenate([one, cos, cos, jnp.ones((S, 32), F32)], axis=1)
    a = jnp.concatenate([zero, z16, sin, z32], axis=1)
    b = jnp.concatenate([zero, -sin, z16, z32], axis=1)
    return c, a, b


def _t(w):
    return jnp.swapaxes(w, -1, -2)


def _col_to_row(v, tq):
    S = v.shape[0]
    return v[:, ::LANES].T.reshape(MLA_HEADS, S // tq, 1, tq)


def device_step(x, mem, positions, target, W, fwd_token=None, late_weights=None, ship_grads=None):
    S = x.shape[0]
    G = {}
    tabs = _rope_tables(positions)
    keep = (positions != 0).astype(F32)[:, None]
    row = lambda v: v.reshape(1, -1)

    w_in = W['ev_w_in'][0]
    ev_win = jnp.concatenate([w_in[:, :896], _pad_heads(w_in[:, 896:], 1, 32, 64)], axis=1)
    ev_wq = _pad_heads(W['ev_w_q_up'][0], MLA_HEADS, QK_DIM, 0)
    kvw = W['ev_w_kv_up'][0].reshape(KV_LORA, MLA_HEADS, 128)
    ev_wk = _pad_heads(kvw[:, :, :64].reshape(KV_LORA, 512), MLA_HEADS, 64, 0)
    ev_wv = _pad_heads(kvw[:, :, 64:].reshape(KV_LORA, 512), MLA_HEADS, 64, 0)
    ev_wo_pool = W['ev_w_out'][0][:POOL_DIM]
    ev_wo_att = _t(_pad_heads(_t(W['ev_w_out'][0][POOL_DIM:]), MLA_HEADS, 64, 0))
    pw = W['ev_pool_w'][0].astype(BF16)
    ev_g, ps, qg, kvg = row(W['ev_norm'][0]), row(W['ev_pool_scale'][0]), row(W['ev_q_norm'][0]), row(W['ev_kv_norm'][0])

    z0, qp, kp, vp, ypool = even_pre(x, tabs, ev_g, ev_win, pw, ps, qg, ev_wq, kvg, ev_wk, ev_wv)
    o_att, lse = attn_fwd(qp, kp, vp, fwd_token)
    if late_weights is not None:
        W = {**W, **late_weights(lse)}
    x1 = even_post(x, ypool, o_att, ev_wo_pool, ev_wo_att)

    def xa_ffn_fwd(xin, l):
        mn, km, vm = mem_kv(mem, row(W['xa_norm_mem'][l]), W['xa_w_kv'][l])
        xm = xattn_fwd(xin, row(W['xa_norm_x'][l]), W['xa_w_q'][l], km, vm, W['xa_w_o'][l])
        xo = ffn_fwd(xm, row(W['ffn_norm'][l]), W['ffn_w_gate_up'][:, l],
                     W['ffn_w_down'][:, l].reshape(FF_HALF, FF_CHUNK, D))
        return xm, xo, (mn, km, vm)

    x2, x3, memkv0 = xa_ffn_fwd(x1, 0)

    od_g, lam = row(W['od_norm'][0]), row(W['od_lambda'][0])
    cw, cb = W['od_conv_w'][0], row(W['od_conv_b'][0])
    wr, wi = W['od_w_rgate'][0], W['od_w_igate'][0]
    br, bi = row(W['od_b_rgate'][0]), row(W['od_b_igate'][0])
    z1, a_t, b_t = odd_pre(x3, keep, od_g, W['od_w_in'][0], cw, cb, wr, br, wi, bi, lam)
    hseq = lru_scan(a_t, b_t)
    x4 = odd_post(x3, z1, hseq, W['od_w_out'][0])
    x5, x6, memkv1 = xa_ffn_fwd(x4, 1)

    dx, G['final_norm'], loss = loss_head(x6, target, row(W['final_norm']))
    G['final_norm'] = G['final_norm'].reshape(D)

    gnx, gnm, gwq, gwkv, gwo, gfn, gwgu, gwd = ([None, None] for _ in range(8))

    def xa_ffn_bwd(dy, xin, xm, memkv, l):
        mn, km, vm = memkv
        fg = row(W['ffn_norm'][l])
        wgu = W['ffn_w_gate_up'][:, l]
        hf, act, dgu = ffn_bwd_a(xm, dy, fg, wgu, _t(W['ffn_w_down'][:, l].reshape(FF_HALF, FF_CHUNK, D)))
        gwd[l] = matmul_tn("ffn_dwd", act, dy).reshape(N_DEV, D_FF // N_DEV, D)
        gwgu[l] = matmul_tn("ffn_dwgu", hf, dgu)
        dxm, dfg = ffn_bwd_b(xm, dy, dgu, fg, _t(wgu))
        gfn[l] = dfg[0]
        dxin, o, dq, hx, dgx, dk, dv = xattn_bwd(xin, dxm, row(W['xa_norm_x'][l]), W['xa_w_q'][l],
                                                  _t(W['xa_w_q'][l]), km, vm, _t(W['xa_w_o'][l]))
        gnx[l] = dgx[0]
        gwo[l] = matmul_tn("xa_dwo", o, dxm)
        gwq[l] = matmul_tn("xa_dwq", hx, dq)
        dkv, dgm = mem_bwd(mem, row(W['xa_norm_mem'][l]), dk, dv, _t(W['xa_w_kv'][l]))
        gnm[l] = dgm[0]
        gwkv[l] = matmul_tn("xa_dwkv", mn, dkv)
        return dxin

    dx4 = xa_ffn_bwd(dx, x4, x5, memkv1, 1)

    y_od, dgate, dhs = odd_post_bwd(dx4, z1, hseq, _t(W['od_w_out'][0]))
    G['od_w_out'] = matmul_tn("od_dwout", y_od, dx4)[None]
    lam_grad = lru_scan(a_t, dhs, reverse=True)
    dxb, dcb, dbr, dbi, dlam, dwr, dwi = odd_gates_bwd(z1, lam_grad, hseq, keep, cw, cb, wr, _t(wr), br, wi, _t(wi),
                                                        bi, lam)
    dx3, h_od, dz1, dcw, dg_od = odd_pre_bwd(x3, dx4, z1, dxb, dgate, od_g, cw, _t(W['od_w_in'][0]))
    G['od_w_in'] = matmul_tn("od_dwin", h_od, dz1)[None]
    G['od_norm'], G['od_conv_w'], G['od_conv_b'] = dg_od, dcw[None], dcb
    G['od_w_rgate'], G['od_b_rgate'], G['od_w_igate'], G['od_b_igate'], G['od_lambda'] = (
        dwr[None], dbr, dwi[None], dbi, dlam)

    dx1 = xa_ffn_bwd(dx3, x1, x2, memkv0, 0)
    G['xa_norm_x'], G['xa_norm_mem'], G['ffn_norm'] = jnp.stack(gnx), jnp.stack(gnm), jnp.stack(gfn)
    G['xa_w_q'], G['xa_w_kv'], G['xa_w_o'] = jnp.stack(gwq), jnp.stack(gwkv), jnp.stack(gwo)
    G['ffn_w_gate_up'], G['ffn_w_down'] = jnp.stack(gwgu, axis=1), jnp.stack(gwd, axis=1)
    bwd_token = ship_grads(G) if ship_grads is not None else None

    dyp, do_att, delta = even_post_bwd(dx1, o_att, _t(ev_wo_pool), _t(ev_wo_att))
    g_wo_pool = matmul_tn("ev_dwo_pool", ypool, dx1)
    g_wo_att = matmul_tn("ev_dwo_att", o_att, dx1)
    G['ev_w_out'] = jnp.concatenate([g_wo_pool, _t(_unpad_heads(_t(g_wo_att), MLA_HEADS, 64, 0))], axis=0)[None]
    tq = _tile_rows(S, 512)
    dq, dk, dv = attn_bwd(qp, kp, vp, do_att, _col_to_row(lse, tq), _col_to_row(delta, tq), bwd_token)
    (grad_x, h_ev, dz0, dg_ev, dpw, dps, dqg, dwq, dkvg, dwk, dwv) = even_pre_bwd(
        x, dx1, z0, dq, dk, dv, dyp, tabs, ev_g, _t(ev_win), pw, _t(pw), ps, qg, _t(ev_wq), kvg, _t(ev_wk),
        _t(ev_wv))
    g_win = matmul_tn("ev_dwin", h_ev, dz0)
    G['ev_w_in'] = jnp.concatenate([g_win[:, :896], _unpad_heads(g_win[:, 896:], 1, 32, 64)], axis=1)[None]
    G['ev_norm'], G['ev_pool_w'], G['ev_pool_scale'], G['ev_q_norm'], G['ev_kv_norm'] = (
        dg_ev, dpw[None], dps, dqg, dkvg)
    G['ev_w_q_up'] = _unpad_heads(dwq, MLA_HEADS, QK_DIM, 0)[None]
    gk = _unpad_heads(dwk, MLA_HEADS, 64, 0).reshape(KV_LORA, MLA_HEADS, 64)
    gv = _unpad_heads(dwv, MLA_HEADS, 64, 0).reshape(KV_LORA, MLA_HEADS, 64)
    G['ev_w_kv_up'] = jnp.concatenate([gk, gv], axis=2).reshape(1, KV_LORA, MLA_HEADS * 128)
    return loss[0, 0], grad_x, G


def kernel(x, mem, positions, ev_norm, ev_w_in, ev_pool_w, ev_pool_scale, ev_q_norm, ev_w_q_up, ev_kv_norm, ev_w_kv_up, ev_w_out, od_norm, od_w_in, od_conv_w, od_conv_b, od_w_rgate, od_b_rgate, od_w_igate, od_b_igate, od_lambda, od_w_out, xa_norm_x, xa_norm_mem, xa_w_q, xa_w_kv, xa_w_o, ffn_norm, ffn_w_gate_up, ffn_w_down, final_norm, loss_target, m_ev_norm, m_ev_w_in, m_ev_pool_w, m_ev_pool_scale, m_ev_q_norm, m_ev_w_q_up, m_ev_kv_norm, m_ev_w_kv_up, m_ev_w_out, m_od_norm, m_od_w_in, m_od_conv_w, m_od_conv_b, m_od_w_rgate, m_od_b_rgate, m_od_w_igate, m_od_b_igate, m_od_lambda, m_od_w_out, m_xa_norm_x, m_xa_norm_mem, m_xa_w_q, m_xa_w_kv, m_xa_w_o, m_ffn_norm, m_ffn_w_gate_up, m_ffn_w_down, m_final_norm, v_ev_norm, v_ev_w_in, v_ev_pool_w, v_ev_pool_scale, v_ev_q_norm, v_ev_w_q_up, v_ev_kv_norm, v_ev_w_kv_up, v_ev_w_out, v_od_norm, v_od_w_in, v_od_conv_w, v_od_conv_b, v_od_w_rgate, v_od_b_rgate, v_od_w_igate, v_od_b_igate, v_od_lambda, v_od_w_out, v_xa_norm_x, v_xa_norm_mem, v_xa_w_q, v_xa_w_kv, v_xa_w_o, v_ffn_norm, v_ffn_w_gate_up, v_ffn_w_down, v_final_norm):
    args = dict(locals())
    w = {n: args[n] for n in WEIGHTS}
    m = {n: args['m_' + n] for n in WEIGHTS}
    v = {n: args['v_' + n] for n in WEIGHTS}
    big = [n for n in SHARDED if n not in SMALL_F32]
    small = [n for n in SHARDED if n in SMALL_F32]

    small_shapes = [w[n].shape for n in small]
    first = [n for n in big if n.startswith('ev_')]
    late = [n for n in big if n not in first]

    def full(n, st):
        return st if n in STACKED else _to_full(st, SHARD_AXIS[n])

    W = {n: w[n] for n in REPLICATED}
    W.update((n, full(n, st)) for n, st in zip(first, all_gather("gather_ev_weights", [w[n].astype(BF16) for n in first])))
    gather = split_start("gather_start", [w[n].astype(BF16) for n in late] + [_pack_rows([w[n] for n in small])], True)

    def late_weights(after):
        srcs, lands = split_wait("gather_wait", gather, after, True)
        lands = [_fill_own_slot(s, l, True) for s, l in zip(srcs, lands)]
        out = {n: full(n, st) for n, st in zip(late, lands)}
        out.update((n, _to_full(st, SHARD_AXIS[n])) for n, st in zip(small, _unpack_rows(lands[-1], small_shapes, True)))
        return out

    def shards(G, n):
        return G[n] if n in STACKED else _to_shards(G[n], SHARD_AXIS[n])

    shipped = []

    def ship_grads(G):
        shipped.append(split_start("exchange_start", [shards(G, n).astype(BF16) for n in late] +
                                   [_pack_rows([shards(G, n) for n in small], lead=True)], False))
        return shipped[0][-1]

    loss, grad_x, G = device_step(x[0], mem[0], positions[0], loss_target[0], W, gather[-1], late_weights, ship_grads)
    outs = [{}, {}, {}, {}]

    rep_shapes = [w[n].shape for n in REPLICATED] + [(LANES,)]
    zero = jnp.zeros((LANES,), F32)
    rep_parts, = all_gather("gather_rep_grads", [_pack(
        [G[n] for n in REPLICATED] + [jnp.broadcast_to(loss, (LANES,))], F32)])
    rep = adamw("adamw_rep", rep_parts, *[_pack([d[n] for n in REPLICATED] + [zero], F32) for d in (w, m, v)])
    for k in range(4):
        outs[k].update(zip(REPLICATED + ['loss'], _unpack(rep[k], rep_shapes)))
    loss = outs[0]['loss'][0]

    srcs, lands = split_wait("exchange_wait", shipped[0], grad_x, False)
    late_parts = [_fill_own_slot(s, l, False) for s, l in zip(srcs, lands)]
    parts = list(exchange("exchange_ev_grads", [shards(G, n).astype(BF16) for n in first])) + late_parts
    two_d = lambda a: a.reshape(-1, a.shape[-1])
    for n, p in zip(first + late, parts):
        res = adamw("adamw_" + n, p.reshape((N_DEV,) + two_d(w[n]).shape), two_d(w[n]), two_d(m[n]), two_d(v[n]))
        for k in range(4):
            outs[k][n] = res[k].reshape(w[n].shape)
    res = adamw("adamw_small", parts[-1], *[_pack_rows([d[n] for n in small]) for d in (w, m, v)])
    for k in range(4):
        outs[k].update(zip(small, _unpack_rows(res[k], small_shapes)))

    return (loss, grad_x[None], *[outs[0][n] for n in WEIGHTS], *[outs[1][n] for n in WEIGHTS],
            *[outs[2][n] for n in WEIGHTS], *[outs[3][n] for n in WEIGHTS])
```

```python
import functools

import jax
import jax.numpy as jnp
from jax import lax
from jax.experimental import pallas as pl
from jax.experimental.pallas import tpu as pltpu

F32, BF16 = jnp.float32, jnp.bfloat16
N_DEV = 8
D = 1024
POOL_DIM = 512
POOL_WINDOWS = (2, 4, 8, 16)
MLA_HEADS = 8
QK_DIM = 96
Q_LORA, KV_LORA = 256, 128
LRU_HEADS, LRU_HEAD_DIM = 4, 256
LRU_C = 8.0
MEM_HEADS, MEM_HEAD_DIM = 4, 256
D_FF = 2816
RMS_EPS = 1e-6
ADAM_LR, ADAM_B1, ADAM_B2, ADAM_EPS, ADAM_WD, ADAM_STEP = 0.001, 0.9, 0.999, 1e-08, 0.01, 10
LANES = 128
POOL_HALO = 16
CONV_HALO = 8
VMEM_LIMIT = 60000 * 1024


def _cp():
    return pltpu.CompilerParams(dimension_semantics=("arbitrary",), vmem_limit_bytes=VMEM_LIMIT)


def _cp2():
    return pltpu.CompilerParams(dimension_semantics=("arbitrary", "arbitrary"), vmem_limit_bytes=VMEM_LIMIT)


def _row(ts, c, col=0):
    return pl.BlockSpec((ts, c), lambda i: (i, col))


def _prev(hr, c, ts, col=0):
    r = ts // hr
    return pl.BlockSpec((hr, c), lambda i: (jnp.maximum(i * r - 1, 0), col))


def _next(hr, c, ts, n, col=0):
    r = ts // hr
    return pl.BlockSpec((hr, c), lambda i: (jnp.minimum((i + 1) * r, n * r - 1), col))


def _const(shape):
    nd = len(shape)
    return pl.BlockSpec(tuple(shape), lambda i: (0,) * nd, pipeline_mode=pl.Buffered(1))


def _acc(shape):
    nd = len(shape)
    return pl.BlockSpec(tuple(shape), lambda i: (0,) * nd)


def _sds(shape, dt):
    return jax.ShapeDtypeStruct(tuple(shape), dt)


def _dot(a, b):
    return jnp.dot(a.astype(BF16), b.astype(BF16), preferred_element_type=F32)


def _dot_nt(a, b):
    return lax.dot_general(a.astype(BF16), b.astype(BF16), (((1,), (1,)), ((), ())), preferred_element_type=F32)


def _dot_tn(a, b):
    return lax.dot_general(a.astype(BF16), b.astype(BF16), (((0,), (0,)), ((), ())), preferred_element_type=F32)


def _rms(x, g):
    rstd = lax.rsqrt(jnp.mean(x * x, axis=-1, keepdims=True) + RMS_EPS)
    return x * rstd * g, rstd


def _rms_bwd(x, g, rstd, dy):
    xn = x * rstd
    dyg = dy * g
    dx = rstd * (dyg - xn * jnp.mean(dyg * xn, axis=-1, keepdims=True))
    return dx, dy * xn


def _rowsum(v):
    return jnp.sum(v, axis=0, keepdims=True)


def _roll(v, s, axis):
    n = v.shape[axis]
    return pltpu.roll(v, s % n, axis)


def _rope(t, c, a, b):
    k = t.shape[1] // LANES
    if k > 1:
        c, a, b = (jnp.tile(v, (1, k)) for v in (c, a, b))
    return t * c + _roll(t, 16, 1) * a + _roll(t, -16, 1) * b


def _rope_bwd(d, c, a, b):
    k = d.shape[1] // LANES
    if k > 1:
        c, a, b = (jnp.tile(v, (1, k)) for v in (c, a, b))
    return d * c + _roll(d * a, -16, 1) + _roll(d * b, 16, 1)


def _gelu(x):
    c = 0.7978845608028654
    t = jnp.tanh(c * (x + 0.044715 * x * x * x))
    return 0.5 * x * (1.0 + t), t


def _gelu_grad(x, t):
    c = 0.7978845608028654
    return 0.5 * (1.0 + t) + 0.5 * x * (1.0 - t * t) * c * (1.0 + 3.0 * 0.044715 * x * x)


def _blockdot(v, w_ref, nblk, width):
    return jnp.concatenate(
        [_dot(v[:, j * width:(j + 1) * width], w_ref[j]) for j in range(nblk)], axis=1)


def _pool_cnt(row0, rows):
    t = row0 + lax.broadcasted_iota(jnp.int32, (rows, POOL_DIM), 0)
    w = jnp.left_shift(2, lax.broadcasted_iota(jnp.int32, (rows, POOL_DIM), 1) // LANES)
    return jnp.minimum(t + 1, w).astype(F32)


def _pool_windows(ext, sign):
    s2 = ext + _roll(ext, sign * 1, 0)
    t = s2[:, LANES:]
    s4 = t + _roll(t, sign * 2, 0)
    t = s4[:, LANES:]
    s8 = t + _roll(t, sign * 4, 0)
    t = s8[:, LANES:]
    s16 = t + _roll(t, sign * 8, 0)
    return jnp.concatenate([s2[:, :LANES], s4[:, :LANES], s8[:, :LANES], s16], axis=1)


def _pooled(uprev, u, row0):
    ts = u.shape[0]
    ext = jnp.concatenate([uprev, u], axis=0)
    sums = _pool_windows(ext, 1)[POOL_HALO:]
    return sums / _pool_cnt(row0, ts) - u


def _expm1(x):
    return jnp.where(jnp.abs(x) < 0.01, x * (1.0 + 0.5 * x * (1.0 + x * (1.0 / 3.0))), jnp.exp(x) - 1.0)


def _softplus(z):
    return jnp.maximum(z, 0.0) + jnp.log1p(jnp.exp(-jnp.abs(z)))


def _tile_rows(s, want):
    while s % want:
        want //= 2
    return want


def even_pre(x, tabs, g, win, pw, pscale, qg, wq, kvg, wk, wv):
    S = x.shape[0]
    ts = _tile_rows(S, 512)

    def body(x_ref, xp_ref, c_ref, a_ref, b_ref, g_ref, win_ref, pw_ref, ps_ref, qg_ref, wq_ref, kvg_ref,
             wk_ref, wv_ref, z_ref, q_ref, k_ref, v_ref, yp_ref):
        i = pl.program_id(0)
        h, _ = _rms(x_ref[...], g_ref[...])
        z = _dot(h, win_ref[...])
        z_ref[...] = z
        hp, _ = _rms(xp_ref[...], g_ref[...])
        uprev = _dot(hp, win_ref[:, :POOL_DIM]) * (i > 0).astype(F32)
        u = z[:, :POOL_DIM]
        pooled = _pooled(uprev, u, i * ts)
        yp_ref[...] = (_blockdot(pooled, pw_ref, 4, LANES) * ps_ref[...]).astype(BF16)
        c, a, b = c_ref[...], a_ref[...], b_ref[...]
        cqn, _ = _rms(z[:, 512:768], qg_ref[...])
        q_ref[...] = (_rope(_dot(cqn, wq_ref[...]), c, a, b) * (ATTN_SCALE * LOG2_E)).astype(BF16)
        ckvn, _ = _rms(z[:, 768:896], kvg_ref[...])
        krr = _rope(z[:, 896:1024], c, a, b)
        k_ref[...] = (_dot(ckvn, wk_ref[...]) + jnp.tile(krr, (1, MLA_HEADS))).astype(BF16)
        lane = lax.broadcasted_iota(jnp.int32, (ts, D), 1) % LANES
        v_ref[...] = jnp.where(lane == ONES_LANE, 1.0, _dot(ckvn, wv_ref[...])).astype(BF16)

    ins = [x, x, *tabs, g, win, pw, pscale, qg, wq, kvg, wk, wv]
    in_specs = [_row(ts, D), _prev(POOL_HALO, D, ts), _row(ts, LANES), _row(ts, LANES), _row(ts, LANES)]
    in_specs += [_const(v.shape) for v in ins[5:]]
    return pl.pallas_call(
        body, name="even_pre", grid=(S // ts,), in_specs=in_specs,
        out_specs=[_row(ts, D)] * 4 + [_row(ts, POOL_DIM)],
        out_shape=[_sds((S, D), F32)] + [_sds((S, D), BF16)] * 3 + [_sds((S, POOL_DIM), BF16)],
        compiler_params=_cp())(*ins)


ATTN_SCALE = QK_DIM ** -0.5
LOG2_E = 1.4426950408889634
LN_2 = 0.6931471805599453
ONES_LANE = 64


def _exp2(x):
    return jnp.exp2(x)


def _pair_loop(lo, hi, step, init, unroll=2):
    groups = (hi - lo) // unroll

    def group(j, c):
        for u in range(unroll):
            c = step(lo + unroll * j + u, c)
        return c

    carry = lax.fori_loop(0, groups, group, init)
    return lax.fori_loop(lo + unroll * groups, hi, step, carry)


def _after(token):
    return ([], []) if token is None else ([token], [pl.BlockSpec(memory_space=pl.ANY)])


def attn_fwd(qp, kp, vp, token=None):
    S = qp.shape[0]
    tq = _tile_rows(S, 512)
    extra, extra_specs = _after(token)

    def body(q_ref, k_ref, v_ref, *rest):
        o_ref, lse_ref = rest[-2:]
        qi = pl.program_id(1)
        q = q_ref[...]

        def block(ki, carry, masked):
            m, acc = carry
            off = pl.multiple_of(ki * tq, tq)
            s = _dot_nt(q, k_ref[pl.ds(off, tq), :])
            if masked:
                row = lax.broadcasted_iota(jnp.int32, (tq, tq), 0)
                col = lax.broadcasted_iota(jnp.int32, (tq, tq), 1)
                s = jnp.where(col <= row, s, -1e30)
            m_new = jnp.maximum(m, jnp.max(s, axis=1, keepdims=True))
            acc = _exp2(m - m_new) * acc + _dot(_exp2(s - m_new), v_ref[pl.ds(off, tq), :])
            return m_new, acc

        init = (jnp.full((tq, 1), -1e30, F32), jnp.zeros((tq, LANES), F32))
        carry = _pair_loop(0, qi, lambda ki, c: block(ki, c, False), init, unroll=4)
        m, acc = block(qi, carry, True)
        l = acc[:, ONES_LANE:ONES_LANE + 1]
        o_ref[...] = acc / l
        lse_ref[...] = jnp.broadcast_to(m + jnp.log(l) * LOG2_E, (tq, LANES))

    blk = pl.BlockSpec((tq, LANES), lambda h, i: (i, h))
    full = pl.BlockSpec((S, LANES), lambda h, i: (0, h))
    return pl.pallas_call(
        body, name="attn_fwd", grid=(MLA_HEADS, S // tq), in_specs=[blk, full, full] + extra_specs,
        out_specs=[blk, blk], out_shape=[_sds((S, D), F32), _sds((S, D), F32)], compiler_params=_cp2())(
            qp, kp, vp, *extra)


def even_post(x, ypool, o, wo_pool, wo_att):
    S = x.shape[0]
    ts = _tile_rows(S, 512)

    def body(x_ref, yp_ref, o_ref, wp_ref, wa_ref, out_ref):
        out_ref[...] = x_ref[...] + _dot(yp_ref[...], wp_ref[...]) + _dot(o_ref[...], wa_ref[...])

    return pl.pallas_call(
        body, name="even_post", grid=(S // ts,),
        in_specs=[_row(ts, D), _row(ts, POOL_DIM), _row(ts, D), _const(wo_pool.shape), _const(wo_att.shape)],
        out_specs=_row(ts, D), out_shape=_sds((S, D), F32), compiler_params=_cp())(x, ypool, o, wo_pool, wo_att)


def mem_kv(mem, g, wkv):
    M = mem.shape[0]

    def body(mem_ref, g_ref, w_ref, mn_ref, k_ref, v_ref):
        mn, _ = _rms(mem_ref[...], g_ref[...])
        mn_ref[...] = mn.astype(BF16)
        k_ref[...] = _dot(mn, w_ref[:, :D]).astype(BF16)
        v_ref[...] = _dot(mn, w_ref[:, D:]).astype(BF16)

    return pl.pallas_call(
        body, name="mem_kv", grid=(1,), in_specs=[_acc(mem.shape), _acc(g.shape), _acc(wkv.shape)],
        out_specs=[_acc((M, D))] * 3, out_shape=[_sds((M, D), BF16)] * 3, compiler_params=_cp())(mem, g, wkv)


def _xattn_heads(hx, wq_ref, k_ref, v_ref):
    q = _dot(hx, wq_ref[...])
    scale = MEM_HEAD_DIM ** -0.5
    ps, os_ = [], []
    for h in range(MEM_HEADS):
        sl = slice(h * MEM_HEAD_DIM, (h + 1) * MEM_HEAD_DIM)
        s = _dot_nt(q[:, sl], k_ref[:, sl]) * scale
        e = jnp.exp(s - jnp.max(s, axis=1, keepdims=True))
        p = e / jnp.sum(e, axis=1, keepdims=True)
        ps.append(p)
        os_.append(_dot(p, v_ref[:, sl]))
    return q, ps, jnp.concatenate(os_, axis=1)


def xattn_fwd(x, g, wq, kmem, vmem, wo):
    S = x.shape[0]
    ts = _tile_rows(S, 512)

    def body(x_ref, g_ref, wq_ref, k_ref, v_ref, wo_ref, out_ref):
        x_ = x_ref[...]
        hx, _ = _rms(x_, g_ref[...])
        _, _, o = _xattn_heads(hx, wq_ref, k_ref, v_ref)
        out_ref[...] = x_ + _dot(o, wo_ref[...])

    ins = [x, g, wq, kmem, vmem, wo]
    return pl.pallas_call(
        body, name="xattn_fwd", grid=(S // ts,), in_specs=[_row(ts, D)] + [_const(v.shape) for v in ins[1:]],
        out_specs=_row(ts, D), out_shape=_sds((S, D), F32), compiler_params=_cp())(*ins)


def xattn_bwd(x, dy, g, wq, wqT, kmem, vmem, woT):
    S = x.shape[0]
    M = kmem.shape[0]
    ts = _tile_rows(S, 512)
    scale = MEM_HEAD_DIM ** -0.5

    def body(x_ref, dy_ref, g_ref, wq_ref, wqT_ref, k_ref, v_ref, woT_ref,
             dx_ref, o_ref, dq_ref, hx_ref, dg_ref, dk_ref, dv_ref):
        i = pl.program_id(0)

        @pl.when(i == 0)
        def _():
            dg_ref[...] = jnp.zeros_like(dg_ref)
            dk_ref[...] = jnp.zeros_like(dk_ref)
            dv_ref[...] = jnp.zeros_like(dv_ref)

        x_, dy_ = x_ref[...], dy_ref[...]
        hx, rstd = _rms(x_, g_ref[...])
        q, ps, o = _xattn_heads(hx, wq_ref, k_ref, v_ref)
        hx_ref[...] = hx.astype(BF16)
        o_ref[...] = o.astype(BF16)
        do = _dot(dy_, woT_ref[...])
        dqs = []
        for h in range(MEM_HEADS):
            sl = slice(h * MEM_HEAD_DIM, (h + 1) * MEM_HEAD_DIM)
            p, do_h = ps[h], do[:, sl]
            dp = _dot_nt(do_h, v_ref[:, sl])
            ds = p * (dp - jnp.sum(p * dp, axis=1, keepdims=True)) * scale
            dqs.append(_dot(ds, k_ref[:, sl]))
            dk_ref[:, sl] += _dot_tn(ds, q[:, sl])
            dv_ref[:, sl] += _dot_tn(p, do_h)
        dq = jnp.concatenate(dqs, axis=1)
        dq_ref[...] = dq.astype(BF16)
        dxn, dgr = _rms_bwd(x_, g_ref[...], rstd, _dot(dq, wqT_ref[...]))
        dx_ref[...] = dy_ + dxn
        dg_ref[...] += _rowsum(dgr)

    ins = [x, dy, g, wq, wqT, kmem, vmem, woT]
    return pl.pallas_call(
        body, name="xattn_bwd", grid=(S // ts,),
        in_specs=[_row(ts, D), _row(ts, D)] + [_const(v.shape) for v in ins[2:]],
        out_specs=[_row(ts, D)] * 4 + [_acc((1, D)), _acc((M, D)), _acc((M, D))],
        out_shape=[_sds((S, D), F32)] + [_sds((S, D), BF16)] * 3 + [_sds((1, D), F32), _sds((M, D), F32),
                                                                    _sds((M, D), F32)],
        compiler_params=_cp())(*ins)


def mem_bwd(mem, g, dk, dv, wkvT):
    M = mem.shape[0]

    def body(mem_ref, g_ref, dk_ref, dv_ref, w_ref, dkv_ref, dg_ref):
        dkv = jnp.concatenate([dk_ref[...], dv_ref[...]], axis=1)
        dkv_ref[...] = dkv.astype(BF16)
        _, rstd = _rms(mem_ref[...], g_ref[...])
        dg_ref[...] = _rowsum(_dot(dkv, w_ref[...]) * (mem_ref[...] * rstd))

    ins = [mem, g, dk, dv, wkvT]
    return pl.pallas_call(
        body, name="mem_bwd", grid=(1,), in_specs=[_acc(v.shape) for v in ins],
        out_specs=[_acc((M, 2 * D)), _acc((1, D))], out_shape=[_sds((M, 2 * D), BF16), _sds((1, D), F32)],
        compiler_params=_cp())(*ins)


FF_CHUNK = 2 * D_FF // N_DEV
FF_HALF = N_DEV // 2


def _layer_of(w, layer):
    return pl.BlockSpec((N_DEV, None) + w.shape[2:], lambda i: (0, layer, 0, 0), pipeline_mode=pl.Buffered(1))


def ffn_fwd(x, g, wgu, layer, wd):
    S = x.shape[0]
    ts = _tile_rows(S, 256)

    def body(x_ref, g_ref, wgu_ref, wd_ref, out_ref):
        x_ = x_ref[...]
        hf = _rms(x_, g_ref[...])[0].astype(BF16)
        out = x_
        for j in range(FF_HALF):
            gg = _dot(hf, wgu_ref[j])
            out = out + _dot(gg * jax.nn.sigmoid(gg) * _dot(hf, wgu_ref[j + FF_HALF]), wd_ref[j])
        out_ref[...] = out

    return pl.pallas_call(
        body, name="ffn_fwd", grid=(S // ts,),
        in_specs=[_row(ts, D), _const(g.shape), _layer_of(wgu, layer), _const(wd.shape)],
        out_specs=_row(ts, D), out_shape=_sds((S, D), F32), compiler_params=_cp())(x, g, wgu, wd)


def ffn_bwd_a(x, dy, g, wgu, layer, wdT):
    S = x.shape[0]
    ts = _tile_rows(S, 256)

    def body(x_ref, dy_ref, g_ref, wgu_ref, wdT_ref, hf_ref, act_ref, dgu_ref):
        hf = _rms(x_ref[...], g_ref[...])[0].astype(BF16)
        hf_ref[...] = hf
        dy_ = dy_ref[...].astype(BF16)
        for j in range(FF_HALF):
            gg, uu = _dot(hf, wgu_ref[j]), _dot(hf, wgu_ref[j + FF_HALF])
            sg = jax.nn.sigmoid(gg)
            silu = gg * sg
            act_ref[j] = (silu * uu).astype(BF16)
            dact = _dot(dy_, wdT_ref[j])
            dgu_ref[j] = (dact * uu * (sg * (1.0 + gg * (1.0 - sg)))).astype(BF16)
            dgu_ref[j + FF_HALF] = (dact * silu).astype(BF16)

    ins = [x, dy, g, wgu, wdT]
    chunked = lambda c: pl.BlockSpec((c, ts, FF_CHUNK), lambda i: (0, i, 0))
    return pl.pallas_call(
        body, name="ffn_bwd_a", grid=(S // ts,),
        in_specs=[_row(ts, D), _row(ts, D), _const(g.shape), _layer_of(wgu, layer), _const(wdT.shape)],
        out_specs=[_row(ts, D), chunked(FF_HALF), chunked(N_DEV)],
        out_shape=[_sds((S, D), BF16), _sds((FF_HALF, S, FF_CHUNK), BF16), _sds((N_DEV, S, FF_CHUNK), BF16)],
        compiler_params=_cp())(*ins)


def ffn_bwd_b(x, dy, dgu, g, wguT):
    S = x.shape[0]
    ts = _tile_rows(S, 512)

    def body(x_ref, dy_ref, dgu_ref, g_ref, wT_ref, dx_ref, dg_ref):
        @pl.when(pl.program_id(0) == 0)
        def _():
            dg_ref[...] = jnp.zeros_like(dg_ref)

        dh = _dot(dgu_ref[0], wT_ref[0])
        for j in range(1, N_DEV):
            dh = dh + _dot(dgu_ref[j], wT_ref[j])
        x_ = x_ref[...]
        _, rstd = _rms(x_, g_ref[...])
        dxn, dgr = _rms_bwd(x_, g_ref[...], rstd, dh)
        dx_ref[...] = dy_ref[...] + dxn
        dg_ref[...] += _rowsum(dgr)

    return pl.pallas_call(
        body, name="ffn_bwd_b", grid=(S // ts,),
        in_specs=[_row(ts, D), _row(ts, D), pl.BlockSpec((N_DEV, ts, FF_CHUNK), lambda i: (0, i, 0)),
                  _const(g.shape), _const(wguT.shape)],
        out_specs=[_row(ts, D), _acc((1, D))], out_shape=[_sds((S, D), F32), _sds((1, D), F32)],
        compiler_params=_cp())(x, dy, dgu, g, wguT)


def _conv_fwd(xprev, xbp, cw_ref, cb):
    ext = jnp.concatenate([xprev, xbp], axis=0)
    acc = cb + cw_ref[3:4, :] * xbp
    for k in range(3):
        acc = acc + cw_ref[k:k + 1, :] * _roll(ext, 3 - k, 0)[CONV_HALO:]
    return acc


def _gates(xb, keep, wr_ref, br, wi_ref, bi, lam):
    r = jax.nn.sigmoid(_blockdot(xb, wr_ref, LRU_HEADS, LRU_HEAD_DIM) + br)
    ig = jax.nn.sigmoid(_blockdot(xb, wi_ref, LRU_HEADS, LRU_HEAD_DIM) + bi)
    sp = _softplus(-lam)
    log_a = -LRU_C * r * sp
    a = jnp.exp(log_a)
    mult = jnp.sqrt(jnp.maximum(-_expm1(2.0 * log_a), 0.0))
    return r, ig, sp, a, mult


def odd_pre(x, keep, g, win, cw, cb, wr, br, wi, bi, lam):
    S = x.shape[0]
    ts = _tile_rows(S, 512)

    def body(x_ref, xp_ref, keep_ref, g_ref, win_ref, cw_ref, cb_ref, wr_ref, br_ref, wi_ref, bi_ref, lam_ref,
             z_ref, a_ref, b_ref):
        i = pl.program_id(0)
        h, _ = _rms(x_ref[...], g_ref[...])
        z = _dot(h, win_ref[...])
        z_ref[...] = z
        hp, _ = _rms(xp_ref[...], g_ref[...])
        xprev = _dot(hp, win_ref[:, D:]) * (i > 0).astype(F32)
        xb = _conv_fwd(xprev, z[:, D:], cw_ref, cb_ref[...])
        keep_ = keep_ref[...]
        _, ig, _, a, mult = _gates(xb, keep_, wr_ref, br_ref[...], wi_ref, bi_ref[...], lam_ref[...])
        a_ref[...] = a * keep_
        b_ref[...] = jnp.where(keep_ > 0.0, mult, 1.0) * (ig * xb)

    ins = [x, x, keep, g, win, cw, cb, wr, br, wi, bi, lam]
    return pl.pallas_call(
        body, name="odd_pre", grid=(S // ts,),
        in_specs=[_row(ts, D), _prev(CONV_HALO, D, ts), _row(ts, 1)] + [_const(v.shape) for v in ins[3:]],
        out_specs=[_row(ts, 2 * D), _row(ts, D), _row(ts, D)],
        out_shape=[_sds((S, 2 * D), F32), _sds((S, D), F32), _sds((S, D), F32)], compiler_params=_cp())(*ins)


def lru_scan(a, b, reverse=False):
    S = a.shape[0]
    ts = _tile_rows(S, 512)
    n = S // ts
    groups = ts // 8

    def body(a_ref, an_ref, b_ref, h_ref, carry_ref, ash_ref):
        i = pl.program_id(0)

        @pl.when(i == 0)
        def _():
            carry_ref[...] = jnp.zeros_like(carry_ref)

        rid = lax.broadcasted_iota(jnp.int32, (8, D), 0)
        if reverse:
            ext = jnp.concatenate([a_ref[...], an_ref[...] * (i > 0).astype(F32)], axis=0)
            ash_ref[...] = _roll(ext, -1, 0)[:ts]
        src = ash_ref if reverse else a_ref

        def group(j, carry):
            off = pl.multiple_of((groups - 1 - j if reverse else j) * 8, 8)
            a8, b8 = src[pl.ds(off, 8), :], b_ref[pl.ds(off, 8), :]
            for k in (1, 2, 4):
                inside = (rid < 8 - k) if reverse else (rid >= k)
                sh = -k if reverse else k
                a_sh = jnp.where(inside, _roll(a8, sh, 0), 1.0)
                b_sh = jnp.where(inside, _roll(b8, sh, 0), 0.0)
                b8 = a8 * b_sh + b8
                a8 = a8 * a_sh
            h8 = a8 * carry + b8
            h_ref[pl.ds(off, 8), :] = h8
            return h8[0:1, :] if reverse else h8[7:8, :]

        carry_ref[...] = lax.fori_loop(0, groups, group, carry_ref[...])

    if reverse:
        r = ts // 8
        tile = pl.BlockSpec((ts, D), lambda i: (n - 1 - i, 0))
        halo = pl.BlockSpec((8, D), lambda i: (jnp.minimum((n - i) * r, n * r - 1), 0))
    else:
        tile, halo = _row(ts, D), _prev(8, D, ts)
    return pl.pallas_call(
        body, name="lru_scan_rev" if reverse else "lru_scan", grid=(n,), in_specs=[tile, halo, tile],
        out_specs=tile, out_shape=_sds((S, D), F32),
        scratch_shapes=[pltpu.VMEM((1, D), F32), pltpu.VMEM((ts, D), F32)], compiler_params=_cp())(a, a, b)


def odd_post(x, z, hseq, wout):
    S = x.shape[0]
    ts = _tile_rows(S, 512)

    def body(x_ref, gate_ref, h_ref, w_ref, out_ref):
        gl, _ = _gelu(gate_ref[...])
        out_ref[...] = x_ref[...] + _dot(gl * h_ref[...], w_ref[...])

    return pl.pallas_call(
        body, name="odd_post", grid=(S // ts,),
        in_specs=[_row(ts, D), _row(ts, D), _row(ts, D), _const(wout.shape)],
        out_specs=_row(ts, D), out_shape=_sds((S, D), F32), compiler_params=_cp())(x, z, hseq, wout)


def odd_post_bwd(dy, z, hseq, woutT):
    S = dy.shape[0]
    ts = _tile_rows(S, 512)

    def body(dy_ref, gate_ref, h_ref, w_ref, y_ref, dgate_ref, dh_ref):
        gate, hs = gate_ref[...], h_ref[...]
        gl, t = _gelu(gate)
        y_ref[...] = (gl * hs).astype(BF16)
        dyy = _dot(dy_ref[...], w_ref[...])
        dgate_ref[...] = dyy * hs * _gelu_grad(gate, t)
        dh_ref[...] = dyy * gl

    return pl.pallas_call(
        body, name="odd_post_bwd", grid=(S // ts,),
        in_specs=[_row(ts, D), _row(ts, D), _row(ts, D), _const(woutT.shape)],
        out_specs=[_row(ts, D)] * 3, out_shape=[_sds((S, D), BF16), _sds((S, D), F32), _sds((S, D), F32)],
        compiler_params=_cp())(dy, z, hseq, woutT)


def odd_gates_bwd(z, lam_grad, hseq, keep, cw, cb, wr, wrT, br, wi, wiT, bi, lam):
    S = z.shape[0]
    ts = _tile_rows(S, 512)

    def body(xbp_ref, xbpp_ref, lg_ref, h_ref, hp_ref, keep_ref, cw_ref, cb_ref, wr_ref, wrT_ref, br_ref, wi_ref,
             wiT_ref, bi_ref, lam_ref, dxb_ref, dcb_ref, dbr_ref, dbi_ref, dlam_ref, dwr_ref, dwi_ref):
        i = pl.program_id(0)

        @pl.when(i == 0)
        def _():
            for ref in (dcb_ref, dbr_ref, dbi_ref, dlam_ref, dwr_ref, dwi_ref):
                ref[...] = jnp.zeros_like(ref)

        first = (i > 0).astype(F32)
        xb = _conv_fwd(xbpp_ref[...] * first, xbp_ref[...], cw_ref, cb_ref[...])
        keep_ = keep_ref[...]
        lam_ = lam_ref[...]
        r, ig, sp, a, mult = _gates(xb, keep_, wr_ref, br_ref[...], wi_ref, bi_ref[...], lam_)
        hs = h_ref[...]
        hprev = _roll(jnp.concatenate([hp_ref[...] * first, hs], axis=0), 1, 0)[CONV_HALO:]
        lg = lg_ref[...]
        da = lg * hprev * keep_
        ixb = ig * xb
        dmult = lg * ixb * keep_
        dixb = lg * jnp.where(keep_ > 0.0, mult, 1.0)
        dlog_a = da * a - dmult * jnp.where(mult > 0.0, a * a / mult, 0.0)
        dr = dlog_a * (-LRU_C * sp)
        dlam_ref[...] += _rowsum(dlog_a * (-LRU_C * r)) * (-jax.nn.sigmoid(-lam_))
        dpr = dr * r * (1.0 - r)
        dpi = dixb * xb * ig * (1.0 - ig)
        dbr_ref[...] += _rowsum(dpr)
        dbi_ref[...] += _rowsum(dpi)
        dxb = dixb * ig
        parts = []
        for h in range(LRU_HEADS):
            sl = slice(h * LRU_HEAD_DIM, (h + 1) * LRU_HEAD_DIM)
            dwr_ref[h] += _dot_tn(xb[:, sl], dpr[:, sl])
            dwi_ref[h] += _dot_tn(xb[:, sl], dpi[:, sl])
            parts.append(_dot(dpr[:, sl], wrT_ref[h]) + _dot(dpi[:, sl], wiT_ref[h]))
        dxb = dxb + jnp.concatenate(parts, axis=1)
        dxb_ref[...] = dxb
        dcb_ref[...] += _rowsum(dxb)

    ins = [z, z, lam_grad, hseq, hseq, keep, cw, cb, wr, wrT, br, wi, wiT, bi, lam]
    in_specs = [_row(ts, D, 1), _prev(CONV_HALO, D, ts, 1), _row(ts, D), _row(ts, D), _prev(CONV_HALO, D, ts),
                _row(ts, 1)] + [_const(v.shape) for v in ins[6:]]
    gshape = (LRU_HEADS, LRU_HEAD_DIM, LRU_HEAD_DIM)
    return pl.pallas_call(
        body, name="odd_gates_bwd", grid=(S // ts,), in_specs=in_specs,
        out_specs=[_row(ts, D)] + [_acc((1, D))] * 4 + [_acc(gshape)] * 2,
        out_shape=[_sds((S, D), F32)] + [_sds((1, D), F32)] * 4 + [_sds(gshape, F32)] * 2,
        compiler_params=_cp())(*ins)


def odd_pre_bwd(x, dy, z, dxb, dgate, g, cw, winT):
    S = x.shape[0]
    ts = _tile_rows(S, 512)
    n = S // ts

    def body(x_ref, dy_ref, xbp_ref, xbpp_ref, dxb_ref, dxbn_ref, dgate_ref, g_ref, cw_ref, winT_ref,
             dx_ref, h_ref, dz_ref, dcw_ref, dg_ref):
        i = pl.program_id(0)

        @pl.when(i == 0)
        def _():
            dcw_ref[...] = jnp.zeros_like(dcw_ref)
            dg_ref[...] = jnp.zeros_like(dg_ref)

        dxb = dxb_ref[...]
        extd = jnp.concatenate([dxb, dxbn_ref[...] * (i < n - 1).astype(F32)], axis=0)
        extx = jnp.concatenate([xbpp_ref[...] * (i > 0).astype(F32), xbp_ref[...]], axis=0)
        dxbp = cw_ref[3:4, :] * dxb
        dcw_ref[3:4, :] += _rowsum(dxb * xbp_ref[...])
        for k in range(3):
            dxbp = dxbp + cw_ref[k:k + 1, :] * _roll(extd, -(3 - k), 0)[:ts]
            dcw_ref[k:k + 1, :] += _rowsum(dxb * _roll(extx, 3 - k, 0)[CONV_HALO:])
        dz = jnp.concatenate([dgate_ref[...], dxbp], axis=1)
        dz_ref[...] = dz.astype(BF16)
        x_ = x_ref[...]
        h, rstd = _rms(x_, g_ref[...])
        h_ref[...] = h.astype(BF16)
        dxn, dgr = _rms_bwd(x_, g_ref[...], rstd, _dot(dz, winT_ref[...]))
        dx_ref[...] = dy_ref[...] + dxn
        dg_ref[...] += _rowsum(dgr)

    ins = [x, dy, z, z, dxb, dxb, dgate, g, cw, winT]
    in_specs = [_row(ts, D), _row(ts, D), _row(ts, D, 1), _prev(CONV_HALO, D, ts, 1), _row(ts, D),
                _next(CONV_HALO, D, ts, n), _row(ts, D)] + [_const(v.shape) for v in ins[7:]]
    return pl.pallas_call(
        body, name="odd_pre_bwd", grid=(n,), in_specs=in_specs,
        out_specs=[_row(ts, D), _row(ts, D), _row(ts, 2 * D), _acc((4, D)), _acc((1, D))],
        out_shape=[_sds((S, D), F32), _sds((S, D), BF16), _sds((S, 2 * D), BF16), _sds((4, D), F32),
                   _sds((1, D), F32)],
        compiler_params=_cp())(*ins)


def loss_head(x, target, g):
    S = x.shape[0]
    ts = _tile_rows(S, 512)

    def body(x_ref, t_ref, g_ref, dx_ref, dg_ref, loss_ref):
        @pl.when(pl.program_id(0) == 0)
        def _():
            dg_ref[...] = jnp.zeros_like(dg_ref)
            loss_ref[...] = jnp.zeros_like(loss_ref)

        x_ = x_ref[...]
        y, rstd = _rms(x_, g_ref[...])
        err = y - t_ref[...]
        loss_ref[...] += 0.5 * _rowsum(jnp.mean(err * err, axis=1, keepdims=True))
        dxn, dgr = _rms_bwd(x_, g_ref[...], rstd, err * (1.0 / D))
        dx_ref[...] = dxn
        dg_ref[...] += _rowsum(dgr)

    return pl.pallas_call(
        body, name="loss_head", grid=(S // ts,), in_specs=[_row(ts, D), _row(ts, D), _const(g.shape)],
        out_specs=[_row(ts, D), _acc((1, D)), _acc((1, 1))],
        out_shape=[_sds((S, D), F32), _sds((1, D), F32), _sds((1, 1), F32)], compiler_params=_cp())(x, target, g)


def even_post_bwd(dy, o, woT_pool, woT_att):
    S = dy.shape[0]
    ts = _tile_rows(S, 512)

    def body(dy_ref, o_ref, wp_ref, wa_ref, dyp_ref, do_ref, delta_ref):
        dy_ = dy_ref[...]
        dyp_ref[...] = _dot(dy_, wp_ref[...])
        do = _dot(dy_, wa_ref[...])
        do_ref[...] = do.astype(BF16)
        prod = do * o_ref[...]
        delta_ref[...] = jnp.concatenate(
            [jnp.broadcast_to(jnp.sum(prod[:, h * LANES:(h + 1) * LANES], axis=1, keepdims=True), (ts, LANES))
             for h in range(MLA_HEADS)], axis=1)

    return pl.pallas_call(
        body, name="even_post_bwd", grid=(S // ts,),
        in_specs=[_row(ts, D), _row(ts, D), _const(woT_pool.shape), _const(woT_att.shape)],
        out_specs=[_row(ts, POOL_DIM), _row(ts, D), _row(ts, D)],
        out_shape=[_sds((S, POOL_DIM), F32), _sds((S, D), BF16), _sds((S, D), F32)],
        compiler_params=_cp())(dy, o, woT_pool, woT_att)


def attn_bwd(qp, kp, vp, do, lse_row, delta_row, token=None):
    S = qp.shape[0]
    tk = _tile_rows(S, 512)
    nq = S // tk
    extra, extra_specs = _after(token)

    def body(q_ref, k_ref, v_ref, do_ref, lse_ref, delta_ref, *rest):
        dq_ref, dk_ref, dv_ref = rest[-3:]
        kj = pl.program_id(1)

        @pl.when(kj == 0)
        def _():
            dq_ref[...] = jnp.zeros_like(dq_ref)

        k, v = k_ref[...], v_ref[...]

        def block(qi, carry, masked):
            dk, dv = carry
            off = pl.multiple_of(qi * tk, tk)
            q = q_ref[pl.ds(off, tk), :]
            do_ = do_ref[pl.ds(off, tk), :]
            st = _dot_nt(k, q)
            if masked:
                row = lax.broadcasted_iota(jnp.int32, (tk, tk), 0)
                col = lax.broadcasted_iota(jnp.int32, (tk, tk), 1)
                st = jnp.where(col >= row, st, -1e30)
            pt = _exp2(st - lse_ref[qi])
            dv = dv + _dot(pt, do_)
            dst = (pt * (_dot_nt(v, do_) - delta_ref[qi])).astype(BF16)
            dk = dk + _dot(dst, q)
            dq_ref[pl.ds(off, tk), :] += _dot_tn(dst, k)
            return dk, dv

        zero = jnp.zeros((tk, LANES), F32)
        carry = block(kj, (zero, zero), True)
        dk, dv = _pair_loop(kj + 1, nq, lambda qi, c: block(qi, c, False), carry)
        dk_ref[...] = dk * LN_2
        dv_ref[...] = dv

    blk = pl.BlockSpec((tk, LANES), lambda h, j: (j, h))
    full = pl.BlockSpec((S, LANES), lambda h, j: (0, h))
    rowv = pl.BlockSpec((None, nq, 1, tk), lambda h, j: (h, 0, 0, 0))
    return pl.pallas_call(
        body, name="attn_bwd", grid=(MLA_HEADS, nq), in_specs=[full, blk, blk, full, rowv, rowv] + extra_specs,
        out_specs=[full, blk, blk], out_shape=[_sds((S, D), F32)] * 3, compiler_params=_cp2())(
            qp, kp, vp, do, lse_row, delta_row, *extra)


def even_pre_bwd(x, dy, z, dq, dk, dv, dyp, tabs, g, winT, pw, pwT, pscale, qg, wqT, kvg, wkT, wvT):
    S = x.shape[0]
    ts = _tile_rows(S, 512)
    n = S // ts

    def body(x_ref, dy_ref, z_ref, up_ref, dq_ref, dk_ref, dv_ref, dyp_ref, dypn_ref, c_ref, a_ref, b_ref,
             g_ref, winT_ref, pw_ref, pwT_ref, ps_ref, qg_ref, wqT_ref, kvg_ref, wkT_ref, wvT_ref,
             dx_ref, h_ref, dz_ref, dg_ref, dpw_ref, dps_ref, dqg_ref, dwq_ref, dkvg_ref, dwk_ref, dwv_ref):
        i = pl.program_id(0)

        @pl.when(i == 0)
        def _():
            for ref in (dg_ref, dpw_ref, dps_ref, dqg_ref, dwq_ref, dkvg_ref, dwk_ref, dwv_ref):
                ref[...] = jnp.zeros_like(ref)

        z = z_ref[...]
        c, a, b = c_ref[...], a_ref[...], b_ref[...]
        ps = ps_ref[...]
        u = z[:, :POOL_DIM]
        pooled = _pooled(up_ref[...] * (i > 0).astype(F32), u, i * ts)
        dyp_ = dyp_ref[...]
        dps_ref[...] += _rowsum(dyp_ * _blockdot(pooled, pw_ref, 4, LANES))
        ext = jnp.concatenate([dyp_, dypn_ref[...] * (i < n - 1).astype(F32)], axis=0) * ps
        for gidx in range(4):
            sl = slice(gidx * LANES, (gidx + 1) * LANES)
            dpw_ref[gidx] += _dot_tn(pooled[:, sl], ext[:ts, sl])
        dpooled = _blockdot(ext, pwT_ref, 4, LANES)
        dm = dpooled / _pool_cnt(i * ts, ts + POOL_HALO)
        du = _pool_windows(dm, -1)[:ts] - dpooled[:ts]
        cq = z[:, 512:768]
        cqn, rstd_q = _rms(cq, qg_ref[...])
        dqf = _rope_bwd(dq_ref[...] * ATTN_SCALE, c, a, b)
        dwq_ref[...] += _dot_tn(cqn, dqf)
        dcq, dqg_rows = _rms_bwd(cq, qg_ref[...], rstd_q, _dot(dqf, wqT_ref[...]))
        dqg_ref[...] += _rowsum(dqg_rows)
        ckv = z[:, 768:896]
        ckvn, rstd_kv = _rms(ckv, kvg_ref[...])
        dk_, dv_ = dk_ref[...], dv_ref[...]
        dwk_ref[...] += _dot_tn(ckvn, dk_)
        dwv_ref[...] += _dot_tn(ckvn, dv_)
        dckv, dkvg_rows = _rms_bwd(ckv, kvg_ref[...], rstd_kv, _dot(dk_, wkT_ref[...]) + _dot(dv_, wvT_ref[...]))
        dkvg_ref[...] += _rowsum(dkvg_rows)
        dkr = dk_[:, :LANES]
        for h in range(1, MLA_HEADS):
            dkr = dkr + dk_[:, h * LANES:(h + 1) * LANES]
        lane = lax.broadcasted_iota(jnp.int32, (ts, LANES), 1)
        dkr = jnp.where((lane >= 64) & (lane < 96), _rope_bwd(dkr, c, a, b), 0.0)
        dz = jnp.concatenate([du, dcq, dckv, dkr], axis=1)
        dz_ref[...] = dz.astype(BF16)
        x_ = x_ref[...]
        h, rstd = _rms(x_, g_ref[...])
        h_ref[...] = h.astype(BF16)
        dxn, dgr = _rms_bwd(x_, g_ref[...], rstd, _dot(dz, winT_ref[...]))
        dx_ref[...] = dy_ref[...] + dxn
        dg_ref[...] += _rowsum(dgr)

    ins = [x, dy, z, z, dq, dk, dv, dyp, dyp, *tabs, g, winT, pw, pwT, pscale, qg, wqT, kvg, wkT, wvT]
    in_specs = [_row(ts, D), _row(ts, D), _row(ts, D), _prev(POOL_HALO, POOL_DIM, ts), _row(ts, D), _row(ts, D),
                _row(ts, D), _row(ts, POOL_DIM), _next(POOL_HALO, POOL_DIM, ts, n), _row(ts, LANES),
                _row(ts, LANES), _row(ts, LANES)] + [_const(v.shape) for v in ins[12:]]
    acc_shapes = [(1, D), (4, LANES, LANES), (1, POOL_DIM), (1, Q_LORA), (Q_LORA, D), (1, KV_LORA), (KV_LORA, D),
                  (KV_LORA, D)]
    return pl.pallas_call(
        body, name="even_pre_bwd", grid=(n,), in_specs=in_specs,
        out_specs=[_row(ts, D)] * 3 + [_acc(s) for s in acc_shapes],
        out_shape=[_sds((S, D), F32), _sds((S, D), BF16), _sds((S, D), BF16)] + [_sds(s, F32) for s in acc_shapes],
        compiler_params=_cp())(*ins)


def _pick(n, options):
    for o in options:
        if n % o == 0:
            return o
    return n


def matmul_tn(name, a, b):
    out_dtype = BF16
    S = a.shape[-2]
    ts = _tile_rows(S, 2048)
    steps = S // ts

    def body(a_ref, b_ref, o_ref, acc_ref):
        s = pl.program_id(2)

        @pl.when(s == 0)
        def _():
            acc_ref[...] = jnp.zeros_like(acc_ref)

        acc_ref[...] += _dot_tn(a_ref[...], b_ref[...])

        @pl.when(s == steps - 1)
        def _():
            o_ref[...] = acc_ref[...].astype(o_ref.dtype)

    if a.ndim == 3:
        C, _, K = a.shape
        N = b.shape[1]
        tn = _pick(N, (512, 256, 128))
        grid = (C, N // tn, S // ts)
        in_specs = [pl.BlockSpec((None, ts, K), lambda c, j, s: (c, s, 0)),
                    pl.BlockSpec((ts, tn), lambda c, j, s: (s, j))]
        out_spec, out_shape, tile = pl.BlockSpec((None, K, tn), lambda c, j, s: (c, 0, j)), (C, K, N), (K, tn)
    elif b.ndim == 3:
        C, _, N = b.shape
        K = a.shape[1]
        tk = _pick(K, (512, 256, 128))
        grid = (C, K // tk, S // ts)
        in_specs = [pl.BlockSpec((ts, tk), lambda c, i, s: (s, i)),
                    pl.BlockSpec((None, ts, N), lambda c, i, s: (c, s, 0))]
        out_spec, out_shape, tile = pl.BlockSpec((None, tk, N), lambda c, i, s: (c, i, 0)), (C, K, N), (tk, N)
    else:
        K, N = a.shape[1], b.shape[1]
        tk = _pick(K, (512, 256, 128))
        tn = _pick(N, (512, 256, 128))
        grid = (K // tk, N // tn, S // ts)
        in_specs = [pl.BlockSpec((ts, tk), lambda i, j, s: (s, i)), pl.BlockSpec((ts, tn), lambda i, j, s: (s, j))]
        out_spec, out_shape, tile = pl.BlockSpec((tk, tn), lambda i, j, s: (i, j)), (K, N), (tk, tn)
    return pl.pallas_call(
        body, name=name, grid=grid, in_specs=in_specs, out_specs=out_spec, out_shape=_sds(out_shape, out_dtype),
        scratch_shapes=[pltpu.VMEM(tile, F32)], compiler_params=pltpu.CompilerParams(dimension_semantics=("arbitrary",) * 3, vmem_limit_bytes=VMEM_LIMIT))(
            a, b)


def _my_id():
    return lax.axis_index("x") * 4 + lax.axis_index("y") * 2 + lax.axis_index("c")


def _peer(j):
    x, y, c = lax.axis_index("x"), lax.axis_index("y"), lax.axis_index("c")
    px = 1 - x if j & 4 else x
    py = 1 - y if j & 2 else y
    pc = 1 - c if j & 1 else c
    return (px, py, pc), px * 4 + py * 2 + pc


def all_gather(name, arrays):
    n = len(arrays)

    def body(*refs):
        ins, outs = refs[:n], refs[n:2 * n]
        send_sems, recv_sems, local_sems = refs[2 * n:]
        me = _my_id()
        local = [pltpu.make_async_copy(ins[k], outs[k].at[me], local_sems.at[k]) for k in range(n)]
        for cp in local:
            cp.start()
        sends = []
        for j in range(1, N_DEV):
            peer, _ = _peer(j)
            for k in range(n):
                cp = pltpu.make_async_remote_copy(
                    src_ref=ins[k], dst_ref=outs[k].at[me], send_sem=send_sems.at[k, j - 1],
                    recv_sem=recv_sems.at[k, j - 1], device_id=peer, device_id_type=pl.DeviceIdType.MESH)
                cp.start()
                sends.append(cp)
        for j in range(1, N_DEV):
            peer, pid = _peer(j)
            for k in range(n):
                pltpu.make_async_remote_copy(
                    src_ref=ins[k], dst_ref=outs[k].at[pid], send_sem=send_sems.at[k, j - 1],
                    recv_sem=recv_sems.at[k, j - 1], device_id=peer, device_id_type=pl.DeviceIdType.MESH).wait_recv()
        for cp in sends:
            cp.wait_send()
        for cp in local:
            cp.wait()

    any_spec = pl.BlockSpec(memory_space=pl.ANY)
    return pl.pallas_call(
        body, name=name, in_specs=[any_spec] * n, out_specs=[any_spec] * n,
        out_shape=[_sds((N_DEV,) + a.shape, a.dtype) for a in arrays],
        scratch_shapes=[pltpu.SemaphoreType.DMA((n, N_DEV - 1)), pltpu.SemaphoreType.DMA((n, N_DEV - 1)),
                        pltpu.SemaphoreType.DMA((n,))],
        compiler_params=pltpu.CompilerParams(has_side_effects=True))(*arrays)


def exchange(name, arrays):
    n = len(arrays)

    def body(*refs):
        ins, outs = refs[:n], refs[n:2 * n]
        send_sems, recv_sems, local_sems = refs[2 * n:]
        me = _my_id()
        local = [pltpu.make_async_copy(ins[k].at[me], outs[k].at[me], local_sems.at[k]) for k in range(n)]
        for cp in local:
            cp.start()
        sends = []
        for j in range(1, N_DEV):
            peer, pid = _peer(j)
            for k in range(n):
                cp = pltpu.make_async_remote_copy(
                    src_ref=ins[k].at[pid], dst_ref=outs[k].at[me], send_sem=send_sems.at[k, j - 1],
                    recv_sem=recv_sems.at[k, j - 1], device_id=peer, device_id_type=pl.DeviceIdType.MESH)
                cp.start()
                sends.append(cp)
        for j in range(1, N_DEV):
            peer, pid = _peer(j)
            for k in range(n):
                pltpu.make_async_remote_copy(
                    src_ref=ins[k].at[me], dst_ref=outs[k].at[pid], send_sem=send_sems.at[k, j - 1],
                    recv_sem=recv_sems.at[k, j - 1], device_id=peer, device_id_type=pl.DeviceIdType.MESH).wait_recv()
        for cp in sends:
            cp.wait_send()
        for cp in local:
            cp.wait()

    any_spec = pl.BlockSpec(memory_space=pl.ANY)
    return pl.pallas_call(
        body, name=name, in_specs=[any_spec] * n, out_specs=[any_spec] * n,
        out_shape=[_sds(a.shape, a.dtype) for a in arrays],
        scratch_shapes=[pltpu.SemaphoreType.DMA((n, N_DEV - 1)), pltpu.SemaphoreType.DMA((n, N_DEV - 1)),
                        pltpu.SemaphoreType.DMA((n,))],
        compiler_params=pltpu.CompilerParams(has_side_effects=True))(*arrays)


_HBM = pl.BlockSpec(memory_space=pltpu.HBM)
_SEM = pl.BlockSpec(memory_space=pltpu.SEMAPHORE)
_DATAFLOW = pltpu.SideEffectType.DATAFLOW_SIDE_EFFECTING


def _in_hbm(v):
    return pltpu.with_memory_space_constraint(v, pltpu.HBM)


N_PEERS = N_DEV - 1


def _split_copy(k, j, srcs, lands, send_sems, recv_sems, gather, slot):
    peer, pid = _peer(j)
    return pltpu.make_async_remote_copy(
        src_ref=srcs[k] if gather else srcs[k].at[pid], dst_ref=lands[k].at[_my_id() if slot == "mine" else pid],
        send_sem=send_sems[j - 1], recv_sem=recv_sems[j - 1], device_id=peer, device_id_type=pl.DeviceIdType.MESH)


def split_start(name, arrays, gather):
    n = len(arrays)
    lands = [lax.empty((N_DEV,) + a.shape if gather else a.shape, a.dtype) for a in arrays]

    def body(*refs):
        srcs, lnds = refs[:n], refs[n:2 * n]
        sems = refs[4 * n:4 * n + 2 * N_PEERS]
        token = refs[-1]
        for j in range(1, N_DEV):
            for k in range(n):
                _split_copy(k, j, srcs, lnds, sems[:N_PEERS], sems[N_PEERS:], gather, "mine").start()
        token[...] = jnp.zeros_like(token)

    out = pl.pallas_call(
        body, name=name,
        out_shape=(*[pltpu.HBM(a.shape, a.dtype) for a in arrays], *[pltpu.HBM(l.shape, l.dtype) for l in lands],
                   *[pltpu.SemaphoreType.DMA(())] * (2 * N_PEERS), _sds((8, LANES), F32)),
        in_specs=[_HBM] * (2 * n),
        out_specs=(*[_HBM] * (2 * n), *[_SEM] * (2 * N_PEERS), pl.BlockSpec(memory_space=pltpu.VMEM)),
        input_output_aliases={k: k for k in range(2 * n)},
        compiler_params=pltpu.CompilerParams(has_side_effects=_DATAFLOW))(
            *[_in_hbm(a) for a in arrays], *[_in_hbm(l) for l in lands])
    sems = list(out[2 * n:2 * n + 2 * N_PEERS])
    return sems[:N_PEERS], sems[N_PEERS:], list(out[:n]), list(out[n:2 * n]), out[-1]


def split_wait(name, handle, after, gather):
    send_sems, recv_sems, srcs, lands, _ = handle
    n = len(srcs)

    def body(*refs):
        srcs_r, lnds_r = refs[:n], refs[n:2 * n]
        sems = refs[2 * n:2 * n + 2 * N_PEERS]
        for j in range(1, N_DEV):
            for k in range(n):
                cp = _split_copy(k, j, srcs_r, lnds_r, sems[:N_PEERS], sems[N_PEERS:], gather, "peer")
                cp.wait_send()
                cp.wait_recv()

    out = pl.pallas_call(
        body, name=name, out_shape=tuple(pltpu.HBM(a.shape, a.dtype) for a in srcs + lands),
        in_specs=[_HBM] * (2 * n) + [_SEM] * (2 * N_PEERS) + [pl.BlockSpec(memory_space=pl.ANY)],
        out_specs=tuple([_HBM] * (2 * n)), input_output_aliases={k: k for k in range(2 * n)},
        compiler_params=pltpu.CompilerParams(has_side_effects=_DATAFLOW))(
            *srcs, *lands, *send_sems, *recv_sems, after)
    return list(out[:n]), list(out[n:])


def _fill_own_slot(src, land, gather):
    me = _my_id()
    own = src[None] if gather else lax.dynamic_index_in_dim(src, me, 0, keepdims=True)
    return lax.dynamic_update_slice_in_dim(land, own, me, 0)


ADAMW_BLOCK_ELEMS = 128 * 1024


def adamw(name, parts, w, m, v):
    R, C = w.shape
    tr = _pick(R, [t for t in (512, 256, 128, 64, 32, 16, 8) if t * C <= ADAMW_BLOCK_ELEMS])
    c1 = 1.0 - ADAM_B1 ** ADAM_STEP
    c2 = 1.0 - ADAM_B2 ** ADAM_STEP

    def body(p_ref, w_ref, m_ref, v_ref, g_ref, d_ref, nm_ref, nv_ref):
        g = p_ref[0].astype(F32)
        for s in range(1, N_DEV):
            g = g + p_ref[s].astype(F32)
        g_ref[...] = g
        m_ = ADAM_B1 * m_ref[...] + (1.0 - ADAM_B1) * g
        v_ = ADAM_B2 * v_ref[...] + (1.0 - ADAM_B2) * (g * g)
        nm_ref[...] = m_
        nv_ref[...] = v_
        d_ref[...] = -ADAM_LR * ((m_ / c1) / (jnp.sqrt(v_ / c2) + ADAM_EPS) + ADAM_WD * w_ref[...])

    row = pl.BlockSpec((tr, C), lambda i: (i, 0))
    return pl.pallas_call(
        body, name=name, grid=(R // tr,),
        in_specs=[pl.BlockSpec((N_DEV, tr, C), lambda i: (0, i, 0)), row, row, row], out_specs=[row] * 4,
        out_shape=[_sds((R, C), F32)] * 4, compiler_params=_cp())(parts, w, m, v)


WEIGHTS = ['ev_norm', 'ev_w_in', 'ev_pool_w', 'ev_pool_scale', 'ev_q_norm', 'ev_w_q_up', 'ev_kv_norm', 'ev_w_kv_up',
           'ev_w_out', 'od_norm', 'od_w_in', 'od_conv_w', 'od_conv_b', 'od_w_rgate', 'od_b_rgate', 'od_w_igate',
           'od_b_igate', 'od_lambda', 'od_w_out', 'xa_norm_x', 'xa_norm_mem', 'xa_w_q', 'xa_w_kv', 'xa_w_o',
           'ffn_norm', 'ffn_w_gate_up', 'ffn_w_down', 'final_norm']
SHARD_AXIS = {'ev_w_in': 1, 'ev_w_q_up': 2, 'ev_w_kv_up': 2, 'ev_w_out': 1, 'od_norm': 1, 'od_w_in': 2,
              'od_conv_w': 2, 'od_conv_b': 1, 'od_w_rgate': 2, 'od_b_rgate': 1, 'od_w_igate': 2, 'od_b_igate': 1,
              'od_lambda': 1, 'od_w_out': 1, 'xa_w_q': 1, 'xa_w_kv': 2, 'xa_w_o': 1, 'ffn_w_gate_up': 2,
              'ffn_w_down': 1}
SMALL_F32 = ('od_norm', 'od_conv_w', 'od_conv_b', 'od_b_rgate', 'od_b_igate', 'od_lambda')
STACKED = ('ffn_w_gate_up', 'ffn_w_down')
SHARDED = [n for n in WEIGHTS if n in SHARD_AXIS]
REPLICATED = [n for n in WEIGHTS if n not in SHARD_AXIS]
ROW_ALIGN = 512


def _pack(flats, dtype):
    v = jnp.concatenate([f.reshape(-1).astype(dtype) for f in flats])
    pad = (-v.shape[0]) % (ROW_ALIGN * LANES)
    return jnp.pad(v, (0, pad)).reshape(-1, LANES)


def _rows8(n_elems):
    return -(-n_elems // (8 * LANES)) * 8


def _pack_rows(arrays, lead=False):
    out = []
    for a in arrays:
        r = a.reshape((N_DEV, -1, LANES) if lead else (-1, LANES))
        pad = _rows8(r.shape[-2] * LANES) - r.shape[-2]
        out.append(jnp.pad(r, [(0, 0)] * (r.ndim - 2) + [(0, pad), (0, 0)]))
    return jnp.concatenate(out, axis=-2)


def _unpack_rows(buf, shapes, lead=False):
    out, off = [], 0
    for s in shapes:
        n = 1
        for d in s:
            n *= d
        rows = buf[..., off:off + n // LANES, :]
        out.append(rows.reshape(((N_DEV,) if lead else ()) + tuple(s)))
        off += _rows8(n)
    return out


def _unpack(flat, shapes):
    out, off = [], 0
    v = flat.reshape(-1)
    for s in shapes:
        n = 1
        for d in s:
            n *= d
        out.append(v[off:off + n].reshape(s))
        off += n
    return out


def _to_full(stacked, axis):
    v = jnp.moveaxis(stacked, 0, axis)
    s = v.shape
    return v.reshape(s[:axis] + (s[axis] * s[axis + 1],) + s[axis + 2:])


def _to_shards(full, axis):
    s = full.shape
    v = full.reshape(s[:axis] + (N_DEV, s[axis] // N_DEV) + s[axis + 1:])
    return jnp.moveaxis(v, axis, 0)


def _pad_heads(w, nh, dh, lead):
    s = w.shape
    v = w.reshape(s[:-1] + (nh, dh))
    v = jnp.pad(v, [(0, 0)] * (len(s) - 1) + [(0, 0), (lead, LANES - dh - lead)])
    return v.reshape(s[:-1] + (nh * LANES,))


def _unpad_heads(w, nh, dh, lead):
    s = w.shape
    return w.reshape(s[:-1] + (nh, LANES))[..., lead:lead + dh].reshape(s[:-1] + (nh * dh,))


def _rope_tables(positions):
    inv_freq = 10000.0 ** (-jnp.arange(0, 32, 2, dtype=F32) / 32)
    ang = positions.astype(F32)[:, None] * inv_freq
    cos, sin = jnp.cos(ang), jnp.sin(ang)
    S = positions.shape[0]
    one, zero = jnp.ones((S, 64), F32), jnp.zeros((S, 64), F32)
    z16, z32 = jnp.zeros((S, 16), F32), jnp.zeros((S, 32), F32)
    c = jnp.concatenate([one, cos, cos, jnp.ones((S, 32), F32)], axis=1)
    a = jnp.concatenate([zero, z16, sin, z32], axis=1)
    b = jnp.concatenate([zero, -sin, z16, z32], axis=1)
    return c, a, b


def _t(w):
    return jnp.swapaxes(w, -1, -2)


def _col_to_row(v, tq):
    S = v.shape[0]
    return v[:, ::LANES].T.reshape(MLA_HEADS, S // tq, 1, tq)


def device_step(x, mem, positions, target, W, fwd_token=None, late_weights=None, ship_grads=None):
    S = x.shape[0]
    G = {}
    tabs = _rope_tables(positions)
    keep = (positions != 0).astype(F32)[:, None]
    row = lambda v: v.reshape(1, -1)

    w_in = W['ev_w_in'][0]
    ev_win = jnp.concatenate([w_in[:, :896], _pad_heads(w_in[:, 896:], 1, 32, 64)], axis=1)
    ev_wq = _pad_heads(W['ev_w_q_up'][0], MLA_HEADS, QK_DIM, 0)
    kvw = W['ev_w_kv_up'][0].reshape(KV_LORA, MLA_HEADS, 128)
    ev_wk = _pad_heads(kvw[:, :, :64].reshape(KV_LORA, 512), MLA_HEADS, 64, 0)
    ev_wv = _pad_heads(kvw[:, :, 64:].reshape(KV_LORA, 512), MLA_HEADS, 64, 0)
    ev_wo_pool = W['ev_w_out'][0][:POOL_DIM]
    ev_wo_att = _t(_pad_heads(_t(W['ev_w_out'][0][POOL_DIM:]), MLA_HEADS, 64, 0))
    pw = W['ev_pool_w'][0].astype(BF16)
    ev_g, ps, qg, kvg = row(W['ev_norm'][0]), row(W['ev_pool_scale'][0]), row(W['ev_q_norm'][0]), row(W['ev_kv_norm'][0])

    z0, qp, kp, vp, ypool = even_pre(x, tabs, ev_g, ev_win, pw, ps, qg, ev_wq, kvg, ev_wk, ev_wv)
    o_att, lse = attn_fwd(qp, kp, vp, fwd_token)
    if late_weights is not None:
        W = {**W, **late_weights(lse)}
    x1 = even_post(x, ypool, o_att, ev_wo_pool, ev_wo_att)

    def xa_ffn_fwd(xin, l):
        mn, km, vm = mem_kv(mem, row(W['xa_norm_mem'][l]), W['xa_w_kv'][l])
        xm = xattn_fwd(xin, row(W['xa_norm_x'][l]), W['xa_w_q'][l], km, vm, W['xa_w_o'][l])
        xo = ffn_fwd(xm, row(W['ffn_norm'][l]), W['ffn_w_gate_up'], l,
                     W['ffn_w_down'][:, l].reshape(FF_HALF, FF_CHUNK, D))
        return xm, xo, (mn, km, vm)

    x2, x3, memkv0 = xa_ffn_fwd(x1, 0)

    od_g, lam = row(W['od_norm'][0]), row(W['od_lambda'][0])
    cw, cb = W['od_conv_w'][0], row(W['od_conv_b'][0])
    wr, wi = W['od_w_rgate'][0], W['od_w_igate'][0]
    br, bi = row(W['od_b_rgate'][0]), row(W['od_b_igate'][0])
    z1, a_t, b_t = odd_pre(x3, keep, od_g, W['od_w_in'][0], cw, cb, wr, br, wi, bi, lam)
    hseq = lru_scan(a_t, b_t)
    x4 = odd_post(x3, z1, hseq, W['od_w_out'][0])
    x5, x6, memkv1 = xa_ffn_fwd(x4, 1)

    dx, G['final_norm'], loss = loss_head(x6, target, row(W['final_norm']))
    G['final_norm'] = G['final_norm'].reshape(D)

    gnx, gnm, gwq, gwkv, gwo, gfn, gwgu, gwd = ([None, None] for _ in range(8))

    def xa_ffn_bwd(dy, xin, xm, memkv, l):
        mn, km, vm = memkv
        fg = row(W['ffn_norm'][l])
        hf, act, dgu = ffn_bwd_a(xm, dy, fg, W['ffn_w_gate_up'], l,
                                 _t(W['ffn_w_down'][:, l].reshape(FF_HALF, FF_CHUNK, D)))
        gwd[l] = matmul_tn("ffn_dwd", act, dy).reshape(N_DEV, D_FF // N_DEV, D)
        gwgu[l] = matmul_tn("ffn_dwgu", hf, dgu)
        dxm, dfg = ffn_bwd_b(xm, dy, dgu, fg, _t(W['ffn_w_gate_up'][:, l]))
        gfn[l] = dfg[0]
        dxin, o, dq, hx, dgx, dk, dv = xattn_bwd(xin, dxm, row(W['xa_norm_x'][l]), W['xa_w_q'][l],
                                                  _t(W['xa_w_q'][l]), km, vm, _t(W['xa_w_o'][l]))
        gnx[l] = dgx[0]
        gwo[l] = matmul_tn("xa_dwo", o, dxm)
        gwq[l] = matmul_tn("xa_dwq", hx, dq)
        dkv, dgm = mem_bwd(mem, row(W['xa_norm_mem'][l]), dk, dv, _t(W['xa_w_kv'][l]))
        gnm[l] = dgm[0]
        gwkv[l] = matmul_tn("xa_dwkv", mn, dkv)
        return dxin

    dx4 = xa_ffn_bwd(dx, x4, x5, memkv1, 1)

    y_od, dgate, dhs = odd_post_bwd(dx4, z1, hseq, _t(W['od_w_out'][0]))
    G['od_w_out'] = matmul_tn("od_dwout", y_od, dx4)[None]
    lam_grad = lru_scan(a_t, dhs, reverse=True)
    dxb, dcb, dbr, dbi, dlam, dwr, dwi = odd_gates_bwd(z1, lam_grad, hseq, keep, cw, cb, wr, _t(wr), br, wi, _t(wi),
                                                        bi, lam)
    dx3, h_od, dz1, dcw, dg_od = odd_pre_bwd(x3, dx4, z1, dxb, dgate, od_g, cw, _t(W['od_w_in'][0]))
    G['od_w_in'] = matmul_tn("od_dwin", h_od, dz1)[None]
    G['od_norm'], G['od_conv_w'], G['od_conv_b'] = dg_od, dcw[None], dcb
    G['od_w_rgate'], G['od_b_rgate'], G['od_w_igate'], G['od_b_igate'], G['od_lambda'] = (
        dwr[None], dbr, dwi[None], dbi, dlam)

    dx1 = xa_ffn_bwd(dx3, x1, x2, memkv0, 0)
    G['xa_norm_x'], G['xa_norm_mem'], G['ffn_norm'] = jnp.stack(gnx), jnp.stack(gnm), jnp.stack(gfn)
    G['xa_w_q'], G['xa_w_kv'], G['xa_w_o'] = jnp.stack(gwq), jnp.stack(gwkv), jnp.stack(gwo)
    G['ffn_w_gate_up'], G['ffn_w_down'] = jnp.stack(gwgu, axis=1), jnp.stack(gwd, axis=1)
    bwd_token = ship_grads(G) if ship_grads is not None else None

    dyp, do_att, delta = even_post_bwd(dx1, o_att, _t(ev_wo_pool), _t(ev_wo_att))
    g_wo_pool = matmul_tn("ev_dwo_pool", ypool, dx1)
    g_wo_att = matmul_tn("ev_dwo_att", o_att, dx1)
    G['ev_w_out'] = jnp.concatenate([g_wo_pool, _t(_unpad_heads(_t(g_wo_att), MLA_HEADS, 64, 0))], axis=0)[None]
    tq = _tile_rows(S, 512)
    dq, dk, dv = attn_bwd(qp, kp, vp, do_att, _col_to_row(lse, tq), _col_to_row(delta, tq), bwd_token)
    (grad_x, h_ev, dz0, dg_ev, dpw, dps, dqg, dwq, dkvg, dwk, dwv) = even_pre_bwd(
        x, dx1, z0, dq, dk, dv, dyp, tabs, ev_g, _t(ev_win), pw, _t(pw), ps, qg, _t(ev_wq), kvg, _t(ev_wk),
        _t(ev_wv))
    g_win = matmul_tn("ev_dwin", h_ev, dz0)
    G['ev_w_in'] = jnp.concatenate([g_win[:, :896], _unpad_heads(g_win[:, 896:], 1, 32, 64)], axis=1)[None]
    G['ev_norm'], G['ev_pool_w'], G['ev_pool_scale'], G['ev_q_norm'], G['ev_kv_norm'] = (
        dg_ev, dpw[None], dps, dqg, dkvg)
    G['ev_w_q_up'] = _unpad_heads(dwq, MLA_HEADS, QK_DIM, 0)[None]
    gk = _unpad_heads(dwk, MLA_HEADS, 64, 0).reshape(KV_LORA, MLA_HEADS, 64)
    gv = _unpad_heads(dwv, MLA_HEADS, 64, 0).reshape(KV_LORA, MLA_HEADS, 64)
    G['ev_w_kv_up'] = jnp.concatenate([gk, gv], axis=2).reshape(1, KV_LORA, MLA_HEADS * 128)
    return loss[0, 0], grad_x, G


def kernel(x, mem, positions, ev_norm, ev_w_in, ev_pool_w, ev_pool_scale, ev_q_norm, ev_w_q_up, ev_kv_norm, ev_w_kv_up, ev_w_out, od_norm, od_w_in, od_conv_w, od_conv_b, od_w_rgate, od_b_rgate, od_w_igate, od_b_igate, od_lambda, od_w_out, xa_norm_x, xa_norm_mem, xa_w_q, xa_w_kv, xa_w_o, ffn_norm, ffn_w_gate_up, ffn_w_down, final_norm, loss_target, m_ev_norm, m_ev_w_in, m_ev_pool_w, m_ev_pool_scale, m_ev_q_norm, m_ev_w_q_up, m_ev_kv_norm, m_ev_w_kv_up, m_ev_w_out, m_od_norm, m_od_w_in, m_od_conv_w, m_od_conv_b, m_od_w_rgate, m_od_b_rgate, m_od_w_igate, m_od_b_igate, m_od_lambda, m_od_w_out, m_xa_norm_x, m_xa_norm_mem, m_xa_w_q, m_xa_w_kv, m_xa_w_o, m_ffn_norm, m_ffn_w_gate_up, m_ffn_w_down, m_final_norm, v_ev_norm, v_ev_w_in, v_ev_pool_w, v_ev_pool_scale, v_ev_q_norm, v_ev_w_q_up, v_ev_kv_norm, v_ev_w_kv_up, v_ev_w_out, v_od_norm, v_od_w_in, v_od_conv_w, v_od_conv_b, v_od_w_rgate, v_od_b_rgate, v_od_w_igate, v_od_b_igate, v_od_lambda, v_od_w_out, v_xa_norm_x, v_xa_norm_mem, v_xa_w_q, v_xa_w_kv, v_xa_w_o, v_ffn_norm, v_ffn_w_gate_up, v_ffn_w_down, v_final_norm):
    args = dict(locals())
    w = {n: args[n] for n in WEIGHTS}
    m = {n: args['m_' + n] for n in WEIGHTS}
    v = {n: args['v_' + n] for n in WEIGHTS}
    big = [n for n in SHARDED if n not in SMALL_F32]
    small = [n for n in SHARDED if n in SMALL_F32]

    small_shapes = [w[n].shape for n in small]
    first = [n for n in big if n.startswith('ev_')]
    late = [n for n in big if n not in first]

    def full(n, st):
        return st if n in STACKED else _to_full(st, SHARD_AXIS[n])

    W = {n: w[n] for n in REPLICATED}
    W.update((n, full(n, st)) for n, st in zip(first, all_gather("gather_ev_weights", [w[n].astype(BF16) for n in first])))
    gather = split_start("gather_start", [w[n].astype(BF16) for n in late] + [_pack_rows([w[n] for n in small])], True)

    def late_weights(after):
        srcs, lands = split_wait("gather_wait", gather, after, True)
        lands = [_fill_own_slot(s, l, True) for s, l in zip(srcs, lands)]
        out = {n: full(n, st) for n, st in zip(late, lands)}
        out.update((n, _to_full(st, SHARD_AXIS[n])) for n, st in zip(small, _unpack_rows(lands[-1], small_shapes, True)))
        return out

    def shards(G, n):
        return G[n] if n in STACKED else _to_shards(G[n], SHARD_AXIS[n])

    shipped = []

    def ship_grads(G):
        shipped.append(split_start("exchange_start", [shards(G, n).astype(BF16) for n in late] +
                                   [_pack_rows([shards(G, n) for n in small], lead=True)], False))
        return shipped[0][-1]

    loss, grad_x, G = device_step(x[0], mem[0], positions[0], loss_target[0], W, gather[-1], late_weights, ship_grads)
    outs = [{}, {}, {}, {}]

    rep_shapes = [w[n].shape for n in REPLICATED] + [(LANES,)]
    zero = jnp.zeros((LANES,), F32)
    rep_parts, = all_gather("gather_rep_grads", [_pack(
        [G[n] for n in REPLICATED] + [jnp.broadcast_to(loss, (LANES,))], F32)])
    rep = adamw("adamw_rep", rep_parts, *[_pack([d[n] for n in REPLICATED] + [zero], F32) for d in (w, m, v)])
    for k in range(4):
        outs[k].update(zip(REPLICATED + ['loss'], _unpack(rep[k], rep_shapes)))
    loss = outs[0]['loss'][0]

    srcs, lands = split_wait("exchange_wait", shipped[0], grad_x, False)
    late_parts = [_fill_own_slot(s, l, False) for s, l in zip(srcs, lands)]
    parts = list(exchange("exchange_ev_grads", [shards(G, n).astype(BF16) for n in first])) + late_parts
    two_d = lambda a: a.reshape(-1, a.shape[-1])
    for n, p in zip(first + late, parts):
        res = adamw("adamw_" + n, p.reshape((N_DEV,) + two_d(w[n]).shape), two_d(w[n]), two_d(m[n]), two_d(v[n]))
        for k in range(4):
            outs[k][n] = res[k].reshape(w[n].shape)
    res = adamw("adamw_small", parts[-1], *[_pack_rows([d[n] for n in small]) for d in (w, m, v)])
    for k in range(4):
        outs[k].update(zip(small, _unpack_rows(res[k], small_shapes)))

    return (loss, grad_x[None], *[outs[0][n] for n in WEIGHTS], *[outs[1][n] for n in WEIGHTS],
            *[outs[2][n] for n in WEIGHTS], *[outs[3][n] for n in WEIGHTS])
```

```python
import functools

import jax
import jax.numpy as jnp
from jax import lax
from jax.experimental import pallas as pl
from jax.experimental.pallas import tpu as pltpu

F32, BF16 = jnp.float32, jnp.bfloat16
N_DEV = 8
D = 1024
POOL_DIM = 512
POOL_WINDOWS = (2, 4, 8, 16)
MLA_HEADS = 8
QK_DIM = 96
Q_LORA, KV_LORA = 256, 128
LRU_HEADS, LRU_HEAD_DIM = 4, 256
LRU_C = 8.0
MEM_HEADS, MEM_HEAD_DIM = 4, 256
D_FF = 2816
RMS_EPS = 1e-6
ADAM_LR, ADAM_B1, ADAM_B2, ADAM_EPS, ADAM_WD, ADAM_STEP = 0.001, 0.9, 0.999, 1e-08, 0.01, 10
LANES = 128
POOL_HALO = 16
CONV_HALO = 8
VMEM_LIMIT = 60000 * 1024


def _cp():
    return pltpu.CompilerParams(dimension_semantics=("arbitrary",), vmem_limit_bytes=VMEM_LIMIT)


def _cp2():
    return pltpu.CompilerParams(dimension_semantics=("arbitrary", "arbitrary"), vmem_limit_bytes=VMEM_LIMIT)


def _row(ts, c, col=0):
    return pl.BlockSpec((ts, c), lambda i: (i, col))


def _prev(hr, c, ts, col=0):
    r = ts // hr
    return pl.BlockSpec((hr, c), lambda i: (jnp.maximum(i * r - 1, 0), col))


def _next(hr, c, ts, n, col=0):
    r = ts // hr
    return pl.BlockSpec((hr, c), lambda i: (jnp.minimum((i + 1) * r, n * r - 1), col))


def _const(shape):
    nd = len(shape)
    return pl.BlockSpec(tuple(shape), lambda i: (0,) * nd, pipeline_mode=pl.Buffered(1))


def _acc(shape):
    nd = len(shape)
    return pl.BlockSpec(tuple(shape), lambda i: (0,) * nd)


def _sds(shape, dt):
    return jax.ShapeDtypeStruct(tuple(shape), dt)


def _dot(a, b):
    return jnp.dot(a.astype(BF16), b.astype(BF16), preferred_element_type=F32)


def _dot_nt(a, b):
    return lax.dot_general(a.astype(BF16), b.astype(BF16), (((1,), (1,)), ((), ())), preferred_element_type=F32)


def _dot_tn(a, b):
    return lax.dot_general(a.astype(BF16), b.astype(BF16), (((0,), (0,)), ((), ())), preferred_element_type=F32)


def _rms(x, g):
    rstd = lax.rsqrt(jnp.mean(x * x, axis=-1, keepdims=True) + RMS_EPS)
    return x * rstd * g, rstd


def _rms_bwd(x, g, rstd, dy):
    xn = x * rstd
    dyg = dy * g
    dx = rstd * (dyg - xn * jnp.mean(dyg * xn, axis=-1, keepdims=True))
    return dx, dy * xn


def _rowsum(v):
    return jnp.sum(v, axis=0, keepdims=True)


def _roll(v, s, axis):
    n = v.shape[axis]
    return pltpu.roll(v, s % n, axis)


def _rope(t, c, a, b):
    k = t.shape[1] // LANES
    if k > 1:
        c, a, b = (jnp.tile(v, (1, k)) for v in (c, a, b))
    return t * c + _roll(t, 16, 1) * a + _roll(t, -16, 1) * b


def _rope_bwd(d, c, a, b):
    k = d.shape[1] // LANES
    if k > 1:
        c, a, b = (jnp.tile(v, (1, k)) for v in (c, a, b))
    return d * c + _roll(d * a, -16, 1) + _roll(d * b, 16, 1)


def _gelu(x):
    c = 0.7978845608028654
    t = jnp.tanh(c * (x + 0.044715 * x * x * x))
    return 0.5 * x * (1.0 + t), t


def _gelu_grad(x, t):
    c = 0.7978845608028654
    return 0.5 * (1.0 + t) + 0.5 * x * (1.0 - t * t) * c * (1.0 + 3.0 * 0.044715 * x * x)


def _blockdot(v, w_ref, nblk, width):
    return jnp.concatenate(
        [_dot(v[:, j * width:(j + 1) * width], w_ref[j]) for j in range(nblk)], axis=1)


def _pool_cnt(row0, rows):
    t = row0 + lax.broadcasted_iota(jnp.int32, (rows, POOL_DIM), 0)
    w = jnp.left_shift(2, lax.broadcasted_iota(jnp.int32, (rows, POOL_DIM), 1) // LANES)
    return jnp.minimum(t + 1, w).astype(F32)


def _pool_windows(ext, sign):
    s2 = ext + _roll(ext, sign * 1, 0)
    t = s2[:, LANES:]
    s4 = t + _roll(t, sign * 2, 0)
    t = s4[:, LANES:]
    s8 = t + _roll(t, sign * 4, 0)
    t = s8[:, LANES:]
    s16 = t + _roll(t, sign * 8, 0)
    return jnp.concatenate([s2[:, :LANES], s4[:, :LANES], s8[:, :LANES], s16], axis=1)


def _pooled(uprev, u, row0):
    ts = u.shape[0]
    ext = jnp.concatenate([uprev, u], axis=0)
    sums = _pool_windows(ext, 1)[POOL_HALO:]
    return sums / _pool_cnt(row0, ts) - u


def _expm1(x):
    return jnp.where(jnp.abs(x) < 0.01, x * (1.0 + 0.5 * x * (1.0 + x * (1.0 / 3.0))), jnp.exp(x) - 1.0)


def _softplus(z):
    return jnp.maximum(z, 0.0) + jnp.log1p(jnp.exp(-jnp.abs(z)))


def _tile_rows(s, want):
    while s % want:
        want //= 2
    return want


def even_pre(x, tabs, g, win, pw, pscale, qg, wq, kvg, wk, wv):
    S = x.shape[0]
    ts = _tile_rows(S, 512)

    def body(x_ref, xp_ref, c_ref, a_ref, b_ref, g_ref, win_ref, pw_ref, ps_ref, qg_ref, wq_ref, kvg_ref,
             wk_ref, wv_ref, z_ref, q_ref, k_ref, v_ref, yp_ref):
        i = pl.program_id(0)
        h, _ = _rms(x_ref[...], g_ref[...])
        z = _dot(h, win_ref[...])
        z_ref[...] = z
        hp, _ = _rms(xp_ref[...], g_ref[...])
        uprev = _dot(hp, win_ref[:, :POOL_DIM]) * (i > 0).astype(F32)
        u = z[:, :POOL_DIM]
        pooled = _pooled(uprev, u, i * ts)
        yp_ref[...] = (_blockdot(pooled, pw_ref, 4, LANES) * ps_ref[...]).astype(BF16)
        c, a, b = c_ref[...], a_ref[...], b_ref[...]
        cqn, _ = _rms(z[:, 512:768], qg_ref[...])
        q_ref[...] = (_rope(_dot(cqn, wq_ref[...]), c, a, b) * (ATTN_SCALE * LOG2_E)).astype(BF16)
        ckvn, _ = _rms(z[:, 768:896], kvg_ref[...])
        krr = _rope(z[:, 896:1024], c, a, b)
        k_ref[...] = (_dot(ckvn, wk_ref[...]) + jnp.tile(krr, (1, MLA_HEADS))).astype(BF16)
        lane = lax.broadcasted_iota(jnp.int32, (ts, D), 1) % LANES
        v_ref[...] = jnp.where(lane == ONES_LANE, 1.0, _dot(ckvn, wv_ref[...])).astype(BF16)

    ins = [x, x, *tabs, g, win, pw, pscale, qg, wq, kvg, wk, wv]
    in_specs = [_row(ts, D), _prev(POOL_HALO, D, ts), _row(ts, LANES), _row(ts, LANES), _row(ts, LANES)]
    in_specs += [_const(v.shape) for v in ins[5:]]
    return pl.pallas_call(
        body, name="even_pre", grid=(S // ts,), in_specs=in_specs,
        out_specs=[_row(ts, D)] * 4 + [_row(ts, POOL_DIM)],
        out_shape=[_sds((S, D), F32)] + [_sds((S, D), BF16)] * 3 + [_sds((S, POOL_DIM), BF16)],
        compiler_params=_cp())(*ins)


ATTN_SCALE = QK_DIM ** -0.5
LOG2_E = 1.4426950408889634
LN_2 = 0.6931471805599453
ONES_LANE = 64


def _exp2(x):
    return jnp.exp2(x)


def _pair_loop(lo, hi, step, init, unroll=2):
    groups = (hi - lo) // unroll

    def group(j, c):
        for u in range(unroll):
            c = step(lo + unroll * j + u, c)
        return c

    carry = lax.fori_loop(0, groups, group, init)
    return lax.fori_loop(lo + unroll * groups, hi, step, carry)


def _as_row(col):
    return jnp.transpose(jnp.broadcast_to(col, (col.shape[0], LANES)))[0:1, :]


def _after(token):
    return ([], []) if token is None else ([token], [pl.BlockSpec(memory_space=pl.ANY)])


def attn_fwd(qp, kp, vp, token=None):
    S = qp.shape[0]
    tq = _tile_rows(S, 512)
    extra, extra_specs = _after(token)

    def body(q_ref, k_ref, v_ref, *rest):
        o_ref, lse_ref = rest[-2:]
        qi = pl.program_id(1)
        q = q_ref[...]

        def block(ki, carry, masked):
            m, acc = carry
            off = pl.multiple_of(ki * tq, tq)
            s = _dot_nt(q, k_ref[pl.ds(off, tq), :])
            if masked:
                row = lax.broadcasted_iota(jnp.int32, (tq, tq), 0)
                col = lax.broadcasted_iota(jnp.int32, (tq, tq), 1)
                s = jnp.where(col <= row, s, -1e30)
            m_new = jnp.maximum(m, jnp.max(s, axis=1, keepdims=True))
            acc = _exp2(m - m_new) * acc + _dot(_exp2(s - m_new), v_ref[pl.ds(off, tq), :])
            return m_new, acc

        init = (jnp.full((tq, 1), -1e30, F32), jnp.zeros((tq, LANES), F32))
        carry = _pair_loop(0, qi, lambda ki, c: block(ki, c, False), init, unroll=4)
        m, acc = block(qi, carry, True)
        l = acc[:, ONES_LANE:ONES_LANE + 1]
        o_ref[...] = acc / l
        lse_ref[...] = _as_row(m + jnp.log(l) * LOG2_E)

    blk = pl.BlockSpec((tq, LANES), lambda h, i: (i, h))
    full = pl.BlockSpec((S, LANES), lambda h, i: (0, h))
    return pl.pallas_call(
        body, name="attn_fwd", grid=(MLA_HEADS, S // tq), in_specs=[blk, full, full] + extra_specs,
        out_specs=[blk, pl.BlockSpec((None, None, 1, tq), lambda h, i: (h, i, 0, 0))],
        out_shape=[_sds((S, D), F32), _sds((MLA_HEADS, S // tq, 1, tq), F32)], compiler_params=_cp2())(
            qp, kp, vp, *extra)


def even_post(x, ypool, o, wo_pool, wo_att):
    S = x.shape[0]
    ts = _tile_rows(S, 512)

    def body(x_ref, yp_ref, o_ref, wp_ref, wa_ref, out_ref):
        out_ref[...] = x_ref[...] + _dot(yp_ref[...], wp_ref[...]) + _dot(o_ref[...], wa_ref[...])

    return pl.pallas_call(
        body, name="even_post", grid=(S // ts,),
        in_specs=[_row(ts, D), _row(ts, POOL_DIM), _row(ts, D), _const(wo_pool.shape), _const(wo_att.shape)],
        out_specs=_row(ts, D), out_shape=_sds((S, D), F32), compiler_params=_cp())(x, ypool, o, wo_pool, wo_att)


def mem_kv(mem, g, wkv):
    M = mem.shape[0]

    def body(mem_ref, g_ref, w_ref, mn_ref, k_ref, v_ref):
        mn, _ = _rms(mem_ref[...], g_ref[...])
        mn_ref[...] = mn.astype(BF16)
        k_ref[...] = _dot(mn, w_ref[:, :D]).astype(BF16)
        v_ref[...] = _dot(mn, w_ref[:, D:]).astype(BF16)

    return pl.pallas_call(
        body, name="mem_kv", grid=(1,), in_specs=[_acc(mem.shape), _acc(g.shape), _acc(wkv.shape)],
        out_specs=[_acc((M, D))] * 3, out_shape=[_sds((M, D), BF16)] * 3, compiler_params=_cp())(mem, g, wkv)


def _xattn_heads(hx, wq_ref, k_ref, v_ref):
    q = _dot(hx, wq_ref[...])
    scale = MEM_HEAD_DIM ** -0.5
    ps, os_ = [], []
    for h in range(MEM_HEADS):
        sl = slice(h * MEM_HEAD_DIM, (h + 1) * MEM_HEAD_DIM)
        s = _dot_nt(q[:, sl], k_ref[:, sl]) * scale
        e = jnp.exp(s - jnp.max(s, axis=1, keepdims=True))
        p = e / jnp.sum(e, axis=1, keepdims=True)
        ps.append(p)
        os_.append(_dot(p, v_ref[:, sl]))
    return q, ps, jnp.concatenate(os_, axis=1)


def xattn_fwd(x, g, wq, kmem, vmem, wo):
    S = x.shape[0]
    ts = _tile_rows(S, 512)

    def body(x_ref, g_ref, wq_ref, k_ref, v_ref, wo_ref, out_ref):
        x_ = x_ref[...]
        hx, _ = _rms(x_, g_ref[...])
        _, _, o = _xattn_heads(hx, wq_ref, k_ref, v_ref)
        out_ref[...] = x_ + _dot(o, wo_ref[...])

    ins = [x, g, wq, kmem, vmem, wo]
    return pl.pallas_call(
        body, name="xattn_fwd", grid=(S // ts,), in_specs=[_row(ts, D)] + [_const(v.shape) for v in ins[1:]],
        out_specs=_row(ts, D), out_shape=_sds((S, D), F32), compiler_params=_cp())(*ins)


def xattn_bwd(x, dy, g, wq, kmem, vmem, wo):
    S = x.shape[0]
    M = kmem.shape[0]
    ts = _tile_rows(S, 512)
    scale = MEM_HEAD_DIM ** -0.5

    def body(x_ref, dy_ref, g_ref, wq_ref, k_ref, v_ref, wo_ref,
             dx_ref, o_ref, dq_ref, hx_ref, dg_ref, dk_ref, dv_ref):
        i = pl.program_id(0)

        @pl.when(i == 0)
        def _():
            dg_ref[...] = jnp.zeros_like(dg_ref)
            dk_ref[...] = jnp.zeros_like(dk_ref)
            dv_ref[...] = jnp.zeros_like(dv_ref)

        x_, dy_ = x_ref[...], dy_ref[...]
        hx, rstd = _rms(x_, g_ref[...])
        q, ps, o = _xattn_heads(hx, wq_ref, k_ref, v_ref)
        hx_ref[...] = hx.astype(BF16)
        o_ref[...] = o.astype(BF16)
        do = _dot_nt(dy_, wo_ref[...])
        dqs = []
        for h in range(MEM_HEADS):
            sl = slice(h * MEM_HEAD_DIM, (h + 1) * MEM_HEAD_DIM)
            p, do_h = ps[h], do[:, sl]
            dp = _dot_nt(do_h, v_ref[:, sl])
            ds = p * (dp - jnp.sum(p * dp, axis=1, keepdims=True)) * scale
            dqs.append(_dot(ds, k_ref[:, sl]))
            dk_ref[:, sl] += _dot_tn(ds, q[:, sl])
            dv_ref[:, sl] += _dot_tn(p, do_h)
        dq = jnp.concatenate(dqs, axis=1)
        dq_ref[...] = dq.astype(BF16)
        dxn, dgr = _rms_bwd(x_, g_ref[...], rstd, _dot_nt(dq, wq_ref[...]))
        dx_ref[...] = dy_ + dxn
        dg_ref[...] += _rowsum(dgr)

    ins = [x, dy, g, wq, kmem, vmem, wo]
    return pl.pallas_call(
        body, name="xattn_bwd", grid=(S // ts,),
        in_specs=[_row(ts, D), _row(ts, D)] + [_const(v.shape) for v in ins[2:]],
        out_specs=[_row(ts, D)] * 4 + [_acc((1, D)), _acc((M, D)), _acc((M, D))],
        out_shape=[_sds((S, D), F32)] + [_sds((S, D), BF16)] * 3 + [_sds((1, D), F32), _sds((M, D), F32),
                                                                    _sds((M, D), F32)],
        compiler_params=_cp())(*ins)


def mem_bwd(mem, g, dk, dv, wkv):
    M = mem.shape[0]

    def body(mem_ref, g_ref, dk_ref, dv_ref, w_ref, dkv_ref, dg_ref):
        dkv = jnp.concatenate([dk_ref[...], dv_ref[...]], axis=1)
        dkv_ref[...] = dkv.astype(BF16)
        _, rstd = _rms(mem_ref[...], g_ref[...])
        dg_ref[...] = _rowsum(_dot_nt(dkv, w_ref[...]) * (mem_ref[...] * rstd))

    ins = [mem, g, dk, dv, wkv]
    return pl.pallas_call(
        body, name="mem_bwd", grid=(1,), in_specs=[_acc(v.shape) for v in ins],
        out_specs=[_acc((M, 2 * D)), _acc((1, D))], out_shape=[_sds((M, 2 * D), BF16), _sds((1, D), F32)],
        compiler_params=_cp())(*ins)


FF_CHUNK = 2 * D_FF // N_DEV
FF_HALF = N_DEV // 2


def _layer_of(w, layer):
    return pl.BlockSpec((N_DEV, None) + w.shape[2:], lambda i: (0, layer, 0, 0), pipeline_mode=pl.Buffered(1))


def _ff_chunks(c, ts):
    return pl.BlockSpec((c, ts, FF_CHUNK), lambda i: (0, i, 0))


def ffn_fwd(x, g, wgu, layer, wd):
    S = x.shape[0]
    ts = _tile_rows(S, 256)

    def body(x_ref, g_ref, wgu_ref, wd_ref, out_ref, hf_ref, gu_ref):
        x_ = x_ref[...]
        hf = _rms(x_, g_ref[...])[0].astype(BF16)
        hf_ref[...] = hf
        out = x_
        for j in range(FF_HALF):
            gg, uu = _dot(hf, wgu_ref[j]), _dot(hf, wgu_ref[j + FF_HALF])
            gu_ref[j] = gg
            gu_ref[j + FF_HALF] = uu
            out = out + _dot(gg * jax.nn.sigmoid(gg) * uu, wd_ref[j])
        out_ref[...] = out

    return pl.pallas_call(
        body, name="ffn_fwd", grid=(S // ts,),
        in_specs=[_row(ts, D), _const(g.shape), _layer_of(wgu, layer), _const(wd.shape)],
        out_specs=[_row(ts, D), _row(ts, D), _ff_chunks(N_DEV, ts)],
        out_shape=[_sds((S, D), F32), _sds((S, D), BF16), _sds((N_DEV, S, FF_CHUNK), F32)],
        compiler_params=_cp())(x, g, wgu, wd)


def ffn_bwd_a(dy, gu, wd):
    S = dy.shape[0]
    ts = _tile_rows(S, 256)

    def body(dy_ref, gu_ref, wd_ref, act_ref, dgu_ref):
        dy_ = dy_ref[...].astype(BF16)
        for j in range(FF_HALF):
            gg, uu = gu_ref[j], gu_ref[j + FF_HALF]
            sg = jax.nn.sigmoid(gg)
            silu = gg * sg
            act_ref[j] = (silu * uu).astype(BF16)
            dact = _dot_nt(dy_, wd_ref[j])
            dgu_ref[j] = (dact * uu * (sg * (1.0 + gg * (1.0 - sg)))).astype(BF16)
            dgu_ref[j + FF_HALF] = (dact * silu).astype(BF16)

    return pl.pallas_call(
        body, name="ffn_bwd_a", grid=(S // ts,),
        in_specs=[_row(ts, D), _ff_chunks(N_DEV, ts), _const(wd.shape)],
        out_specs=[_ff_chunks(FF_HALF, ts), _ff_chunks(N_DEV, ts)],
        out_shape=[_sds((FF_HALF, S, FF_CHUNK), BF16), _sds((N_DEV, S, FF_CHUNK), BF16)],
        compiler_params=_cp())(dy, gu, wd)


def ffn_bwd_b(x, dy, dgu, g, wgu, layer):
    S = x.shape[0]
    ts = _tile_rows(S, 512)

    def body(x_ref, dy_ref, dgu_ref, g_ref, w_ref, dx_ref, dg_ref):
        @pl.when(pl.program_id(0) == 0)
        def _():
            dg_ref[...] = jnp.zeros_like(dg_ref)

        dh = _dot_nt(dgu_ref[0], w_ref[0])
        for j in range(1, N_DEV):
            dh = dh + _dot_nt(dgu_ref[j], w_ref[j])
        x_ = x_ref[...]
        _, rstd = _rms(x_, g_ref[...])
        dxn, dgr = _rms_bwd(x_, g_ref[...], rstd, dh)
        dx_ref[...] = dy_ref[...] + dxn
        dg_ref[...] += _rowsum(dgr)

    return pl.pallas_call(
        body, name="ffn_bwd_b", grid=(S // ts,),
        in_specs=[_row(ts, D), _row(ts, D), pl.BlockSpec((N_DEV, ts, FF_CHUNK), lambda i: (0, i, 0)),
                  _const(g.shape), _layer_of(wgu, layer)],
        out_specs=[_row(ts, D), _acc((1, D))], out_shape=[_sds((S, D), F32), _sds((1, D), F32)],
        compiler_params=_cp())(x, dy, dgu, g, wgu)


def _conv_fwd(xprev, xbp, cw_ref, cb):
    ext = jnp.concatenate([xprev, xbp], axis=0)
    acc = cb + cw_ref[3:4, :] * xbp
    for k in range(3):
        acc = acc + cw_ref[k:k + 1, :] * _roll(ext, 3 - k, 0)[CONV_HALO:]
    return acc


def _gates(xb, keep, wr_ref, br, wi_ref, bi, lam):
    r = jax.nn.sigmoid(_blockdot(xb, wr_ref, LRU_HEADS, LRU_HEAD_DIM) + br)
    ig = jax.nn.sigmoid(_blockdot(xb, wi_ref, LRU_HEADS, LRU_HEAD_DIM) + bi)
    sp = _softplus(-lam)
    log_a = -LRU_C * r * sp
    a = jnp.exp(log_a)
    mult = jnp.sqrt(jnp.maximum(-_expm1(2.0 * log_a), 0.0))
    return r, ig, sp, a, mult


def odd_pre(x, keep, g, win, cw, cb, wr, br, wi, bi, lam):
    S = x.shape[0]
    ts = _tile_rows(S, 512)

    def body(x_ref, xp_ref, keep_ref, g_ref, win_ref, cw_ref, cb_ref, wr_ref, br_ref, wi_ref, bi_ref, lam_ref,
             z_ref, a_ref, b_ref):
        i = pl.program_id(0)
        h, _ = _rms(x_ref[...], g_ref[...])
        z = _dot(h, win_ref[...])
        z_ref[...] = z
        hp, _ = _rms(xp_ref[...], g_ref[...])
        xprev = _dot(hp, win_ref[:, D:]) * (i > 0).astype(F32)
        xb = _conv_fwd(xprev, z[:, D:], cw_ref, cb_ref[...])
        keep_ = keep_ref[...]
        _, ig, _, a, mult = _gates(xb, keep_, wr_ref, br_ref[...], wi_ref, bi_ref[...], lam_ref[...])
        a_ref[...] = a * keep_
        b_ref[...] = jnp.where(keep_ > 0.0, mult, 1.0) * (ig * xb)

    ins = [x, x, keep, g, win, cw, cb, wr, br, wi, bi, lam]
    return pl.pallas_call(
        body, name="odd_pre", grid=(S // ts,),
        in_specs=[_row(ts, D), _prev(CONV_HALO, D, ts), _row(ts, 1)] + [_const(v.shape) for v in ins[3:]],
        out_specs=[_row(ts, 2 * D), _row(ts, D), _row(ts, D)],
        out_shape=[_sds((S, 2 * D), F32), _sds((S, D), F32), _sds((S, D), F32)], compiler_params=_cp())(*ins)


def lru_scan(a, b, reverse=False):
    S = a.shape[0]
    ts = _tile_rows(S, 512)
    n = S // ts
    groups = ts // 8

    def body(a_ref, an_ref, b_ref, h_ref, carry_ref, ash_ref):
        i = pl.program_id(0)

        @pl.when(i == 0)
        def _():
            carry_ref[...] = jnp.zeros_like(carry_ref)

        rid = lax.broadcasted_iota(jnp.int32, (8, D), 0)
        if reverse:
            ext = jnp.concatenate([a_ref[...], an_ref[...] * (i > 0).astype(F32)], axis=0)
            ash_ref[...] = _roll(ext, -1, 0)[:ts]
        src = ash_ref if reverse else a_ref

        def group(j, carry):
            off = pl.multiple_of((groups - 1 - j if reverse else j) * 8, 8)
            a8, b8 = src[pl.ds(off, 8), :], b_ref[pl.ds(off, 8), :]
            for k in (1, 2, 4):
                inside = (rid < 8 - k) if reverse else (rid >= k)
                sh = -k if reverse else k
                a_sh = jnp.where(inside, _roll(a8, sh, 0), 1.0)
                b_sh = jnp.where(inside, _roll(b8, sh, 0), 0.0)
                b8 = a8 * b_sh + b8
                a8 = a8 * a_sh
            h8 = a8 * carry + b8
            h_ref[pl.ds(off, 8), :] = h8
            return h8[0:1, :] if reverse else h8[7:8, :]

        carry_ref[...] = lax.fori_loop(0, groups, group, carry_ref[...])

    if reverse:
        r = ts // 8
        tile = pl.BlockSpec((ts, D), lambda i: (n - 1 - i, 0))
        halo = pl.BlockSpec((8, D), lambda i: (jnp.minimum((n - i) * r, n * r - 1), 0))
    else:
        tile, halo = _row(ts, D), _prev(8, D, ts)
    return pl.pallas_call(
        body, name="lru_scan_rev" if reverse else "lru_scan", grid=(n,), in_specs=[tile, halo, tile],
        out_specs=tile, out_shape=_sds((S, D), F32),
        scratch_shapes=[pltpu.VMEM((1, D), F32), pltpu.VMEM((ts, D), F32)], compiler_params=_cp())(a, a, b)


def odd_post(x, z, hseq, wout):
    S = x.shape[0]
    ts = _tile_rows(S, 512)

    def body(x_ref, gate_ref, h_ref, w_ref, out_ref):
        gl, _ = _gelu(gate_ref[...])
        out_ref[...] = x_ref[...] + _dot(gl * h_ref[...], w_ref[...])

    return pl.pallas_call(
        body, name="odd_post", grid=(S // ts,),
        in_specs=[_row(ts, D), _row(ts, D), _row(ts, D), _const(wout.shape)],
        out_specs=_row(ts, D), out_shape=_sds((S, D), F32), compiler_params=_cp())(x, z, hseq, wout)


def odd_post_bwd(dy, z, hseq, wout):
    S = dy.shape[0]
    ts = _tile_rows(S, 512)

    def body(dy_ref, gate_ref, h_ref, w_ref, y_ref, dgate_ref, dh_ref):
        gate, hs = gate_ref[...], h_ref[...]
        gl, t = _gelu(gate)
        y_ref[...] = (gl * hs).astype(BF16)
        dyy = _dot_nt(dy_ref[...], w_ref[...])
        dgate_ref[...] = dyy * hs * _gelu_grad(gate, t)
        dh_ref[...] = dyy * gl

    return pl.pallas_call(
        body, name="odd_post_bwd", grid=(S // ts,),
        in_specs=[_row(ts, D), _row(ts, D), _row(ts, D), _const(wout.shape)],
        out_specs=[_row(ts, D)] * 3, out_shape=[_sds((S, D), BF16), _sds((S, D), F32), _sds((S, D), F32)],
        compiler_params=_cp())(dy, z, hseq, wout)


def odd_gates_bwd(z, lam_grad, hseq, keep, cw, cb, wr, br, wi, bi, lam):
    S = z.shape[0]
    ts = _tile_rows(S, 512)

    def body(xbp_ref, xbpp_ref, lg_ref, h_ref, hp_ref, keep_ref, cw_ref, cb_ref, wr_ref, br_ref, wi_ref,
             bi_ref, lam_ref, dxb_ref, dcb_ref, dbr_ref, dbi_ref, dlam_ref, dwr_ref, dwi_ref):
        i = pl.program_id(0)

        @pl.when(i == 0)
        def _():
            for ref in (dcb_ref, dbr_ref, dbi_ref, dlam_ref, dwr_ref, dwi_ref):
                ref[...] = jnp.zeros_like(ref)

        first = (i > 0).astype(F32)
        xb = _conv_fwd(xbpp_ref[...] * first, xbp_ref[...], cw_ref, cb_ref[...])
        keep_ = keep_ref[...]
        lam_ = lam_ref[...]
        r, ig, sp, a, mult = _gates(xb, keep_, wr_ref, br_ref[...], wi_ref, bi_ref[...], lam_)
        hs = h_ref[...]
        hprev = _roll(jnp.concatenate([hp_ref[...] * first, hs], axis=0), 1, 0)[CONV_HALO:]
        lg = lg_ref[...]
        da = lg * hprev * keep_
        ixb = ig * xb
        dmult = lg * ixb * keep_
        dixb = lg * jnp.where(keep_ > 0.0, mult, 1.0)
        dlog_a = da * a - dmult * jnp.where(mult > 0.0, a * a / mult, 0.0)
        dr = dlog_a * (-LRU_C * sp)
        dlam_ref[...] += _rowsum(dlog_a * (-LRU_C * r)) * (-jax.nn.sigmoid(-lam_))
        dpr = dr * r * (1.0 - r)
        dpi = dixb * xb * ig * (1.0 - ig)
        dbr_ref[...] += _rowsum(dpr)
        dbi_ref[...] += _rowsum(dpi)
        dxb = dixb * ig
        parts = []
        for h in range(LRU_HEADS):
            sl = slice(h * LRU_HEAD_DIM, (h + 1) * LRU_HEAD_DIM)
            dwr_ref[h] += _dot_tn(xb[:, sl], dpr[:, sl])
            dwi_ref[h] += _dot_tn(xb[:, sl], dpi[:, sl])
            parts.append(_dot_nt(dpr[:, sl], wr_ref[h]) + _dot_nt(dpi[:, sl], wi_ref[h]))
        dxb = dxb + jnp.concatenate(parts, axis=1)
        dxb_ref[...] = dxb
        dcb_ref[...] += _rowsum(dxb)

    ins = [z, z, lam_grad, hseq, hseq, keep, cw, cb, wr, br, wi, bi, lam]
    in_specs = [_row(ts, D, 1), _prev(CONV_HALO, D, ts, 1), _row(ts, D), _row(ts, D), _prev(CONV_HALO, D, ts),
                _row(ts, 1)] + [_const(v.shape) for v in ins[6:]]
    gshape = (LRU_HEADS, LRU_HEAD_DIM, LRU_HEAD_DIM)
    return pl.pallas_call(
        body, name="odd_gates_bwd", grid=(S // ts,), in_specs=in_specs,
        out_specs=[_row(ts, D)] + [_acc((1, D))] * 4 + [_acc(gshape)] * 2,
        out_shape=[_sds((S, D), F32)] + [_sds((1, D), F32)] * 4 + [_sds(gshape, F32)] * 2,
        compiler_params=_cp())(*ins)


def odd_pre_bwd(x, dy, z, dxb, dgate, g, cw, win):
    S = x.shape[0]
    ts = _tile_rows(S, 512)
    n = S // ts

    def body(x_ref, dy_ref, xbp_ref, xbpp_ref, dxb_ref, dxbn_ref, dgate_ref, g_ref, cw_ref, win_ref,
             dx_ref, h_ref, dz_ref, dcw_ref, dg_ref):
        i = pl.program_id(0)

        @pl.when(i == 0)
        def _():
            dcw_ref[...] = jnp.zeros_like(dcw_ref)
            dg_ref[...] = jnp.zeros_like(dg_ref)

        dxb = dxb_ref[...]
        extd = jnp.concatenate([dxb, dxbn_ref[...] * (i < n - 1).astype(F32)], axis=0)
        extx = jnp.concatenate([xbpp_ref[...] * (i > 0).astype(F32), xbp_ref[...]], axis=0)
        dxbp = cw_ref[3:4, :] * dxb
        dcw_ref[3:4, :] += _rowsum(dxb * xbp_ref[...])
        for k in range(3):
            dxbp = dxbp + cw_ref[k:k + 1, :] * _roll(extd, -(3 - k), 0)[:ts]
            dcw_ref[k:k + 1, :] += _rowsum(dxb * _roll(extx, 3 - k, 0)[CONV_HALO:])
        dz = jnp.concatenate([dgate_ref[...], dxbp], axis=1)
        dz_ref[...] = dz.astype(BF16)
        x_ = x_ref[...]
        h, rstd = _rms(x_, g_ref[...])
        h_ref[...] = h.astype(BF16)
        dxn, dgr = _rms_bwd(x_, g_ref[...], rstd, _dot_nt(dz, win_ref[...]))
        dx_ref[...] = dy_ref[...] + dxn
        dg_ref[...] += _rowsum(dgr)

    ins = [x, dy, z, z, dxb, dxb, dgate, g, cw, win]
    in_specs = [_row(ts, D), _row(ts, D), _row(ts, D, 1), _prev(CONV_HALO, D, ts, 1), _row(ts, D),
                _next(CONV_HALO, D, ts, n), _row(ts, D)] + [_const(v.shape) for v in ins[7:]]
    return pl.pallas_call(
        body, name="odd_pre_bwd", grid=(n,), in_specs=in_specs,
        out_specs=[_row(ts, D), _row(ts, D), _row(ts, 2 * D), _acc((4, D)), _acc((1, D))],
        out_shape=[_sds((S, D), F32), _sds((S, D), BF16), _sds((S, 2 * D), BF16), _sds((4, D), F32),
                   _sds((1, D), F32)],
        compiler_params=_cp())(*ins)


def loss_head(x, target, g):
    S = x.shape[0]
    ts = _tile_rows(S, 512)

    def body(x_ref, t_ref, g_ref, dx_ref, dg_ref, loss_ref):
        @pl.when(pl.program_id(0) == 0)
        def _():
            dg_ref[...] = jnp.zeros_like(dg_ref)
            loss_ref[...] = jnp.zeros_like(loss_ref)

        x_ = x_ref[...]
        y, rstd = _rms(x_, g_ref[...])
        err = y - t_ref[...]
        loss_ref[...] += 0.5 * _rowsum(jnp.mean(err * err, axis=1, keepdims=True))
        dxn, dgr = _rms_bwd(x_, g_ref[...], rstd, err * (1.0 / D))
        dx_ref[...] = dxn
        dg_ref[...] += _rowsum(dgr)

    return pl.pallas_call(
        body, name="loss_head", grid=(S // ts,), in_specs=[_row(ts, D), _row(ts, D), _const(g.shape)],
        out_specs=[_row(ts, D), _acc((1, D)), _acc((1, 1))],
        out_shape=[_sds((S, D), F32), _sds((1, D), F32), _sds((1, 1), F32)], compiler_params=_cp())(x, target, g)


def even_post_bwd(dy, o, wo_pool, wo_att):
    S = dy.shape[0]
    ts = _tile_rows(S, 512)

    def body(dy_ref, o_ref, wp_ref, wa_ref, dyp_ref, do_ref, delta_ref):
        dy_ = dy_ref[...]
        dyp_ref[...] = _dot_nt(dy_, wp_ref[...])
        do = _dot_nt(dy_, wa_ref[...])
        do_ref[...] = do.astype(BF16)
        prod = do * o_ref[...]
        for h in range(MLA_HEADS):
            delta_ref[h] = _as_row(jnp.sum(prod[:, h * LANES:(h + 1) * LANES], axis=1, keepdims=True))

    return pl.pallas_call(
        body, name="even_post_bwd", grid=(S // ts,),
        in_specs=[_row(ts, D), _row(ts, D), _const(wo_pool.shape), _const(wo_att.shape)],
        out_specs=[_row(ts, POOL_DIM), _row(ts, D),
                   pl.BlockSpec((MLA_HEADS, None, 1, ts), lambda i: (0, i, 0, 0))],
        out_shape=[_sds((S, POOL_DIM), F32), _sds((S, D), BF16), _sds((MLA_HEADS, S // ts, 1, ts), F32)],
        compiler_params=_cp())(dy, o, wo_pool, wo_att)


def attn_bwd(qp, kp, vp, do, lse_row, delta_row, token=None):
    S = qp.shape[0]
    tk = _tile_rows(S, 512)
    nq = S // tk
    extra, extra_specs = _after(token)

    def body(q_ref, k_ref, v_ref, do_ref, lse_ref, delta_ref, *rest):
        dq_ref, dk_ref, dv_ref = rest[-3:]
        kj = pl.program_id(1)

        @pl.when(kj == 0)
        def _():
            dq_ref[...] = jnp.zeros_like(dq_ref)

        k, v = k_ref[...], v_ref[...]

        def block(qi, carry, masked):
            dk, dv = carry
            off = pl.multiple_of(qi * tk, tk)
            q = q_ref[pl.ds(off, tk), :]
            do_ = do_ref[pl.ds(off, tk), :]
            st = _dot_nt(k, q)
            if masked:
                row = lax.broadcasted_iota(jnp.int32, (tk, tk), 0)
                col = lax.broadcasted_iota(jnp.int32, (tk, tk), 1)
                st = jnp.where(col >= row, st, -1e30)
            pt = _exp2(st - lse_ref[qi])
            dv = dv + _dot(pt, do_)
            dst = (pt * (_dot_nt(v, do_) - delta_ref[qi])).astype(BF16)
            dk = dk + _dot(dst, q)
            dq_ref[pl.ds(off, tk), :] += _dot_tn(dst, k)
            return dk, dv

        zero = jnp.zeros((tk, LANES), F32)
        carry = block(kj, (zero, zero), True)
        dk, dv = _pair_loop(kj + 1, nq, lambda qi, c: block(qi, c, False), carry, unroll=4)
        dk_ref[...] = dk * LN_2
        dv_ref[...] = dv

    blk = pl.BlockSpec((tk, LANES), lambda h, j: (j, h))
    full = pl.BlockSpec((S, LANES), lambda h, j: (0, h))
    rowv = pl.BlockSpec((None, nq, 1, tk), lambda h, j: (h, 0, 0, 0))
    return pl.pallas_call(
        body, name="attn_bwd", grid=(MLA_HEADS, nq), in_specs=[full, blk, blk, full, rowv, rowv] + extra_specs,
        out_specs=[full, blk, blk], out_shape=[_sds((S, D), F32)] * 3, compiler_params=_cp2())(
            qp, kp, vp, do, lse_row, delta_row, *extra)


def even_pre_bwd(x, dy, z, dq, dk, dv, dyp, tabs, g, win, pw, pscale, qg, wq, kvg, wk, wv):
    S = x.shape[0]
    ts = _tile_rows(S, 512)
    n = S // ts

    def body(x_ref, dy_ref, z_ref, up_ref, dq_ref, dk_ref, dv_ref, dyp_ref, dypn_ref, c_ref, a_ref, b_ref,
             g_ref, win_ref, pw_ref, ps_ref, qg_ref, wq_ref, kvg_ref, wk_ref, wv_ref,
             dx_ref, h_ref, dz_ref, dg_ref, dpw_ref, dps_ref, dqg_ref, dwq_ref, dkvg_ref, dwk_ref, dwv_ref):
        i = pl.program_id(0)

        @pl.when(i == 0)
        def _():
            for ref in (dg_ref, dpw_ref, dps_ref, dqg_ref, dwq_ref, dkvg_ref, dwk_ref, dwv_ref):
                ref[...] = jnp.zeros_like(ref)

        z = z_ref[...]
        c, a, b = c_ref[...], a_ref[...], b_ref[...]
        ps = ps_ref[...]
        u = z[:, :POOL_DIM]
        pooled = _pooled(up_ref[...] * (i > 0).astype(F32), u, i * ts)
        dyp_ = dyp_ref[...]
        dps_ref[...] += _rowsum(dyp_ * _blockdot(pooled, pw_ref, 4, LANES))
        ext = jnp.concatenate([dyp_, dypn_ref[...] * (i < n - 1).astype(F32)], axis=0) * ps
        for gidx in range(4):
            sl = slice(gidx * LANES, (gidx + 1) * LANES)
            dpw_ref[gidx] += _dot_tn(pooled[:, sl], ext[:ts, sl])
        dpooled = jnp.concatenate(
            [_dot_nt(ext[:, gidx * LANES:(gidx + 1) * LANES], pw_ref[gidx]) for gidx in range(4)], axis=1)
        dm = dpooled / _pool_cnt(i * ts, ts + POOL_HALO)
        du = _pool_windows(dm, -1)[:ts] - dpooled[:ts]
        cq = z[:, 512:768]
        cqn, rstd_q = _rms(cq, qg_ref[...])
        dqf = _rope_bwd(dq_ref[...] * ATTN_SCALE, c, a, b)
        dwq_ref[...] += _dot_tn(cqn, dqf)
        dcq, dqg_rows = _rms_bwd(cq, qg_ref[...], rstd_q, _dot_nt(dqf, wq_ref[...]))
        dqg_ref[...] += _rowsum(dqg_rows)
        ckv = z[:, 768:896]
        ckvn, rstd_kv = _rms(ckv, kvg_ref[...])
        dk_, dv_ = dk_ref[...], dv_ref[...]
        dwk_ref[...] += _dot_tn(ckvn, dk_)
        dwv_ref[...] += _dot_tn(ckvn, dv_)
        dckv, dkvg_rows = _rms_bwd(ckv, kvg_ref[...], rstd_kv,
                                   _dot_nt(dk_, wk_ref[...]) + _dot_nt(dv_, wv_ref[...]))
        dkvg_ref[...] += _rowsum(dkvg_rows)
        dkr = dk_[:, :LANES]
        for h in range(1, MLA_HEADS):
            dkr = dkr + dk_[:, h * LANES:(h + 1) * LANES]
        lane = lax.broadcasted_iota(jnp.int32, (ts, LANES), 1)
        dkr = jnp.where((lane >= 64) & (lane < 96), _rope_bwd(dkr, c, a, b), 0.0)
        dz = jnp.concatenate([du, dcq, dckv, dkr], axis=1)
        dz_ref[...] = dz.astype(BF16)
        x_ = x_ref[...]
        h, rstd = _rms(x_, g_ref[...])
        h_ref[...] = h.astype(BF16)
        dxn, dgr = _rms_bwd(x_, g_ref[...], rstd, _dot_nt(dz, win_ref[...]))
        dx_ref[...] = dy_ref[...] + dxn
        dg_ref[...] += _rowsum(dgr)

    ins = [x, dy, z, z, dq, dk, dv, dyp, dyp, *tabs, g, win, pw, pscale, qg, wq, kvg, wk, wv]
    in_specs = [_row(ts, D), _row(ts, D), _row(ts, D), _prev(POOL_HALO, POOL_DIM, ts), _row(ts, D), _row(ts, D),
                _row(ts, D), _row(ts, POOL_DIM), _next(POOL_HALO, POOL_DIM, ts, n), _row(ts, LANES),
                _row(ts, LANES), _row(ts, LANES)] + [_const(v.shape) for v in ins[12:]]
    acc_shapes = [(1, D), (4, LANES, LANES), (1, POOL_DIM), (1, Q_LORA), (Q_LORA, D), (1, KV_LORA), (KV_LORA, D),
                  (KV_LORA, D)]
    return pl.pallas_call(
        body, name="even_pre_bwd", grid=(n,), in_specs=in_specs,
        out_specs=[_row(ts, D)] * 3 + [_acc(s) for s in acc_shapes],
        out_shape=[_sds((S, D), F32), _sds((S, D), BF16), _sds((S, D), BF16)] + [_sds(s, F32) for s in acc_shapes],
        compiler_params=_cp())(*ins)


def _pick(n, options):
    for o in options:
        if n % o == 0:
            return o
    return n


def matmul_tn(name, a, b):
    out_dtype = BF16
    S = a.shape[-2]
    ts = _tile_rows(S, 2048)
    steps = S // ts

    def body(a_ref, b_ref, o_ref, acc_ref):
        s = pl.program_id(2)

        @pl.when(s == 0)
        def _():
            acc_ref[...] = jnp.zeros_like(acc_ref)

        acc_ref[...] += _dot_tn(a_ref[...], b_ref[...])

        @pl.when(s == steps - 1)
        def _():
            o_ref[...] = acc_ref[...].astype(o_ref.dtype)

    if a.ndim == 3:
        C, _, K = a.shape
        N = b.shape[1]
        tn = _pick(N, (512, 256, 128))
        grid = (C, N // tn, S // ts)
        in_specs = [pl.BlockSpec((None, ts, K), lambda c, j, s: (c, s, 0)),
                    pl.BlockSpec((ts, tn), lambda c, j, s: (s, j))]
        out_spec, out_shape, tile = pl.BlockSpec((None, K, tn), lambda c, j, s: (c, 0, j)), (C, K, N), (K, tn)
    elif b.ndim == 3:
        C, _, N = b.shape
        K = a.shape[1]
        tk = _pick(K, (512, 256, 128))
        grid = (C, K // tk, S // ts)
        in_specs = [pl.BlockSpec((ts, tk), lambda c, i, s: (s, i)),
                    pl.BlockSpec((None, ts, N), lambda c, i, s: (c, s, 0))]
        out_spec, out_shape, tile = pl.BlockSpec((None, tk, N), lambda c, i, s: (c, i, 0)), (C, K, N), (tk, N)
    else:
        K, N = a.shape[1], b.shape[1]
        tk = _pick(K, (512, 256, 128))
        tn = _pick(N, (512, 256, 128))
        grid = (K // tk, N // tn, S // ts)
        in_specs = [pl.BlockSpec((ts, tk), lambda i, j, s: (s, i)), pl.BlockSpec((ts, tn), lambda i, j, s: (s, j))]
        out_spec, out_shape, tile = pl.BlockSpec((tk, tn), lambda i, j, s: (i, j)), (K, N), (tk, tn)
    return pl.pallas_call(
        body, name=name, grid=grid, in_specs=in_specs, out_specs=out_spec, out_shape=_sds(out_shape, out_dtype),
        scratch_shapes=[pltpu.VMEM(tile, F32)], compiler_params=pltpu.CompilerParams(dimension_semantics=("arbitrary",) * 3, vmem_limit_bytes=VMEM_LIMIT))(
            a, b)


def _my_id():
    return lax.axis_index("x") * 4 + lax.axis_index("y") * 2 + lax.axis_index("c")


def _peer(j):
    x, y, c = lax.axis_index("x"), lax.axis_index("y"), lax.axis_index("c")
    px = 1 - x if j & 4 else x
    py = 1 - y if j & 2 else y
    pc = 1 - c if j & 1 else c
    return (px, py, pc), px * 4 + py * 2 + pc


def all_gather(name, arrays):
    n = len(arrays)

    def body(*refs):
        ins, outs = refs[:n], refs[n:2 * n]
        send_sems, recv_sems, local_sems = refs[2 * n:]
        me = _my_id()
        local = [pltpu.make_async_copy(ins[k], outs[k].at[me], local_sems.at[k]) for k in range(n)]
        for cp in local:
            cp.start()
        sends = []
        for j in range(1, N_DEV):
            peer, _ = _peer(j)
            for k in range(n):
                cp = pltpu.make_async_remote_copy(
                    src_ref=ins[k], dst_ref=outs[k].at[me], send_sem=send_sems.at[k, j - 1],
                    recv_sem=recv_sems.at[k, j - 1], device_id=peer, device_id_type=pl.DeviceIdType.MESH)
                cp.start()
                sends.append(cp)
        for j in range(1, N_DEV):
            peer, pid = _peer(j)
            for k in range(n):
                pltpu.make_async_remote_copy(
                    src_ref=ins[k], dst_ref=outs[k].at[pid], send_sem=send_sems.at[k, j - 1],
                    recv_sem=recv_sems.at[k, j - 1], device_id=peer, device_id_type=pl.DeviceIdType.MESH).wait_recv()
        for cp in sends:
            cp.wait_send()
        for cp in local:
            cp.wait()

    any_spec = pl.BlockSpec(memory_space=pl.ANY)
    return pl.pallas_call(
        body, name=name, in_specs=[any_spec] * n, out_specs=[any_spec] * n,
        out_shape=[_sds((N_DEV,) + a.shape, a.dtype) for a in arrays],
        scratch_shapes=[pltpu.SemaphoreType.DMA((n, N_DEV - 1)), pltpu.SemaphoreType.DMA((n, N_DEV - 1)),
                        pltpu.SemaphoreType.DMA((n,))],
        compiler_params=pltpu.CompilerParams(has_side_effects=True))(*arrays)


def exchange(name, arrays):
    n = len(arrays)

    def body(*refs):
        ins, outs = refs[:n], refs[n:2 * n]
        send_sems, recv_sems, local_sems = refs[2 * n:]
        me = _my_id()
        local = [pltpu.make_async_copy(ins[k].at[me], outs[k].at[me], local_sems.at[k]) for k in range(n)]
        for cp in local:
            cp.start()
        sends = []
        for j in range(1, N_DEV):
            peer, pid = _peer(j)
            for k in range(n):
                cp = pltpu.make_async_remote_copy(
                    src_ref=ins[k].at[pid], dst_ref=outs[k].at[me], send_sem=send_sems.at[k, j - 1],
                    recv_sem=recv_sems.at[k, j - 1], device_id=peer, device_id_type=pl.DeviceIdType.MESH)
                cp.start()
                sends.append(cp)
        for j in range(1, N_DEV):
            peer, pid = _peer(j)
            for k in range(n):
                pltpu.make_async_remote_copy(
                    src_ref=ins[k].at[me], dst_ref=outs[k].at[pid], send_sem=send_sems.at[k, j - 1],
                    recv_sem=recv_sems.at[k, j - 1], device_id=peer, device_id_type=pl.DeviceIdType.MESH).wait_recv()
        for cp in sends:
            cp.wait_send()
        for cp in local:
            cp.wait()

    any_spec = pl.BlockSpec(memory_space=pl.ANY)
    return pl.pallas_call(
        body, name=name, in_specs=[any_spec] * n, out_specs=[any_spec] * n,
        out_shape=[_sds(a.shape, a.dtype) for a in arrays],
        scratch_shapes=[pltpu.SemaphoreType.DMA((n, N_DEV - 1)), pltpu.SemaphoreType.DMA((n, N_DEV - 1)),
                        pltpu.SemaphoreType.DMA((n,))],
        compiler_params=pltpu.CompilerParams(has_side_effects=True))(*arrays)


_HBM = pl.BlockSpec(memory_space=pltpu.HBM)
_SEM = pl.BlockSpec(memory_space=pltpu.SEMAPHORE)
_DATAFLOW = pltpu.SideEffectType.DATAFLOW_SIDE_EFFECTING


def _in_hbm(v):
    return pltpu.with_memory_space_constraint(v, pltpu.HBM)


N_PEERS = N_DEV - 1


def _split_copy(k, j, srcs, lands, send_sems, recv_sems, gather, slot):
    peer, pid = _peer(j)
    return pltpu.make_async_remote_copy(
        src_ref=srcs[k] if gather else srcs[k].at[pid], dst_ref=lands[k].at[_my_id() if slot == "mine" else pid],
        send_sem=send_sems[j - 1], recv_sem=recv_sems[j - 1], device_id=peer, device_id_type=pl.DeviceIdType.MESH)


def split_start(name, arrays, gather):
    n = len(arrays)
    lands = [lax.empty((N_DEV,) + a.shape if gather else a.shape, a.dtype) for a in arrays]

    def body(*refs):
        srcs, lnds = refs[:n], refs[n:2 * n]
        sems = refs[4 * n:4 * n + 2 * N_PEERS]
        token = refs[-1]
        for j in range(1, N_DEV):
            for k in range(n):
                _split_copy(k, j, srcs, lnds, sems[:N_PEERS], sems[N_PEERS:], gather, "mine").start()
        token[...] = jnp.zeros_like(token)

    out = pl.pallas_call(
        body, name=name,
        out_shape=(*[pltpu.HBM(a.shape, a.dtype) for a in arrays], *[pltpu.HBM(l.shape, l.dtype) for l in lands],
                   *[pltpu.SemaphoreType.DMA(())] * (2 * N_PEERS), _sds((8, LANES), F32)),
        in_specs=[_HBM] * (2 * n),
        out_specs=(*[_HBM] * (2 * n), *[_SEM] * (2 * N_PEERS), pl.BlockSpec(memory_space=pltpu.VMEM)),
        input_output_aliases={k: k for k in range(2 * n)},
        compiler_params=pltpu.CompilerParams(has_side_effects=_DATAFLOW))(
            *[_in_hbm(a) for a in arrays], *[_in_hbm(l) for l in lands])
    sems = list(out[2 * n:2 * n + 2 * N_PEERS])
    return sems[:N_PEERS], sems[N_PEERS:], list(out[:n]), list(out[n:2 * n]), out[-1]


def split_wait(name, handle, after, gather):
    send_sems, recv_sems, srcs, lands, _ = handle
    n = len(srcs)

    def body(*refs):
        srcs_r, lnds_r = refs[:n], refs[n:2 * n]
        sems = refs[2 * n:2 * n + 2 * N_PEERS]
        for j in range(1, N_DEV):
            for k in range(n):
                cp = _split_copy(k, j, srcs_r, lnds_r, sems[:N_PEERS], sems[N_PEERS:], gather, "peer")
                cp.wait_send()
                cp.wait_recv()

    out = pl.pallas_call(
        body, name=name, out_shape=tuple(pltpu.HBM(a.shape, a.dtype) for a in srcs + lands),
        in_specs=[_HBM] * (2 * n) + [_SEM] * (2 * N_PEERS) + [pl.BlockSpec(memory_space=pl.ANY)],
        out_specs=tuple([_HBM] * (2 * n)), input_output_aliases={k: k for k in range(2 * n)},
        compiler_params=pltpu.CompilerParams(has_side_effects=_DATAFLOW))(
            *srcs, *lands, *send_sems, *recv_sems, after)
    return list(out[:n]), list(out[n:])


def _fill_own_slot(src, land, gather):
    me = _my_id()
    own = src[None] if gather else lax.dynamic_index_in_dim(src, me, 0, keepdims=True)
    return lax.dynamic_update_slice_in_dim(land, own, me, 0)


ADAMW_BLOCK_ELEMS = 128 * 1024


def adamw(name, parts, w, m, v):
    R, C = w.shape
    tr = _pick(R, [t for t in (512, 256, 128, 64, 32, 16, 8) if t * C <= ADAMW_BLOCK_ELEMS])
    c1 = 1.0 - ADAM_B1 ** ADAM_STEP
    c2 = 1.0 - ADAM_B2 ** ADAM_STEP

    def body(p_ref, w_ref, m_ref, v_ref, g_ref, d_ref, nm_ref, nv_ref):
        g = p_ref[0].astype(F32)
        for s in range(1, N_DEV):
            g = g + p_ref[s].astype(F32)
        g_ref[...] = g
        m_ = ADAM_B1 * m_ref[...] + (1.0 - ADAM_B1) * g
        v_ = ADAM_B2 * v_ref[...] + (1.0 - ADAM_B2) * (g * g)
        nm_ref[...] = m_
        nv_ref[...] = v_
        d_ref[...] = -ADAM_LR * ((m_ / c1) / (jnp.sqrt(v_ / c2) + ADAM_EPS) + ADAM_WD * w_ref[...])

    row = pl.BlockSpec((tr, C), lambda i: (i, 0))
    return pl.pallas_call(
        body, name=name, grid=(R // tr,),
        in_specs=[pl.BlockSpec((N_DEV, tr, C), lambda i: (0, i, 0)), row, row, row], out_specs=[row] * 4,
        out_shape=[_sds((R, C), F32)] * 4, compiler_params=_cp())(parts, w, m, v)


WEIGHTS = ['ev_norm', 'ev_w_in', 'ev_pool_w', 'ev_pool_scale', 'ev_q_norm', 'ev_w_q_up', 'ev_kv_norm', 'ev_w_kv_up',
           'ev_w_out', 'od_norm', 'od_w_in', 'od_conv_w', 'od_conv_b', 'od_w_rgate', 'od_b_rgate', 'od_w_igate',
           'od_b_igate', 'od_lambda', 'od_w_out', 'xa_norm_x', 'xa_norm_mem', 'xa_w_q', 'xa_w_kv', 'xa_w_o',
           'ffn_norm', 'ffn_w_gate_up', 'ffn_w_down', 'final_norm']
SHARD_AXIS = {'ev_w_in': 1, 'ev_w_q_up': 2, 'ev_w_kv_up': 2, 'ev_w_out': 1, 'od_norm': 1, 'od_w_in': 2,
              'od_conv_w': 2, 'od_conv_b': 1, 'od_w_rgate': 2, 'od_b_rgate': 1, 'od_w_igate': 2, 'od_b_igate': 1,
              'od_lambda': 1, 'od_w_out': 1, 'xa_w_q': 1, 'xa_w_kv': 2, 'xa_w_o': 1, 'ffn_w_gate_up': 2,
              'ffn_w_down': 1}
SMALL_F32 = ('od_norm', 'od_conv_w', 'od_conv_b', 'od_b_rgate', 'od_b_igate', 'od_lambda')
STACKED = ('ffn_w_gate_up', 'ffn_w_down')
SHARDED = [n for n in WEIGHTS if n in SHARD_AXIS]
REPLICATED = [n for n in WEIGHTS if n not in SHARD_AXIS]
ROW_ALIGN = 512


def _pack(flats, dtype):
    v = jnp.concatenate([f.reshape(-1).astype(dtype) for f in flats])
    pad = (-v.shape[0]) % (ROW_ALIGN * LANES)
    return jnp.pad(v, (0, pad)).reshape(-1, LANES)


def _rows8(n_elems):
    return -(-n_elems // (8 * LANES)) * 8


def _pack_rows(arrays, lead=False):
    out = []
    for a in arrays:
        r = a.reshape((N_DEV, -1, LANES) if lead else (-1, LANES))
        pad = _rows8(r.shape[-2] * LANES) - r.shape[-2]
        out.append(jnp.pad(r, [(0, 0)] * (r.ndim - 2) + [(0, pad), (0, 0)]))
    return jnp.concatenate(out, axis=-2)


def _unpack_rows(buf, shapes, lead=False):
    out, off = [], 0
    for s in shapes:
        n = 1
        for d in s:
            n *= d
        rows = buf[..., off:off + n // LANES, :]
        out.append(rows.reshape(((N_DEV,) if lead else ()) + tuple(s)))
        off += _rows8(n)
    return out


def _unpack(flat, shapes):
    out, off = [], 0
    v = flat.reshape(-1)
    for s in shapes:
        n = 1
        for d in s:
            n *= d
        out.append(v[off:off + n].reshape(s))
        off += n
    return out


def _to_full(stacked, axis):
    v = jnp.moveaxis(stacked, 0, axis)
    s = v.shape
    return v.reshape(s[:axis] + (s[axis] * s[axis + 1],) + s[axis + 2:])


def _to_shards(full, axis):
    s = full.shape
    v = full.reshape(s[:axis] + (N_DEV, s[axis] // N_DEV) + s[axis + 1:])
    return jnp.moveaxis(v, axis, 0)


def _pad_heads(w, nh, dh, lead):
    s = w.shape
    v = w.reshape(s[:-1] + (nh, dh))
    v = jnp.pad(v, [(0, 0)] * (len(s) - 1) + [(0, 0), (lead, LANES - dh - lead)])
    return v.reshape(s[:-1] + (nh * LANES,))


def _unpad_heads(w, nh, dh, lead):
    s = w.shape
    return w.reshape(s[:-1] + (nh, LANES))[..., lead:lead + dh].reshape(s[:-1] + (nh * dh,))


def _rope_tables(positions):
    inv_freq = 10000.0 ** (-jnp.arange(0, 32, 2, dtype=F32) / 32)
    ang = positions.astype(F32)[:, None] * inv_freq
    cos, sin = jnp.cos(ang), jnp.sin(ang)
    S = positions.shape[0]
    one, zero = jnp.ones((S, 64), F32), jnp.zeros((S, 64), F32)
    z16, z32 = jnp.zeros((S, 16), F32), jnp.zeros((S, 32), F32)
    c = jnp.concatenate([one, cos, cos, jnp.ones((S, 32), F32)], axis=1)
    a = jnp.concatenate([zero, z16, sin, z32], axis=1)
    b = jnp.concatenate([zero, -sin, z16, z32], axis=1)
    return c, a, b


def _t(w):
    return jnp.swapaxes(w, -1, -2)


def device_step(x, mem, positions, target, W, fwd_token=None, late_weights=None, ship_grads=None):
    S = x.shape[0]
    G = {}
    tabs = _rope_tables(positions)
    keep = (positions != 0).astype(F32)[:, None]
    row = lambda v: v.reshape(1, -1)

    w_in = W['ev_w_in'][0]
    ev_win = jnp.concatenate([w_in[:, :896], _pad_heads(w_in[:, 896:], 1, 32, 64)], axis=1)
    ev_wq = _pad_heads(W['ev_w_q_up'][0], MLA_HEADS, QK_DIM, 0)
    kvw = W['ev_w_kv_up'][0].reshape(KV_LORA, MLA_HEADS, 128)
    ev_wk = _pad_heads(kvw[:, :, :64].reshape(KV_LORA, 512), MLA_HEADS, 64, 0)
    ev_wv = _pad_heads(kvw[:, :, 64:].reshape(KV_LORA, 512), MLA_HEADS, 64, 0)
    ev_wo_pool = W['ev_w_out'][0][:POOL_DIM]
    ev_wo_att = _t(_pad_heads(_t(W['ev_w_out'][0][POOL_DIM:]), MLA_HEADS, 64, 0))
    pw = W['ev_pool_w'][0].astype(BF16)
    ev_g, ps, qg, kvg = row(W['ev_norm'][0]), row(W['ev_pool_scale'][0]), row(W['ev_q_norm'][0]), row(W['ev_kv_norm'][0])

    z0, qp, kp, vp, ypool = even_pre(x, tabs, ev_g, ev_win, pw, ps, qg, ev_wq, kvg, ev_wk, ev_wv)
    o_att, lse = attn_fwd(qp, kp, vp, fwd_token)
    if late_weights is not None:
        W = {**W, **late_weights(lse)}
    x1 = even_post(x, ypool, o_att, ev_wo_pool, ev_wo_att)

    def xa_ffn_fwd(xin, l):
        mn, km, vm = mem_kv(mem, row(W['xa_norm_mem'][l]), W['xa_w_kv'][l])
        xm = xattn_fwd(xin, row(W['xa_norm_x'][l]), W['xa_w_q'][l], km, vm, W['xa_w_o'][l])
        xo, hf, gu = ffn_fwd(xm, row(W['ffn_norm'][l]), W['ffn_w_gate_up'], l,
                             W['ffn_w_down'][:, l].reshape(FF_HALF, FF_CHUNK, D))
        return xm, xo, (mn, km, vm, hf, gu)

    x2, x3, memkv0 = xa_ffn_fwd(x1, 0)

    od_g, lam = row(W['od_norm'][0]), row(W['od_lambda'][0])
    cw, cb = W['od_conv_w'][0], row(W['od_conv_b'][0])
    wr, wi = W['od_w_rgate'][0], W['od_w_igate'][0]
    br, bi = row(W['od_b_rgate'][0]), row(W['od_b_igate'][0])
    z1, a_t, b_t = odd_pre(x3, keep, od_g, W['od_w_in'][0], cw, cb, wr, br, wi, bi, lam)
    hseq = lru_scan(a_t, b_t)
    x4 = odd_post(x3, z1, hseq, W['od_w_out'][0])
    x5, x6, memkv1 = xa_ffn_fwd(x4, 1)

    dx, G['final_norm'], loss = loss_head(x6, target, row(W['final_norm']))
    G['final_norm'] = G['final_norm'].reshape(D)

    gnx, gnm, gwq, gwkv, gwo, gfn, gwgu, gwd = ([None, None] for _ in range(8))

    def xa_ffn_bwd(dy, xin, xm, memkv, l):
        mn, km, vm, hf, gu = memkv
        fg = row(W['ffn_norm'][l])
        act, dgu = ffn_bwd_a(dy, gu, W['ffn_w_down'][:, l].reshape(FF_HALF, FF_CHUNK, D))
        gwd[l] = matmul_tn("ffn_dwd", act, dy).reshape(N_DEV, D_FF // N_DEV, D)
        gwgu[l] = matmul_tn("ffn_dwgu", hf, dgu)
        dxm, dfg = ffn_bwd_b(xm, dy, dgu, fg, W['ffn_w_gate_up'], l)
        gfn[l] = dfg[0]
        dxin, o, dq, hx, dgx, dk, dv = xattn_bwd(xin, dxm, row(W['xa_norm_x'][l]), W['xa_w_q'][l], km, vm,
                                                  W['xa_w_o'][l])
        gnx[l] = dgx[0]
        gwo[l] = matmul_tn("xa_dwo", o, dxm)
        gwq[l] = matmul_tn("xa_dwq", hx, dq)
        dkv, dgm = mem_bwd(mem, row(W['xa_norm_mem'][l]), dk, dv, W['xa_w_kv'][l])
        gnm[l] = dgm[0]
        gwkv[l] = matmul_tn("xa_dwkv", mn, dkv)
        return dxin

    dx4 = xa_ffn_bwd(dx, x4, x5, memkv1, 1)

    y_od, dgate, dhs = odd_post_bwd(dx4, z1, hseq, W['od_w_out'][0])
    G['od_w_out'] = matmul_tn("od_dwout", y_od, dx4)[None]
    lam_grad = lru_scan(a_t, dhs, reverse=True)
    dxb, dcb, dbr, dbi, dlam, dwr, dwi = odd_gates_bwd(z1, lam_grad, hseq, keep, cw, cb, wr, br, wi, bi, lam)
    dx3, h_od, dz1, dcw, dg_od = odd_pre_bwd(x3, dx4, z1, dxb, dgate, od_g, cw, W['od_w_in'][0])
    G['od_w_in'] = matmul_tn("od_dwin", h_od, dz1)[None]
    G['od_norm'], G['od_conv_w'], G['od_conv_b'] = dg_od, dcw[None], dcb
    G['od_w_rgate'], G['od_b_rgate'], G['od_w_igate'], G['od_b_igate'], G['od_lambda'] = (
        dwr[None], dbr, dwi[None], dbi, dlam)

    dx1 = xa_ffn_bwd(dx3, x1, x2, memkv0, 0)
    G['xa_norm_x'], G['xa_norm_mem'], G['ffn_norm'] = jnp.stack(gnx), jnp.stack(gnm), jnp.stack(gfn)
    G['xa_w_q'], G['xa_w_kv'], G['xa_w_o'] = jnp.stack(gwq), jnp.stack(gwkv), jnp.stack(gwo)
    G['ffn_w_gate_up'], G['ffn_w_down'] = jnp.stack(gwgu, axis=1), jnp.stack(gwd, axis=1)
    bwd_token = ship_grads(G) if ship_grads is not None else None

    dyp, do_att, delta = even_post_bwd(dx1, o_att, ev_wo_pool, ev_wo_att)
    g_wo_pool = matmul_tn("ev_dwo_pool", ypool, dx1)
    g_wo_att = matmul_tn("ev_dwo_att", o_att, dx1)
    G['ev_w_out'] = jnp.concatenate([g_wo_pool, _t(_unpad_heads(_t(g_wo_att), MLA_HEADS, 64, 0))], axis=0)[None]
    dq, dk, dv = attn_bwd(qp, kp, vp, do_att, lse, delta, bwd_token)
    (grad_x, h_ev, dz0, dg_ev, dpw, dps, dqg, dwq, dkvg, dwk, dwv) = even_pre_bwd(
        x, dx1, z0, dq, dk, dv, dyp, tabs, ev_g, ev_win, pw, ps, qg, ev_wq, kvg, ev_wk, ev_wv)
    g_win = matmul_tn("ev_dwin", h_ev, dz0)
    G['ev_w_in'] = jnp.concatenate([g_win[:, :896], _unpad_heads(g_win[:, 896:], 1, 32, 64)], axis=1)[None]
    G['ev_norm'], G['ev_pool_w'], G['ev_pool_scale'], G['ev_q_norm'], G['ev_kv_norm'] = (
        dg_ev, dpw[None], dps, dqg, dkvg)
    G['ev_w_q_up'] = _unpad_heads(dwq, MLA_HEADS, QK_DIM, 0)[None]
    gk = _unpad_heads(dwk, MLA_HEADS, 64, 0).reshape(KV_LORA, MLA_HEADS, 64)
    gv = _unpad_heads(dwv, MLA_HEADS, 64, 0).reshape(KV_LORA, MLA_HEADS, 64)
    G['ev_w_kv_up'] = jnp.concatenate([gk, gv], axis=2).reshape(1, KV_LORA, MLA_HEADS * 128)
    return loss[0, 0], grad_x, G


def kernel(x, mem, positions, ev_norm, ev_w_in, ev_pool_w, ev_pool_scale, ev_q_norm, ev_w_q_up, ev_kv_norm, ev_w_kv_up, ev_w_out, od_norm, od_w_in, od_conv_w, od_conv_b, od_w_rgate, od_b_rgate, od_w_igate, od_b_igate, od_lambda, od_w_out, xa_norm_x, xa_norm_mem, xa_w_q, xa_w_kv, xa_w_o, ffn_norm, ffn_w_gate_up, ffn_w_down, final_norm, loss_target, m_ev_norm, m_ev_w_in, m_ev_pool_w, m_ev_pool_scale, m_ev_q_norm, m_ev_w_q_up, m_ev_kv_norm, m_ev_w_kv_up, m_ev_w_out, m_od_norm, m_od_w_in, m_od_conv_w, m_od_conv_b, m_od_w_rgate, m_od_b_rgate, m_od_w_igate, m_od_b_igate, m_od_lambda, m_od_w_out, m_xa_norm_x, m_xa_norm_mem, m_xa_w_q, m_xa_w_kv, m_xa_w_o, m_ffn_norm, m_ffn_w_gate_up, m_ffn_w_down, m_final_norm, v_ev_norm, v_ev_w_in, v_ev_pool_w, v_ev_pool_scale, v_ev_q_norm, v_ev_w_q_up, v_ev_kv_norm, v_ev_w_kv_up, v_ev_w_out, v_od_norm, v_od_w_in, v_od_conv_w, v_od_conv_b, v_od_w_rgate, v_od_b_rgate, v_od_w_igate, v_od_b_igate, v_od_lambda, v_od_w_out, v_xa_norm_x, v_xa_norm_mem, v_xa_w_q, v_xa_w_kv, v_xa_w_o, v_ffn_norm, v_ffn_w_gate_up, v_ffn_w_down, v_final_norm):
    args = dict(locals())
    w = {n: args[n] for n in WEIGHTS}
    m = {n: args['m_' + n] for n in WEIGHTS}
    v = {n: args['v_' + n] for n in WEIGHTS}
    big = [n for n in SHARDED if n not in SMALL_F32]
    small = [n for n in SHARDED if n in SMALL_F32]

    small_shapes = [w[n].shape for n in small]
    first = [n for n in big if n.startswith('ev_')]
    late = [n for n in big if n not in first]

    def full(n, st):
        return st if n in STACKED else _to_full(st, SHARD_AXIS[n])

    W = {n: w[n] for n in REPLICATED}
    W.update((n, full(n, st)) for n, st in zip(first, all_gather("gather_ev_weights", [w[n].astype(BF16) for n in first])))
    gather = split_start("gather_start", [w[n].astype(BF16) for n in late] + [_pack_rows([w[n] for n in small])], True)

    def late_weights(after):
        srcs, lands = split_wait("gather_wait", gather, after, True)
        lands = [_fill_own_slot(s, l, True) for s, l in zip(srcs, lands)]
        out = {n: full(n, st) for n, st in zip(late, lands)}
        out.update((n, _to_full(st, SHARD_AXIS[n])) for n, st in zip(small, _unpack_rows(lands[-1], small_shapes, True)))
        return out

    def shards(G, n):
        return G[n] if n in STACKED else _to_shards(G[n], SHARD_AXIS[n])

    shipped = []

    def ship_grads(G):
        shipped.append(split_start("exchange_start", [shards(G, n).astype(BF16) for n in late] +
                                   [_pack_rows([shards(G, n) for n in small], lead=True)], False))
        return shipped[0][-1]

    loss, grad_x, G = device_step(x[0], mem[0], positions[0], loss_target[0], W, gather[-1], late_weights, ship_grads)
    outs = [{}, {}, {}, {}]

    rep_shapes = [w[n].shape for n in REPLICATED] + [(LANES,)]
    zero = jnp.zeros((LANES,), F32)
    rep_parts, = all_gather("gather_rep_grads", [_pack(
        [G[n] for n in REPLICATED] + [jnp.broadcast_to(loss, (LANES,))], F32)])
    rep = adamw("adamw_rep", rep_parts, *[_pack([d[n] for n in REPLICATED] + [zero], F32) for d in (w, m, v)])
    for k in range(4):
        outs[k].update(zip(REPLICATED + ['loss'], _unpack(rep[k], rep_shapes)))
    loss = outs[0]['loss'][0]

    srcs, lands = split_wait("exchange_wait", shipped[0], grad_x, False)
    late_parts = [_fill_own_slot(s, l, False) for s, l in zip(srcs, lands)]
    parts = list(exchange("exchange_ev_grads", [shards(G, n).astype(BF16) for n in first])) + late_parts
    two_d = lambda a: a.reshape(-1, a.shape[-1])
    for n, p in zip(first + late, parts):
        res = adamw("adamw_" + n, p.reshape((N_DEV,) + two_d(w[n]).shape), two_d(w[n]), two_d(m[n]), two_d(v[n]))
        for k in range(4):
            outs[k][n] = res[k].reshape(w[n].shape)
    res = adamw("adamw_small", parts[-1], *[_pack_rows([d[n] for n in small]) for d in (w, m, v)])
    for k in range(4):
        outs[k].update(zip(small, _unpack_rows(res[k], small_shapes)))

    return (loss, grad_x[None], *[outs[0][n] for n in WEIGHTS], *[outs[1][n] for n in WEIGHTS],
            *[outs[2][n] for n in WEIGHTS], *[outs[3][n] for n in WEIGHTS])
```

```python
import functools

import jax
import jax.numpy as jnp
from jax import lax
from jax.experimental import pallas as pl
from jax.experimental.pallas import tpu as pltpu

F32, BF16 = jnp.float32, jnp.bfloat16
N_DEV = 8
D = 1024
POOL_DIM = 512
POOL_WINDOWS = (2, 4, 8, 16)
MLA_HEADS = 8
QK_DIM = 96
Q_LORA, KV_LORA = 256, 128
LRU_HEADS, LRU_HEAD_DIM = 4, 256
LRU_C = 8.0
MEM_HEADS, MEM_HEAD_DIM = 4, 256
D_FF = 2816
RMS_EPS = 1e-6
ADAM_LR, ADAM_B1, ADAM_B2, ADAM_EPS, ADAM_WD, ADAM_STEP = 0.001, 0.9, 0.999, 1e-08, 0.01, 10
LANES = 128
POOL_HALO = 16
CONV_HALO = 8
VMEM_LIMIT = 60000 * 1024


def _cp():
    return pltpu.CompilerParams(dimension_semantics=("arbitrary",), vmem_limit_bytes=VMEM_LIMIT)


def _cp2():
    return pltpu.CompilerParams(dimension_semantics=("arbitrary", "arbitrary"), vmem_limit_bytes=VMEM_LIMIT)


def _row(ts, c, col=0):
    return pl.BlockSpec((ts, c), lambda i: (i, col))


def _prev(hr, c, ts, col=0):
    r = ts // hr
    return pl.BlockSpec((hr, c), lambda i: (jnp.maximum(i * r - 1, 0), col))


def _next(hr, c, ts, n, col=0):
    r = ts // hr
    return pl.BlockSpec((hr, c), lambda i: (jnp.minimum((i + 1) * r, n * r - 1), col))


def _const(shape):
    nd = len(shape)
    return pl.BlockSpec(tuple(shape), lambda i: (0,) * nd, pipeline_mode=pl.Buffered(1))


def _acc(shape):
    nd = len(shape)
    return pl.BlockSpec(tuple(shape), lambda i: (0,) * nd)


def _sds(shape, dt):
    return jax.ShapeDtypeStruct(tuple(shape), dt)


def _dot(a, b):
    return jnp.dot(a.astype(BF16), b.astype(BF16), preferred_element_type=F32)


def _dot_nt(a, b):
    return lax.dot_general(a.astype(BF16), b.astype(BF16), (((1,), (1,)), ((), ())), preferred_element_type=F32)


def _dot_tn(a, b):
    return lax.dot_general(a.astype(BF16), b.astype(BF16), (((0,), (0,)), ((), ())), preferred_element_type=F32)


def _rms(x, g):
    rstd = lax.rsqrt(jnp.mean(x * x, axis=-1, keepdims=True) + RMS_EPS)
    return x * rstd * g, rstd


def _rms_bwd(x, g, rstd, dy):
    xn = x * rstd
    dyg = dy * g
    dx = rstd * (dyg - xn * jnp.mean(dyg * xn, axis=-1, keepdims=True))
    return dx, dy * xn


def _rowsum(v):
    return jnp.sum(v, axis=0, keepdims=True)


def _roll(v, s, axis):
    n = v.shape[axis]
    return pltpu.roll(v, s % n, axis)


def _rope(t, c, a, b):
    k = t.shape[1] // LANES
    if k > 1:
        c, a, b = (jnp.tile(v, (1, k)) for v in (c, a, b))
    return t * c + _roll(t, 16, 1) * a + _roll(t, -16, 1) * b


def _rope_bwd(d, c, a, b):
    k = d.shape[1] // LANES
    if k > 1:
        c, a, b = (jnp.tile(v, (1, k)) for v in (c, a, b))
    return d * c + _roll(d * a, -16, 1) + _roll(d * b, 16, 1)


def _gelu(x):
    c = 0.7978845608028654
    t = jnp.tanh(c * (x + 0.044715 * x * x * x))
    return 0.5 * x * (1.0 + t), t


def _gelu_grad(x, t):
    c = 0.7978845608028654
    return 0.5 * (1.0 + t) + 0.5 * x * (1.0 - t * t) * c * (1.0 + 3.0 * 0.044715 * x * x)


def _blockdot(v, w_ref, nblk, width):
    return jnp.concatenate(
        [_dot(v[:, j * width:(j + 1) * width], w_ref[j]) for j in range(nblk)], axis=1)


def _pool_cnt(row0, rows):
    t = row0 + lax.broadcasted_iota(jnp.int32, (rows, POOL_DIM), 0)
    w = jnp.left_shift(2, lax.broadcasted_iota(jnp.int32, (rows, POOL_DIM), 1) // LANES)
    return jnp.minimum(t + 1, w).astype(F32)


def _pool_windows(ext, sign):
    s2 = ext + _roll(ext, sign * 1, 0)
    t = s2[:, LANES:]
    s4 = t + _roll(t, sign * 2, 0)
    t = s4[:, LANES:]
    s8 = t + _roll(t, sign * 4, 0)
    t = s8[:, LANES:]
    s16 = t + _roll(t, sign * 8, 0)
    return jnp.concatenate([s2[:, :LANES], s4[:, :LANES], s8[:, :LANES], s16], axis=1)


def _pooled(uprev, u, row0):
    ts = u.shape[0]
    ext = jnp.concatenate([uprev, u], axis=0)
    sums = _pool_windows(ext, 1)[POOL_HALO:]
    return sums / _pool_cnt(row0, ts) - u


def _expm1(x):
    return jnp.where(jnp.abs(x) < 0.01, x * (1.0 + 0.5 * x * (1.0 + x * (1.0 / 3.0))), jnp.exp(x) - 1.0)


def _softplus(z):
    return jnp.maximum(z, 0.0) + jnp.log1p(jnp.exp(-jnp.abs(z)))


def _tile_rows(s, want):
    while s % want:
        want //= 2
    return want


def even_pre(x, tabs, g, win, pw, pscale, qg, wq, kvg, wk, wv):
    S = x.shape[0]
    ts = _tile_rows(S, 512)

    def body(x_ref, xp_ref, c_ref, a_ref, b_ref, g_ref, win_ref, pw_ref, ps_ref, qg_ref, wq_ref, kvg_ref,
             wk_ref, wv_ref, z_ref, q_ref, k_ref, v_ref, yp_ref):
        i = pl.program_id(0)
        h, _ = _rms(x_ref[...], g_ref[...])
        z = _dot(h, win_ref[...])
        z_ref[...] = z
        hp, _ = _rms(xp_ref[...], g_ref[...])
        uprev = _dot(hp, win_ref[:, :POOL_DIM]) * (i > 0).astype(F32)
        u = z[:, :POOL_DIM]
        pooled = _pooled(uprev, u, i * ts)
        yp_ref[...] = (_blockdot(pooled, pw_ref, 4, LANES) * ps_ref[...]).astype(BF16)
        c, a, b = c_ref[...], a_ref[...], b_ref[...]
        cqn, _ = _rms(z[:, 512:768], qg_ref[...])
        q_ref[...] = (_rope(_dot(cqn, wq_ref[...]), c, a, b) * (ATTN_SCALE * LOG2_E)).astype(BF16)
        ckvn, _ = _rms(z[:, 768:896], kvg_ref[...])
        krr = _rope(z[:, 896:1024], c, a, b)
        k_ref[...] = (_dot(ckvn, wk_ref[...]) + jnp.tile(krr, (1, MLA_HEADS))).astype(BF16)
        lane = lax.broadcasted_iota(jnp.int32, (ts, D), 1) % LANES
        v_ref[...] = jnp.where(lane == ONES_LANE, 1.0, _dot(ckvn, wv_ref[...])).astype(BF16)

    ins = [x, x, *tabs, g, win, pw, pscale, qg, wq, kvg, wk, wv]
    in_specs = [_row(ts, D), _prev(POOL_HALO, D, ts), _row(ts, LANES), _row(ts, LANES), _row(ts, LANES)]
    in_specs += [_const(v.shape) for v in ins[5:]]
    return pl.pallas_call(
        body, name="even_pre", grid=(S // ts,), in_specs=in_specs,
        out_specs=[_row(ts, D)] * 4 + [_row(ts, POOL_DIM)],
        out_shape=[_sds((S, D), F32)] + [_sds((S, D), BF16)] * 3 + [_sds((S, POOL_DIM), BF16)],
        compiler_params=_cp())(*ins)


ATTN_SCALE = QK_DIM ** -0.5
LOG2_E = 1.4426950408889634
LN_2 = 0.6931471805599453
ONES_LANE = 64


def _exp2(x):
    return jnp.exp2(x)


def _pair_loop(lo, hi, step, init, unrolls=(2, 1)):
    carry = init
    for unroll in unrolls:
        groups = (hi - lo) // unroll

        def group(j, c, lo=lo, unroll=unroll):
            for u in range(unroll):
                c = step(lo + unroll * j + u, c)
            return c

        carry = lax.fori_loop(0, groups, group, carry)
        lo = lo + unroll * groups
    return carry


def _as_row(col):
    return jnp.transpose(jnp.broadcast_to(col, (col.shape[0], LANES)))[0:1, :]


def _after(token):
    return ([], []) if token is None else ([token], [pl.BlockSpec(memory_space=pl.ANY)])


def attn_fwd(qp, kp, vp, token=None):
    S = qp.shape[0]
    tq = _tile_rows(S, 512)
    extra, extra_specs = _after(token)

    def body(q_ref, k_ref, v_ref, *rest):
        o_ref, lse_ref = rest[-2:]
        qi = pl.program_id(1)
        q = q_ref[...]

        def block(ki, carry, masked):
            m, acc = carry
            off = pl.multiple_of(ki * tq, tq)
            s = _dot_nt(q, k_ref[pl.ds(off, tq), :])
            if masked:
                row = lax.broadcasted_iota(jnp.int32, (tq, tq), 0)
                col = lax.broadcasted_iota(jnp.int32, (tq, tq), 1)
                s = jnp.where(col <= row, s, -1e30)
            m_new = jnp.maximum(m, jnp.max(s, axis=1, keepdims=True))
            acc = _exp2(m - m_new) * acc + _dot(_exp2(s - m_new), v_ref[pl.ds(off, tq), :])
            return m_new, acc

        init = (jnp.full((tq, 1), -1e30, F32), jnp.zeros((tq, LANES), F32))
        carry = _pair_loop(0, qi, lambda ki, c: block(ki, c, False), init, unrolls=(8, 4, 2, 1))
        m, acc = block(qi, carry, True)
        l = acc[:, ONES_LANE:ONES_LANE + 1]
        o_ref[...] = acc / l
        lse_ref[...] = _as_row(m + jnp.log(l) * LOG2_E)

    blk = pl.BlockSpec((tq, LANES), lambda h, i: (i, h))
    full = pl.BlockSpec((S, LANES), lambda h, i: (0, h))
    return pl.pallas_call(
        body, name="attn_fwd", grid=(MLA_HEADS, S // tq), in_specs=[blk, full, full] + extra_specs,
        out_specs=[blk, pl.BlockSpec((None, None, 1, tq), lambda h, i: (h, i, 0, 0))],
        out_shape=[_sds((S, D), F32), _sds((MLA_HEADS, S // tq, 1, tq), F32)], compiler_params=_cp2())(
            qp, kp, vp, *extra)


def even_post(x, ypool, o, wo_pool, wo_att):
    S = x.shape[0]
    ts = _tile_rows(S, 512)

    def body(x_ref, yp_ref, o_ref, wp_ref, wa_ref, out_ref):
        out_ref[...] = x_ref[...] + _dot(yp_ref[...], wp_ref[...]) + _dot(o_ref[...], wa_ref[...])

    return pl.pallas_call(
        body, name="even_post", grid=(S // ts,),
        in_specs=[_row(ts, D), _row(ts, POOL_DIM), _row(ts, D), _const(wo_pool.shape), _const(wo_att.shape)],
        out_specs=_row(ts, D), out_shape=_sds((S, D), F32), compiler_params=_cp())(x, ypool, o, wo_pool, wo_att)


def mem_kv(mem, g, wkv):
    M = mem.shape[0]

    def body(mem_ref, g_ref, w_ref, mn_ref, k_ref, v_ref):
        mn, _ = _rms(mem_ref[...], g_ref[...])
        mn_ref[...] = mn.astype(BF16)
        k_ref[...] = _dot(mn, w_ref[:, :D]).astype(BF16)
        v_ref[...] = _dot(mn, w_ref[:, D:]).astype(BF16)

    return pl.pallas_call(
        body, name="mem_kv", grid=(1,), in_specs=[_acc(mem.shape), _acc(g.shape), _acc(wkv.shape)],
        out_specs=[_acc((M, D))] * 3, out_shape=[_sds((M, D), BF16)] * 3, compiler_params=_cp())(mem, g, wkv)


def _xattn_heads(hx, wq_ref, k_ref, v_ref):
    q = _dot(hx, wq_ref[...])
    scale = MEM_HEAD_DIM ** -0.5
    ps, os_ = [], []
    for h in range(MEM_HEADS):
        sl = slice(h * MEM_HEAD_DIM, (h + 1) * MEM_HEAD_DIM)
        s = _dot_nt(q[:, sl], k_ref[:, sl]) * scale
        e = jnp.exp(s - jnp.max(s, axis=1, keepdims=True))
        p = e / jnp.sum(e, axis=1, keepdims=True)
        ps.append(p)
        os_.append(_dot(p, v_ref[:, sl]))
    return q, ps, jnp.concatenate(os_, axis=1)


def xattn_fwd(x, g, wq, kmem, vmem, wo):
    S = x.shape[0]
    ts = _tile_rows(S, 512)

    def body(x_ref, g_ref, wq_ref, k_ref, v_ref, wo_ref, out_ref):
        x_ = x_ref[...]
        hx, _ = _rms(x_, g_ref[...])
        _, _, o = _xattn_heads(hx, wq_ref, k_ref, v_ref)
        out_ref[...] = x_ + _dot(o, wo_ref[...])

    ins = [x, g, wq, kmem, vmem, wo]
    return pl.pallas_call(
        body, name="xattn_fwd", grid=(S // ts,), in_specs=[_row(ts, D)] + [_const(v.shape) for v in ins[1:]],
        out_specs=_row(ts, D), out_shape=_sds((S, D), F32), compiler_params=_cp())(*ins)


def xattn_bwd(x, dy, g, wq, kmem, vmem, wo):
    S = x.shape[0]
    M = kmem.shape[0]
    ts = _tile_rows(S, 512)
    scale = MEM_HEAD_DIM ** -0.5

    def body(x_ref, dy_ref, g_ref, wq_ref, k_ref, v_ref, wo_ref,
             dx_ref, o_ref, dq_ref, hx_ref, dg_ref, dk_ref, dv_ref):
        i = pl.program_id(0)

        @pl.when(i == 0)
        def _():
            dg_ref[...] = jnp.zeros_like(dg_ref)
            dk_ref[...] = jnp.zeros_like(dk_ref)
            dv_ref[...] = jnp.zeros_like(dv_ref)

        x_, dy_ = x_ref[...], dy_ref[...]
        hx, rstd = _rms(x_, g_ref[...])
        q, ps, o = _xattn_heads(hx, wq_ref, k_ref, v_ref)
        hx_ref[...] = hx.astype(BF16)
        o_ref[...] = o.astype(BF16)
        do = _dot_nt(dy_, wo_ref[...])
        dqs = []
        for h in range(MEM_HEADS):
            sl = slice(h * MEM_HEAD_DIM, (h + 1) * MEM_HEAD_DIM)
            p, do_h = ps[h], do[:, sl]
            dp = _dot_nt(do_h, v_ref[:, sl])
            ds = p * (dp - jnp.sum(p * dp, axis=1, keepdims=True)) * scale
            dqs.append(_dot(ds, k_ref[:, sl]))
            dk_ref[:, sl] += _dot_tn(ds, q[:, sl])
            dv_ref[:, sl] += _dot_tn(p, do_h)
        dq = jnp.concatenate(dqs, axis=1)
        dq_ref[...] = dq.astype(BF16)
        dxn, dgr = _rms_bwd(x_, g_ref[...], rstd, _dot_nt(dq, wq_ref[...]))
        dx_ref[...] = dy_ + dxn
        dg_ref[...] += _rowsum(dgr)

    ins = [x, dy, g, wq, kmem, vmem, wo]
    return pl.pallas_call(
        body, name="xattn_bwd", grid=(S // ts,),
        in_specs=[_row(ts, D), _row(ts, D)] + [_const(v.shape) for v in ins[2:]],
        out_specs=[_row(ts, D)] * 4 + [_acc((1, D)), _acc((M, D)), _acc((M, D))],
        out_shape=[_sds((S, D), F32)] + [_sds((S, D), BF16)] * 3 + [_sds((1, D), F32), _sds((M, D), F32),
                                                                    _sds((M, D), F32)],
        compiler_params=_cp())(*ins)


def mem_bwd(mem, g, dk, dv, wkv):
    M = mem.shape[0]

    def body(mem_ref, g_ref, dk_ref, dv_ref, w_ref, dkv_ref, dg_ref):
        dkv = jnp.concatenate([dk_ref[...], dv_ref[...]], axis=1)
        dkv_ref[...] = dkv.astype(BF16)
        _, rstd = _rms(mem_ref[...], g_ref[...])
        dg_ref[...] = _rowsum(_dot_nt(dkv, w_ref[...]) * (mem_ref[...] * rstd))

    ins = [mem, g, dk, dv, wkv]
    return pl.pallas_call(
        body, name="mem_bwd", grid=(1,), in_specs=[_acc(v.shape) for v in ins],
        out_specs=[_acc((M, 2 * D)), _acc((1, D))], out_shape=[_sds((M, 2 * D), BF16), _sds((1, D), F32)],
        compiler_params=_cp())(*ins)


FF_CHUNK = 2 * D_FF // N_DEV
FF_HALF = N_DEV // 2


def _layer_of(w, layer):
    return pl.BlockSpec((N_DEV, None) + w.shape[2:], lambda i: (0, layer, 0, 0), pipeline_mode=pl.Buffered(1))


def _ff_chunks(c, ts):
    return pl.BlockSpec((c, ts, FF_CHUNK), lambda i: (0, i, 0))


def ffn_fwd(x, g, wgu, layer, wd):
    S = x.shape[0]
    ts = _tile_rows(S, 256)

    def body(x_ref, g_ref, wgu_ref, wd_ref, out_ref, hf_ref, gu_ref):
        x_ = x_ref[...]
        hf = _rms(x_, g_ref[...])[0].astype(BF16)
        hf_ref[...] = hf
        out = x_
        for j in range(FF_HALF):
            gg, uu = _dot(hf, wgu_ref[j]), _dot(hf, wgu_ref[j + FF_HALF])
            gu_ref[j] = gg
            gu_ref[j + FF_HALF] = uu
            out = out + _dot(gg * jax.nn.sigmoid(gg) * uu, wd_ref[j])
        out_ref[...] = out

    return pl.pallas_call(
        body, name="ffn_fwd", grid=(S // ts,),
        in_specs=[_row(ts, D), _const(g.shape), _layer_of(wgu, layer), _const(wd.shape)],
        out_specs=[_row(ts, D), _row(ts, D), _ff_chunks(N_DEV, ts)],
        out_shape=[_sds((S, D), F32), _sds((S, D), BF16), _sds((N_DEV, S, FF_CHUNK), F32)],
        compiler_params=_cp())(x, g, wgu, wd)


def ffn_bwd_a(dy, gu, wd):
    S = dy.shape[0]
    ts = _tile_rows(S, 256)

    def body(dy_ref, gu_ref, wd_ref, act_ref, dgu_ref):
        dy_ = dy_ref[...].astype(BF16)
        for j in range(FF_HALF):
            gg, uu = gu_ref[j], gu_ref[j + FF_HALF]
            sg = jax.nn.sigmoid(gg)
            silu = gg * sg
            act_ref[j] = (silu * uu).astype(BF16)
            dact = _dot_nt(dy_, wd_ref[j])
            dgu_ref[j] = (dact * uu * (sg * (1.0 + gg * (1.0 - sg)))).astype(BF16)
            dgu_ref[j + FF_HALF] = (dact * silu).astype(BF16)

    return pl.pallas_call(
        body, name="ffn_bwd_a", grid=(S // ts,),
        in_specs=[_row(ts, D), _ff_chunks(N_DEV, ts), _const(wd.shape)],
        out_specs=[_ff_chunks(FF_HALF, ts), _ff_chunks(N_DEV, ts)],
        out_shape=[_sds((FF_HALF, S, FF_CHUNK), BF16), _sds((N_DEV, S, FF_CHUNK), BF16)],
        compiler_params=_cp())(dy, gu, wd)


def ffn_bwd_b(x, dy, dgu, g, wgu, layer):
    S = x.shape[0]
    ts = _tile_rows(S, 512)

    def body(x_ref, dy_ref, dgu_ref, g_ref, w_ref, dx_ref, dg_ref):
        @pl.when(pl.program_id(0) == 0)
        def _():
            dg_ref[...] = jnp.zeros_like(dg_ref)

        dh = _dot_nt(dgu_ref[0], w_ref[0])
        for j in range(1, N_DEV):
            dh = dh + _dot_nt(dgu_ref[j], w_ref[j])
        x_ = x_ref[...]
        _, rstd = _rms(x_, g_ref[...])
        dxn, dgr = _rms_bwd(x_, g_ref[...], rstd, dh)
        dx_ref[...] = dy_ref[...] + dxn
        dg_ref[...] += _rowsum(dgr)

    return pl.pallas_call(
        body, name="ffn_bwd_b", grid=(S // ts,),
        in_specs=[_row(ts, D), _row(ts, D), pl.BlockSpec((N_DEV, ts, FF_CHUNK), lambda i: (0, i, 0)),
                  _const(g.shape), _layer_of(wgu, layer)],
        out_specs=[_row(ts, D), _acc((1, D))], out_shape=[_sds((S, D), F32), _sds((1, D), F32)],
        compiler_params=_cp())(x, dy, dgu, g, wgu)


def _conv_fwd(xprev, xbp, cw_ref, cb):
    ext = jnp.concatenate([xprev, xbp], axis=0)
    acc = cb + cw_ref[3:4, :] * xbp
    for k in range(3):
        acc = acc + cw_ref[k:k + 1, :] * _roll(ext, 3 - k, 0)[CONV_HALO:]
    return acc


def _gates(xb, keep, wr_ref, br, wi_ref, bi, lam):
    r = jax.nn.sigmoid(_blockdot(xb, wr_ref, LRU_HEADS, LRU_HEAD_DIM) + br)
    ig = jax.nn.sigmoid(_blockdot(xb, wi_ref, LRU_HEADS, LRU_HEAD_DIM) + bi)
    sp = _softplus(-lam)
    log_a = -LRU_C * r * sp
    a = jnp.exp(log_a)
    mult = jnp.sqrt(jnp.maximum(-_expm1(2.0 * log_a), 0.0))
    return r, ig, sp, a, mult


def odd_pre(x, keep, g, win, cw, cb, wr, br, wi, bi, lam):
    S = x.shape[0]
    ts = _tile_rows(S, 512)

    def body(x_ref, xp_ref, keep_ref, g_ref, win_ref, cw_ref, cb_ref, wr_ref, br_ref, wi_ref, bi_ref, lam_ref,
             z_ref, a_ref, b_ref):
        i = pl.program_id(0)
        h, _ = _rms(x_ref[...], g_ref[...])
        z = _dot(h, win_ref[...])
        z_ref[...] = z
        hp, _ = _rms(xp_ref[...], g_ref[...])
        xprev = _dot(hp, win_ref[:, D:]) * (i > 0).astype(F32)
        xb = _conv_fwd(xprev, z[:, D:], cw_ref, cb_ref[...])
        keep_ = keep_ref[...]
        _, ig, _, a, mult = _gates(xb, keep_, wr_ref, br_ref[...], wi_ref, bi_ref[...], lam_ref[...])
        a_ref[...] = a * keep_
        b_ref[...] = jnp.where(keep_ > 0.0, mult, 1.0) * (ig * xb)

    ins = [x, x, keep, g, win, cw, cb, wr, br, wi, bi, lam]
    return pl.pallas_call(
        body, name="odd_pre", grid=(S // ts,),
        in_specs=[_row(ts, D), _prev(CONV_HALO, D, ts), _row(ts, 1)] + [_const(v.shape) for v in ins[3:]],
        out_specs=[_row(ts, 2 * D), _row(ts, D), _row(ts, D)],
        out_shape=[_sds((S, 2 * D), F32), _sds((S, D), F32), _sds((S, D), F32)], compiler_params=_cp())(*ins)


def lru_scan(a, b, reverse=False):
    S = a.shape[0]
    ts = _tile_rows(S, 512)
    n = S // ts
    groups = ts // 8

    def body(a_ref, an_ref, b_ref, h_ref, carry_ref, ash_ref):
        i = pl.program_id(0)

        @pl.when(i == 0)
        def _():
            carry_ref[...] = jnp.zeros_like(carry_ref)

        rid = lax.broadcasted_iota(jnp.int32, (8, D), 0)
        if reverse:
            ext = jnp.concatenate([a_ref[...], an_ref[...] * (i > 0).astype(F32)], axis=0)
            ash_ref[...] = _roll(ext, -1, 0)[:ts]
        src = ash_ref if reverse else a_ref

        def group(j, carry):
            off = pl.multiple_of((groups - 1 - j if reverse else j) * 8, 8)
            a8, b8 = src[pl.ds(off, 8), :], b_ref[pl.ds(off, 8), :]
            for k in (1, 2, 4):
                inside = (rid < 8 - k) if reverse else (rid >= k)
                sh = -k if reverse else k
                a_sh = jnp.where(inside, _roll(a8, sh, 0), 1.0)
                b_sh = jnp.where(inside, _roll(b8, sh, 0), 0.0)
                b8 = a8 * b_sh + b8
                a8 = a8 * a_sh
            h8 = a8 * carry + b8
            h_ref[pl.ds(off, 8), :] = h8
            return h8[0:1, :] if reverse else h8[7:8, :]

        carry_ref[...] = lax.fori_loop(0, groups, group, carry_ref[...])

    if reverse:
        r = ts // 8
        tile = pl.BlockSpec((ts, D), lambda i: (n - 1 - i, 0))
        halo = pl.BlockSpec((8, D), lambda i: (jnp.minimum((n - i) * r, n * r - 1), 0))
    else:
        tile, halo = _row(ts, D), _prev(8, D, ts)
    return pl.pallas_call(
        body, name="lru_scan_rev" if reverse else "lru_scan", grid=(n,), in_specs=[tile, halo, tile],
        out_specs=tile, out_shape=_sds((S, D), F32),
        scratch_shapes=[pltpu.VMEM((1, D), F32), pltpu.VMEM((ts, D), F32)], compiler_params=_cp())(a, a, b)


def odd_post(x, z, hseq, wout):
    S = x.shape[0]
    ts = _tile_rows(S, 512)

    def body(x_ref, gate_ref, h_ref, w_ref, out_ref):
        gl, _ = _gelu(gate_ref[...])
        out_ref[...] = x_ref[...] + _dot(gl * h_ref[...], w_ref[...])

    return pl.pallas_call(
        body, name="odd_post", grid=(S // ts,),
        in_specs=[_row(ts, D), _row(ts, D), _row(ts, D), _const(wout.shape)],
        out_specs=_row(ts, D), out_shape=_sds((S, D), F32), compiler_params=_cp())(x, z, hseq, wout)


def odd_post_bwd(dy, z, hseq, wout):
    S = dy.shape[0]
    ts = _tile_rows(S, 512)

    def body(dy_ref, gate_ref, h_ref, w_ref, y_ref, dgate_ref, dh_ref):
        gate, hs = gate_ref[...], h_ref[...]
        gl, t = _gelu(gate)
        y_ref[...] = (gl * hs).astype(BF16)
        dyy = _dot_nt(dy_ref[...], w_ref[...])
        dgate_ref[...] = dyy * hs * _gelu_grad(gate, t)
        dh_ref[...] = dyy * gl

    return pl.pallas_call(
        body, name="odd_post_bwd", grid=(S // ts,),
        in_specs=[_row(ts, D), _row(ts, D), _row(ts, D), _const(wout.shape)],
        out_specs=[_row(ts, D)] * 3, out_shape=[_sds((S, D), BF16), _sds((S, D), F32), _sds((S, D), F32)],
        compiler_params=_cp())(dy, z, hseq, wout)


def odd_gates_bwd(z, lam_grad, hseq, keep, cw, cb, wr, br, wi, bi, lam):
    S = z.shape[0]
    ts = _tile_rows(S, 512)

    def body(xbp_ref, xbpp_ref, lg_ref, h_ref, hp_ref, keep_ref, cw_ref, cb_ref, wr_ref, br_ref, wi_ref,
             bi_ref, lam_ref, dxb_ref, dcb_ref, dbr_ref, dbi_ref, dlam_ref, dwr_ref, dwi_ref):
        i = pl.program_id(0)

        @pl.when(i == 0)
        def _():
            for ref in (dcb_ref, dbr_ref, dbi_ref, dlam_ref, dwr_ref, dwi_ref):
                ref[...] = jnp.zeros_like(ref)

        first = (i > 0).astype(F32)
        xb = _conv_fwd(xbpp_ref[...] * first, xbp_ref[...], cw_ref, cb_ref[...])
        keep_ = keep_ref[...]
        lam_ = lam_ref[...]
        r, ig, sp, a, mult = _gates(xb, keep_, wr_ref, br_ref[...], wi_ref, bi_ref[...], lam_)
        hs = h_ref[...]
        hprev = _roll(jnp.concatenate([hp_ref[...] * first, hs], axis=0), 1, 0)[CONV_HALO:]
        lg = lg_ref[...]
        da = lg * hprev * keep_
        ixb = ig * xb
        dmult = lg * ixb * keep_
        dixb = lg * jnp.where(keep_ > 0.0, mult, 1.0)
        dlog_a = da * a - dmult * jnp.where(mult > 0.0, a * a / mult, 0.0)
        dr = dlog_a * (-LRU_C * sp)
        dlam_ref[...] += _rowsum(dlog_a * (-LRU_C * r)) * (-jax.nn.sigmoid(-lam_))
        dpr = dr * r * (1.0 - r)
        dpi = dixb * xb * ig * (1.0 - ig)
        dbr_ref[...] += _rowsum(dpr)
        dbi_ref[...] += _rowsum(dpi)
        dxb = dixb * ig
        parts = []
        for h in range(LRU_HEADS):
            sl = slice(h * LRU_HEAD_DIM, (h + 1) * LRU_HEAD_DIM)
            dwr_ref[h] += _dot_tn(xb[:, sl], dpr[:, sl])
            dwi_ref[h] += _dot_tn(xb[:, sl], dpi[:, sl])
            parts.append(_dot_nt(dpr[:, sl], wr_ref[h]) + _dot_nt(dpi[:, sl], wi_ref[h]))
        dxb = dxb + jnp.concatenate(parts, axis=1)
        dxb_ref[...] = dxb
        dcb_ref[...] += _rowsum(dxb)

    ins = [z, z, lam_grad, hseq, hseq, keep, cw, cb, wr, br, wi, bi, lam]
    in_specs = [_row(ts, D, 1), _prev(CONV_HALO, D, ts, 1), _row(ts, D), _row(ts, D), _prev(CONV_HALO, D, ts),
                _row(ts, 1)] + [_const(v.shape) for v in ins[6:]]
    gshape = (LRU_HEADS, LRU_HEAD_DIM, LRU_HEAD_DIM)
    return pl.pallas_call(
        body, name="odd_gates_bwd", grid=(S // ts,), in_specs=in_specs,
        out_specs=[_row(ts, D)] + [_acc((1, D))] * 4 + [_acc(gshape)] * 2,
        out_shape=[_sds((S, D), F32)] + [_sds((1, D), F32)] * 4 + [_sds(gshape, F32)] * 2,
        compiler_params=_cp())(*ins)


def odd_pre_bwd(x, dy, z, dxb, dgate, g, cw, win):
    S = x.shape[0]
    ts = _tile_rows(S, 512)
    n = S // ts

    def body(x_ref, dy_ref, xbp_ref, xbpp_ref, dxb_ref, dxbn_ref, dgate_ref, g_ref, cw_ref, win_ref,
             dx_ref, h_ref, dz_ref, dcw_ref, dg_ref):
        i = pl.program_id(0)

        @pl.when(i == 0)
        def _():
            dcw_ref[...] = jnp.zeros_like(dcw_ref)
            dg_ref[...] = jnp.zeros_like(dg_ref)

        dxb = dxb_ref[...]
        extd = jnp.concatenate([dxb, dxbn_ref[...] * (i < n - 1).astype(F32)], axis=0)
        extx = jnp.concatenate([xbpp_ref[...] * (i > 0).astype(F32), xbp_ref[...]], axis=0)
        dxbp = cw_ref[3:4, :] * dxb
        dcw_ref[3:4, :] += _rowsum(dxb * xbp_ref[...])
        for k in range(3):
            dxbp = dxbp + cw_ref[k:k + 1, :] * _roll(extd, -(3 - k), 0)[:ts]
            dcw_ref[k:k + 1, :] += _rowsum(dxb * _roll(extx, 3 - k, 0)[CONV_HALO:])
        dz = jnp.concatenate([dgate_ref[...], dxbp], axis=1)
        dz_ref[...] = dz.astype(BF16)
        x_ = x_ref[...]
        h, rstd = _rms(x_, g_ref[...])
        h_ref[...] = h.astype(BF16)
        dxn, dgr = _rms_bwd(x_, g_ref[...], rstd, _dot_nt(dz, win_ref[...]))
        dx_ref[...] = dy_ref[...] + dxn
        dg_ref[...] += _rowsum(dgr)

    ins = [x, dy, z, z, dxb, dxb, dgate, g, cw, win]
    in_specs = [_row(ts, D), _row(ts, D), _row(ts, D, 1), _prev(CONV_HALO, D, ts, 1), _row(ts, D),
                _next(CONV_HALO, D, ts, n), _row(ts, D)] + [_const(v.shape) for v in ins[7:]]
    return pl.pallas_call(
        body, name="odd_pre_bwd", grid=(n,), in_specs=in_specs,
        out_specs=[_row(ts, D), _row(ts, D), _row(ts, 2 * D), _acc((4, D)), _acc((1, D))],
        out_shape=[_sds((S, D), F32), _sds((S, D), BF16), _sds((S, 2 * D), BF16), _sds((4, D), F32),
                   _sds((1, D), F32)],
        compiler_params=_cp())(*ins)


def loss_head(x, target, g):
    S = x.shape[0]
    ts = _tile_rows(S, 512)

    def body(x_ref, t_ref, g_ref, dx_ref, dg_ref, loss_ref):
        @pl.when(pl.program_id(0) == 0)
        def _():
            dg_ref[...] = jnp.zeros_like(dg_ref)
            loss_ref[...] = jnp.zeros_like(loss_ref)

        x_ = x_ref[...]
        y, rstd = _rms(x_, g_ref[...])
        err = y - t_ref[...]
        loss_ref[...] += 0.5 * _rowsum(jnp.mean(err * err, axis=1, keepdims=True))
        dxn, dgr = _rms_bwd(x_, g_ref[...], rstd, err * (1.0 / D))
        dx_ref[...] = dxn
        dg_ref[...] += _rowsum(dgr)

    return pl.pallas_call(
        body, name="loss_head", grid=(S // ts,), in_specs=[_row(ts, D), _row(ts, D), _const(g.shape)],
        out_specs=[_row(ts, D), _acc((1, D)), _acc((1, 1))],
        out_shape=[_sds((S, D), F32), _sds((1, D), F32), _sds((1, 1), F32)], compiler_params=_cp())(x, target, g)


def even_post_bwd(dy, o, wo_pool, wo_att):
    S = dy.shape[0]
    ts = _tile_rows(S, 512)

    def body(dy_ref, o_ref, wp_ref, wa_ref, dyp_ref, do_ref, delta_ref):
        dy_ = dy_ref[...]
        dyp_ref[...] = _dot_nt(dy_, wp_ref[...])
        do = _dot_nt(dy_, wa_ref[...])
        do_ref[...] = do.astype(BF16)
        prod = do * o_ref[...]
        for h in range(MLA_HEADS):
            delta_ref[h] = _as_row(jnp.sum(prod[:, h * LANES:(h + 1) * LANES], axis=1, keepdims=True))

    return pl.pallas_call(
        body, name="even_post_bwd", grid=(S // ts,),
        in_specs=[_row(ts, D), _row(ts, D), _const(wo_pool.shape), _const(wo_att.shape)],
        out_specs=[_row(ts, POOL_DIM), _row(ts, D),
                   pl.BlockSpec((MLA_HEADS, None, 1, ts), lambda i: (0, i, 0, 0))],
        out_shape=[_sds((S, POOL_DIM), F32), _sds((S, D), BF16), _sds((MLA_HEADS, S // ts, 1, ts), F32)],
        compiler_params=_cp())(dy, o, wo_pool, wo_att)


def attn_bwd(qp, kp, vp, do, lse_row, delta_row, token=None):
    S = qp.shape[0]
    tk = _tile_rows(S, 512)
    nq = S // tk
    extra, extra_specs = _after(token)

    def body(q_ref, k_ref, v_ref, do_ref, lse_ref, delta_ref, *rest):
        dq_ref, dk_ref, dv_ref = rest[-3:]
        kj = pl.program_id(1)

        @pl.when(kj == 0)
        def _():
            dq_ref[...] = jnp.zeros_like(dq_ref)

        k, v = k_ref[...], v_ref[...]

        def block(qi, carry, masked):
            dk, dv = carry
            off = pl.multiple_of(qi * tk, tk)
            q = q_ref[pl.ds(off, tk), :]
            do_ = do_ref[pl.ds(off, tk), :]
            st = _dot_nt(k, q)
            if masked:
                row = lax.broadcasted_iota(jnp.int32, (tk, tk), 0)
                col = lax.broadcasted_iota(jnp.int32, (tk, tk), 1)
                st = jnp.where(col >= row, st, -1e30)
            pt = _exp2(st - lse_ref[qi])
            dv = dv + _dot(pt, do_)
            dst = (pt * (_dot_nt(v, do_) - delta_ref[qi])).astype(BF16)
            dk = dk + _dot(dst, q)
            dq_ref[pl.ds(off, tk), :] += _dot_tn(dst, k)
            return dk, dv

        zero = jnp.zeros((tk, LANES), F32)
        carry = block(kj, (zero, zero), True)
        dk, dv = _pair_loop(kj + 1, nq, lambda qi, c: block(qi, c, False), carry, unrolls=(4, 2, 1))
        dk_ref[...] = dk * LN_2
        dv_ref[...] = dv

    blk = pl.BlockSpec((tk, LANES), lambda h, j: (j, h))
    full = pl.BlockSpec((S, LANES), lambda h, j: (0, h))
    rowv = pl.BlockSpec((None, nq, 1, tk), lambda h, j: (h, 0, 0, 0))
    return pl.pallas_call(
        body, name="attn_bwd", grid=(MLA_HEADS, nq), in_specs=[full, blk, blk, full, rowv, rowv] + extra_specs,
        out_specs=[full, blk, blk], out_shape=[_sds((S, D), F32)] * 3, compiler_params=_cp2())(
            qp, kp, vp, do, lse_row, delta_row, *extra)


def even_pre_bwd(x, dy, z, dq, dk, dv, dyp, tabs, g, win, pw, pscale, qg, wq, kvg, wk, wv):
    S = x.shape[0]
    ts = _tile_rows(S, 512)
    n = S // ts

    def body(x_ref, dy_ref, z_ref, up_ref, dq_ref, dk_ref, dv_ref, dyp_ref, dypn_ref, c_ref, a_ref, b_ref,
             g_ref, win_ref, pw_ref, ps_ref, qg_ref, wq_ref, kvg_ref, wk_ref, wv_ref,
             dx_ref, h_ref, dz_ref, dg_ref, dpw_ref, dps_ref, dqg_ref, dwq_ref, dkvg_ref, dwk_ref, dwv_ref):
        i = pl.program_id(0)

        @pl.when(i == 0)
        def _():
            for ref in (dg_ref, dpw_ref, dps_ref, dqg_ref, dwq_ref, dkvg_ref, dwk_ref, dwv_ref):
                ref[...] = jnp.zeros_like(ref)

        z = z_ref[...]
        c, a, b = c_ref[...], a_ref[...], b_ref[...]
        ps = ps_ref[...]
        u = z[:, :POOL_DIM]
        pooled = _pooled(up_ref[...] * (i > 0).astype(F32), u, i * ts)
        dyp_ = dyp_ref[...]
        dps_ref[...] += _rowsum(dyp_ * _blockdot(pooled, pw_ref, 4, LANES))
        ext = jnp.concatenate([dyp_, dypn_ref[...] * (i < n - 1).astype(F32)], axis=0) * ps
        for gidx in range(4):
            sl = slice(gidx * LANES, (gidx + 1) * LANES)
            dpw_ref[gidx] += _dot_tn(pooled[:, sl], ext[:ts, sl])
        dpooled = jnp.concatenate(
            [_dot_nt(ext[:, gidx * LANES:(gidx + 1) * LANES], pw_ref[gidx]) for gidx in range(4)], axis=1)
        dm = dpooled / _pool_cnt(i * ts, ts + POOL_HALO)
        du = _pool_windows(dm, -1)[:ts] - dpooled[:ts]
        cq = z[:, 512:768]
        cqn, rstd_q = _rms(cq, qg_ref[...])
        dqf = _rope_bwd(dq_ref[...] * ATTN_SCALE, c, a, b)
        dwq_ref[...] += _dot_tn(cqn, dqf)
        dcq, dqg_rows = _rms_bwd(cq, qg_ref[...], rstd_q, _dot_nt(dqf, wq_ref[...]))
        dqg_ref[...] += _rowsum(dqg_rows)
        ckv = z[:, 768:896]
        ckvn, rstd_kv = _rms(ckv, kvg_ref[...])
        dk_, dv_ = dk_ref[...], dv_ref[...]
        dwk_ref[...] += _dot_tn(ckvn, dk_)
        dwv_ref[...] += _dot_tn(ckvn, dv_)
        dckv, dkvg_rows = _rms_bwd(ckv, kvg_ref[...], rstd_kv,
                                   _dot_nt(dk_, wk_ref[...]) + _dot_nt(dv_, wv_ref[...]))
        dkvg_ref[...] += _rowsum(dkvg_rows)
        dkr = dk_[:, :LANES]
        for h in range(1, MLA_HEADS):
            dkr = dkr + dk_[:, h * LANES:(h + 1) * LANES]
        lane = lax.broadcasted_iota(jnp.int32, (ts, LANES), 1)
        dkr = jnp.where((lane >= 64) & (lane < 96), _rope_bwd(dkr, c, a, b), 0.0)
        dz = jnp.concatenate([du, dcq, dckv, dkr], axis=1)
        dz_ref[...] = dz.astype(BF16)
        x_ = x_ref[...]
        h, rstd = _rms(x_, g_ref[...])
        h_ref[...] = h.astype(BF16)
        dxn, dgr = _rms_bwd(x_, g_ref[...], rstd, _dot_nt(dz, win_ref[...]))
        dx_ref[...] = dy_ref[...] + dxn
        dg_ref[...] += _rowsum(dgr)

    ins = [x, dy, z, z, dq, dk, dv, dyp, dyp, *tabs, g, win, pw, pscale, qg, wq, kvg, wk, wv]
    in_specs = [_row(ts, D), _row(ts, D), _row(ts, D), _prev(POOL_HALO, POOL_DIM, ts), _row(ts, D), _row(ts, D),
                _row(ts, D), _row(ts, POOL_DIM), _next(POOL_HALO, POOL_DIM, ts, n), _row(ts, LANES),
                _row(ts, LANES), _row(ts, LANES)] + [_const(v.shape) for v in ins[12:]]
    acc_shapes = [(1, D), (4, LANES, LANES), (1, POOL_DIM), (1, Q_LORA), (Q_LORA, D), (1, KV_LORA), (KV_LORA, D),
                  (KV_LORA, D)]
    return pl.pallas_call(
        body, name="even_pre_bwd", grid=(n,), in_specs=in_specs,
        out_specs=[_row(ts, D)] * 3 + [_acc(s) for s in acc_shapes],
        out_shape=[_sds((S, D), F32), _sds((S, D), BF16), _sds((S, D), BF16)] + [_sds(s, F32) for s in acc_shapes],
        compiler_params=_cp())(*ins)


def _pick(n, options):
    for o in options:
        if n % o == 0:
            return o
    return n


def matmul_tn(name, a, b):
    out_dtype = BF16
    S = a.shape[-2]
    ts = _tile_rows(S, 2048)
    steps = S // ts

    def body(a_ref, b_ref, o_ref, acc_ref):
        s = pl.program_id(2)

        @pl.when(s == 0)
        def _():
            acc_ref[...] = jnp.zeros_like(acc_ref)

        acc_ref[...] += _dot_tn(a_ref[...], b_ref[...])

        @pl.when(s == steps - 1)
        def _():
            o_ref[...] = acc_ref[...].astype(o_ref.dtype)

    if a.ndim == 3:
        C, _, K = a.shape
        N = b.shape[1]
        tn = _pick(N, (512, 256, 128))
        grid = (C, N // tn, S // ts)
        in_specs = [pl.BlockSpec((None, ts, K), lambda c, j, s: (c, s, 0)),
                    pl.BlockSpec((ts, tn), lambda c, j, s: (s, j))]
        out_spec, out_shape, tile = pl.BlockSpec((None, K, tn), lambda c, j, s: (c, 0, j)), (C, K, N), (K, tn)
    elif b.ndim == 3:
        C, _, N = b.shape
        K = a.shape[1]
        tk = _pick(K, (512, 256, 128))
        grid = (C, K // tk, S // ts)
        in_specs = [pl.BlockSpec((ts, tk), lambda c, i, s: (s, i)),
                    pl.BlockSpec((None, ts, N), lambda c, i, s: (c, s, 0))]
        out_spec, out_shape, tile = pl.BlockSpec((None, tk, N), lambda c, i, s: (c, i, 0)), (C, K, N), (tk, N)
    else:
        K, N = a.shape[1], b.shape[1]
        tk = _pick(K, (512, 256, 128))
        tn = _pick(N, (512, 256, 128))
        grid = (K // tk, N // tn, S // ts)
        in_specs = [pl.BlockSpec((ts, tk), lambda i, j, s: (s, i)), pl.BlockSpec((ts, tn), lambda i, j, s: (s, j))]
        out_spec, out_shape, tile = pl.BlockSpec((tk, tn), lambda i, j, s: (i, j)), (K, N), (tk, tn)
    return pl.pallas_call(
        body, name=name, grid=grid, in_specs=in_specs, out_specs=out_spec, out_shape=_sds(out_shape, out_dtype),
        scratch_shapes=[pltpu.VMEM(tile, F32)], compiler_params=pltpu.CompilerParams(dimension_semantics=("arbitrary",) * 3, vmem_limit_bytes=VMEM_LIMIT))(
            a, b)


def _my_id():
    return lax.axis_index("x") * 4 + lax.axis_index("y") * 2 + lax.axis_index("c")


def _peer(j):
    x, y, c = lax.axis_index("x"), lax.axis_index("y"), lax.axis_index("c")
    px = 1 - x if j & 4 else x
    py = 1 - y if j & 2 else y
    pc = 1 - c if j & 1 else c
    return (px, py, pc), px * 4 + py * 2 + pc


def all_gather(name, arrays):
    n = len(arrays)

    def body(*refs):
        ins, outs = refs[:n], refs[n:2 * n]
        send_sems, recv_sems, local_sems = refs[2 * n:]
        me = _my_id()
        local = [pltpu.make_async_copy(ins[k], outs[k].at[me], local_sems.at[k]) for k in range(n)]
        for cp in local:
            cp.start()
        sends = []
        for j in range(1, N_DEV):
            peer, _ = _peer(j)
            for k in range(n):
                cp = pltpu.make_async_remote_copy(
                    src_ref=ins[k], dst_ref=outs[k].at[me], send_sem=send_sems.at[k, j - 1],
                    recv_sem=recv_sems.at[k, j - 1], device_id=peer, device_id_type=pl.DeviceIdType.MESH)
                cp.start()
                sends.append(cp)
        for j in range(1, N_DEV):
            peer, pid = _peer(j)
            for k in range(n):
                pltpu.make_async_remote_copy(
                    src_ref=ins[k], dst_ref=outs[k].at[pid], send_sem=send_sems.at[k, j - 1],
                    recv_sem=recv_sems.at[k, j - 1], device_id=peer, device_id_type=pl.DeviceIdType.MESH).wait_recv()
        for cp in sends:
            cp.wait_send()
        for cp in local:
            cp.wait()

    any_spec = pl.BlockSpec(memory_space=pl.ANY)
    return pl.pallas_call(
        body, name=name, in_specs=[any_spec] * n, out_specs=[any_spec] * n,
        out_shape=[_sds((N_DEV,) + a.shape, a.dtype) for a in arrays],
        scratch_shapes=[pltpu.SemaphoreType.DMA((n, N_DEV - 1)), pltpu.SemaphoreType.DMA((n, N_DEV - 1)),
                        pltpu.SemaphoreType.DMA((n,))],
        compiler_params=pltpu.CompilerParams(has_side_effects=True))(*arrays)


def exchange(name, arrays, gathers=()):
    n_ex, n = len(arrays), len(arrays) + len(gathers)

    def body(*refs):
        ins, outs = refs[:n], refs[n:2 * n]
        send_sems, recv_sems, local_sems = refs[2 * n:]
        me = _my_id()

        def mine(k, slot):
            return ins[k].at[slot] if k < n_ex else ins[k]

        local = [pltpu.make_async_copy(mine(k, me), outs[k].at[me], local_sems.at[k]) for k in range(n)]
        for cp in local:
            cp.start()
        sends = []
        for j in range(1, N_DEV):
            peer, pid = _peer(j)
            for k in range(n):
                cp = pltpu.make_async_remote_copy(
                    src_ref=mine(k, pid), dst_ref=outs[k].at[me], send_sem=send_sems.at[k, j - 1],
                    recv_sem=recv_sems.at[k, j - 1], device_id=peer, device_id_type=pl.DeviceIdType.MESH)
                cp.start()
                sends.append(cp)
        for j in range(1, N_DEV):
            peer, pid = _peer(j)
            for k in range(n):
                pltpu.make_async_remote_copy(
                    src_ref=mine(k, me), dst_ref=outs[k].at[pid], send_sem=send_sems.at[k, j - 1],
                    recv_sem=recv_sems.at[k, j - 1], device_id=peer, device_id_type=pl.DeviceIdType.MESH).wait_recv()
        for cp in sends:
            cp.wait_send()
        for cp in local:
            cp.wait()

    any_spec = pl.BlockSpec(memory_space=pl.ANY)
    return pl.pallas_call(
        body, name=name, in_specs=[any_spec] * n, out_specs=[any_spec] * n,
        out_shape=[_sds(a.shape, a.dtype) for a in arrays] + [_sds((N_DEV,) + a.shape, a.dtype) for a in gathers],
        scratch_shapes=[pltpu.SemaphoreType.DMA((n, N_DEV - 1)), pltpu.SemaphoreType.DMA((n, N_DEV - 1)),
                        pltpu.SemaphoreType.DMA((n,))],
        compiler_params=pltpu.CompilerParams(has_side_effects=True))(*arrays, *gathers)


_HBM = pl.BlockSpec(memory_space=pltpu.HBM)
_SEM = pl.BlockSpec(memory_space=pltpu.SEMAPHORE)
_DATAFLOW = pltpu.SideEffectType.DATAFLOW_SIDE_EFFECTING


def _in_hbm(v):
    return pltpu.with_memory_space_constraint(v, pltpu.HBM)


N_PEERS = N_DEV - 1


def _split_copy(k, j, srcs, lands, send_sems, recv_sems, gather, slot):
    peer, pid = _peer(j)
    return pltpu.make_async_remote_copy(
        src_ref=srcs[k] if gather else srcs[k].at[pid], dst_ref=lands[k].at[_my_id() if slot == "mine" else pid],
        send_sem=send_sems[j - 1], recv_sem=recv_sems[j - 1], device_id=peer, device_id_type=pl.DeviceIdType.MESH)


def split_start(name, arrays, gather):
    n = len(arrays)
    lands = [lax.empty((N_DEV,) + a.shape if gather else a.shape, a.dtype) for a in arrays]

    def body(*refs):
        srcs, lnds = refs[:n], refs[n:2 * n]
        sems = refs[4 * n:4 * n + 2 * N_PEERS]
        token = refs[-1]
        for j in range(1, N_DEV):
            for k in range(n):
                _split_copy(k, j, srcs, lnds, sems[:N_PEERS], sems[N_PEERS:], gather, "mine").start()
        token[...] = jnp.zeros_like(token)

    out = pl.pallas_call(
        body, name=name,
        out_shape=(*[pltpu.HBM(a.shape, a.dtype) for a in arrays], *[pltpu.HBM(l.shape, l.dtype) for l in lands],
                   *[pltpu.SemaphoreType.DMA(())] * (2 * N_PEERS), _sds((8, LANES), F32)),
        in_specs=[_HBM] * (2 * n),
        out_specs=(*[_HBM] * (2 * n), *[_SEM] * (2 * N_PEERS), pl.BlockSpec(memory_space=pltpu.VMEM)),
        input_output_aliases={k: k for k in range(2 * n)},
        compiler_params=pltpu.CompilerParams(has_side_effects=_DATAFLOW))(
            *[_in_hbm(a) for a in arrays], *[_in_hbm(l) for l in lands])
    sems = list(out[2 * n:2 * n + 2 * N_PEERS])
    return sems[:N_PEERS], sems[N_PEERS:], list(out[:n]), list(out[n:2 * n]), out[-1]


def split_wait(name, handle, after, gather):
    send_sems, recv_sems, srcs, lands, _ = handle
    n = len(srcs)

    def body(*refs):
        srcs_r, lnds_r = refs[:n], refs[n:2 * n]
        sems = refs[2 * n:2 * n + 2 * N_PEERS]
        for j in range(1, N_DEV):
            for k in range(n):
                cp = _split_copy(k, j, srcs_r, lnds_r, sems[:N_PEERS], sems[N_PEERS:], gather, "peer")
                cp.wait_send()
                cp.wait_recv()

    out = pl.pallas_call(
        body, name=name, out_shape=tuple(pltpu.HBM(a.shape, a.dtype) for a in srcs + lands),
        in_specs=[_HBM] * (2 * n) + [_SEM] * (2 * N_PEERS) + [pl.BlockSpec(memory_space=pl.ANY)],
        out_specs=tuple([_HBM] * (2 * n)), input_output_aliases={k: k for k in range(2 * n)},
        compiler_params=pltpu.CompilerParams(has_side_effects=_DATAFLOW))(
            *srcs, *lands, *send_sems, *recv_sems, after)
    return list(out[:n]), list(out[n:])


def _fill_own_slot(src, land, gather):
    me = _my_id()
    own = src[None] if gather else lax.dynamic_index_in_dim(src, me, 0, keepdims=True)
    return lax.dynamic_update_slice_in_dim(land, own, me, 0)


ADAMW_BLOCK_ELEMS = 128 * 1024


def adamw(name, parts, w, m, v):
    R, C = w.shape
    tr = _pick(R, [t for t in (512, 256, 128, 64, 32, 16, 8) if t * C <= ADAMW_BLOCK_ELEMS])
    c1 = 1.0 - ADAM_B1 ** ADAM_STEP
    c2 = 1.0 - ADAM_B2 ** ADAM_STEP

    def body(p_ref, w_ref, m_ref, v_ref, g_ref, d_ref, nm_ref, nv_ref):
        g = p_ref[0].astype(F32)
        for s in range(1, N_DEV):
            g = g + p_ref[s].astype(F32)
        g_ref[...] = g
        m_ = ADAM_B1 * m_ref[...] + (1.0 - ADAM_B1) * g
        v_ = ADAM_B2 * v_ref[...] + (1.0 - ADAM_B2) * (g * g)
        nm_ref[...] = m_
        nv_ref[...] = v_
        d_ref[...] = -ADAM_LR * ((m_ / c1) / (jnp.sqrt(v_ / c2) + ADAM_EPS) + ADAM_WD * w_ref[...])

    row = pl.BlockSpec((tr, C), lambda i: (i, 0))
    return pl.pallas_call(
        body, name=name, grid=(R // tr,),
        in_specs=[pl.BlockSpec((N_DEV, tr, C), lambda i: (0, i, 0)), row, row, row], out_specs=[row] * 4,
        out_shape=[_sds((R, C), F32)] * 4, compiler_params=_cp())(parts, w, m, v)


WEIGHTS = ['ev_norm', 'ev_w_in', 'ev_pool_w', 'ev_pool_scale', 'ev_q_norm', 'ev_w_q_up', 'ev_kv_norm', 'ev_w_kv_up',
           'ev_w_out', 'od_norm', 'od_w_in', 'od_conv_w', 'od_conv_b', 'od_w_rgate', 'od_b_rgate', 'od_w_igate',
           'od_b_igate', 'od_lambda', 'od_w_out', 'xa_norm_x', 'xa_norm_mem', 'xa_w_q', 'xa_w_kv', 'xa_w_o',
           'ffn_norm', 'ffn_w_gate_up', 'ffn_w_down', 'final_norm']
SHARD_AXIS = {'ev_w_in': 1, 'ev_w_q_up': 2, 'ev_w_kv_up': 2, 'ev_w_out': 1, 'od_norm': 1, 'od_w_in': 2,
              'od_conv_w': 2, 'od_conv_b': 1, 'od_w_rgate': 2, 'od_b_rgate': 1, 'od_w_igate': 2, 'od_b_igate': 1,
              'od_lambda': 1, 'od_w_out': 1, 'xa_w_q': 1, 'xa_w_kv': 2, 'xa_w_o': 1, 'ffn_w_gate_up': 2,
              'ffn_w_down': 1}
SMALL_F32 = ('od_norm', 'od_conv_w', 'od_conv_b', 'od_b_rgate', 'od_b_igate', 'od_lambda')
STACKED = ('ffn_w_gate_up', 'ffn_w_down')
SHARDED = [n for n in WEIGHTS if n in SHARD_AXIS]
REPLICATED = [n for n in WEIGHTS if n not in SHARD_AXIS]
ROW_ALIGN = 512


def _pack(flats, dtype):
    v = jnp.concatenate([f.reshape(-1).astype(dtype) for f in flats])
    pad = (-v.shape[0]) % (ROW_ALIGN * LANES)
    return jnp.pad(v, (0, pad)).reshape(-1, LANES)


def _rows8(n_elems):
    return -(-n_elems // (8 * LANES)) * 8


def _pack_rows(arrays, lead=False):
    out = []
    for a in arrays:
        r = a.reshape((N_DEV, -1, LANES) if lead else (-1, LANES))
        pad = _rows8(r.shape[-2] * LANES) - r.shape[-2]
        out.append(jnp.pad(r, [(0, 0)] * (r.ndim - 2) + [(0, pad), (0, 0)]))
    return jnp.concatenate(out, axis=-2)


def _unpack_rows(buf, shapes, lead=False):
    out, off = [], 0
    for s in shapes:
        n = 1
        for d in s:
            n *= d
        rows = buf[..., off:off + n // LANES, :]
        out.append(rows.reshape(((N_DEV,) if lead else ()) + tuple(s)))
        off += _rows8(n)
    return out


def _unpack(flat, shapes):
    out, off = [], 0
    v = flat.reshape(-1)
    for s in shapes:
        n = 1
        for d in s:
            n *= d
        out.append(v[off:off + n].reshape(s))
        off += n
    return out


def _to_full(stacked, axis):
    v = jnp.moveaxis(stacked, 0, axis)
    s = v.shape
    return v.reshape(s[:axis] + (s[axis] * s[axis + 1],) + s[axis + 2:])


def _to_shards(full, axis):
    s = full.shape
    v = full.reshape(s[:axis] + (N_DEV, s[axis] // N_DEV) + s[axis + 1:])
    return jnp.moveaxis(v, axis, 0)


def _pad_heads(w, nh, dh, lead):
    s = w.shape
    v = w.reshape(s[:-1] + (nh, dh))
    v = jnp.pad(v, [(0, 0)] * (len(s) - 1) + [(0, 0), (lead, LANES - dh - lead)])
    return v.reshape(s[:-1] + (nh * LANES,))


def _unpad_heads(w, nh, dh, lead):
    s = w.shape
    return w.reshape(s[:-1] + (nh, LANES))[..., lead:lead + dh].reshape(s[:-1] + (nh * dh,))


def _rope_tables(positions):
    inv_freq = 10000.0 ** (-jnp.arange(0, 32, 2, dtype=F32) / 32)
    ang = positions.astype(F32)[:, None] * inv_freq
    cos, sin = jnp.cos(ang), jnp.sin(ang)
    S = positions.shape[0]
    one, zero = jnp.ones((S, 64), F32), jnp.zeros((S, 64), F32)
    z16, z32 = jnp.zeros((S, 16), F32), jnp.zeros((S, 32), F32)
    c = jnp.concatenate([one, cos, cos, jnp.ones((S, 32), F32)], axis=1)
    a = jnp.concatenate([zero, z16, sin, z32], axis=1)
    b = jnp.concatenate([zero, -sin, z16, z32], axis=1)
    return c, a, b


def _t(w):
    return jnp.swapaxes(w, -1, -2)


def device_step(x, mem, positions, target, W, fwd_token=None, late_weights=None, ship_grads=None):
    S = x.shape[0]
    G = {}
    tabs = _rope_tables(positions)
    keep = (positions != 0).astype(F32)[:, None]
    row = lambda v: v.reshape(1, -1)

    w_in = W['ev_w_in'][0]
    ev_win = jnp.concatenate([w_in[:, :896], _pad_heads(w_in[:, 896:], 1, 32, 64)], axis=1)
    ev_wq = _pad_heads(W['ev_w_q_up'][0], MLA_HEADS, QK_DIM, 0)
    kvw = W['ev_w_kv_up'][0].reshape(KV_LORA, MLA_HEADS, 128)
    ev_wk = _pad_heads(kvw[:, :, :64].reshape(KV_LORA, 512), MLA_HEADS, 64, 0)
    ev_wv = _pad_heads(kvw[:, :, 64:].reshape(KV_LORA, 512), MLA_HEADS, 64, 0)
    ev_wo_pool = W['ev_w_out'][0][:POOL_DIM]
    ev_wo_att = _t(_pad_heads(_t(W['ev_w_out'][0][POOL_DIM:]), MLA_HEADS, 64, 0))
    pw = W['ev_pool_w'][0].astype(BF16)
    ev_g, ps, qg, kvg = row(W['ev_norm'][0]), row(W['ev_pool_scale'][0]), row(W['ev_q_norm'][0]), row(W['ev_kv_norm'][0])

    z0, qp, kp, vp, ypool = even_pre(x, tabs, ev_g, ev_win, pw, ps, qg, ev_wq, kvg, ev_wk, ev_wv)
    o_att, lse = attn_fwd(qp, kp, vp, fwd_token)
    if late_weights is not None:
        W = {**W, **late_weights(lse)}
    x1 = even_post(x, ypool, o_att, ev_wo_pool, ev_wo_att)

    def xa_ffn_fwd(xin, l):
        mn, km, vm = mem_kv(mem, row(W['xa_norm_mem'][l]), W['xa_w_kv'][l])
        xm = xattn_fwd(xin, row(W['xa_norm_x'][l]), W['xa_w_q'][l], km, vm, W['xa_w_o'][l])
        xo, hf, gu = ffn_fwd(xm, row(W['ffn_norm'][l]), W['ffn_w_gate_up'], l,
                             W['ffn_w_down'][:, l].reshape(FF_HALF, FF_CHUNK, D))
        return xm, xo, (mn, km, vm, hf, gu)

    x2, x3, memkv0 = xa_ffn_fwd(x1, 0)

    od_g, lam = row(W['od_norm'][0]), row(W['od_lambda'][0])
    cw, cb = W['od_conv_w'][0], row(W['od_conv_b'][0])
    wr, wi = W['od_w_rgate'][0], W['od_w_igate'][0]
    br, bi = row(W['od_b_rgate'][0]), row(W['od_b_igate'][0])
    z1, a_t, b_t = odd_pre(x3, keep, od_g, W['od_w_in'][0], cw, cb, wr, br, wi, bi, lam)
    hseq = lru_scan(a_t, b_t)
    x4 = odd_post(x3, z1, hseq, W['od_w_out'][0])
    x5, x6, memkv1 = xa_ffn_fwd(x4, 1)

    dx, G['final_norm'], loss = loss_head(x6, target, row(W['final_norm']))
    G['final_norm'] = G['final_norm'].reshape(D)

    gnx, gnm, gwq, gwkv, gwo, gfn, gwgu, gwd = ([None, None] for _ in range(8))

    def xa_ffn_bwd(dy, xin, xm, memkv, l):
        mn, km, vm, hf, gu = memkv
        fg = row(W['ffn_norm'][l])
        act, dgu = ffn_bwd_a(dy, gu, W['ffn_w_down'][:, l].reshape(FF_HALF, FF_CHUNK, D))
        gwd[l] = matmul_tn("ffn_dwd", act, dy).reshape(N_DEV, D_FF // N_DEV, D)
        gwgu[l] = matmul_tn("ffn_dwgu", hf, dgu)
        dxm, dfg = ffn_bwd_b(xm, dy, dgu, fg, W['ffn_w_gate_up'], l)
        gfn[l] = dfg[0]
        dxin, o, dq, hx, dgx, dk, dv = xattn_bwd(xin, dxm, row(W['xa_norm_x'][l]), W['xa_w_q'][l], km, vm,
                                                  W['xa_w_o'][l])
        gnx[l] = dgx[0]
        gwo[l] = matmul_tn("xa_dwo", o, dxm)
        gwq[l] = matmul_tn("xa_dwq", hx, dq)
        dkv, dgm = mem_bwd(mem, row(W['xa_norm_mem'][l]), dk, dv, W['xa_w_kv'][l])
        gnm[l] = dgm[0]
        gwkv[l] = matmul_tn("xa_dwkv", mn, dkv)
        return dxin

    dx4 = xa_ffn_bwd(dx, x4, x5, memkv1, 1)

    y_od, dgate, dhs = odd_post_bwd(dx4, z1, hseq, W['od_w_out'][0])
    G['od_w_out'] = matmul_tn("od_dwout", y_od, dx4)[None]
    lam_grad = lru_scan(a_t, dhs, reverse=True)
    dxb, dcb, dbr, dbi, dlam, dwr, dwi = odd_gates_bwd(z1, lam_grad, hseq, keep, cw, cb, wr, br, wi, bi, lam)
    dx3, h_od, dz1, dcw, dg_od = odd_pre_bwd(x3, dx4, z1, dxb, dgate, od_g, cw, W['od_w_in'][0])
    G['od_w_in'] = matmul_tn("od_dwin", h_od, dz1)[None]
    G['od_norm'], G['od_conv_w'], G['od_conv_b'] = dg_od, dcw[None], dcb
    G['od_w_rgate'], G['od_b_rgate'], G['od_w_igate'], G['od_b_igate'], G['od_lambda'] = (
        dwr[None], dbr, dwi[None], dbi, dlam)

    dx1 = xa_ffn_bwd(dx3, x1, x2, memkv0, 0)
    G['xa_norm_x'], G['xa_norm_mem'], G['ffn_norm'] = jnp.stack(gnx), jnp.stack(gnm), jnp.stack(gfn)
    G['xa_w_q'], G['xa_w_kv'], G['xa_w_o'] = jnp.stack(gwq), jnp.stack(gwkv), jnp.stack(gwo)
    G['ffn_w_gate_up'], G['ffn_w_down'] = jnp.stack(gwgu, axis=1), jnp.stack(gwd, axis=1)
    bwd_token = ship_grads(G) if ship_grads is not None else None

    dyp, do_att, delta = even_post_bwd(dx1, o_att, ev_wo_pool, ev_wo_att)
    g_wo_pool = matmul_tn("ev_dwo_pool", ypool, dx1)
    g_wo_att = matmul_tn("ev_dwo_att", o_att, dx1)
    G['ev_w_out'] = jnp.concatenate([g_wo_pool, _t(_unpad_heads(_t(g_wo_att), MLA_HEADS, 64, 0))], axis=0)[None]
    dq, dk, dv = attn_bwd(qp, kp, vp, do_att, lse, delta, bwd_token)
    (grad_x, h_ev, dz0, dg_ev, dpw, dps, dqg, dwq, dkvg, dwk, dwv) = even_pre_bwd(
        x, dx1, z0, dq, dk, dv, dyp, tabs, ev_g, ev_win, pw, ps, qg, ev_wq, kvg, ev_wk, ev_wv)
    g_win = matmul_tn("ev_dwin", h_ev, dz0)
    G['ev_w_in'] = jnp.concatenate([g_win[:, :896], _unpad_heads(g_win[:, 896:], 1, 32, 64)], axis=1)[None]
    G['ev_norm'], G['ev_pool_w'], G['ev_pool_scale'], G['ev_q_norm'], G['ev_kv_norm'] = (
        dg_ev, dpw[None], dps, dqg, dkvg)
    G['ev_w_q_up'] = _unpad_heads(dwq, MLA_HEADS, QK_DIM, 0)[None]
    gk = _unpad_heads(dwk, MLA_HEADS, 64, 0).reshape(KV_LORA, MLA_HEADS, 64)
    gv = _unpad_heads(dwv, MLA_HEADS, 64, 0).reshape(KV_LORA, MLA_HEADS, 64)
    G['ev_w_kv_up'] = jnp.concatenate([gk, gv], axis=2).reshape(1, KV_LORA, MLA_HEADS * 128)
    return loss[0, 0], grad_x, G


def kernel(x, mem, positions, ev_norm, ev_w_in, ev_pool_w, ev_pool_scale, ev_q_norm, ev_w_q_up, ev_kv_norm, ev_w_kv_up, ev_w_out, od_norm, od_w_in, od_conv_w, od_conv_b, od_w_rgate, od_b_rgate, od_w_igate, od_b_igate, od_lambda, od_w_out, xa_norm_x, xa_norm_mem, xa_w_q, xa_w_kv, xa_w_o, ffn_norm, ffn_w_gate_up, ffn_w_down, final_norm, loss_target, m_ev_norm, m_ev_w_in, m_ev_pool_w, m_ev_pool_scale, m_ev_q_norm, m_ev_w_q_up, m_ev_kv_norm, m_ev_w_kv_up, m_ev_w_out, m_od_norm, m_od_w_in, m_od_conv_w, m_od_conv_b, m_od_w_rgate, m_od_b_rgate, m_od_w_igate, m_od_b_igate, m_od_lambda, m_od_w_out, m_xa_norm_x, m_xa_norm_mem, m_xa_w_q, m_xa_w_kv, m_xa_w_o, m_ffn_norm, m_ffn_w_gate_up, m_ffn_w_down, m_final_norm, v_ev_norm, v_ev_w_in, v_ev_pool_w, v_ev_pool_scale, v_ev_q_norm, v_ev_w_q_up, v_ev_kv_norm, v_ev_w_kv_up, v_ev_w_out, v_od_norm, v_od_w_in, v_od_conv_w, v_od_conv_b, v_od_w_rgate, v_od_b_rgate, v_od_w_igate, v_od_b_igate, v_od_lambda, v_od_w_out, v_xa_norm_x, v_xa_norm_mem, v_xa_w_q, v_xa_w_kv, v_xa_w_o, v_ffn_norm, v_ffn_w_gate_up, v_ffn_w_down, v_final_norm):
    args = dict(locals())
    w = {n: args[n] for n in WEIGHTS}
    m = {n: args['m_' + n] for n in WEIGHTS}
    v = {n: args['v_' + n] for n in WEIGHTS}
    big = [n for n in SHARDED if n not in SMALL_F32]
    small = [n for n in SHARDED if n in SMALL_F32]

    small_shapes = [w[n].shape for n in small]
    first = [n for n in big if n.startswith('ev_')]
    late = [n for n in big if n not in first]

    def full(n, st):
        return st if n in STACKED else _to_full(st, SHARD_AXIS[n])

    W = {n: w[n] for n in REPLICATED}
    W.update((n, full(n, st)) for n, st in zip(first, all_gather("gather_ev_weights", [w[n].astype(BF16) for n in first])))
    gather = split_start("gather_start", [w[n].astype(BF16) for n in late] + [_pack_rows([w[n] for n in small])], True)

    def late_weights(after):
        srcs, lands = split_wait("gather_wait", gather, after, True)
        lands = [_fill_own_slot(s, l, True) for s, l in zip(srcs, lands)]
        out = {n: full(n, st) for n, st in zip(late, lands)}
        out.update((n, _to_full(st, SHARD_AXIS[n])) for n, st in zip(small, _unpack_rows(lands[-1], small_shapes, True)))
        return out

    def shards(G, n):
        return G[n] if n in STACKED else _to_shards(G[n], SHARD_AXIS[n])

    shipped = []

    def ship_grads(G):
        shipped.append(split_start("exchange_start", [shards(G, n).astype(BF16) for n in late] +
                                   [_pack_rows([shards(G, n) for n in small], lead=True)], False))
        return shipped[0][-1]

    loss, grad_x, G = device_step(x[0], mem[0], positions[0], loss_target[0], W, gather[-1], late_weights, ship_grads)
    outs = [{}, {}, {}, {}]

    rep_shapes = [w[n].shape for n in REPLICATED] + [(LANES,)]
    zero = jnp.zeros((LANES,), F32)
    *first_parts, rep_parts = exchange(
        "exchange_ev_and_rep_grads", [shards(G, n).astype(BF16) for n in first],
        [_pack([G[n] for n in REPLICATED] + [jnp.broadcast_to(loss, (LANES,))], F32)])
    rep = adamw("adamw_rep", rep_parts, *[_pack([d[n] for n in REPLICATED] + [zero], F32) for d in (w, m, v)])
    for k in range(4):
        outs[k].update(zip(REPLICATED + ['loss'], _unpack(rep[k], rep_shapes)))
    loss = outs[0]['loss'][0]

    srcs, lands = split_wait("exchange_wait", shipped[0], grad_x, False)
    late_parts = [_fill_own_slot(s, l, False) for s, l in zip(srcs, lands)]
    parts = first_parts + late_parts
    two_d = lambda a: a.reshape(-1, a.shape[-1])
    for n, p in zip(first + late, parts):
        res = adamw("adamw_" + n, p.reshape((N_DEV,) + two_d(w[n]).shape), two_d(w[n]), two_d(m[n]), two_d(v[n]))
        for k in range(4):
            outs[k][n] = res[k].reshape(w[n].shape)
    res = adamw("adamw_small", parts[-1], *[_pack_rows([d[n] for n in small]) for d in (w, m, v)])
    for k in range(4):
        outs[k].update(zip(small, _unpack_rows(res[k], small_shapes)))

    return (loss, grad_x[None], *[outs[0][n] for n in WEIGHTS], *[outs[1][n] for n in WEIGHTS],
            *[outs[2][n] for n in WEIGHTS], *[outs[3][n] for n in WEIGHTS])
```

```python
import functools

import jax
import jax.numpy as jnp
from jax import lax
from jax.experimental import pallas as pl
from jax.experimental.pallas import tpu as pltpu

F32, BF16 = jnp.float32, jnp.bfloat16
N_DEV = 8
D = 1024
POOL_DIM = 512
POOL_WINDOWS = (2, 4, 8, 16)
MLA_HEADS = 8
QK_DIM = 96
Q_LORA, KV_LORA = 256, 128
LRU_HEADS, LRU_HEAD_DIM = 4, 256
LRU_C = 8.0
MEM_HEADS, MEM_HEAD_DIM = 4, 256
D_FF = 2816
RMS_EPS = 1e-6
ADAM_LR, ADAM_B1, ADAM_B2, ADAM_EPS, ADAM_WD, ADAM_STEP = 0.001, 0.9, 0.999, 1e-08, 0.01, 10
LANES = 128
POOL_HALO = 16
CONV_HALO = 8
VMEM_LIMIT = 60000 * 1024


def _cp():
    return pltpu.CompilerParams(dimension_semantics=("arbitrary",), vmem_limit_bytes=VMEM_LIMIT)


def _cp2():
    return pltpu.CompilerParams(dimension_semantics=("arbitrary", "arbitrary"), vmem_limit_bytes=VMEM_LIMIT)


def _row(ts, c, col=0):
    return pl.BlockSpec((ts, c), lambda i: (i, col))


def _prev(hr, c, ts, col=0):
    r = ts // hr
    return pl.BlockSpec((hr, c), lambda i: (jnp.maximum(i * r - 1, 0), col))


def _next(hr, c, ts, n, col=0):
    r = ts // hr
    return pl.BlockSpec((hr, c), lambda i: (jnp.minimum((i + 1) * r, n * r - 1), col))


def _const(shape):
    nd = len(shape)
    return pl.BlockSpec(tuple(shape), lambda i: (0,) * nd, pipeline_mode=pl.Buffered(1))


def _acc(shape):
    nd = len(shape)
    return pl.BlockSpec(tuple(shape), lambda i: (0,) * nd)


def _sds(shape, dt):
    return jax.ShapeDtypeStruct(tuple(shape), dt)


def _dot(a, b):
    return jnp.dot(a.astype(BF16), b.astype(BF16), preferred_element_type=F32)


def _dot_nt(a, b):
    return lax.dot_general(a.astype(BF16), b.astype(BF16), (((1,), (1,)), ((), ())), preferred_element_type=F32)


def _dot_tn(a, b):
    return lax.dot_general(a.astype(BF16), b.astype(BF16), (((0,), (0,)), ((), ())), preferred_element_type=F32)


def _rms(x, g):
    rstd = lax.rsqrt(jnp.mean(x * x, axis=-1, keepdims=True) + RMS_EPS)
    return x * rstd * g, rstd


def _rms_bwd(x, g, rstd, dy):
    xn = x * rstd
    dyg = dy * g
    dx = rstd * (dyg - xn * jnp.mean(dyg * xn, axis=-1, keepdims=True))
    return dx, dy * xn


def _rowsum(v):
    return jnp.sum(v, axis=0, keepdims=True)


def _roll(v, s, axis):
    n = v.shape[axis]
    return pltpu.roll(v, s % n, axis)


def _rope(t, c, a, b):
    k = t.shape[1] // LANES
    if k > 1:
        c, a, b = (jnp.tile(v, (1, k)) for v in (c, a, b))
    return t * c + _roll(t, 16, 1) * a + _roll(t, -16, 1) * b


def _rope_bwd(d, c, a, b):
    k = d.shape[1] // LANES
    if k > 1:
        c, a, b = (jnp.tile(v, (1, k)) for v in (c, a, b))
    return d * c + _roll(d * a, -16, 1) + _roll(d * b, 16, 1)


def _gelu(x):
    c = 0.7978845608028654
    t = jnp.tanh(c * (x + 0.044715 * x * x * x))
    return 0.5 * x * (1.0 + t), t


def _gelu_grad(x, t):
    c = 0.7978845608028654
    return 0.5 * (1.0 + t) + 0.5 * x * (1.0 - t * t) * c * (1.0 + 3.0 * 0.044715 * x * x)


def _blockdot(v, w_ref, nblk, width):
    return jnp.concatenate(
        [_dot(v[:, j * width:(j + 1) * width], w_ref[j]) for j in range(nblk)], axis=1)


def _pool_cnt(row0, rows):
    t = row0 + lax.broadcasted_iota(jnp.int32, (rows, POOL_DIM), 0)
    w = jnp.left_shift(2, lax.broadcasted_iota(jnp.int32, (rows, POOL_DIM), 1) // LANES)
    return jnp.minimum(t + 1, w).astype(F32)


def _pool_windows(ext, sign):
    s2 = ext + _roll(ext, sign * 1, 0)
    t = s2[:, LANES:]
    s4 = t + _roll(t, sign * 2, 0)
    t = s4[:, LANES:]
    s8 = t + _roll(t, sign * 4, 0)
    t = s8[:, LANES:]
    s16 = t + _roll(t, sign * 8, 0)
    return jnp.concatenate([s2[:, :LANES], s4[:, :LANES], s8[:, :LANES], s16], axis=1)


def _pooled(uprev, u, row0):
    ts = u.shape[0]
    ext = jnp.concatenate([uprev, u], axis=0)
    sums = _pool_windows(ext, 1)[POOL_HALO:]
    return sums / _pool_cnt(row0, ts) - u


def _expm1(x):
    return jnp.where(jnp.abs(x) < 0.01, x * (1.0 + 0.5 * x * (1.0 + x * (1.0 / 3.0))), jnp.exp(x) - 1.0)


def _softplus(z):
    return jnp.maximum(z, 0.0) + jnp.log1p(jnp.exp(-jnp.abs(z)))


def _tile_rows(s, want):
    while s % want:
        want //= 2
    return want


def even_pre(x, tabs, g, win, pw, pscale, qg, wq, kvg, wk, wv):
    S = x.shape[0]
    ts = _tile_rows(S, 512)

    def body(x_ref, xp_ref, c_ref, a_ref, b_ref, g_ref, win_ref, pw_ref, ps_ref, qg_ref, wq_ref, kvg_ref,
             wk_ref, wv_ref, z_ref, q_ref, k_ref, v_ref, yp_ref):
        i = pl.program_id(0)
        h, _ = _rms(x_ref[...], g_ref[...])
        z = _dot(h, win_ref[...])
        z_ref[...] = z
        hp, _ = _rms(xp_ref[...], g_ref[...])
        uprev = _dot(hp, win_ref[:, :POOL_DIM]) * (i > 0).astype(F32)
        u = z[:, :POOL_DIM]
        pooled = _pooled(uprev, u, i * ts)
        yp_ref[...] = (_blockdot(pooled, pw_ref, 4, LANES) * ps_ref[...]).astype(BF16)
        c, a, b = c_ref[...], a_ref[...], b_ref[...]
        cqn, _ = _rms(z[:, 512:768], qg_ref[...])
        q_ref[...] = (_rope(_dot(cqn, wq_ref[...]), c, a, b) * (ATTN_SCALE * LOG2_E)).astype(BF16)
        ckvn, _ = _rms(z[:, 768:896], kvg_ref[...])
        krr = _rope(z[:, 896:1024], c, a, b)
        k_ref[...] = (_dot(ckvn, wk_ref[...]) + jnp.tile(krr, (1, MLA_HEADS))).astype(BF16)
        lane = lax.broadcasted_iota(jnp.int32, (ts, D), 1) % LANES
        v_ref[...] = jnp.where(lane == ONES_LANE, 1.0, _dot(ckvn, wv_ref[...])).astype(BF16)

    ins = [x, x, *tabs, g, win, pw, pscale, qg, wq, kvg, wk, wv]
    in_specs = [_row(ts, D), _prev(POOL_HALO, D, ts), _row(ts, LANES), _row(ts, LANES), _row(ts, LANES)]
    in_specs += [_const(v.shape) for v in ins[5:]]
    return pl.pallas_call(
        body, name="even_pre", grid=(S // ts,), in_specs=in_specs,
        out_specs=[_row(ts, D)] * 4 + [_row(ts, POOL_DIM)],
        out_shape=[_sds((S, D), F32)] + [_sds((S, D), BF16)] * 3 + [_sds((S, POOL_DIM), BF16)],
        compiler_params=_cp())(*ins)


ATTN_SCALE = QK_DIM ** -0.5
LOG2_E = 1.4426950408889634
LN_2 = 0.6931471805599453
ONES_LANE = 64


def _exp2(x):
    return jnp.exp2(x)


def _pair_loop(lo, hi, step, init, unrolls=(2, 1)):
    carry = init
    for unroll in unrolls:
        groups = (hi - lo) // unroll

        def group(j, c, lo=lo, unroll=unroll):
            for u in range(unroll):
                c = step(lo + unroll * j + u, c)
            return c

        carry = lax.fori_loop(0, groups, group, carry)
        lo = lo + unroll * groups
    return carry


def _as_row(col):
    return jnp.transpose(jnp.broadcast_to(col, (col.shape[0], LANES)))[0:1, :]


def _after(token):
    return ([], []) if token is None else ([token], [pl.BlockSpec(memory_space=pl.ANY)])


def attn_fwd(qp, kp, vp, token=None):
    S = qp.shape[0]
    tq = _tile_rows(S, 512)
    extra, extra_specs = _after(token)

    def body(q_ref, k_ref, v_ref, *rest):
        o_ref, lse_ref = rest[-2:]
        qi = pl.program_id(1)
        q = q_ref[...]

        def block(ki, carry, masked):
            m, acc = carry
            off = pl.multiple_of(ki * tq, tq)
            s = _dot_nt(q, k_ref[pl.ds(off, tq), :])
            if masked:
                row = lax.broadcasted_iota(jnp.int32, (tq, tq), 0)
                col = lax.broadcasted_iota(jnp.int32, (tq, tq), 1)
                s = jnp.where(col <= row, s, -1e30)
            m_new = jnp.maximum(m, jnp.max(s, axis=1, keepdims=True))
            acc = _exp2(m - m_new) * acc + _dot(_exp2(s - m_new), v_ref[pl.ds(off, tq), :])
            return m_new, acc

        init = (jnp.full((tq, 1), -1e30, F32), jnp.zeros((tq, LANES), F32))
        carry = _pair_loop(0, qi, lambda ki, c: block(ki, c, False), init, unrolls=(8, 4, 2, 1))
        m, acc = block(qi, carry, True)
        l = acc[:, ONES_LANE:ONES_LANE + 1]
        o_ref[...] = acc / l
        lse_ref[...] = _as_row(m + jnp.log(l) * LOG2_E)

    blk = pl.BlockSpec((tq, LANES), lambda h, i: (i, h))
    full = pl.BlockSpec((S, LANES), lambda h, i: (0, h))
    return pl.pallas_call(
        body, name="attn_fwd", grid=(MLA_HEADS, S // tq), in_specs=[blk, full, full] + extra_specs,
        out_specs=[blk, pl.BlockSpec((None, None, 1, tq), lambda h, i: (h, i, 0, 0))],
        out_shape=[_sds((S, D), F32), _sds((MLA_HEADS, S // tq, 1, tq), F32)], compiler_params=_cp2())(
            qp, kp, vp, *extra)


def even_post(x, ypool, o, wo_pool, wo_att):
    S = x.shape[0]
    ts = _tile_rows(S, 512)

    def body(x_ref, yp_ref, o_ref, wp_ref, wa_ref, out_ref):
        out_ref[...] = x_ref[...] + _dot(yp_ref[...], wp_ref[...]) + _dot(o_ref[...], wa_ref[...])

    return pl.pallas_call(
        body, name="even_post", grid=(S // ts,),
        in_specs=[_row(ts, D), _row(ts, POOL_DIM), _row(ts, D), _const(wo_pool.shape), _const(wo_att.shape)],
        out_specs=_row(ts, D), out_shape=_sds((S, D), F32), compiler_params=_cp())(x, ypool, o, wo_pool, wo_att)


def mem_kv(mem, g, wkv):
    M = mem.shape[0]

    def body(mem_ref, g_ref, w_ref, mn_ref, k_ref, v_ref):
        mn, _ = _rms(mem_ref[...], g_ref[...])
        mn_ref[...] = mn.astype(BF16)
        k_ref[...] = _dot(mn, w_ref[:, :D]).astype(BF16)
        v_ref[...] = _dot(mn, w_ref[:, D:]).astype(BF16)

    return pl.pallas_call(
        body, name="mem_kv", grid=(1,), in_specs=[_acc(mem.shape), _acc(g.shape), _acc(wkv.shape)],
        out_specs=[_acc((M, D))] * 3, out_shape=[_sds((M, D), BF16)] * 3, compiler_params=_cp())(mem, g, wkv)


def _xattn_heads(hx, wq_ref, k_ref, v_ref):
    q = _dot(hx, wq_ref[...])
    scale = MEM_HEAD_DIM ** -0.5
    ps, os_ = [], []
    for h in range(MEM_HEADS):
        sl = slice(h * MEM_HEAD_DIM, (h + 1) * MEM_HEAD_DIM)
        s = _dot_nt(q[:, sl], k_ref[:, sl]) * scale
        e = jnp.exp(s - jnp.max(s, axis=1, keepdims=True))
        p = e / jnp.sum(e, axis=1, keepdims=True)
        ps.append(p)
        os_.append(_dot(p, v_ref[:, sl]))
    return q, ps, jnp.concatenate(os_, axis=1)


def xattn_fwd(x, g, wq, kmem, vmem, wo):
    S = x.shape[0]
    ts = _tile_rows(S, 512)

    def body(x_ref, g_ref, wq_ref, k_ref, v_ref, wo_ref, out_ref):
        x_ = x_ref[...]
        hx, _ = _rms(x_, g_ref[...])
        _, _, o = _xattn_heads(hx, wq_ref, k_ref, v_ref)
        out_ref[...] = x_ + _dot(o, wo_ref[...])

    ins = [x, g, wq, kmem, vmem, wo]
    return pl.pallas_call(
        body, name="xattn_fwd", grid=(S // ts,), in_specs=[_row(ts, D)] + [_const(v.shape) for v in ins[1:]],
        out_specs=_row(ts, D), out_shape=_sds((S, D), F32), compiler_params=_cp())(*ins)


def xattn_bwd(x, dy, g, wq, kmem, vmem, wo):
    S = x.shape[0]
    M = kmem.shape[0]
    ts = _tile_rows(S, 512)
    scale = MEM_HEAD_DIM ** -0.5

    def body(x_ref, dy_ref, g_ref, wq_ref, k_ref, v_ref, wo_ref,
             dx_ref, o_ref, dq_ref, hx_ref, dg_ref, dk_ref, dv_ref):
        i = pl.program_id(0)

        @pl.when(i == 0)
        def _():
            dg_ref[...] = jnp.zeros_like(dg_ref)
            dk_ref[...] = jnp.zeros_like(dk_ref)
            dv_ref[...] = jnp.zeros_like(dv_ref)

        x_, dy_ = x_ref[...], dy_ref[...]
        hx, rstd = _rms(x_, g_ref[...])
        q, ps, o = _xattn_heads(hx, wq_ref, k_ref, v_ref)
        hx_ref[...] = hx.astype(BF16)
        o_ref[...] = o.astype(BF16)
        do = _dot_nt(dy_, wo_ref[...])
        dqs = []
        for h in range(MEM_HEADS):
            sl = slice(h * MEM_HEAD_DIM, (h + 1) * MEM_HEAD_DIM)
            p, do_h = ps[h], do[:, sl]
            dp = _dot_nt(do_h, v_ref[:, sl])
            ds = p * (dp - jnp.sum(p * dp, axis=1, keepdims=True)) * scale
            dqs.append(_dot(ds, k_ref[:, sl]))
            dk_ref[:, sl] += _dot_tn(ds, q[:, sl])
            dv_ref[:, sl] += _dot_tn(p, do_h)
        dq = jnp.concatenate(dqs, axis=1)
        dq_ref[...] = dq.astype(BF16)
        dxn, dgr = _rms_bwd(x_, g_ref[...], rstd, _dot_nt(dq, wq_ref[...]))
        dx_ref[...] = dy_ + dxn
        dg_ref[...] += _rowsum(dgr)

    ins = [x, dy, g, wq, kmem, vmem, wo]
    return pl.pallas_call(
        body, name="xattn_bwd", grid=(S // ts,),
        in_specs=[_row(ts, D), _row(ts, D)] + [_const(v.shape) for v in ins[2:]],
        out_specs=[_row(ts, D)] * 4 + [_acc((1, D)), _acc((M, D)), _acc((M, D))],
        out_shape=[_sds((S, D), F32)] + [_sds((S, D), BF16)] * 3 + [_sds((1, D), F32), _sds((M, D), F32),
                                                                    _sds((M, D), F32)],
        compiler_params=_cp())(*ins)


def mem_bwd(mem, g, dk, dv, wkv):
    M = mem.shape[0]

    def body(mem_ref, g_ref, dk_ref, dv_ref, w_ref, dkv_ref, dg_ref):
        dkv = jnp.concatenate([dk_ref[...], dv_ref[...]], axis=1)
        dkv_ref[...] = dkv.astype(BF16)
        _, rstd = _rms(mem_ref[...], g_ref[...])
        dg_ref[...] = _rowsum(_dot_nt(dkv, w_ref[...]) * (mem_ref[...] * rstd))

    ins = [mem, g, dk, dv, wkv]
    return pl.pallas_call(
        body, name="mem_bwd", grid=(1,), in_specs=[_acc(v.shape) for v in ins],
        out_specs=[_acc((M, 2 * D)), _acc((1, D))], out_shape=[_sds((M, 2 * D), BF16), _sds((1, D), F32)],
        compiler_params=_cp())(*ins)


FF_CHUNK = 2 * D_FF // N_DEV
FF_HALF = N_DEV // 2


def _layer_of(w, layer):
    return pl.BlockSpec((N_DEV, None) + w.shape[2:], lambda i: (0, layer, 0, 0), pipeline_mode=pl.Buffered(1))


def _ff_chunks(c, ts):
    return pl.BlockSpec((c, ts, FF_CHUNK), lambda i: (0, i, 0))


def ffn_fwd(x, g, wgu, layer, wd):
    S = x.shape[0]
    ts = _tile_rows(S, 256)

    def body(x_ref, g_ref, wgu_ref, wd_ref, out_ref, hf_ref, gu_ref):
        x_ = x_ref[...]
        hf = _rms(x_, g_ref[...])[0].astype(BF16)
        hf_ref[...] = hf
        out = x_
        for j in range(FF_HALF):
            gg, uu = _dot(hf, wgu_ref[j]), _dot(hf, wgu_ref[j + FF_HALF])
            gu_ref[j] = gg.astype(BF16)
            gu_ref[j + FF_HALF] = uu.astype(BF16)
            out = out + _dot(gg * jax.nn.sigmoid(gg) * uu, wd_ref[j])
        out_ref[...] = out

    return pl.pallas_call(
        body, name="ffn_fwd", grid=(S // ts,),
        in_specs=[_row(ts, D), _const(g.shape), _layer_of(wgu, layer), _const(wd.shape)],
        out_specs=[_row(ts, D), _row(ts, D), _ff_chunks(N_DEV, ts)],
        out_shape=[_sds((S, D), F32), _sds((S, D), BF16), _sds((N_DEV, S, FF_CHUNK), BF16)],
        compiler_params=_cp())(x, g, wgu, wd)


def ffn_bwd_a(dy, gu, wd):
    S = dy.shape[0]
    ts = _tile_rows(S, 256)

    def body(dy_ref, gu_ref, wd_ref, act_ref, dgu_ref):
        dy_ = dy_ref[...].astype(BF16)
        for j in range(FF_HALF):
            gg, uu = gu_ref[j].astype(F32), gu_ref[j + FF_HALF].astype(F32)
            sg = jax.nn.sigmoid(gg)
            silu = gg * sg
            act_ref[j] = (silu * uu).astype(BF16)
            dact = _dot_nt(dy_, wd_ref[j])
            dgu_ref[j] = (dact * uu * (sg * (1.0 + gg * (1.0 - sg)))).astype(BF16)
            dgu_ref[j + FF_HALF] = (dact * silu).astype(BF16)

    return pl.pallas_call(
        body, name="ffn_bwd_a", grid=(S // ts,),
        in_specs=[_row(ts, D), _ff_chunks(N_DEV, ts), _const(wd.shape)],
        out_specs=[_ff_chunks(FF_HALF, ts), _ff_chunks(N_DEV, ts)],
        out_shape=[_sds((FF_HALF, S, FF_CHUNK), BF16), _sds((N_DEV, S, FF_CHUNK), BF16)],
        compiler_params=_cp())(dy, gu, wd)


def ffn_bwd_b(x, dy, dgu, g, wgu, layer):
    S = x.shape[0]
    ts = _tile_rows(S, 512)

    def body(x_ref, dy_ref, dgu_ref, g_ref, w_ref, dx_ref, dg_ref):
        @pl.when(pl.program_id(0) == 0)
        def _():
            dg_ref[...] = jnp.zeros_like(dg_ref)

        dh = _dot_nt(dgu_ref[0], w_ref[0])
        for j in range(1, N_DEV):
            dh = dh + _dot_nt(dgu_ref[j], w_ref[j])
        x_ = x_ref[...]
        _, rstd = _rms(x_, g_ref[...])
        dxn, dgr = _rms_bwd(x_, g_ref[...], rstd, dh)
        dx_ref[...] = dy_ref[...] + dxn
        dg_ref[...] += _rowsum(dgr)

    return pl.pallas_call(
        body, name="ffn_bwd_b", grid=(S // ts,),
        in_specs=[_row(ts, D), _row(ts, D), pl.BlockSpec((N_DEV, ts, FF_CHUNK), lambda i: (0, i, 0)),
                  _const(g.shape), _layer_of(wgu, layer)],
        out_specs=[_row(ts, D), _acc((1, D))], out_shape=[_sds((S, D), F32), _sds((1, D), F32)],
        compiler_params=_cp())(x, dy, dgu, g, wgu)


def _conv_fwd(xprev, xbp, cw_ref, cb):
    ext = jnp.concatenate([xprev, xbp], axis=0)
    acc = cb + cw_ref[3:4, :] * xbp
    for k in range(3):
        acc = acc + cw_ref[k:k + 1, :] * _roll(ext, 3 - k, 0)[CONV_HALO:]
    return acc


def _gates(xb, keep, wr_ref, br, wi_ref, bi, lam):
    r = jax.nn.sigmoid(_blockdot(xb, wr_ref, LRU_HEADS, LRU_HEAD_DIM) + br)
    ig = jax.nn.sigmoid(_blockdot(xb, wi_ref, LRU_HEADS, LRU_HEAD_DIM) + bi)
    sp = _softplus(-lam)
    log_a = -LRU_C * r * sp
    a = jnp.exp(log_a)
    mult = jnp.sqrt(jnp.maximum(-_expm1(2.0 * log_a), 0.0))
    return r, ig, sp, a, mult


def odd_pre(x, keep, g, win, cw, cb, wr, br, wi, bi, lam):
    S = x.shape[0]
    ts = _tile_rows(S, 512)

    def body(x_ref, xp_ref, keep_ref, g_ref, win_ref, cw_ref, cb_ref, wr_ref, br_ref, wi_ref, bi_ref, lam_ref,
             z_ref, a_ref, b_ref):
        i = pl.program_id(0)
        h, _ = _rms(x_ref[...], g_ref[...])
        z = _dot(h, win_ref[...])
        z_ref[...] = z
        hp, _ = _rms(xp_ref[...], g_ref[...])
        xprev = _dot(hp, win_ref[:, D:]) * (i > 0).astype(F32)
        xb = _conv_fwd(xprev, z[:, D:], cw_ref, cb_ref[...])
        keep_ = keep_ref[...]
        _, ig, _, a, mult = _gates(xb, keep_, wr_ref, br_ref[...], wi_ref, bi_ref[...], lam_ref[...])
        a_ref[...] = a * keep_
        b_ref[...] = jnp.where(keep_ > 0.0, mult, 1.0) * (ig * xb)

    ins = [x, x, keep, g, win, cw, cb, wr, br, wi, bi, lam]
    return pl.pallas_call(
        body, name="odd_pre", grid=(S // ts,),
        in_specs=[_row(ts, D), _prev(CONV_HALO, D, ts), _row(ts, 1)] + [_const(v.shape) for v in ins[3:]],
        out_specs=[_row(ts, 2 * D), _row(ts, D), _row(ts, D)],
        out_shape=[_sds((S, 2 * D), F32), _sds((S, D), F32), _sds((S, D), F32)], compiler_params=_cp())(*ins)


def lru_scan(a, b, reverse=False):
    S = a.shape[0]
    ts = _tile_rows(S, 512)
    n = S // ts
    groups = ts // 8

    def body(a_ref, an_ref, b_ref, h_ref, carry_ref, ash_ref):
        i = pl.program_id(0)

        @pl.when(i == 0)
        def _():
            carry_ref[...] = jnp.zeros_like(carry_ref)

        rid = lax.broadcasted_iota(jnp.int32, (8, D), 0)
        if reverse:
            ext = jnp.concatenate([a_ref[...], an_ref[...] * (i > 0).astype(F32)], axis=0)
            ash_ref[...] = _roll(ext, -1, 0)[:ts]
        src = ash_ref if reverse else a_ref

        def group(j, carry):
            off = pl.multiple_of((groups - 1 - j if reverse else j) * 8, 8)
            a8, b8 = src[pl.ds(off, 8), :], b_ref[pl.ds(off, 8), :]
            for k in (1, 2, 4):
                inside = (rid < 8 - k) if reverse else (rid >= k)
                sh = -k if reverse else k
                a_sh = jnp.where(inside, _roll(a8, sh, 0), 1.0)
                b_sh = jnp.where(inside, _roll(b8, sh, 0), 0.0)
                b8 = a8 * b_sh + b8
                a8 = a8 * a_sh
            h8 = a8 * carry + b8
            h_ref[pl.ds(off, 8), :] = h8
            return h8[0:1, :] if reverse else h8[7:8, :]

        carry_ref[...] = lax.fori_loop(0, groups, group, carry_ref[...], unroll=4)

    if reverse:
        r = ts // 8
        tile = pl.BlockSpec((ts, D), lambda i: (n - 1 - i, 0))
        halo = pl.BlockSpec((8, D), lambda i: (jnp.minimum((n - i) * r, n * r - 1), 0))
    else:
        tile, halo = _row(ts, D), _prev(8, D, ts)
    return pl.pallas_call(
        body, name="lru_scan_rev" if reverse else "lru_scan", grid=(n,), in_specs=[tile, halo, tile],
        out_specs=tile, out_shape=_sds((S, D), F32),
        scratch_shapes=[pltpu.VMEM((1, D), F32), pltpu.VMEM((ts, D), F32)], compiler_params=_cp())(a, a, b)


def odd_post(x, z, hseq, wout):
    S = x.shape[0]
    ts = _tile_rows(S, 512)

    def body(x_ref, gate_ref, h_ref, w_ref, out_ref):
        gl, _ = _gelu(gate_ref[...])
        out_ref[...] = x_ref[...] + _dot(gl * h_ref[...], w_ref[...])

    return pl.pallas_call(
        body, name="odd_post", grid=(S // ts,),
        in_specs=[_row(ts, D), _row(ts, D), _row(ts, D), _const(wout.shape)],
        out_specs=_row(ts, D), out_shape=_sds((S, D), F32), compiler_params=_cp())(x, z, hseq, wout)


def _accumulate_tn(acc_ref, out_ref, a, b, steps):
    i = pl.program_id(0)

    @pl.when(i == 0)
    def _():
        acc_ref[...] = jnp.zeros_like(acc_ref)

    acc_ref[...] += _dot_tn(a, b)

    @pl.when(i == steps - 1)
    def _():
        out_ref[...] = acc_ref[...].astype(out_ref.dtype)


def odd_post_bwd(dy, z, hseq, wout):
    S = dy.shape[0]
    ts = _tile_rows(S, 512)
    n = S // ts

    def body(dy_ref, gate_ref, h_ref, w_ref, dgate_ref, dh_ref, dw_ref, acc_ref):
        gate, hs, dy_ = gate_ref[...], h_ref[...], dy_ref[...]
        gl, t = _gelu(gate)
        dyy = _dot_nt(dy_, w_ref[...])
        dgate_ref[...] = dyy * hs * _gelu_grad(gate, t)
        dh_ref[...] = dyy * gl
        _accumulate_tn(acc_ref, dw_ref, gl * hs, dy_, n)

    return pl.pallas_call(
        body, name="odd_post_bwd", grid=(n,),
        in_specs=[_row(ts, D), _row(ts, D), _row(ts, D), _const(wout.shape)],
        out_specs=[_row(ts, D), _row(ts, D), _acc((D, D))],
        out_shape=[_sds((S, D), F32), _sds((S, D), F32), _sds((D, D), BF16)],
        scratch_shapes=[pltpu.VMEM((D, D), F32)], compiler_params=_cp())(dy, z, hseq, wout)


def odd_gates_bwd(z, lam_grad, hseq, keep, cw, cb, wr, br, wi, bi, lam):
    S = z.shape[0]
    ts = _tile_rows(S, 512)

    def body(xbp_ref, xbpp_ref, lg_ref, h_ref, hp_ref, keep_ref, cw_ref, cb_ref, wr_ref, br_ref, wi_ref,
             bi_ref, lam_ref, dxb_ref, dcb_ref, dbr_ref, dbi_ref, dlam_ref, dwr_ref, dwi_ref):
        i = pl.program_id(0)

        @pl.when(i == 0)
        def _():
            for ref in (dcb_ref, dbr_ref, dbi_ref, dlam_ref, dwr_ref, dwi_ref):
                ref[...] = jnp.zeros_like(ref)

        first = (i > 0).astype(F32)
        xb = _conv_fwd(xbpp_ref[...] * first, xbp_ref[...], cw_ref, cb_ref[...])
        keep_ = keep_ref[...]
        lam_ = lam_ref[...]
        r, ig, sp, a, mult = _gates(xb, keep_, wr_ref, br_ref[...], wi_ref, bi_ref[...], lam_)
        hs = h_ref[...]
        hprev = _roll(jnp.concatenate([hp_ref[...] * first, hs], axis=0), 1, 0)[CONV_HALO:]
        lg = lg_ref[...]
        da = lg * hprev * keep_
        ixb = ig * xb
        dmult = lg * ixb * keep_
        dixb = lg * jnp.where(keep_ > 0.0, mult, 1.0)
        dlog_a = da * a - dmult * jnp.where(mult > 0.0, a * a / mult, 0.0)
        dr = dlog_a * (-LRU_C * sp)
        dlam_ref[...] += _rowsum(dlog_a * (-LRU_C * r)) * (-jax.nn.sigmoid(-lam_))
        dpr = dr * r * (1.0 - r)
        dpi = dixb * xb * ig * (1.0 - ig)
        dbr_ref[...] += _rowsum(dpr)
        dbi_ref[...] += _rowsum(dpi)
        dxb = dixb * ig
        parts = []
        for h in range(LRU_HEADS):
            sl = slice(h * LRU_HEAD_DIM, (h + 1) * LRU_HEAD_DIM)
            dwr_ref[h] += _dot_tn(xb[:, sl], dpr[:, sl])
            dwi_ref[h] += _dot_tn(xb[:, sl], dpi[:, sl])
            parts.append(_dot_nt(dpr[:, sl], wr_ref[h]) + _dot_nt(dpi[:, sl], wi_ref[h]))
        dxb = dxb + jnp.concatenate(parts, axis=1)
        dxb_ref[...] = dxb
        dcb_ref[...] += _rowsum(dxb)

    ins = [z, z, lam_grad, hseq, hseq, keep, cw, cb, wr, br, wi, bi, lam]
    in_specs = [_row(ts, D, 1), _prev(CONV_HALO, D, ts, 1), _row(ts, D), _row(ts, D), _prev(CONV_HALO, D, ts),
                _row(ts, 1)] + [_const(v.shape) for v in ins[6:]]
    gshape = (LRU_HEADS, LRU_HEAD_DIM, LRU_HEAD_DIM)
    return pl.pallas_call(
        body, name="odd_gates_bwd", grid=(S // ts,), in_specs=in_specs,
        out_specs=[_row(ts, D)] + [_acc((1, D))] * 4 + [_acc(gshape)] * 2,
        out_shape=[_sds((S, D), F32)] + [_sds((1, D), F32)] * 4 + [_sds(gshape, F32)] * 2,
        compiler_params=_cp())(*ins)


def odd_pre_bwd(x, dy, z, dxb, dgate, g, cw, win):
    S = x.shape[0]
    ts = _tile_rows(S, 512)
    n = S // ts

    def body(x_ref, dy_ref, xbp_ref, xbpp_ref, dxb_ref, dxbn_ref, dgate_ref, g_ref, cw_ref, win_ref,
             dx_ref, dcw_ref, dg_ref, dwin_ref, acc_ref):
        i = pl.program_id(0)

        @pl.when(i == 0)
        def _():
            dcw_ref[...] = jnp.zeros_like(dcw_ref)
            dg_ref[...] = jnp.zeros_like(dg_ref)

        dxb = dxb_ref[...]
        extd = jnp.concatenate([dxb, dxbn_ref[...] * (i < n - 1).astype(F32)], axis=0)
        extx = jnp.concatenate([xbpp_ref[...] * (i > 0).astype(F32), xbp_ref[...]], axis=0)
        dxbp = cw_ref[3:4, :] * dxb
        dcw_ref[3:4, :] += _rowsum(dxb * xbp_ref[...])
        for k in range(3):
            dxbp = dxbp + cw_ref[k:k + 1, :] * _roll(extd, -(3 - k), 0)[:ts]
            dcw_ref[k:k + 1, :] += _rowsum(dxb * _roll(extx, 3 - k, 0)[CONV_HALO:])
        dz = jnp.concatenate([dgate_ref[...], dxbp], axis=1).astype(BF16)
        x_ = x_ref[...]
        h, rstd = _rms(x_, g_ref[...])
        dxn, dgr = _rms_bwd(x_, g_ref[...], rstd, _dot_nt(dz, win_ref[...]))
        dx_ref[...] = dy_ref[...] + dxn
        dg_ref[...] += _rowsum(dgr)
        _accumulate_tn(acc_ref, dwin_ref, h, dz, n)

    ins = [x, dy, z, z, dxb, dxb, dgate, g, cw, win]
    in_specs = [_row(ts, D), _row(ts, D), _row(ts, D, 1), _prev(CONV_HALO, D, ts, 1), _row(ts, D),
                _next(CONV_HALO, D, ts, n), _row(ts, D)] + [_const(v.shape) for v in ins[7:]]
    return pl.pallas_call(
        body, name="odd_pre_bwd", grid=(n,), in_specs=in_specs,
        out_specs=[_row(ts, D), _acc((4, D)), _acc((1, D)), _acc((D, 2 * D))],
        out_shape=[_sds((S, D), F32), _sds((4, D), F32), _sds((1, D), F32), _sds((D, 2 * D), BF16)],
        scratch_shapes=[pltpu.VMEM((D, 2 * D), F32)], compiler_params=_cp())(*ins)


def loss_head(x, target, g):
    S = x.shape[0]
    ts = _tile_rows(S, 512)

    def body(x_ref, t_ref, g_ref, dx_ref, dg_ref, loss_ref):
        @pl.when(pl.program_id(0) == 0)
        def _():
            dg_ref[...] = jnp.zeros_like(dg_ref)
            loss_ref[...] = jnp.zeros_like(loss_ref)

        x_ = x_ref[...]
        y, rstd = _rms(x_, g_ref[...])
        err = y - t_ref[...]
        loss_ref[...] += 0.5 * _rowsum(jnp.mean(err * err, axis=1, keepdims=True))
        dxn, dgr = _rms_bwd(x_, g_ref[...], rstd, err * (1.0 / D))
        dx_ref[...] = dxn
        dg_ref[...] += _rowsum(dgr)

    return pl.pallas_call(
        body, name="loss_head", grid=(S // ts,), in_specs=[_row(ts, D), _row(ts, D), _const(g.shape)],
        out_specs=[_row(ts, D), _acc((1, D)), _acc((1, 1))],
        out_shape=[_sds((S, D), F32), _sds((1, D), F32), _sds((1, 1), F32)], compiler_params=_cp())(x, target, g)


def even_post_bwd(dy, o, wo_pool, wo_att):
    S = dy.shape[0]
    ts = _tile_rows(S, 512)

    def body(dy_ref, o_ref, wp_ref, wa_ref, dyp_ref, do_ref, delta_ref):
        dy_ = dy_ref[...]
        dyp_ref[...] = _dot_nt(dy_, wp_ref[...])
        do = _dot_nt(dy_, wa_ref[...])
        do_ref[...] = do.astype(BF16)
        prod = do * o_ref[...]
        for h in range(MLA_HEADS):
            delta_ref[h] = _as_row(jnp.sum(prod[:, h * LANES:(h + 1) * LANES], axis=1, keepdims=True))

    return pl.pallas_call(
        body, name="even_post_bwd", grid=(S // ts,),
        in_specs=[_row(ts, D), _row(ts, D), _const(wo_pool.shape), _const(wo_att.shape)],
        out_specs=[_row(ts, POOL_DIM), _row(ts, D),
                   pl.BlockSpec((MLA_HEADS, None, 1, ts), lambda i: (0, i, 0, 0))],
        out_shape=[_sds((S, POOL_DIM), F32), _sds((S, D), BF16), _sds((MLA_HEADS, S // ts, 1, ts), F32)],
        compiler_params=_cp())(dy, o, wo_pool, wo_att)


def attn_bwd(qp, kp, vp, do, lse_row, delta_row, token=None):
    S = qp.shape[0]
    tk = _tile_rows(S, 512)
    nq = S // tk
    extra, extra_specs = _after(token)

    def body(q_ref, k_ref, v_ref, do_ref, lse_ref, delta_ref, *rest):
        dq_ref, dk_ref, dv_ref = rest[-3:]
        kj = pl.program_id(1)

        @pl.when(kj == 0)
        def _():
            dq_ref[...] = jnp.zeros_like(dq_ref)

        k, v = k_ref[...], v_ref[...]

        def block(qi, carry, masked):
            dk, dv = carry
            off = pl.multiple_of(qi * tk, tk)
            q = q_ref[pl.ds(off, tk), :]
            do_ = do_ref[pl.ds(off, tk), :]
            st = _dot_nt(k, q)
            if masked:
                row = lax.broadcasted_iota(jnp.int32, (tk, tk), 0)
                col = lax.broadcasted_iota(jnp.int32, (tk, tk), 1)
                st = jnp.where(col >= row, st, -1e30)
            pt = _exp2(st - lse_ref[qi])
            dv = dv + _dot(pt, do_)
            dst = (pt * (_dot_nt(v, do_) - delta_ref[qi])).astype(BF16)
            dk = dk + _dot(dst, q)
            dq_ref[pl.ds(off, tk), :] += _dot_tn(dst, k)
            return dk, dv

        zero = jnp.zeros((tk, LANES), F32)
        carry = block(kj, (zero, zero), True)
        dk, dv = _pair_loop(kj + 1, nq, lambda qi, c: block(qi, c, False), carry, unrolls=(4, 2, 1))
        dk_ref[...] = dk * LN_2
        dv_ref[...] = dv

    blk = pl.BlockSpec((tk, LANES), lambda h, j: (j, h))
    full = pl.BlockSpec((S, LANES), lambda h, j: (0, h))
    rowv = pl.BlockSpec((None, nq, 1, tk), lambda h, j: (h, 0, 0, 0))
    return pl.pallas_call(
        body, name="attn_bwd", grid=(MLA_HEADS, nq), in_specs=[full, blk, blk, full, rowv, rowv] + extra_specs,
        out_specs=[full, blk, blk], out_shape=[_sds((S, D), F32)] * 3, compiler_params=_cp2())(
            qp, kp, vp, do, lse_row, delta_row, *extra)


def even_pre_bwd(x, dy, z, dq, dk, dv, dyp, tabs, g, win, pw, pscale, qg, wq, kvg, wk, wv):
    S = x.shape[0]
    ts = _tile_rows(S, 512)
    n = S // ts

    def body(x_ref, dy_ref, z_ref, up_ref, dq_ref, dk_ref, dv_ref, dyp_ref, dypn_ref, c_ref, a_ref, b_ref,
             g_ref, win_ref, pw_ref, ps_ref, qg_ref, wq_ref, kvg_ref, wk_ref, wv_ref,
             dx_ref, dg_ref, dpw_ref, dps_ref, dqg_ref, dwq_ref, dkvg_ref, dwk_ref, dwv_ref, dwin_ref, acc_ref):
        i = pl.program_id(0)

        @pl.when(i == 0)
        def _():
            for ref in (dg_ref, dpw_ref, dps_ref, dqg_ref, dwq_ref, dkvg_ref, dwk_ref, dwv_ref):
                ref[...] = jnp.zeros_like(ref)

        z = z_ref[...]
        c, a, b = c_ref[...], a_ref[...], b_ref[...]
        ps = ps_ref[...]
        u = z[:, :POOL_DIM]
        pooled = _pooled(up_ref[...] * (i > 0).astype(F32), u, i * ts)
        dyp_ = dyp_ref[...]
        dps_ref[...] += _rowsum(dyp_ * _blockdot(pooled, pw_ref, 4, LANES))
        ext = jnp.concatenate([dyp_, dypn_ref[...] * (i < n - 1).astype(F32)], axis=0) * ps
        for gidx in range(4):
            sl = slice(gidx * LANES, (gidx + 1) * LANES)
            dpw_ref[gidx] += _dot_tn(pooled[:, sl], ext[:ts, sl])
        dpooled = jnp.concatenate(
            [_dot_nt(ext[:, gidx * LANES:(gidx + 1) * LANES], pw_ref[gidx]) for gidx in range(4)], axis=1)
        dm = dpooled / _pool_cnt(i * ts, ts + POOL_HALO)
        du = _pool_windows(dm, -1)[:ts] - dpooled[:ts]
        cq = z[:, 512:768]
        cqn, rstd_q = _rms(cq, qg_ref[...])
        dqf = _rope_bwd(dq_ref[...] * ATTN_SCALE, c, a, b)
        dwq_ref[...] += _dot_tn(cqn, dqf)
        dcq, dqg_rows = _rms_bwd(cq, qg_ref[...], rstd_q, _dot_nt(dqf, wq_ref[...]))
        dqg_ref[...] += _rowsum(dqg_rows)
        ckv = z[:, 768:896]
        ckvn, rstd_kv = _rms(ckv, kvg_ref[...])
        dk_, dv_ = dk_ref[...], dv_ref[...]
        dwk_ref[...] += _dot_tn(ckvn, dk_)
        dwv_ref[...] += _dot_tn(ckvn, dv_)
        dckv, dkvg_rows = _rms_bwd(ckv, kvg_ref[...], rstd_kv,
                                   _dot_nt(dk_, wk_ref[...]) + _dot_nt(dv_, wv_ref[...]))
        dkvg_ref[...] += _rowsum(dkvg_rows)
        dkr = dk_[:, :LANES]
        for h in range(1, MLA_HEADS):
            dkr = dkr + dk_[:, h * LANES:(h + 1) * LANES]
        lane = lax.broadcasted_iota(jnp.int32, (ts, LANES), 1)
        dkr = jnp.where((lane >= 64) & (lane < 96), _rope_bwd(dkr, c, a, b), 0.0)
        dz = jnp.concatenate([du, dcq, dckv, dkr], axis=1).astype(BF16)
        x_ = x_ref[...]
        h, rstd = _rms(x_, g_ref[...])
        dxn, dgr = _rms_bwd(x_, g_ref[...], rstd, _dot_nt(dz, win_ref[...]))
        dx_ref[...] = dy_ref[...] + dxn
        dg_ref[...] += _rowsum(dgr)
        _accumulate_tn(acc_ref, dwin_ref, h, dz, n)

    ins = [x, dy, z, z, dq, dk, dv, dyp, dyp, *tabs, g, win, pw, pscale, qg, wq, kvg, wk, wv]
    in_specs = [_row(ts, D), _row(ts, D), _row(ts, D), _prev(POOL_HALO, POOL_DIM, ts), _row(ts, D), _row(ts, D),
                _row(ts, D), _row(ts, POOL_DIM), _next(POOL_HALO, POOL_DIM, ts, n), _row(ts, LANES),
                _row(ts, LANES), _row(ts, LANES)] + [_const(v.shape) for v in ins[12:]]
    acc_shapes = [(1, D), (4, LANES, LANES), (1, POOL_DIM), (1, Q_LORA), (Q_LORA, D), (1, KV_LORA), (KV_LORA, D),
                  (KV_LORA, D)]
    return pl.pallas_call(
        body, name="even_pre_bwd", grid=(n,), in_specs=in_specs,
        out_specs=[_row(ts, D)] + [_acc(s) for s in acc_shapes] + [_acc((D, D))],
        out_shape=[_sds((S, D), F32)] + [_sds(s, F32) for s in acc_shapes] + [_sds((D, D), BF16)],
        scratch_shapes=[pltpu.VMEM((D, D), F32)], compiler_params=_cp())(*ins)


def _pick(n, options):
    for o in options:
        if n % o == 0:
            return o
    return n


def matmul_tn(name, a, b):
    out_dtype = BF16
    S = a.shape[-2]
    ts = _tile_rows(S, 2048)
    steps = S // ts

    def body(a_ref, b_ref, o_ref, acc_ref):
        s = pl.program_id(2)

        @pl.when(s == 0)
        def _():
            acc_ref[...] = jnp.zeros_like(acc_ref)

        acc_ref[...] += _dot_tn(a_ref[...], b_ref[...])

        @pl.when(s == steps - 1)
        def _():
            o_ref[...] = acc_ref[...].astype(o_ref.dtype)

    if a.ndim == 3:
        C, _, K = a.shape
        N = b.shape[1]
        tn = _pick(N, (512, 256, 128))
        grid = (C, N // tn, S // ts)
        in_specs = [pl.BlockSpec((None, ts, K), lambda c, j, s: (c, s, 0)),
                    pl.BlockSpec((ts, tn), lambda c, j, s: (s, j))]
        out_spec, out_shape, tile = pl.BlockSpec((None, K, tn), lambda c, j, s: (c, 0, j)), (C, K, N), (K, tn)
    elif b.ndim == 3:
        C, _, N = b.shape
        K = a.shape[1]
        tk = _pick(K, (512, 256, 128))
        grid = (C, K // tk, S // ts)
        in_specs = [pl.BlockSpec((ts, tk), lambda c, i, s: (s, i)),
                    pl.BlockSpec((None, ts, N), lambda c, i, s: (c, s, 0))]
        out_spec, out_shape, tile = pl.BlockSpec((None, tk, N), lambda c, i, s: (c, i, 0)), (C, K, N), (tk, N)
    else:
        K, N = a.shape[1], b.shape[1]
        tk = _pick(K, (512, 256, 128))
        tn = _pick(N, (512, 256, 128))
        grid = (K // tk, N // tn, S // ts)
        in_specs = [pl.BlockSpec((ts, tk), lambda i, j, s: (s, i)), pl.BlockSpec((ts, tn), lambda i, j, s: (s, j))]
        out_spec, out_shape, tile = pl.BlockSpec((tk, tn), lambda i, j, s: (i, j)), (K, N), (tk, tn)
    return pl.pallas_call(
        body, name=name, grid=grid, in_specs=in_specs, out_specs=out_spec, out_shape=_sds(out_shape, out_dtype),
        scratch_shapes=[pltpu.VMEM(tile, F32)], compiler_params=pltpu.CompilerParams(dimension_semantics=("arbitrary",) * 3, vmem_limit_bytes=VMEM_LIMIT))(
            a, b)


def _my_id():
    return lax.axis_index("x") * 4 + lax.axis_index("y") * 2 + lax.axis_index("c")


def _peer(j):
    x, y, c = lax.axis_index("x"), lax.axis_index("y"), lax.axis_index("c")
    px = 1 - x if j & 4 else x
    py = 1 - y if j & 2 else y
    pc = 1 - c if j & 1 else c
    return (px, py, pc), px * 4 + py * 2 + pc


def all_gather(name, arrays):
    n = len(arrays)

    def body(*refs):
        ins, outs = refs[:n], refs[n:2 * n]
        send_sems, recv_sems, local_sems = refs[2 * n:]
        me = _my_id()
        local = [pltpu.make_async_copy(ins[k], outs[k].at[me], local_sems.at[k]) for k in range(n)]
        for cp in local:
            cp.start()
        sends = []
        for j in range(1, N_DEV):
            peer, _ = _peer(j)
            for k in range(n):
                cp = pltpu.make_async_remote_copy(
                    src_ref=ins[k], dst_ref=outs[k].at[me], send_sem=send_sems.at[k, j - 1],
                    recv_sem=recv_sems.at[k, j - 1], device_id=peer, device_id_type=pl.DeviceIdType.MESH)
                cp.start()
                sends.append(cp)
        for j in range(1, N_DEV):
            peer, pid = _peer(j)
            for k in range(n):
                pltpu.make_async_remote_copy(
                    src_ref=ins[k], dst_ref=outs[k].at[pid], send_sem=send_sems.at[k, j - 1],
                    recv_sem=recv_sems.at[k, j - 1], device_id=peer, device_id_type=pl.DeviceIdType.MESH).wait_recv()
        for cp in sends:
            cp.wait_send()
        for cp in local:
            cp.wait()

    any_spec = pl.BlockSpec(memory_space=pl.ANY)
    return pl.pallas_call(
        body, name=name, in_specs=[any_spec] * n, out_specs=[any_spec] * n,
        out_shape=[_sds((N_DEV,) + a.shape, a.dtype) for a in arrays],
        scratch_shapes=[pltpu.SemaphoreType.DMA((n, N_DEV - 1)), pltpu.SemaphoreType.DMA((n, N_DEV - 1)),
                        pltpu.SemaphoreType.DMA((n,))],
        compiler_params=pltpu.CompilerParams(has_side_effects=True))(*arrays)


def exchange(name, arrays, gathers=()):
    n_ex, n = len(arrays), len(arrays) + len(gathers)

    def body(*refs):
        ins, outs = refs[:n], refs[n:2 * n]
        send_sems, recv_sems, local_sems = refs[2 * n:]
        me = _my_id()

        def mine(k, slot):
            return ins[k].at[slot] if k < n_ex else ins[k]

        local = [pltpu.make_async_copy(mine(k, me), outs[k].at[me], local_sems.at[k]) for k in range(n)]
        for cp in local:
            cp.start()
        sends = []
        for j in range(1, N_DEV):
            peer, pid = _peer(j)
            for k in range(n):
                cp = pltpu.make_async_remote_copy(
                    src_ref=mine(k, pid), dst_ref=outs[k].at[me], send_sem=send_sems.at[k, j - 1],
                    recv_sem=recv_sems.at[k, j - 1], device_id=peer, device_id_type=pl.DeviceIdType.MESH)
                cp.start()
                sends.append(cp)
        for j in range(1, N_DEV):
            peer, pid = _peer(j)
            for k in range(n):
                pltpu.make_async_remote_copy(
                    src_ref=mine(k, me), dst_ref=outs[k].at[pid], send_sem=send_sems.at[k, j - 1],
                    recv_sem=recv_sems.at[k, j - 1], device_id=peer, device_id_type=pl.DeviceIdType.MESH).wait_recv()
        for cp in sends:
            cp.wait_send()
        for cp in local:
            cp.wait()

    any_spec = pl.BlockSpec(memory_space=pl.ANY)
    return pl.pallas_call(
        body, name=name, in_specs=[any_spec] * n, out_specs=[any_spec] * n,
        out_shape=[_sds(a.shape, a.dtype) for a in arrays] + [_sds((N_DEV,) + a.shape, a.dtype) for a in gathers],
        scratch_shapes=[pltpu.SemaphoreType.DMA((n, N_DEV - 1)), pltpu.SemaphoreType.DMA((n, N_DEV - 1)),
                        pltpu.SemaphoreType.DMA((n,))],
        compiler_params=pltpu.CompilerParams(has_side_effects=True))(*arrays, *gathers)


_HBM = pl.BlockSpec(memory_space=pltpu.HBM)
_SEM = pl.BlockSpec(memory_space=pltpu.SEMAPHORE)
_DATAFLOW = pltpu.SideEffectType.DATAFLOW_SIDE_EFFECTING


def _in_hbm(v):
    return pltpu.with_memory_space_constraint(v, pltpu.HBM)


N_PEERS = N_DEV - 1


def _split_copy(k, j, srcs, lands, send_sems, recv_sems, gather, slot):
    peer, pid = _peer(j)
    return pltpu.make_async_remote_copy(
        src_ref=srcs[k] if gather else srcs[k].at[pid], dst_ref=lands[k].at[_my_id() if slot == "mine" else pid],
        send_sem=send_sems[j - 1], recv_sem=recv_sems[j - 1], device_id=peer, device_id_type=pl.DeviceIdType.MESH)


def split_start(name, arrays, gather):
    n = len(arrays)
    lands = [lax.empty((N_DEV,) + a.shape if gather else a.shape, a.dtype) for a in arrays]

    def body(*refs):
        srcs, lnds = refs[:n], refs[n:2 * n]
        sems = refs[4 * n:4 * n + 2 * N_PEERS]
        token = refs[-1]
        for j in range(1, N_DEV):
            for k in range(n):
                _split_copy(k, j, srcs, lnds, sems[:N_PEERS], sems[N_PEERS:], gather, "mine").start()
        token[...] = jnp.zeros_like(token)

    out = pl.pallas_call(
        body, name=name,
        out_shape=(*[pltpu.HBM(a.shape, a.dtype) for a in arrays], *[pltpu.HBM(l.shape, l.dtype) for l in lands],
                   *[pltpu.SemaphoreType.DMA(())] * (2 * N_PEERS), _sds((8, LANES), F32)),
        in_specs=[_HBM] * (2 * n),
        out_specs=(*[_HBM] * (2 * n), *[_SEM] * (2 * N_PEERS), pl.BlockSpec(memory_space=pltpu.VMEM)),
        input_output_aliases={k: k for k in range(2 * n)},
        compiler_params=pltpu.CompilerParams(has_side_effects=_DATAFLOW))(
            *[_in_hbm(a) for a in arrays], *[_in_hbm(l) for l in lands])
    sems = list(out[2 * n:2 * n + 2 * N_PEERS])
    return sems[:N_PEERS], sems[N_PEERS:], list(out[:n]), list(out[n:2 * n]), out[-1]


def split_wait(name, handle, after, gather):
    send_sems, recv_sems, srcs, lands, _ = handle
    n = len(srcs)

    def body(*refs):
        srcs_r, lnds_r = refs[:n], refs[n:2 * n]
        sems = refs[2 * n:2 * n + 2 * N_PEERS]
        for j in range(1, N_DEV):
            for k in range(n):
                cp = _split_copy(k, j, srcs_r, lnds_r, sems[:N_PEERS], sems[N_PEERS:], gather, "peer")
                cp.wait_send()
                cp.wait_recv()

    out = pl.pallas_call(
        body, name=name, out_shape=tuple(pltpu.HBM(a.shape, a.dtype) for a in srcs + lands),
        in_specs=[_HBM] * (2 * n) + [_SEM] * (2 * N_PEERS) + [pl.BlockSpec(memory_space=pl.ANY)],
        out_specs=tuple([_HBM] * (2 * n)), input_output_aliases={k: k for k in range(2 * n)},
        compiler_params=pltpu.CompilerParams(has_side_effects=_DATAFLOW))(
            *srcs, *lands, *send_sems, *recv_sems, after)
    return list(out[:n]), list(out[n:])


def _fill_own_slot(src, land, gather):
    me = _my_id()
    own = src[None] if gather else lax.dynamic_index_in_dim(src, me, 0, keepdims=True)
    return lax.dynamic_update_slice_in_dim(land, own, me, 0)


ADAMW_BLOCK_ELEMS = 128 * 1024


def adamw(name, parts, w, m, v):
    R, C = w.shape
    tr = _pick(R, [t for t in (512, 256, 128, 64, 32, 16, 8) if t * C <= ADAMW_BLOCK_ELEMS])
    c1 = 1.0 - ADAM_B1 ** ADAM_STEP
    c2 = 1.0 - ADAM_B2 ** ADAM_STEP

    def body(p_ref, w_ref, m_ref, v_ref, g_ref, d_ref, nm_ref, nv_ref):
        g = p_ref[0].astype(F32)
        for s in range(1, N_DEV):
            g = g + p_ref[s].astype(F32)
        g_ref[...] = g
        m_ = ADAM_B1 * m_ref[...] + (1.0 - ADAM_B1) * g
        v_ = ADAM_B2 * v_ref[...] + (1.0 - ADAM_B2) * (g * g)
        nm_ref[...] = m_
        nv_ref[...] = v_
        d_ref[...] = -ADAM_LR * ((m_ / c1) / (jnp.sqrt(v_ / c2) + ADAM_EPS) + ADAM_WD * w_ref[...])

    row = pl.BlockSpec((tr, C), lambda i: (i, 0))
    return pl.pallas_call(
        body, name=name, grid=(R // tr,),
        in_specs=[pl.BlockSpec((N_DEV, tr, C), lambda i: (0, i, 0)), row, row, row], out_specs=[row] * 4,
        out_shape=[_sds((R, C), F32)] * 4, compiler_params=_cp())(parts, w, m, v)


WEIGHTS = ['ev_norm', 'ev_w_in', 'ev_pool_w', 'ev_pool_scale', 'ev_q_norm', 'ev_w_q_up', 'ev_kv_norm', 'ev_w_kv_up',
           'ev_w_out', 'od_norm', 'od_w_in', 'od_conv_w', 'od_conv_b', 'od_w_rgate', 'od_b_rgate', 'od_w_igate',
           'od_b_igate', 'od_lambda', 'od_w_out', 'xa_norm_x', 'xa_norm_mem', 'xa_w_q', 'xa_w_kv', 'xa_w_o',
           'ffn_norm', 'ffn_w_gate_up', 'ffn_w_down', 'final_norm']
SHARD_AXIS = {'ev_w_in': 1, 'ev_w_q_up': 2, 'ev_w_kv_up': 2, 'ev_w_out': 1, 'od_norm': 1, 'od_w_in': 2,
              'od_conv_w': 2, 'od_conv_b': 1, 'od_w_rgate': 2, 'od_b_rgate': 1, 'od_w_igate': 2, 'od_b_igate': 1,
              'od_lambda': 1, 'od_w_out': 1, 'xa_w_q': 1, 'xa_w_kv': 2, 'xa_w_o': 1, 'ffn_w_gate_up': 2,
              'ffn_w_down': 1}
SMALL_F32 = ('od_norm', 'od_conv_w', 'od_conv_b', 'od_b_rgate', 'od_b_igate', 'od_lambda')
STACKED = ('ffn_w_gate_up', 'ffn_w_down')
SHARDED = [n for n in WEIGHTS if n in SHARD_AXIS]
REPLICATED = [n for n in WEIGHTS if n not in SHARD_AXIS]
ROW_ALIGN = 512


def _pack(flats, dtype):
    v = jnp.concatenate([f.reshape(-1).astype(dtype) for f in flats])
    pad = (-v.shape[0]) % (ROW_ALIGN * LANES)
    return jnp.pad(v, (0, pad)).reshape(-1, LANES)


def _rows8(n_elems):
    return -(-n_elems // (8 * LANES)) * 8


def _pack_rows(arrays, lead=False):
    out = []
    for a in arrays:
        r = a.reshape((N_DEV, -1, LANES) if lead else (-1, LANES))
        pad = _rows8(r.shape[-2] * LANES) - r.shape[-2]
        out.append(jnp.pad(r, [(0, 0)] * (r.ndim - 2) + [(0, pad), (0, 0)]))
    return jnp.concatenate(out, axis=-2)


def _unpack_rows(buf, shapes, lead=False):
    out, off = [], 0
    for s in shapes:
        n = 1
        for d in s:
            n *= d
        rows = buf[..., off:off + n // LANES, :]
        out.append(rows.reshape(((N_DEV,) if lead else ()) + tuple(s)))
        off += _rows8(n)
    return out


def _unpack(flat, shapes):
    out, off = [], 0
    v = flat.reshape(-1)
    for s in shapes:
        n = 1
        for d in s:
            n *= d
        out.append(v[off:off + n].reshape(s))
        off += n
    return out


def _to_full(stacked, axis):
    v = jnp.moveaxis(stacked, 0, axis)
    s = v.shape
    return v.reshape(s[:axis] + (s[axis] * s[axis + 1],) + s[axis + 2:])


def _to_shards(full, axis):
    s = full.shape
    v = full.reshape(s[:axis] + (N_DEV, s[axis] // N_DEV) + s[axis + 1:])
    return jnp.moveaxis(v, axis, 0)


def _pad_heads(w, nh, dh, lead):
    s = w.shape
    v = w.reshape(s[:-1] + (nh, dh))
    v = jnp.pad(v, [(0, 0)] * (len(s) - 1) + [(0, 0), (lead, LANES - dh - lead)])
    return v.reshape(s[:-1] + (nh * LANES,))


def _unpad_heads(w, nh, dh, lead):
    s = w.shape
    return w.reshape(s[:-1] + (nh, LANES))[..., lead:lead + dh].reshape(s[:-1] + (nh * dh,))


def _rope_tables(positions):
    inv_freq = 10000.0 ** (-jnp.arange(0, 32, 2, dtype=F32) / 32)
    ang = positions.astype(F32)[:, None] * inv_freq
    cos, sin = jnp.cos(ang), jnp.sin(ang)
    S = positions.shape[0]
    one, zero = jnp.ones((S, 64), F32), jnp.zeros((S, 64), F32)
    z16, z32 = jnp.zeros((S, 16), F32), jnp.zeros((S, 32), F32)
    c = jnp.concatenate([one, cos, cos, jnp.ones((S, 32), F32)], axis=1)
    a = jnp.concatenate([zero, z16, sin, z32], axis=1)
    b = jnp.concatenate([zero, -sin, z16, z32], axis=1)
    return c, a, b


def _t(w):
    return jnp.swapaxes(w, -1, -2)


def device_step(x, mem, positions, target, W, fwd_token=None, late_weights=None, ship_grads=None):
    S = x.shape[0]
    G = {}
    tabs = _rope_tables(positions)
    keep = (positions != 0).astype(F32)[:, None]
    row = lambda v: v.reshape(1, -1)

    w_in = W['ev_w_in'][0]
    ev_win = jnp.concatenate([w_in[:, :896], _pad_heads(w_in[:, 896:], 1, 32, 64)], axis=1)
    ev_wq = _pad_heads(W['ev_w_q_up'][0], MLA_HEADS, QK_DIM, 0)
    kvw = W['ev_w_kv_up'][0].reshape(KV_LORA, MLA_HEADS, 128)
    ev_wk = _pad_heads(kvw[:, :, :64].reshape(KV_LORA, 512), MLA_HEADS, 64, 0)
    ev_wv = _pad_heads(kvw[:, :, 64:].reshape(KV_LORA, 512), MLA_HEADS, 64, 0)
    ev_wo_pool = W['ev_w_out'][0][:POOL_DIM]
    ev_wo_att = _t(_pad_heads(_t(W['ev_w_out'][0][POOL_DIM:]), MLA_HEADS, 64, 0))
    pw = W['ev_pool_w'][0].astype(BF16)
    ev_g, ps, qg, kvg = row(W['ev_norm'][0]), row(W['ev_pool_scale'][0]), row(W['ev_q_norm'][0]), row(W['ev_kv_norm'][0])

    z0, qp, kp, vp, ypool = even_pre(x, tabs, ev_g, ev_win, pw, ps, qg, ev_wq, kvg, ev_wk, ev_wv)
    o_att, lse = attn_fwd(qp, kp, vp, fwd_token)
    if late_weights is not None:
        W = {**W, **late_weights(lse)}
    x1 = even_post(x, ypool, o_att, ev_wo_pool, ev_wo_att)

    def xa_ffn_fwd(xin, l):
        mn, km, vm = mem_kv(mem, row(W['xa_norm_mem'][l]), W['xa_w_kv'][l])
        xm = xattn_fwd(xin, row(W['xa_norm_x'][l]), W['xa_w_q'][l], km, vm, W['xa_w_o'][l])
        xo, hf, gu = ffn_fwd(xm, row(W['ffn_norm'][l]), W['ffn_w_gate_up'], l,
                             W['ffn_w_down'][:, l].reshape(FF_HALF, FF_CHUNK, D))
        return xm, xo, (mn, km, vm, hf, gu)

    x2, x3, memkv0 = xa_ffn_fwd(x1, 0)

    od_g, lam = row(W['od_norm'][0]), row(W['od_lambda'][0])
    cw, cb = W['od_conv_w'][0], row(W['od_conv_b'][0])
    wr, wi = W['od_w_rgate'][0], W['od_w_igate'][0]
    br, bi = row(W['od_b_rgate'][0]), row(W['od_b_igate'][0])
    z1, a_t, b_t = odd_pre(x3, keep, od_g, W['od_w_in'][0], cw, cb, wr, br, wi, bi, lam)
    hseq = lru_scan(a_t, b_t)
    x4 = odd_post(x3, z1, hseq, W['od_w_out'][0])
    x5, x6, memkv1 = xa_ffn_fwd(x4, 1)

    dx, G['final_norm'], loss = loss_head(x6, target, row(W['final_norm']))
    G['final_norm'] = G['final_norm'].reshape(D)

    gnx, gnm, gwq, gwkv, gwo, gfn, gwgu, gwd = ([None, None] for _ in range(8))

    def xa_ffn_bwd(dy, xin, xm, memkv, l):
        mn, km, vm, hf, gu = memkv
        fg = row(W['ffn_norm'][l])
        act, dgu = ffn_bwd_a(dy, gu, W['ffn_w_down'][:, l].reshape(FF_HALF, FF_CHUNK, D))
        gwd[l] = matmul_tn("ffn_dwd", act, dy).reshape(N_DEV, D_FF // N_DEV, D)
        gwgu[l] = matmul_tn("ffn_dwgu", hf, dgu)
        dxm, dfg = ffn_bwd_b(xm, dy, dgu, fg, W['ffn_w_gate_up'], l)
        gfn[l] = dfg[0]
        dxin, o, dq, hx, dgx, dk, dv = xattn_bwd(xin, dxm, row(W['xa_norm_x'][l]), W['xa_w_q'][l], km, vm,
                                                  W['xa_w_o'][l])
        gnx[l] = dgx[0]
        gwo[l] = matmul_tn("xa_dwo", o, dxm)
        gwq[l] = matmul_tn("xa_dwq", hx, dq)
        dkv, dgm = mem_bwd(mem, row(W['xa_norm_mem'][l]), dk, dv, W['xa_w_kv'][l])
        gnm[l] = dgm[0]
        gwkv[l] = matmul_tn("xa_dwkv", mn, dkv)
        return dxin

    dx4 = xa_ffn_bwd(dx, x4, x5, memkv1, 1)

    dgate, dhs, g_od_wout = odd_post_bwd(dx4, z1, hseq, W['od_w_out'][0])
    G['od_w_out'] = g_od_wout[None]
    lam_grad = lru_scan(a_t, dhs, reverse=True)
    dxb, dcb, dbr, dbi, dlam, dwr, dwi = odd_gates_bwd(z1, lam_grad, hseq, keep, cw, cb, wr, br, wi, bi, lam)
    dx3, dcw, dg_od, g_od_win = odd_pre_bwd(x3, dx4, z1, dxb, dgate, od_g, cw, W['od_w_in'][0])
    G['od_w_in'] = g_od_win[None]
    G['od_norm'], G['od_conv_w'], G['od_conv_b'] = dg_od, dcw[None], dcb
    G['od_w_rgate'], G['od_b_rgate'], G['od_w_igate'], G['od_b_igate'], G['od_lambda'] = (
        dwr[None], dbr, dwi[None], dbi, dlam)

    dx1 = xa_ffn_bwd(dx3, x1, x2, memkv0, 0)
    G['xa_norm_x'], G['xa_norm_mem'], G['ffn_norm'] = jnp.stack(gnx), jnp.stack(gnm), jnp.stack(gfn)
    G['xa_w_q'], G['xa_w_kv'], G['xa_w_o'] = jnp.stack(gwq), jnp.stack(gwkv), jnp.stack(gwo)
    G['ffn_w_gate_up'], G['ffn_w_down'] = jnp.stack(gwgu, axis=1), jnp.stack(gwd, axis=1)
    bwd_token = ship_grads(G) if ship_grads is not None else None

    dyp, do_att, delta = even_post_bwd(dx1, o_att, ev_wo_pool, ev_wo_att)
    g_wo_pool = matmul_tn("ev_dwo_pool", ypool, dx1)
    g_wo_att = matmul_tn("ev_dwo_att", o_att, dx1)
    G['ev_w_out'] = jnp.concatenate([g_wo_pool, _t(_unpad_heads(_t(g_wo_att), MLA_HEADS, 64, 0))], axis=0)[None]
    dq, dk, dv = attn_bwd(qp, kp, vp, do_att, lse, delta, bwd_token)
    (grad_x, dg_ev, dpw, dps, dqg, dwq, dkvg, dwk, dwv, g_win) = even_pre_bwd(
        x, dx1, z0, dq, dk, dv, dyp, tabs, ev_g, ev_win, pw, ps, qg, ev_wq, kvg, ev_wk, ev_wv)
    G['ev_w_in'] = jnp.concatenate([g_win[:, :896], _unpad_heads(g_win[:, 896:], 1, 32, 64)], axis=1)[None]
    G['ev_norm'], G['ev_pool_w'], G['ev_pool_scale'], G['ev_q_norm'], G['ev_kv_norm'] = (
        dg_ev, dpw[None], dps, dqg, dkvg)
    G['ev_w_q_up'] = _unpad_heads(dwq, MLA_HEADS, QK_DIM, 0)[None]
    gk = _unpad_heads(dwk, MLA_HEADS, 64, 0).reshape(KV_LORA, MLA_HEADS, 64)
    gv = _unpad_heads(dwv, MLA_HEADS, 64, 0).reshape(KV_LORA, MLA_HEADS, 64)
    G['ev_w_kv_up'] = jnp.concatenate([gk, gv], axis=2).reshape(1, KV_LORA, MLA_HEADS * 128)
    return loss[0, 0], grad_x, G


def kernel(x, mem, positions, ev_norm, ev_w_in, ev_pool_w, ev_pool_scale, ev_q_norm, ev_w_q_up, ev_kv_norm, ev_w_kv_up, ev_w_out, od_norm, od_w_in, od_conv_w, od_conv_b, od_w_rgate, od_b_rgate, od_w_igate, od_b_igate, od_lambda, od_w_out, xa_norm_x, xa_norm_mem, xa_w_q, xa_w_kv, xa_w_o, ffn_norm, ffn_w_gate_up, ffn_w_down, final_norm, loss_target, m_ev_norm, m_ev_w_in, m_ev_pool_w, m_ev_pool_scale, m_ev_q_norm, m_ev_w_q_up, m_ev_kv_norm, m_ev_w_kv_up, m_ev_w_out, m_od_norm, m_od_w_in, m_od_conv_w, m_od_conv_b, m_od_w_rgate, m_od_b_rgate, m_od_w_igate, m_od_b_igate, m_od_lambda, m_od_w_out, m_xa_norm_x, m_xa_norm_mem, m_xa_w_q, m_xa_w_kv, m_xa_w_o, m_ffn_norm, m_ffn_w_gate_up, m_ffn_w_down, m_final_norm, v_ev_norm, v_ev_w_in, v_ev_pool_w, v_ev_pool_scale, v_ev_q_norm, v_ev_w_q_up, v_ev_kv_norm, v_ev_w_kv_up, v_ev_w_out, v_od_norm, v_od_w_in, v_od_conv_w, v_od_conv_b, v_od_w_rgate, v_od_b_rgate, v_od_w_igate, v_od_b_igate, v_od_lambda, v_od_w_out, v_xa_norm_x, v_xa_norm_mem, v_xa_w_q, v_xa_w_kv, v_xa_w_o, v_ffn_norm, v_ffn_w_gate_up, v_ffn_w_down, v_final_norm):
    args = dict(locals())
    w = {n: args[n] for n in WEIGHTS}
    m = {n: args['m_' + n] for n in WEIGHTS}
    v = {n: args['v_' + n] for n in WEIGHTS}
    big = [n for n in SHARDED if n not in SMALL_F32]
    small = [n for n in SHARDED if n in SMALL_F32]

    small_shapes = [w[n].shape for n in small]
    first = [n for n in big if n.startswith('ev_')]
    late = [n for n in big if n not in first]

    def full(n, st):
        return st if n in STACKED else _to_full(st, SHARD_AXIS[n])

    W = {n: w[n] for n in REPLICATED}
    W.update((n, full(n, st)) for n, st in zip(first, all_gather("gather_ev_weights", [w[n].astype(BF16) for n in first])))
    gather = split_start("gather_start", [w[n].astype(BF16) for n in late] + [_pack_rows([w[n] for n in small])], True)

    def late_weights(after):
        srcs, lands = split_wait("gather_wait", gather, after, True)
        lands = [_fill_own_slot(s, l, True) for s, l in zip(srcs, lands)]
        out = {n: full(n, st) for n, st in zip(late, lands)}
        out.update((n, _to_full(st, SHARD_AXIS[n])) for n, st in zip(small, _unpack_rows(lands[-1], small_shapes, True)))
        return out

    def shards(G, n):
        return G[n] if n in STACKED else _to_shards(G[n], SHARD_AXIS[n])

    shipped = []

    def ship_grads(G):
        shipped.append(split_start("exchange_start", [shards(G, n).astype(BF16) for n in late] +
                                   [_pack_rows([shards(G, n) for n in small], lead=True)], False))
        return shipped[0][-1]

    loss, grad_x, G = device_step(x[0], mem[0], positions[0], loss_target[0], W, gather[-1], late_weights, ship_grads)
    outs = [{}, {}, {}, {}]

    rep_shapes = [w[n].shape for n in REPLICATED] + [(LANES,)]
    zero = jnp.zeros((LANES,), F32)
    *first_parts, rep_parts = exchange(
        "exchange_ev_and_rep_grads", [shards(G, n).astype(BF16) for n in first],
        [_pack([G[n] for n in REPLICATED] + [jnp.broadcast_to(loss, (LANES,))], F32)])
    rep = adamw("adamw_rep", rep_parts, *[_pack([d[n] for n in REPLICATED] + [zero], F32) for d in (w, m, v)])
    for k in range(4):
        outs[k].update(zip(REPLICATED + ['loss'], _unpack(rep[k], rep_shapes)))
    loss = outs[0]['loss'][0]

    srcs, lands = split_wait("exchange_wait", shipped[0], grad_x, False)
    late_parts = [_fill_own_slot(s, l, False) for s, l in zip(srcs, lands)]
    parts = first_parts + late_parts
    two_d = lambda a: a.reshape(-1, a.shape[-1])
    for n, p in zip(first + late, parts):
        res = adamw("adamw_" + n, p.reshape((N_DEV,) + two_d(w[n]).shape), two_d(w[n]), two_d(m[n]), two_d(v[n]))
        for k in range(4):
            outs[k][n] = res[k].reshape(w[n].shape)
    res = adamw("adamw_small", parts[-1], *[_pack_rows([d[n] for n in small]) for d in (w, m, v)])
    for k in range(4):
        outs[k].update(zip(small, _unpack_rows(res[k], small_shapes)))

    return (loss, grad_x[None], *[outs[0][n] for n in WEIGHTS], *[outs[1][n] for n in WEIGHTS],
            *[outs[2][n] for n in WEIGHTS], *[outs[3][n] for n in WEIGHTS])
```

```python
import functools

import jax
import jax.numpy as jnp
from jax import lax
from jax.experimental import pallas as pl
from jax.experimental.pallas import tpu as pltpu

F32, BF16 = jnp.float32, jnp.bfloat16
N_DEV = 8
D = 1024
POOL_DIM = 512
POOL_WINDOWS = (2, 4, 8, 16)
MLA_HEADS = 8
QK_DIM = 96
Q_LORA, KV_LORA = 256, 128
LRU_HEADS, LRU_HEAD_DIM = 4, 256
LRU_C = 8.0
MEM_HEADS, MEM_HEAD_DIM = 4, 256
D_FF = 2816
RMS_EPS = 1e-6
ADAM_LR, ADAM_B1, ADAM_B2, ADAM_EPS, ADAM_WD, ADAM_STEP = 0.001, 0.9, 0.999, 1e-08, 0.01, 10
LANES = 128
POOL_HALO = 16
CONV_HALO = 8
VMEM_LIMIT = 60000 * 1024


def _cp():
    return pltpu.CompilerParams(dimension_semantics=("arbitrary",), vmem_limit_bytes=VMEM_LIMIT)


def _cp2():
    return pltpu.CompilerParams(dimension_semantics=("arbitrary", "arbitrary"), vmem_limit_bytes=VMEM_LIMIT)


def _row(ts, c, col=0):
    return pl.BlockSpec((ts, c), lambda i: (i, col))


def _prev(hr, c, ts, col=0):
    r = ts // hr
    return pl.BlockSpec((hr, c), lambda i: (jnp.maximum(i * r - 1, 0), col))


def _next(hr, c, ts, n, col=0):
    r = ts // hr
    return pl.BlockSpec((hr, c), lambda i: (jnp.minimum((i + 1) * r, n * r - 1), col))


def _const(shape):
    nd = len(shape)
    return pl.BlockSpec(tuple(shape), lambda i: (0,) * nd, pipeline_mode=pl.Buffered(1))


def _acc(shape):
    nd = len(shape)
    return pl.BlockSpec(tuple(shape), lambda i: (0,) * nd)


def _sds(shape, dt):
    return jax.ShapeDtypeStruct(tuple(shape), dt)


def _dot(a, b):
    return jnp.dot(a.astype(BF16), b.astype(BF16), preferred_element_type=F32)


def _dot_nt(a, b):
    return lax.dot_general(a.astype(BF16), b.astype(BF16), (((1,), (1,)), ((), ())), preferred_element_type=F32)


def _dot_tn(a, b):
    return lax.dot_general(a.astype(BF16), b.astype(BF16), (((0,), (0,)), ((), ())), preferred_element_type=F32)


def _rms(x, g):
    rstd = lax.rsqrt(jnp.mean(x * x, axis=-1, keepdims=True) + RMS_EPS)
    return x * rstd * g, rstd


def _rms_bwd(x, g, rstd, dy):
    xn = x * rstd
    dyg = dy * g
    dx = rstd * (dyg - xn * jnp.mean(dyg * xn, axis=-1, keepdims=True))
    return dx, dy * xn


def _rowsum(v):
    return jnp.sum(v, axis=0, keepdims=True)


def _roll(v, s, axis):
    n = v.shape[axis]
    return pltpu.roll(v, s % n, axis)


def _rope(t, c, a, b):
    k = t.shape[1] // LANES
    if k > 1:
        c, a, b = (jnp.tile(v, (1, k)) for v in (c, a, b))
    return t * c + _roll(t, 16, 1) * a + _roll(t, -16, 1) * b


def _rope_bwd(d, c, a, b):
    k = d.shape[1] // LANES
    if k > 1:
        c, a, b = (jnp.tile(v, (1, k)) for v in (c, a, b))
    return d * c + _roll(d * a, -16, 1) + _roll(d * b, 16, 1)


def _gelu(x):
    c = 0.7978845608028654
    t = jnp.tanh(c * (x + 0.044715 * x * x * x))
    return 0.5 * x * (1.0 + t), t


def _gelu_grad(x, t):
    c = 0.7978845608028654
    return 0.5 * (1.0 + t) + 0.5 * x * (1.0 - t * t) * c * (1.0 + 3.0 * 0.044715 * x * x)


def _blockdot(v, w_ref, nblk, width):
    return jnp.concatenate(
        [_dot(v[:, j * width:(j + 1) * width], w_ref[j]) for j in range(nblk)], axis=1)


def _pool_cnt(row0, rows):
    t = row0 + lax.broadcasted_iota(jnp.int32, (rows, POOL_DIM), 0)
    w = jnp.left_shift(2, lax.broadcasted_iota(jnp.int32, (rows, POOL_DIM), 1) // LANES)
    return jnp.minimum(t + 1, w).astype(F32)


def _pool_windows(ext, sign):
    s2 = ext + _roll(ext, sign * 1, 0)
    t = s2[:, LANES:]
    s4 = t + _roll(t, sign * 2, 0)
    t = s4[:, LANES:]
    s8 = t + _roll(t, sign * 4, 0)
    t = s8[:, LANES:]
    s16 = t + _roll(t, sign * 8, 0)
    return jnp.concatenate([s2[:, :LANES], s4[:, :LANES], s8[:, :LANES], s16], axis=1)


def _pooled(uprev, u, row0):
    ts = u.shape[0]
    ext = jnp.concatenate([uprev, u], axis=0)
    sums = _pool_windows(ext, 1)[POOL_HALO:]
    return sums / _pool_cnt(row0, ts) - u


def _expm1(x):
    return jnp.where(jnp.abs(x) < 0.01, x * (1.0 + 0.5 * x * (1.0 + x * (1.0 / 3.0))), jnp.exp(x) - 1.0)


def _softplus(z):
    return jnp.maximum(z, 0.0) + jnp.log1p(jnp.exp(-jnp.abs(z)))


def _tile_rows(s, want):
    while s % want:
        want //= 2
    return want


def even_pre(x, tabs, g, win, pw, pscale, qg, wq, kvg, wk, wv):
    S = x.shape[0]
    ts = _tile_rows(S, 512)

    def body(x_ref, xp_ref, c_ref, a_ref, b_ref, g_ref, win_ref, pw_ref, ps_ref, qg_ref, wq_ref, kvg_ref,
             wk_ref, wv_ref, z_ref, q_ref, k_ref, v_ref, yp_ref):
        i = pl.program_id(0)
        h, _ = _rms(x_ref[...], g_ref[...])
        z = _dot(h, win_ref[...])
        z_ref[...] = z
        hp, _ = _rms(xp_ref[...], g_ref[...])
        uprev = _dot(hp, win_ref[:, :POOL_DIM]) * (i > 0).astype(F32)
        u = z[:, :POOL_DIM]
        pooled = _pooled(uprev, u, i * ts)
        yp_ref[...] = (_blockdot(pooled, pw_ref, 4, LANES) * ps_ref[...]).astype(BF16)
        c, a, b = c_ref[...], a_ref[...], b_ref[...]
        cqn, _ = _rms(z[:, 512:768], qg_ref[...])
        q_ref[...] = (_rope(_dot(cqn, wq_ref[...]), c, a, b) * (ATTN_SCALE * LOG2_E)).astype(BF16)
        ckvn, _ = _rms(z[:, 768:896], kvg_ref[...])
        krr = _rope(z[:, 896:1024], c, a, b)
        k_ref[...] = (_dot(ckvn, wk_ref[...]) + jnp.tile(krr, (1, MLA_HEADS))).astype(BF16)
        lane = lax.broadcasted_iota(jnp.int32, (ts, D), 1) % LANES
        v_ref[...] = jnp.where(lane == ONES_LANE, 1.0, _dot(ckvn, wv_ref[...])).astype(BF16)

    ins = [x, x, *tabs, g, win, pw, pscale, qg, wq, kvg, wk, wv]
    in_specs = [_row(ts, D), _prev(POOL_HALO, D, ts), _row(ts, LANES), _row(ts, LANES), _row(ts, LANES)]
    in_specs += [_const(v.shape) for v in ins[5:]]
    return pl.pallas_call(
        body, name="even_pre", grid=(S // ts,), in_specs=in_specs,
        out_specs=[_row(ts, D)] * 4 + [_row(ts, POOL_DIM)],
        out_shape=[_sds((S, D), F32)] + [_sds((S, D), BF16)] * 3 + [_sds((S, POOL_DIM), BF16)],
        compiler_params=_cp())(*ins)


ATTN_SCALE = QK_DIM ** -0.5
LOG2_E = 1.4426950408889634
LN_2 = 0.6931471805599453
ONES_LANE = 64


def _exp2(x):
    return jnp.exp2(x)


def _pair_loop(lo, hi, step, init, unrolls=(2, 1)):
    carry = init
    for unroll in unrolls:
        groups = (hi - lo) // unroll

        def group(j, c, lo=lo, unroll=unroll):
            for u in range(unroll):
                c = step(lo + unroll * j + u, c)
            return c

        carry = lax.fori_loop(0, groups, group, carry)
        lo = lo + unroll * groups
    return carry


def _as_row(col):
    return jnp.transpose(jnp.broadcast_to(col, (col.shape[0], LANES)))[0:1, :]


def _after(token):
    return ([], []) if token is None else ([token], [pl.BlockSpec(memory_space=pl.ANY)])


def attn_fwd(qp, kp, vp, token=None):
    S = qp.shape[0]
    tq = _tile_rows(S, 512)
    extra, extra_specs = _after(token)

    def body(q_ref, k_ref, v_ref, *rest):
        o_ref, lse_ref = rest[-2:]
        qi = pl.program_id(1)
        q = q_ref[...]

        def block(ki, carry, masked):
            m, acc = carry
            off = pl.multiple_of(ki * tq, tq)
            s = _dot_nt(q, k_ref[pl.ds(off, tq), :])
            if masked:
                row = lax.broadcasted_iota(jnp.int32, (tq, tq), 0)
                col = lax.broadcasted_iota(jnp.int32, (tq, tq), 1)
                s = jnp.where(col <= row, s, -1e30)
            m_new = jnp.maximum(m, jnp.max(s, axis=1, keepdims=True))
            acc = _exp2(m - m_new) * acc + _dot(_exp2(s - m_new), v_ref[pl.ds(off, tq), :])
            return m_new, acc

        init = (jnp.full((tq, 1), -1e30, F32), jnp.zeros((tq, LANES), F32))
        carry = _pair_loop(0, qi, lambda ki, c: block(ki, c, False), init, unrolls=(8, 4, 2, 1))
        m, acc = block(qi, carry, True)
        l = acc[:, ONES_LANE:ONES_LANE + 1]
        o_ref[...] = acc / l
        lse_ref[...] = _as_row(m + jnp.log(l) * LOG2_E)

    blk = pl.BlockSpec((tq, LANES), lambda h, i: (i, h))
    full = pl.BlockSpec((S, LANES), lambda h, i: (0, h))
    return pl.pallas_call(
        body, name="attn_fwd", grid=(MLA_HEADS, S // tq), in_specs=[blk, full, full] + extra_specs,
        out_specs=[blk, pl.BlockSpec((None, None, 1, tq), lambda h, i: (h, i, 0, 0))],
        out_shape=[_sds((S, D), F32), _sds((MLA_HEADS, S // tq, 1, tq), F32)], compiler_params=_cp2())(
            qp, kp, vp, *extra)


def even_post(x, ypool, o, wo_pool, wo_att):
    S = x.shape[0]
    ts = _tile_rows(S, 512)

    def body(x_ref, yp_ref, o_ref, wp_ref, wa_ref, out_ref):
        out_ref[...] = x_ref[...] + _dot(yp_ref[...], wp_ref[...]) + _dot(o_ref[...], wa_ref[...])

    return pl.pallas_call(
        body, name="even_post", grid=(S // ts,),
        in_specs=[_row(ts, D), _row(ts, POOL_DIM), _row(ts, D), _const(wo_pool.shape), _const(wo_att.shape)],
        out_specs=_row(ts, D), out_shape=_sds((S, D), F32), compiler_params=_cp())(x, ypool, o, wo_pool, wo_att)


def mem_kv(mem, g, wkv):
    M = mem.shape[0]

    def body(mem_ref, g_ref, w_ref, mn_ref, k_ref, v_ref):
        mn, _ = _rms(mem_ref[...], g_ref[...])
        mn_ref[...] = mn.astype(BF16)
        k_ref[...] = _dot(mn, w_ref[:, :D]).astype(BF16)
        v_ref[...] = _dot(mn, w_ref[:, D:]).astype(BF16)

    return pl.pallas_call(
        body, name="mem_kv", grid=(1,), in_specs=[_acc(mem.shape), _acc(g.shape), _acc(wkv.shape)],
        out_specs=[_acc((M, D))] * 3, out_shape=[_sds((M, D), BF16)] * 3, compiler_params=_cp())(mem, g, wkv)


def _xattn_heads(hx, wq_ref, k_ref, v_ref):
    q = _dot(hx, wq_ref[...])
    scale = MEM_HEAD_DIM ** -0.5
    ps, os_ = [], []
    for h in range(MEM_HEADS):
        sl = slice(h * MEM_HEAD_DIM, (h + 1) * MEM_HEAD_DIM)
        s = _dot_nt(q[:, sl], k_ref[:, sl]) * scale
        e = jnp.exp(s - jnp.max(s, axis=1, keepdims=True))
        p = e / jnp.sum(e, axis=1, keepdims=True)
        ps.append(p)
        os_.append(_dot(p, v_ref[:, sl]))
    return q, ps, jnp.concatenate(os_, axis=1)


def xattn_fwd(x, g, wq, kmem, vmem, wo):
    S = x.shape[0]
    ts = _tile_rows(S, 512)

    def body(x_ref, g_ref, wq_ref, k_ref, v_ref, wo_ref, out_ref):
        x_ = x_ref[...]
        hx, _ = _rms(x_, g_ref[...])
        _, _, o = _xattn_heads(hx, wq_ref, k_ref, v_ref)
        out_ref[...] = x_ + _dot(o, wo_ref[...])

    ins = [x, g, wq, kmem, vmem, wo]
    return pl.pallas_call(
        body, name="xattn_fwd", grid=(S // ts,), in_specs=[_row(ts, D)] + [_const(v.shape) for v in ins[1:]],
        out_specs=_row(ts, D), out_shape=_sds((S, D), F32), compiler_params=_cp())(*ins)


def xattn_bwd(x, dy, g, wq, kmem, vmem, wo):
    S = x.shape[0]
    M = kmem.shape[0]
    ts = _tile_rows(S, 512)
    scale = MEM_HEAD_DIM ** -0.5

    def body(x_ref, dy_ref, g_ref, wq_ref, k_ref, v_ref, wo_ref,
             dx_ref, o_ref, dq_ref, hx_ref, dg_ref, dk_ref, dv_ref):
        i = pl.program_id(0)

        @pl.when(i == 0)
        def _():
            dg_ref[...] = jnp.zeros_like(dg_ref)
            dk_ref[...] = jnp.zeros_like(dk_ref)
            dv_ref[...] = jnp.zeros_like(dv_ref)

        x_, dy_ = x_ref[...], dy_ref[...]
        hx, rstd = _rms(x_, g_ref[...])
        q, ps, o = _xattn_heads(hx, wq_ref, k_ref, v_ref)
        hx_ref[...] = hx.astype(BF16)
        o_ref[...] = o.astype(BF16)
        do = _dot_nt(dy_, wo_ref[...])
        dqs = []
        for h in range(MEM_HEADS):
            sl = slice(h * MEM_HEAD_DIM, (h + 1) * MEM_HEAD_DIM)
            p, do_h = ps[h], do[:, sl]
            dp = _dot_nt(do_h, v_ref[:, sl])
            ds = p * (dp - jnp.sum(p * dp, axis=1, keepdims=True)) * scale
            dqs.append(_dot(ds, k_ref[:, sl]))
            dk_ref[:, sl] += _dot_tn(ds, q[:, sl])
            dv_ref[:, sl] += _dot_tn(p, do_h)
        dq = jnp.concatenate(dqs, axis=1).astype(BF16)
        dq_ref[...] = dq
        dxn, dgr = _rms_bwd(x_, g_ref[...], rstd, _dot_nt(dq, wq_ref[...]))
        dx_ref[...] = dy_ + dxn
        dg_ref[...] += _rowsum(dgr)

    ins = [x, dy, g, wq, kmem, vmem, wo]
    return pl.pallas_call(
        body, name="xattn_bwd", grid=(S // ts,),
        in_specs=[_row(ts, D), _row(ts, D)] + [_const(v.shape) for v in ins[2:]],
        out_specs=[_row(ts, D)] * 4 + [_acc((1, D)), _acc((M, D)), _acc((M, D))],
        out_shape=[_sds((S, D), F32)] + [_sds((S, D), BF16)] * 3 + [_sds((1, D), F32), _sds((M, D), F32),
                                                                    _sds((M, D), F32)],
        compiler_params=_cp())(*ins)


def mem_bwd(mem, g, dk, dv, wkv):
    M = mem.shape[0]

    def body(mem_ref, g_ref, dk_ref, dv_ref, w_ref, dkv_ref, dg_ref):
        dkv = jnp.concatenate([dk_ref[...], dv_ref[...]], axis=1)
        dkv_ref[...] = dkv.astype(BF16)
        _, rstd = _rms(mem_ref[...], g_ref[...])
        dg_ref[...] = _rowsum(_dot_nt(dkv, w_ref[...]) * (mem_ref[...] * rstd))

    ins = [mem, g, dk, dv, wkv]
    return pl.pallas_call(
        body, name="mem_bwd", grid=(1,), in_specs=[_acc(v.shape) for v in ins],
        out_specs=[_acc((M, 2 * D)), _acc((1, D))], out_shape=[_sds((M, 2 * D), BF16), _sds((1, D), F32)],
        compiler_params=_cp())(*ins)


FF_CHUNK = 2 * D_FF // N_DEV
FF_HALF = N_DEV // 2


def _layer_of(w, layer):
    return pl.BlockSpec((N_DEV, None) + w.shape[2:], lambda i: (0, layer, 0, 0), pipeline_mode=pl.Buffered(1))


def _ff_chunks(c, ts):
    return pl.BlockSpec((c, ts, FF_CHUNK), lambda i: (0, i, 0))


def ffn_fwd(x, g, wgu, layer, wd):
    S = x.shape[0]
    ts = _tile_rows(S, 256)

    def body(x_ref, g_ref, wgu_ref, wd_ref, out_ref, hf_ref, gu_ref):
        x_ = x_ref[...]
        hf = _rms(x_, g_ref[...])[0].astype(BF16)
        hf_ref[...] = hf
        out = x_
        for j in range(FF_HALF):
            gg, uu = _dot(hf, wgu_ref[j]), _dot(hf, wgu_ref[j + FF_HALF])
            gu_ref[j] = gg.astype(BF16)
            gu_ref[j + FF_HALF] = uu.astype(BF16)
            out = out + _dot(gg * jax.nn.sigmoid(gg) * uu, wd_ref[j])
        out_ref[...] = out

    return pl.pallas_call(
        body, name="ffn_fwd", grid=(S // ts,),
        in_specs=[_row(ts, D), _const(g.shape), _layer_of(wgu, layer), _const(wd.shape)],
        out_specs=[_row(ts, D), _row(ts, D), _ff_chunks(N_DEV, ts)],
        out_shape=[_sds((S, D), F32), _sds((S, D), BF16), _sds((N_DEV, S, FF_CHUNK), BF16)],
        compiler_params=_cp())(x, g, wgu, wd)


def ffn_bwd_a(dy, gu, wd):
    S = dy.shape[0]
    ts = _tile_rows(S, 256)

    def body(dy_ref, gu_ref, wd_ref, act_ref, dgu_ref):
        dy_ = dy_ref[...].astype(BF16)
        for j in range(FF_HALF):
            gg, uu = gu_ref[j].astype(F32), gu_ref[j + FF_HALF].astype(F32)
            sg = jax.nn.sigmoid(gg)
            silu = gg * sg
            act_ref[j] = (silu * uu).astype(BF16)
            dact = _dot_nt(dy_, wd_ref[j])
            dgu_ref[j] = (dact * uu * (sg * (1.0 + gg * (1.0 - sg)))).astype(BF16)
            dgu_ref[j + FF_HALF] = (dact * silu).astype(BF16)

    return pl.pallas_call(
        body, name="ffn_bwd_a", grid=(S // ts,),
        in_specs=[_row(ts, D), _ff_chunks(N_DEV, ts), _const(wd.shape)],
        out_specs=[_ff_chunks(FF_HALF, ts), _ff_chunks(N_DEV, ts)],
        out_shape=[_sds((FF_HALF, S, FF_CHUNK), BF16), _sds((N_DEV, S, FF_CHUNK), BF16)],
        compiler_params=_cp())(dy, gu, wd)


def ffn_bwd_b(x, dy, dgu, g, wgu, layer):
    S = x.shape[0]
    ts = _tile_rows(S, 512)

    def body(x_ref, dy_ref, dgu_ref, g_ref, w_ref, dx_ref, dg_ref):
        @pl.when(pl.program_id(0) == 0)
        def _():
            dg_ref[...] = jnp.zeros_like(dg_ref)

        dh = _dot_nt(dgu_ref[0], w_ref[0])
        for j in range(1, N_DEV):
            dh = dh + _dot_nt(dgu_ref[j], w_ref[j])
        x_ = x_ref[...]
        _, rstd = _rms(x_, g_ref[...])
        dxn, dgr = _rms_bwd(x_, g_ref[...], rstd, dh)
        dx_ref[...] = dy_ref[...] + dxn
        dg_ref[...] += _rowsum(dgr)

    return pl.pallas_call(
        body, name="ffn_bwd_b", grid=(S // ts,),
        in_specs=[_row(ts, D), _row(ts, D), pl.BlockSpec((N_DEV, ts, FF_CHUNK), lambda i: (0, i, 0)),
                  _const(g.shape), _layer_of(wgu, layer)],
        out_specs=[_row(ts, D), _acc((1, D))], out_shape=[_sds((S, D), F32), _sds((1, D), F32)],
        compiler_params=_cp())(x, dy, dgu, g, wgu)


def _conv_fwd(xprev, xbp, cw_ref, cb):
    ext = jnp.concatenate([xprev, xbp], axis=0)
    acc = cb + cw_ref[3:4, :] * xbp
    for k in range(3):
        acc = acc + cw_ref[k:k + 1, :] * _roll(ext, 3 - k, 0)[CONV_HALO:]
    return acc


def _gates(xb, keep, wr_ref, br, wi_ref, bi, lam):
    r = jax.nn.sigmoid(_blockdot(xb, wr_ref, LRU_HEADS, LRU_HEAD_DIM) + br)
    ig = jax.nn.sigmoid(_blockdot(xb, wi_ref, LRU_HEADS, LRU_HEAD_DIM) + bi)
    sp = _softplus(-lam)
    log_a = -LRU_C * r * sp
    a = jnp.exp(log_a)
    mult = jnp.sqrt(jnp.maximum(-_expm1(2.0 * log_a), 0.0))
    return r, ig, sp, a, mult


def odd_pre(x, keep, g, win, cw, cb, wr, br, wi, bi, lam):
    S = x.shape[0]
    ts = _tile_rows(S, 512)

    def body(x_ref, xp_ref, keep_ref, g_ref, win_ref, cw_ref, cb_ref, wr_ref, br_ref, wi_ref, bi_ref, lam_ref,
             z_ref, a_ref, b_ref):
        i = pl.program_id(0)
        h, _ = _rms(x_ref[...], g_ref[...])
        z = _dot(h, win_ref[...])
        z_ref[...] = z
        hp, _ = _rms(xp_ref[...], g_ref[...])
        xprev = _dot(hp, win_ref[:, D:]) * (i > 0).astype(F32)
        xb = _conv_fwd(xprev, z[:, D:], cw_ref, cb_ref[...])
        keep_ = keep_ref[...]
        _, ig, _, a, mult = _gates(xb, keep_, wr_ref, br_ref[...], wi_ref, bi_ref[...], lam_ref[...])
        a_ref[...] = a * keep_
        b_ref[...] = jnp.where(keep_ > 0.0, mult, 1.0) * (ig * xb)

    ins = [x, x, keep, g, win, cw, cb, wr, br, wi, bi, lam]
    return pl.pallas_call(
        body, name="odd_pre", grid=(S // ts,),
        in_specs=[_row(ts, D), _prev(CONV_HALO, D, ts), _row(ts, 1)] + [_const(v.shape) for v in ins[3:]],
        out_specs=[_row(ts, 2 * D), _row(ts, D), _row(ts, D)],
        out_shape=[_sds((S, 2 * D), F32), _sds((S, D), F32), _sds((S, D), F32)], compiler_params=_cp())(*ins)


def lru_scan(a, b, reverse=False):
    S = a.shape[0]
    ts = _tile_rows(S, 512)
    n = S // ts
    groups = ts // 8

    def body(a_ref, an_ref, b_ref, h_ref, carry_ref, ash_ref):
        i = pl.program_id(0)

        @pl.when(i == 0)
        def _():
            carry_ref[...] = jnp.zeros_like(carry_ref)

        rid = lax.broadcasted_iota(jnp.int32, (8, D), 0)
        if reverse:
            ext = jnp.concatenate([a_ref[...], an_ref[...] * (i > 0).astype(F32)], axis=0)
            ash_ref[...] = _roll(ext, -1, 0)[:ts]
        src = ash_ref if reverse else a_ref

        def group(j, carry):
            off = pl.multiple_of((groups - 1 - j if reverse else j) * 8, 8)
            a8, b8 = src[pl.ds(off, 8), :], b_ref[pl.ds(off, 8), :]
            for k in (1, 2, 4):
                inside = (rid < 8 - k) if reverse else (rid >= k)
                sh = -k if reverse else k
                a_sh = jnp.where(inside, _roll(a8, sh, 0), 1.0)
                b_sh = jnp.where(inside, _roll(b8, sh, 0), 0.0)
                b8 = a8 * b_sh + b8
                a8 = a8 * a_sh
            h8 = a8 * carry + b8
            h_ref[pl.ds(off, 8), :] = h8
            return h8[0:1, :] if reverse else h8[7:8, :]

        carry_ref[...] = lax.fori_loop(0, groups, group, carry_ref[...], unroll=4)

    if reverse:
        r = ts // 8
        tile = pl.BlockSpec((ts, D), lambda i: (n - 1 - i, 0))
        halo = pl.BlockSpec((8, D), lambda i: (jnp.minimum((n - i) * r, n * r - 1), 0))
    else:
        tile, halo = _row(ts, D), _prev(8, D, ts)
    return pl.pallas_call(
        body, name="lru_scan_rev" if reverse else "lru_scan", grid=(n,), in_specs=[tile, halo, tile],
        out_specs=tile, out_shape=_sds((S, D), F32),
        scratch_shapes=[pltpu.VMEM((1, D), F32), pltpu.VMEM((ts, D), F32)], compiler_params=_cp())(a, a, b)


def odd_post(x, z, hseq, wout):
    S = x.shape[0]
    ts = _tile_rows(S, 512)

    def body(x_ref, gate_ref, h_ref, w_ref, out_ref):
        gl, _ = _gelu(gate_ref[...])
        out_ref[...] = x_ref[...] + _dot(gl * h_ref[...], w_ref[...])

    return pl.pallas_call(
        body, name="odd_post", grid=(S // ts,),
        in_specs=[_row(ts, D), _row(ts, D), _row(ts, D), _const(wout.shape)],
        out_specs=_row(ts, D), out_shape=_sds((S, D), F32), compiler_params=_cp())(x, z, hseq, wout)


def _accumulate_tn(acc_ref, out_ref, a, b, steps):
    i = pl.program_id(0)

    @pl.when(i == 0)
    def _():
        acc_ref[...] = jnp.zeros_like(acc_ref)

    acc_ref[...] += _dot_tn(a, b)

    @pl.when(i == steps - 1)
    def _():
        out_ref[...] = acc_ref[...].astype(out_ref.dtype)


def odd_post_bwd(dy, z, hseq, wout):
    S = dy.shape[0]
    ts = _tile_rows(S, 512)
    n = S // ts

    def body(dy_ref, gate_ref, h_ref, w_ref, dgate_ref, dh_ref, dw_ref, acc_ref):
        gate, hs, dy_ = gate_ref[...], h_ref[...], dy_ref[...]
        gl, t = _gelu(gate)
        dyy = _dot_nt(dy_, w_ref[...])
        dgate_ref[...] = dyy * hs * _gelu_grad(gate, t)
        dh_ref[...] = dyy * gl
        _accumulate_tn(acc_ref, dw_ref, gl * hs, dy_, n)

    return pl.pallas_call(
        body, name="odd_post_bwd", grid=(n,),
        in_specs=[_row(ts, D), _row(ts, D), _row(ts, D), _const(wout.shape)],
        out_specs=[_row(ts, D), _row(ts, D), _acc((D, D))],
        out_shape=[_sds((S, D), F32), _sds((S, D), F32), _sds((D, D), BF16)],
        scratch_shapes=[pltpu.VMEM((D, D), F32)], compiler_params=_cp())(dy, z, hseq, wout)


def odd_gates_bwd(z, lam_grad, hseq, keep, cw, cb, wr, br, wi, bi, lam):
    S = z.shape[0]
    ts = _tile_rows(S, 512)

    def body(xbp_ref, xbpp_ref, lg_ref, h_ref, hp_ref, keep_ref, cw_ref, cb_ref, wr_ref, br_ref, wi_ref,
             bi_ref, lam_ref, dxb_ref, dcb_ref, dbr_ref, dbi_ref, dlam_ref, dwr_ref, dwi_ref):
        i = pl.program_id(0)

        @pl.when(i == 0)
        def _():
            for ref in (dcb_ref, dbr_ref, dbi_ref, dlam_ref, dwr_ref, dwi_ref):
                ref[...] = jnp.zeros_like(ref)

        first = (i > 0).astype(F32)
        xb = _conv_fwd(xbpp_ref[...] * first, xbp_ref[...], cw_ref, cb_ref[...])
        keep_ = keep_ref[...]
        lam_ = lam_ref[...]
        r, ig, sp, a, mult = _gates(xb, keep_, wr_ref, br_ref[...], wi_ref, bi_ref[...], lam_)
        hs = h_ref[...]
        hprev = _roll(jnp.concatenate([hp_ref[...] * first, hs], axis=0), 1, 0)[CONV_HALO:]
        lg = lg_ref[...]
        da = lg * hprev * keep_
        ixb = ig * xb
        dmult = lg * ixb * keep_
        dixb = lg * jnp.where(keep_ > 0.0, mult, 1.0)
        dlog_a = da * a - dmult * jnp.where(mult > 0.0, a * a / mult, 0.0)
        dr = dlog_a * (-LRU_C * sp)
        dlam_ref[...] += _rowsum(dlog_a * (-LRU_C * r)) * (-jax.nn.sigmoid(-lam_))
        dpr = dr * r * (1.0 - r)
        dpi = dixb * xb * ig * (1.0 - ig)
        dbr_ref[...] += _rowsum(dpr)
        dbi_ref[...] += _rowsum(dpi)
        dxb = dixb * ig
        parts = []
        for h in range(LRU_HEADS):
            sl = slice(h * LRU_HEAD_DIM, (h + 1) * LRU_HEAD_DIM)
            dwr_ref[h] += _dot_tn(xb[:, sl], dpr[:, sl])
            dwi_ref[h] += _dot_tn(xb[:, sl], dpi[:, sl])
            parts.append(_dot_nt(dpr[:, sl], wr_ref[h]) + _dot_nt(dpi[:, sl], wi_ref[h]))
        dxb = dxb + jnp.concatenate(parts, axis=1)
        dxb_ref[...] = dxb
        dcb_ref[...] += _rowsum(dxb)

    ins = [z, z, lam_grad, hseq, hseq, keep, cw, cb, wr, br, wi, bi, lam]
    in_specs = [_row(ts, D, 1), _prev(CONV_HALO, D, ts, 1), _row(ts, D), _row(ts, D), _prev(CONV_HALO, D, ts),
                _row(ts, 1)] + [_const(v.shape) for v in ins[6:]]
    gshape = (LRU_HEADS, LRU_HEAD_DIM, LRU_HEAD_DIM)
    return pl.pallas_call(
        body, name="odd_gates_bwd", grid=(S // ts,), in_specs=in_specs,
        out_specs=[_row(ts, D)] + [_acc((1, D))] * 4 + [_acc(gshape)] * 2,
        out_shape=[_sds((S, D), F32)] + [_sds((1, D), F32)] * 4 + [_sds(gshape, F32)] * 2,
        compiler_params=_cp())(*ins)


def odd_pre_bwd(x, dy, z, dxb, dgate, g, cw, win):
    S = x.shape[0]
    ts = _tile_rows(S, 512)
    n = S // ts

    def body(x_ref, dy_ref, xbp_ref, xbpp_ref, dxb_ref, dxbn_ref, dgate_ref, g_ref, cw_ref, win_ref,
             dx_ref, dcw_ref, dg_ref, dwin_ref, acc_ref):
        i = pl.program_id(0)

        @pl.when(i == 0)
        def _():
            dcw_ref[...] = jnp.zeros_like(dcw_ref)
            dg_ref[...] = jnp.zeros_like(dg_ref)

        dxb = dxb_ref[...]
        extd = jnp.concatenate([dxb, dxbn_ref[...] * (i < n - 1).astype(F32)], axis=0)
        extx = jnp.concatenate([xbpp_ref[...] * (i > 0).astype(F32), xbp_ref[...]], axis=0)
        dxbp = cw_ref[3:4, :] * dxb
        dcw_ref[3:4, :] += _rowsum(dxb * xbp_ref[...])
        for k in range(3):
            dxbp = dxbp + cw_ref[k:k + 1, :] * _roll(extd, -(3 - k), 0)[:ts]
            dcw_ref[k:k + 1, :] += _rowsum(dxb * _roll(extx, 3 - k, 0)[CONV_HALO:])
        dz = jnp.concatenate([dgate_ref[...], dxbp], axis=1).astype(BF16)
        x_ = x_ref[...]
        h, rstd = _rms(x_, g_ref[...])
        dxn, dgr = _rms_bwd(x_, g_ref[...], rstd, _dot_nt(dz, win_ref[...]))
        dx_ref[...] = dy_ref[...] + dxn
        dg_ref[...] += _rowsum(dgr)
        _accumulate_tn(acc_ref, dwin_ref, h, dz, n)

    ins = [x, dy, z, z, dxb, dxb, dgate, g, cw, win]
    in_specs = [_row(ts, D), _row(ts, D), _row(ts, D, 1), _prev(CONV_HALO, D, ts, 1), _row(ts, D),
                _next(CONV_HALO, D, ts, n), _row(ts, D)] + [_const(v.shape) for v in ins[7:]]
    return pl.pallas_call(
        body, name="odd_pre_bwd", grid=(n,), in_specs=in_specs,
        out_specs=[_row(ts, D), _acc((4, D)), _acc((1, D)), _acc((D, 2 * D))],
        out_shape=[_sds((S, D), F32), _sds((4, D), F32), _sds((1, D), F32), _sds((D, 2 * D), BF16)],
        scratch_shapes=[pltpu.VMEM((D, 2 * D), F32)], compiler_params=_cp())(*ins)


def loss_head(x, target, g):
    S = x.shape[0]
    ts = _tile_rows(S, 512)

    def body(x_ref, t_ref, g_ref, dx_ref, dg_ref, loss_ref):
        @pl.when(pl.program_id(0) == 0)
        def _():
            dg_ref[...] = jnp.zeros_like(dg_ref)
            loss_ref[...] = jnp.zeros_like(loss_ref)

        x_ = x_ref[...]
        y, rstd = _rms(x_, g_ref[...])
        err = y - t_ref[...]
        loss_ref[...] += 0.5 * _rowsum(jnp.mean(err * err, axis=1, keepdims=True))
        dxn, dgr = _rms_bwd(x_, g_ref[...], rstd, err * (1.0 / D))
        dx_ref[...] = dxn
        dg_ref[...] += _rowsum(dgr)

    return pl.pallas_call(
        body, name="loss_head", grid=(S // ts,), in_specs=[_row(ts, D), _row(ts, D), _const(g.shape)],
        out_specs=[_row(ts, D), _acc((1, D)), _acc((1, 1))],
        out_shape=[_sds((S, D), F32), _sds((1, D), F32), _sds((1, 1), F32)], compiler_params=_cp())(x, target, g)


def even_post_bwd(dy, ypool, o, wo_pool, wo_att):
    S = dy.shape[0]
    ts = _tile_rows(S, 512)
    n = S // ts

    def body(dy_ref, yp_ref, o_ref, wp_ref, wa_ref, dyp_ref, do_ref, delta_ref, dwp_ref, dwa_ref, accp_ref,
             acca_ref):
        dy_, o_ = dy_ref[...], o_ref[...]
        dyp_ref[...] = _dot_nt(dy_, wp_ref[...])
        do = _dot_nt(dy_, wa_ref[...])
        do_ref[...] = do.astype(BF16)
        prod = do * o_
        for h in range(MLA_HEADS):
            delta_ref[h] = _as_row(jnp.sum(prod[:, h * LANES:(h + 1) * LANES], axis=1, keepdims=True))
        _accumulate_tn(accp_ref, dwp_ref, yp_ref[...], dy_, n)
        _accumulate_tn(acca_ref, dwa_ref, o_, dy_, n)

    return pl.pallas_call(
        body, name="even_post_bwd", grid=(n,),
        in_specs=[_row(ts, D), _row(ts, POOL_DIM), _row(ts, D), _const(wo_pool.shape), _const(wo_att.shape)],
        out_specs=[_row(ts, POOL_DIM), _row(ts, D),
                   pl.BlockSpec((MLA_HEADS, None, 1, ts), lambda i: (0, i, 0, 0)), _acc((POOL_DIM, D)),
                   _acc((D, D))],
        out_shape=[_sds((S, POOL_DIM), F32), _sds((S, D), BF16), _sds((MLA_HEADS, n, 1, ts), F32),
                   _sds((POOL_DIM, D), BF16), _sds((D, D), BF16)],
        scratch_shapes=[pltpu.VMEM((POOL_DIM, D), F32), pltpu.VMEM((D, D), F32)],
        compiler_params=_cp())(dy, ypool, o, wo_pool, wo_att)


def attn_bwd(qp, kp, vp, do, lse_row, delta_row, token=None):
    S = qp.shape[0]
    tk = _tile_rows(S, 512)
    nq = S // tk
    extra, extra_specs = _after(token)

    def body(q_ref, k_ref, v_ref, do_ref, lse_ref, delta_ref, *rest):
        dq_ref, dk_ref, dv_ref = rest[-3:]
        kj = pl.program_id(1)

        @pl.when(kj == 0)
        def _():
            dq_ref[...] = jnp.zeros_like(dq_ref)

        k, v = k_ref[...], v_ref[...]

        def block(qi, carry, masked):
            dk, dv = carry
            off = pl.multiple_of(qi * tk, tk)
            q = q_ref[pl.ds(off, tk), :]
            do_ = do_ref[pl.ds(off, tk), :]
            st = _dot_nt(k, q)
            if masked:
                row = lax.broadcasted_iota(jnp.int32, (tk, tk), 0)
                col = lax.broadcasted_iota(jnp.int32, (tk, tk), 1)
                st = jnp.where(col >= row, st, -1e30)
            pt = _exp2(st - lse_ref[qi])
            dv = dv + _dot(pt, do_)
            dst = (pt * (_dot_nt(v, do_) - delta_ref[qi])).astype(BF16)
            dk = dk + _dot(dst, q)
            dq_ref[pl.ds(off, tk), :] += _dot_tn(dst, k)
            return dk, dv

        zero = jnp.zeros((tk, LANES), F32)
        carry = block(kj, (zero, zero), True)
        dk, dv = _pair_loop(kj + 1, nq, lambda qi, c: block(qi, c, False), carry, unrolls=(4, 2, 1))
        dk_ref[...] = dk * LN_2
        dv_ref[...] = dv

    blk = pl.BlockSpec((tk, LANES), lambda h, j: (j, h))
    full = pl.BlockSpec((S, LANES), lambda h, j: (0, h))
    rowv = pl.BlockSpec((None, nq, 1, tk), lambda h, j: (h, 0, 0, 0))
    return pl.pallas_call(
        body, name="attn_bwd", grid=(MLA_HEADS, nq), in_specs=[full, blk, blk, full, rowv, rowv] + extra_specs,
        out_specs=[full, blk, blk], out_shape=[_sds((S, D), F32)] * 3, compiler_params=_cp2())(
            qp, kp, vp, do, lse_row, delta_row, *extra)


def even_pre_bwd(x, dy, z, dq, dk, dv, dyp, tabs, g, win, pw, pscale, qg, wq, kvg, wk, wv):
    S = x.shape[0]
    ts = _tile_rows(S, 512)
    n = S // ts

    def body(x_ref, dy_ref, z_ref, up_ref, dq_ref, dk_ref, dv_ref, dyp_ref, dypn_ref, c_ref, a_ref, b_ref,
             g_ref, win_ref, pw_ref, ps_ref, qg_ref, wq_ref, kvg_ref, wk_ref, wv_ref,
             dx_ref, dg_ref, dpw_ref, dps_ref, dqg_ref, dwq_ref, dkvg_ref, dwk_ref, dwv_ref, dwin_ref, acc_ref):
        i = pl.program_id(0)

        @pl.when(i == 0)
        def _():
            for ref in (dg_ref, dpw_ref, dps_ref, dqg_ref, dwq_ref, dkvg_ref, dwk_ref, dwv_ref):
                ref[...] = jnp.zeros_like(ref)

        z = z_ref[...]
        c, a, b = c_ref[...], a_ref[...], b_ref[...]
        ps = ps_ref[...]
        u = z[:, :POOL_DIM]
        pooled = _pooled(up_ref[...] * (i > 0).astype(F32), u, i * ts)
        dyp_ = dyp_ref[...]
        dps_ref[...] += _rowsum(dyp_ * _blockdot(pooled, pw_ref, 4, LANES))
        ext = jnp.concatenate([dyp_, dypn_ref[...] * (i < n - 1).astype(F32)], axis=0) * ps
        for gidx in range(4):
            sl = slice(gidx * LANES, (gidx + 1) * LANES)
            dpw_ref[gidx] += _dot_tn(pooled[:, sl], ext[:ts, sl])
        dpooled = jnp.concatenate(
            [_dot_nt(ext[:, gidx * LANES:(gidx + 1) * LANES], pw_ref[gidx]) for gidx in range(4)], axis=1)
        dm = dpooled / _pool_cnt(i * ts, ts + POOL_HALO)
        du = _pool_windows(dm, -1)[:ts] - dpooled[:ts]
        cq = z[:, 512:768]
        cqn, rstd_q = _rms(cq, qg_ref[...])
        dqf = _rope_bwd(dq_ref[...] * ATTN_SCALE, c, a, b)
        dwq_ref[...] += _dot_tn(cqn, dqf)
        dcq, dqg_rows = _rms_bwd(cq, qg_ref[...], rstd_q, _dot_nt(dqf, wq_ref[...]))
        dqg_ref[...] += _rowsum(dqg_rows)
        ckv = z[:, 768:896]
        ckvn, rstd_kv = _rms(ckv, kvg_ref[...])
        dk_, dv_ = dk_ref[...], dv_ref[...]
        dwk_ref[...] += _dot_tn(ckvn, dk_)
        dwv_ref[...] += _dot_tn(ckvn, dv_)
        dckv, dkvg_rows = _rms_bwd(ckv, kvg_ref[...], rstd_kv,
                                   _dot_nt(dk_, wk_ref[...]) + _dot_nt(dv_, wv_ref[...]))
        dkvg_ref[...] += _rowsum(dkvg_rows)
        dkr = dk_[:, :LANES]
        for h in range(1, MLA_HEADS):
            dkr = dkr + dk_[:, h * LANES:(h + 1) * LANES]
        lane = lax.broadcasted_iota(jnp.int32, (ts, LANES), 1)
        dkr = jnp.where((lane >= 64) & (lane < 96), _rope_bwd(dkr, c, a, b), 0.0)
        dz = jnp.concatenate([du, dcq, dckv, dkr], axis=1).astype(BF16)
        x_ = x_ref[...]
        h, rstd = _rms(x_, g_ref[...])
        dxn, dgr = _rms_bwd(x_, g_ref[...], rstd, _dot_nt(dz, win_ref[...]))
        dx_ref[...] = dy_ref[...] + dxn
        dg_ref[...] += _rowsum(dgr)
        _accumulate_tn(acc_ref, dwin_ref, h, dz, n)

    ins = [x, dy, z, z, dq, dk, dv, dyp, dyp, *tabs, g, win, pw, pscale, qg, wq, kvg, wk, wv]
    in_specs = [_row(ts, D), _row(ts, D), _row(ts, D), _prev(POOL_HALO, POOL_DIM, ts), _row(ts, D), _row(ts, D),
                _row(ts, D), _row(ts, POOL_DIM), _next(POOL_HALO, POOL_DIM, ts, n), _row(ts, LANES),
                _row(ts, LANES), _row(ts, LANES)] + [_const(v.shape) for v in ins[12:]]
    acc_shapes = [(1, D), (4, LANES, LANES), (1, POOL_DIM), (1, Q_LORA), (Q_LORA, D), (1, KV_LORA), (KV_LORA, D),
                  (KV_LORA, D)]
    return pl.pallas_call(
        body, name="even_pre_bwd", grid=(n,), in_specs=in_specs,
        out_specs=[_row(ts, D)] + [_acc(s) for s in acc_shapes] + [_acc((D, D))],
        out_shape=[_sds((S, D), F32)] + [_sds(s, F32) for s in acc_shapes] + [_sds((D, D), BF16)],
        scratch_shapes=[pltpu.VMEM((D, D), F32)], compiler_params=_cp())(*ins)


def _pick(n, options):
    for o in options:
        if n % o == 0:
            return o
    return n


def matmul_tn(name, a, b):
    out_dtype = BF16
    S = a.shape[-2]
    ts = _tile_rows(S, 2048)
    steps = S // ts

    def body(a_ref, b_ref, o_ref, acc_ref):
        s = pl.program_id(2)

        @pl.when(s == 0)
        def _():
            acc_ref[...] = jnp.zeros_like(acc_ref)

        acc_ref[...] += _dot_tn(a_ref[...], b_ref[...])

        @pl.when(s == steps - 1)
        def _():
            o_ref[...] = acc_ref[...].astype(o_ref.dtype)

    if a.ndim == 3:
        C, _, K = a.shape
        N = b.shape[1]
        tn = _pick(N, (512, 256, 128))
        grid = (C, N // tn, S // ts)
        in_specs = [pl.BlockSpec((None, ts, K), lambda c, j, s: (c, s, 0)),
                    pl.BlockSpec((ts, tn), lambda c, j, s: (s, j))]
        out_spec, out_shape, tile = pl.BlockSpec((None, K, tn), lambda c, j, s: (c, 0, j)), (C, K, N), (K, tn)
    elif b.ndim == 3:
        C, _, N = b.shape
        K = a.shape[1]
        tk = _pick(K, (1024, 512, 256, 128))
        grid = (C, K // tk, S // ts)
        in_specs = [pl.BlockSpec((ts, tk), lambda c, i, s: (s, i)),
                    pl.BlockSpec((None, ts, N), lambda c, i, s: (c, s, 0))]
        out_spec, out_shape, tile = pl.BlockSpec((None, tk, N), lambda c, i, s: (c, i, 0)), (C, K, N), (tk, N)
    else:
        K, N = a.shape[1], b.shape[1]
        tk = _pick(K, (1024, 512, 256, 128))
        tn = _pick(N, (512, 256, 128))
        grid = (K // tk, N // tn, S // ts)
        in_specs = [pl.BlockSpec((ts, tk), lambda i, j, s: (s, i)), pl.BlockSpec((ts, tn), lambda i, j, s: (s, j))]
        out_spec, out_shape, tile = pl.BlockSpec((tk, tn), lambda i, j, s: (i, j)), (K, N), (tk, tn)
    return pl.pallas_call(
        body, name=name, grid=grid, in_specs=in_specs, out_specs=out_spec, out_shape=_sds(out_shape, out_dtype),
        scratch_shapes=[pltpu.VMEM(tile, F32)], compiler_params=pltpu.CompilerParams(dimension_semantics=("arbitrary",) * 3, vmem_limit_bytes=VMEM_LIMIT))(
            a, b)


def _my_id():
    return lax.axis_index("x") * 4 + lax.axis_index("y") * 2 + lax.axis_index("c")


def _peer(j):
    x, y, c = lax.axis_index("x"), lax.axis_index("y"), lax.axis_index("c")
    px = 1 - x if j & 4 else x
    py = 1 - y if j & 2 else y
    pc = 1 - c if j & 1 else c
    return (px, py, pc), px * 4 + py * 2 + pc


def all_gather(name, arrays):
    n = len(arrays)

    def body(*refs):
        ins, outs = refs[:n], refs[n:2 * n]
        send_sems, recv_sems, local_sems = refs[2 * n:]
        me = _my_id()
        local = [pltpu.make_async_copy(ins[k], outs[k].at[me], local_sems.at[k]) for k in range(n)]
        for cp in local:
            cp.start()
        sends = []
        for j in range(1, N_DEV):
            peer, _ = _peer(j)
            for k in range(n):
                cp = pltpu.make_async_remote_copy(
                    src_ref=ins[k], dst_ref=outs[k].at[me], send_sem=send_sems.at[k, j - 1],
                    recv_sem=recv_sems.at[k, j - 1], device_id=peer, device_id_type=pl.DeviceIdType.MESH)
                cp.start()
                sends.append(cp)
        for j in range(1, N_DEV):
            peer, pid = _peer(j)
            for k in range(n):
                pltpu.make_async_remote_copy(
                    src_ref=ins[k], dst_ref=outs[k].at[pid], send_sem=send_sems.at[k, j - 1],
                    recv_sem=recv_sems.at[k, j - 1], device_id=peer, device_id_type=pl.DeviceIdType.MESH).wait_recv()
        for cp in sends:
            cp.wait_send()
        for cp in local:
            cp.wait()

    any_spec = pl.BlockSpec(memory_space=pl.ANY)
    return pl.pallas_call(
        body, name=name, in_specs=[any_spec] * n, out_specs=[any_spec] * n,
        out_shape=[_sds((N_DEV,) + a.shape, a.dtype) for a in arrays],
        scratch_shapes=[pltpu.SemaphoreType.DMA((n, N_DEV - 1)), pltpu.SemaphoreType.DMA((n, N_DEV - 1)),
                        pltpu.SemaphoreType.DMA((n,))],
        compiler_params=pltpu.CompilerParams(has_side_effects=True))(*arrays)


def exchange(name, arrays, gathers=()):
    n_ex, n = len(arrays), len(arrays) + len(gathers)

    def body(*refs):
        ins, outs = refs[:n], refs[n:2 * n]
        send_sems, recv_sems, local_sems = refs[2 * n:]
        me = _my_id()

        def mine(k, slot):
            return ins[k].at[slot] if k < n_ex else ins[k]

        local = [pltpu.make_async_copy(mine(k, me), outs[k].at[me], local_sems.at[k]) for k in range(n)]
        for cp in local:
            cp.start()
        sends = []
        for j in range(1, N_DEV):
            peer, pid = _peer(j)
            for k in range(n):
                cp = pltpu.make_async_remote_copy(
                    src_ref=mine(k, pid), dst_ref=outs[k].at[me], send_sem=send_sems.at[k, j - 1],
                    recv_sem=recv_sems.at[k, j - 1], device_id=peer, device_id_type=pl.DeviceIdType.MESH)
                cp.start()
                sends.append(cp)
        for j in range(1, N_DEV):
            peer, pid = _peer(j)
            for k in range(n):
                pltpu.make_async_remote_copy(
                    src_ref=mine(k, me), dst_ref=outs[k].at[pid], send_sem=send_sems.at[k, j - 1],
                    recv_sem=recv_sems.at[k, j - 1], device_id=peer, device_id_type=pl.DeviceIdType.MESH).wait_recv()
        for cp in sends:
            cp.wait_send()
        for cp in local:
            cp.wait()

    any_spec = pl.BlockSpec(memory_space=pl.ANY)
    return pl.pallas_call(
        body, name=name, in_specs=[any_spec] * n, out_specs=[any_spec] * n,
        out_shape=[_sds(a.shape, a.dtype) for a in arrays] + [_sds((N_DEV,) + a.shape, a.dtype) for a in gathers],
        scratch_shapes=[pltpu.SemaphoreType.DMA((n, N_DEV - 1)), pltpu.SemaphoreType.DMA((n, N_DEV - 1)),
                        pltpu.SemaphoreType.DMA((n,))],
        compiler_params=pltpu.CompilerParams(has_side_effects=True))(*arrays, *gathers)


_HBM = pl.BlockSpec(memory_space=pltpu.HBM)
_SEM = pl.BlockSpec(memory_space=pltpu.SEMAPHORE)
_DATAFLOW = pltpu.SideEffectType.DATAFLOW_SIDE_EFFECTING


def _in_hbm(v):
    return pltpu.with_memory_space_constraint(v, pltpu.HBM)


N_PEERS = N_DEV - 1


def _split_copy(k, j, srcs, lands, send_sems, recv_sems, gather, slot):
    peer, pid = _peer(j)
    return pltpu.make_async_remote_copy(
        src_ref=srcs[k] if gather else srcs[k].at[pid], dst_ref=lands[k].at[_my_id() if slot == "mine" else pid],
        send_sem=send_sems[j - 1], recv_sem=recv_sems[j - 1], device_id=peer, device_id_type=pl.DeviceIdType.MESH)


def split_start(name, arrays, gather):
    n = len(arrays)
    lands = [lax.empty((N_DEV,) + a.shape if gather else a.shape, a.dtype) for a in arrays]

    def body(*refs):
        srcs, lnds = refs[:n], refs[n:2 * n]
        sems = refs[4 * n:4 * n + 2 * N_PEERS]
        token = refs[-1]
        for j in range(1, N_DEV):
            for k in range(n):
                _split_copy(k, j, srcs, lnds, sems[:N_PEERS], sems[N_PEERS:], gather, "mine").start()
        token[...] = jnp.zeros_like(token)

    out = pl.pallas_call(
        body, name=name,
        out_shape=(*[pltpu.HBM(a.shape, a.dtype) for a in arrays], *[pltpu.HBM(l.shape, l.dtype) for l in lands],
                   *[pltpu.SemaphoreType.DMA(())] * (2 * N_PEERS), _sds((8, LANES), F32)),
        in_specs=[_HBM] * (2 * n),
        out_specs=(*[_HBM] * (2 * n), *[_SEM] * (2 * N_PEERS), pl.BlockSpec(memory_space=pltpu.VMEM)),
        input_output_aliases={k: k for k in range(2 * n)},
        compiler_params=pltpu.CompilerParams(has_side_effects=_DATAFLOW))(
            *[_in_hbm(a) for a in arrays], *[_in_hbm(l) for l in lands])
    sems = list(out[2 * n:2 * n + 2 * N_PEERS])
    return sems[:N_PEERS], sems[N_PEERS:], list(out[:n]), list(out[n:2 * n]), out[-1]


def split_wait(name, handle, after, gather):
    send_sems, recv_sems, srcs, lands, _ = handle
    n = len(srcs)

    def body(*refs):
        srcs_r, lnds_r = refs[:n], refs[n:2 * n]
        sems = refs[2 * n:2 * n + 2 * N_PEERS]
        for j in range(1, N_DEV):
            for k in range(n):
                cp = _split_copy(k, j, srcs_r, lnds_r, sems[:N_PEERS], sems[N_PEERS:], gather, "peer")
                cp.wait_send()
                cp.wait_recv()

    out = pl.pallas_call(
        body, name=name, out_shape=tuple(pltpu.HBM(a.shape, a.dtype) for a in srcs + lands),
        in_specs=[_HBM] * (2 * n) + [_SEM] * (2 * N_PEERS) + [pl.BlockSpec(memory_space=pl.ANY)],
        out_specs=tuple([_HBM] * (2 * n)), input_output_aliases={k: k for k in range(2 * n)},
        compiler_params=pltpu.CompilerParams(has_side_effects=_DATAFLOW))(
            *srcs, *lands, *send_sems, *recv_sems, after)
    return list(out[:n]), list(out[n:])


def _fill_own_slot(src, land, gather):
    me = _my_id()
    own = src[None] if gather else lax.dynamic_index_in_dim(src, me, 0, keepdims=True)
    return lax.dynamic_update_slice_in_dim(land, own, me, 0)


ADAMW_BLOCK_ELEMS = 128 * 1024


def adamw(name, parts, w, m, v):
    R, C = w.shape
    tr = _pick(R, [t for t in (512, 256, 128, 64, 32, 16, 8) if t * C <= ADAMW_BLOCK_ELEMS])
    c1 = 1.0 - ADAM_B1 ** ADAM_STEP
    c2 = 1.0 - ADAM_B2 ** ADAM_STEP

    def body(p_ref, w_ref, m_ref, v_ref, g_ref, d_ref, nm_ref, nv_ref):
        g = p_ref[0].astype(F32)
        for s in range(1, N_DEV):
            g = g + p_ref[s].astype(F32)
        g_ref[...] = g
        m_ = ADAM_B1 * m_ref[...] + (1.0 - ADAM_B1) * g
        v_ = ADAM_B2 * v_ref[...] + (1.0 - ADAM_B2) * (g * g)
        nm_ref[...] = m_
        nv_ref[...] = v_
        d_ref[...] = -ADAM_LR * ((m_ / c1) / (jnp.sqrt(v_ / c2) + ADAM_EPS) + ADAM_WD * w_ref[...])

    row = pl.BlockSpec((tr, C), lambda i: (i, 0))
    return pl.pallas_call(
        body, name=name, grid=(R // tr,),
        in_specs=[pl.BlockSpec((N_DEV, tr, C), lambda i: (0, i, 0)), row, row, row], out_specs=[row] * 4,
        out_shape=[_sds((R, C), F32)] * 4, compiler_params=_cp())(parts, w, m, v)


WEIGHTS = ['ev_norm', 'ev_w_in', 'ev_pool_w', 'ev_pool_scale', 'ev_q_norm', 'ev_w_q_up', 'ev_kv_norm', 'ev_w_kv_up',
           'ev_w_out', 'od_norm', 'od_w_in', 'od_conv_w', 'od_conv_b', 'od_w_rgate', 'od_b_rgate', 'od_w_igate',
           'od_b_igate', 'od_lambda', 'od_w_out', 'xa_norm_x', 'xa_norm_mem', 'xa_w_q', 'xa_w_kv', 'xa_w_o',
           'ffn_norm', 'ffn_w_gate_up', 'ffn_w_down', 'final_norm']
SHARD_AXIS = {'ev_w_in': 1, 'ev_w_q_up': 2, 'ev_w_kv_up': 2, 'ev_w_out': 1, 'od_norm': 1, 'od_w_in': 2,
              'od_conv_w': 2, 'od_conv_b': 1, 'od_w_rgate': 2, 'od_b_rgate': 1, 'od_w_igate': 2, 'od_b_igate': 1,
              'od_lambda': 1, 'od_w_out': 1, 'xa_w_q': 1, 'xa_w_kv': 2, 'xa_w_o': 1, 'ffn_w_gate_up': 2,
              'ffn_w_down': 1}
SMALL_F32 = ('od_norm', 'od_conv_w', 'od_conv_b', 'od_b_rgate', 'od_b_igate', 'od_lambda')
STACKED = ('ffn_w_gate_up', 'ffn_w_down')
SHARDED = [n for n in WEIGHTS if n in SHARD_AXIS]
REPLICATED = [n for n in WEIGHTS if n not in SHARD_AXIS]
ROW_ALIGN = 512


def _pack(flats, dtype):
    v = jnp.concatenate([f.reshape(-1).astype(dtype) for f in flats])
    pad = (-v.shape[0]) % (ROW_ALIGN * LANES)
    return jnp.pad(v, (0, pad)).reshape(-1, LANES)


def _rows8(n_elems):
    return -(-n_elems // (8 * LANES)) * 8


def _pack_rows(arrays, lead=False):
    out = []
    for a in arrays:
        r = a.reshape((N_DEV, -1, LANES) if lead else (-1, LANES))
        pad = _rows8(r.shape[-2] * LANES) - r.shape[-2]
        out.append(jnp.pad(r, [(0, 0)] * (r.ndim - 2) + [(0, pad), (0, 0)]))
    return jnp.concatenate(out, axis=-2)


def _unpack_rows(buf, shapes, lead=False):
    out, off = [], 0
    for s in shapes:
        n = 1
        for d in s:
            n *= d
        rows = buf[..., off:off + n // LANES, :]
        out.append(rows.reshape(((N_DEV,) if lead else ()) + tuple(s)))
        off += _rows8(n)
    return out


def _unpack(flat, shapes):
    out, off = [], 0
    v = flat.reshape(-1)
    for s in shapes:
        n = 1
        for d in s:
            n *= d
        out.append(v[off:off + n].reshape(s))
        off += n
    return out


def _to_full(stacked, axis):
    v = jnp.moveaxis(stacked, 0, axis)
    s = v.shape
    return v.reshape(s[:axis] + (s[axis] * s[axis + 1],) + s[axis + 2:])


def _to_shards(full, axis):
    s = full.shape
    v = full.reshape(s[:axis] + (N_DEV, s[axis] // N_DEV) + s[axis + 1:])
    return jnp.moveaxis(v, axis, 0)


def _pad_heads(w, nh, dh, lead):
    s = w.shape
    v = w.reshape(s[:-1] + (nh, dh))
    v = jnp.pad(v, [(0, 0)] * (len(s) - 1) + [(0, 0), (lead, LANES - dh - lead)])
    return v.reshape(s[:-1] + (nh * LANES,))


def _unpad_heads(w, nh, dh, lead):
    s = w.shape
    return w.reshape(s[:-1] + (nh, LANES))[..., lead:lead + dh].reshape(s[:-1] + (nh * dh,))


def _rope_tables(positions):
    inv_freq = 10000.0 ** (-jnp.arange(0, 32, 2, dtype=F32) / 32)
    ang = positions.astype(F32)[:, None] * inv_freq
    cos, sin = jnp.cos(ang), jnp.sin(ang)
    S = positions.shape[0]
    one, zero = jnp.ones((S, 64), F32), jnp.zeros((S, 64), F32)
    z16, z32 = jnp.zeros((S, 16), F32), jnp.zeros((S, 32), F32)
    c = jnp.concatenate([one, cos, cos, jnp.ones((S, 32), F32)], axis=1)
    a = jnp.concatenate([zero, z16, sin, z32], axis=1)
    b = jnp.concatenate([zero, -sin, z16, z32], axis=1)
    return c, a, b


def _t(w):
    return jnp.swapaxes(w, -1, -2)


def device_step(x, mem, positions, target, W, fwd_token=None, late_weights=None, ship_grads=None):
    S = x.shape[0]
    G = {}
    tabs = _rope_tables(positions)
    keep = (positions != 0).astype(F32)[:, None]
    row = lambda v: v.reshape(1, -1)

    w_in = W['ev_w_in'][0]
    ev_win = jnp.concatenate([w_in[:, :896], _pad_heads(w_in[:, 896:], 1, 32, 64)], axis=1)
    ev_wq = _pad_heads(W['ev_w_q_up'][0], MLA_HEADS, QK_DIM, 0)
    kvw = W['ev_w_kv_up'][0].reshape(KV_LORA, MLA_HEADS, 128)
    ev_wk = _pad_heads(kvw[:, :, :64].reshape(KV_LORA, 512), MLA_HEADS, 64, 0)
    ev_wv = _pad_heads(kvw[:, :, 64:].reshape(KV_LORA, 512), MLA_HEADS, 64, 0)
    ev_wo_pool = W['ev_w_out'][0][:POOL_DIM]
    ev_wo_att = _t(_pad_heads(_t(W['ev_w_out'][0][POOL_DIM:]), MLA_HEADS, 64, 0))
    pw = W['ev_pool_w'][0].astype(BF16)
    ev_g, ps, qg, kvg = row(W['ev_norm'][0]), row(W['ev_pool_scale'][0]), row(W['ev_q_norm'][0]), row(W['ev_kv_norm'][0])

    z0, qp, kp, vp, ypool = even_pre(x, tabs, ev_g, ev_win, pw, ps, qg, ev_wq, kvg, ev_wk, ev_wv)
    o_att, lse = attn_fwd(qp, kp, vp, fwd_token)
    if late_weights is not None:
        W = {**W, **late_weights(lse)}
    x1 = even_post(x, ypool, o_att, ev_wo_pool, ev_wo_att)

    def xa_ffn_fwd(xin, l):
        mn, km, vm = mem_kv(mem, row(W['xa_norm_mem'][l]), W['xa_w_kv'][l])
        xm = xattn_fwd(xin, row(W['xa_norm_x'][l]), W['xa_w_q'][l], km, vm, W['xa_w_o'][l])
        xo, hf, gu = ffn_fwd(xm, row(W['ffn_norm'][l]), W['ffn_w_gate_up'], l,
                             W['ffn_w_down'][:, l].reshape(FF_HALF, FF_CHUNK, D))
        return xm, xo, (mn, km, vm, hf, gu)

    x2, x3, memkv0 = xa_ffn_fwd(x1, 0)

    od_g, lam = row(W['od_norm'][0]), row(W['od_lambda'][0])
    cw, cb = W['od_conv_w'][0], row(W['od_conv_b'][0])
    wr, wi = W['od_w_rgate'][0], W['od_w_igate'][0]
    br, bi = row(W['od_b_rgate'][0]), row(W['od_b_igate'][0])
    z1, a_t, b_t = odd_pre(x3, keep, od_g, W['od_w_in'][0], cw, cb, wr, br, wi, bi, lam)
    hseq = lru_scan(a_t, b_t)
    x4 = odd_post(x3, z1, hseq, W['od_w_out'][0])
    x5, x6, memkv1 = xa_ffn_fwd(x4, 1)

    dx, G['final_norm'], loss = loss_head(x6, target, row(W['final_norm']))
    G['final_norm'] = G['final_norm'].reshape(D)

    gnx, gnm, gwq, gwkv, gwo, gfn, gwgu, gwd = ([None, None] for _ in range(8))

    def xa_ffn_bwd(dy, xin, xm, memkv, l):
        mn, km, vm, hf, gu = memkv
        fg = row(W['ffn_norm'][l])
        act, dgu = ffn_bwd_a(dy, gu, W['ffn_w_down'][:, l].reshape(FF_HALF, FF_CHUNK, D))
        gwd[l] = matmul_tn("ffn_dwd", act, dy).reshape(N_DEV, D_FF // N_DEV, D)
        gwgu[l] = matmul_tn("ffn_dwgu", hf, dgu)
        dxm, dfg = ffn_bwd_b(xm, dy, dgu, fg, W['ffn_w_gate_up'], l)
        gfn[l] = dfg[0]
        dxin, o, dq, hx, dgx, dk, dv = xattn_bwd(xin, dxm, row(W['xa_norm_x'][l]), W['xa_w_q'][l], km, vm,
                                                  W['xa_w_o'][l])
        gnx[l] = dgx[0]
        gwo[l] = matmul_tn("xa_dwo", o, dxm)
        gwq[l] = matmul_tn("xa_dwq", hx, dq)
        dkv, dgm = mem_bwd(mem, row(W['xa_norm_mem'][l]), dk, dv, W['xa_w_kv'][l])
        gnm[l] = dgm[0]
        gwkv[l] = matmul_tn("xa_dwkv", mn, dkv)
        return dxin

    dx4 = xa_ffn_bwd(dx, x4, x5, memkv1, 1)

    dgate, dhs, g_od_wout = odd_post_bwd(dx4, z1, hseq, W['od_w_out'][0])
    G['od_w_out'] = g_od_wout[None]
    lam_grad = lru_scan(a_t, dhs, reverse=True)
    dxb, dcb, dbr, dbi, dlam, dwr, dwi = odd_gates_bwd(z1, lam_grad, hseq, keep, cw, cb, wr, br, wi, bi, lam)
    dx3, dcw, dg_od, g_od_win = odd_pre_bwd(x3, dx4, z1, dxb, dgate, od_g, cw, W['od_w_in'][0])
    G['od_w_in'] = g_od_win[None]
    G['od_norm'], G['od_conv_w'], G['od_conv_b'] = dg_od, dcw[None], dcb
    G['od_w_rgate'], G['od_b_rgate'], G['od_w_igate'], G['od_b_igate'], G['od_lambda'] = (
        dwr[None], dbr, dwi[None], dbi, dlam)

    dx1 = xa_ffn_bwd(dx3, x1, x2, memkv0, 0)
    G['xa_norm_x'], G['xa_norm_mem'], G['ffn_norm'] = jnp.stack(gnx), jnp.stack(gnm), jnp.stack(gfn)
    G['xa_w_q'], G['xa_w_kv'], G['xa_w_o'] = jnp.stack(gwq), jnp.stack(gwkv), jnp.stack(gwo)
    G['ffn_w_gate_up'], G['ffn_w_down'] = jnp.stack(gwgu, axis=1), jnp.stack(gwd, axis=1)
    bwd_token = ship_grads(G) if ship_grads is not None else None

    dyp, do_att, delta, g_wo_pool, g_wo_att = even_post_bwd(dx1, ypool, o_att, ev_wo_pool, ev_wo_att)
    G['ev_w_out'] = jnp.concatenate([g_wo_pool, _t(_unpad_heads(_t(g_wo_att), MLA_HEADS, 64, 0))], axis=0)[None]
    dq, dk, dv = attn_bwd(qp, kp, vp, do_att, lse, delta, bwd_token)
    (grad_x, dg_ev, dpw, dps, dqg, dwq, dkvg, dwk, dwv, g_win) = even_pre_bwd(
        x, dx1, z0, dq, dk, dv, dyp, tabs, ev_g, ev_win, pw, ps, qg, ev_wq, kvg, ev_wk, ev_wv)
    G['ev_w_in'] = jnp.concatenate([g_win[:, :896], _unpad_heads(g_win[:, 896:], 1, 32, 64)], axis=1)[None]
    G['ev_norm'], G['ev_pool_w'], G['ev_pool_scale'], G['ev_q_norm'], G['ev_kv_norm'] = (
        dg_ev, dpw[None], dps, dqg, dkvg)
    G['ev_w_q_up'] = _unpad_heads(dwq, MLA_HEADS, QK_DIM, 0)[None]
    gk = _unpad_heads(dwk, MLA_HEADS, 64, 0).reshape(KV_LORA, MLA_HEADS, 64)
    gv = _unpad_heads(dwv, MLA_HEADS, 64, 0).reshape(KV_LORA, MLA_HEADS, 64)
    G['ev_w_kv_up'] = jnp.concatenate([gk, gv], axis=2).reshape(1, KV_LORA, MLA_HEADS * 128)
    return loss[0, 0], grad_x, G


def kernel(x, mem, positions, ev_norm, ev_w_in, ev_pool_w, ev_pool_scale, ev_q_norm, ev_w_q_up, ev_kv_norm, ev_w_kv_up, ev_w_out, od_norm, od_w_in, od_conv_w, od_conv_b, od_w_rgate, od_b_rgate, od_w_igate, od_b_igate, od_lambda, od_w_out, xa_norm_x, xa_norm_mem, xa_w_q, xa_w_kv, xa_w_o, ffn_norm, ffn_w_gate_up, ffn_w_down, final_norm, loss_target, m_ev_norm, m_ev_w_in, m_ev_pool_w, m_ev_pool_scale, m_ev_q_norm, m_ev_w_q_up, m_ev_kv_norm, m_ev_w_kv_up, m_ev_w_out, m_od_norm, m_od_w_in, m_od_conv_w, m_od_conv_b, m_od_w_rgate, m_od_b_rgate, m_od_w_igate, m_od_b_igate, m_od_lambda, m_od_w_out, m_xa_norm_x, m_xa_norm_mem, m_xa_w_q, m_xa_w_kv, m_xa_w_o, m_ffn_norm, m_ffn_w_gate_up, m_ffn_w_down, m_final_norm, v_ev_norm, v_ev_w_in, v_ev_pool_w, v_ev_pool_scale, v_ev_q_norm, v_ev_w_q_up, v_ev_kv_norm, v_ev_w_kv_up, v_ev_w_out, v_od_norm, v_od_w_in, v_od_conv_w, v_od_conv_b, v_od_w_rgate, v_od_b_rgate, v_od_w_igate, v_od_b_igate, v_od_lambda, v_od_w_out, v_xa_norm_x, v_xa_norm_mem, v_xa_w_q, v_xa_w_kv, v_xa_w_o, v_ffn_norm, v_ffn_w_gate_up, v_ffn_w_down, v_final_norm):
    args = dict(locals())
    w = {n: args[n] for n in WEIGHTS}
    m = {n: args['m_' + n] for n in WEIGHTS}
    v = {n: args['v_' + n] for n in WEIGHTS}
    big = [n for n in SHARDED if n not in SMALL_F32]
    small = [n for n in SHARDED if n in SMALL_F32]

    small_shapes = [w[n].shape for n in small]
    first = [n for n in big if n.startswith('ev_')]
    late = [n for n in big if n not in first]

    def full(n, st):
        return st if n in STACKED else _to_full(st, SHARD_AXIS[n])

    W = {n: w[n] for n in REPLICATED}
    W.update((n, full(n, st)) for n, st in zip(first, all_gather("gather_ev_weights", [w[n].astype(BF16) for n in first])))
    gather = split_start("gather_start", [w[n].astype(BF16) for n in late] + [_pack_rows([w[n] for n in small])], True)

    def late_weights(after):
        srcs, lands = split_wait("gather_wait", gather, after, True)
        lands = [_fill_own_slot(s, l, True) for s, l in zip(srcs, lands)]
        out = {n: full(n, st) for n, st in zip(late, lands)}
        out.update((n, _to_full(st, SHARD_AXIS[n])) for n, st in zip(small, _unpack_rows(lands[-1], small_shapes, True)))
        return out

    def shards(G, n):
        return G[n] if n in STACKED else _to_shards(G[n], SHARD_AXIS[n])

    shipped = []

    def ship_grads(G):
        shipped.append(split_start("exchange_start", [shards(G, n).astype(BF16) for n in late] +
                                   [_pack_rows([shards(G, n) for n in small], lead=True)], False))
        return shipped[0][-1]

    loss, grad_x, G = device_step(x[0], mem[0], positions[0], loss_target[0], W, gather[-1], late_weights, ship_grads)
    outs = [{}, {}, {}, {}]

    rep_shapes = [w[n].shape for n in REPLICATED] + [(LANES,)]
    zero = jnp.zeros((LANES,), F32)
    *first_parts, rep_parts = exchange(
        "exchange_ev_and_rep_grads", [shards(G, n).astype(BF16) for n in first],
        [_pack([G[n] for n in REPLICATED] + [jnp.broadcast_to(loss, (LANES,))], F32)])
    rep = adamw("adamw_rep", rep_parts, *[_pack([d[n] for n in REPLICATED] + [zero], F32) for d in (w, m, v)])
    for k in range(4):
        outs[k].update(zip(REPLICATED + ['loss'], _unpack(rep[k], rep_shapes)))
    loss = outs[0]['loss'][0]

    srcs, lands = split_wait("exchange_wait", shipped[0], grad_x, False)
    late_parts = [_fill_own_slot(s, l, False) for s, l in zip(srcs, lands)]
    parts = first_parts + late_parts
    two_d = lambda a: a.reshape(-1, a.shape[-1])
    for n, p in zip(first + late, parts):
        res = adamw("adamw_" + n, p.reshape((N_DEV,) + two_d(w[n]).shape), two_d(w[n]), two_d(m[n]), two_d(v[n]))
        for k in range(4):
            outs[k][n] = res[k].reshape(w[n].shape)
    res = adamw("adamw_small", parts[-1], *[_pack_rows([d[n] for n in small]) for d in (w, m, v)])
    for k in range(4):
        outs[k].update(zip(small, _unpack_rows(res[k], small_shapes)))

    return (loss, grad_x[None], *[outs[0][n] for n in WEIGHTS], *[outs[1][n] for n in WEIGHTS],
            *[outs[2][n] for n in WEIGHTS], *[outs[3][n] for n in WEIGHTS])
```

```python
import functools

import jax
import jax.numpy as jnp
from jax import lax
from jax.experimental import pallas as pl
from jax.experimental.pallas import tpu as pltpu

F32, BF16 = jnp.float32, jnp.bfloat16
N_DEV = 8
D = 1024
POOL_DIM = 512
POOL_WINDOWS = (2, 4, 8, 16)
MLA_HEADS = 8
QK_DIM = 96
Q_LORA, KV_LORA = 256, 128
LRU_HEADS, LRU_HEAD_DIM = 4, 256
LRU_C = 8.0
MEM_HEADS, MEM_HEAD_DIM = 4, 256
D_FF = 2816
RMS_EPS = 1e-6
ADAM_LR, ADAM_B1, ADAM_B2, ADAM_EPS, ADAM_WD, ADAM_STEP = 0.001, 0.9, 0.999, 1e-08, 0.01, 10
LANES = 128
POOL_HALO = 16
CONV_HALO = 8
VMEM_LIMIT = 60000 * 1024


def _cp():
    return pltpu.CompilerParams(dimension_semantics=("arbitrary",), vmem_limit_bytes=VMEM_LIMIT)


def _cp2():
    return pltpu.CompilerParams(dimension_semantics=("arbitrary", "arbitrary"), vmem_limit_bytes=VMEM_LIMIT)


def _row(ts, c, col=0):
    return pl.BlockSpec((ts, c), lambda i: (i, col))


def _prev(hr, c, ts, col=0):
    r = ts // hr
    return pl.BlockSpec((hr, c), lambda i: (jnp.maximum(i * r - 1, 0), col))


def _next(hr, c, ts, n, col=0):
    r = ts // hr
    return pl.BlockSpec((hr, c), lambda i: (jnp.minimum((i + 1) * r, n * r - 1), col))


def _const(shape):
    nd = len(shape)
    return pl.BlockSpec(tuple(shape), lambda i: (0,) * nd, pipeline_mode=pl.Buffered(1))


def _acc(shape):
    nd = len(shape)
    return pl.BlockSpec(tuple(shape), lambda i: (0,) * nd)


def _sds(shape, dt):
    return jax.ShapeDtypeStruct(tuple(shape), dt)


def _dot(a, b):
    return jnp.dot(a.astype(BF16), b.astype(BF16), preferred_element_type=F32)


def _dot_nt(a, b):
    return lax.dot_general(a.astype(BF16), b.astype(BF16), (((1,), (1,)), ((), ())), preferred_element_type=F32)


def _dot_tn(a, b):
    return lax.dot_general(a.astype(BF16), b.astype(BF16), (((0,), (0,)), ((), ())), preferred_element_type=F32)


def _rms(x, g):
    rstd = lax.rsqrt(jnp.mean(x * x, axis=-1, keepdims=True) + RMS_EPS)
    return x * rstd * g, rstd


def _rms_bwd(x, g, rstd, dy):
    xn = x * rstd
    dyg = dy * g
    dx = rstd * (dyg - xn * jnp.mean(dyg * xn, axis=-1, keepdims=True))
    return dx, dy * xn


def _rowsum(v):
    return jnp.sum(v, axis=0, keepdims=True)


def _roll(v, s, axis):
    n = v.shape[axis]
    return pltpu.roll(v, s % n, axis)


def _rope(t, c, a, b):
    k = t.shape[1] // LANES
    if k > 1:
        c, a, b = (jnp.tile(v, (1, k)) for v in (c, a, b))
    return t * c + _roll(t, 16, 1) * a + _roll(t, -16, 1) * b


def _rope_bwd(d, c, a, b):
    k = d.shape[1] // LANES
    if k > 1:
        c, a, b = (jnp.tile(v, (1, k)) for v in (c, a, b))
    return d * c + _roll(d * a, -16, 1) + _roll(d * b, 16, 1)


def _gelu(x):
    c = 0.7978845608028654
    t = jnp.tanh(c * (x + 0.044715 * x * x * x))
    return 0.5 * x * (1.0 + t), t


def _gelu_grad(x, t):
    c = 0.7978845608028654
    return 0.5 * (1.0 + t) + 0.5 * x * (1.0 - t * t) * c * (1.0 + 3.0 * 0.044715 * x * x)


def _blockdot(v, w_ref, nblk, width):
    return jnp.concatenate(
        [_dot(v[:, j * width:(j + 1) * width], w_ref[j]) for j in range(nblk)], axis=1)


def _pool_cnt(row0, rows):
    t = row0 + lax.broadcasted_iota(jnp.int32, (rows, POOL_DIM), 0)
    w = jnp.left_shift(2, lax.broadcasted_iota(jnp.int32, (rows, POOL_DIM), 1) // LANES)
    return jnp.minimum(t + 1, w).astype(F32)


def _pool_windows(ext, sign):
    s2 = ext + _roll(ext, sign * 1, 0)
    t = s2[:, LANES:]
    s4 = t + _roll(t, sign * 2, 0)
    t = s4[:, LANES:]
    s8 = t + _roll(t, sign * 4, 0)
    t = s8[:, LANES:]
    s16 = t + _roll(t, sign * 8, 0)
    return jnp.concatenate([s2[:, :LANES], s4[:, :LANES], s8[:, :LANES], s16], axis=1)


def _pooled(uprev, u, row0):
    ts = u.shape[0]
    ext = jnp.concatenate([uprev, u], axis=0)
    sums = _pool_windows(ext, 1)[POOL_HALO:]
    return sums / _pool_cnt(row0, ts) - u


def _expm1(x):
    return jnp.where(jnp.abs(x) < 0.01, x * (1.0 + 0.5 * x * (1.0 + x * (1.0 / 3.0))), jnp.exp(x) - 1.0)


def _softplus(z):
    return jnp.maximum(z, 0.0) + jnp.log1p(jnp.exp(-jnp.abs(z)))


def _tile_rows(s, want):
    while s % want:
        want //= 2
    return want


def even_pre(x, tabs, g, win, pw, pscale, qg, wq, kvg, wk, wv):
    S = x.shape[0]
    ts = _tile_rows(S, 512)

    def body(x_ref, xp_ref, c_ref, a_ref, b_ref, g_ref, win_ref, pw_ref, ps_ref, qg_ref, wq_ref, kvg_ref,
             wk_ref, wv_ref, z_ref, q_ref, k_ref, v_ref, yp_ref):
        i = pl.program_id(0)
        h, _ = _rms(x_ref[...], g_ref[...])
        z = _dot(h, win_ref[...])
        z_ref[...] = z
        hp, _ = _rms(xp_ref[...], g_ref[...])
        uprev = _dot(hp, win_ref[:, :POOL_DIM]) * (i > 0).astype(F32)
        u = z[:, :POOL_DIM]
        pooled = _pooled(uprev, u, i * ts)
        yp_ref[...] = (_blockdot(pooled, pw_ref, 4, LANES) * ps_ref[...]).astype(BF16)
        c, a, b = c_ref[...], a_ref[...], b_ref[...]
        cqn, _ = _rms(z[:, 512:768], qg_ref[...])
        q_ref[...] = (_rope(_dot(cqn, wq_ref[...]), c, a, b) * (ATTN_SCALE * LOG2_E)).astype(BF16)
        ckvn, _ = _rms(z[:, 768:896], kvg_ref[...])
        krr = _rope(z[:, 896:1024], c, a, b)
        k_ref[...] = (_dot(ckvn, wk_ref[...]) + jnp.tile(krr, (1, MLA_HEADS))).astype(BF16)
        lane = lax.broadcasted_iota(jnp.int32, (ts, D), 1) % LANES
        v_ref[...] = jnp.where(lane == ONES_LANE, 1.0, _dot(ckvn, wv_ref[...])).astype(BF16)

    ins = [x, x, *tabs, g, win, pw, pscale, qg, wq, kvg, wk, wv]
    in_specs = [_row(ts, D), _prev(POOL_HALO, D, ts), _row(ts, LANES), _row(ts, LANES), _row(ts, LANES)]
    in_specs += [_const(v.shape) for v in ins[5:]]
    return pl.pallas_call(
        body, name="even_pre", grid=(S // ts,), in_specs=in_specs,
        out_specs=[_row(ts, D)] * 4 + [_row(ts, POOL_DIM)],
        out_shape=[_sds((S, D), F32)] + [_sds((S, D), BF16)] * 3 + [_sds((S, POOL_DIM), BF16)],
        compiler_params=_cp())(*ins)


ATTN_SCALE = QK_DIM ** -0.5
LOG2_E = 1.4426950408889634
LN_2 = 0.6931471805599453
ONES_LANE = 64


def _exp2(x):
    return jnp.exp2(x)


def _pair_loop(lo, hi, step, init, unrolls=(2, 1)):
    carry = init
    for unroll in unrolls:
        groups = (hi - lo) // unroll

        def group(j, c, lo=lo, unroll=unroll):
            for u in range(unroll):
                c = step(lo + unroll * j + u, c)
            return c

        carry = lax.fori_loop(0, groups, group, carry)
        lo = lo + unroll * groups
    return carry


def _as_row(col):
    return jnp.transpose(jnp.broadcast_to(col, (col.shape[0], LANES)))[0:1, :]


def _after(token):
    return ([], []) if token is None else ([token], [pl.BlockSpec(memory_space=pl.ANY)])


def attn_fwd(qp, kp, vp, token=None):
    S = qp.shape[0]
    tq = _tile_rows(S, 512)
    extra, extra_specs = _after(token)

    def body(q_ref, k_ref, v_ref, *rest):
        o_ref, lse_ref = rest[-2:]
        qi = pl.program_id(1)
        q = q_ref[...]

        def block(ki, carry, masked):
            m, acc = carry
            off = pl.multiple_of(ki * tq, tq)
            s = _dot_nt(q, k_ref[pl.ds(off, tq), :])
            if masked:
                row = lax.broadcasted_iota(jnp.int32, (tq, tq), 0)
                col = lax.broadcasted_iota(jnp.int32, (tq, tq), 1)
                s = jnp.where(col <= row, s, -1e30)
            m_new = jnp.maximum(m, jnp.max(s, axis=1, keepdims=True))
            acc = _exp2(m - m_new) * acc + _dot(_exp2(s - m_new), v_ref[pl.ds(off, tq), :])
            return m_new, acc

        init = (jnp.full((tq, 1), -1e30, F32), jnp.zeros((tq, LANES), F32))
        carry = _pair_loop(0, qi, lambda ki, c: block(ki, c, False), init, unrolls=(8, 4, 2, 1))
        m, acc = block(qi, carry, True)
        l = acc[:, ONES_LANE:ONES_LANE + 1]
        o_ref[...] = acc / l
        lse_ref[...] = _as_row(m + jnp.log(l) * LOG2_E)

    blk = pl.BlockSpec((tq, LANES), lambda h, i: (i, h))
    full = pl.BlockSpec((S, LANES), lambda h, i: (0, h))
    return pl.pallas_call(
        body, name="attn_fwd", grid=(MLA_HEADS, S // tq), in_specs=[blk, full, full] + extra_specs,
        out_specs=[blk, pl.BlockSpec((None, None, 1, tq), lambda h, i: (h, i, 0, 0))],
        out_shape=[_sds((S, D), F32), _sds((MLA_HEADS, S // tq, 1, tq), F32)], compiler_params=_cp2())(
            qp, kp, vp, *extra)


def even_post(x, ypool, o, wo_pool, wo_att):
    S = x.shape[0]
    ts = _tile_rows(S, 512)

    def body(x_ref, yp_ref, o_ref, wp_ref, wa_ref, out_ref):
        out_ref[...] = x_ref[...] + _dot(yp_ref[...], wp_ref[...]) + _dot(o_ref[...], wa_ref[...])

    return pl.pallas_call(
        body, name="even_post", grid=(S // ts,),
        in_specs=[_row(ts, D), _row(ts, POOL_DIM), _row(ts, D), _const(wo_pool.shape), _const(wo_att.shape)],
        out_specs=_row(ts, D), out_shape=_sds((S, D), F32), compiler_params=_cp())(x, ypool, o, wo_pool, wo_att)


def mem_kv(mem, g, wkv):
    M = mem.shape[0]

    def body(mem_ref, g_ref, w_ref, mn_ref, k_ref, v_ref):
        mn, _ = _rms(mem_ref[...], g_ref[...])
        mn_ref[...] = mn.astype(BF16)
        k_ref[...] = _dot(mn, w_ref[:, :D]).astype(BF16)
        v_ref[...] = _dot(mn, w_ref[:, D:]).astype(BF16)

    return pl.pallas_call(
        body, name="mem_kv", grid=(1,), in_specs=[_acc(mem.shape), _acc(g.shape), _acc(wkv.shape)],
        out_specs=[_acc((M, D))] * 3, out_shape=[_sds((M, D), BF16)] * 3, compiler_params=_cp())(mem, g, wkv)


def _xattn_heads(hx, wq_ref, k_ref, v_ref):
    q = _dot(hx, wq_ref[...])
    scale = MEM_HEAD_DIM ** -0.5
    ps, os_ = [], []
    for h in range(MEM_HEADS):
        sl = slice(h * MEM_HEAD_DIM, (h + 1) * MEM_HEAD_DIM)
        s = _dot_nt(q[:, sl], k_ref[:, sl]) * scale
        e = jnp.exp(s - jnp.max(s, axis=1, keepdims=True))
        p = e / jnp.sum(e, axis=1, keepdims=True)
        ps.append(p)
        os_.append(_dot(p, v_ref[:, sl]))
    return q, ps, jnp.concatenate(os_, axis=1)


def xattn_fwd(x, g, wq, kmem, vmem, wo):
    S = x.shape[0]
    ts = _tile_rows(S, 512)

    def body(x_ref, g_ref, wq_ref, k_ref, v_ref, wo_ref, out_ref):
        x_ = x_ref[...]
        hx, _ = _rms(x_, g_ref[...])
        _, _, o = _xattn_heads(hx, wq_ref, k_ref, v_ref)
        out_ref[...] = x_ + _dot(o, wo_ref[...])

    ins = [x, g, wq, kmem, vmem, wo]
    return pl.pallas_call(
        body, name="xattn_fwd", grid=(S // ts,), in_specs=[_row(ts, D)] + [_const(v.shape) for v in ins[1:]],
        out_specs=_row(ts, D), out_shape=_sds((S, D), F32), compiler_params=_cp())(*ins)


def xattn_bwd(x, dy, g, wq, kmem, vmem, wo):
    S = x.shape[0]
    M = kmem.shape[0]
    ts = _tile_rows(S, 512)
    scale = MEM_HEAD_DIM ** -0.5

    def body(x_ref, dy_ref, g_ref, wq_ref, k_ref, v_ref, wo_ref,
             dx_ref, o_ref, dq_ref, hx_ref, dg_ref, dk_ref, dv_ref):
        i = pl.program_id(0)

        @pl.when(i == 0)
        def _():
            dg_ref[...] = jnp.zeros_like(dg_ref)
            dk_ref[...] = jnp.zeros_like(dk_ref)
            dv_ref[...] = jnp.zeros_like(dv_ref)

        x_, dy_ = x_ref[...], dy_ref[...]
        hx, rstd = _rms(x_, g_ref[...])
        q, ps, o = _xattn_heads(hx, wq_ref, k_ref, v_ref)
        hx_ref[...] = hx.astype(BF16)
        o_ref[...] = o.astype(BF16)
        do = _dot_nt(dy_, wo_ref[...])
        dqs = []
        for h in range(MEM_HEADS):
            sl = slice(h * MEM_HEAD_DIM, (h + 1) * MEM_HEAD_DIM)
            p, do_h = ps[h], do[:, sl]
            dp = _dot_nt(do_h, v_ref[:, sl])
            ds = p * (dp - jnp.sum(p * dp, axis=1, keepdims=True)) * scale
            dqs.append(_dot(ds, k_ref[:, sl]))
            dk_ref[:, sl] += _dot_tn(ds, q[:, sl])
            dv_ref[:, sl] += _dot_tn(p, do_h)
        dq = jnp.concatenate(dqs, axis=1).astype(BF16)
        dq_ref[...] = dq
        dxn, dgr = _rms_bwd(x_, g_ref[...], rstd, _dot_nt(dq, wq_ref[...]))
        dx_ref[...] = dy_ + dxn
        dg_ref[...] += _rowsum(dgr)

    ins = [x, dy, g, wq, kmem, vmem, wo]
    return pl.pallas_call(
        body, name="xattn_bwd", grid=(S // ts,),
        in_specs=[_row(ts, D), _row(ts, D)] + [_const(v.shape) for v in ins[2:]],
        out_specs=[_row(ts, D)] * 4 + [_acc((1, D)), _acc((M, D)), _acc((M, D))],
        out_shape=[_sds((S, D), F32)] + [_sds((S, D), BF16)] * 3 + [_sds((1, D), F32), _sds((M, D), F32),
                                                                    _sds((M, D), F32)],
        compiler_params=_cp())(*ins)


def mem_bwd(mem, g, dk, dv, wkv):
    M = mem.shape[0]

    def body(mem_ref, g_ref, dk_ref, dv_ref, w_ref, dkv_ref, dg_ref):
        dkv = jnp.concatenate([dk_ref[...], dv_ref[...]], axis=1)
        dkv_ref[...] = dkv.astype(BF16)
        _, rstd = _rms(mem_ref[...], g_ref[...])
        dg_ref[...] = _rowsum(_dot_nt(dkv, w_ref[...]) * (mem_ref[...] * rstd))

    ins = [mem, g, dk, dv, wkv]
    return pl.pallas_call(
        body, name="mem_bwd", grid=(1,), in_specs=[_acc(v.shape) for v in ins],
        out_specs=[_acc((M, 2 * D)), _acc((1, D))], out_shape=[_sds((M, 2 * D), BF16), _sds((1, D), F32)],
        compiler_params=_cp())(*ins)


FF_CHUNK = 2 * D_FF // N_DEV
FF_HALF = N_DEV // 2


def _layer_of(w, layer):
    return pl.BlockSpec((N_DEV, None) + w.shape[2:], lambda i: (0, layer, 0, 0), pipeline_mode=pl.Buffered(1))


def _ff_chunks(c, ts):
    return pl.BlockSpec((c, ts, FF_CHUNK), lambda i: (0, i, 0))


def ffn_fwd(x, g, wgu, layer, wd):
    S = x.shape[0]
    ts = _tile_rows(S, 256)

    def body(x_ref, g_ref, wgu_ref, wd_ref, out_ref, hf_ref, gu_ref):
        x_ = x_ref[...]
        hf = _rms(x_, g_ref[...])[0].astype(BF16)
        hf_ref[...] = hf
        out = x_
        for j in range(FF_HALF):
            gg, uu = _dot(hf, wgu_ref[j]), _dot(hf, wgu_ref[j + FF_HALF])
            gu_ref[j] = gg.astype(BF16)
            gu_ref[j + FF_HALF] = uu.astype(BF16)
            out = out + _dot(gg * jax.nn.sigmoid(gg) * uu, wd_ref[j])
        out_ref[...] = out

    return pl.pallas_call(
        body, name="ffn_fwd", grid=(S // ts,),
        in_specs=[_row(ts, D), _const(g.shape), _layer_of(wgu, layer), _const(wd.shape)],
        out_specs=[_row(ts, D), _row(ts, D), _ff_chunks(N_DEV, ts)],
        out_shape=[_sds((S, D), F32), _sds((S, D), BF16), _sds((N_DEV, S, FF_CHUNK), BF16)],
        compiler_params=_cp())(x, g, wgu, wd)


def ffn_bwd_a(dy, gu, wd):
    S = dy.shape[0]
    ts = _tile_rows(S, 256)

    def body(dy_ref, gu_ref, wd_ref, act_ref, dgu_ref):
        dy_ = dy_ref[...].astype(BF16)
        for j in range(FF_HALF):
            gg, uu = gu_ref[j].astype(F32), gu_ref[j + FF_HALF].astype(F32)
            sg = jax.nn.sigmoid(gg)
            silu = gg * sg
            act_ref[j] = (silu * uu).astype(BF16)
            dact = _dot_nt(dy_, wd_ref[j])
            dgu_ref[j] = (dact * uu * (sg * (1.0 + gg * (1.0 - sg)))).astype(BF16)
            dgu_ref[j + FF_HALF] = (dact * silu).astype(BF16)

    return pl.pallas_call(
        body, name="ffn_bwd_a", grid=(S // ts,),
        in_specs=[_row(ts, D), _ff_chunks(N_DEV, ts), _const(wd.shape)],
        out_specs=[_ff_chunks(FF_HALF, ts), _ff_chunks(N_DEV, ts)],
        out_shape=[_sds((FF_HALF, S, FF_CHUNK), BF16), _sds((N_DEV, S, FF_CHUNK), BF16)],
        compiler_params=_cp())(dy, gu, wd)


def ffn_bwd_b(x, dy, dgu, g, wgu, layer):
    S = x.shape[0]
    ts = _tile_rows(S, 512)

    def body(x_ref, dy_ref, dgu_ref, g_ref, w_ref, dx_ref, dg_ref):
        @pl.when(pl.program_id(0) == 0)
        def _():
            dg_ref[...] = jnp.zeros_like(dg_ref)

        dh = _dot_nt(dgu_ref[0], w_ref[0])
        for j in range(1, N_DEV):
            dh = dh + _dot_nt(dgu_ref[j], w_ref[j])
        x_ = x_ref[...]
        _, rstd = _rms(x_, g_ref[...])
        dxn, dgr = _rms_bwd(x_, g_ref[...], rstd, dh)
        dx_ref[...] = dy_ref[...] + dxn
        dg_ref[...] += _rowsum(dgr)

    return pl.pallas_call(
        body, name="ffn_bwd_b", grid=(S // ts,),
        in_specs=[_row(ts, D), _row(ts, D), pl.BlockSpec((N_DEV, ts, FF_CHUNK), lambda i: (0, i, 0)),
                  _const(g.shape), _layer_of(wgu, layer)],
        out_specs=[_row(ts, D), _acc((1, D))], out_shape=[_sds((S, D), F32), _sds((1, D), F32)],
        compiler_params=_cp())(x, dy, dgu, g, wgu)


def _conv_fwd(xprev, xbp, cw_ref, cb):
    ext = jnp.concatenate([xprev, xbp], axis=0)
    acc = cb + cw_ref[3:4, :] * xbp
    for k in range(3):
        acc = acc + cw_ref[k:k + 1, :] * _roll(ext, 3 - k, 0)[CONV_HALO:]
    return acc


def _decay(r, lam):
    sp = _softplus(-lam)
    log_a = -LRU_C * r * sp
    return sp, jnp.exp(log_a), jnp.sqrt(jnp.maximum(-_expm1(2.0 * log_a), 0.0))


def odd_pre(x, keep, g, win, cw, cb, wr, br, wi, bi, lam):
    S = x.shape[0]
    ts = _tile_rows(S, 512)

    def body(x_ref, xp_ref, keep_ref, g_ref, win_ref, cw_ref, cb_ref, wr_ref, br_ref, wi_ref, bi_ref, lam_ref,
             z_ref, a_ref, b_ref, xb_ref, r_ref, ig_ref):
        i = pl.program_id(0)
        h, _ = _rms(x_ref[...], g_ref[...])
        z = _dot(h, win_ref[...])
        z_ref[...] = z
        hp, _ = _rms(xp_ref[...], g_ref[...])
        xprev = _dot(hp, win_ref[:, D:]) * (i > 0).astype(F32)
        xb = _conv_fwd(xprev, z[:, D:], cw_ref, cb_ref[...])
        xb_ref[...] = xb
        r = jax.nn.sigmoid(_blockdot(xb, wr_ref, LRU_HEADS, LRU_HEAD_DIM) + br_ref[...])
        ig = jax.nn.sigmoid(_blockdot(xb, wi_ref, LRU_HEADS, LRU_HEAD_DIM) + bi_ref[...])
        r_ref[...] = r
        ig_ref[...] = ig
        keep_ = keep_ref[...]
        _, a, mult = _decay(r, lam_ref[...])
        a_ref[...] = a * keep_
        b_ref[...] = jnp.where(keep_ > 0.0, mult, 1.0) * (ig * xb)

    ins = [x, x, keep, g, win, cw, cb, wr, br, wi, bi, lam]
    return pl.pallas_call(
        body, name="odd_pre", grid=(S // ts,),
        in_specs=[_row(ts, D), _prev(CONV_HALO, D, ts), _row(ts, 1)] + [_const(v.shape) for v in ins[3:]],
        out_specs=[_row(ts, 2 * D)] + [_row(ts, D)] * 5,
        out_shape=[_sds((S, 2 * D), F32)] + [_sds((S, D), F32)] * 5, compiler_params=_cp())(*ins)


def lru_scan(a, b, reverse=False):
    S = a.shape[0]
    ts = _tile_rows(S, 512)
    n = S // ts
    groups = ts // 8

    def body(a_ref, an_ref, b_ref, h_ref, carry_ref, ash_ref):
        i = pl.program_id(0)

        @pl.when(i == 0)
        def _():
            carry_ref[...] = jnp.zeros_like(carry_ref)

        rid = lax.broadcasted_iota(jnp.int32, (8, D), 0)
        if reverse:
            ext = jnp.concatenate([a_ref[...], an_ref[...] * (i > 0).astype(F32)], axis=0)
            ash_ref[...] = _roll(ext, -1, 0)[:ts]
        src = ash_ref if reverse else a_ref

        def group(j, carry):
            off = pl.multiple_of((groups - 1 - j if reverse else j) * 8, 8)
            a8, b8 = src[pl.ds(off, 8), :], b_ref[pl.ds(off, 8), :]
            for k in (1, 2, 4):
                inside = (rid < 8 - k) if reverse else (rid >= k)
                sh = -k if reverse else k
                a_sh = jnp.where(inside, _roll(a8, sh, 0), 1.0)
                b_sh = jnp.where(inside, _roll(b8, sh, 0), 0.0)
                b8 = a8 * b_sh + b8
                a8 = a8 * a_sh
            h8 = a8 * carry + b8
            h_ref[pl.ds(off, 8), :] = h8
            return h8[0:1, :] if reverse else h8[7:8, :]

        carry_ref[...] = lax.fori_loop(0, groups, group, carry_ref[...], unroll=4)

    if reverse:
        r = ts // 8
        tile = pl.BlockSpec((ts, D), lambda i: (n - 1 - i, 0))
        halo = pl.BlockSpec((8, D), lambda i: (jnp.minimum((n - i) * r, n * r - 1), 0))
    else:
        tile, halo = _row(ts, D), _prev(8, D, ts)
    return pl.pallas_call(
        body, name="lru_scan_rev" if reverse else "lru_scan", grid=(n,), in_specs=[tile, halo, tile],
        out_specs=tile, out_shape=_sds((S, D), F32),
        scratch_shapes=[pltpu.VMEM((1, D), F32), pltpu.VMEM((ts, D), F32)], compiler_params=_cp())(a, a, b)


def odd_post(x, z, hseq, wout):
    S = x.shape[0]
    ts = _tile_rows(S, 512)

    def body(x_ref, gate_ref, h_ref, w_ref, out_ref):
        gl, _ = _gelu(gate_ref[...])
        out_ref[...] = x_ref[...] + _dot(gl * h_ref[...], w_ref[...])

    return pl.pallas_call(
        body, name="odd_post", grid=(S // ts,),
        in_specs=[_row(ts, D), _row(ts, D), _row(ts, D), _const(wout.shape)],
        out_specs=_row(ts, D), out_shape=_sds((S, D), F32), compiler_params=_cp())(x, z, hseq, wout)


def _accumulate_tn(acc_ref, out_ref, a, b, steps):
    i = pl.program_id(0)

    @pl.when(i == 0)
    def _():
        acc_ref[...] = jnp.zeros_like(acc_ref)

    acc_ref[...] += _dot_tn(a, b)

    @pl.when(i == steps - 1)
    def _():
        out_ref[...] = acc_ref[...].astype(out_ref.dtype)


def odd_post_bwd(dy, z, hseq, wout):
    S = dy.shape[0]
    ts = _tile_rows(S, 512)
    n = S // ts

    def body(dy_ref, gate_ref, h_ref, w_ref, dgate_ref, dh_ref, dw_ref, acc_ref):
        gate, hs, dy_ = gate_ref[...], h_ref[...], dy_ref[...]
        gl, t = _gelu(gate)
        dyy = _dot_nt(dy_, w_ref[...])
        dgate_ref[...] = dyy * hs * _gelu_grad(gate, t)
        dh_ref[...] = dyy * gl
        _accumulate_tn(acc_ref, dw_ref, gl * hs, dy_, n)

    return pl.pallas_call(
        body, name="odd_post_bwd", grid=(n,),
        in_specs=[_row(ts, D), _row(ts, D), _row(ts, D), _const(wout.shape)],
        out_specs=[_row(ts, D), _row(ts, D), _acc((D, D))],
        out_shape=[_sds((S, D), F32), _sds((S, D), F32), _sds((D, D), BF16)],
        scratch_shapes=[pltpu.VMEM((D, D), F32)], compiler_params=_cp())(dy, z, hseq, wout)


def odd_gates_bwd(xb, r, ig, lam_grad, hseq, keep, wr, wi, lam):
    S = xb.shape[0]
    ts = _tile_rows(S, 512)

    def body(xb_ref, r_ref, ig_ref, lg_ref, h_ref, hp_ref, keep_ref, wr_ref, wi_ref, lam_ref,
             dxb_ref, dcb_ref, dbr_ref, dbi_ref, dlam_ref, dwr_ref, dwi_ref):
        i = pl.program_id(0)

        @pl.when(i == 0)
        def _():
            for ref in (dcb_ref, dbr_ref, dbi_ref, dlam_ref, dwr_ref, dwi_ref):
                ref[...] = jnp.zeros_like(ref)

        first = (i > 0).astype(F32)
        xb, r, ig = xb_ref[...], r_ref[...], ig_ref[...]
        keep_ = keep_ref[...]
        lam_ = lam_ref[...]
        sp, a, mult = _decay(r, lam_)
        hs = h_ref[...]
        hprev = _roll(jnp.concatenate([hp_ref[...] * first, hs], axis=0), 1, 0)[CONV_HALO:]
        lg = lg_ref[...]
        da = lg * hprev * keep_
        ixb = ig * xb
        dmult = lg * ixb * keep_
        dixb = lg * jnp.where(keep_ > 0.0, mult, 1.0)
        dlog_a = da * a - dmult * jnp.where(mult > 0.0, a * a / mult, 0.0)
        dr = dlog_a * (-LRU_C * sp)
        dlam_ref[...] += _rowsum(dlog_a * (-LRU_C * r)) * (-jax.nn.sigmoid(-lam_))
        dpr = dr * r * (1.0 - r)
        dpi = dixb * xb * ig * (1.0 - ig)
        dbr_ref[...] += _rowsum(dpr)
        dbi_ref[...] += _rowsum(dpi)
        dxb = dixb * ig
        parts = []
        for h in range(LRU_HEADS):
            sl = slice(h * LRU_HEAD_DIM, (h + 1) * LRU_HEAD_DIM)
            dwr_ref[h] += _dot_tn(xb[:, sl], dpr[:, sl])
            dwi_ref[h] += _dot_tn(xb[:, sl], dpi[:, sl])
            parts.append(_dot_nt(dpr[:, sl], wr_ref[h]) + _dot_nt(dpi[:, sl], wi_ref[h]))
        dxb = dxb + jnp.concatenate(parts, axis=1)
        dxb_ref[...] = dxb
        dcb_ref[...] += _rowsum(dxb)

    ins = [xb, r, ig, lam_grad, hseq, hseq, keep, wr, wi, lam]
    in_specs = [_row(ts, D)] * 5 + [_prev(CONV_HALO, D, ts), _row(ts, 1)] + [_const(v.shape) for v in ins[7:]]
    gshape = (LRU_HEADS, LRU_HEAD_DIM, LRU_HEAD_DIM)
    return pl.pallas_call(
        body, name="odd_gates_bwd", grid=(S // ts,), in_specs=in_specs,
        out_specs=[_row(ts, D)] + [_acc((1, D))] * 4 + [_acc(gshape)] * 2,
        out_shape=[_sds((S, D), F32)] + [_sds((1, D), F32)] * 4 + [_sds(gshape, F32)] * 2,
        compiler_params=_cp())(*ins)


def odd_pre_bwd(x, dy, z, dxb, dgate, g, cw, win):
    S = x.shape[0]
    ts = _tile_rows(S, 512)
    n = S // ts

    def body(x_ref, dy_ref, xbp_ref, xbpp_ref, dxb_ref, dxbn_ref, dgate_ref, g_ref, cw_ref, win_ref,
             dx_ref, dcw_ref, dg_ref, dwin_ref, acc_ref):
        i = pl.program_id(0)

        @pl.when(i == 0)
        def _():
            dcw_ref[...] = jnp.zeros_like(dcw_ref)
            dg_ref[...] = jnp.zeros_like(dg_ref)

        dxb = dxb_ref[...]
        extd = jnp.concatenate([dxb, dxbn_ref[...] * (i < n - 1).astype(F32)], axis=0)
        extx = jnp.concatenate([xbpp_ref[...] * (i > 0).astype(F32), xbp_ref[...]], axis=0)
        dxbp = cw_ref[3:4, :] * dxb
        dcw_ref[3:4, :] += _rowsum(dxb * xbp_ref[...])
        for k in range(3):
            dxbp = dxbp + cw_ref[k:k + 1, :] * _roll(extd, -(3 - k), 0)[:ts]
            dcw_ref[k:k + 1, :] += _rowsum(dxb * _roll(extx, 3 - k, 0)[CONV_HALO:])
        dz = jnp.concatenate([dgate_ref[...], dxbp], axis=1).astype(BF16)
        x_ = x_ref[...]
        h, rstd = _rms(x_, g_ref[...])
        dxn, dgr = _rms_bwd(x_, g_ref[...], rstd, _dot_nt(dz, win_ref[...]))
        dx_ref[...] = dy_ref[...] + dxn
        dg_ref[...] += _rowsum(dgr)
        _accumulate_tn(acc_ref, dwin_ref, h, dz, n)

    ins = [x, dy, z, z, dxb, dxb, dgate, g, cw, win]
    in_specs = [_row(ts, D), _row(ts, D), _row(ts, D, 1), _prev(CONV_HALO, D, ts, 1), _row(ts, D),
                _next(CONV_HALO, D, ts, n), _row(ts, D)] + [_const(v.shape) for v in ins[7:]]
    return pl.pallas_call(
        body, name="odd_pre_bwd", grid=(n,), in_specs=in_specs,
        out_specs=[_row(ts, D), _acc((4, D)), _acc((1, D)), _acc((D, 2 * D))],
        out_shape=[_sds((S, D), F32), _sds((4, D), F32), _sds((1, D), F32), _sds((D, 2 * D), BF16)],
        scratch_shapes=[pltpu.VMEM((D, 2 * D), F32)], compiler_params=_cp())(*ins)


def loss_head(x, target, g):
    S = x.shape[0]
    ts = _tile_rows(S, 512)

    def body(x_ref, t_ref, g_ref, dx_ref, dg_ref, loss_ref):
        @pl.when(pl.program_id(0) == 0)
        def _():
            dg_ref[...] = jnp.zeros_like(dg_ref)
            loss_ref[...] = jnp.zeros_like(loss_ref)

        x_ = x_ref[...]
        y, rstd = _rms(x_, g_ref[...])
        err = y - t_ref[...]
        loss_ref[...] += 0.5 * _rowsum(jnp.mean(err * err, axis=1, keepdims=True))
        dxn, dgr = _rms_bwd(x_, g_ref[...], rstd, err * (1.0 / D))
        dx_ref[...] = dxn
        dg_ref[...] += _rowsum(dgr)

    return pl.pallas_call(
        body, name="loss_head", grid=(S // ts,), in_specs=[_row(ts, D), _row(ts, D), _const(g.shape)],
        out_specs=[_row(ts, D), _acc((1, D)), _acc((1, 1))],
        out_shape=[_sds((S, D), F32), _sds((1, D), F32), _sds((1, 1), F32)], compiler_params=_cp())(x, target, g)


def even_post_bwd(dy, ypool, o, wo_pool, wo_att):
    S = dy.shape[0]
    ts = _tile_rows(S, 512)
    n = S // ts

    def body(dy_ref, yp_ref, o_ref, wp_ref, wa_ref, dyp_ref, do_ref, delta_ref, dwp_ref, dwa_ref, accp_ref,
             acca_ref):
        dy_, o_ = dy_ref[...], o_ref[...]
        dyp_ref[...] = _dot_nt(dy_, wp_ref[...])
        do = _dot_nt(dy_, wa_ref[...])
        do_ref[...] = do.astype(BF16)
        prod = do * o_
        for h in range(MLA_HEADS):
            delta_ref[h] = _as_row(jnp.sum(prod[:, h * LANES:(h + 1) * LANES], axis=1, keepdims=True))
        _accumulate_tn(accp_ref, dwp_ref, yp_ref[...], dy_, n)
        _accumulate_tn(acca_ref, dwa_ref, o_, dy_, n)

    return pl.pallas_call(
        body, name="even_post_bwd", grid=(n,),
        in_specs=[_row(ts, D), _row(ts, POOL_DIM), _row(ts, D), _const(wo_pool.shape), _const(wo_att.shape)],
        out_specs=[_row(ts, POOL_DIM), _row(ts, D),
                   pl.BlockSpec((MLA_HEADS, None, 1, ts), lambda i: (0, i, 0, 0)), _acc((POOL_DIM, D)),
                   _acc((D, D))],
        out_shape=[_sds((S, POOL_DIM), F32), _sds((S, D), BF16), _sds((MLA_HEADS, n, 1, ts), F32),
                   _sds((POOL_DIM, D), BF16), _sds((D, D), BF16)],
        scratch_shapes=[pltpu.VMEM((POOL_DIM, D), F32), pltpu.VMEM((D, D), F32)],
        compiler_params=_cp())(dy, ypool, o, wo_pool, wo_att)


def attn_bwd(qp, kp, vp, do, lse_row, delta_row, token=None):
    S = qp.shape[0]
    tk = _tile_rows(S, 512)
    nq = S // tk
    extra, extra_specs = _after(token)

    def body(q_ref, k_ref, v_ref, do_ref, lse_ref, delta_ref, *rest):
        dq_ref, dk_ref, dv_ref = rest[-3:]
        kj = pl.program_id(1)

        @pl.when(kj == 0)
        def _():
            dq_ref[...] = jnp.zeros_like(dq_ref)

        k, v = k_ref[...], v_ref[...]

        def block(qi, carry, masked):
            dk, dv = carry
            off = pl.multiple_of(qi * tk, tk)
            q = q_ref[pl.ds(off, tk), :]
            do_ = do_ref[pl.ds(off, tk), :]
            st = _dot_nt(k, q)
            if masked:
                row = lax.broadcasted_iota(jnp.int32, (tk, tk), 0)
                col = lax.broadcasted_iota(jnp.int32, (tk, tk), 1)
                st = jnp.where(col >= row, st, -1e30)
            pt = _exp2(st - lse_ref[qi])
            dv = dv + _dot(pt, do_)
            dst = (pt * (_dot_nt(v, do_) - delta_ref[qi])).astype(BF16)
            dk = dk + _dot(dst, q)
            dq_ref[pl.ds(off, tk), :] += _dot_tn(dst, k)
            return dk, dv

        zero = jnp.zeros((tk, LANES), F32)
        carry = block(kj, (zero, zero), True)
        dk, dv = _pair_loop(kj + 1, nq, lambda qi, c: block(qi, c, False), carry, unrolls=(4, 2, 1))
        dk_ref[...] = dk * LN_2
        dv_ref[...] = dv

    blk = pl.BlockSpec((tk, LANES), lambda h, j: (j, h))
    full = pl.BlockSpec((S, LANES), lambda h, j: (0, h))
    rowv = pl.BlockSpec((None, nq, 1, tk), lambda h, j: (h, 0, 0, 0))
    return pl.pallas_call(
        body, name="attn_bwd", grid=(MLA_HEADS, nq), in_specs=[full, blk, blk, full, rowv, rowv] + extra_specs,
        out_specs=[full, blk, blk], out_shape=[_sds((S, D), F32)] * 3, compiler_params=_cp2())(
            qp, kp, vp, do, lse_row, delta_row, *extra)


def even_pre_bwd(x, dy, z, dq, dk, dv, dyp, tabs, g, win, pw, pscale, qg, wq, kvg, wk, wv):
    S = x.shape[0]
    ts = _tile_rows(S, 512)
    n = S // ts

    def body(x_ref, dy_ref, z_ref, up_ref, dq_ref, dk_ref, dv_ref, dyp_ref, dypn_ref, c_ref, a_ref, b_ref,
             g_ref, win_ref, pw_ref, ps_ref, qg_ref, wq_ref, kvg_ref, wk_ref, wv_ref,
             dx_ref, dg_ref, dpw_ref, dps_ref, dqg_ref, dwq_ref, dkvg_ref, dwk_ref, dwv_ref, dwin_ref, acc_ref):
        i = pl.program_id(0)

        @pl.when(i == 0)
        def _():
            for ref in (dg_ref, dpw_ref, dps_ref, dqg_ref, dwq_ref, dkvg_ref, dwk_ref, dwv_ref):
                ref[...] = jnp.zeros_like(ref)

        z = z_ref[...]
        c, a, b = c_ref[...], a_ref[...], b_ref[...]
        ps = ps_ref[...]
        u = z[:, :POOL_DIM]
        pooled = _pooled(up_ref[...] * (i > 0).astype(F32), u, i * ts)
        dyp_ = dyp_ref[...]
        dps_ref[...] += _rowsum(dyp_ * _blockdot(pooled, pw_ref, 4, LANES))
        ext = jnp.concatenate([dyp_, dypn_ref[...] * (i < n - 1).astype(F32)], axis=0) * ps
        for gidx in range(4):
            sl = slice(gidx * LANES, (gidx + 1) * LANES)
            dpw_ref[gidx] += _dot_tn(pooled[:, sl], ext[:ts, sl])
        dpooled = jnp.concatenate(
            [_dot_nt(ext[:, gidx * LANES:(gidx + 1) * LANES], pw_ref[gidx]) for gidx in range(4)], axis=1)
        dm = dpooled / _pool_cnt(i * ts, ts + POOL_HALO)
        du = _pool_windows(dm, -1)[:ts] - dpooled[:ts]
        cq = z[:, 512:768]
        cqn, rstd_q = _rms(cq, qg_ref[...])
        dqf = _rope_bwd(dq_ref[...] * ATTN_SCALE, c, a, b)
        dwq_ref[...] += _dot_tn(cqn, dqf)
        dcq, dqg_rows = _rms_bwd(cq, qg_ref[...], rstd_q, _dot_nt(dqf, wq_ref[...]))
        dqg_ref[...] += _rowsum(dqg_rows)
        ckv = z[:, 768:896]
        ckvn, rstd_kv = _rms(ckv, kvg_ref[...])
        dk_, dv_ = dk_ref[...], dv_ref[...]
        dwk_ref[...] += _dot_tn(ckvn, dk_)
        dwv_ref[...] += _dot_tn(ckvn, dv_)
        dckv, dkvg_rows = _rms_bwd(ckv, kvg_ref[...], rstd_kv,
                                   _dot_nt(dk_, wk_ref[...]) + _dot_nt(dv_, wv_ref[...]))
        dkvg_ref[...] += _rowsum(dkvg_rows)
        dkr = dk_[:, :LANES]
        for h in range(1, MLA_HEADS):
            dkr = dkr + dk_[:, h * LANES:(h + 1) * LANES]
        lane = lax.broadcasted_iota(jnp.int32, (ts, LANES), 1)
        dkr = jnp.where((lane >= 64) & (lane < 96), _rope_bwd(dkr, c, a, b), 0.0)
        dz = jnp.concatenate([du, dcq, dckv, dkr], axis=1).astype(BF16)
        x_ = x_ref[...]
        h, rstd = _rms(x_, g_ref[...])
        dxn, dgr = _rms_bwd(x_, g_ref[...], rstd, _dot_nt(dz, win_ref[...]))
        dx_ref[...] = dy_ref[...] + dxn
        dg_ref[...] += _rowsum(dgr)
        _accumulate_tn(acc_ref, dwin_ref, h, dz, n)

    ins = [x, dy, z, z, dq, dk, dv, dyp, dyp, *tabs, g, win, pw, pscale, qg, wq, kvg, wk, wv]
    in_specs = [_row(ts, D), _row(ts, D), _row(ts, D), _prev(POOL_HALO, POOL_DIM, ts), _row(ts, D), _row(ts, D),
                _row(ts, D), _row(ts, POOL_DIM), _next(POOL_HALO, POOL_DIM, ts, n), _row(ts, LANES),
                _row(ts, LANES), _row(ts, LANES)] + [_const(v.shape) for v in ins[12:]]
    acc_shapes = [(1, D), (4, LANES, LANES), (1, POOL_DIM), (1, Q_LORA), (Q_LORA, D), (1, KV_LORA), (KV_LORA, D),
                  (KV_LORA, D)]
    return pl.pallas_call(
        body, name="even_pre_bwd", grid=(n,), in_specs=in_specs,
        out_specs=[_row(ts, D)] + [_acc(s) for s in acc_shapes] + [_acc((D, D))],
        out_shape=[_sds((S, D), F32)] + [_sds(s, F32) for s in acc_shapes] + [_sds((D, D), BF16)],
        scratch_shapes=[pltpu.VMEM((D, D), F32)], compiler_params=_cp())(*ins)


def _pick(n, options):
    for o in options:
        if n % o == 0:
            return o
    return n


def matmul_tn(name, a, b):
    out_dtype = BF16
    S = a.shape[-2]
    ts = _tile_rows(S, 2048)
    steps = S // ts

    def body(a_ref, b_ref, o_ref, acc_ref):
        s = pl.program_id(2)

        @pl.when(s == 0)
        def _():
            acc_ref[...] = jnp.zeros_like(acc_ref)

        acc_ref[...] += _dot_tn(a_ref[...], b_ref[...])

        @pl.when(s == steps - 1)
        def _():
            o_ref[...] = acc_ref[...].astype(o_ref.dtype)

    if a.ndim == 3:
        C, _, K = a.shape
        N = b.shape[1]
        tn = _pick(N, (1024, 512, 256, 128))
        grid = (C, N // tn, S // ts)
        in_specs = [pl.BlockSpec((None, ts, K), lambda c, j, s: (c, s, 0)),
                    pl.BlockSpec((ts, tn), lambda c, j, s: (s, j))]
        out_spec, out_shape, tile = pl.BlockSpec((None, K, tn), lambda c, j, s: (c, 0, j)), (C, K, N), (K, tn)
    elif b.ndim == 3:
        C, _, N = b.shape
        K = a.shape[1]
        tk = _pick(K, (1024, 512, 256, 128))
        grid = (C, K // tk, S // ts)
        in_specs = [pl.BlockSpec((ts, tk), lambda c, i, s: (s, i)),
                    pl.BlockSpec((None, ts, N), lambda c, i, s: (c, s, 0))]
        out_spec, out_shape, tile = pl.BlockSpec((None, tk, N), lambda c, i, s: (c, i, 0)), (C, K, N), (tk, N)
    else:
        K, N = a.shape[1], b.shape[1]
        tk = _pick(K, (1024, 512, 256, 128))
        tn = _pick(N, (1024, 512, 256, 128))
        grid = (K // tk, N // tn, S // ts)
        in_specs = [pl.BlockSpec((ts, tk), lambda i, j, s: (s, i)), pl.BlockSpec((ts, tn), lambda i, j, s: (s, j))]
        out_spec, out_shape, tile = pl.BlockSpec((tk, tn), lambda i, j, s: (i, j)), (K, N), (tk, tn)
    return pl.pallas_call(
        body, name=name, grid=grid, in_specs=in_specs, out_specs=out_spec, out_shape=_sds(out_shape, out_dtype),
        scratch_shapes=[pltpu.VMEM(tile, F32)], compiler_params=pltpu.CompilerParams(dimension_semantics=("arbitrary",) * 3, vmem_limit_bytes=VMEM_LIMIT))(
            a, b)


def _my_id():
    return lax.axis_index("x") * 4 + lax.axis_index("y") * 2 + lax.axis_index("c")


def _peer(j):
    x, y, c = lax.axis_index("x"), lax.axis_index("y"), lax.axis_index("c")
    px = 1 - x if j & 4 else x
    py = 1 - y if j & 2 else y
    pc = 1 - c if j & 1 else c
    return (px, py, pc), px * 4 + py * 2 + pc


def all_gather(name, arrays):
    n = len(arrays)

    def body(*refs):
        ins, outs = refs[:n], refs[n:2 * n]
        send_sems, recv_sems, local_sems = refs[2 * n:]
        me = _my_id()
        local = [pltpu.make_async_copy(ins[k], outs[k].at[me], local_sems.at[k]) for k in range(n)]
        for cp in local:
            cp.start()
        sends = []
        for j in range(1, N_DEV):
            peer, _ = _peer(j)
            for k in range(n):
                cp = pltpu.make_async_remote_copy(
                    src_ref=ins[k], dst_ref=outs[k].at[me], send_sem=send_sems.at[k, j - 1],
                    recv_sem=recv_sems.at[k, j - 1], device_id=peer, device_id_type=pl.DeviceIdType.MESH)
                cp.start()
                sends.append(cp)
        for j in range(1, N_DEV):
            peer, pid = _peer(j)
            for k in range(n):
                pltpu.make_async_remote_copy(
                    src_ref=ins[k], dst_ref=outs[k].at[pid], send_sem=send_sems.at[k, j - 1],
                    recv_sem=recv_sems.at[k, j - 1], device_id=peer, device_id_type=pl.DeviceIdType.MESH).wait_recv()
        for cp in sends:
            cp.wait_send()
        for cp in local:
            cp.wait()

    any_spec = pl.BlockSpec(memory_space=pl.ANY)
    return pl.pallas_call(
        body, name=name, in_specs=[any_spec] * n, out_specs=[any_spec] * n,
        out_shape=[_sds((N_DEV,) + a.shape, a.dtype) for a in arrays],
        scratch_shapes=[pltpu.SemaphoreType.DMA((n, N_DEV - 1)), pltpu.SemaphoreType.DMA((n, N_DEV - 1)),
                        pltpu.SemaphoreType.DMA((n,))],
        compiler_params=pltpu.CompilerParams(has_side_effects=True))(*arrays)


def exchange(name, arrays, gathers=()):
    n_ex, n = len(arrays), len(arrays) + len(gathers)

    def body(*refs):
        ins, outs = refs[:n], refs[n:2 * n]
        send_sems, recv_sems, local_sems = refs[2 * n:]
        me = _my_id()

        def mine(k, slot):
            return ins[k].at[slot] if k < n_ex else ins[k]

        local = [pltpu.make_async_copy(mine(k, me), outs[k].at[me], local_sems.at[k]) for k in range(n)]
        for cp in local:
            cp.start()
        sends = []
        for j in range(1, N_DEV):
            peer, pid = _peer(j)
            for k in range(n):
                cp = pltpu.make_async_remote_copy(
                    src_ref=mine(k, pid), dst_ref=outs[k].at[me], send_sem=send_sems.at[k, j - 1],
                    recv_sem=recv_sems.at[k, j - 1], device_id=peer, device_id_type=pl.DeviceIdType.MESH)
                cp.start()
                sends.append(cp)
        for j in range(1, N_DEV):
            peer, pid = _peer(j)
            for k in range(n):
                pltpu.make_async_remote_copy(
                    src_ref=mine(k, me), dst_ref=outs[k].at[pid], send_sem=send_sems.at[k, j - 1],
                    recv_sem=recv_sems.at[k, j - 1], device_id=peer, device_id_type=pl.DeviceIdType.MESH).wait_recv()
        for cp in sends:
            cp.wait_send()
        for cp in local:
            cp.wait()

    any_spec = pl.BlockSpec(memory_space=pl.ANY)
    return pl.pallas_call(
        body, name=name, in_specs=[any_spec] * n, out_specs=[any_spec] * n,
        out_shape=[_sds(a.shape, a.dtype) for a in arrays] + [_sds((N_DEV,) + a.shape, a.dtype) for a in gathers],
        scratch_shapes=[pltpu.SemaphoreType.DMA((n, N_DEV - 1)), pltpu.SemaphoreType.DMA((n, N_DEV - 1)),
                        pltpu.SemaphoreType.DMA((n,))],
        compiler_params=pltpu.CompilerParams(has_side_effects=True))(*arrays, *gathers)


_HBM = pl.BlockSpec(memory_space=pltpu.HBM)
_SEM = pl.BlockSpec(memory_space=pltpu.SEMAPHORE)
_DATAFLOW = pltpu.SideEffectType.DATAFLOW_SIDE_EFFECTING


def _in_hbm(v):
    return pltpu.with_memory_space_constraint(v, pltpu.HBM)


N_PEERS = N_DEV - 1


def _split_copy(k, j, srcs, lands, send_sems, recv_sems, gather, slot):
    peer, pid = _peer(j)
    return pltpu.make_async_remote_copy(
        src_ref=srcs[k] if gather else srcs[k].at[pid], dst_ref=lands[k].at[_my_id() if slot == "mine" else pid],
        send_sem=send_sems[j - 1], recv_sem=recv_sems[j - 1], device_id=peer, device_id_type=pl.DeviceIdType.MESH)


def split_start(name, arrays, gather):
    n = len(arrays)
    lands = [lax.empty((N_DEV,) + a.shape if gather else a.shape, a.dtype) for a in arrays]

    def body(*refs):
        srcs, lnds = refs[:n], refs[n:2 * n]
        sems = refs[4 * n:4 * n + 2 * N_PEERS]
        token = refs[-1]
        for j in range(1, N_DEV):
            for k in range(n):
                _split_copy(k, j, srcs, lnds, sems[:N_PEERS], sems[N_PEERS:], gather, "mine").start()
        token[...] = jnp.zeros_like(token)

    out = pl.pallas_call(
        body, name=name,
        out_shape=(*[pltpu.HBM(a.shape, a.dtype) for a in arrays], *[pltpu.HBM(l.shape, l.dtype) for l in lands],
                   *[pltpu.SemaphoreType.DMA(())] * (2 * N_PEERS), _sds((8, LANES), F32)),
        in_specs=[_HBM] * (2 * n),
        out_specs=(*[_HBM] * (2 * n), *[_SEM] * (2 * N_PEERS), pl.BlockSpec(memory_space=pltpu.VMEM)),
        input_output_aliases={k: k for k in range(2 * n)},
        compiler_params=pltpu.CompilerParams(has_side_effects=_DATAFLOW))(
            *[_in_hbm(a) for a in arrays], *[_in_hbm(l) for l in lands])
    sems = list(out[2 * n:2 * n + 2 * N_PEERS])
    return sems[:N_PEERS], sems[N_PEERS:], list(out[:n]), list(out[n:2 * n]), out[-1]


def split_wait(name, handle, after, gather):
    send_sems, recv_sems, srcs, lands, _ = handle
    n = len(srcs)

    def body(*refs):
        srcs_r, lnds_r = refs[:n], refs[n:2 * n]
        sems = refs[2 * n:2 * n + 2 * N_PEERS]
        for j in range(1, N_DEV):
            for k in range(n):
                cp = _split_copy(k, j, srcs_r, lnds_r, sems[:N_PEERS], sems[N_PEERS:], gather, "peer")
                cp.wait_send()
                cp.wait_recv()

    out = pl.pallas_call(
        body, name=name, out_shape=tuple(pltpu.HBM(a.shape, a.dtype) for a in srcs + lands),
        in_specs=[_HBM] * (2 * n) + [_SEM] * (2 * N_PEERS) + [pl.BlockSpec(memory_space=pl.ANY)],
        out_specs=tuple([_HBM] * (2 * n)), input_output_aliases={k: k for k in range(2 * n)},
        compiler_params=pltpu.CompilerParams(has_side_effects=_DATAFLOW))(
            *srcs, *lands, *send_sems, *recv_sems, after)
    return list(out[:n]), list(out[n:])


def _fill_own_slot(src, land, gather):
    me = _my_id()
    own = src[None] if gather else lax.dynamic_index_in_dim(src, me, 0, keepdims=True)
    return lax.dynamic_update_slice_in_dim(land, own, me, 0)


ADAMW_BLOCK_ELEMS = 128 * 1024


def adamw(name, parts, w, m, v):
    R, C = w.shape
    tr = _pick(R, [t for t in (512, 256, 128, 64, 32, 16, 8) if t * C <= ADAMW_BLOCK_ELEMS])
    c1 = 1.0 - ADAM_B1 ** ADAM_STEP
    c2 = 1.0 - ADAM_B2 ** ADAM_STEP

    def body(p_ref, w_ref, m_ref, v_ref, g_ref, d_ref, nm_ref, nv_ref):
        g = p_ref[0].astype(F32)
        for s in range(1, N_DEV):
            g = g + p_ref[s].astype(F32)
        g_ref[...] = g
        m_ = ADAM_B1 * m_ref[...] + (1.0 - ADAM_B1) * g
        v_ = ADAM_B2 * v_ref[...] + (1.0 - ADAM_B2) * (g * g)
        nm_ref[...] = m_
        nv_ref[...] = v_
        d_ref[...] = -ADAM_LR * ((m_ / c1) / (jnp.sqrt(v_ / c2) + ADAM_EPS) + ADAM_WD * w_ref[...])

    row = pl.BlockSpec((tr, C), lambda i: (i, 0))
    return pl.pallas_call(
        body, name=name, grid=(R // tr,),
        in_specs=[pl.BlockSpec((N_DEV, tr, C), lambda i: (0, i, 0)), row, row, row], out_specs=[row] * 4,
        out_shape=[_sds((R, C), F32)] * 4, compiler_params=_cp())(parts, w, m, v)


WEIGHTS = ['ev_norm', 'ev_w_in', 'ev_pool_w', 'ev_pool_scale', 'ev_q_norm', 'ev_w_q_up', 'ev_kv_norm', 'ev_w_kv_up',
           'ev_w_out', 'od_norm', 'od_w_in', 'od_conv_w', 'od_conv_b', 'od_w_rgate', 'od_b_rgate', 'od_w_igate',
           'od_b_igate', 'od_lambda', 'od_w_out', 'xa_norm_x', 'xa_norm_mem', 'xa_w_q', 'xa_w_kv', 'xa_w_o',
           'ffn_norm', 'ffn_w_gate_up', 'ffn_w_down', 'final_norm']
SHARD_AXIS = {'ev_w_in': 1, 'ev_w_q_up': 2, 'ev_w_kv_up': 2, 'ev_w_out': 1, 'od_norm': 1, 'od_w_in': 2,
              'od_conv_w': 2, 'od_conv_b': 1, 'od_w_rgate': 2, 'od_b_rgate': 1, 'od_w_igate': 2, 'od_b_igate': 1,
              'od_lambda': 1, 'od_w_out': 1, 'xa_w_q': 1, 'xa_w_kv': 2, 'xa_w_o': 1, 'ffn_w_gate_up': 2,
              'ffn_w_down': 1}
SMALL_F32 = ('od_norm', 'od_conv_w', 'od_conv_b', 'od_b_rgate', 'od_b_igate', 'od_lambda')
STACKED = ('ffn_w_gate_up', 'ffn_w_down')
SHARDED = [n for n in WEIGHTS if n in SHARD_AXIS]
REPLICATED = [n for n in WEIGHTS if n not in SHARD_AXIS]
ROW_ALIGN = 512


def _pack(flats, dtype):
    v = jnp.concatenate([f.reshape(-1).astype(dtype) for f in flats])
    pad = (-v.shape[0]) % (ROW_ALIGN * LANES)
    return jnp.pad(v, (0, pad)).reshape(-1, LANES)


def _rows8(n_elems):
    return -(-n_elems // (8 * LANES)) * 8


def _pack_rows(arrays, lead=False):
    out = []
    for a in arrays:
        r = a.reshape((N_DEV, -1, LANES) if lead else (-1, LANES))
        pad = _rows8(r.shape[-2] * LANES) - r.shape[-2]
        out.append(jnp.pad(r, [(0, 0)] * (r.ndim - 2) + [(0, pad), (0, 0)]))
    return jnp.concatenate(out, axis=-2)


def _unpack_rows(buf, shapes, lead=False):
    out, off = [], 0
    for s in shapes:
        n = 1
        for d in s:
            n *= d
        rows = buf[..., off:off + n // LANES, :]
        out.append(rows.reshape(((N_DEV,) if lead else ()) + tuple(s)))
        off += _rows8(n)
    return out


def _unpack(flat, shapes):
    out, off = [], 0
    v = flat.reshape(-1)
    for s in shapes:
        n = 1
        for d in s:
            n *= d
        out.append(v[off:off + n].reshape(s))
        off += n
    return out


def _to_full(stacked, axis):
    v = jnp.moveaxis(stacked, 0, axis)
    s = v.shape
    return v.reshape(s[:axis] + (s[axis] * s[axis + 1],) + s[axis + 2:])


def _to_shards(full, axis):
    s = full.shape
    v = full.reshape(s[:axis] + (N_DEV, s[axis] // N_DEV) + s[axis + 1:])
    return jnp.moveaxis(v, axis, 0)


def _pad_heads(w, nh, dh, lead):
    s = w.shape
    v = w.reshape(s[:-1] + (nh, dh))
    v = jnp.pad(v, [(0, 0)] * (len(s) - 1) + [(0, 0), (lead, LANES - dh - lead)])
    return v.reshape(s[:-1] + (nh * LANES,))


def _unpad_heads(w, nh, dh, lead):
    s = w.shape
    return w.reshape(s[:-1] + (nh, LANES))[..., lead:lead + dh].reshape(s[:-1] + (nh * dh,))


def _rope_tables(positions):
    inv_freq = 10000.0 ** (-jnp.arange(0, 32, 2, dtype=F32) / 32)
    ang = positions.astype(F32)[:, None] * inv_freq
    cos, sin = jnp.cos(ang), jnp.sin(ang)
    S = positions.shape[0]
    one, zero = jnp.ones((S, 64), F32), jnp.zeros((S, 64), F32)
    z16, z32 = jnp.zeros((S, 16), F32), jnp.zeros((S, 32), F32)
    c = jnp.concatenate([one, cos, cos, jnp.ones((S, 32), F32)], axis=1)
    a = jnp.concatenate([zero, z16, sin, z32], axis=1)
    b = jnp.concatenate([zero, -sin, z16, z32], axis=1)
    return c, a, b


def _t(w):
    return jnp.swapaxes(w, -1, -2)


def device_step(x, mem, positions, target, W, fwd_token=None, late_weights=None, ship_grads=None):
    S = x.shape[0]
    G = {}
    tabs = _rope_tables(positions)
    keep = (positions != 0).astype(F32)[:, None]
    row = lambda v: v.reshape(1, -1)

    w_in = W['ev_w_in'][0]
    ev_win = jnp.concatenate([w_in[:, :896], _pad_heads(w_in[:, 896:], 1, 32, 64)], axis=1)
    ev_wq = _pad_heads(W['ev_w_q_up'][0], MLA_HEADS, QK_DIM, 0)
    kvw = W['ev_w_kv_up'][0].reshape(KV_LORA, MLA_HEADS, 128)
    ev_wk = _pad_heads(kvw[:, :, :64].reshape(KV_LORA, 512), MLA_HEADS, 64, 0)
    ev_wv = _pad_heads(kvw[:, :, 64:].reshape(KV_LORA, 512), MLA_HEADS, 64, 0)
    ev_wo_pool = W['ev_w_out'][0][:POOL_DIM]
    ev_wo_att = _t(_pad_heads(_t(W['ev_w_out'][0][POOL_DIM:]), MLA_HEADS, 64, 0))
    pw = W['ev_pool_w'][0].astype(BF16)
    ev_g, ps, qg, kvg = row(W['ev_norm'][0]), row(W['ev_pool_scale'][0]), row(W['ev_q_norm'][0]), row(W['ev_kv_norm'][0])

    z0, qp, kp, vp, ypool = even_pre(x, tabs, ev_g, ev_win, pw, ps, qg, ev_wq, kvg, ev_wk, ev_wv)
    o_att, lse = attn_fwd(qp, kp, vp, fwd_token)
    if late_weights is not None:
        W = {**W, **late_weights(lse)}
    x1 = even_post(x, ypool, o_att, ev_wo_pool, ev_wo_att)

    def xa_ffn_fwd(xin, l):
        mn, km, vm = mem_kv(mem, row(W['xa_norm_mem'][l]), W['xa_w_kv'][l])
        xm = xattn_fwd(xin, row(W['xa_norm_x'][l]), W['xa_w_q'][l], km, vm, W['xa_w_o'][l])
        xo, hf, gu = ffn_fwd(xm, row(W['ffn_norm'][l]), W['ffn_w_gate_up'], l,
                             W['ffn_w_down'][:, l].reshape(FF_HALF, FF_CHUNK, D))
        return xm, xo, (mn, km, vm, hf, gu)

    x2, x3, memkv0 = xa_ffn_fwd(x1, 0)

    od_g, lam = row(W['od_norm'][0]), row(W['od_lambda'][0])
    cw, cb = W['od_conv_w'][0], row(W['od_conv_b'][0])
    wr, wi = W['od_w_rgate'][0], W['od_w_igate'][0]
    br, bi = row(W['od_b_rgate'][0]), row(W['od_b_igate'][0])
    z1, a_t, b_t, xb1, r1, ig1 = odd_pre(x3, keep, od_g, W['od_w_in'][0], cw, cb, wr, br, wi, bi, lam)
    hseq = lru_scan(a_t, b_t)
    x4 = odd_post(x3, z1, hseq, W['od_w_out'][0])
    x5, x6, memkv1 = xa_ffn_fwd(x4, 1)

    dx, G['final_norm'], loss = loss_head(x6, target, row(W['final_norm']))
    G['final_norm'] = G['final_norm'].reshape(D)

    gnx, gnm, gwq, gwkv, gwo, gfn, gwgu, gwd = ([None, None] for _ in range(8))

    def xa_ffn_bwd(dy, xin, xm, memkv, l):
        mn, km, vm, hf, gu = memkv
        fg = row(W['ffn_norm'][l])
        act, dgu = ffn_bwd_a(dy, gu, W['ffn_w_down'][:, l].reshape(FF_HALF, FF_CHUNK, D))
        gwd[l] = matmul_tn("ffn_dwd", act, dy).reshape(N_DEV, D_FF // N_DEV, D)
        gwgu[l] = matmul_tn("ffn_dwgu", hf, dgu)
        dxm, dfg = ffn_bwd_b(xm, dy, dgu, fg, W['ffn_w_gate_up'], l)
        gfn[l] = dfg[0]
        dxin, o, dq, hx, dgx, dk, dv = xattn_bwd(xin, dxm, row(W['xa_norm_x'][l]), W['xa_w_q'][l], km, vm,
                                                  W['xa_w_o'][l])
        gnx[l] = dgx[0]
        gwo[l] = matmul_tn("xa_dwo", o, dxm)
        gwq[l] = matmul_tn("xa_dwq", hx, dq)
        dkv, dgm = mem_bwd(mem, row(W['xa_norm_mem'][l]), dk, dv, W['xa_w_kv'][l])
        gnm[l] = dgm[0]
        gwkv[l] = matmul_tn("xa_dwkv", mn, dkv)
        return dxin

    dx4 = xa_ffn_bwd(dx, x4, x5, memkv1, 1)

    dgate, dhs, g_od_wout = odd_post_bwd(dx4, z1, hseq, W['od_w_out'][0])
    G['od_w_out'] = g_od_wout[None]
    lam_grad = lru_scan(a_t, dhs, reverse=True)
    dxb, dcb, dbr, dbi, dlam, dwr, dwi = odd_gates_bwd(xb1, r1, ig1, lam_grad, hseq, keep, wr, wi, lam)
    dx3, dcw, dg_od, g_od_win = odd_pre_bwd(x3, dx4, z1, dxb, dgate, od_g, cw, W['od_w_in'][0])
    G['od_w_in'] = g_od_win[None]
    G['od_norm'], G['od_conv_w'], G['od_conv_b'] = dg_od, dcw[None], dcb
    G['od_w_rgate'], G['od_b_rgate'], G['od_w_igate'], G['od_b_igate'], G['od_lambda'] = (
        dwr[None], dbr, dwi[None], dbi, dlam)

    dx1 = xa_ffn_bwd(dx3, x1, x2, memkv0, 0)
    G['xa_norm_x'], G['xa_norm_mem'], G['ffn_norm'] = jnp.stack(gnx), jnp.stack(gnm), jnp.stack(gfn)
    G['xa_w_q'], G['xa_w_kv'], G['xa_w_o'] = jnp.stack(gwq), jnp.stack(gwkv), jnp.stack(gwo)
    G['ffn_w_gate_up'], G['ffn_w_down'] = jnp.stack(gwgu, axis=1), jnp.stack(gwd, axis=1)
    bwd_token = ship_grads(G) if ship_grads is not None else None

    dyp, do_att, delta, g_wo_pool, g_wo_att = even_post_bwd(dx1, ypool, o_att, ev_wo_pool, ev_wo_att)
    G['ev_w_out'] = jnp.concatenate([g_wo_pool, _t(_unpad_heads(_t(g_wo_att), MLA_HEADS, 64, 0))], axis=0)[None]
    dq, dk, dv = attn_bwd(qp, kp, vp, do_att, lse, delta, bwd_token)
    (grad_x, dg_ev, dpw, dps, dqg, dwq, dkvg, dwk, dwv, g_win) = even_pre_bwd(
        x, dx1, z0, dq, dk, dv, dyp, tabs, ev_g, ev_win, pw, ps, qg, ev_wq, kvg, ev_wk, ev_wv)
    G['ev_w_in'] = jnp.concatenate([g_win[:, :896], _unpad_heads(g_win[:, 896:], 1, 32, 64)], axis=1)[None]
    G['ev_norm'], G['ev_pool_w'], G['ev_pool_scale'], G['ev_q_norm'], G['ev_kv_norm'] = (
        dg_ev, dpw[None], dps, dqg, dkvg)
    G['ev_w_q_up'] = _unpad_heads(dwq, MLA_HEADS, QK_DIM, 0)[None]
    gk = _unpad_heads(dwk, MLA_HEADS, 64, 0).reshape(KV_LORA, MLA_HEADS, 64)
    gv = _unpad_heads(dwv, MLA_HEADS, 64, 0).reshape(KV_LORA, MLA_HEADS, 64)
    G['ev_w_kv_up'] = jnp.concatenate([gk, gv], axis=2).reshape(1, KV_LORA, MLA_HEADS * 128)
    return loss[0, 0], grad_x, G


def kernel(x, mem, positions, ev_norm, ev_w_in, ev_pool_w, ev_pool_scale, ev_q_norm, ev_w_q_up, ev_kv_norm, ev_w_kv_up, ev_w_out, od_norm, od_w_in, od_conv_w, od_conv_b, od_w_rgate, od_b_rgate, od_w_igate, od_b_igate, od_lambda, od_w_out, xa_norm_x, xa_norm_mem, xa_w_q, xa_w_kv, xa_w_o, ffn_norm, ffn_w_gate_up, ffn_w_down, final_norm, loss_target, m_ev_norm, m_ev_w_in, m_ev_pool_w, m_ev_pool_scale, m_ev_q_norm, m_ev_w_q_up, m_ev_kv_norm, m_ev_w_kv_up, m_ev_w_out, m_od_norm, m_od_w_in, m_od_conv_w, m_od_conv_b, m_od_w_rgate, m_od_b_rgate, m_od_w_igate, m_od_b_igate, m_od_lambda, m_od_w_out, m_xa_norm_x, m_xa_norm_mem, m_xa_w_q, m_xa_w_kv, m_xa_w_o, m_ffn_norm, m_ffn_w_gate_up, m_ffn_w_down, m_final_norm, v_ev_norm, v_ev_w_in, v_ev_pool_w, v_ev_pool_scale, v_ev_q_norm, v_ev_w_q_up, v_ev_kv_norm, v_ev_w_kv_up, v_ev_w_out, v_od_norm, v_od_w_in, v_od_conv_w, v_od_conv_b, v_od_w_rgate, v_od_b_rgate, v_od_w_igate, v_od_b_igate, v_od_lambda, v_od_w_out, v_xa_norm_x, v_xa_norm_mem, v_xa_w_q, v_xa_w_kv, v_xa_w_o, v_ffn_norm, v_ffn_w_gate_up, v_ffn_w_down, v_final_norm):
    args = dict(locals())
    w = {n: args[n] for n in WEIGHTS}
    m = {n: args['m_' + n] for n in WEIGHTS}
    v = {n: args['v_' + n] for n in WEIGHTS}
    big = [n for n in SHARDED if n not in SMALL_F32]
    small = [n for n in SHARDED if n in SMALL_F32]

    small_shapes = [w[n].shape for n in small]
    first = [n for n in big if n.startswith('ev_')]
    late = [n for n in big if n not in first]

    def full(n, st):
        return st if n in STACKED else _to_full(st, SHARD_AXIS[n])

    W = {n: w[n] for n in REPLICATED}
    W.update((n, full(n, st)) for n, st in zip(first, all_gather("gather_ev_weights", [w[n].astype(BF16) for n in first])))
    gather = split_start("gather_start", [w[n].astype(BF16) for n in late] + [_pack_rows([w[n] for n in small])], True)

    def late_weights(after):
        srcs, lands = split_wait("gather_wait", gather, after, True)
        lands = [_fill_own_slot(s, l, True) for s, l in zip(srcs, lands)]
        out = {n: full(n, st) for n, st in zip(late, lands)}
        out.update((n, _to_full(st, SHARD_AXIS[n])) for n, st in zip(small, _unpack_rows(lands[-1], small_shapes, True)))
        return out

    def shards(G, n):
        return G[n] if n in STACKED else _to_shards(G[n], SHARD_AXIS[n])

    shipped = []

    def ship_grads(G):
        shipped.append(split_start("exchange_start", [shards(G, n).astype(BF16) for n in late] +
                                   [_pack_rows([shards(G, n) for n in small], lead=True)], False))
        return shipped[0][-1]

    loss, grad_x, G = device_step(x[0], mem[0], positions[0], loss_target[0], W, gather[-1], late_weights, ship_grads)
    outs = [{}, {}, {}, {}]

    rep_shapes = [w[n].shape for n in REPLICATED] + [(LANES,)]
    zero = jnp.zeros((LANES,), F32)
    *first_parts, rep_parts = exchange(
        "exchange_ev_and_rep_grads", [shards(G, n).astype(BF16) for n in first],
        [_pack([G[n] for n in REPLICATED] + [jnp.broadcast_to(loss, (LANES,))], F32)])
    rep = adamw("adamw_rep", rep_parts, *[_pack([d[n] for n in REPLICATED] + [zero], F32) for d in (w, m, v)])
    for k in range(4):
        outs[k].update(zip(REPLICATED + ['loss'], _unpack(rep[k], rep_shapes)))
    loss = outs[0]['loss'][0]

    srcs, lands = split_wait("exchange_wait", shipped[0], grad_x, False)
    late_parts = [_fill_own_slot(s, l, False) for s, l in zip(srcs, lands)]
    parts = first_parts + late_parts
    two_d = lambda a: a.reshape(-1, a.shape[-1])
    for n, p in zip(first + late, parts):
        res = adamw("adamw_" + n, p.reshape((N_DEV,) + two_d(w[n]).shape), two_d(w[n]), two_d(m[n]), two_d(v[n]))
        for k in range(4):
            outs[k][n] = res[k].reshape(w[n].shape)
    res = adamw("adamw_small", parts[-1], *[_pack_rows([d[n] for n in small]) for d in (w, m, v)])
    for k in range(4):
        outs[k].update(zip(small, _unpack_rows(res[k], small_shapes)))

    return (loss, grad_x[None], *[outs[0][n] for n in WEIGHTS], *[outs[1][n] for n in WEIGHTS],
            *[outs[2][n] for n in WEIGHTS], *[outs[3][n] for n in WEIGHTS])
```

```python
import functools

import jax
import jax.numpy as jnp
from jax import lax
from jax.experimental import pallas as pl
from jax.experimental.pallas import tpu as pltpu

F32, BF16 = jnp.float32, jnp.bfloat16
N_DEV = 8
D = 1024
POOL_DIM = 512
POOL_WINDOWS = (2, 4, 8, 16)
MLA_HEADS = 8
QK_DIM = 96
Q_LORA, KV_LORA = 256, 128
LRU_HEADS, LRU_HEAD_DIM = 4, 256
LRU_C = 8.0
MEM_HEADS, MEM_HEAD_DIM = 4, 256
D_FF = 2816
RMS_EPS = 1e-6
ADAM_LR, ADAM_B1, ADAM_B2, ADAM_EPS, ADAM_WD, ADAM_STEP = 0.001, 0.9, 0.999, 1e-08, 0.01, 10
LANES = 128
POOL_HALO = 16
CONV_HALO = 8
VMEM_LIMIT = 60000 * 1024


def _cp():
    return pltpu.CompilerParams(dimension_semantics=("arbitrary",), vmem_limit_bytes=VMEM_LIMIT)


def _cp2():
    return pltpu.CompilerParams(dimension_semantics=("arbitrary", "arbitrary"), vmem_limit_bytes=VMEM_LIMIT)


def _row(ts, c, col=0):
    return pl.BlockSpec((ts, c), lambda i: (i, col))


def _prev(hr, c, ts, col=0):
    r = ts // hr
    return pl.BlockSpec((hr, c), lambda i: (jnp.maximum(i * r - 1, 0), col))


def _next(hr, c, ts, n, col=0):
    r = ts // hr
    return pl.BlockSpec((hr, c), lambda i: (jnp.minimum((i + 1) * r, n * r - 1), col))


def _const(shape):
    nd = len(shape)
    return pl.BlockSpec(tuple(shape), lambda i: (0,) * nd, pipeline_mode=pl.Buffered(1))


def _acc(shape):
    nd = len(shape)
    return pl.BlockSpec(tuple(shape), lambda i: (0,) * nd)


def _sds(shape, dt):
    return jax.ShapeDtypeStruct(tuple(shape), dt)


def _dot(a, b):
    return jnp.dot(a.astype(BF16), b.astype(BF16), preferred_element_type=F32)


def _dot_nt(a, b):
    return lax.dot_general(a.astype(BF16), b.astype(BF16), (((1,), (1,)), ((), ())), preferred_element_type=F32)


def _dot_tn(a, b):
    return lax.dot_general(a.astype(BF16), b.astype(BF16), (((0,), (0,)), ((), ())), preferred_element_type=F32)


def _rms(x, g):
    rstd = lax.rsqrt(jnp.mean(x * x, axis=-1, keepdims=True) + RMS_EPS)
    return x * rstd * g, rstd


def _rms_bwd(x, g, rstd, dy):
    xn = x * rstd
    dyg = dy * g
    dx = rstd * (dyg - xn * jnp.mean(dyg * xn, axis=-1, keepdims=True))
    return dx, dy * xn


def _rowsum(v):
    return jnp.sum(v, axis=0, keepdims=True)


def _roll(v, s, axis):
    n = v.shape[axis]
    return pltpu.roll(v, s % n, axis)


def _rope(t, c, a, b):
    k = t.shape[1] // LANES
    if k > 1:
        c, a, b = (jnp.tile(v, (1, k)) for v in (c, a, b))
    return t * c + _roll(t, 16, 1) * a + _roll(t, -16, 1) * b


def _rope_bwd(d, c, a, b):
    k = d.shape[1] // LANES
    if k > 1:
        c, a, b = (jnp.tile(v, (1, k)) for v in (c, a, b))
    return d * c + _roll(d * a, -16, 1) + _roll(d * b, 16, 1)


def _gelu(x):
    c = 0.7978845608028654
    t = jnp.tanh(c * (x + 0.044715 * x * x * x))
    return 0.5 * x * (1.0 + t), t


def _gelu_grad(x, t):
    c = 0.7978845608028654
    return 0.5 * (1.0 + t) + 0.5 * x * (1.0 - t * t) * c * (1.0 + 3.0 * 0.044715 * x * x)


def _blockdot(v, w_ref, nblk, width):
    return jnp.concatenate(
        [_dot(v[:, j * width:(j + 1) * width], w_ref[j]) for j in range(nblk)], axis=1)


def _pool_cnt(row0, rows):
    t = row0 + lax.broadcasted_iota(jnp.int32, (rows, POOL_DIM), 0)
    w = jnp.left_shift(2, lax.broadcasted_iota(jnp.int32, (rows, POOL_DIM), 1) // LANES)
    return jnp.minimum(t + 1, w).astype(F32)


def _pool_windows(ext, sign):
    s2 = ext + _roll(ext, sign * 1, 0)
    t = s2[:, LANES:]
    s4 = t + _roll(t, sign * 2, 0)
    t = s4[:, LANES:]
    s8 = t + _roll(t, sign * 4, 0)
    t = s8[:, LANES:]
    s16 = t + _roll(t, sign * 8, 0)
    return jnp.concatenate([s2[:, :LANES], s4[:, :LANES], s8[:, :LANES], s16], axis=1)


def _pooled(uprev, u, row0):
    ts = u.shape[0]
    ext = jnp.concatenate([uprev, u], axis=0)
    sums = _pool_windows(ext, 1)[POOL_HALO:]
    return sums / _pool_cnt(row0, ts) - u


def _expm1(x):
    return jnp.where(jnp.abs(x) < 0.01, x * (1.0 + 0.5 * x * (1.0 + x * (1.0 / 3.0))), jnp.exp(x) - 1.0)


def _softplus(z):
    return jnp.maximum(z, 0.0) + jnp.log1p(jnp.exp(-jnp.abs(z)))


def _tile_rows(s, want):
    while s % want:
        want //= 2
    return want


def even_pre(x, tabs, g, win, pw, pscale, qg, wq, kvg, wk, wv):
    S = x.shape[0]
    ts = _tile_rows(S, 512)

    def body(x_ref, xp_ref, c_ref, a_ref, b_ref, g_ref, win_ref, pw_ref, ps_ref, qg_ref, wq_ref, kvg_ref,
             wk_ref, wv_ref, z_ref, q_ref, k_ref, v_ref, yp_ref):
        i = pl.program_id(0)
        h, _ = _rms(x_ref[...], g_ref[...])
        z = _dot(h, win_ref[...])
        z_ref[...] = z
        hp, _ = _rms(xp_ref[...], g_ref[...])
        uprev = _dot(hp, win_ref[:, :POOL_DIM]) * (i > 0).astype(F32)
        u = z[:, :POOL_DIM]
        pooled = _pooled(uprev, u, i * ts)
        yp_ref[...] = (_blockdot(pooled, pw_ref, 4, LANES) * ps_ref[...]).astype(BF16)
        c, a, b = c_ref[...], a_ref[...], b_ref[...]
        cqn, _ = _rms(z[:, 512:768], qg_ref[...])
        q_ref[...] = (_rope(_dot(cqn, wq_ref[...]), c, a, b) * (ATTN_SCALE * LOG2_E)).astype(BF16)
        ckvn, _ = _rms(z[:, 768:896], kvg_ref[...])
        krr = _rope(z[:, 896:1024], c, a, b)
        k_ref[...] = (_dot(ckvn, wk_ref[...]) + jnp.tile(krr, (1, MLA_HEADS))).astype(BF16)
        lane = lax.broadcasted_iota(jnp.int32, (ts, D), 1) % LANES
        v_ref[...] = jnp.where(lane == ONES_LANE, 1.0, _dot(ckvn, wv_ref[...])).astype(BF16)

    ins = [x, x, *tabs, g, win, pw, pscale, qg, wq, kvg, wk, wv]
    in_specs = [_row(ts, D), _prev(POOL_HALO, D, ts), _row(ts, LANES), _row(ts, LANES), _row(ts, LANES)]
    in_specs += [_const(v.shape) for v in ins[5:]]
    return pl.pallas_call(
        body, name="even_pre", grid=(S // ts,), in_specs=in_specs,
        out_specs=[_row(ts, D)] * 4 + [_row(ts, POOL_DIM)],
        out_shape=[_sds((S, D), F32)] + [_sds((S, D), BF16)] * 3 + [_sds((S, POOL_DIM), BF16)],
        compiler_params=_cp())(*ins)


ATTN_SCALE = QK_DIM ** -0.5
LOG2_E = 1.4426950408889634
LN_2 = 0.6931471805599453
ONES_LANE = 64


def _exp2(x):
    return jnp.exp2(x)


def _pair_loop(lo, hi, step, init, unrolls=(2, 1)):
    carry = init
    for unroll in unrolls:
        groups = (hi - lo) // unroll

        def group(j, c, lo=lo, unroll=unroll):
            for u in range(unroll):
                c = step(lo + unroll * j + u, c)
            return c

        carry = lax.fori_loop(0, groups, group, carry)
        lo = lo + unroll * groups
    return carry


def _as_row(col):
    return jnp.transpose(jnp.broadcast_to(col, (col.shape[0], LANES)))[0:1, :]


def _after(token):
    return ([], []) if token is None else ([token], [pl.BlockSpec(memory_space=pl.ANY)])


def attn_fwd(qp, kp, vp, token=None):
    S = qp.shape[0]
    tq = _tile_rows(S, 512)
    extra, extra_specs = _after(token)

    def body(q_ref, k_ref, v_ref, *rest):
        o_ref, lse_ref = rest[-2:]
        qi = pl.program_id(1)
        q = q_ref[...]

        def block(ki, carry, masked):
            m, acc = carry
            off = pl.multiple_of(ki * tq, tq)
            s = _dot_nt(q, k_ref[pl.ds(off, tq), :])
            if masked:
                row = lax.broadcasted_iota(jnp.int32, (tq, tq), 0)
                col = lax.broadcasted_iota(jnp.int32, (tq, tq), 1)
                s = jnp.where(col <= row, s, -1e30)
            m_new = jnp.maximum(m, jnp.max(s, axis=1, keepdims=True))
            acc = _exp2(m - m_new) * acc + _dot(_exp2(s - m_new), v_ref[pl.ds(off, tq), :])
            return m_new, acc

        init = (jnp.full((tq, 1), -1e30, F32), jnp.zeros((tq, LANES), F32))
        carry = _pair_loop(0, qi, lambda ki, c: block(ki, c, False), init, unrolls=(8, 4, 2, 1))
        m, acc = block(qi, carry, True)
        l = acc[:, ONES_LANE:ONES_LANE + 1]
        o_ref[...] = acc / l
        lse_ref[...] = _as_row(m + jnp.log(l) * LOG2_E)

    blk = pl.BlockSpec((tq, LANES), lambda h, i: (i, h))
    full = pl.BlockSpec((S, LANES), lambda h, i: (0, h))
    return pl.pallas_call(
        body, name="attn_fwd", grid=(MLA_HEADS, S // tq), in_specs=[blk, full, full] + extra_specs,
        out_specs=[blk, pl.BlockSpec((None, None, 1, tq), lambda h, i: (h, i, 0, 0))],
        out_shape=[_sds((S, D), F32), _sds((MLA_HEADS, S // tq, 1, tq), F32)], compiler_params=_cp2())(
            qp, kp, vp, *extra)


def even_post(x, ypool, o, wo_pool, wo_att):
    S = x.shape[0]
    ts = _tile_rows(S, 512)

    def body(x_ref, yp_ref, o_ref, wp_ref, wa_ref, out_ref):
        out_ref[...] = x_ref[...] + _dot(yp_ref[...], wp_ref[...]) + _dot(o_ref[...], wa_ref[...])

    return pl.pallas_call(
        body, name="even_post", grid=(S // ts,),
        in_specs=[_row(ts, D), _row(ts, POOL_DIM), _row(ts, D), _const(wo_pool.shape), _const(wo_att.shape)],
        out_specs=_row(ts, D), out_shape=_sds((S, D), F32), compiler_params=_cp())(x, ypool, o, wo_pool, wo_att)


def mem_kv(mem, g, wkv):
    M = mem.shape[0]

    def body(mem_ref, g_ref, w_ref, mn_ref, k_ref, v_ref):
        mn, _ = _rms(mem_ref[...], g_ref[...])
        mn_ref[...] = mn.astype(BF16)
        k_ref[...] = _dot(mn, w_ref[:, :D]).astype(BF16)
        v_ref[...] = _dot(mn, w_ref[:, D:]).astype(BF16)

    return pl.pallas_call(
        body, name="mem_kv", grid=(1,), in_specs=[_acc(mem.shape), _acc(g.shape), _acc(wkv.shape)],
        out_specs=[_acc((M, D))] * 3, out_shape=[_sds((M, D), BF16)] * 3, compiler_params=_cp())(mem, g, wkv)


def _xattn_heads(hx, wq_ref, k_ref, v_ref):
    q = _dot(hx, wq_ref[...])
    scale = MEM_HEAD_DIM ** -0.5
    ps, os_ = [], []
    for h in range(MEM_HEADS):
        sl = slice(h * MEM_HEAD_DIM, (h + 1) * MEM_HEAD_DIM)
        s = _dot_nt(q[:, sl], k_ref[:, sl]) * scale
        e = jnp.exp(s - jnp.max(s, axis=1, keepdims=True))
        p = e / jnp.sum(e, axis=1, keepdims=True)
        ps.append(p)
        os_.append(_dot(p, v_ref[:, sl]))
    return q, ps, jnp.concatenate(os_, axis=1)


def xattn_fwd(x, g, wq, kmem, vmem, wo):
    S = x.shape[0]
    ts = _tile_rows(S, 512)

    def body(x_ref, g_ref, wq_ref, k_ref, v_ref, wo_ref, out_ref):
        x_ = x_ref[...]
        hx, _ = _rms(x_, g_ref[...])
        _, _, o = _xattn_heads(hx, wq_ref, k_ref, v_ref)
        out_ref[...] = x_ + _dot(o, wo_ref[...])

    ins = [x, g, wq, kmem, vmem, wo]
    return pl.pallas_call(
        body, name="xattn_fwd", grid=(S // ts,), in_specs=[_row(ts, D)] + [_const(v.shape) for v in ins[1:]],
        out_specs=_row(ts, D), out_shape=_sds((S, D), F32), compiler_params=_cp())(*ins)


def xattn_bwd(x, dy, g, wq, kmem, vmem, wo):
    S = x.shape[0]
    M = kmem.shape[0]
    ts = _tile_rows(S, 512)
    scale = MEM_HEAD_DIM ** -0.5

    def body(x_ref, dy_ref, g_ref, wq_ref, k_ref, v_ref, wo_ref,
             dx_ref, o_ref, dq_ref, hx_ref, dg_ref, dk_ref, dv_ref):
        i = pl.program_id(0)

        @pl.when(i == 0)
        def _():
            dg_ref[...] = jnp.zeros_like(dg_ref)
            dk_ref[...] = jnp.zeros_like(dk_ref)
            dv_ref[...] = jnp.zeros_like(dv_ref)

        x_, dy_ = x_ref[...], dy_ref[...]
        hx, rstd = _rms(x_, g_ref[...])
        q, ps, o = _xattn_heads(hx, wq_ref, k_ref, v_ref)
        hx_ref[...] = hx.astype(BF16)
        o_ref[...] = o.astype(BF16)
        do = _dot_nt(dy_, wo_ref[...])
        dqs = []
        for h in range(MEM_HEADS):
            sl = slice(h * MEM_HEAD_DIM, (h + 1) * MEM_HEAD_DIM)
            p, do_h = ps[h], do[:, sl]
            dp = _dot_nt(do_h, v_ref[:, sl])
            ds = p * (dp - jnp.sum(p * dp, axis=1, keepdims=True)) * scale
            dqs.append(_dot(ds, k_ref[:, sl]))
            dk_ref[:, sl] += _dot_tn(ds, q[:, sl])
            dv_ref[:, sl] += _dot_tn(p, do_h)
        dq = jnp.concatenate(dqs, axis=1).astype(BF16)
        dq_ref[...] = dq
        dxn, dgr = _rms_bwd(x_, g_ref[...], rstd, _dot_nt(dq, wq_ref[...]))
        dx_ref[...] = dy_ + dxn
        dg_ref[...] += _rowsum(dgr)

    ins = [x, dy, g, wq, kmem, vmem, wo]
    return pl.pallas_call(
        body, name="xattn_bwd", grid=(S // ts,),
        in_specs=[_row(ts, D), _row(ts, D)] + [_const(v.shape) for v in ins[2:]],
        out_specs=[_row(ts, D)] * 4 + [_acc((1, D)), _acc((M, D)), _acc((M, D))],
        out_shape=[_sds((S, D), F32)] + [_sds((S, D), BF16)] * 3 + [_sds((1, D), F32), _sds((M, D), F32),
                                                                    _sds((M, D), F32)],
        compiler_params=_cp())(*ins)


def mem_bwd(mem, g, dk, dv, wkv):
    M = mem.shape[0]

    def body(mem_ref, g_ref, dk_ref, dv_ref, w_ref, dkv_ref, dg_ref):
        dkv = jnp.concatenate([dk_ref[...], dv_ref[...]], axis=1)
        dkv_ref[...] = dkv.astype(BF16)
        _, rstd = _rms(mem_ref[...], g_ref[...])
        dg_ref[...] = _rowsum(_dot_nt(dkv, w_ref[...]) * (mem_ref[...] * rstd))

    ins = [mem, g, dk, dv, wkv]
    return pl.pallas_call(
        body, name="mem_bwd", grid=(1,), in_specs=[_acc(v.shape) for v in ins],
        out_specs=[_acc((M, 2 * D)), _acc((1, D))], out_shape=[_sds((M, 2 * D), BF16), _sds((1, D), F32)],
        compiler_params=_cp())(*ins)


FF_CHUNK = 2 * D_FF // N_DEV
FF_HALF = N_DEV // 2


def _layer_of(w, layer):
    return pl.BlockSpec((N_DEV, None) + w.shape[2:], lambda i: (0, layer, 0, 0), pipeline_mode=pl.Buffered(1))


def _ff_chunks(c, ts):
    return pl.BlockSpec((c, ts, FF_CHUNK), lambda i: (0, i, 0))


def ffn_fwd(x, g, wgu, layer, wd):
    S = x.shape[0]
    ts = _tile_rows(S, 256)

    def body(x_ref, g_ref, wgu_ref, wd_ref, out_ref, hf_ref, gu_ref):
        x_ = x_ref[...]
        hf = _rms(x_, g_ref[...])[0].astype(BF16)
        hf_ref[...] = hf
        out = x_
        for j in range(FF_HALF):
            gg, uu = _dot(hf, wgu_ref[j]), _dot(hf, wgu_ref[j + FF_HALF])
            gu_ref[j] = gg.astype(BF16)
            gu_ref[j + FF_HALF] = uu.astype(BF16)
            out = out + _dot(gg * jax.nn.sigmoid(gg) * uu, wd_ref[j])
        out_ref[...] = out

    return pl.pallas_call(
        body, name="ffn_fwd", grid=(S // ts,),
        in_specs=[_row(ts, D), _const(g.shape), _layer_of(wgu, layer), _const(wd.shape)],
        out_specs=[_row(ts, D), _row(ts, D), _ff_chunks(N_DEV, ts)],
        out_shape=[_sds((S, D), F32), _sds((S, D), BF16), _sds((N_DEV, S, FF_CHUNK), BF16)],
        compiler_params=_cp())(x, g, wgu, wd)


def ffn_bwd(x, dy, gu, g, wgu, layer, wd):
    S = x.shape[0]
    ts = _tile_rows(S, 256)

    def body(x_ref, dy_ref, gu_ref, g_ref, wgu_ref, wd_ref, dx_ref, dg_ref, act_ref, dgu_ref):
        @pl.when(pl.program_id(0) == 0)
        def _():
            dg_ref[...] = jnp.zeros_like(dg_ref)

        dy_ = dy_ref[...]
        dyb = dy_.astype(BF16)
        dh = jnp.zeros((ts, D), F32)
        for j in range(FF_HALF):
            gg, uu = gu_ref[j].astype(F32), gu_ref[j + FF_HALF].astype(F32)
            sg = jax.nn.sigmoid(gg)
            silu = gg * sg
            act_ref[j] = (silu * uu).astype(BF16)
            dact = _dot_nt(dyb, wd_ref[j])
            dgate = (dact * uu * (sg * (1.0 + gg * (1.0 - sg)))).astype(BF16)
            dup = (dact * silu).astype(BF16)
            dgu_ref[j] = dgate
            dgu_ref[j + FF_HALF] = dup
            dh = dh + _dot_nt(dgate, wgu_ref[j]) + _dot_nt(dup, wgu_ref[j + FF_HALF])
        x_ = x_ref[...]
        _, rstd = _rms(x_, g_ref[...])
        dxn, dgr = _rms_bwd(x_, g_ref[...], rstd, dh)
        dx_ref[...] = dy_ + dxn
        dg_ref[...] += _rowsum(dgr)

    return pl.pallas_call(
        body, name="ffn_bwd", grid=(S // ts,),
        in_specs=[_row(ts, D), _row(ts, D), _ff_chunks(N_DEV, ts), _const(g.shape), _layer_of(wgu, layer),
                  _const(wd.shape)],
        out_specs=[_row(ts, D), _acc((1, D)), _ff_chunks(FF_HALF, ts), _ff_chunks(N_DEV, ts)],
        out_shape=[_sds((S, D), F32), _sds((1, D), F32), _sds((FF_HALF, S, FF_CHUNK), BF16),
                   _sds((N_DEV, S, FF_CHUNK), BF16)],
        compiler_params=_cp())(x, dy, gu, g, wgu, wd)


def _conv_fwd(xprev, xbp, cw_ref, cb):
    ext = jnp.concatenate([xprev, xbp], axis=0)
    acc = cb + cw_ref[3:4, :] * xbp
    for k in range(3):
        acc = acc + cw_ref[k:k + 1, :] * _roll(ext, 3 - k, 0)[CONV_HALO:]
    return acc


def _decay(r, lam):
    sp = _softplus(-lam)
    log_a = -LRU_C * r * sp
    return sp, jnp.exp(log_a), jnp.sqrt(jnp.maximum(-_expm1(2.0 * log_a), 0.0))


def odd_pre(x, keep, g, win, cw, cb, wr, br, wi, bi, lam):
    S = x.shape[0]
    ts = _tile_rows(S, 512)

    def body(x_ref, xp_ref, keep_ref, g_ref, win_ref, cw_ref, cb_ref, wr_ref, br_ref, wi_ref, bi_ref, lam_ref,
             z_ref, a_ref, b_ref, xb_ref, r_ref, ig_ref):
        i = pl.program_id(0)
        h, _ = _rms(x_ref[...], g_ref[...])
        z = _dot(h, win_ref[...])
        z_ref[...] = z
        hp, _ = _rms(xp_ref[...], g_ref[...])
        xprev = _dot(hp, win_ref[:, D:]) * (i > 0).astype(F32)
        xb = _conv_fwd(xprev, z[:, D:], cw_ref, cb_ref[...])
        xb_ref[...] = xb
        r = jax.nn.sigmoid(_blockdot(xb, wr_ref, LRU_HEADS, LRU_HEAD_DIM) + br_ref[...])
        ig = jax.nn.sigmoid(_blockdot(xb, wi_ref, LRU_HEADS, LRU_HEAD_DIM) + bi_ref[...])
        r_ref[...] = r
        ig_ref[...] = ig
        keep_ = keep_ref[...]
        _, a, mult = _decay(r, lam_ref[...])
        a_ref[...] = a * keep_
        b_ref[...] = jnp.where(keep_ > 0.0, mult, 1.0) * (ig * xb)

    ins = [x, x, keep, g, win, cw, cb, wr, br, wi, bi, lam]
    return pl.pallas_call(
        body, name="odd_pre", grid=(S // ts,),
        in_specs=[_row(ts, D), _prev(CONV_HALO, D, ts), _row(ts, 1)] + [_const(v.shape) for v in ins[3:]],
        out_specs=[_row(ts, 2 * D)] + [_row(ts, D)] * 5,
        out_shape=[_sds((S, 2 * D), F32)] + [_sds((S, D), F32)] * 5, compiler_params=_cp())(*ins)


def lru_scan(a, b, reverse=False):
    S = a.shape[0]
    ts = _tile_rows(S, 512)
    n = S // ts
    groups = ts // 8

    def body(a_ref, an_ref, b_ref, h_ref, carry_ref, ash_ref):
        i = pl.program_id(0)

        @pl.when(i == 0)
        def _():
            carry_ref[...] = jnp.zeros_like(carry_ref)

        rid = lax.broadcasted_iota(jnp.int32, (8, D), 0)
        if reverse:
            ext = jnp.concatenate([a_ref[...], an_ref[...] * (i > 0).astype(F32)], axis=0)
            ash_ref[...] = _roll(ext, -1, 0)[:ts]
        src = ash_ref if reverse else a_ref

        def group(j, carry):
            off = pl.multiple_of((groups - 1 - j if reverse else j) * 8, 8)
            a8, b8 = src[pl.ds(off, 8), :], b_ref[pl.ds(off, 8), :]
            for k in (1, 2, 4):
                inside = (rid < 8 - k) if reverse else (rid >= k)
                sh = -k if reverse else k
                a_sh = jnp.where(inside, _roll(a8, sh, 0), 1.0)
                b_sh = jnp.where(inside, _roll(b8, sh, 0), 0.0)
                b8 = a8 * b_sh + b8
                a8 = a8 * a_sh
            h8 = a8 * carry + b8
            h_ref[pl.ds(off, 8), :] = h8
            return h8[0:1, :] if reverse else h8[7:8, :]

        carry_ref[...] = lax.fori_loop(0, groups, group, carry_ref[...], unroll=4)

    if reverse:
        r = ts // 8
        tile = pl.BlockSpec((ts, D), lambda i: (n - 1 - i, 0))
        halo = pl.BlockSpec((8, D), lambda i: (jnp.minimum((n - i) * r, n * r - 1), 0))
    else:
        tile, halo = _row(ts, D), _prev(8, D, ts)
    return pl.pallas_call(
        body, name="lru_scan_rev" if reverse else "lru_scan", grid=(n,), in_specs=[tile, halo, tile],
        out_specs=tile, out_shape=_sds((S, D), F32),
        scratch_shapes=[pltpu.VMEM((1, D), F32), pltpu.VMEM((ts, D), F32)], compiler_params=_cp())(a, a, b)


def odd_post(x, z, hseq, wout):
    S = x.shape[0]
    ts = _tile_rows(S, 512)

    def body(x_ref, gate_ref, h_ref, w_ref, out_ref):
        gl, _ = _gelu(gate_ref[...])
        out_ref[...] = x_ref[...] + _dot(gl * h_ref[...], w_ref[...])

    return pl.pallas_call(
        body, name="odd_post", grid=(S // ts,),
        in_specs=[_row(ts, D), _row(ts, D), _row(ts, D), _const(wout.shape)],
        out_specs=_row(ts, D), out_shape=_sds((S, D), F32), compiler_params=_cp())(x, z, hseq, wout)


def _accumulate_tn(acc_ref, out_ref, a, b, steps):
    i = pl.program_id(0)

    @pl.when(i == 0)
    def _():
        acc_ref[...] = jnp.zeros_like(acc_ref)

    acc_ref[...] += _dot_tn(a, b)

    @pl.when(i == steps - 1)
    def _():
        out_ref[...] = acc_ref[...].astype(out_ref.dtype)


def odd_post_bwd(dy, z, hseq, wout):
    S = dy.shape[0]
    ts = _tile_rows(S, 512)
    n = S // ts

    def body(dy_ref, gate_ref, h_ref, w_ref, dgate_ref, dh_ref, dw_ref, acc_ref):
        gate, hs, dy_ = gate_ref[...], h_ref[...], dy_ref[...]
        gl, t = _gelu(gate)
        dyy = _dot_nt(dy_, w_ref[...])
        dgate_ref[...] = dyy * hs * _gelu_grad(gate, t)
        dh_ref[...] = dyy * gl
        _accumulate_tn(acc_ref, dw_ref, gl * hs, dy_, n)

    return pl.pallas_call(
        body, name="odd_post_bwd", grid=(n,),
        in_specs=[_row(ts, D), _row(ts, D), _row(ts, D), _const(wout.shape)],
        out_specs=[_row(ts, D), _row(ts, D), _acc((D, D))],
        out_shape=[_sds((S, D), F32), _sds((S, D), F32), _sds((D, D), BF16)],
        scratch_shapes=[pltpu.VMEM((D, D), F32)], compiler_params=_cp())(dy, z, hseq, wout)


def odd_gates_bwd(xb, r, ig, lam_grad, hseq, keep, wr, wi, lam):
    S = xb.shape[0]
    ts = _tile_rows(S, 512)

    def body(xb_ref, r_ref, ig_ref, lg_ref, h_ref, hp_ref, keep_ref, wr_ref, wi_ref, lam_ref,
             dxb_ref, dcb_ref, dbr_ref, dbi_ref, dlam_ref, dwr_ref, dwi_ref):
        i = pl.program_id(0)

        @pl.when(i == 0)
        def _():
            for ref in (dcb_ref, dbr_ref, dbi_ref, dlam_ref, dwr_ref, dwi_ref):
                ref[...] = jnp.zeros_like(ref)

        first = (i > 0).astype(F32)
        xb, r, ig = xb_ref[...], r_ref[...], ig_ref[...]
        keep_ = keep_ref[...]
        lam_ = lam_ref[...]
        sp, a, mult = _decay(r, lam_)
        hs = h_ref[...]
        hprev = _roll(jnp.concatenate([hp_ref[...] * first, hs], axis=0), 1, 0)[CONV_HALO:]
        lg = lg_ref[...]
        da = lg * hprev * keep_
        ixb = ig * xb
        dmult = lg * ixb * keep_
        dixb = lg * jnp.where(keep_ > 0.0, mult, 1.0)
        dlog_a = da * a - dmult * jnp.where(mult > 0.0, a * a / mult, 0.0)
        dr = dlog_a * (-LRU_C * sp)
        dlam_ref[...] += _rowsum(dlog_a * (-LRU_C * r)) * (-jax.nn.sigmoid(-lam_))
        dpr = dr * r * (1.0 - r)
        dpi = dixb * xb * ig * (1.0 - ig)
        dbr_ref[...] += _rowsum(dpr)
        dbi_ref[...] += _rowsum(dpi)
        dxb = dixb * ig
        parts = []
        for h in range(LRU_HEADS):
            sl = slice(h * LRU_HEAD_DIM, (h + 1) * LRU_HEAD_DIM)
            dwr_ref[h] += _dot_tn(xb[:, sl], dpr[:, sl])
            dwi_ref[h] += _dot_tn(xb[:, sl], dpi[:, sl])
            parts.append(_dot_nt(dpr[:, sl], wr_ref[h]) + _dot_nt(dpi[:, sl], wi_ref[h]))
        dxb = dxb + jnp.concatenate(parts, axis=1)
        dxb_ref[...] = dxb
        dcb_ref[...] += _rowsum(dxb)

    ins = [xb, r, ig, lam_grad, hseq, hseq, keep, wr, wi, lam]
    in_specs = [_row(ts, D)] * 5 + [_prev(CONV_HALO, D, ts), _row(ts, 1)] + [_const(v.shape) for v in ins[7:]]
    gshape = (LRU_HEADS, LRU_HEAD_DIM, LRU_HEAD_DIM)
    return pl.pallas_call(
        body, name="odd_gates_bwd", grid=(S // ts,), in_specs=in_specs,
        out_specs=[_row(ts, D)] + [_acc((1, D))] * 4 + [_acc(gshape)] * 2,
        out_shape=[_sds((S, D), F32)] + [_sds((1, D), F32)] * 4 + [_sds(gshape, F32)] * 2,
        compiler_params=_cp())(*ins)


def odd_pre_bwd(x, dy, z, dxb, dgate, g, cw, win):
    S = x.shape[0]
    ts = _tile_rows(S, 512)
    n = S // ts

    def body(x_ref, dy_ref, xbp_ref, xbpp_ref, dxb_ref, dxbn_ref, dgate_ref, g_ref, cw_ref, win_ref,
             dx_ref, dcw_ref, dg_ref, dwin_ref, acc_ref):
        i = pl.program_id(0)

        @pl.when(i == 0)
        def _():
            dcw_ref[...] = jnp.zeros_like(dcw_ref)
            dg_ref[...] = jnp.zeros_like(dg_ref)

        dxb = dxb_ref[...]
        extd = jnp.concatenate([dxb, dxbn_ref[...] * (i < n - 1).astype(F32)], axis=0)
        extx = jnp.concatenate([xbpp_ref[...] * (i > 0).astype(F32), xbp_ref[...]], axis=0)
        dxbp = cw_ref[3:4, :] * dxb
        dcw_ref[3:4, :] += _rowsum(dxb * xbp_ref[...])
        for k in range(3):
            dxbp = dxbp + cw_ref[k:k + 1, :] * _roll(extd, -(3 - k), 0)[:ts]
            dcw_ref[k:k + 1, :] += _rowsum(dxb * _roll(extx, 3 - k, 0)[CONV_HALO:])
        dz = jnp.concatenate([dgate_ref[...], dxbp], axis=1).astype(BF16)
        x_ = x_ref[...]
        h, rstd = _rms(x_, g_ref[...])
        dxn, dgr = _rms_bwd(x_, g_ref[...], rstd, _dot_nt(dz, win_ref[...]))
        dx_ref[...] = dy_ref[...] + dxn
        dg_ref[...] += _rowsum(dgr)
        _accumulate_tn(acc_ref, dwin_ref, h, dz, n)

    ins = [x, dy, z, z, dxb, dxb, dgate, g, cw, win]
    in_specs = [_row(ts, D), _row(ts, D), _row(ts, D, 1), _prev(CONV_HALO, D, ts, 1), _row(ts, D),
                _next(CONV_HALO, D, ts, n), _row(ts, D)] + [_const(v.shape) for v in ins[7:]]
    return pl.pallas_call(
        body, name="odd_pre_bwd", grid=(n,), in_specs=in_specs,
        out_specs=[_row(ts, D), _acc((4, D)), _acc((1, D)), _acc((D, 2 * D))],
        out_shape=[_sds((S, D), F32), _sds((4, D), F32), _sds((1, D), F32), _sds((D, 2 * D), BF16)],
        scratch_shapes=[pltpu.VMEM((D, 2 * D), F32)], compiler_params=_cp())(*ins)


def loss_head(x, target, g):
    S = x.shape[0]
    ts = _tile_rows(S, 512)

    def body(x_ref, t_ref, g_ref, dx_ref, dg_ref, loss_ref):
        @pl.when(pl.program_id(0) == 0)
        def _():
            dg_ref[...] = jnp.zeros_like(dg_ref)
            loss_ref[...] = jnp.zeros_like(loss_ref)

        x_ = x_ref[...]
        y, rstd = _rms(x_, g_ref[...])
        err = y - t_ref[...]
        loss_ref[...] += 0.5 * _rowsum(jnp.mean(err * err, axis=1, keepdims=True))
        dxn, dgr = _rms_bwd(x_, g_ref[...], rstd, err * (1.0 / D))
        dx_ref[...] = dxn
        dg_ref[...] += _rowsum(dgr)

    return pl.pallas_call(
        body, name="loss_head", grid=(S // ts,), in_specs=[_row(ts, D), _row(ts, D), _const(g.shape)],
        out_specs=[_row(ts, D), _acc((1, D)), _acc((1, 1))],
        out_shape=[_sds((S, D), F32), _sds((1, D), F32), _sds((1, 1), F32)], compiler_params=_cp())(x, target, g)


def even_post_bwd(dy, ypool, o, wo_pool, wo_att):
    S = dy.shape[0]
    ts = _tile_rows(S, 512)
    n = S // ts

    def body(dy_ref, yp_ref, o_ref, wp_ref, wa_ref, dyp_ref, do_ref, delta_ref, dwp_ref, dwa_ref, accp_ref,
             acca_ref):
        dy_, o_ = dy_ref[...], o_ref[...]
        dyp_ref[...] = _dot_nt(dy_, wp_ref[...])
        do = _dot_nt(dy_, wa_ref[...])
        do_ref[...] = do.astype(BF16)
        prod = do * o_
        for h in range(MLA_HEADS):
            delta_ref[h] = _as_row(jnp.sum(prod[:, h * LANES:(h + 1) * LANES], axis=1, keepdims=True))
        _accumulate_tn(accp_ref, dwp_ref, yp_ref[...], dy_, n)
        _accumulate_tn(acca_ref, dwa_ref, o_, dy_, n)

    return pl.pallas_call(
        body, name="even_post_bwd", grid=(n,),
        in_specs=[_row(ts, D), _row(ts, POOL_DIM), _row(ts, D), _const(wo_pool.shape), _const(wo_att.shape)],
        out_specs=[_row(ts, POOL_DIM), _row(ts, D),
                   pl.BlockSpec((MLA_HEADS, None, 1, ts), lambda i: (0, i, 0, 0)), _acc((POOL_DIM, D)),
                   _acc((D, D))],
        out_shape=[_sds((S, POOL_DIM), F32), _sds((S, D), BF16), _sds((MLA_HEADS, n, 1, ts), F32),
                   _sds((POOL_DIM, D), BF16), _sds((D, D), BF16)],
        scratch_shapes=[pltpu.VMEM((POOL_DIM, D), F32), pltpu.VMEM((D, D), F32)],
        compiler_params=_cp())(dy, ypool, o, wo_pool, wo_att)


def attn_bwd(qp, kp, vp, do, lse_row, delta_row, token=None):
    S = qp.shape[0]
    tk = _tile_rows(S, 512)
    nq = S // tk
    extra, extra_specs = _after(token)

    def body(q_ref, k_ref, v_ref, do_ref, lse_ref, delta_ref, *rest):
        dq_ref, dk_ref, dv_ref = rest[-3:]
        kj = pl.program_id(1)

        @pl.when(kj == 0)
        def _():
            dq_ref[...] = jnp.zeros_like(dq_ref)

        k, v = k_ref[...], v_ref[...]

        def block(qi, carry, masked):
            dk, dv = carry
            off = pl.multiple_of(qi * tk, tk)
            q = q_ref[pl.ds(off, tk), :]
            do_ = do_ref[pl.ds(off, tk), :]
            st = _dot_nt(k, q)
            if masked:
                row = lax.broadcasted_iota(jnp.int32, (tk, tk), 0)
                col = lax.broadcasted_iota(jnp.int32, (tk, tk), 1)
                st = jnp.where(col >= row, st, -1e30)
            pt = _exp2(st - lse_ref[qi])
            dv = dv + _dot(pt, do_)
            dst = (pt * (_dot_nt(v, do_) - delta_ref[qi])).astype(BF16)
            dk = dk + _dot(dst, q)
            dq_ref[pl.ds(off, tk), :] += _dot_tn(dst, k)
            return dk, dv

        zero = jnp.zeros((tk, LANES), F32)
        carry = block(kj, (zero, zero), True)
        dk, dv = _pair_loop(kj + 1, nq, lambda qi, c: block(qi, c, False), carry, unrolls=(4, 2, 1))
        dk_ref[...] = dk * LN_2
        dv_ref[...] = dv

    blk = pl.BlockSpec((tk, LANES), lambda h, j: (j, h))
    full = pl.BlockSpec((S, LANES), lambda h, j: (0, h))
    rowv = pl.BlockSpec((None, nq, 1, tk), lambda h, j: (h, 0, 0, 0))
    return pl.pallas_call(
        body, name="attn_bwd", grid=(MLA_HEADS, nq), in_specs=[full, blk, blk, full, rowv, rowv] + extra_specs,
        out_specs=[full, blk, blk], out_shape=[_sds((S, D), F32)] * 3, compiler_params=_cp2())(
            qp, kp, vp, do, lse_row, delta_row, *extra)


def even_pre_bwd(x, dy, z, dq, dk, dv, dyp, tabs, g, win, pw, pscale, qg, wq, kvg, wk, wv):
    S = x.shape[0]
    ts = _tile_rows(S, 512)
    n = S // ts

    def body(x_ref, dy_ref, z_ref, up_ref, dq_ref, dk_ref, dv_ref, dyp_ref, dypn_ref, c_ref, a_ref, b_ref,
             g_ref, win_ref, pw_ref, ps_ref, qg_ref, wq_ref, kvg_ref, wk_ref, wv_ref,
             dx_ref, dg_ref, dpw_ref, dps_ref, dqg_ref, dwq_ref, dkvg_ref, dwk_ref, dwv_ref, dwin_ref, acc_ref):
        i = pl.program_id(0)

        @pl.when(i == 0)
        def _():
            for ref in (dg_ref, dpw_ref, dps_ref, dqg_ref, dwq_ref, dkvg_ref, dwk_ref, dwv_ref):
                ref[...] = jnp.zeros_like(ref)

        z = z_ref[...]
        c, a, b = c_ref[...], a_ref[...], b_ref[...]
        ps = ps_ref[...]
        u = z[:, :POOL_DIM]
        pooled = _pooled(up_ref[...] * (i > 0).astype(F32), u, i * ts)
        dyp_ = dyp_ref[...]
        dps_ref[...] += _rowsum(dyp_ * _blockdot(pooled, pw_ref, 4, LANES))
        ext = jnp.concatenate([dyp_, dypn_ref[...] * (i < n - 1).astype(F32)], axis=0) * ps
        for gidx in range(4):
            sl = slice(gidx * LANES, (gidx + 1) * LANES)
            dpw_ref[gidx] += _dot_tn(pooled[:, sl], ext[:ts, sl])
        dpooled = jnp.concatenate(
            [_dot_nt(ext[:, gidx * LANES:(gidx + 1) * LANES], pw_ref[gidx]) for gidx in range(4)], axis=1)
        dm = dpooled / _pool_cnt(i * ts, ts + POOL_HALO)
        du = _pool_windows(dm, -1)[:ts] - dpooled[:ts]
        cq = z[:, 512:768]
        cqn, rstd_q = _rms(cq, qg_ref[...])
        dqf = _rope_bwd(dq_ref[...] * ATTN_SCALE, c, a, b)
        dwq_ref[...] += _dot_tn(cqn, dqf)
        dcq, dqg_rows = _rms_bwd(cq, qg_ref[...], rstd_q, _dot_nt(dqf, wq_ref[...]))
        dqg_ref[...] += _rowsum(dqg_rows)
        ckv = z[:, 768:896]
        ckvn, rstd_kv = _rms(ckv, kvg_ref[...])
        dk_, dv_ = dk_ref[...], dv_ref[...]
        dwk_ref[...] += _dot_tn(ckvn, dk_)
        dwv_ref[...] += _dot_tn(ckvn, dv_)
        dckv, dkvg_rows = _rms_bwd(ckv, kvg_ref[...], rstd_kv,
                                   _dot_nt(dk_, wk_ref[...]) + _dot_nt(dv_, wv_ref[...]))
        dkvg_ref[...] += _rowsum(dkvg_rows)
        dkr = dk_[:, :LANES]
        for h in range(1, MLA_HEADS):
            dkr = dkr + dk_[:, h * LANES:(h + 1) * LANES]
        lane = lax.broadcasted_iota(jnp.int32, (ts, LANES), 1)
        dkr = jnp.where((lane >= 64) & (lane < 96), _rope_bwd(dkr, c, a, b), 0.0)
        dz = jnp.concatenate([du, dcq, dckv, dkr], axis=1).astype(BF16)
        x_ = x_ref[...]
        h, rstd = _rms(x_, g_ref[...])
        dxn, dgr = _rms_bwd(x_, g_ref[...], rstd, _dot_nt(dz, win_ref[...]))
        dx_ref[...] = dy_ref[...] + dxn
        dg_ref[...] += _rowsum(dgr)
        _accumulate_tn(acc_ref, dwin_ref, h, dz, n)

    ins = [x, dy, z, z, dq, dk, dv, dyp, dyp, *tabs, g, win, pw, pscale, qg, wq, kvg, wk, wv]
    in_specs = [_row(ts, D), _row(ts, D), _row(ts, D), _prev(POOL_HALO, POOL_DIM, ts), _row(ts, D), _row(ts, D),
                _row(ts, D), _row(ts, POOL_DIM), _next(POOL_HALO, POOL_DIM, ts, n), _row(ts, LANES),
                _row(ts, LANES), _row(ts, LANES)] + [_const(v.shape) for v in ins[12:]]
    acc_shapes = [(1, D), (4, LANES, LANES), (1, POOL_DIM), (1, Q_LORA), (Q_LORA, D), (1, KV_LORA), (KV_LORA, D),
                  (KV_LORA, D)]
    return pl.pallas_call(
        body, name="even_pre_bwd", grid=(n,), in_specs=in_specs,
        out_specs=[_row(ts, D)] + [_acc(s) for s in acc_shapes] + [_acc((D, D))],
        out_shape=[_sds((S, D), F32)] + [_sds(s, F32) for s in acc_shapes] + [_sds((D, D), BF16)],
        scratch_shapes=[pltpu.VMEM((D, D), F32)], compiler_params=_cp())(*ins)


def _pick(n, options):
    for o in options:
        if n % o == 0:
            return o
    return n


def matmul_tn(name, a, b):
    out_dtype = BF16
    S = a.shape[-2]
    ts = _tile_rows(S, 2048)
    steps = S // ts

    def body(a_ref, b_ref, o_ref, acc_ref):
        s = pl.program_id(2)

        @pl.when(s == 0)
        def _():
            acc_ref[...] = jnp.zeros_like(acc_ref)

        acc_ref[...] += _dot_tn(a_ref[...], b_ref[...])

        @pl.when(s == steps - 1)
        def _():
            o_ref[...] = acc_ref[...].astype(o_ref.dtype)

    if a.ndim == 3:
        C, _, K = a.shape
        N = b.shape[1]
        tn = _pick(N, (1024, 512, 256, 128))
        grid = (C, N // tn, S // ts)
        in_specs = [pl.BlockSpec((None, ts, K), lambda c, j, s: (c, s, 0)),
                    pl.BlockSpec((ts, tn), lambda c, j, s: (s, j))]
        out_spec, out_shape, tile = pl.BlockSpec((None, K, tn), lambda c, j, s: (c, 0, j)), (C, K, N), (K, tn)
    elif b.ndim == 3:
        C, _, N = b.shape
        K = a.shape[1]
        tk = _pick(K, (1024, 512, 256, 128))
        grid = (C, K // tk, S // ts)
        in_specs = [pl.BlockSpec((ts, tk), lambda c, i, s: (s, i)),
                    pl.BlockSpec((None, ts, N), lambda c, i, s: (c, s, 0))]
        out_spec, out_shape, tile = pl.BlockSpec((None, tk, N), lambda c, i, s: (c, i, 0)), (C, K, N), (tk, N)
    else:
        K, N = a.shape[1], b.shape[1]
        tk = _pick(K, (1024, 512, 256, 128))
        tn = _pick(N, (1024, 512, 256, 128))
        grid = (K // tk, N // tn, S // ts)
        in_specs = [pl.BlockSpec((ts, tk), lambda i, j, s: (s, i)), pl.BlockSpec((ts, tn), lambda i, j, s: (s, j))]
        out_spec, out_shape, tile = pl.BlockSpec((tk, tn), lambda i, j, s: (i, j)), (K, N), (tk, tn)
    return pl.pallas_call(
        body, name=name, grid=grid, in_specs=in_specs, out_specs=out_spec, out_shape=_sds(out_shape, out_dtype),
        scratch_shapes=[pltpu.VMEM(tile, F32)], compiler_params=pltpu.CompilerParams(dimension_semantics=("arbitrary",) * 3, vmem_limit_bytes=VMEM_LIMIT))(
            a, b)


def _my_id():
    return lax.axis_index("x") * 4 + lax.axis_index("y") * 2 + lax.axis_index("c")


def _peer(j):
    x, y, c = lax.axis_index("x"), lax.axis_index("y"), lax.axis_index("c")
    px = 1 - x if j & 4 else x
    py = 1 - y if j & 2 else y
    pc = 1 - c if j & 1 else c
    return (px, py, pc), px * 4 + py * 2 + pc


def all_gather(name, arrays):
    n = len(arrays)

    def body(*refs):
        ins, outs = refs[:n], refs[n:2 * n]
        send_sems, recv_sems, local_sems = refs[2 * n:]
        me = _my_id()
        local = [pltpu.make_async_copy(ins[k], outs[k].at[me], local_sems.at[k]) for k in range(n)]
        for cp in local:
            cp.start()
        sends = []
        for j in range(1, N_DEV):
            peer, _ = _peer(j)
            for k in range(n):
                cp = pltpu.make_async_remote_copy(
                    src_ref=ins[k], dst_ref=outs[k].at[me], send_sem=send_sems.at[k, j - 1],
                    recv_sem=recv_sems.at[k, j - 1], device_id=peer, device_id_type=pl.DeviceIdType.MESH)
                cp.start()
                sends.append(cp)
        for j in range(1, N_DEV):
            peer, pid = _peer(j)
            for k in range(n):
                pltpu.make_async_remote_copy(
                    src_ref=ins[k], dst_ref=outs[k].at[pid], send_sem=send_sems.at[k, j - 1],
                    recv_sem=recv_sems.at[k, j - 1], device_id=peer, device_id_type=pl.DeviceIdType.MESH).wait_recv()
        for cp in sends:
            cp.wait_send()
        for cp in local:
            cp.wait()

    any_spec = pl.BlockSpec(memory_space=pl.ANY)
    return pl.pallas_call(
        body, name=name, in_specs=[any_spec] * n, out_specs=[any_spec] * n,
        out_shape=[_sds((N_DEV,) + a.shape, a.dtype) for a in arrays],
        scratch_shapes=[pltpu.SemaphoreType.DMA((n, N_DEV - 1)), pltpu.SemaphoreType.DMA((n, N_DEV - 1)),
                        pltpu.SemaphoreType.DMA((n,))],
        compiler_params=pltpu.CompilerParams(has_side_effects=True))(*arrays)


def exchange(name, arrays, gathers=()):
    n_ex, n = len(arrays), len(arrays) + len(gathers)

    def body(*refs):
        ins, outs = refs[:n], refs[n:2 * n]
        send_sems, recv_sems, local_sems = refs[2 * n:]
        me = _my_id()

        def mine(k, slot):
            return ins[k].at[slot] if k < n_ex else ins[k]

        local = [pltpu.make_async_copy(mine(k, me), outs[k].at[me], local_sems.at[k]) for k in range(n)]
        for cp in local:
            cp.start()
        sends = []
        for j in range(1, N_DEV):
            peer, pid = _peer(j)
            for k in range(n):
                cp = pltpu.make_async_remote_copy(
                    src_ref=mine(k, pid), dst_ref=outs[k].at[me], send_sem=send_sems.at[k, j - 1],
                    recv_sem=recv_sems.at[k, j - 1], device_id=peer, device_id_type=pl.DeviceIdType.MESH)
                cp.start()
                sends.append(cp)
        for j in range(1, N_DEV):
            peer, pid = _peer(j)
            for k in range(n):
                pltpu.make_async_remote_copy(
                    src_ref=mine(k, me), dst_ref=outs[k].at[pid], send_sem=send_sems.at[k, j - 1],
                    recv_sem=recv_sems.at[k, j - 1], device_id=peer, device_id_type=pl.DeviceIdType.MESH).wait_recv()
        for cp in sends:
            cp.wait_send()
        for cp in local:
            cp.wait()

    any_spec = pl.BlockSpec(memory_space=pl.ANY)
    return pl.pallas_call(
        body, name=name, in_specs=[any_spec] * n, out_specs=[any_spec] * n,
        out_shape=[_sds(a.shape, a.dtype) for a in arrays] + [_sds((N_DEV,) + a.shape, a.dtype) for a in gathers],
        scratch_shapes=[pltpu.SemaphoreType.DMA((n, N_DEV - 1)), pltpu.SemaphoreType.DMA((n, N_DEV - 1)),
                        pltpu.SemaphoreType.DMA((n,))],
        compiler_params=pltpu.CompilerParams(has_side_effects=True))(*arrays, *gathers)


_HBM = pl.BlockSpec(memory_space=pltpu.HBM)
_SEM = pl.BlockSpec(memory_space=pltpu.SEMAPHORE)
_DATAFLOW = pltpu.SideEffectType.DATAFLOW_SIDE_EFFECTING


def _in_hbm(v):
    return pltpu.with_memory_space_constraint(v, pltpu.HBM)


N_PEERS = N_DEV - 1


def _split_copy(k, j, srcs, lands, send_sems, recv_sems, gather, slot):
    peer, pid = _peer(j)
    return pltpu.make_async_remote_copy(
        src_ref=srcs[k] if gather else srcs[k].at[pid], dst_ref=lands[k].at[_my_id() if slot == "mine" else pid],
        send_sem=send_sems[j - 1], recv_sem=recv_sems[j - 1], device_id=peer, device_id_type=pl.DeviceIdType.MESH)


def split_start(name, arrays, gather):
    n = len(arrays)
    lands = [lax.empty((N_DEV,) + a.shape if gather else a.shape, a.dtype) for a in arrays]

    def body(*refs):
        srcs, lnds = refs[:n], refs[n:2 * n]
        sems = refs[4 * n:4 * n + 2 * N_PEERS]
        token = refs[-1]
        for j in range(1, N_DEV):
            for k in range(n):
                _split_copy(k, j, srcs, lnds, sems[:N_PEERS], sems[N_PEERS:], gather, "mine").start()
        token[...] = jnp.zeros_like(token)

    out = pl.pallas_call(
        body, name=name,
        out_shape=(*[pltpu.HBM(a.shape, a.dtype) for a in arrays], *[pltpu.HBM(l.shape, l.dtype) for l in lands],
                   *[pltpu.SemaphoreType.DMA(())] * (2 * N_PEERS), _sds((8, LANES), F32)),
        in_specs=[_HBM] * (2 * n),
        out_specs=(*[_HBM] * (2 * n), *[_SEM] * (2 * N_PEERS), pl.BlockSpec(memory_space=pltpu.VMEM)),
        input_output_aliases={k: k for k in range(2 * n)},
        compiler_params=pltpu.CompilerParams(has_side_effects=_DATAFLOW))(
            *[_in_hbm(a) for a in arrays], *[_in_hbm(l) for l in lands])
    sems = list(out[2 * n:2 * n + 2 * N_PEERS])
    return sems[:N_PEERS], sems[N_PEERS:], list(out[:n]), list(out[n:2 * n]), out[-1]


def split_wait(name, handle, after, gather):
    send_sems, recv_sems, srcs, lands, _ = handle
    n = len(srcs)

    def body(*refs):
        srcs_r, lnds_r = refs[:n], refs[n:2 * n]
        sems = refs[2 * n:2 * n + 2 * N_PEERS]
        for j in range(1, N_DEV):
            for k in range(n):
                cp = _split_copy(k, j, srcs_r, lnds_r, sems[:N_PEERS], sems[N_PEERS:], gather, "peer")
                cp.wait_send()
                cp.wait_recv()

    out = pl.pallas_call(
        body, name=name, out_shape=tuple(pltpu.HBM(a.shape, a.dtype) for a in srcs + lands),
        in_specs=[_HBM] * (2 * n) + [_SEM] * (2 * N_PEERS) + [pl.BlockSpec(memory_space=pl.ANY)],
        out_specs=tuple([_HBM] * (2 * n)), input_output_aliases={k: k for k in range(2 * n)},
        compiler_params=pltpu.CompilerParams(has_side_effects=_DATAFLOW))(
            *srcs, *lands, *send_sems, *recv_sems, after)
    return list(out[:n]), list(out[n:])


def _fill_own_slot(src, land, gather):
    me = _my_id()
    own = src[None] if gather else lax.dynamic_index_in_dim(src, me, 0, keepdims=True)
    return lax.dynamic_update_slice_in_dim(land, own, me, 0)


ADAMW_BLOCK_ELEMS = 128 * 1024


def adamw(name, parts, w, m, v):
    R, C = w.shape
    tr = _pick(R, [t for t in (512, 256, 128, 64, 32, 16, 8) if t * C <= ADAMW_BLOCK_ELEMS])
    c1 = 1.0 - ADAM_B1 ** ADAM_STEP
    c2 = 1.0 - ADAM_B2 ** ADAM_STEP

    def body(p_ref, w_ref, m_ref, v_ref, g_ref, d_ref, nm_ref, nv_ref):
        g = p_ref[0].astype(F32)
        for s in range(1, N_DEV):
            g = g + p_ref[s].astype(F32)
        g_ref[...] = g
        m_ = ADAM_B1 * m_ref[...] + (1.0 - ADAM_B1) * g
        v_ = ADAM_B2 * v_ref[...] + (1.0 - ADAM_B2) * (g * g)
        nm_ref[...] = m_
        nv_ref[...] = v_
        d_ref[...] = -ADAM_LR * ((m_ / c1) / (jnp.sqrt(v_ / c2) + ADAM_EPS) + ADAM_WD * w_ref[...])

    row = pl.BlockSpec((tr, C), lambda i: (i, 0))
    return pl.pallas_call(
        body, name=name, grid=(R // tr,),
        in_specs=[pl.BlockSpec((N_DEV, tr, C), lambda i: (0, i, 0)), row, row, row], out_specs=[row] * 4,
        out_shape=[_sds((R, C), F32)] * 4, compiler_params=_cp())(parts, w, m, v)


WEIGHTS = ['ev_norm', 'ev_w_in', 'ev_pool_w', 'ev_pool_scale', 'ev_q_norm', 'ev_w_q_up', 'ev_kv_norm', 'ev_w_kv_up',
           'ev_w_out', 'od_norm', 'od_w_in', 'od_conv_w', 'od_conv_b', 'od_w_rgate', 'od_b_rgate', 'od_w_igate',
           'od_b_igate', 'od_lambda', 'od_w_out', 'xa_norm_x', 'xa_norm_mem', 'xa_w_q', 'xa_w_kv', 'xa_w_o',
           'ffn_norm', 'ffn_w_gate_up', 'ffn_w_down', 'final_norm']
SHARD_AXIS = {'ev_w_in': 1, 'ev_w_q_up': 2, 'ev_w_kv_up': 2, 'ev_w_out': 1, 'od_norm': 1, 'od_w_in': 2,
              'od_conv_w': 2, 'od_conv_b': 1, 'od_w_rgate': 2, 'od_b_rgate': 1, 'od_w_igate': 2, 'od_b_igate': 1,
              'od_lambda': 1, 'od_w_out': 1, 'xa_w_q': 1, 'xa_w_kv': 2, 'xa_w_o': 1, 'ffn_w_gate_up': 2,
              'ffn_w_down': 1}
SMALL_F32 = ('od_norm', 'od_conv_w', 'od_conv_b', 'od_b_rgate', 'od_b_igate', 'od_lambda')
STACKED = ('ffn_w_gate_up', 'ffn_w_down')
SHARDED = [n for n in WEIGHTS if n in SHARD_AXIS]
REPLICATED = [n for n in WEIGHTS if n not in SHARD_AXIS]
ROW_ALIGN = 512


def _pack(flats, dtype):
    v = jnp.concatenate([f.reshape(-1).astype(dtype) for f in flats])
    pad = (-v.shape[0]) % (ROW_ALIGN * LANES)
    return jnp.pad(v, (0, pad)).reshape(-1, LANES)


def _rows8(n_elems):
    return -(-n_elems // (8 * LANES)) * 8


def _pack_rows(arrays, lead=False):
    out = []
    for a in arrays:
        r = a.reshape((N_DEV, -1, LANES) if lead else (-1, LANES))
        pad = _rows8(r.shape[-2] * LANES) - r.shape[-2]
        out.append(jnp.pad(r, [(0, 0)] * (r.ndim - 2) + [(0, pad), (0, 0)]))
    return jnp.concatenate(out, axis=-2)


def _unpack_rows(buf, shapes, lead=False):
    out, off = [], 0
    for s in shapes:
        n = 1
        for d in s:
            n *= d
        rows = buf[..., off:off + n // LANES, :]
        out.append(rows.reshape(((N_DEV,) if lead else ()) + tuple(s)))
        off += _rows8(n)
    return out


def _unpack(flat, shapes):
    out, off = [], 0
    v = flat.reshape(-1)
    for s in shapes:
        n = 1
        for d in s:
            n *= d
        out.append(v[off:off + n].reshape(s))
        off += n
    return out


def _to_full(stacked, axis):
    v = jnp.moveaxis(stacked, 0, axis)
    s = v.shape
    return v.reshape(s[:axis] + (s[axis] * s[axis + 1],) + s[axis + 2:])


def _to_shards(full, axis):
    s = full.shape
    v = full.reshape(s[:axis] + (N_DEV, s[axis] // N_DEV) + s[axis + 1:])
    return jnp.moveaxis(v, axis, 0)


def _pad_heads(w, nh, dh, lead):
    s = w.shape
    v = w.reshape(s[:-1] + (nh, dh))
    v = jnp.pad(v, [(0, 0)] * (len(s) - 1) + [(0, 0), (lead, LANES - dh - lead)])
    return v.reshape(s[:-1] + (nh * LANES,))


def _unpad_heads(w, nh, dh, lead):
    s = w.shape
    return w.reshape(s[:-1] + (nh, LANES))[..., lead:lead + dh].reshape(s[:-1] + (nh * dh,))


def _rope_tables(positions):
    inv_freq = 10000.0 ** (-jnp.arange(0, 32, 2, dtype=F32) / 32)
    ang = positions.astype(F32)[:, None] * inv_freq
    cos, sin = jnp.cos(ang), jnp.sin(ang)
    S = positions.shape[0]
    one, zero = jnp.ones((S, 64), F32), jnp.zeros((S, 64), F32)
    z16, z32 = jnp.zeros((S, 16), F32), jnp.zeros((S, 32), F32)
    c = jnp.concatenate([one, cos, cos, jnp.ones((S, 32), F32)], axis=1)
    a = jnp.concatenate([zero, z16, sin, z32], axis=1)
    b = jnp.concatenate([zero, -sin, z16, z32], axis=1)
    return c, a, b


def _t(w):
    return jnp.swapaxes(w, -1, -2)


def device_step(x, mem, positions, target, W, fwd_token=None, late_weights=None, ship_grads=None):
    S = x.shape[0]
    G = {}
    tabs = _rope_tables(positions)
    keep = (positions != 0).astype(F32)[:, None]
    row = lambda v: v.reshape(1, -1)

    w_in = W['ev_w_in'][0]
    ev_win = jnp.concatenate([w_in[:, :896], _pad_heads(w_in[:, 896:], 1, 32, 64)], axis=1)
    ev_wq = _pad_heads(W['ev_w_q_up'][0], MLA_HEADS, QK_DIM, 0)
    kvw = W['ev_w_kv_up'][0].reshape(KV_LORA, MLA_HEADS, 128)
    ev_wk = _pad_heads(kvw[:, :, :64].reshape(KV_LORA, 512), MLA_HEADS, 64, 0)
    ev_wv = _pad_heads(kvw[:, :, 64:].reshape(KV_LORA, 512), MLA_HEADS, 64, 0)
    ev_wo_pool = W['ev_w_out'][0][:POOL_DIM]
    ev_wo_att = _t(_pad_heads(_t(W['ev_w_out'][0][POOL_DIM:]), MLA_HEADS, 64, 0))
    pw = W['ev_pool_w'][0].astype(BF16)
    ev_g, ps, qg, kvg = row(W['ev_norm'][0]), row(W['ev_pool_scale'][0]), row(W['ev_q_norm'][0]), row(W['ev_kv_norm'][0])

    z0, qp, kp, vp, ypool = even_pre(x, tabs, ev_g, ev_win, pw, ps, qg, ev_wq, kvg, ev_wk, ev_wv)
    o_att, lse = attn_fwd(qp, kp, vp, fwd_token)
    if late_weights is not None:
        W = {**W, **late_weights(lse)}
    x1 = even_post(x, ypool, o_att, ev_wo_pool, ev_wo_att)

    def xa_ffn_fwd(xin, l):
        mn, km, vm = mem_kv(mem, row(W['xa_norm_mem'][l]), W['xa_w_kv'][l])
        xm = xattn_fwd(xin, row(W['xa_norm_x'][l]), W['xa_w_q'][l], km, vm, W['xa_w_o'][l])
        xo, hf, gu = ffn_fwd(xm, row(W['ffn_norm'][l]), W['ffn_w_gate_up'], l,
                             W['ffn_w_down'][:, l].reshape(FF_HALF, FF_CHUNK, D))
        return xm, xo, (mn, km, vm, hf, gu)

    x2, x3, memkv0 = xa_ffn_fwd(x1, 0)

    od_g, lam = row(W['od_norm'][0]), row(W['od_lambda'][0])
    cw, cb = W['od_conv_w'][0], row(W['od_conv_b'][0])
    wr, wi = W['od_w_rgate'][0], W['od_w_igate'][0]
    br, bi = row(W['od_b_rgate'][0]), row(W['od_b_igate'][0])
    z1, a_t, b_t, xb1, r1, ig1 = odd_pre(x3, keep, od_g, W['od_w_in'][0], cw, cb, wr, br, wi, bi, lam)
    hseq = lru_scan(a_t, b_t)
    x4 = odd_post(x3, z1, hseq, W['od_w_out'][0])
    x5, x6, memkv1 = xa_ffn_fwd(x4, 1)

    dx, G['final_norm'], loss = loss_head(x6, target, row(W['final_norm']))
    G['final_norm'] = G['final_norm'].reshape(D)

    gnx, gnm, gwq, gwkv, gwo, gfn, gwgu, gwd = ([None, None] for _ in range(8))

    def xa_ffn_bwd(dy, xin, xm, memkv, l):
        mn, km, vm, hf, gu = memkv
        fg = row(W['ffn_norm'][l])
        dxm, dfg, act, dgu = ffn_bwd(xm, dy, gu, fg, W['ffn_w_gate_up'], l,
                                     W['ffn_w_down'][:, l].reshape(FF_HALF, FF_CHUNK, D))
        gwd[l] = matmul_tn("ffn_dwd", act, dy).reshape(N_DEV, D_FF // N_DEV, D)
        gwgu[l] = matmul_tn("ffn_dwgu", hf, dgu)
        gfn[l] = dfg[0]
        dxin, o, dq, hx, dgx, dk, dv = xattn_bwd(xin, dxm, row(W['xa_norm_x'][l]), W['xa_w_q'][l], km, vm,
                                                  W['xa_w_o'][l])
        gnx[l] = dgx[0]
        gwo[l] = matmul_tn("xa_dwo", o, dxm)
        gwq[l] = matmul_tn("xa_dwq", hx, dq)
        dkv, dgm = mem_bwd(mem, row(W['xa_norm_mem'][l]), dk, dv, W['xa_w_kv'][l])
        gnm[l] = dgm[0]
        gwkv[l] = matmul_tn("xa_dwkv", mn, dkv)
        return dxin

    dx4 = xa_ffn_bwd(dx, x4, x5, memkv1, 1)

    dgate, dhs, g_od_wout = odd_post_bwd(dx4, z1, hseq, W['od_w_out'][0])
    G['od_w_out'] = g_od_wout[None]
    lam_grad = lru_scan(a_t, dhs, reverse=True)
    dxb, dcb, dbr, dbi, dlam, dwr, dwi = odd_gates_bwd(xb1, r1, ig1, lam_grad, hseq, keep, wr, wi, lam)
    dx3, dcw, dg_od, g_od_win = odd_pre_bwd(x3, dx4, z1, dxb, dgate, od_g, cw, W['od_w_in'][0])
    G['od_w_in'] = g_od_win[None]
    G['od_norm'], G['od_conv_w'], G['od_conv_b'] = dg_od, dcw[None], dcb
    G['od_w_rgate'], G['od_b_rgate'], G['od_w_igate'], G['od_b_igate'], G['od_lambda'] = (
        dwr[None], dbr, dwi[None], dbi, dlam)

    dx1 = xa_ffn_bwd(dx3, x1, x2, memkv0, 0)
    G['xa_norm_x'], G['xa_norm_mem'], G['ffn_norm'] = jnp.stack(gnx), jnp.stack(gnm), jnp.stack(gfn)
    G['xa_w_q'], G['xa_w_kv'], G['xa_w_o'] = jnp.stack(gwq), jnp.stack(gwkv), jnp.stack(gwo)
    G['ffn_w_gate_up'], G['ffn_w_down'] = jnp.stack(gwgu, axis=1), jnp.stack(gwd, axis=1)
    bwd_token = ship_grads(G) if ship_grads is not None else None

    dyp, do_att, delta, g_wo_pool, g_wo_att = even_post_bwd(dx1, ypool, o_att, ev_wo_pool, ev_wo_att)
    G['ev_w_out'] = jnp.concatenate([g_wo_pool, _t(_unpad_heads(_t(g_wo_att), MLA_HEADS, 64, 0))], axis=0)[None]
    dq, dk, dv = attn_bwd(qp, kp, vp, do_att, lse, delta, bwd_token)
    (grad_x, dg_ev, dpw, dps, dqg, dwq, dkvg, dwk, dwv, g_win) = even_pre_bwd(
        x, dx1, z0, dq, dk, dv, dyp, tabs, ev_g, ev_win, pw, ps, qg, ev_wq, kvg, ev_wk, ev_wv)
    G['ev_w_in'] = jnp.concatenate([g_win[:, :896], _unpad_heads(g_win[:, 896:], 1, 32, 64)], axis=1)[None]
    G['ev_norm'], G['ev_pool_w'], G['ev_pool_scale'], G['ev_q_norm'], G['ev_kv_norm'] = (
        dg_ev, dpw[None], dps, dqg, dkvg)
    G['ev_w_q_up'] = _unpad_heads(dwq, MLA_HEADS, QK_DIM, 0)[None]
    gk = _unpad_heads(dwk, MLA_HEADS, 64, 0).reshape(KV_LORA, MLA_HEADS, 64)
    gv = _unpad_heads(dwv, MLA_HEADS, 64, 0).reshape(KV_LORA, MLA_HEADS, 64)
    G['ev_w_kv_up'] = jnp.concatenate([gk, gv], axis=2).reshape(1, KV_LORA, MLA_HEADS * 128)
    return loss[0, 0], grad_x, G


def kernel(x, mem, positions, ev_norm, ev_w_in, ev_pool_w, ev_pool_scale, ev_q_norm, ev_w_q_up, ev_kv_norm, ev_w_kv_up, ev_w_out, od_norm, od_w_in, od_conv_w, od_conv_b, od_w_rgate, od_b_rgate, od_w_igate, od_b_igate, od_lambda, od_w_out, xa_norm_x, xa_norm_mem, xa_w_q, xa_w_kv, xa_w_o, ffn_norm, ffn_w_gate_up, ffn_w_down, final_norm, loss_target, m_ev_norm, m_ev_w_in, m_ev_pool_w, m_ev_pool_scale, m_ev_q_norm, m_ev_w_q_up, m_ev_kv_norm, m_ev_w_kv_up, m_ev_w_out, m_od_norm, m_od_w_in, m_od_conv_w, m_od_conv_b, m_od_w_rgate, m_od_b_rgate, m_od_w_igate, m_od_b_igate, m_od_lambda, m_od_w_out, m_xa_norm_x, m_xa_norm_mem, m_xa_w_q, m_xa_w_kv, m_xa_w_o, m_ffn_norm, m_ffn_w_gate_up, m_ffn_w_down, m_final_norm, v_ev_norm, v_ev_w_in, v_ev_pool_w, v_ev_pool_scale, v_ev_q_norm, v_ev_w_q_up, v_ev_kv_norm, v_ev_w_kv_up, v_ev_w_out, v_od_norm, v_od_w_in, v_od_conv_w, v_od_conv_b, v_od_w_rgate, v_od_b_rgate, v_od_w_igate, v_od_b_igate, v_od_lambda, v_od_w_out, v_xa_norm_x, v_xa_norm_mem, v_xa_w_q, v_xa_w_kv, v_xa_w_o, v_ffn_norm, v_ffn_w_gate_up, v_ffn_w_down, v_final_norm):
    args = dict(locals())
    w = {n: args[n] for n in WEIGHTS}
    m = {n: args['m_' + n] for n in WEIGHTS}
    v = {n: args['v_' + n] for n in WEIGHTS}
    big = [n for n in SHARDED if n not in SMALL_F32]
    small = [n for n in SHARDED if n in SMALL_F32]

    small_shapes = [w[n].shape for n in small]
    first = [n for n in big if n.startswith('ev_')]
    late = [n for n in big if n not in first]

    def full(n, st):
        return st if n in STACKED else _to_full(st, SHARD_AXIS[n])

    W = {n: w[n] for n in REPLICATED}
    W.update((n, full(n, st)) for n, st in zip(first, all_gather("gather_ev_weights", [w[n].astype(BF16) for n in first])))
    gather = split_start("gather_start", [w[n].astype(BF16) for n in late] + [_pack_rows([w[n] for n in small])], True)

    def late_weights(after):
        srcs, lands = split_wait("gather_wait", gather, after, True)
        lands = [_fill_own_slot(s, l, True) for s, l in zip(srcs, lands)]
        out = {n: full(n, st) for n, st in zip(late, lands)}
        out.update((n, _to_full(st, SHARD_AXIS[n])) for n, st in zip(small, _unpack_rows(lands[-1], small_shapes, True)))
        return out

    def shards(G, n):
        return G[n] if n in STACKED else _to_shards(G[n], SHARD_AXIS[n])

    shipped = []

    def ship_grads(G):
        shipped.append(split_start("exchange_start", [shards(G, n).astype(BF16) for n in late] +
                                   [_pack_rows([shards(G, n) for n in small], lead=True)], False))
        return shipped[0][-1]

    loss, grad_x, G = device_step(x[0], mem[0], positions[0], loss_target[0], W, gather[-1], late_weights, ship_grads)
    outs = [{}, {}, {}, {}]

    rep_shapes = [w[n].shape for n in REPLICATED] + [(LANES,)]
    zero = jnp.zeros((LANES,), F32)
    *first_parts, rep_parts = exchange(
        "exchange_ev_and_rep_grads", [shards(G, n).astype(BF16) for n in first],
        [_pack([G[n] for n in REPLICATED] + [jnp.broadcast_to(loss, (LANES,))], F32)])
    rep = adamw("adamw_rep", rep_parts, *[_pack([d[n] for n in REPLICATED] + [zero], F32) for d in (w, m, v)])
    for k in range(4):
        outs[k].update(zip(REPLICATED + ['loss'], _unpack(rep[k], rep_shapes)))
    loss = outs[0]['loss'][0]

    srcs, lands = split_wait("exchange_wait", shipped[0], grad_x, False)
    late_parts = [_fill_own_slot(s, l, False) for s, l in zip(srcs, lands)]
    parts = first_parts + late_parts
    two_d = lambda a: a.reshape(-1, a.shape[-1])
    for n, p in zip(first + late, parts):
        res = adamw("adamw_" + n, p.reshape((N_DEV,) + two_d(w[n]).shape), two_d(w[n]), two_d(m[n]), two_d(v[n]))
        for k in range(4):
            outs[k][n] = res[k].reshape(w[n].shape)
    res = adamw("adamw_small", parts[-1], *[_pack_rows([d[n] for n in small]) for d in (w, m, v)])
    for k in range(4):
        outs[k].update(zip(small, _unpack_rows(res[k], small_shapes)))

    return (loss, grad_x[None], *[outs[0][n] for n in WEIGHTS], *[outs[1][n] for n in WEIGHTS],
            *[outs[2][n] for n in WEIGHTS], *[outs[3][n] for n in WEIGHTS])
```

```python
import functools

import jax
import jax.numpy as jnp
from jax import lax
from jax.experimental import pallas as pl
from jax.experimental.pallas import tpu as pltpu

F32, BF16 = jnp.float32, jnp.bfloat16
N_DEV = 8
D = 1024
POOL_DIM = 512
POOL_WINDOWS = (2, 4, 8, 16)
MLA_HEADS = 8
QK_DIM = 96
Q_LORA, KV_LORA = 256, 128
LRU_HEADS, LRU_HEAD_DIM = 4, 256
LRU_C = 8.0
MEM_HEADS, MEM_HEAD_DIM = 4, 256
D_FF = 2816
RMS_EPS = 1e-6
ADAM_LR, ADAM_B1, ADAM_B2, ADAM_EPS, ADAM_WD, ADAM_STEP = 0.001, 0.9, 0.999, 1e-08, 0.01, 10
LANES = 128
POOL_HALO = 16
CONV_HALO = 8
VMEM_LIMIT = 60000 * 1024


def _cp():
    return pltpu.CompilerParams(dimension_semantics=("arbitrary",), vmem_limit_bytes=VMEM_LIMIT)


def _cp2():
    return pltpu.CompilerParams(dimension_semantics=("arbitrary", "arbitrary"), vmem_limit_bytes=VMEM_LIMIT)


def _row(ts, c, col=0):
    return pl.BlockSpec((ts, c), lambda i: (i, col))


def _prev(hr, c, ts, col=0):
    r = ts // hr
    return pl.BlockSpec((hr, c), lambda i: (jnp.maximum(i * r - 1, 0), col))


def _next(hr, c, ts, n, col=0):
    r = ts // hr
    return pl.BlockSpec((hr, c), lambda i: (jnp.minimum((i + 1) * r, n * r - 1), col))


def _const(shape):
    nd = len(shape)
    return pl.BlockSpec(tuple(shape), lambda i: (0,) * nd, pipeline_mode=pl.Buffered(1))


def _acc(shape):
    nd = len(shape)
    return pl.BlockSpec(tuple(shape), lambda i: (0,) * nd)


def _sds(shape, dt):
    return jax.ShapeDtypeStruct(tuple(shape), dt)


def _dot(a, b):
    return jnp.dot(a.astype(BF16), b.astype(BF16), preferred_element_type=F32)


def _dot_nt(a, b):
    return lax.dot_general(a.astype(BF16), b.astype(BF16), (((1,), (1,)), ((), ())), preferred_element_type=F32)


def _dot_tn(a, b):
    return lax.dot_general(a.astype(BF16), b.astype(BF16), (((0,), (0,)), ((), ())), preferred_element_type=F32)


def _rms(x, g):
    rstd = lax.rsqrt(jnp.mean(x * x, axis=-1, keepdims=True) + RMS_EPS)
    return x * rstd * g, rstd


def _rms_bwd(x, g, rstd, dy):
    xn = x * rstd
    dyg = dy * g
    dx = rstd * (dyg - xn * jnp.mean(dyg * xn, axis=-1, keepdims=True))
    return dx, dy * xn


def _rowsum(v):
    return jnp.sum(v, axis=0, keepdims=True)


def _roll(v, s, axis):
    n = v.shape[axis]
    return pltpu.roll(v, s % n, axis)


def _rope(t, c, a, b):
    k = t.shape[1] // LANES
    if k > 1:
        c, a, b = (jnp.tile(v, (1, k)) for v in (c, a, b))
    return t * c + _roll(t, 16, 1) * a + _roll(t, -16, 1) * b


def _rope_bwd(d, c, a, b):
    k = d.shape[1] // LANES
    if k > 1:
        c, a, b = (jnp.tile(v, (1, k)) for v in (c, a, b))
    return d * c + _roll(d * a, -16, 1) + _roll(d * b, 16, 1)


def _gelu(x):
    c = 0.7978845608028654
    t = jnp.tanh(c * (x + 0.044715 * x * x * x))
    return 0.5 * x * (1.0 + t), t


def _gelu_grad(x, t):
    c = 0.7978845608028654
    return 0.5 * (1.0 + t) + 0.5 * x * (1.0 - t * t) * c * (1.0 + 3.0 * 0.044715 * x * x)


def _blockdot(v, w_ref, nblk, width):
    return jnp.concatenate(
        [_dot(v[:, j * width:(j + 1) * width], w_ref[j]) for j in range(nblk)], axis=1)


def _pool_cnt(row0, rows):
    t = row0 + lax.broadcasted_iota(jnp.int32, (rows, POOL_DIM), 0)
    w = jnp.left_shift(2, lax.broadcasted_iota(jnp.int32, (rows, POOL_DIM), 1) // LANES)
    return jnp.minimum(t + 1, w).astype(F32)


def _pool_windows(ext, sign):
    s2 = ext + _roll(ext, sign * 1, 0)
    t = s2[:, LANES:]
    s4 = t + _roll(t, sign * 2, 0)
    t = s4[:, LANES:]
    s8 = t + _roll(t, sign * 4, 0)
    t = s8[:, LANES:]
    s16 = t + _roll(t, sign * 8, 0)
    return jnp.concatenate([s2[:, :LANES], s4[:, :LANES], s8[:, :LANES], s16], axis=1)


def _pooled(uprev, u, row0):
    ts = u.shape[0]
    ext = jnp.concatenate([uprev, u], axis=0)
    sums = _pool_windows(ext, 1)[POOL_HALO:]
    return sums / _pool_cnt(row0, ts) - u


def _expm1(x):
    return jnp.where(jnp.abs(x) < 0.01, x * (1.0 + 0.5 * x * (1.0 + x * (1.0 / 3.0))), jnp.exp(x) - 1.0)


def _softplus(z):
    return jnp.maximum(z, 0.0) + jnp.log1p(jnp.exp(-jnp.abs(z)))


def _tile_rows(s, want):
    while s % want:
        want //= 2
    return want


def even_pre(x, tabs, g, win, pw, pscale, qg, wq, kvg, wk, wv):
    S = x.shape[0]
    ts = _tile_rows(S, 512)

    def body(x_ref, xp_ref, c_ref, a_ref, b_ref, g_ref, win_ref, pw_ref, ps_ref, qg_ref, wq_ref, kvg_ref,
             wk_ref, wv_ref, z_ref, q_ref, k_ref, v_ref, yp_ref):
        i = pl.program_id(0)
        h, _ = _rms(x_ref[...], g_ref[...])
        z = _dot(h, win_ref[...])
        z_ref[...] = z
        hp, _ = _rms(xp_ref[...], g_ref[...])
        uprev = _dot(hp, win_ref[:, :POOL_DIM]) * (i > 0).astype(F32)
        u = z[:, :POOL_DIM]
        pooled = _pooled(uprev, u, i * ts)
        yp_ref[...] = (_blockdot(pooled, pw_ref, 4, LANES) * ps_ref[...]).astype(BF16)
        c, a, b = c_ref[...], a_ref[...], b_ref[...]
        cqn, _ = _rms(z[:, 512:768], qg_ref[...])
        q_ref[...] = (_rope(_dot(cqn, wq_ref[...]), c, a, b) * (ATTN_SCALE * LOG2_E)).astype(BF16)
        ckvn, _ = _rms(z[:, 768:896], kvg_ref[...])
        krr = _rope(z[:, 896:1024], c, a, b)
        k_ref[...] = (_dot(ckvn, wk_ref[...]) + jnp.tile(krr, (1, MLA_HEADS))).astype(BF16)
        lane = lax.broadcasted_iota(jnp.int32, (ts, D), 1) % LANES
        v_ref[...] = jnp.where(lane == ONES_LANE, 1.0, _dot(ckvn, wv_ref[...])).astype(BF16)

    ins = [x, x, *tabs, g, win, pw, pscale, qg, wq, kvg, wk, wv]
    in_specs = [_row(ts, D), _prev(POOL_HALO, D, ts), _row(ts, LANES), _row(ts, LANES), _row(ts, LANES)]
    in_specs += [_const(v.shape) for v in ins[5:]]
    return pl.pallas_call(
        body, name="even_pre", grid=(S // ts,), in_specs=in_specs,
        out_specs=[_row(ts, D)] * 4 + [_row(ts, POOL_DIM)],
        out_shape=[_sds((S, D), F32)] + [_sds((S, D), BF16)] * 3 + [_sds((S, POOL_DIM), BF16)],
        compiler_params=_cp())(*ins)


ATTN_SCALE = QK_DIM ** -0.5
LOG2_E = 1.4426950408889634
LN_2 = 0.6931471805599453
ONES_LANE = 64


def _exp2(x):
    return jnp.exp2(x)


def _pair_loop(lo, hi, step, init, unrolls=(2, 1)):
    carry = init
    for unroll in unrolls:
        groups = (hi - lo) // unroll

        def group(j, c, lo=lo, unroll=unroll):
            for u in range(unroll):
                c = step(lo + unroll * j + u, c)
            return c

        carry = lax.fori_loop(0, groups, group, carry)
        lo = lo + unroll * groups
    return carry


def _as_row(col):
    return jnp.transpose(jnp.broadcast_to(col, (col.shape[0], LANES)))[0:1, :]


def _after(token):
    return ([], []) if token is None else ([token], [pl.BlockSpec(memory_space=pl.ANY)])


def attn_fwd(qp, kp, vp, token=None):
    S = qp.shape[0]
    tq = _tile_rows(S, 512)
    extra, extra_specs = _after(token)

    def body(q_ref, k_ref, v_ref, *rest):
        o_ref, lse_ref = rest[-2:]
        qi = pl.program_id(1)
        q = q_ref[...]

        def block(ki, carry, masked):
            m, acc = carry
            off = pl.multiple_of(ki * tq, tq)
            s = _dot_nt(q, k_ref[pl.ds(off, tq), :])
            if masked:
                row = lax.broadcasted_iota(jnp.int32, (tq, tq), 0)
                col = lax.broadcasted_iota(jnp.int32, (tq, tq), 1)
                s = jnp.where(col <= row, s, -1e30)
            m_new = jnp.maximum(m, jnp.max(s, axis=1, keepdims=True))
            acc = _exp2(m - m_new) * acc + _dot(_exp2(s - m_new), v_ref[pl.ds(off, tq), :])
            return m_new, acc

        init = (jnp.full((tq, 1), -1e30, F32), jnp.zeros((tq, LANES), F32))
        carry = _pair_loop(0, qi, lambda ki, c: block(ki, c, False), init, unrolls=(8, 4, 2, 1))
        m, acc = block(qi, carry, True)
        l = acc[:, ONES_LANE:ONES_LANE + 1]
        o_ref[...] = acc / l
        lse_ref[...] = _as_row(m + jnp.log(l) * LOG2_E)

    blk = pl.BlockSpec((tq, LANES), lambda h, i: (i, h))
    full = pl.BlockSpec((S, LANES), lambda h, i: (0, h))
    return pl.pallas_call(
        body, name="attn_fwd", grid=(MLA_HEADS, S // tq), in_specs=[blk, full, full] + extra_specs,
        out_specs=[blk, pl.BlockSpec((None, None, 1, tq), lambda h, i: (h, i, 0, 0))],
        out_shape=[_sds((S, D), F32), _sds((MLA_HEADS, S // tq, 1, tq), F32)], compiler_params=_cp2())(
            qp, kp, vp, *extra)


def even_post(x, ypool, o, wo_pool, wo_att):
    S = x.shape[0]
    ts = _tile_rows(S, 512)

    def body(x_ref, yp_ref, o_ref, wp_ref, wa_ref, out_ref):
        out_ref[...] = x_ref[...] + _dot(yp_ref[...], wp_ref[...]) + _dot(o_ref[...], wa_ref[...])

    return pl.pallas_call(
        body, name="even_post", grid=(S // ts,),
        in_specs=[_row(ts, D), _row(ts, POOL_DIM), _row(ts, D), _const(wo_pool.shape), _const(wo_att.shape)],
        out_specs=_row(ts, D), out_shape=_sds((S, D), F32), compiler_params=_cp())(x, ypool, o, wo_pool, wo_att)


def mem_kv(mem, g, wkv):
    M = mem.shape[0]

    def body(mem_ref, g_ref, w_ref, mn_ref, k_ref, v_ref):
        mn, _ = _rms(mem_ref[...], g_ref[...])
        mn_ref[...] = mn.astype(BF16)
        k_ref[...] = _dot(mn, w_ref[:, :D]).astype(BF16)
        v_ref[...] = _dot(mn, w_ref[:, D:]).astype(BF16)

    return pl.pallas_call(
        body, name="mem_kv", grid=(1,), in_specs=[_acc(mem.shape), _acc(g.shape), _acc(wkv.shape)],
        out_specs=[_acc((M, D))] * 3, out_shape=[_sds((M, D), BF16)] * 3, compiler_params=_cp())(mem, g, wkv)


def _xattn_heads(hx, wq_ref, k_ref, v_ref):
    q = _dot(hx, wq_ref[...])
    scale = MEM_HEAD_DIM ** -0.5
    ps, os_ = [], []
    for h in range(MEM_HEADS):
        sl = slice(h * MEM_HEAD_DIM, (h + 1) * MEM_HEAD_DIM)
        s = _dot_nt(q[:, sl], k_ref[:, sl]) * scale
        e = jnp.exp(s - jnp.max(s, axis=1, keepdims=True))
        p = e / jnp.sum(e, axis=1, keepdims=True)
        ps.append(p)
        os_.append(_dot(p, v_ref[:, sl]))
    return q, ps, jnp.concatenate(os_, axis=1)


def xattn_fwd(x, g, wq, kmem, vmem, wo):
    S = x.shape[0]
    ts = _tile_rows(S, 512)

    def body(x_ref, g_ref, wq_ref, k_ref, v_ref, wo_ref, out_ref):
        x_ = x_ref[...]
        hx, _ = _rms(x_, g_ref[...])
        _, _, o = _xattn_heads(hx, wq_ref, k_ref, v_ref)
        out_ref[...] = x_ + _dot(o, wo_ref[...])

    ins = [x, g, wq, kmem, vmem, wo]
    return pl.pallas_call(
        body, name="xattn_fwd", grid=(S // ts,), in_specs=[_row(ts, D)] + [_const(v.shape) for v in ins[1:]],
        out_specs=_row(ts, D), out_shape=_sds((S, D), F32), compiler_params=_cp())(*ins)


def xattn_bwd(x, dy, g, wq, kmem, vmem, wo):
    S = x.shape[0]
    M = kmem.shape[0]
    ts = _tile_rows(S, 512)
    scale = MEM_HEAD_DIM ** -0.5

    def body(x_ref, dy_ref, g_ref, wq_ref, k_ref, v_ref, wo_ref,
             dx_ref, o_ref, dq_ref, hx_ref, dg_ref, dk_ref, dv_ref):
        i = pl.program_id(0)

        @pl.when(i == 0)
        def _():
            dg_ref[...] = jnp.zeros_like(dg_ref)
            dk_ref[...] = jnp.zeros_like(dk_ref)
            dv_ref[...] = jnp.zeros_like(dv_ref)

        x_, dy_ = x_ref[...], dy_ref[...]
        hx, rstd = _rms(x_, g_ref[...])
        q, ps, o = _xattn_heads(hx, wq_ref, k_ref, v_ref)
        hx_ref[...] = hx.astype(BF16)
        o_ref[...] = o.astype(BF16)
        do = _dot_nt(dy_, wo_ref[...])
        dqs = []
        for h in range(MEM_HEADS):
            sl = slice(h * MEM_HEAD_DIM, (h + 1) * MEM_HEAD_DIM)
            p, do_h = ps[h], do[:, sl]
            dp = _dot_nt(do_h, v_ref[:, sl])
            ds = p * (dp - jnp.sum(p * dp, axis=1, keepdims=True)) * scale
            dqs.append(_dot(ds, k_ref[:, sl]))
            dk_ref[:, sl] += _dot_tn(ds, q[:, sl])
            dv_ref[:, sl] += _dot_tn(p, do_h)
        dq = jnp.concatenate(dqs, axis=1).astype(BF16)
        dq_ref[...] = dq
        dxn, dgr = _rms_bwd(x_, g_ref[...], rstd, _dot_nt(dq, wq_ref[...]))
        dx_ref[...] = dy_ + dxn
        dg_ref[...] += _rowsum(dgr)

    ins = [x, dy, g, wq, kmem, vmem, wo]
    return pl.pallas_call(
        body, name="xattn_bwd", grid=(S // ts,),
        in_specs=[_row(ts, D), _row(ts, D)] + [_const(v.shape) for v in ins[2:]],
        out_specs=[_row(ts, D)] * 4 + [_acc((1, D)), _acc((M, D)), _acc((M, D))],
        out_shape=[_sds((S, D), F32)] + [_sds((S, D), BF16)] * 3 + [_sds((1, D), F32), _sds((M, D), F32),
                                                                    _sds((M, D), F32)],
        compiler_params=_cp())(*ins)


def mem_bwd(mem, g, dk, dv, wkv):
    M = mem.shape[0]

    def body(mem_ref, g_ref, dk_ref, dv_ref, w_ref, dkv_ref, dg_ref):
        dkv = jnp.concatenate([dk_ref[...], dv_ref[...]], axis=1)
        dkv_ref[...] = dkv.astype(BF16)
        _, rstd = _rms(mem_ref[...], g_ref[...])
        dg_ref[...] = _rowsum(_dot_nt(dkv, w_ref[...]) * (mem_ref[...] * rstd))

    ins = [mem, g, dk, dv, wkv]
    return pl.pallas_call(
        body, name="mem_bwd", grid=(1,), in_specs=[_acc(v.shape) for v in ins],
        out_specs=[_acc((M, 2 * D)), _acc((1, D))], out_shape=[_sds((M, 2 * D), BF16), _sds((1, D), F32)],
        compiler_params=_cp())(*ins)


FF_CHUNK = 2 * D_FF // N_DEV
FF_HALF = N_DEV // 2


def _layer_of(w, layer):
    return pl.BlockSpec((N_DEV, None) + w.shape[2:], lambda i: (0, layer, 0, 0), pipeline_mode=pl.Buffered(1))


def _ff_chunks(c, ts):
    return pl.BlockSpec((c, ts, FF_CHUNK), lambda i: (0, i, 0))


def _ffn(x_, g_ref, wgu_ref, wd_ref, hf_ref, gu_ref):
    hf = _rms(x_, g_ref[...])[0].astype(BF16)
    hf_ref[...] = hf
    out = x_
    for j in range(FF_HALF):
        gg, uu = _dot(hf, wgu_ref[j]), _dot(hf, wgu_ref[j + FF_HALF])
        gu_ref[j] = gg.astype(BF16)
        gu_ref[j + FF_HALF] = uu.astype(BF16)
        out = out + _dot(gg * jax.nn.sigmoid(gg) * uu, wd_ref[j])
    return out


def ffn_fwd(x, g, wgu, layer, wd):
    S = x.shape[0]
    ts = _tile_rows(S, 256)

    def body(x_ref, g_ref, wgu_ref, wd_ref, out_ref, hf_ref, gu_ref):
        out_ref[...] = _ffn(x_ref[...], g_ref, wgu_ref, wd_ref, hf_ref, gu_ref)

    return pl.pallas_call(
        body, name="ffn_fwd", grid=(S // ts,),
        in_specs=[_row(ts, D), _const(g.shape), _layer_of(wgu, layer), _const(wd.shape)],
        out_specs=[_row(ts, D), _row(ts, D), _ff_chunks(N_DEV, ts)],
        out_shape=[_sds((S, D), F32), _sds((S, D), BF16), _sds((N_DEV, S, FF_CHUNK), BF16)],
        compiler_params=_cp())(x, g, wgu, wd)


def ffn_fwd_loss(x, g, wgu, layer, wd, target, gf):
    S = x.shape[0]
    ts = _tile_rows(S, 256)

    def body(x_ref, g_ref, wgu_ref, wd_ref, t_ref, gf_ref, dx_ref, dgf_ref, loss_ref, hf_ref, gu_ref):
        @pl.when(pl.program_id(0) == 0)
        def _():
            dgf_ref[...] = jnp.zeros_like(dgf_ref)
            loss_ref[...] = jnp.zeros_like(loss_ref)

        out = _ffn(x_ref[...], g_ref, wgu_ref, wd_ref, hf_ref, gu_ref)
        y, rstd = _rms(out, gf_ref[...])
        err = y - t_ref[...]
        loss_ref[...] += 0.5 * _rowsum(jnp.mean(err * err, axis=1, keepdims=True))
        dxn, dgr = _rms_bwd(out, gf_ref[...], rstd, err * (1.0 / D))
        dx_ref[...] = dxn
        dgf_ref[...] += _rowsum(dgr)

    return pl.pallas_call(
        body, name="ffn_fwd_loss", grid=(S // ts,),
        in_specs=[_row(ts, D), _const(g.shape), _layer_of(wgu, layer), _const(wd.shape), _row(ts, D),
                  _const(gf.shape)],
        out_specs=[_row(ts, D), _acc((1, D)), _acc((1, 1)), _row(ts, D), _ff_chunks(N_DEV, ts)],
        out_shape=[_sds((S, D), F32), _sds((1, D), F32), _sds((1, 1), F32), _sds((S, D), BF16),
                   _sds((N_DEV, S, FF_CHUNK), BF16)],
        compiler_params=_cp())(x, g, wgu, wd, target, gf)


def ffn_bwd(x, dy, gu, g, wgu, layer, wd):
    S = x.shape[0]
    ts = _tile_rows(S, 256)

    def body(x_ref, dy_ref, gu_ref, g_ref, wgu_ref, wd_ref, dx_ref, dg_ref, act_ref, dgu_ref):
        @pl.when(pl.program_id(0) == 0)
        def _():
            dg_ref[...] = jnp.zeros_like(dg_ref)

        dy_ = dy_ref[...]
        dyb = dy_.astype(BF16)
        dh = jnp.zeros((ts, D), F32)
        dacts = [_dot_nt(dyb, wd_ref[j]) for j in range(FF_HALF)]
        for j in range(FF_HALF):
            gg, uu = gu_ref[j].astype(F32), gu_ref[j + FF_HALF].astype(F32)
            sg = jax.nn.sigmoid(gg)
            silu = gg * sg
            act_ref[j] = (silu * uu).astype(BF16)
            dact = dacts[j]
            dgate = (dact * uu * (sg * (1.0 + gg * (1.0 - sg)))).astype(BF16)
            dup = (dact * silu).astype(BF16)
            dgu_ref[j] = dgate
            dgu_ref[j + FF_HALF] = dup
            dh = dh + _dot_nt(dgate, wgu_ref[j]) + _dot_nt(dup, wgu_ref[j + FF_HALF])
        x_ = x_ref[...]
        _, rstd = _rms(x_, g_ref[...])
        dxn, dgr = _rms_bwd(x_, g_ref[...], rstd, dh)
        dx_ref[...] = dy_ + dxn
        dg_ref[...] += _rowsum(dgr)

    return pl.pallas_call(
        body, name="ffn_bwd", grid=(S // ts,),
        in_specs=[_row(ts, D), _row(ts, D), _ff_chunks(N_DEV, ts), _const(g.shape), _layer_of(wgu, layer),
                  _const(wd.shape)],
        out_specs=[_row(ts, D), _acc((1, D)), _ff_chunks(FF_HALF, ts), _ff_chunks(N_DEV, ts)],
        out_shape=[_sds((S, D), F32), _sds((1, D), F32), _sds((FF_HALF, S, FF_CHUNK), BF16),
                   _sds((N_DEV, S, FF_CHUNK), BF16)],
        compiler_params=_cp())(x, dy, gu, g, wgu, wd)


def _conv_fwd(xprev, xbp, cw_ref, cb):
    ext = jnp.concatenate([xprev, xbp], axis=0)
    acc = cb + cw_ref[3:4, :] * xbp
    for k in range(3):
        acc = acc + cw_ref[k:k + 1, :] * _roll(ext, 3 - k, 0)[CONV_HALO:]
    return acc


def _decay(r, lam):
    sp = _softplus(-lam)
    log_a = -LRU_C * r * sp
    return sp, jnp.exp(log_a), jnp.sqrt(jnp.maximum(-_expm1(2.0 * log_a), 0.0))


def odd_pre(x, keep, g, win, cw, cb, wr, br, wi, bi, lam):
    S = x.shape[0]
    ts = _tile_rows(S, 512)

    def body(x_ref, xp_ref, keep_ref, g_ref, win_ref, cw_ref, cb_ref, wr_ref, br_ref, wi_ref, bi_ref, lam_ref,
             z_ref, a_ref, b_ref, xb_ref, r_ref, ig_ref):
        i = pl.program_id(0)
        h, _ = _rms(x_ref[...], g_ref[...])
        z = _dot(h, win_ref[...])
        z_ref[...] = z
        hp, _ = _rms(xp_ref[...], g_ref[...])
        xprev = _dot(hp, win_ref[:, D:]) * (i > 0).astype(F32)
        xb = _conv_fwd(xprev, z[:, D:], cw_ref, cb_ref[...])
        xb_ref[...] = xb
        r = jax.nn.sigmoid(_blockdot(xb, wr_ref, LRU_HEADS, LRU_HEAD_DIM) + br_ref[...])
        ig = jax.nn.sigmoid(_blockdot(xb, wi_ref, LRU_HEADS, LRU_HEAD_DIM) + bi_ref[...])
        r_ref[...] = r
        ig_ref[...] = ig
        keep_ = keep_ref[...]
        _, a, mult = _decay(r, lam_ref[...])
        a_ref[...] = a * keep_
        b_ref[...] = jnp.where(keep_ > 0.0, mult, 1.0) * (ig * xb)

    ins = [x, x, keep, g, win, cw, cb, wr, br, wi, bi, lam]
    return pl.pallas_call(
        body, name="odd_pre", grid=(S // ts,),
        in_specs=[_row(ts, D), _prev(CONV_HALO, D, ts), _row(ts, 1)] + [_const(v.shape) for v in ins[3:]],
        out_specs=[_row(ts, 2 * D)] + [_row(ts, D)] * 5,
        out_shape=[_sds((S, 2 * D), F32)] + [_sds((S, D), F32)] * 5, compiler_params=_cp())(*ins)


def lru_scan(a, b, reverse=False):
    S = a.shape[0]
    ts = _tile_rows(S, 512)
    n = S // ts
    groups = ts // 8

    def body(a_ref, an_ref, b_ref, h_ref, carry_ref, ash_ref):
        i = pl.program_id(0)

        @pl.when(i == 0)
        def _():
            carry_ref[...] = jnp.zeros_like(carry_ref)

        rid = lax.broadcasted_iota(jnp.int32, (8, D), 0)
        if reverse:
            ext = jnp.concatenate([a_ref[...], an_ref[...] * (i > 0).astype(F32)], axis=0)
            ash_ref[...] = _roll(ext, -1, 0)[:ts]
        src = ash_ref if reverse else a_ref

        def group(j, carry):
            off = pl.multiple_of((groups - 1 - j if reverse else j) * 8, 8)
            a8, b8 = src[pl.ds(off, 8), :], b_ref[pl.ds(off, 8), :]
            for k in (1, 2, 4):
                inside = (rid < 8 - k) if reverse else (rid >= k)
                sh = -k if reverse else k
                a_sh = jnp.where(inside, _roll(a8, sh, 0), 1.0)
                b_sh = jnp.where(inside, _roll(b8, sh, 0), 0.0)
                b8 = a8 * b_sh + b8
                a8 = a8 * a_sh
            h8 = a8 * carry + b8
            h_ref[pl.ds(off, 8), :] = h8
            return h8[0:1, :] if reverse else h8[7:8, :]

        carry_ref[...] = lax.fori_loop(0, groups, group, carry_ref[...], unroll=4)

    if reverse:
        r = ts // 8
        tile = pl.BlockSpec((ts, D), lambda i: (n - 1 - i, 0))
        halo = pl.BlockSpec((8, D), lambda i: (jnp.minimum((n - i) * r, n * r - 1), 0))
    else:
        tile, halo = _row(ts, D), _prev(8, D, ts)
    return pl.pallas_call(
        body, name="lru_scan_rev" if reverse else "lru_scan", grid=(n,), in_specs=[tile, halo, tile],
        out_specs=tile, out_shape=_sds((S, D), F32),
        scratch_shapes=[pltpu.VMEM((1, D), F32), pltpu.VMEM((ts, D), F32)], compiler_params=_cp())(a, a, b)


def odd_post(x, z, hseq, wout):
    S = x.shape[0]
    ts = _tile_rows(S, 512)

    def body(x_ref, gate_ref, h_ref, w_ref, out_ref):
        gl, _ = _gelu(gate_ref[...])
        out_ref[...] = x_ref[...] + _dot(gl * h_ref[...], w_ref[...])

    return pl.pallas_call(
        body, name="odd_post", grid=(S // ts,),
        in_specs=[_row(ts, D), _row(ts, D), _row(ts, D), _const(wout.shape)],
        out_specs=_row(ts, D), out_shape=_sds((S, D), F32), compiler_params=_cp())(x, z, hseq, wout)


def _accumulate_tn(acc_ref, out_ref, a, b, steps):
    i = pl.program_id(0)

    @pl.when(i == 0)
    def _():
        acc_ref[...] = jnp.zeros_like(acc_ref)

    acc_ref[...] += _dot_tn(a, b)

    @pl.when(i == steps - 1)
    def _():
        out_ref[...] = acc_ref[...].astype(out_ref.dtype)


def odd_post_bwd(dy, z, hseq, wout):
    S = dy.shape[0]
    ts = _tile_rows(S, 512)
    n = S // ts

    def body(dy_ref, gate_ref, h_ref, w_ref, dgate_ref, dh_ref, dw_ref, acc_ref):
        gate, hs, dy_ = gate_ref[...], h_ref[...], dy_ref[...]
        gl, t = _gelu(gate)
        dyy = _dot_nt(dy_, w_ref[...])
        dgate_ref[...] = dyy * hs * _gelu_grad(gate, t)
        dh_ref[...] = dyy * gl
        _accumulate_tn(acc_ref, dw_ref, gl * hs, dy_, n)

    return pl.pallas_call(
        body, name="odd_post_bwd", grid=(n,),
        in_specs=[_row(ts, D), _row(ts, D), _row(ts, D), _const(wout.shape)],
        out_specs=[_row(ts, D), _row(ts, D), _acc((D, D))],
        out_shape=[_sds((S, D), F32), _sds((S, D), F32), _sds((D, D), BF16)],
        scratch_shapes=[pltpu.VMEM((D, D), F32)], compiler_params=_cp())(dy, z, hseq, wout)


def odd_gates_bwd(xb, r, ig, lam_grad, hseq, keep, wr, wi, lam):
    S = xb.shape[0]
    ts = _tile_rows(S, 512)

    def body(xb_ref, r_ref, ig_ref, lg_ref, h_ref, hp_ref, keep_ref, wr_ref, wi_ref, lam_ref,
             dxb_ref, dcb_ref, dbr_ref, dbi_ref, dlam_ref, dwr_ref, dwi_ref):
        i = pl.program_id(0)

        @pl.when(i == 0)
        def _():
            for ref in (dcb_ref, dbr_ref, dbi_ref, dlam_ref, dwr_ref, dwi_ref):
                ref[...] = jnp.zeros_like(ref)

        first = (i > 0).astype(F32)
        xb, r, ig = xb_ref[...], r_ref[...], ig_ref[...]
        keep_ = keep_ref[...]
        lam_ = lam_ref[...]
        sp, a, mult = _decay(r, lam_)
        hs = h_ref[...]
        hprev = _roll(jnp.concatenate([hp_ref[...] * first, hs], axis=0), 1, 0)[CONV_HALO:]
        lg = lg_ref[...]
        da = lg * hprev * keep_
        ixb = ig * xb
        dmult = lg * ixb * keep_
        dixb = lg * jnp.where(keep_ > 0.0, mult, 1.0)
        dlog_a = da * a - dmult * jnp.where(mult > 0.0, a * a / mult, 0.0)
        dr = dlog_a * (-LRU_C * sp)
        dlam_ref[...] += _rowsum(dlog_a * (-LRU_C * r)) * (-jax.nn.sigmoid(-lam_))
        dpr = dr * r * (1.0 - r)
        dpi = dixb * xb * ig * (1.0 - ig)
        dbr_ref[...] += _rowsum(dpr)
        dbi_ref[...] += _rowsum(dpi)
        dxb = dixb * ig
        parts = []
        for h in range(LRU_HEADS):
            sl = slice(h * LRU_HEAD_DIM, (h + 1) * LRU_HEAD_DIM)
            dwr_ref[h] += _dot_tn(xb[:, sl], dpr[:, sl])
            dwi_ref[h] += _dot_tn(xb[:, sl], dpi[:, sl])
            parts.append(_dot_nt(dpr[:, sl], wr_ref[h]) + _dot_nt(dpi[:, sl], wi_ref[h]))
        dxb = dxb + jnp.concatenate(parts, axis=1)
        dxb_ref[...] = dxb
        dcb_ref[...] += _rowsum(dxb)

    ins = [xb, r, ig, lam_grad, hseq, hseq, keep, wr, wi, lam]
    in_specs = [_row(ts, D)] * 5 + [_prev(CONV_HALO, D, ts), _row(ts, 1)] + [_const(v.shape) for v in ins[7:]]
    gshape = (LRU_HEADS, LRU_HEAD_DIM, LRU_HEAD_DIM)
    return pl.pallas_call(
        body, name="odd_gates_bwd", grid=(S // ts,), in_specs=in_specs,
        out_specs=[_row(ts, D)] + [_acc((1, D))] * 4 + [_acc(gshape)] * 2,
        out_shape=[_sds((S, D), F32)] + [_sds((1, D), F32)] * 4 + [_sds(gshape, F32)] * 2,
        compiler_params=_cp())(*ins)


def odd_pre_bwd(x, dy, z, dxb, dgate, g, cw, win):
    S = x.shape[0]
    ts = _tile_rows(S, 512)
    n = S // ts

    def body(x_ref, dy_ref, xbp_ref, xbpp_ref, dxb_ref, dxbn_ref, dgate_ref, g_ref, cw_ref, win_ref,
             dx_ref, dcw_ref, dg_ref, dwin_ref, acc_ref):
        i = pl.program_id(0)

        @pl.when(i == 0)
        def _():
            dcw_ref[...] = jnp.zeros_like(dcw_ref)
            dg_ref[...] = jnp.zeros_like(dg_ref)

        dxb = dxb_ref[...]
        extd = jnp.concatenate([dxb, dxbn_ref[...] * (i < n - 1).astype(F32)], axis=0)
        extx = jnp.concatenate([xbpp_ref[...] * (i > 0).astype(F32), xbp_ref[...]], axis=0)
        dxbp = cw_ref[3:4, :] * dxb
        dcw_ref[3:4, :] += _rowsum(dxb * xbp_ref[...])
        for k in range(3):
            dxbp = dxbp + cw_ref[k:k + 1, :] * _roll(extd, -(3 - k), 0)[:ts]
            dcw_ref[k:k + 1, :] += _rowsum(dxb * _roll(extx, 3 - k, 0)[CONV_HALO:])
        dz = jnp.concatenate([dgate_ref[...], dxbp], axis=1).astype(BF16)
        x_ = x_ref[...]
        h, rstd = _rms(x_, g_ref[...])
        dxn, dgr = _rms_bwd(x_, g_ref[...], rstd, _dot_nt(dz, win_ref[...]))
        dx_ref[...] = dy_ref[...] + dxn
        dg_ref[...] += _rowsum(dgr)
        _accumulate_tn(acc_ref, dwin_ref, h, dz, n)

    ins = [x, dy, z, z, dxb, dxb, dgate, g, cw, win]
    in_specs = [_row(ts, D), _row(ts, D), _row(ts, D, 1), _prev(CONV_HALO, D, ts, 1), _row(ts, D),
                _next(CONV_HALO, D, ts, n), _row(ts, D)] + [_const(v.shape) for v in ins[7:]]
    return pl.pallas_call(
        body, name="odd_pre_bwd", grid=(n,), in_specs=in_specs,
        out_specs=[_row(ts, D), _acc((4, D)), _acc((1, D)), _acc((D, 2 * D))],
        out_shape=[_sds((S, D), F32), _sds((4, D), F32), _sds((1, D), F32), _sds((D, 2 * D), BF16)],
        scratch_shapes=[pltpu.VMEM((D, 2 * D), F32)], compiler_params=_cp())(*ins)


def even_post_bwd(dy, ypool, o, wo_pool, wo_att):
    S = dy.shape[0]
    ts = _tile_rows(S, 512)
    n = S // ts

    def body(dy_ref, yp_ref, o_ref, wp_ref, wa_ref, dyp_ref, do_ref, delta_ref, dwp_ref, dwa_ref, accp_ref,
             acca_ref):
        dy_, o_ = dy_ref[...], o_ref[...]
        dyp_ref[...] = _dot_nt(dy_, wp_ref[...])
        do = _dot_nt(dy_, wa_ref[...])
        do_ref[...] = do.astype(BF16)
        prod = do * o_
        for h in range(MLA_HEADS):
            delta_ref[h] = _as_row(jnp.sum(prod[:, h * LANES:(h + 1) * LANES], axis=1, keepdims=True))
        _accumulate_tn(accp_ref, dwp_ref, yp_ref[...], dy_, n)
        _accumulate_tn(acca_ref, dwa_ref, o_, dy_, n)

    return pl.pallas_call(
        body, name="even_post_bwd", grid=(n,),
        in_specs=[_row(ts, D), _row(ts, POOL_DIM), _row(ts, D), _const(wo_pool.shape), _const(wo_att.shape)],
        out_specs=[_row(ts, POOL_DIM), _row(ts, D),
                   pl.BlockSpec((MLA_HEADS, None, 1, ts), lambda i: (0, i, 0, 0)), _acc((POOL_DIM, D)),
                   _acc((D, D))],
        out_shape=[_sds((S, POOL_DIM), F32), _sds((S, D), BF16), _sds((MLA_HEADS, n, 1, ts), F32),
                   _sds((POOL_DIM, D), BF16), _sds((D, D), BF16)],
        scratch_shapes=[pltpu.VMEM((POOL_DIM, D), F32), pltpu.VMEM((D, D), F32)],
        compiler_params=_cp())(dy, ypool, o, wo_pool, wo_att)


def attn_bwd(qp, kp, vp, do, lse_row, delta_row, token=None):
    S = qp.shape[0]
    tk = _tile_rows(S, 512)
    nq = S // tk
    extra, extra_specs = _after(token)

    def body(q_ref, k_ref, v_ref, do_ref, lse_ref, delta_ref, *rest):
        dq_ref, dk_ref, dv_ref = rest[-3:]
        kj = pl.program_id(1)

        @pl.when(kj == 0)
        def _():
            dq_ref[...] = jnp.zeros_like(dq_ref)

        k, v = k_ref[...], v_ref[...]

        def block(qi, carry, masked):
            dk, dv = carry
            off = pl.multiple_of(qi * tk, tk)
            q = q_ref[pl.ds(off, tk), :]
            do_ = do_ref[pl.ds(off, tk), :]
            st = _dot_nt(k, q)
            if masked:
                row = lax.broadcasted_iota(jnp.int32, (tk, tk), 0)
                col = lax.broadcasted_iota(jnp.int32, (tk, tk), 1)
                st = jnp.where(col >= row, st, -1e30)
            pt = _exp2(st - lse_ref[qi])
            dv = dv + _dot(pt, do_)
            dst = (pt * (_dot_nt(v, do_) - delta_ref[qi])).astype(BF16)
            dk = dk + _dot(dst, q)
            dq_ref[pl.ds(off, tk), :] += _dot_tn(dst, k)
            return dk, dv

        zero = jnp.zeros((tk, LANES), F32)
        carry = block(kj, (zero, zero), True)
        dk, dv = _pair_loop(kj + 1, nq, lambda qi, c: block(qi, c, False), carry, unrolls=(4, 2, 1))
        dk_ref[...] = dk * LN_2
        dv_ref[...] = dv

    blk = pl.BlockSpec((tk, LANES), lambda h, j: (j, h))
    full = pl.BlockSpec((S, LANES), lambda h, j: (0, h))
    rowv = pl.BlockSpec((None, nq, 1, tk), lambda h, j: (h, 0, 0, 0))
    return pl.pallas_call(
        body, name="attn_bwd", grid=(MLA_HEADS, nq), in_specs=[full, blk, blk, full, rowv, rowv] + extra_specs,
        out_specs=[full, blk, blk], out_shape=[_sds((S, D), F32)] * 3, compiler_params=_cp2())(
            qp, kp, vp, do, lse_row, delta_row, *extra)


def even_pre_bwd(x, dy, z, dq, dk, dv, dyp, tabs, g, win, pw, pscale, qg, wq, kvg, wk, wv):
    S = x.shape[0]
    ts = _tile_rows(S, 512)
    n = S // ts

    def body(x_ref, dy_ref, z_ref, up_ref, dq_ref, dk_ref, dv_ref, dyp_ref, dypn_ref, c_ref, a_ref, b_ref,
             g_ref, win_ref, pw_ref, ps_ref, qg_ref, wq_ref, kvg_ref, wk_ref, wv_ref,
             dx_ref, dg_ref, dpw_ref, dps_ref, dqg_ref, dwq_ref, dkvg_ref, dwk_ref, dwv_ref, dwin_ref, acc_ref):
        i = pl.program_id(0)

        @pl.when(i == 0)
        def _():
            for ref in (dg_ref, dpw_ref, dps_ref, dqg_ref, dwq_ref, dkvg_ref, dwk_ref, dwv_ref):
                ref[...] = jnp.zeros_like(ref)

        z = z_ref[...]
        c, a, b = c_ref[...], a_ref[...], b_ref[...]
        ps = ps_ref[...]
        u = z[:, :POOL_DIM]
        pooled = _pooled(up_ref[...] * (i > 0).astype(F32), u, i * ts)
        dyp_ = dyp_ref[...]
        dps_ref[...] += _rowsum(dyp_ * _blockdot(pooled, pw_ref, 4, LANES))
        ext = jnp.concatenate([dyp_, dypn_ref[...] * (i < n - 1).astype(F32)], axis=0) * ps
        for gidx in range(4):
            sl = slice(gidx * LANES, (gidx + 1) * LANES)
            dpw_ref[gidx] += _dot_tn(pooled[:, sl], ext[:ts, sl])
        dpooled = jnp.concatenate(
            [_dot_nt(ext[:, gidx * LANES:(gidx + 1) * LANES], pw_ref[gidx]) for gidx in range(4)], axis=1)
        dm = dpooled / _pool_cnt(i * ts, ts + POOL_HALO)
        du = _pool_windows(dm, -1)[:ts] - dpooled[:ts]
        cq = z[:, 512:768]
        cqn, rstd_q = _rms(cq, qg_ref[...])
        dqf = _rope_bwd(dq_ref[...] * ATTN_SCALE, c, a, b)
        dwq_ref[...] += _dot_tn(cqn, dqf)
        dcq, dqg_rows = _rms_bwd(cq, qg_ref[...], rstd_q, _dot_nt(dqf, wq_ref[...]))
        dqg_ref[...] += _rowsum(dqg_rows)
        ckv = z[:, 768:896]
        ckvn, rstd_kv = _rms(ckv, kvg_ref[...])
        dk_, dv_ = dk_ref[...], dv_ref[...]
        dwk_ref[...] += _dot_tn(ckvn, dk_)
        dwv_ref[...] += _dot_tn(ckvn, dv_)
        dckv, dkvg_rows = _rms_bwd(ckv, kvg_ref[...], rstd_kv,
                                   _dot_nt(dk_, wk_ref[...]) + _dot_nt(dv_, wv_ref[...]))
        dkvg_ref[...] += _rowsum(dkvg_rows)
        dkr = dk_[:, :LANES]
        for h in range(1, MLA_HEADS):
            dkr = dkr + dk_[:, h * LANES:(h + 1) * LANES]
        lane = lax.broadcasted_iota(jnp.int32, (ts, LANES), 1)
        dkr = jnp.where((lane >= 64) & (lane < 96), _rope_bwd(dkr, c, a, b), 0.0)
        dz = jnp.concatenate([du, dcq, dckv, dkr], axis=1).astype(BF16)
        x_ = x_ref[...]
        h, rstd = _rms(x_, g_ref[...])
        dxn, dgr = _rms_bwd(x_, g_ref[...], rstd, _dot_nt(dz, win_ref[...]))
        dx_ref[...] = dy_ref[...] + dxn
        dg_ref[...] += _rowsum(dgr)
        _accumulate_tn(acc_ref, dwin_ref, h, dz, n)

    ins = [x, dy, z, z, dq, dk, dv, dyp, dyp, *tabs, g, win, pw, pscale, qg, wq, kvg, wk, wv]
    in_specs = [_row(ts, D), _row(ts, D), _row(ts, D), _prev(POOL_HALO, POOL_DIM, ts), _row(ts, D), _row(ts, D),
                _row(ts, D), _row(ts, POOL_DIM), _next(POOL_HALO, POOL_DIM, ts, n), _row(ts, LANES),
                _row(ts, LANES), _row(ts, LANES)] + [_const(v.shape) for v in ins[12:]]
    acc_shapes = [(1, D), (4, LANES, LANES), (1, POOL_DIM), (1, Q_LORA), (Q_LORA, D), (1, KV_LORA), (KV_LORA, D),
                  (KV_LORA, D)]
    return pl.pallas_call(
        body, name="even_pre_bwd", grid=(n,), in_specs=in_specs,
        out_specs=[_row(ts, D)] + [_acc(s) for s in acc_shapes] + [_acc((D, D))],
        out_shape=[_sds((S, D), F32)] + [_sds(s, F32) for s in acc_shapes] + [_sds((D, D), BF16)],
        scratch_shapes=[pltpu.VMEM((D, D), F32)], compiler_params=_cp())(*ins)


def _pick(n, options):
    for o in options:
        if n % o == 0:
            return o
    return n


def matmul_tn(name, a, b):
    out_dtype = BF16
    S = a.shape[-2]
    ts = _tile_rows(S, 2048)
    steps = S // ts

    def body(a_ref, b_ref, o_ref, acc_ref):
        s = pl.program_id(2)

        @pl.when(s == 0)
        def _():
            acc_ref[...] = jnp.zeros_like(acc_ref)

        acc_ref[...] += _dot_tn(a_ref[...], b_ref[...])

        @pl.when(s == steps - 1)
        def _():
            o_ref[...] = acc_ref[...].astype(o_ref.dtype)

    if a.ndim == 3:
        C, _, K = a.shape
        N = b.shape[1]
        tn = _pick(N, (1024, 512, 256, 128))
        grid = (C, N // tn, S // ts)
        in_specs = [pl.BlockSpec((None, ts, K), lambda c, j, s: (c, s, 0)),
                    pl.BlockSpec((ts, tn), lambda c, j, s: (s, j))]
        out_spec, out_shape, tile = pl.BlockSpec((None, K, tn), lambda c, j, s: (c, 0, j)), (C, K, N), (K, tn)
    elif b.ndim == 3:
        C, _, N = b.shape
        K = a.shape[1]
        tk = _pick(K, (1024, 512, 256, 128))
        grid = (C, K // tk, S // ts)
        in_specs = [pl.BlockSpec((ts, tk), lambda c, i, s: (s, i)),
                    pl.BlockSpec((None, ts, N), lambda c, i, s: (c, s, 0))]
        out_spec, out_shape, tile = pl.BlockSpec((None, tk, N), lambda c, i, s: (c, i, 0)), (C, K, N), (tk, N)
    else:
        K, N = a.shape[1], b.shape[1]
        tk = _pick(K, (1024, 512, 256, 128))
        tn = _pick(N, (1024, 512, 256, 128))
        grid = (K // tk, N // tn, S // ts)
        in_specs = [pl.BlockSpec((ts, tk), lambda i, j, s: (s, i)), pl.BlockSpec((ts, tn), lambda i, j, s: (s, j))]
        out_spec, out_shape, tile = pl.BlockSpec((tk, tn), lambda i, j, s: (i, j)), (K, N), (tk, tn)
    return pl.pallas_call(
        body, name=name, grid=grid, in_specs=in_specs, out_specs=out_spec, out_shape=_sds(out_shape, out_dtype),
        scratch_shapes=[pltpu.VMEM(tile, F32)], compiler_params=pltpu.CompilerParams(dimension_semantics=("arbitrary",) * 3, vmem_limit_bytes=VMEM_LIMIT))(
            a, b)


def _my_id():
    return lax.axis_index("x") * 4 + lax.axis_index("y") * 2 + lax.axis_index("c")


def _peer(j):
    x, y, c = lax.axis_index("x"), lax.axis_index("y"), lax.axis_index("c")
    px = 1 - x if j & 4 else x
    py = 1 - y if j & 2 else y
    pc = 1 - c if j & 1 else c
    return (px, py, pc), px * 4 + py * 2 + pc


def all_gather(name, arrays):
    n = len(arrays)

    def body(*refs):
        ins, outs = refs[:n], refs[n:2 * n]
        send_sems, recv_sems, local_sems = refs[2 * n:]
        me = _my_id()
        local = [pltpu.make_async_copy(ins[k], outs[k].at[me], local_sems.at[k]) for k in range(n)]
        for cp in local:
            cp.start()
        sends = []
        for j in range(1, N_DEV):
            peer, _ = _peer(j)
            for k in range(n):
                cp = pltpu.make_async_remote_copy(
                    src_ref=ins[k], dst_ref=outs[k].at[me], send_sem=send_sems.at[k, j - 1],
                    recv_sem=recv_sems.at[k, j - 1], device_id=peer, device_id_type=pl.DeviceIdType.MESH)
                cp.start()
                sends.append(cp)
        for j in range(1, N_DEV):
            peer, pid = _peer(j)
            for k in range(n):
                pltpu.make_async_remote_copy(
                    src_ref=ins[k], dst_ref=outs[k].at[pid], send_sem=send_sems.at[k, j - 1],
                    recv_sem=recv_sems.at[k, j - 1], device_id=peer, device_id_type=pl.DeviceIdType.MESH).wait_recv()
        for cp in sends:
            cp.wait_send()
        for cp in local:
            cp.wait()

    any_spec = pl.BlockSpec(memory_space=pl.ANY)
    return pl.pallas_call(
        body, name=name, in_specs=[any_spec] * n, out_specs=[any_spec] * n,
        out_shape=[_sds((N_DEV,) + a.shape, a.dtype) for a in arrays],
        scratch_shapes=[pltpu.SemaphoreType.DMA((n, N_DEV - 1)), pltpu.SemaphoreType.DMA((n, N_DEV - 1)),
                        pltpu.SemaphoreType.DMA((n,))],
        compiler_params=pltpu.CompilerParams(has_side_effects=True))(*arrays)


def exchange(name, arrays, gathers=()):
    n_ex, n = len(arrays), len(arrays) + len(gathers)

    def body(*refs):
        ins, outs = refs[:n], refs[n:2 * n]
        send_sems, recv_sems, local_sems = refs[2 * n:]
        me = _my_id()

        def mine(k, slot):
            return ins[k].at[slot] if k < n_ex else ins[k]

        local = [pltpu.make_async_copy(mine(k, me), outs[k].at[me], local_sems.at[k]) for k in range(n)]
        for cp in local:
            cp.start()
        sends = []
        for j in range(1, N_DEV):
            peer, pid = _peer(j)
            for k in range(n):
                cp = pltpu.make_async_remote_copy(
                    src_ref=mine(k, pid), dst_ref=outs[k].at[me], send_sem=send_sems.at[k, j - 1],
                    recv_sem=recv_sems.at[k, j - 1], device_id=peer, device_id_type=pl.DeviceIdType.MESH)
                cp.start()
                sends.append(cp)
        for j in range(1, N_DEV):
            peer, pid = _peer(j)
            for k in range(n):
                pltpu.make_async_remote_copy(
                    src_ref=mine(k, me), dst_ref=outs[k].at[pid], send_sem=send_sems.at[k, j - 1],
                    recv_sem=recv_sems.at[k, j - 1], device_id=peer, device_id_type=pl.DeviceIdType.MESH).wait_recv()
        for cp in sends:
            cp.wait_send()
        for cp in local:
            cp.wait()

    any_spec = pl.BlockSpec(memory_space=pl.ANY)
    return pl.pallas_call(
        body, name=name, in_specs=[any_spec] * n, out_specs=[any_spec] * n,
        out_shape=[_sds(a.shape, a.dtype) for a in arrays] + [_sds((N_DEV,) + a.shape, a.dtype) for a in gathers],
        scratch_shapes=[pltpu.SemaphoreType.DMA((n, N_DEV - 1)), pltpu.SemaphoreType.DMA((n, N_DEV - 1)),
                        pltpu.SemaphoreType.DMA((n,))],
        compiler_params=pltpu.CompilerParams(has_side_effects=True))(*arrays, *gathers)


_HBM = pl.BlockSpec(memory_space=pltpu.HBM)
_SEM = pl.BlockSpec(memory_space=pltpu.SEMAPHORE)
_DATAFLOW = pltpu.SideEffectType.DATAFLOW_SIDE_EFFECTING


def _in_hbm(v):
    return pltpu.with_memory_space_constraint(v, pltpu.HBM)


N_PEERS = N_DEV - 1


def _split_copy(k, j, srcs, lands, send_sems, recv_sems, gather, slot):
    peer, pid = _peer(j)
    return pltpu.make_async_remote_copy(
        src_ref=srcs[k] if gather else srcs[k].at[pid], dst_ref=lands[k].at[_my_id() if slot == "mine" else pid],
        send_sem=send_sems[j - 1], recv_sem=recv_sems[j - 1], device_id=peer, device_id_type=pl.DeviceIdType.MESH)


def split_start(name, arrays, gather):
    n = len(arrays)
    lands = [lax.empty((N_DEV,) + a.shape if gather else a.shape, a.dtype) for a in arrays]

    def body(*refs):
        srcs, lnds = refs[:n], refs[n:2 * n]
        sems = refs[4 * n:4 * n + 2 * N_PEERS]
        token = refs[-1]
        for j in range(1, N_DEV):
            for k in range(n):
                _split_copy(k, j, srcs, lnds, sems[:N_PEERS], sems[N_PEERS:], gather, "mine").start()
        token[...] = jnp.zeros_like(token)

    out = pl.pallas_call(
        body, name=name,
        out_shape=(*[pltpu.HBM(a.shape, a.dtype) for a in arrays], *[pltpu.HBM(l.shape, l.dtype) for l in lands],
                   *[pltpu.SemaphoreType.DMA(())] * (2 * N_PEERS), _sds((8, LANES), F32)),
        in_specs=[_HBM] * (2 * n),
        out_specs=(*[_HBM] * (2 * n), *[_SEM] * (2 * N_PEERS), pl.BlockSpec(memory_space=pltpu.VMEM)),
        input_output_aliases={k: k for k in range(2 * n)},
        compiler_params=pltpu.CompilerParams(has_side_effects=_DATAFLOW))(
            *[_in_hbm(a) for a in arrays], *[_in_hbm(l) for l in lands])
    sems = list(out[2 * n:2 * n + 2 * N_PEERS])
    return sems[:N_PEERS], sems[N_PEERS:], list(out[:n]), list(out[n:2 * n]), out[-1]


def split_wait(name, handle, after, gather):
    send_sems, recv_sems, srcs, lands, _ = handle
    n = len(srcs)

    def body(*refs):
        srcs_r, lnds_r = refs[:n], refs[n:2 * n]
        sems = refs[2 * n:2 * n + 2 * N_PEERS]
        for j in range(1, N_DEV):
            for k in range(n):
                cp = _split_copy(k, j, srcs_r, lnds_r, sems[:N_PEERS], sems[N_PEERS:], gather, "peer")
                cp.wait_send()
                cp.wait_recv()

    out = pl.pallas_call(
        body, name=name, out_shape=tuple(pltpu.HBM(a.shape, a.dtype) for a in srcs + lands),
        in_specs=[_HBM] * (2 * n) + [_SEM] * (2 * N_PEERS) + [pl.BlockSpec(memory_space=pl.ANY)],
        out_specs=tuple([_HBM] * (2 * n)), input_output_aliases={k: k for k in range(2 * n)},
        compiler_params=pltpu.CompilerParams(has_side_effects=_DATAFLOW))(
            *srcs, *lands, *send_sems, *recv_sems, after)
    return list(out[:n]), list(out[n:])


def _fill_own_slot(src, land, gather):
    me = _my_id()
    own = src[None] if gather else lax.dynamic_index_in_dim(src, me, 0, keepdims=True)
    return lax.dynamic_update_slice_in_dim(land, own, me, 0)


ADAMW_BLOCK_ELEMS = 128 * 1024


def adamw(name, parts, w, m, v):
    R, C = w.shape
    tr = _pick(R, [t for t in (512, 256, 128, 64, 32, 16, 8) if t * C <= ADAMW_BLOCK_ELEMS])
    c1 = 1.0 - ADAM_B1 ** ADAM_STEP
    c2 = 1.0 - ADAM_B2 ** ADAM_STEP

    def body(p_ref, w_ref, m_ref, v_ref, g_ref, d_ref, nm_ref, nv_ref):
        g = p_ref[0].astype(F32)
        for s in range(1, N_DEV):
            g = g + p_ref[s].astype(F32)
        g_ref[...] = g
        m_ = ADAM_B1 * m_ref[...] + (1.0 - ADAM_B1) * g
        v_ = ADAM_B2 * v_ref[...] + (1.0 - ADAM_B2) * (g * g)
        nm_ref[...] = m_
        nv_ref[...] = v_
        d_ref[...] = -ADAM_LR * ((m_ / c1) / (jnp.sqrt(v_ / c2) + ADAM_EPS) + ADAM_WD * w_ref[...])

    row = pl.BlockSpec((tr, C), lambda i: (i, 0))
    return pl.pallas_call(
        body, name=name, grid=(R // tr,),
        in_specs=[pl.BlockSpec((N_DEV, tr, C), lambda i: (0, i, 0)), row, row, row], out_specs=[row] * 4,
        out_shape=[_sds((R, C), F32)] * 4, compiler_params=_cp())(parts, w, m, v)


WEIGHTS = ['ev_norm', 'ev_w_in', 'ev_pool_w', 'ev_pool_scale', 'ev_q_norm', 'ev_w_q_up', 'ev_kv_norm', 'ev_w_kv_up',
           'ev_w_out', 'od_norm', 'od_w_in', 'od_conv_w', 'od_conv_b', 'od_w_rgate', 'od_b_rgate', 'od_w_igate',
           'od_b_igate', 'od_lambda', 'od_w_out', 'xa_norm_x', 'xa_norm_mem', 'xa_w_q', 'xa_w_kv', 'xa_w_o',
           'ffn_norm', 'ffn_w_gate_up', 'ffn_w_down', 'final_norm']
SHARD_AXIS = {'ev_w_in': 1, 'ev_w_q_up': 2, 'ev_w_kv_up': 2, 'ev_w_out': 1, 'od_norm': 1, 'od_w_in': 2,
              'od_conv_w': 2, 'od_conv_b': 1, 'od_w_rgate': 2, 'od_b_rgate': 1, 'od_w_igate': 2, 'od_b_igate': 1,
              'od_lambda': 1, 'od_w_out': 1, 'xa_w_q': 1, 'xa_w_kv': 2, 'xa_w_o': 1, 'ffn_w_gate_up': 2,
              'ffn_w_down': 1}
SMALL_F32 = ('od_norm', 'od_conv_w', 'od_conv_b', 'od_b_rgate', 'od_b_igate', 'od_lambda')
STACKED = ('ffn_w_gate_up', 'ffn_w_down')
SHARDED = [n for n in WEIGHTS if n in SHARD_AXIS]
REPLICATED = [n for n in WEIGHTS if n not in SHARD_AXIS]
ROW_ALIGN = 512


def _pack(flats, dtype):
    v = jnp.concatenate([f.reshape(-1).astype(dtype) for f in flats])
    pad = (-v.shape[0]) % (ROW_ALIGN * LANES)
    return jnp.pad(v, (0, pad)).reshape(-1, LANES)


def _rows8(n_elems):
    return -(-n_elems // (8 * LANES)) * 8


def _pack_rows(arrays, lead=False):
    out = []
    for a in arrays:
        r = a.reshape((N_DEV, -1, LANES) if lead else (-1, LANES))
        pad = _rows8(r.shape[-2] * LANES) - r.shape[-2]
        out.append(jnp.pad(r, [(0, 0)] * (r.ndim - 2) + [(0, pad), (0, 0)]))
    return jnp.concatenate(out, axis=-2)


def _unpack_rows(buf, shapes, lead=False):
    out, off = [], 0
    for s in shapes:
        n = 1
        for d in s:
            n *= d
        rows = buf[..., off:off + n // LANES, :]
        out.append(rows.reshape(((N_DEV,) if lead else ()) + tuple(s)))
        off += _rows8(n)
    return out


def _unpack(flat, shapes):
    out, off = [], 0
    v = flat.reshape(-1)
    for s in shapes:
        n = 1
        for d in s:
            n *= d
        out.append(v[off:off + n].reshape(s))
        off += n
    return out


def _to_full(stacked, axis):
    v = jnp.moveaxis(stacked, 0, axis)
    s = v.shape
    return v.reshape(s[:axis] + (s[axis] * s[axis + 1],) + s[axis + 2:])


def _to_shards(full, axis):
    s = full.shape
    v = full.reshape(s[:axis] + (N_DEV, s[axis] // N_DEV) + s[axis + 1:])
    return jnp.moveaxis(v, axis, 0)


def _pad_heads(w, nh, dh, lead):
    s = w.shape
    v = w.reshape(s[:-1] + (nh, dh))
    v = jnp.pad(v, [(0, 0)] * (len(s) - 1) + [(0, 0), (lead, LANES - dh - lead)])
    return v.reshape(s[:-1] + (nh * LANES,))


def _unpad_heads(w, nh, dh, lead):
    s = w.shape
    return w.reshape(s[:-1] + (nh, LANES))[..., lead:lead + dh].reshape(s[:-1] + (nh * dh,))


def _rope_tables(positions):
    inv_freq = 10000.0 ** (-jnp.arange(0, 32, 2, dtype=F32) / 32)
    ang = positions.astype(F32)[:, None] * inv_freq
    cos, sin = jnp.tile(jnp.cos(ang), (1, LANES // 16)), jnp.tile(jnp.sin(ang), (1, LANES // 16))
    lane = lax.broadcasted_iota(jnp.int32, cos.shape, 1)
    c = jnp.where((lane >= 64) & (lane < 96), cos, 1.0)
    a = jnp.where((lane >= 80) & (lane < 96), sin, 0.0)
    b = jnp.where((lane >= 64) & (lane < 80), -sin, 0.0)
    return c, a, b


def _t(w):
    return jnp.swapaxes(w, -1, -2)


def device_step(x, mem, positions, target, W, fwd_token=None, late_weights=None, ship_grads=None):
    S = x.shape[0]
    G = {}
    tabs = _rope_tables(positions)
    keep = (positions != 0).astype(F32)[:, None]
    row = lambda v: v.reshape(1, -1)

    w_in = W['ev_w_in'][0]
    ev_win = jnp.concatenate([w_in[:, :896], _pad_heads(w_in[:, 896:], 1, 32, 64)], axis=1)
    ev_wq = _pad_heads(W['ev_w_q_up'][0], MLA_HEADS, QK_DIM, 0)
    kvw = W['ev_w_kv_up'][0].reshape(KV_LORA, MLA_HEADS, 128)
    ev_wk = _pad_heads(kvw[:, :, :64].reshape(KV_LORA, 512), MLA_HEADS, 64, 0)
    ev_wv = _pad_heads(kvw[:, :, 64:].reshape(KV_LORA, 512), MLA_HEADS, 64, 0)
    ev_wo_pool = W['ev_w_out'][0][:POOL_DIM]
    ev_wo_att = _t(_pad_heads(_t(W['ev_w_out'][0][POOL_DIM:]), MLA_HEADS, 64, 0))
    pw = W['ev_pool_w'][0].astype(BF16)
    ev_g, ps, qg, kvg = row(W['ev_norm'][0]), row(W['ev_pool_scale'][0]), row(W['ev_q_norm'][0]), row(W['ev_kv_norm'][0])

    z0, qp, kp, vp, ypool = even_pre(x, tabs, ev_g, ev_win, pw, ps, qg, ev_wq, kvg, ev_wk, ev_wv)
    o_att, lse = attn_fwd(qp, kp, vp, fwd_token)
    if late_weights is not None:
        W = {**W, **late_weights(lse)}
    x1 = even_post(x, ypool, o_att, ev_wo_pool, ev_wo_att)

    def xa_ffn_fwd(xin, l, head=()):
        mn, km, vm = mem_kv(mem, row(W['xa_norm_mem'][l]), W['xa_w_kv'][l])
        xm = xattn_fwd(xin, row(W['xa_norm_x'][l]), W['xa_w_q'][l], km, vm, W['xa_w_o'][l])
        *xo, hf, gu = (ffn_fwd_loss if head else ffn_fwd)(
            xm, row(W['ffn_norm'][l]), W['ffn_w_gate_up'], l, W['ffn_w_down'][:, l].reshape(FF_HALF, FF_CHUNK, D),
            *head)
        return xm, (xo if head else xo[0]), (mn, km, vm, hf, gu)

    x2, x3, memkv0 = xa_ffn_fwd(x1, 0)

    od_g, lam = row(W['od_norm'][0]), row(W['od_lambda'][0])
    cw, cb = W['od_conv_w'][0], row(W['od_conv_b'][0])
    wr, wi = W['od_w_rgate'][0], W['od_w_igate'][0]
    br, bi = row(W['od_b_rgate'][0]), row(W['od_b_igate'][0])
    z1, a_t, b_t, xb1, r1, ig1 = odd_pre(x3, keep, od_g, W['od_w_in'][0], cw, cb, wr, br, wi, bi, lam)
    hseq = lru_scan(a_t, b_t)
    x4 = odd_post(x3, z1, hseq, W['od_w_out'][0])
    x5, (dx, g_final, loss), memkv1 = xa_ffn_fwd(x4, 1, (target, row(W['final_norm'])))
    G['final_norm'] = g_final.reshape(D)

    gnx, gnm, gwq, gwkv, gwo, gfn, gwgu, gwd = ([None, None] for _ in range(8))

    def xa_ffn_bwd(dy, xin, xm, memkv, l):
        mn, km, vm, hf, gu = memkv
        fg = row(W['ffn_norm'][l])
        dxm, dfg, act, dgu = ffn_bwd(xm, dy, gu, fg, W['ffn_w_gate_up'], l,
                                     W['ffn_w_down'][:, l].reshape(FF_HALF, FF_CHUNK, D))
        gwd[l] = matmul_tn("ffn_dwd", act, dy).reshape(N_DEV, D_FF // N_DEV, D)
        gwgu[l] = matmul_tn("ffn_dwgu", hf, dgu)
        gfn[l] = dfg[0]
        dxin, o, dq, hx, dgx, dk, dv = xattn_bwd(xin, dxm, row(W['xa_norm_x'][l]), W['xa_w_q'][l], km, vm,
                                                  W['xa_w_o'][l])
        gnx[l] = dgx[0]
        gwo[l] = matmul_tn("xa_dwo", o, dxm)
        gwq[l] = matmul_tn("xa_dwq", hx, dq)
        dkv, dgm = mem_bwd(mem, row(W['xa_norm_mem'][l]), dk, dv, W['xa_w_kv'][l])
        gnm[l] = dgm[0]
        gwkv[l] = matmul_tn("xa_dwkv", mn, dkv)
        return dxin

    dx4 = xa_ffn_bwd(dx, x4, x5, memkv1, 1)

    dgate, dhs, g_od_wout = odd_post_bwd(dx4, z1, hseq, W['od_w_out'][0])
    G['od_w_out'] = g_od_wout[None]
    lam_grad = lru_scan(a_t, dhs, reverse=True)
    dxb, dcb, dbr, dbi, dlam, dwr, dwi = odd_gates_bwd(xb1, r1, ig1, lam_grad, hseq, keep, wr, wi, lam)
    dx3, dcw, dg_od, g_od_win = odd_pre_bwd(x3, dx4, z1, dxb, dgate, od_g, cw, W['od_w_in'][0])
    G['od_w_in'] = g_od_win[None]
    G['od_norm'], G['od_conv_w'], G['od_conv_b'] = dg_od, dcw[None], dcb
    G['od_w_rgate'], G['od_b_rgate'], G['od_w_igate'], G['od_b_igate'], G['od_lambda'] = (
        dwr[None], dbr, dwi[None], dbi, dlam)

    dx1 = xa_ffn_bwd(dx3, x1, x2, memkv0, 0)
    G['xa_norm_x'], G['xa_norm_mem'], G['ffn_norm'] = jnp.stack(gnx), jnp.stack(gnm), jnp.stack(gfn)
    G['xa_w_q'], G['xa_w_kv'], G['xa_w_o'] = jnp.stack(gwq), jnp.stack(gwkv), jnp.stack(gwo)
    G['ffn_w_gate_up'], G['ffn_w_down'] = jnp.stack(gwgu, axis=1), jnp.stack(gwd, axis=1)
    bwd_token = ship_grads(G) if ship_grads is not None else None

    dyp, do_att, delta, g_wo_pool, g_wo_att = even_post_bwd(dx1, ypool, o_att, ev_wo_pool, ev_wo_att)
    G['ev_w_out'] = jnp.concatenate([g_wo_pool, _t(_unpad_heads(_t(g_wo_att), MLA_HEADS, 64, 0))], axis=0)[None]
    dq, dk, dv = attn_bwd(qp, kp, vp, do_att, lse, delta, bwd_token)
    (grad_x, dg_ev, dpw, dps, dqg, dwq, dkvg, dwk, dwv, g_win) = even_pre_bwd(
        x, dx1, z0, dq, dk, dv, dyp, tabs, ev_g, ev_win, pw, ps, qg, ev_wq, kvg, ev_wk, ev_wv)
    G['ev_w_in'] = jnp.concatenate([g_win[:, :896], _unpad_heads(g_win[:, 896:], 1, 32, 64)], axis=1)[None]
    G['ev_norm'], G['ev_pool_w'], G['ev_pool_scale'], G['ev_q_norm'], G['ev_kv_norm'] = (
        dg_ev, dpw[None], dps, dqg, dkvg)
    G['ev_w_q_up'] = _unpad_heads(dwq, MLA_HEADS, QK_DIM, 0)[None]
    gk = _unpad_heads(dwk, MLA_HEADS, 64, 0).reshape(KV_LORA, MLA_HEADS, 64)
    gv = _unpad_heads(dwv, MLA_HEADS, 64, 0).reshape(KV_LORA, MLA_HEADS, 64)
    G['ev_w_kv_up'] = jnp.concatenate([gk, gv], axis=2).reshape(1, KV_LORA, MLA_HEADS * 128)
    return loss[0, 0], grad_x, G


def kernel(x, mem, positions, ev_norm, ev_w_in, ev_pool_w, ev_pool_scale, ev_q_norm, ev_w_q_up, ev_kv_norm, ev_w_kv_up, ev_w_out, od_norm, od_w_in, od_conv_w, od_conv_b, od_w_rgate, od_b_rgate, od_w_igate, od_b_igate, od_lambda, od_w_out, xa_norm_x, xa_norm_mem, xa_w_q, xa_w_kv, xa_w_o, ffn_norm, ffn_w_gate_up, ffn_w_down, final_norm, loss_target, m_ev_norm, m_ev_w_in, m_ev_pool_w, m_ev_pool_scale, m_ev_q_norm, m_ev_w_q_up, m_ev_kv_norm, m_ev_w_kv_up, m_ev_w_out, m_od_norm, m_od_w_in, m_od_conv_w, m_od_conv_b, m_od_w_rgate, m_od_b_rgate, m_od_w_igate, m_od_b_igate, m_od_lambda, m_od_w_out, m_xa_norm_x, m_xa_norm_mem, m_xa_w_q, m_xa_w_kv, m_xa_w_o, m_ffn_norm, m_ffn_w_gate_up, m_ffn_w_down, m_final_norm, v_ev_norm, v_ev_w_in, v_ev_pool_w, v_ev_pool_scale, v_ev_q_norm, v_ev_w_q_up, v_ev_kv_norm, v_ev_w_kv_up, v_ev_w_out, v_od_norm, v_od_w_in, v_od_conv_w, v_od_conv_b, v_od_w_rgate, v_od_b_rgate, v_od_w_igate, v_od_b_igate, v_od_lambda, v_od_w_out, v_xa_norm_x, v_xa_norm_mem, v_xa_w_q, v_xa_w_kv, v_xa_w_o, v_ffn_norm, v_ffn_w_gate_up, v_ffn_w_down, v_final_norm):
    args = dict(locals())
    w = {n: args[n] for n in WEIGHTS}
    m = {n: args['m_' + n] for n in WEIGHTS}
    v = {n: args['v_' + n] for n in WEIGHTS}
    big = [n for n in SHARDED if n not in SMALL_F32]
    small = [n for n in SHARDED if n in SMALL_F32]

    small_shapes = [w[n].shape for n in small]
    first = [n for n in big if n.startswith('ev_')]
    late = [n for n in big if n not in first]

    def full(n, st):
        return st if n in STACKED else _to_full(st, SHARD_AXIS[n])

    W = {n: w[n] for n in REPLICATED}
    W.update((n, full(n, st)) for n, st in zip(first, all_gather("gather_ev_weights", [w[n].astype(BF16) for n in first])))
    gather = split_start("gather_start", [w[n].astype(BF16) for n in late] + [_pack_rows([w[n] for n in small])], True)

    def late_weights(after):
        srcs, lands = split_wait("gather_wait", gather, after, True)
        lands = [_fill_own_slot(s, l, True) for s, l in zip(srcs, lands)]
        out = {n: full(n, st) for n, st in zip(late, lands)}
        out.update((n, _to_full(st, SHARD_AXIS[n])) for n, st in zip(small, _unpack_rows(lands[-1], small_shapes, True)))
        return out

    def shards(G, n):
        return G[n] if n in STACKED else _to_shards(G[n], SHARD_AXIS[n])

    shipped = []

    def ship_grads(G):
        shipped.append(split_start("exchange_start", [shards(G, n).astype(BF16) for n in late] +
                                   [_pack_rows([shards(G, n) for n in small], lead=True)], False))
        return shipped[0][-1]

    loss, grad_x, G = device_step(x[0], mem[0], positions[0], loss_target[0], W, gather[-1], late_weights, ship_grads)
    outs = [{}, {}, {}, {}]

    rep_shapes = [w[n].shape for n in REPLICATED] + [(LANES,)]
    zero = jnp.zeros((LANES,), F32)
    *first_parts, rep_parts = exchange(
        "exchange_ev_and_rep_grads", [shards(G, n).astype(BF16) for n in first],
        [_pack([G[n] for n in REPLICATED] + [jnp.broadcast_to(loss, (LANES,))], F32)])
    rep = adamw("adamw_rep", rep_parts, *[_pack([d[n] for n in REPLICATED] + [zero], F32) for d in (w, m, v)])
    for k in range(4):
        outs[k].update(zip(REPLICATED + ['loss'], _unpack(rep[k], rep_shapes)))
    loss = outs[0]['loss'][0]

    srcs, lands = split_wait("exchange_wait", shipped[0], grad_x, False)
    late_parts = [_fill_own_slot(s, l, False) for s, l in zip(srcs, lands)]
    parts = first_parts + late_parts
    two_d = lambda a: a.reshape(-1, a.shape[-1])
    for n, p in zip(first + late, parts):
        res = adamw("adamw_" + n, p.reshape((N_DEV,) + two_d(w[n]).shape), two_d(w[n]), two_d(m[n]), two_d(v[n]))
        for k in range(4):
            outs[k][n] = res[k].reshape(w[n].shape)
    res = adamw("adamw_small", parts[-1], *[_pack_rows([d[n] for n in small]) for d in (w, m, v)])
    for k in range(4):
        outs[k].update(zip(small, _unpack_rows(res[k], small_shapes)))

    return (loss, grad_x[None], *[outs[0][n] for n in WEIGHTS], *[outs[1][n] for n in WEIGHTS],
            *[outs[2][n] for n in WEIGHTS], *[outs[3][n] for n in WEIGHTS])
```

```python
import functools

import jax
import jax.numpy as jnp
from jax import lax
from jax.experimental import pallas as pl
from jax.experimental.pallas import tpu as pltpu

F32, BF16 = jnp.float32, jnp.bfloat16
N_DEV = 8
D = 1024
POOL_DIM = 512
POOL_WINDOWS = (2, 4, 8, 16)
MLA_HEADS = 8
QK_DIM = 96
Q_LORA, KV_LORA = 256, 128
LRU_HEADS, LRU_HEAD_DIM = 4, 256
LRU_C = 8.0
MEM_HEADS, MEM_HEAD_DIM = 4, 256
D_FF = 2816
RMS_EPS = 1e-6
ADAM_LR, ADAM_B1, ADAM_B2, ADAM_EPS, ADAM_WD, ADAM_STEP = 0.001, 0.9, 0.999, 1e-08, 0.01, 10
LANES = 128
POOL_HALO = 16
CONV_HALO = 8
VMEM_LIMIT = 60000 * 1024


def _cp():
    return pltpu.CompilerParams(dimension_semantics=("arbitrary",), vmem_limit_bytes=VMEM_LIMIT)


def _cp2():
    return pltpu.CompilerParams(dimension_semantics=("arbitrary", "arbitrary"), vmem_limit_bytes=VMEM_LIMIT)


def _row(ts, c, col=0):
    return pl.BlockSpec((ts, c), lambda i: (i, col))


def _prev(hr, c, ts, col=0):
    r = ts // hr
    return pl.BlockSpec((hr, c), lambda i: (jnp.maximum(i * r - 1, 0), col))


def _next(hr, c, ts, n, col=0):
    r = ts // hr
    return pl.BlockSpec((hr, c), lambda i: (jnp.minimum((i + 1) * r, n * r - 1), col))


def _const(shape):
    nd = len(shape)
    return pl.BlockSpec(tuple(shape), lambda i: (0,) * nd, pipeline_mode=pl.Buffered(1))


def _acc(shape):
    nd = len(shape)
    return pl.BlockSpec(tuple(shape), lambda i: (0,) * nd)


def _sds(shape, dt):
    return jax.ShapeDtypeStruct(tuple(shape), dt)


def _dot(a, b):
    return jnp.dot(a.astype(BF16), b.astype(BF16), preferred_element_type=F32)


def _dot_nt(a, b):
    return lax.dot_general(a.astype(BF16), b.astype(BF16), (((1,), (1,)), ((), ())), preferred_element_type=F32)


def _dot_tn(a, b):
    return lax.dot_general(a.astype(BF16), b.astype(BF16), (((0,), (0,)), ((), ())), preferred_element_type=F32)


def _rms(x, g):
    rstd = lax.rsqrt(jnp.mean(x * x, axis=-1, keepdims=True) + RMS_EPS)
    return x * rstd * g, rstd


def _rms_bwd(x, g, rstd, dy):
    xn = x * rstd
    dyg = dy * g
    dx = rstd * (dyg - xn * jnp.mean(dyg * xn, axis=-1, keepdims=True))
    return dx, dy * xn


def _rowsum(v):
    return jnp.sum(v, axis=0, keepdims=True)


def _roll(v, s, axis):
    n = v.shape[axis]
    return pltpu.roll(v, s % n, axis)


def _rope(t, c, a, b):
    k = t.shape[1] // LANES
    if k > 1:
        c, a, b = (jnp.tile(v, (1, k)) for v in (c, a, b))
    return t * c + _roll(t, 16, 1) * a + _roll(t, -16, 1) * b


def _rope_bwd(d, c, a, b):
    k = d.shape[1] // LANES
    if k > 1:
        c, a, b = (jnp.tile(v, (1, k)) for v in (c, a, b))
    return d * c + _roll(d * a, -16, 1) + _roll(d * b, 16, 1)


def _gelu(x):
    c = 0.7978845608028654
    t = jnp.tanh(c * (x + 0.044715 * x * x * x))
    return 0.5 * x * (1.0 + t), t


def _gelu_grad(x, t):
    c = 0.7978845608028654
    return 0.5 * (1.0 + t) + 0.5 * x * (1.0 - t * t) * c * (1.0 + 3.0 * 0.044715 * x * x)


def _blockdot(v, w_ref, nblk, width):
    return jnp.concatenate(
        [_dot(v[:, j * width:(j + 1) * width], w_ref[j]) for j in range(nblk)], axis=1)


def _pool_cnt(row0, rows):
    t = row0 + lax.broadcasted_iota(jnp.int32, (rows, POOL_DIM), 0)
    w = jnp.left_shift(2, lax.broadcasted_iota(jnp.int32, (rows, POOL_DIM), 1) // LANES)
    return jnp.minimum(t + 1, w).astype(F32)


def _pool_windows(ext, sign):
    s2 = ext + _roll(ext, sign * 1, 0)
    t = s2[:, LANES:]
    s4 = t + _roll(t, sign * 2, 0)
    t = s4[:, LANES:]
    s8 = t + _roll(t, sign * 4, 0)
    t = s8[:, LANES:]
    s16 = t + _roll(t, sign * 8, 0)
    return jnp.concatenate([s2[:, :LANES], s4[:, :LANES], s8[:, :LANES], s16], axis=1)


def _pooled(uprev, u, row0):
    ts = u.shape[0]
    ext = jnp.concatenate([uprev, u], axis=0)
    sums = _pool_windows(ext, 1)[POOL_HALO:]
    return sums / _pool_cnt(row0, ts) - u


def _expm1(x):
    return jnp.where(jnp.abs(x) < 0.01, x * (1.0 + 0.5 * x * (1.0 + x * (1.0 / 3.0))), jnp.exp(x) - 1.0)


def _softplus(z):
    return jnp.maximum(z, 0.0) + jnp.log1p(jnp.exp(-jnp.abs(z)))


def _tile_rows(s, want):
    while s % want:
        want //= 2
    return want


def even_pre(x, tabs, g, win, pw, pscale, qg, wq, kvg, wk, wv):
    S = x.shape[0]
    ts = _tile_rows(S, 512)

    def body(x_ref, xp_ref, c_ref, a_ref, b_ref, g_ref, win_ref, pw_ref, ps_ref, qg_ref, wq_ref, kvg_ref,
             wk_ref, wv_ref, z_ref, q_ref, k_ref, v_ref, yp_ref):
        i = pl.program_id(0)
        h, _ = _rms(x_ref[...], g_ref[...])
        z = _dot(h, win_ref[...])
        z_ref[...] = z
        hp, _ = _rms(xp_ref[...], g_ref[...])
        uprev = _dot(hp, win_ref[:, :POOL_DIM]) * (i > 0).astype(F32)
        u = z[:, :POOL_DIM]
        pooled = _pooled(uprev, u, i * ts)
        yp_ref[...] = (_blockdot(pooled, pw_ref, 4, LANES) * ps_ref[...]).astype(BF16)
        c, a, b = c_ref[...], a_ref[...], b_ref[...]
        cqn, _ = _rms(z[:, 512:768], qg_ref[...])
        q_ref[...] = (_rope(_dot(cqn, wq_ref[...]), c, a, b) * (ATTN_SCALE * LOG2_E)).astype(BF16)
        ckvn, _ = _rms(z[:, 768:896], kvg_ref[...])
        krr = _rope(z[:, 896:1024], c, a, b)
        k_ref[...] = (_dot(ckvn, wk_ref[...]) + jnp.tile(krr, (1, MLA_HEADS))).astype(BF16)
        lane = lax.broadcasted_iota(jnp.int32, (ts, D), 1) % LANES
        v_ref[...] = jnp.where(lane == ONES_LANE, 1.0, _dot(ckvn, wv_ref[...])).astype(BF16)

    ins = [x, x, *tabs, g, win, pw, pscale, qg, wq, kvg, wk, wv]
    in_specs = [_row(ts, D), _prev(POOL_HALO, D, ts), _row(ts, LANES), _row(ts, LANES), _row(ts, LANES)]
    in_specs += [_const(v.shape) for v in ins[5:]]
    return pl.pallas_call(
        body, name="even_pre", grid=(S // ts,), in_specs=in_specs,
        out_specs=[_row(ts, D)] * 4 + [_row(ts, POOL_DIM)],
        out_shape=[_sds((S, D), F32)] + [_sds((S, D), BF16)] * 3 + [_sds((S, POOL_DIM), BF16)],
        compiler_params=_cp())(*ins)


ATTN_SCALE = QK_DIM ** -0.5
LOG2_E = 1.4426950408889634
LN_2 = 0.6931471805599453
ONES_LANE = 64


def _exp2(x):
    return jnp.exp2(x)


def _pair_loop(lo, hi, step, init, unrolls=(2, 1)):
    carry = init
    for unroll in unrolls:
        groups = (hi - lo) // unroll

        def group(j, c, lo=lo, unroll=unroll):
            for u in range(unroll):
                c = step(lo + unroll * j + u, c)
            return c

        carry = lax.fori_loop(0, groups, group, carry)
        lo = lo + unroll * groups
    return carry


def _as_row(col):
    return jnp.transpose(jnp.broadcast_to(col, (col.shape[0], LANES)))[0:1, :]


def _after(token):
    return ([], []) if token is None else ([token], [pl.BlockSpec(memory_space=pl.ANY)])


def attn_fwd(qp, kp, vp, token=None):
    S = qp.shape[0]
    tq = _tile_rows(S, 512)
    extra, extra_specs = _after(token)

    def body(q_ref, k_ref, v_ref, *rest):
        o_ref, lse_ref = rest[-2:]
        qi = pl.program_id(1)
        q = q_ref[...]

        def block(ki, carry, masked):
            m, acc = carry
            off = pl.multiple_of(ki * tq, tq)
            s = _dot_nt(q, k_ref[pl.ds(off, tq), :])
            if masked:
                row = lax.broadcasted_iota(jnp.int32, (tq, tq), 0)
                col = lax.broadcasted_iota(jnp.int32, (tq, tq), 1)
                s = jnp.where(col <= row, s, -1e30)
            m_new = jnp.maximum(m, jnp.max(s, axis=1, keepdims=True))
            acc = _exp2(m - m_new) * acc + _dot(_exp2(s - m_new), v_ref[pl.ds(off, tq), :])
            return m_new, acc

        init = (jnp.full((tq, 1), -1e30, F32), jnp.zeros((tq, LANES), F32))
        carry = _pair_loop(0, qi, lambda ki, c: block(ki, c, False), init, unrolls=(8, 4, 2, 1))
        m, acc = block(qi, carry, True)
        l = acc[:, ONES_LANE:ONES_LANE + 1]
        o_ref[...] = acc / l
        lse_ref[...] = _as_row(m + jnp.log(l) * LOG2_E)

    blk = pl.BlockSpec((tq, LANES), lambda h, i: (i, h))
    full = pl.BlockSpec((S, LANES), lambda h, i: (0, h))
    return pl.pallas_call(
        body, name="attn_fwd", grid=(MLA_HEADS, S // tq), in_specs=[blk, full, full] + extra_specs,
        out_specs=[blk, pl.BlockSpec((None, None, 1, tq), lambda h, i: (h, i, 0, 0))],
        out_shape=[_sds((S, D), F32), _sds((MLA_HEADS, S // tq, 1, tq), F32)], compiler_params=_cp2())(
            qp, kp, vp, *extra)


def even_post(x, ypool, o, wo_pool, wo_att):
    S = x.shape[0]
    ts = _tile_rows(S, 512)

    def body(x_ref, yp_ref, o_ref, wp_ref, wa_ref, out_ref):
        out_ref[...] = x_ref[...] + _dot(yp_ref[...], wp_ref[...]) + _dot(o_ref[...], wa_ref[...])

    return pl.pallas_call(
        body, name="even_post", grid=(S // ts,),
        in_specs=[_row(ts, D), _row(ts, POOL_DIM), _row(ts, D), _const(wo_pool.shape), _const(wo_att.shape)],
        out_specs=_row(ts, D), out_shape=_sds((S, D), F32), compiler_params=_cp())(x, ypool, o, wo_pool, wo_att)


def mem_kv(mem, g, wkv):
    M = mem.shape[0]

    def body(mem_ref, g_ref, w_ref, mn_ref, k_ref, v_ref):
        mn, _ = _rms(mem_ref[...], g_ref[...])
        mn_ref[...] = mn.astype(BF16)
        k_ref[...] = _dot(mn, w_ref[:, :D]).astype(BF16)
        v_ref[...] = _dot(mn, w_ref[:, D:]).astype(BF16)

    return pl.pallas_call(
        body, name="mem_kv", grid=(1,), in_specs=[_acc(mem.shape), _acc(g.shape), _acc(wkv.shape)],
        out_specs=[_acc((M, D))] * 3, out_shape=[_sds((M, D), BF16)] * 3, compiler_params=_cp())(mem, g, wkv)


def _xattn_heads(hx, wq_ref, k_ref, v_ref):
    q = _dot(hx, wq_ref[...])
    scale = MEM_HEAD_DIM ** -0.5
    ps, os_ = [], []
    for h in range(MEM_HEADS):
        sl = slice(h * MEM_HEAD_DIM, (h + 1) * MEM_HEAD_DIM)
        s = _dot_nt(q[:, sl], k_ref[:, sl]) * scale
        e = jnp.exp(s - jnp.max(s, axis=1, keepdims=True))
        p = e / jnp.sum(e, axis=1, keepdims=True)
        ps.append(p)
        os_.append(_dot(p, v_ref[:, sl]))
    return q, ps, jnp.concatenate(os_, axis=1)


def xattn_fwd(x, g, wq, kmem, vmem, wo):
    S = x.shape[0]
    ts = _tile_rows(S, 512)

    def body(x_ref, g_ref, wq_ref, k_ref, v_ref, wo_ref, out_ref):
        x_ = x_ref[...]
        hx, _ = _rms(x_, g_ref[...])
        _, _, o = _xattn_heads(hx, wq_ref, k_ref, v_ref)
        out_ref[...] = x_ + _dot(o, wo_ref[...])

    ins = [x, g, wq, kmem, vmem, wo]
    return pl.pallas_call(
        body, name="xattn_fwd", grid=(S // ts,), in_specs=[_row(ts, D)] + [_const(v.shape) for v in ins[1:]],
        out_specs=_row(ts, D), out_shape=_sds((S, D), F32), compiler_params=_cp())(*ins)


def xattn_bwd(x, dy, g, wq, kmem, vmem, wo):
    S = x.shape[0]
    M = kmem.shape[0]
    ts = _tile_rows(S, 512)
    scale = MEM_HEAD_DIM ** -0.5

    def body(x_ref, dy_ref, g_ref, wq_ref, k_ref, v_ref, wo_ref,
             dx_ref, o_ref, dq_ref, hx_ref, dg_ref, dk_ref, dv_ref):
        i = pl.program_id(0)

        @pl.when(i == 0)
        def _():
            dg_ref[...] = jnp.zeros_like(dg_ref)
            dk_ref[...] = jnp.zeros_like(dk_ref)
            dv_ref[...] = jnp.zeros_like(dv_ref)

        x_, dy_ = x_ref[...], dy_ref[...]
        hx, rstd = _rms(x_, g_ref[...])
        q, ps, o = _xattn_heads(hx, wq_ref, k_ref, v_ref)
        hx_ref[...] = hx.astype(BF16)
        o_ref[...] = o.astype(BF16)
        do = _dot_nt(dy_, wo_ref[...])
        dqs = []
        for h in range(MEM_HEADS):
            sl = slice(h * MEM_HEAD_DIM, (h + 1) * MEM_HEAD_DIM)
            p, do_h = ps[h], do[:, sl]
            dp = _dot_nt(do_h, v_ref[:, sl])
            ds = p * (dp - jnp.sum(p * dp, axis=1, keepdims=True)) * scale
            dqs.append(_dot(ds, k_ref[:, sl]))
            dk_ref[:, sl] += _dot_tn(ds, q[:, sl])
            dv_ref[:, sl] += _dot_tn(p, do_h)
        dq = jnp.concatenate(dqs, axis=1).astype(BF16)
        dq_ref[...] = dq
        dxn, dgr = _rms_bwd(x_, g_ref[...], rstd, _dot_nt(dq, wq_ref[...]))
        dx_ref[...] = dy_ + dxn
        dg_ref[...] += _rowsum(dgr)

    ins = [x, dy, g, wq, kmem, vmem, wo]
    return pl.pallas_call(
        body, name="xattn_bwd", grid=(S // ts,),
        in_specs=[_row(ts, D), _row(ts, D)] + [_const(v.shape) for v in ins[2:]],
        out_specs=[_row(ts, D)] * 4 + [_acc((1, D)), _acc((M, D)), _acc((M, D))],
        out_shape=[_sds((S, D), F32)] + [_sds((S, D), BF16)] * 3 + [_sds((1, D), F32), _sds((M, D), F32),
                                                                    _sds((M, D), F32)],
        compiler_params=_cp())(*ins)


def mem_bwd(mem, g, dk, dv, wkv):
    M = mem.shape[0]

    def body(mem_ref, g_ref, dk_ref, dv_ref, w_ref, dkv_ref, dg_ref):
        dkv = jnp.concatenate([dk_ref[...], dv_ref[...]], axis=1)
        dkv_ref[...] = dkv.astype(BF16)
        _, rstd = _rms(mem_ref[...], g_ref[...])
        dg_ref[...] = _rowsum(_dot_nt(dkv, w_ref[...]) * (mem_ref[...] * rstd))

    ins = [mem, g, dk, dv, wkv]
    return pl.pallas_call(
        body, name="mem_bwd", grid=(1,), in_specs=[_acc(v.shape) for v in ins],
        out_specs=[_acc((M, 2 * D)), _acc((1, D))], out_shape=[_sds((M, 2 * D), BF16), _sds((1, D), F32)],
        compiler_params=_cp())(*ins)


FF_CHUNK = 2 * D_FF // N_DEV
FF_HALF = N_DEV // 2


def _layer_of(w, layer):
    return pl.BlockSpec((N_DEV, None) + w.shape[2:], lambda i: (0, layer, 0, 0), pipeline_mode=pl.Buffered(1))


def _ff_chunks(c, ts):
    return pl.BlockSpec((c, ts, FF_CHUNK), lambda i: (0, i, 0))


def _ffn(x_, g_ref, wgu_ref, wd_ref, hf_ref, gu_ref):
    hf = _rms(x_, g_ref[...])[0].astype(BF16)
    hf_ref[...] = hf
    out = x_
    for j in range(FF_HALF):
        gg, uu = _dot(hf, wgu_ref[j]), _dot(hf, wgu_ref[j + FF_HALF])
        gu_ref[j] = gg.astype(BF16)
        gu_ref[j + FF_HALF] = uu.astype(BF16)
        out = out + _dot(gg * jax.nn.sigmoid(gg) * uu, wd_ref[j])
    return out


def ffn_fwd(x, g, wgu, layer, wd):
    S = x.shape[0]
    ts = _tile_rows(S, 256)

    def body(x_ref, g_ref, wgu_ref, wd_ref, out_ref, hf_ref, gu_ref):
        out_ref[...] = _ffn(x_ref[...], g_ref, wgu_ref, wd_ref, hf_ref, gu_ref)

    return pl.pallas_call(
        body, name="ffn_fwd", grid=(S // ts,),
        in_specs=[_row(ts, D), _const(g.shape), _layer_of(wgu, layer), _const(wd.shape)],
        out_specs=[_row(ts, D), _row(ts, D), _ff_chunks(N_DEV, ts)],
        out_shape=[_sds((S, D), F32), _sds((S, D), BF16), _sds((N_DEV, S, FF_CHUNK), BF16)],
        compiler_params=_cp())(x, g, wgu, wd)


def ffn_fwd_loss(x, g, wgu, layer, wd, target, gf):
    S = x.shape[0]
    ts = _tile_rows(S, 256)

    def body(x_ref, g_ref, wgu_ref, wd_ref, t_ref, gf_ref, dx_ref, dgf_ref, loss_ref, hf_ref, gu_ref):
        @pl.when(pl.program_id(0) == 0)
        def _():
            dgf_ref[...] = jnp.zeros_like(dgf_ref)
            loss_ref[...] = jnp.zeros_like(loss_ref)

        out = _ffn(x_ref[...], g_ref, wgu_ref, wd_ref, hf_ref, gu_ref)
        y, rstd = _rms(out, gf_ref[...])
        err = y - t_ref[...]
        loss_ref[...] += 0.5 * _rowsum(jnp.mean(err * err, axis=1, keepdims=True))
        dxn, dgr = _rms_bwd(out, gf_ref[...], rstd, err * (1.0 / D))
        dx_ref[...] = dxn
        dgf_ref[...] += _rowsum(dgr)

    return pl.pallas_call(
        body, name="ffn_fwd_loss", grid=(S // ts,),
        in_specs=[_row(ts, D), _const(g.shape), _layer_of(wgu, layer), _const(wd.shape), _row(ts, D),
                  _const(gf.shape)],
        out_specs=[_row(ts, D), _acc((1, D)), _acc((1, 1)), _row(ts, D), _ff_chunks(N_DEV, ts)],
        out_shape=[_sds((S, D), F32), _sds((1, D), F32), _sds((1, 1), F32), _sds((S, D), BF16),
                   _sds((N_DEV, S, FF_CHUNK), BF16)],
        compiler_params=_cp())(x, g, wgu, wd, target, gf)


def ffn_bwd(x, dy, gu, g, wgu, layer, wd):
    S = x.shape[0]
    ts = _tile_rows(S, 256)

    def body(x_ref, dy_ref, gu_ref, g_ref, wgu_ref, wd_ref, dx_ref, dg_ref, act_ref, dgu_ref):
        @pl.when(pl.program_id(0) == 0)
        def _():
            dg_ref[...] = jnp.zeros_like(dg_ref)

        dy_ = dy_ref[...]
        dyb = dy_.astype(BF16)
        dh = jnp.zeros((ts, D), F32)
        dacts = [_dot_nt(dyb, wd_ref[j]) for j in range(FF_HALF)]
        for j in range(FF_HALF):
            gg, uu = gu_ref[j].astype(F32), gu_ref[j + FF_HALF].astype(F32)
            sg = jax.nn.sigmoid(gg)
            silu = gg * sg
            act_ref[j] = (silu * uu).astype(BF16)
            dact = dacts[j]
            dgate = (dact * uu * (sg * (1.0 + gg * (1.0 - sg)))).astype(BF16)
            dup = (dact * silu).astype(BF16)
            dgu_ref[j] = dgate
            dgu_ref[j + FF_HALF] = dup
            dh = dh + _dot_nt(dgate, wgu_ref[j]) + _dot_nt(dup, wgu_ref[j + FF_HALF])
        x_ = x_ref[...]
        _, rstd = _rms(x_, g_ref[...])
        dxn, dgr = _rms_bwd(x_, g_ref[...], rstd, dh)
        dx_ref[...] = dy_ + dxn
        dg_ref[...] += _rowsum(dgr)

    return pl.pallas_call(
        body, name="ffn_bwd", grid=(S // ts,),
        in_specs=[_row(ts, D), _row(ts, D), _ff_chunks(N_DEV, ts), _const(g.shape), _layer_of(wgu, layer),
                  _const(wd.shape)],
        out_specs=[_row(ts, D), _acc((1, D)), _ff_chunks(FF_HALF, ts), _ff_chunks(N_DEV, ts)],
        out_shape=[_sds((S, D), F32), _sds((1, D), F32), _sds((FF_HALF, S, FF_CHUNK), BF16),
                   _sds((N_DEV, S, FF_CHUNK), BF16)],
        compiler_params=_cp())(x, dy, gu, g, wgu, wd)


def _conv_fwd(xprev, xbp, cw_ref, cb):
    ext = jnp.concatenate([xprev, xbp], axis=0)
    acc = cb + cw_ref[3:4, :] * xbp
    for k in range(3):
        acc = acc + cw_ref[k:k + 1, :] * _roll(ext, 3 - k, 0)[CONV_HALO:]
    return acc


def _decay(r, lam):
    sp = _softplus(-lam)
    log_a = -LRU_C * r * sp
    return sp, jnp.exp(log_a), jnp.sqrt(jnp.maximum(-_expm1(2.0 * log_a), 0.0))


def odd_pre(x, keep, g, win, cw, cb, wr, br, wi, bi, lam):
    S = x.shape[0]
    ts = _tile_rows(S, 512)

    def body(x_ref, xp_ref, keep_ref, g_ref, win_ref, cw_ref, cb_ref, wr_ref, br_ref, wi_ref, bi_ref, lam_ref,
             z_ref, a_ref, b_ref, xb_ref, r_ref, ig_ref):
        i = pl.program_id(0)
        h, _ = _rms(x_ref[...], g_ref[...])
        z = _dot(h, win_ref[...])
        z_ref[...] = z
        hp, _ = _rms(xp_ref[...], g_ref[...])
        xprev = _dot(hp, win_ref[:, D:]) * (i > 0).astype(F32)
        xb = _conv_fwd(xprev, z[:, D:], cw_ref, cb_ref[...])
        xb_ref[...] = xb
        r = jax.nn.sigmoid(_blockdot(xb, wr_ref, LRU_HEADS, LRU_HEAD_DIM) + br_ref[...])
        ig = jax.nn.sigmoid(_blockdot(xb, wi_ref, LRU_HEADS, LRU_HEAD_DIM) + bi_ref[...])
        r_ref[...] = r
        ig_ref[...] = ig
        keep_ = keep_ref[...]
        _, a, mult = _decay(r, lam_ref[...])
        a_ref[...] = a * keep_
        b_ref[...] = jnp.where(keep_ > 0.0, mult, 1.0) * (ig * xb)

    ins = [x, x, keep, g, win, cw, cb, wr, br, wi, bi, lam]
    return pl.pallas_call(
        body, name="odd_pre", grid=(S // ts,),
        in_specs=[_row(ts, D), _prev(CONV_HALO, D, ts), _row(ts, 1)] + [_const(v.shape) for v in ins[3:]],
        out_specs=[_row(ts, 2 * D)] + [_row(ts, D)] * 5,
        out_shape=[_sds((S, 2 * D), F32)] + [_sds((S, D), F32)] * 5, compiler_params=_cp())(*ins)


def lru_scan(a, b, reverse=False):
    S = a.shape[0]
    ts = _tile_rows(S, 512)
    n = S // ts
    groups = ts // 8

    def body(a_ref, an_ref, b_ref, h_ref, carry_ref, ash_ref):
        i = pl.program_id(0)

        @pl.when(i == 0)
        def _():
            carry_ref[...] = jnp.zeros_like(carry_ref)

        rid = lax.broadcasted_iota(jnp.int32, (8, D), 0)
        if reverse:
            ext = jnp.concatenate([a_ref[...], an_ref[...] * (i > 0).astype(F32)], axis=0)
            ash_ref[...] = _roll(ext, -1, 0)[:ts]
        src = ash_ref if reverse else a_ref

        def group(j, carry):
            off = pl.multiple_of((groups - 1 - j if reverse else j) * 8, 8)
            a8, b8 = src[pl.ds(off, 8), :], b_ref[pl.ds(off, 8), :]
            for k in (1, 2, 4):
                inside = (rid < 8 - k) if reverse else (rid >= k)
                sh = -k if reverse else k
                a_sh = jnp.where(inside, _roll(a8, sh, 0), 1.0)
                b_sh = jnp.where(inside, _roll(b8, sh, 0), 0.0)
                b8 = a8 * b_sh + b8
                a8 = a8 * a_sh
            h8 = a8 * carry + b8
            h_ref[pl.ds(off, 8), :] = h8
            return h8[0:1, :] if reverse else h8[7:8, :]

        carry_ref[...] = lax.fori_loop(0, groups, group, carry_ref[...], unroll=4)

    if reverse:
        r = ts // 8
        tile = pl.BlockSpec((ts, D), lambda i: (n - 1 - i, 0))
        halo = pl.BlockSpec((8, D), lambda i: (jnp.minimum((n - i) * r, n * r - 1), 0))
    else:
        tile, halo = _row(ts, D), _prev(8, D, ts)
    return pl.pallas_call(
        body, name="lru_scan_rev" if reverse else "lru_scan", grid=(n,), in_specs=[tile, halo, tile],
        out_specs=tile, out_shape=_sds((S, D), F32),
        scratch_shapes=[pltpu.VMEM((1, D), F32), pltpu.VMEM((ts, D), F32)], compiler_params=_cp())(a, a, b)


def odd_post(x, z, hseq, wout):
    S = x.shape[0]
    ts = _tile_rows(S, 512)

    def body(x_ref, gate_ref, h_ref, w_ref, out_ref):
        gl, _ = _gelu(gate_ref[...])
        out_ref[...] = x_ref[...] + _dot(gl * h_ref[...], w_ref[...])

    return pl.pallas_call(
        body, name="odd_post", grid=(S // ts,),
        in_specs=[_row(ts, D), _row(ts, D), _row(ts, D), _const(wout.shape)],
        out_specs=_row(ts, D), out_shape=_sds((S, D), F32), compiler_params=_cp())(x, z, hseq, wout)


def _accumulate_tn(acc_ref, out_ref, a, b, steps):
    i = pl.program_id(0)

    @pl.when(i == 0)
    def _():
        acc_ref[...] = jnp.zeros_like(acc_ref)

    acc_ref[...] += _dot_tn(a, b)

    @pl.when(i == steps - 1)
    def _():
        out_ref[...] = acc_ref[...].astype(out_ref.dtype)


def odd_post_bwd(dy, z, hseq, wout):
    S = dy.shape[0]
    ts = _tile_rows(S, 512)
    n = S // ts

    def body(dy_ref, gate_ref, h_ref, w_ref, dgate_ref, dh_ref, dw_ref, acc_ref):
        gate, hs, dy_ = gate_ref[...], h_ref[...], dy_ref[...]
        gl, t = _gelu(gate)
        dyy = _dot_nt(dy_, w_ref[...])
        dgate_ref[...] = dyy * hs * _gelu_grad(gate, t)
        dh_ref[...] = dyy * gl
        _accumulate_tn(acc_ref, dw_ref, gl * hs, dy_, n)

    return pl.pallas_call(
        body, name="odd_post_bwd", grid=(n,),
        in_specs=[_row(ts, D), _row(ts, D), _row(ts, D), _const(wout.shape)],
        out_specs=[_row(ts, D), _row(ts, D), _acc((D, D))],
        out_shape=[_sds((S, D), F32), _sds((S, D), F32), _sds((D, D), BF16)],
        scratch_shapes=[pltpu.VMEM((D, D), F32)], compiler_params=_cp())(dy, z, hseq, wout)


def odd_gates_bwd(xb, r, ig, lam_grad, hseq, keep, wr, wi, lam):
    S = xb.shape[0]
    ts = _tile_rows(S, 512)

    def body(xb_ref, r_ref, ig_ref, lg_ref, h_ref, hp_ref, keep_ref, wr_ref, wi_ref, lam_ref,
             dxb_ref, dcb_ref, dbr_ref, dbi_ref, dlam_ref, dwr_ref, dwi_ref):
        i = pl.program_id(0)

        @pl.when(i == 0)
        def _():
            for ref in (dcb_ref, dbr_ref, dbi_ref, dlam_ref, dwr_ref, dwi_ref):
                ref[...] = jnp.zeros_like(ref)

        first = (i > 0).astype(F32)
        xb, r, ig = xb_ref[...], r_ref[...], ig_ref[...]
        keep_ = keep_ref[...]
        lam_ = lam_ref[...]
        sp, a, mult = _decay(r, lam_)
        hs = h_ref[...]
        hprev = _roll(jnp.concatenate([hp_ref[...] * first, hs], axis=0), 1, 0)[CONV_HALO:]
        lg = lg_ref[...]
        da = lg * hprev * keep_
        ixb = ig * xb
        dmult = lg * ixb * keep_
        dixb = lg * jnp.where(keep_ > 0.0, mult, 1.0)
        dlog_a = da * a - dmult * jnp.where(mult > 0.0, a * a / mult, 0.0)
        dr = dlog_a * (-LRU_C * sp)
        dlam_ref[...] += _rowsum(dlog_a * (-LRU_C * r)) * (-jax.nn.sigmoid(-lam_))
        dpr = dr * r * (1.0 - r)
        dpi = dixb * xb * ig * (1.0 - ig)
        dbr_ref[...] += _rowsum(dpr)
        dbi_ref[...] += _rowsum(dpi)
        dxb = dixb * ig
        parts = []
        for h in range(LRU_HEADS):
            sl = slice(h * LRU_HEAD_DIM, (h + 1) * LRU_HEAD_DIM)
            dwr_ref[h] += _dot_tn(xb[:, sl], dpr[:, sl])
            dwi_ref[h] += _dot_tn(xb[:, sl], dpi[:, sl])
            parts.append(_dot_nt(dpr[:, sl], wr_ref[h]) + _dot_nt(dpi[:, sl], wi_ref[h]))
        dxb = dxb + jnp.concatenate(parts, axis=1)
        dxb_ref[...] = dxb
        dcb_ref[...] += _rowsum(dxb)

    ins = [xb, r, ig, lam_grad, hseq, hseq, keep, wr, wi, lam]
    in_specs = [_row(ts, D)] * 5 + [_prev(CONV_HALO, D, ts), _row(ts, 1)] + [_const(v.shape) for v in ins[7:]]
    gshape = (LRU_HEADS, LRU_HEAD_DIM, LRU_HEAD_DIM)
    return pl.pallas_call(
        body, name="odd_gates_bwd", grid=(S // ts,), in_specs=in_specs,
        out_specs=[_row(ts, D)] + [_acc((1, D))] * 4 + [_acc(gshape)] * 2,
        out_shape=[_sds((S, D), F32)] + [_sds((1, D), F32)] * 4 + [_sds(gshape, F32)] * 2,
        compiler_params=_cp())(*ins)


def odd_pre_bwd(x, dy, z, dxb, dgate, g, cw, win):
    S = x.shape[0]
    ts = _tile_rows(S, 512)
    n = S // ts

    def body(x_ref, dy_ref, xbp_ref, xbpp_ref, dxb_ref, dxbn_ref, dgate_ref, g_ref, cw_ref, win_ref,
             dx_ref, dcw_ref, dg_ref, dwin_ref, acc_ref):
        i = pl.program_id(0)

        @pl.when(i == 0)
        def _():
            dcw_ref[...] = jnp.zeros_like(dcw_ref)
            dg_ref[...] = jnp.zeros_like(dg_ref)

        dxb = dxb_ref[...]
        extd = jnp.concatenate([dxb, dxbn_ref[...] * (i < n - 1).astype(F32)], axis=0)
        extx = jnp.concatenate([xbpp_ref[...] * (i > 0).astype(F32), xbp_ref[...]], axis=0)
        dxbp = cw_ref[3:4, :] * dxb
        dcw_ref[3:4, :] += _rowsum(dxb * xbp_ref[...])
        for k in range(3):
            dxbp = dxbp + cw_ref[k:k + 1, :] * _roll(extd, -(3 - k), 0)[:ts]
            dcw_ref[k:k + 1, :] += _rowsum(dxb * _roll(extx, 3 - k, 0)[CONV_HALO:])
        dz = jnp.concatenate([dgate_ref[...], dxbp], axis=1).astype(BF16)
        x_ = x_ref[...]
        h, rstd = _rms(x_, g_ref[...])
        dxn, dgr = _rms_bwd(x_, g_ref[...], rstd, _dot_nt(dz, win_ref[...]))
        dx_ref[...] = dy_ref[...] + dxn
        dg_ref[...] += _rowsum(dgr)
        _accumulate_tn(acc_ref, dwin_ref, h, dz, n)

    ins = [x, dy, z, z, dxb, dxb, dgate, g, cw, win]
    in_specs = [_row(ts, D), _row(ts, D), _row(ts, D, 1), _prev(CONV_HALO, D, ts, 1), _row(ts, D),
                _next(CONV_HALO, D, ts, n), _row(ts, D)] + [_const(v.shape) for v in ins[7:]]
    return pl.pallas_call(
        body, name="odd_pre_bwd", grid=(n,), in_specs=in_specs,
        out_specs=[_row(ts, D), _acc((4, D)), _acc((1, D)), _acc((D, 2 * D))],
        out_shape=[_sds((S, D), F32), _sds((4, D), F32), _sds((1, D), F32), _sds((D, 2 * D), BF16)],
        scratch_shapes=[pltpu.VMEM((D, 2 * D), F32)], compiler_params=_cp())(*ins)


def even_post_bwd(dy, ypool, o, wo_pool, wo_att):
    S = dy.shape[0]
    ts = _tile_rows(S, 512)
    n = S // ts

    def body(dy_ref, yp_ref, o_ref, wp_ref, wa_ref, dyp_ref, do_ref, delta_ref, dwp_ref, dwa_ref, accp_ref,
             acca_ref):
        dy_, o_ = dy_ref[...], o_ref[...]
        dyp_ref[...] = _dot_nt(dy_, wp_ref[...])
        do = _dot_nt(dy_, wa_ref[...])
        do_ref[...] = do.astype(BF16)
        prod = do * o_
        for h in range(MLA_HEADS):
            delta_ref[h] = _as_row(jnp.sum(prod[:, h * LANES:(h + 1) * LANES], axis=1, keepdims=True))
        _accumulate_tn(accp_ref, dwp_ref, yp_ref[...], dy_, n)
        _accumulate_tn(acca_ref, dwa_ref, o_, dy_, n)

    return pl.pallas_call(
        body, name="even_post_bwd", grid=(n,),
        in_specs=[_row(ts, D), _row(ts, POOL_DIM), _row(ts, D), _const(wo_pool.shape), _const(wo_att.shape)],
        out_specs=[_row(ts, POOL_DIM), _row(ts, D),
                   pl.BlockSpec((MLA_HEADS, None, 1, ts), lambda i: (0, i, 0, 0)), _acc((POOL_DIM, D)),
                   _acc((D, D))],
        out_shape=[_sds((S, POOL_DIM), F32), _sds((S, D), BF16), _sds((MLA_HEADS, n, 1, ts), F32),
                   _sds((POOL_DIM, D), BF16), _sds((D, D), BF16)],
        scratch_shapes=[pltpu.VMEM((POOL_DIM, D), F32), pltpu.VMEM((D, D), F32)],
        compiler_params=_cp())(dy, ypool, o, wo_pool, wo_att)


def attn_bwd(qp, kp, vp, do, lse_row, delta_row, token=None):
    S = qp.shape[0]
    tk = _tile_rows(S, 512)
    nq = S // tk
    extra, extra_specs = _after(token)

    def body(q_ref, k_ref, v_ref, do_ref, lse_ref, delta_ref, *rest):
        dq_ref, dk_ref, dv_ref = rest[-3:]
        kj = pl.program_id(1)

        @pl.when(kj == 0)
        def _():
            dq_ref[...] = jnp.zeros_like(dq_ref)

        k, v = k_ref[...], v_ref[...]

        def block(qi, carry, masked):
            dk, dv = carry
            off = pl.multiple_of(qi * tk, tk)
            q = q_ref[pl.ds(off, tk), :]
            do_ = do_ref[pl.ds(off, tk), :]
            st = _dot_nt(k, q)
            if masked:
                row = lax.broadcasted_iota(jnp.int32, (tk, tk), 0)
                col = lax.broadcasted_iota(jnp.int32, (tk, tk), 1)
                st = jnp.where(col >= row, st, -1e30)
            pt = _exp2(st - lse_ref[qi])
            dv = dv + _dot(pt, do_)
            dst = (pt * (_dot_nt(v, do_) - delta_ref[qi])).astype(BF16)
            dk = dk + _dot(dst, q)
            dq_ref[pl.ds(off, tk), :] += _dot_tn(dst, k)
            return dk, dv

        zero = jnp.zeros((tk, LANES), F32)
        carry = block(kj, (zero, zero), True)
        dk, dv = _pair_loop(kj + 1, nq, lambda qi, c: block(qi, c, False), carry, unrolls=(4, 2, 1))
        dk_ref[...] = dk * LN_2
        dv_ref[...] = dv

    blk = pl.BlockSpec((tk, LANES), lambda h, j: (j, h))
    full = pl.BlockSpec((S, LANES), lambda h, j: (0, h))
    rowv = pl.BlockSpec((None, nq, 1, tk), lambda h, j: (h, 0, 0, 0))
    return pl.pallas_call(
        body, name="attn_bwd", grid=(MLA_HEADS, nq), in_specs=[full, blk, blk, full, rowv, rowv] + extra_specs,
        out_specs=[full, blk, blk], out_shape=[_sds((S, D), F32)] * 3, compiler_params=_cp2())(
            qp, kp, vp, do, lse_row, delta_row, *extra)


def even_pre_bwd(x, dy, z, dq, dk, dv, dyp, tabs, g, win, pw, pscale, qg, wq, kvg, wk, wv):
    S = x.shape[0]
    ts = _tile_rows(S, 512)
    n = S // ts

    def body(x_ref, dy_ref, z_ref, up_ref, dq_ref, dk_ref, dv_ref, dyp_ref, dypn_ref, c_ref, a_ref, b_ref,
             g_ref, win_ref, pw_ref, ps_ref, qg_ref, wq_ref, kvg_ref, wk_ref, wv_ref,
             dx_ref, dg_ref, dpw_ref, dps_ref, dqg_ref, dwq_ref, dkvg_ref, dwk_ref, dwv_ref, dwin_ref, acc_ref):
        i = pl.program_id(0)

        @pl.when(i == 0)
        def _():
            for ref in (dg_ref, dpw_ref, dps_ref, dqg_ref, dwq_ref, dkvg_ref, dwk_ref, dwv_ref):
                ref[...] = jnp.zeros_like(ref)

        z = z_ref[...]
        c, a, b = c_ref[...], a_ref[...], b_ref[...]
        ps = ps_ref[...]
        u = z[:, :POOL_DIM]
        pooled = _pooled(up_ref[...] * (i > 0).astype(F32), u, i * ts)
        dyp_ = dyp_ref[...]
        dps_ref[...] += _rowsum(dyp_ * _blockdot(pooled, pw_ref, 4, LANES))
        ext = jnp.concatenate([dyp_, dypn_ref[...] * (i < n - 1).astype(F32)], axis=0) * ps
        for gidx in range(4):
            sl = slice(gidx * LANES, (gidx + 1) * LANES)
            dpw_ref[gidx] += _dot_tn(pooled[:, sl], ext[:ts, sl])
        dpooled = jnp.concatenate(
            [_dot_nt(ext[:, gidx * LANES:(gidx + 1) * LANES], pw_ref[gidx]) for gidx in range(4)], axis=1)
        dm = dpooled / _pool_cnt(i * ts, ts + POOL_HALO)
        du = _pool_windows(dm, -1)[:ts] - dpooled[:ts]
        cq = z[:, 512:768]
        cqn, rstd_q = _rms(cq, qg_ref[...])
        dqf = _rope_bwd(dq_ref[...] * ATTN_SCALE, c, a, b)
        dwq_ref[...] += _dot_tn(cqn, dqf)
        dcq, dqg_rows = _rms_bwd(cq, qg_ref[...], rstd_q, _dot_nt(dqf, wq_ref[...]))
        dqg_ref[...] += _rowsum(dqg_rows)
        ckv = z[:, 768:896]
        ckvn, rstd_kv = _rms(ckv, kvg_ref[...])
        dk_, dv_ = dk_ref[...], dv_ref[...]
        dwk_ref[...] += _dot_tn(ckvn, dk_)
        dwv_ref[...] += _dot_tn(ckvn, dv_)
        dckv, dkvg_rows = _rms_bwd(ckv, kvg_ref[...], rstd_kv,
                                   _dot_nt(dk_, wk_ref[...]) + _dot_nt(dv_, wv_ref[...]))
        dkvg_ref[...] += _rowsum(dkvg_rows)
        dkr = dk_[:, :LANES]
        for h in range(1, MLA_HEADS):
            dkr = dkr + dk_[:, h * LANES:(h + 1) * LANES]
        lane = lax.broadcasted_iota(jnp.int32, (ts, LANES), 1)
        dkr = jnp.where((lane >= 64) & (lane < 96), _rope_bwd(dkr, c, a, b), 0.0)
        dz = jnp.concatenate([du, dcq, dckv, dkr], axis=1).astype(BF16)
        x_ = x_ref[...]
        h, rstd = _rms(x_, g_ref[...])
        dxn, dgr = _rms_bwd(x_, g_ref[...], rstd, _dot_nt(dz, win_ref[...]))
        dx_ref[...] = dy_ref[...] + dxn
        dg_ref[...] += _rowsum(dgr)
        _accumulate_tn(acc_ref, dwin_ref, h, dz, n)

    ins = [x, dy, z, z, dq, dk, dv, dyp, dyp, *tabs, g, win, pw, pscale, qg, wq, kvg, wk, wv]
    in_specs = [_row(ts, D), _row(ts, D), _row(ts, D), _prev(POOL_HALO, POOL_DIM, ts), _row(ts, D), _row(ts, D),
                _row(ts, D), _row(ts, POOL_DIM), _next(POOL_HALO, POOL_DIM, ts, n), _row(ts, LANES),
                _row(ts, LANES), _row(ts, LANES)] + [_const(v.shape) for v in ins[12:]]
    acc_shapes = [(1, D), (4, LANES, LANES), (1, POOL_DIM), (1, Q_LORA), (Q_LORA, D), (1, KV_LORA), (KV_LORA, D),
                  (KV_LORA, D)]
    return pl.pallas_call(
        body, name="even_pre_bwd", grid=(n,), in_specs=in_specs,
        out_specs=[_row(ts, D)] + [_acc(s) for s in acc_shapes] + [_acc((D, D))],
        out_shape=[_sds((S, D), F32)] + [_sds(s, F32) for s in acc_shapes] + [_sds((D, D), BF16)],
        scratch_shapes=[pltpu.VMEM((D, D), F32)], compiler_params=_cp())(*ins)


def _pick(n, options):
    for o in options:
        if n % o == 0:
            return o
    return n


def matmul_tn(name, a, b):
    out_dtype = BF16
    S = a.shape[-2]
    ts = _tile_rows(S, 2048)
    steps = S // ts

    def body(a_ref, b_ref, o_ref, acc_ref):
        s = pl.program_id(2)

        @pl.when(s == 0)
        def _():
            acc_ref[...] = jnp.zeros_like(acc_ref)

        acc_ref[...] += _dot_tn(a_ref[...], b_ref[...])

        @pl.when(s == steps - 1)
        def _():
            o_ref[...] = acc_ref[...].astype(o_ref.dtype)

    if a.ndim == 3:
        C, _, K = a.shape
        N = b.shape[1]
        tn = _pick(N, (1024, 512, 256, 128))
        grid = (C, N // tn, S // ts)
        in_specs = [pl.BlockSpec((None, ts, K), lambda c, j, s: (c, s, 0)),
                    pl.BlockSpec((ts, tn), lambda c, j, s: (s, j))]
        out_spec, out_shape, tile = pl.BlockSpec((None, K, tn), lambda c, j, s: (c, 0, j)), (C, K, N), (K, tn)
    elif b.ndim == 3:
        C, _, N = b.shape
        K = a.shape[1]
        tk = _pick(K, (1024, 512, 256, 128))
        grid = (C, K // tk, S // ts)
        in_specs = [pl.BlockSpec((ts, tk), lambda c, i, s: (s, i)),
                    pl.BlockSpec((None, ts, N), lambda c, i, s: (c, s, 0))]
        out_spec, out_shape, tile = pl.BlockSpec((None, tk, N), lambda c, i, s: (c, i, 0)), (C, K, N), (tk, N)
    else:
        K, N = a.shape[1], b.shape[1]
        tk = _pick(K, (1024, 512, 256, 128))
        tn = _pick(N, (1024, 512, 256, 128))
        grid = (K // tk, N // tn, S // ts)
        in_specs = [pl.BlockSpec((ts, tk), lambda i, j, s: (s, i)), pl.BlockSpec((ts, tn), lambda i, j, s: (s, j))]
        out_spec, out_shape, tile = pl.BlockSpec((tk, tn), lambda i, j, s: (i, j)), (K, N), (tk, tn)
    return pl.pallas_call(
        body, name=name, grid=grid, in_specs=in_specs, out_specs=out_spec, out_shape=_sds(out_shape, out_dtype),
        scratch_shapes=[pltpu.VMEM(tile, F32)], compiler_params=pltpu.CompilerParams(dimension_semantics=("arbitrary",) * 3, vmem_limit_bytes=VMEM_LIMIT))(
            a, b)


def _my_id():
    return lax.axis_index("x") * 4 + lax.axis_index("y") * 2 + lax.axis_index("c")


def _peer(j):
    x, y, c = lax.axis_index("x"), lax.axis_index("y"), lax.axis_index("c")
    px = 1 - x if j & 4 else x
    py = 1 - y if j & 2 else y
    pc = 1 - c if j & 1 else c
    return (px, py, pc), px * 4 + py * 2 + pc


def all_gather(name, arrays):
    n = len(arrays)

    def body(*refs):
        ins, outs = refs[:n], refs[n:2 * n]
        send_sems, recv_sems, local_sems = refs[2 * n:]
        me = _my_id()
        local = [pltpu.make_async_copy(ins[k], outs[k].at[me], local_sems.at[k]) for k in range(n)]
        for cp in local:
            cp.start()
        sends = []
        for j in range(1, N_DEV):
            peer, _ = _peer(j)
            for k in range(n):
                cp = pltpu.make_async_remote_copy(
                    src_ref=ins[k], dst_ref=outs[k].at[me], send_sem=send_sems.at[k, j - 1],
                    recv_sem=recv_sems.at[k, j - 1], device_id=peer, device_id_type=pl.DeviceIdType.MESH)
                cp.start()
                sends.append(cp)
        for j in range(1, N_DEV):
            peer, pid = _peer(j)
            for k in range(n):
                pltpu.make_async_remote_copy(
                    src_ref=ins[k], dst_ref=outs[k].at[pid], send_sem=send_sems.at[k, j - 1],
                    recv_sem=recv_sems.at[k, j - 1], device_id=peer, device_id_type=pl.DeviceIdType.MESH).wait_recv()
        for cp in sends:
            cp.wait_send()
        for cp in local:
            cp.wait()

    any_spec = pl.BlockSpec(memory_space=pl.ANY)
    return pl.pallas_call(
        body, name=name, in_specs=[any_spec] * n, out_specs=[any_spec] * n,
        out_shape=[_sds((N_DEV,) + a.shape, a.dtype) for a in arrays],
        scratch_shapes=[pltpu.SemaphoreType.DMA((n, N_DEV - 1)), pltpu.SemaphoreType.DMA((n, N_DEV - 1)),
                        pltpu.SemaphoreType.DMA((n,))],
        compiler_params=pltpu.CompilerParams(has_side_effects=True))(*arrays)


def exchange(name, arrays, gathers=()):
    n_ex, n = len(arrays), len(arrays) + len(gathers)

    def body(*refs):
        ins, outs = refs[:n], refs[n:2 * n]
        send_sems, recv_sems, local_sems = refs[2 * n:]
        me = _my_id()

        def mine(k, slot):
            return ins[k].at[slot] if k < n_ex else ins[k]

        local = [pltpu.make_async_copy(mine(k, me), outs[k].at[me], local_sems.at[k]) for k in range(n)]
        for cp in local:
            cp.start()
        sends = []
        for j in range(1, N_DEV):
            peer, pid = _peer(j)
            for k in range(n):
                cp = pltpu.make_async_remote_copy(
                    src_ref=mine(k, pid), dst_ref=outs[k].at[me], send_sem=send_sems.at[k, j - 1],
                    recv_sem=recv_sems.at[k, j - 1], device_id=peer, device_id_type=pl.DeviceIdType.MESH)
                cp.start()
                sends.append(cp)
        for j in range(1, N_DEV):
            peer, pid = _peer(j)
            for k in range(n):
                pltpu.make_async_remote_copy(
                    src_ref=mine(k, me), dst_ref=outs[k].at[pid], send_sem=send_sems.at[k, j - 1],
                    recv_sem=recv_sems.at[k, j - 1], device_id=peer, device_id_type=pl.DeviceIdType.MESH).wait_recv()
        for cp in sends:
            cp.wait_send()
        for cp in local:
            cp.wait()

    any_spec = pl.BlockSpec(memory_space=pl.ANY)
    return pl.pallas_call(
        body, name=name, in_specs=[any_spec] * n, out_specs=[any_spec] * n,
        out_shape=[_sds(a.shape, a.dtype) for a in arrays] + [_sds((N_DEV,) + a.shape, a.dtype) for a in gathers],
        scratch_shapes=[pltpu.SemaphoreType.DMA((n, N_DEV - 1)), pltpu.SemaphoreType.DMA((n, N_DEV - 1)),
                        pltpu.SemaphoreType.DMA((n,))],
        compiler_params=pltpu.CompilerParams(has_side_effects=True))(*arrays, *gathers)


_HBM = pl.BlockSpec(memory_space=pltpu.HBM)
_SEM = pl.BlockSpec(memory_space=pltpu.SEMAPHORE)
_DATAFLOW = pltpu.SideEffectType.DATAFLOW_SIDE_EFFECTING


def _in_hbm(v):
    return pltpu.with_memory_space_constraint(v, pltpu.HBM)


N_PEERS = N_DEV - 1


def _split_copy(k, j, srcs, lands, send_sems, recv_sems, gather, slot):
    peer, pid = _peer(j)
    return pltpu.make_async_remote_copy(
        src_ref=srcs[k] if _flag(gather, k) else srcs[k].at[pid],
        dst_ref=lands[k].at[_my_id() if slot == "mine" else pid],
        send_sem=send_sems[j - 1], recv_sem=recv_sems[j - 1], device_id=peer, device_id_type=pl.DeviceIdType.MESH)


def _flag(gather, k):
    return gather[k] if isinstance(gather, tuple) else gather


def split_start(name, arrays, gather):
    n = len(arrays)
    lands = [lax.empty((N_DEV,) + a.shape if _flag(gather, k) else a.shape, a.dtype) for k, a in enumerate(arrays)]

    def body(*refs):
        srcs, lnds = refs[:n], refs[n:2 * n]
        sems = refs[4 * n:4 * n + 2 * N_PEERS]
        token = refs[-1]
        for j in range(1, N_DEV):
            for k in range(n):
                _split_copy(k, j, srcs, lnds, sems[:N_PEERS], sems[N_PEERS:], gather, "mine").start()
        token[...] = jnp.zeros_like(token)

    out = pl.pallas_call(
        body, name=name,
        out_shape=(*[pltpu.HBM(a.shape, a.dtype) for a in arrays], *[pltpu.HBM(l.shape, l.dtype) for l in lands],
                   *[pltpu.SemaphoreType.DMA(())] * (2 * N_PEERS), _sds((8, LANES), F32)),
        in_specs=[_HBM] * (2 * n),
        out_specs=(*[_HBM] * (2 * n), *[_SEM] * (2 * N_PEERS), pl.BlockSpec(memory_space=pltpu.VMEM)),
        input_output_aliases={k: k for k in range(2 * n)},
        compiler_params=pltpu.CompilerParams(has_side_effects=_DATAFLOW))(
            *[_in_hbm(a) for a in arrays], *[_in_hbm(l) for l in lands])
    sems = list(out[2 * n:2 * n + 2 * N_PEERS])
    return sems[:N_PEERS], sems[N_PEERS:], list(out[:n]), list(out[n:2 * n]), out[-1]


def split_wait(name, handle, after, gather):
    send_sems, recv_sems, srcs, lands, _ = handle
    n = len(srcs)

    def body(*refs):
        srcs_r, lnds_r = refs[:n], refs[n:2 * n]
        sems = refs[2 * n:2 * n + 2 * N_PEERS]
        for j in range(1, N_DEV):
            for k in range(n):
                cp = _split_copy(k, j, srcs_r, lnds_r, sems[:N_PEERS], sems[N_PEERS:], gather, "peer")
                cp.wait_send()
                cp.wait_recv()

    out = pl.pallas_call(
        body, name=name, out_shape=tuple(pltpu.HBM(a.shape, a.dtype) for a in srcs + lands),
        in_specs=[_HBM] * (2 * n) + [_SEM] * (2 * N_PEERS) + [pl.BlockSpec(memory_space=pl.ANY)],
        out_specs=tuple([_HBM] * (2 * n)), input_output_aliases={k: k for k in range(2 * n)},
        compiler_params=pltpu.CompilerParams(has_side_effects=_DATAFLOW))(
            *srcs, *lands, *send_sems, *recv_sems, after)
    return list(out[:n]), list(out[n:])


def _fill_own_slot(src, land, gather):
    me = _my_id()
    own = src[None] if gather else lax.dynamic_index_in_dim(src, me, 0, keepdims=True)
    return lax.dynamic_update_slice_in_dim(land, own, me, 0)


ADAMW_BLOCK_ELEMS = 128 * 1024


def adamw(name, parts, w, m, v):
    R, C = w.shape
    tr = _pick(R, [t for t in (512, 256, 128, 64, 32, 16, 8) if t * C <= ADAMW_BLOCK_ELEMS])
    c1 = 1.0 - ADAM_B1 ** ADAM_STEP
    c2 = 1.0 - ADAM_B2 ** ADAM_STEP

    def body(p_ref, w_ref, m_ref, v_ref, g_ref, d_ref, nm_ref, nv_ref):
        g = p_ref[0].astype(F32)
        for s in range(1, N_DEV):
            g = g + p_ref[s].astype(F32)
        g_ref[...] = g
        m_ = ADAM_B1 * m_ref[...] + (1.0 - ADAM_B1) * g
        v_ = ADAM_B2 * v_ref[...] + (1.0 - ADAM_B2) * (g * g)
        nm_ref[...] = m_
        nv_ref[...] = v_
        d_ref[...] = -ADAM_LR * ((m_ / c1) / (jnp.sqrt(v_ / c2) + ADAM_EPS) + ADAM_WD * w_ref[...])

    row = pl.BlockSpec((tr, C), lambda i: (i, 0))
    return pl.pallas_call(
        body, name=name, grid=(R // tr,),
        in_specs=[pl.BlockSpec((N_DEV, tr, C), lambda i: (0, i, 0)), row, row, row], out_specs=[row] * 4,
        out_shape=[_sds((R, C), F32)] * 4, compiler_params=_cp())(parts, w, m, v)


WEIGHTS = ['ev_norm', 'ev_w_in', 'ev_pool_w', 'ev_pool_scale', 'ev_q_norm', 'ev_w_q_up', 'ev_kv_norm', 'ev_w_kv_up',
           'ev_w_out', 'od_norm', 'od_w_in', 'od_conv_w', 'od_conv_b', 'od_w_rgate', 'od_b_rgate', 'od_w_igate',
           'od_b_igate', 'od_lambda', 'od_w_out', 'xa_norm_x', 'xa_norm_mem', 'xa_w_q', 'xa_w_kv', 'xa_w_o',
           'ffn_norm', 'ffn_w_gate_up', 'ffn_w_down', 'final_norm']
SHARD_AXIS = {'ev_w_in': 1, 'ev_w_q_up': 2, 'ev_w_kv_up': 2, 'ev_w_out': 1, 'od_norm': 1, 'od_w_in': 2,
              'od_conv_w': 2, 'od_conv_b': 1, 'od_w_rgate': 2, 'od_b_rgate': 1, 'od_w_igate': 2, 'od_b_igate': 1,
              'od_lambda': 1, 'od_w_out': 1, 'xa_w_q': 1, 'xa_w_kv': 2, 'xa_w_o': 1, 'ffn_w_gate_up': 2,
              'ffn_w_down': 1}
SMALL_F32 = ('od_norm', 'od_conv_w', 'od_conv_b', 'od_b_rgate', 'od_b_igate', 'od_lambda')
STACKED = ('ffn_w_gate_up', 'ffn_w_down')
SHARDED = [n for n in WEIGHTS if n in SHARD_AXIS]
REPLICATED = [n for n in WEIGHTS if n not in SHARD_AXIS]
ROW_ALIGN = 512


def _pack(flats, dtype):
    v = jnp.concatenate([f.reshape(-1).astype(dtype) for f in flats])
    pad = (-v.shape[0]) % (ROW_ALIGN * LANES)
    return jnp.pad(v, (0, pad)).reshape(-1, LANES)


def _rows8(n_elems):
    return -(-n_elems // (8 * LANES)) * 8


def _pack_rows(arrays, lead=False):
    out = []
    for a in arrays:
        r = a.reshape((N_DEV, -1, LANES) if lead else (-1, LANES))
        pad = _rows8(r.shape[-2] * LANES) - r.shape[-2]
        out.append(jnp.pad(r, [(0, 0)] * (r.ndim - 2) + [(0, pad), (0, 0)]))
    return jnp.concatenate(out, axis=-2)


def _unpack_rows(buf, shapes, lead=False):
    out, off = [], 0
    for s in shapes:
        n = 1
        for d in s:
            n *= d
        rows = buf[..., off:off + n // LANES, :]
        out.append(rows.reshape(((N_DEV,) if lead else ()) + tuple(s)))
        off += _rows8(n)
    return out


def _unpack(flat, shapes):
    out, off = [], 0
    v = flat.reshape(-1)
    for s in shapes:
        n = 1
        for d in s:
            n *= d
        out.append(v[off:off + n].reshape(s))
        off += n
    return out


def _to_full(stacked, axis):
    v = jnp.moveaxis(stacked, 0, axis)
    s = v.shape
    return v.reshape(s[:axis] + (s[axis] * s[axis + 1],) + s[axis + 2:])


def _to_shards(full, axis):
    s = full.shape
    v = full.reshape(s[:axis] + (N_DEV, s[axis] // N_DEV) + s[axis + 1:])
    return jnp.moveaxis(v, axis, 0)


def _pad_heads(w, nh, dh, lead):
    s = w.shape
    v = w.reshape(s[:-1] + (nh, dh))
    v = jnp.pad(v, [(0, 0)] * (len(s) - 1) + [(0, 0), (lead, LANES - dh - lead)])
    return v.reshape(s[:-1] + (nh * LANES,))


def _unpad_heads(w, nh, dh, lead):
    s = w.shape
    return w.reshape(s[:-1] + (nh, LANES))[..., lead:lead + dh].reshape(s[:-1] + (nh * dh,))


def _rope_tables(positions):
    inv_freq = 10000.0 ** (-jnp.arange(0, 32, 2, dtype=F32) / 32)
    ang = positions.astype(F32)[:, None] * inv_freq
    cos, sin = jnp.tile(jnp.cos(ang), (1, LANES // 16)), jnp.tile(jnp.sin(ang), (1, LANES // 16))
    lane = lax.broadcasted_iota(jnp.int32, cos.shape, 1)
    c = jnp.where((lane >= 64) & (lane < 96), cos, 1.0)
    a = jnp.where((lane >= 80) & (lane < 96), sin, 0.0)
    b = jnp.where((lane >= 64) & (lane < 80), -sin, 0.0)
    return c, a, b


def _t(w):
    return jnp.swapaxes(w, -1, -2)


def device_step(x, mem, positions, target, W, fwd_token=None, late_weights=None, ship_grads=None):
    S = x.shape[0]
    G = {}
    tabs = _rope_tables(positions)
    keep = (positions != 0).astype(F32)[:, None]
    row = lambda v: v.reshape(1, -1)

    w_in = W['ev_w_in'][0]
    ev_win = jnp.concatenate([w_in[:, :896], _pad_heads(w_in[:, 896:], 1, 32, 64)], axis=1)
    ev_wq = _pad_heads(W['ev_w_q_up'][0], MLA_HEADS, QK_DIM, 0)
    kvw = W['ev_w_kv_up'][0].reshape(KV_LORA, MLA_HEADS, 128)
    ev_wk = _pad_heads(kvw[:, :, :64].reshape(KV_LORA, 512), MLA_HEADS, 64, 0)
    ev_wv = _pad_heads(kvw[:, :, 64:].reshape(KV_LORA, 512), MLA_HEADS, 64, 0)
    ev_wo_pool = W['ev_w_out'][0][:POOL_DIM]
    ev_wo_att = _t(_pad_heads(_t(W['ev_w_out'][0][POOL_DIM:]), MLA_HEADS, 64, 0))
    pw = W['ev_pool_w'][0].astype(BF16)
    ev_g, ps, qg, kvg = row(W['ev_norm'][0]), row(W['ev_pool_scale'][0]), row(W['ev_q_norm'][0]), row(W['ev_kv_norm'][0])

    z0, qp, kp, vp, ypool = even_pre(x, tabs, ev_g, ev_win, pw, ps, qg, ev_wq, kvg, ev_wk, ev_wv)
    o_att, lse = attn_fwd(qp, kp, vp, fwd_token)
    if late_weights is not None:
        W = {**W, **late_weights(lse)}
    x1 = even_post(x, ypool, o_att, ev_wo_pool, ev_wo_att)

    def xa_ffn_fwd(xin, l, head=()):
        mn, km, vm = mem_kv(mem, row(W['xa_norm_mem'][l]), W['xa_w_kv'][l])
        xm = xattn_fwd(xin, row(W['xa_norm_x'][l]), W['xa_w_q'][l], km, vm, W['xa_w_o'][l])
        *xo, hf, gu = (ffn_fwd_loss if head else ffn_fwd)(
            xm, row(W['ffn_norm'][l]), W['ffn_w_gate_up'], l, W['ffn_w_down'][:, l].reshape(FF_HALF, FF_CHUNK, D),
            *head)
        return xm, (xo if head else xo[0]), (mn, km, vm, hf, gu)

    x2, x3, memkv0 = xa_ffn_fwd(x1, 0)

    od_g, lam = row(W['od_norm'][0]), row(W['od_lambda'][0])
    cw, cb = W['od_conv_w'][0], row(W['od_conv_b'][0])
    wr, wi = W['od_w_rgate'][0], W['od_w_igate'][0]
    br, bi = row(W['od_b_rgate'][0]), row(W['od_b_igate'][0])
    z1, a_t, b_t, xb1, r1, ig1 = odd_pre(x3, keep, od_g, W['od_w_in'][0], cw, cb, wr, br, wi, bi, lam)
    hseq = lru_scan(a_t, b_t)
    x4 = odd_post(x3, z1, hseq, W['od_w_out'][0])
    x5, (dx, g_final, loss), memkv1 = xa_ffn_fwd(x4, 1, (target, row(W['final_norm'])))
    G['final_norm'] = g_final.reshape(D)

    gnx, gnm, gwq, gwkv, gwo, gfn, gwgu, gwd = ([None, None] for _ in range(8))

    def xa_ffn_bwd(dy, xin, xm, memkv, l):
        mn, km, vm, hf, gu = memkv
        fg = row(W['ffn_norm'][l])
        dxm, dfg, act, dgu = ffn_bwd(xm, dy, gu, fg, W['ffn_w_gate_up'], l,
                                     W['ffn_w_down'][:, l].reshape(FF_HALF, FF_CHUNK, D))
        gwd[l] = matmul_tn("ffn_dwd", act, dy).reshape(N_DEV, D_FF // N_DEV, D)
        gwgu[l] = matmul_tn("ffn_dwgu", hf, dgu)
        gfn[l] = dfg[0]
        dxin, o, dq, hx, dgx, dk, dv = xattn_bwd(xin, dxm, row(W['xa_norm_x'][l]), W['xa_w_q'][l], km, vm,
                                                  W['xa_w_o'][l])
        gnx[l] = dgx[0]
        gwo[l] = matmul_tn("xa_dwo", o, dxm)
        gwq[l] = matmul_tn("xa_dwq", hx, dq)
        dkv, dgm = mem_bwd(mem, row(W['xa_norm_mem'][l]), dk, dv, W['xa_w_kv'][l])
        gnm[l] = dgm[0]
        gwkv[l] = matmul_tn("xa_dwkv", mn, dkv)
        return dxin

    dx4 = xa_ffn_bwd(dx, x4, x5, memkv1, 1)

    dgate, dhs, g_od_wout = odd_post_bwd(dx4, z1, hseq, W['od_w_out'][0])
    G['od_w_out'] = g_od_wout[None]
    lam_grad = lru_scan(a_t, dhs, reverse=True)
    dxb, dcb, dbr, dbi, dlam, dwr, dwi = odd_gates_bwd(xb1, r1, ig1, lam_grad, hseq, keep, wr, wi, lam)
    dx3, dcw, dg_od, g_od_win = odd_pre_bwd(x3, dx4, z1, dxb, dgate, od_g, cw, W['od_w_in'][0])
    G['od_w_in'] = g_od_win[None]
    G['od_norm'], G['od_conv_w'], G['od_conv_b'] = dg_od, dcw[None], dcb
    G['od_w_rgate'], G['od_b_rgate'], G['od_w_igate'], G['od_b_igate'], G['od_lambda'] = (
        dwr[None], dbr, dwi[None], dbi, dlam)

    dx1 = xa_ffn_bwd(dx3, x1, x2, memkv0, 0)
    G['xa_norm_x'], G['xa_norm_mem'], G['ffn_norm'] = jnp.stack(gnx), jnp.stack(gnm), jnp.stack(gfn)
    G['xa_w_q'], G['xa_w_kv'], G['xa_w_o'] = jnp.stack(gwq), jnp.stack(gwkv), jnp.stack(gwo)
    G['ffn_w_gate_up'], G['ffn_w_down'] = jnp.stack(gwgu, axis=1), jnp.stack(gwd, axis=1)
    bwd_token = ship_grads(G) if ship_grads is not None else None

    dyp, do_att, delta, g_wo_pool, g_wo_att = even_post_bwd(dx1, ypool, o_att, ev_wo_pool, ev_wo_att)
    G['ev_w_out'] = jnp.concatenate([g_wo_pool, _t(_unpad_heads(_t(g_wo_att), MLA_HEADS, 64, 0))], axis=0)[None]
    dq, dk, dv = attn_bwd(qp, kp, vp, do_att, lse, delta, bwd_token)
    (grad_x, dg_ev, dpw, dps, dqg, dwq, dkvg, dwk, dwv, g_win) = even_pre_bwd(
        x, dx1, z0, dq, dk, dv, dyp, tabs, ev_g, ev_win, pw, ps, qg, ev_wq, kvg, ev_wk, ev_wv)
    G['ev_w_in'] = jnp.concatenate([g_win[:, :896], _unpad_heads(g_win[:, 896:], 1, 32, 64)], axis=1)[None]
    G['ev_norm'], G['ev_pool_w'], G['ev_pool_scale'], G['ev_q_norm'], G['ev_kv_norm'] = (
        dg_ev, dpw[None], dps, dqg, dkvg)
    G['ev_w_q_up'] = _unpad_heads(dwq, MLA_HEADS, QK_DIM, 0)[None]
    gk = _unpad_heads(dwk, MLA_HEADS, 64, 0).reshape(KV_LORA, MLA_HEADS, 64)
    gv = _unpad_heads(dwv, MLA_HEADS, 64, 0).reshape(KV_LORA, MLA_HEADS, 64)
    G['ev_w_kv_up'] = jnp.concatenate([gk, gv], axis=2).reshape(1, KV_LORA, MLA_HEADS * 128)
    return loss[0, 0], grad_x, G


def kernel(x, mem, positions, ev_norm, ev_w_in, ev_pool_w, ev_pool_scale, ev_q_norm, ev_w_q_up, ev_kv_norm, ev_w_kv_up, ev_w_out, od_norm, od_w_in, od_conv_w, od_conv_b, od_w_rgate, od_b_rgate, od_w_igate, od_b_igate, od_lambda, od_w_out, xa_norm_x, xa_norm_mem, xa_w_q, xa_w_kv, xa_w_o, ffn_norm, ffn_w_gate_up, ffn_w_down, final_norm, loss_target, m_ev_norm, m_ev_w_in, m_ev_pool_w, m_ev_pool_scale, m_ev_q_norm, m_ev_w_q_up, m_ev_kv_norm, m_ev_w_kv_up, m_ev_w_out, m_od_norm, m_od_w_in, m_od_conv_w, m_od_conv_b, m_od_w_rgate, m_od_b_rgate, m_od_w_igate, m_od_b_igate, m_od_lambda, m_od_w_out, m_xa_norm_x, m_xa_norm_mem, m_xa_w_q, m_xa_w_kv, m_xa_w_o, m_ffn_norm, m_ffn_w_gate_up, m_ffn_w_down, m_final_norm, v_ev_norm, v_ev_w_in, v_ev_pool_w, v_ev_pool_scale, v_ev_q_norm, v_ev_w_q_up, v_ev_kv_norm, v_ev_w_kv_up, v_ev_w_out, v_od_norm, v_od_w_in, v_od_conv_w, v_od_conv_b, v_od_w_rgate, v_od_b_rgate, v_od_w_igate, v_od_b_igate, v_od_lambda, v_od_w_out, v_xa_norm_x, v_xa_norm_mem, v_xa_w_q, v_xa_w_kv, v_xa_w_o, v_ffn_norm, v_ffn_w_gate_up, v_ffn_w_down, v_final_norm):
    args = dict(locals())
    w = {n: args[n] for n in WEIGHTS}
    m = {n: args['m_' + n] for n in WEIGHTS}
    v = {n: args['v_' + n] for n in WEIGHTS}
    big = [n for n in SHARDED if n not in SMALL_F32]
    small = [n for n in SHARDED if n in SMALL_F32]

    small_shapes = [w[n].shape for n in small]
    first = [n for n in big if n.startswith('ev_')]
    late = [n for n in big if n not in first]

    def full(n, st):
        return st if n in STACKED else _to_full(st, SHARD_AXIS[n])

    W = {n: w[n] for n in REPLICATED}
    W.update((n, full(n, st)) for n, st in zip(first, all_gather("gather_ev_weights", [w[n].astype(BF16) for n in first])))
    gather = split_start("gather_start", [w[n].astype(BF16) for n in late] + [_pack_rows([w[n] for n in small])], True)

    def late_weights(after):
        srcs, lands = split_wait("gather_wait", gather, after, True)
        lands = [_fill_own_slot(s, l, True) for s, l in zip(srcs, lands)]
        out = {n: full(n, st) for n, st in zip(late, lands)}
        out.update((n, _to_full(st, SHARD_AXIS[n])) for n, st in zip(small, _unpack_rows(lands[-1], small_shapes, True)))
        return out

    def shards(G, n):
        return G[n] if n in STACKED else _to_shards(G[n], SHARD_AXIS[n])

    shipped = []

    def ship_grads(G):
        shipped.append(split_start("exchange_start", [shards(G, n).astype(BF16) for n in late] +
                                   [_pack_rows([shards(G, n) for n in small], lead=True)], False))
        return shipped[0][-1]

    loss, grad_x, G = device_step(x[0], mem[0], positions[0], loss_target[0], W, gather[-1], late_weights, ship_grads)
    outs = [{}, {}, {}, {}]

    last_flags = (False,) * len(first) + (True,)
    last = split_start("last_start", [shards(G, n).astype(BF16) for n in first] + [_pack(
        [G[n] for n in REPLICATED] + [jnp.broadcast_to(loss, (LANES,))], F32)], last_flags)

    two_d = lambda a: a.reshape(-1, a.shape[-1])

    def update(names, parts):
        for n, p in zip(names, parts):
            res = adamw("adamw_" + n, p.reshape((N_DEV,) + two_d(w[n]).shape), two_d(w[n]), two_d(m[n]),
                        two_d(v[n]))
            for k in range(4):
                outs[k][n] = res[k].reshape(w[n].shape)
        return res[0]

    srcs, lands = split_wait("exchange_wait", shipped[0], last[-1], False)
    late_parts = [_fill_own_slot(s, l, False) for s, l in zip(srcs, lands)]
    after = update(late, late_parts)
    res = adamw("adamw_small", late_parts[-1], *[_pack_rows([d[n] for n in small]) for d in (w, m, v)])
    for k in range(4):
        outs[k].update(zip(small, _unpack_rows(res[k], small_shapes)))

    srcs, lands = split_wait("last_wait", last, after, last_flags)
    *first_parts, rep_parts = [_fill_own_slot(s, l, f) for s, l, f in zip(srcs, lands, last_flags)]
    update(first, first_parts)

    rep_shapes = [w[n].shape for n in REPLICATED] + [(LANES,)]
    zero = jnp.zeros((LANES,), F32)
    rep = adamw("adamw_rep", rep_parts, *[_pack([d[n] for n in REPLICATED] + [zero], F32) for d in (w, m, v)])
    for k in range(4):
        outs[k].update(zip(REPLICATED + ['loss'], _unpack(rep[k], rep_shapes)))
    loss = outs[0]['loss'][0]

    return (loss, grad_x[None], *[outs[0][n] for n in WEIGHTS], *[outs[1][n] for n in WEIGHTS],
            *[outs[2][n] for n in WEIGHTS], *[outs[3][n] for n in WEIGHTS])
```

```python
import functools

import jax
import jax.numpy as jnp
from jax import lax
from jax.experimental import pallas as pl
from jax.experimental.pallas import tpu as pltpu

F32, BF16 = jnp.float32, jnp.bfloat16
N_DEV = 8
D = 1024
POOL_DIM = 512
POOL_WINDOWS = (2, 4, 8, 16)
MLA_HEADS = 8
QK_DIM = 96
Q_LORA, KV_LORA = 256, 128
LRU_HEADS, LRU_HEAD_DIM = 4, 256
LRU_C = 8.0
MEM_HEADS, MEM_HEAD_DIM = 4, 256
D_FF = 2816
RMS_EPS = 1e-6
ADAM_LR, ADAM_B1, ADAM_B2, ADAM_EPS, ADAM_WD, ADAM_STEP = 0.001, 0.9, 0.999, 1e-08, 0.01, 10
LANES = 128
POOL_HALO = 16
CONV_HALO = 8
VMEM_LIMIT = 60000 * 1024


def _cp():
    return pltpu.CompilerParams(dimension_semantics=("arbitrary",), vmem_limit_bytes=VMEM_LIMIT)


def _cp2():
    return pltpu.CompilerParams(dimension_semantics=("arbitrary", "arbitrary"), vmem_limit_bytes=VMEM_LIMIT)


def _row(ts, c, col=0):
    return pl.BlockSpec((ts, c), lambda i: (i, col))


def _prev(hr, c, ts, col=0):
    r = ts // hr
    return pl.BlockSpec((hr, c), lambda i: (jnp.maximum(i * r - 1, 0), col))


def _next(hr, c, ts, n, col=0):
    r = ts // hr
    return pl.BlockSpec((hr, c), lambda i: (jnp.minimum((i + 1) * r, n * r - 1), col))


def _const(shape):
    nd = len(shape)
    return pl.BlockSpec(tuple(shape), lambda i: (0,) * nd, pipeline_mode=pl.Buffered(1))


def _acc(shape):
    nd = len(shape)
    return pl.BlockSpec(tuple(shape), lambda i: (0,) * nd)


def _sds(shape, dt):
    return jax.ShapeDtypeStruct(tuple(shape), dt)


def _dot(a, b):
    return jnp.dot(a.astype(BF16), b.astype(BF16), preferred_element_type=F32)


def _dot_nt(a, b):
    return lax.dot_general(a.astype(BF16), b.astype(BF16), (((1,), (1,)), ((), ())), preferred_element_type=F32)


def _dot_tn(a, b):
    return lax.dot_general(a.astype(BF16), b.astype(BF16), (((0,), (0,)), ((), ())), preferred_element_type=F32)


def _rms(x, g):
    rstd = lax.rsqrt(jnp.mean(x * x, axis=-1, keepdims=True) + RMS_EPS)
    return x * rstd * g, rstd


def _rms_bwd(x, g, rstd, dy):
    xn = x * rstd
    dyg = dy * g
    dx = rstd * (dyg - xn * jnp.mean(dyg * xn, axis=-1, keepdims=True))
    return dx, dy * xn


def _rowsum(v):
    return jnp.sum(v, axis=0, keepdims=True)


def _roll(v, s, axis):
    n = v.shape[axis]
    return pltpu.roll(v, s % n, axis)


def _rope(t, c, a, b):
    k = t.shape[1] // LANES
    if k > 1:
        c, a, b = (jnp.tile(v, (1, k)) for v in (c, a, b))
    return t * c + _roll(t, 16, 1) * a + _roll(t, -16, 1) * b


def _rope_bwd(d, c, a, b):
    k = d.shape[1] // LANES
    if k > 1:
        c, a, b = (jnp.tile(v, (1, k)) for v in (c, a, b))
    return d * c + _roll(d * a, -16, 1) + _roll(d * b, 16, 1)


def _gelu(x):
    c = 0.7978845608028654
    t = jnp.tanh(c * (x + 0.044715 * x * x * x))
    return 0.5 * x * (1.0 + t), t


def _gelu_grad(x, t):
    c = 0.7978845608028654
    return 0.5 * (1.0 + t) + 0.5 * x * (1.0 - t * t) * c * (1.0 + 3.0 * 0.044715 * x * x)


def _blockdot(v, w_ref, nblk, width):
    return jnp.concatenate(
        [_dot(v[:, j * width:(j + 1) * width], w_ref[j]) for j in range(nblk)], axis=1)


def _pool_cnt(row0, rows):
    t = row0 + lax.broadcasted_iota(jnp.int32, (rows, POOL_DIM), 0)
    w = jnp.left_shift(2, lax.broadcasted_iota(jnp.int32, (rows, POOL_DIM), 1) // LANES)
    return jnp.minimum(t + 1, w).astype(F32)


def _pool_windows(ext, sign):
    s2 = ext + _roll(ext, sign * 1, 0)
    t = s2[:, LANES:]
    s4 = t + _roll(t, sign * 2, 0)
    t = s4[:, LANES:]
    s8 = t + _roll(t, sign * 4, 0)
    t = s8[:, LANES:]
    s16 = t + _roll(t, sign * 8, 0)
    return jnp.concatenate([s2[:, :LANES], s4[:, :LANES], s8[:, :LANES], s16], axis=1)


def _pooled(uprev, u, row0):
    ts = u.shape[0]
    ext = jnp.concatenate([uprev, u], axis=0)
    sums = _pool_windows(ext, 1)[POOL_HALO:]
    return sums / _pool_cnt(row0, ts) - u


def _expm1(x):
    return jnp.where(jnp.abs(x) < 0.01, x * (1.0 + 0.5 * x * (1.0 + x * (1.0 / 3.0))), jnp.exp(x) - 1.0)


def _softplus(z):
    return jnp.maximum(z, 0.0) + jnp.log1p(jnp.exp(-jnp.abs(z)))


def _tile_rows(s, want):
    while s % want:
        want //= 2
    return want


def even_pre(x, tabs, g, win, pw, pscale, qg, wq, kvg, wk, wv):
    S = x.shape[0]
    ts = _tile_rows(S, 512)

    def body(x_ref, xp_ref, c_ref, a_ref, b_ref, g_ref, win_ref, pw_ref, ps_ref, qg_ref, wq_ref, kvg_ref,
             wk_ref, wv_ref, z_ref, q_ref, k_ref, v_ref, yp_ref):
        i = pl.program_id(0)
        h, _ = _rms(x_ref[...], g_ref[...])
        z = _dot(h, win_ref[...])
        z_ref[...] = z
        hp, _ = _rms(xp_ref[...], g_ref[...])
        uprev = _dot(hp, win_ref[:, :POOL_DIM]) * (i > 0).astype(F32)
        u = z[:, :POOL_DIM]
        pooled = _pooled(uprev, u, i * ts)
        yp_ref[...] = (_blockdot(pooled, pw_ref, 4, LANES) * ps_ref[...]).astype(BF16)
        c, a, b = c_ref[...], a_ref[...], b_ref[...]
        cqn, _ = _rms(z[:, 512:768], qg_ref[...])
        q_ref[...] = (_rope(_dot(cqn, wq_ref[...]), c, a, b) * (ATTN_SCALE * LOG2_E)).astype(BF16)
        ckvn, _ = _rms(z[:, 768:896], kvg_ref[...])
        krr = _rope(z[:, 896:1024], c, a, b)
        k_ref[...] = (_dot(ckvn, wk_ref[...]) + jnp.tile(krr, (1, MLA_HEADS))).astype(BF16)
        lane = lax.broadcasted_iota(jnp.int32, (ts, D), 1) % LANES
        v_ref[...] = jnp.where(lane == ONES_LANE, 1.0, _dot(ckvn, wv_ref[...])).astype(BF16)

    ins = [x, x, *tabs, g, win, pw, pscale, qg, wq, kvg, wk, wv]
    in_specs = [_row(ts, D), _prev(POOL_HALO, D, ts), _row(ts, LANES), _row(ts, LANES), _row(ts, LANES)]
    in_specs += [_const(v.shape) for v in ins[5:]]
    return pl.pallas_call(
        body, name="even_pre", grid=(S // ts,), in_specs=in_specs,
        out_specs=[_row(ts, D)] * 4 + [_row(ts, POOL_DIM)],
        out_shape=[_sds((S, D), F32)] + [_sds((S, D), BF16)] * 3 + [_sds((S, POOL_DIM), BF16)],
        compiler_params=_cp())(*ins)


ATTN_SCALE = QK_DIM ** -0.5
LOG2_E = 1.4426950408889634
LN_2 = 0.6931471805599453
ONES_LANE = 64


def _exp2(x):
    return jnp.exp2(x)


def _pair_loop(lo, hi, step, init, unrolls=(2, 1)):
    carry = init
    for unroll in unrolls:
        groups = (hi - lo) // unroll

        def group(j, c, lo=lo, unroll=unroll):
            for u in range(unroll):
                c = step(lo + unroll * j + u, c)
            return c

        carry = lax.fori_loop(0, groups, group, carry)
        lo = lo + unroll * groups
    return carry


def _as_row(col):
    return jnp.transpose(jnp.broadcast_to(col, (col.shape[0], LANES)))[0:1, :]


def _after(token):
    return ([], []) if token is None else ([token], [pl.BlockSpec(memory_space=pl.ANY)])


def attn_fwd(qp, kp, vp, token=None):
    S = qp.shape[0]
    tq = _tile_rows(S, 512)
    extra, extra_specs = _after(token)

    def body(q_ref, k_ref, v_ref, *rest):
        o_ref, lse_ref = rest[-2:]
        qi = pl.program_id(1)
        q = q_ref[...]

        def block(ki, carry, masked):
            m, acc = carry
            off = pl.multiple_of(ki * tq, tq)
            s = _dot_nt(q, k_ref[pl.ds(off, tq), :])
            if masked:
                row = lax.broadcasted_iota(jnp.int32, (tq, tq), 0)
                col = lax.broadcasted_iota(jnp.int32, (tq, tq), 1)
                s = jnp.where(col <= row, s, -1e30)
            m_new = jnp.maximum(m, jnp.max(s, axis=1, keepdims=True))
            acc = _exp2(m - m_new) * acc + _dot(_exp2(s - m_new), v_ref[pl.ds(off, tq), :])
            return m_new, acc

        init = (jnp.full((tq, 1), -1e30, F32), jnp.zeros((tq, LANES), F32))
        carry = _pair_loop(0, qi, lambda ki, c: block(ki, c, False), init, unrolls=(8, 4, 2, 1))
        m, acc = block(qi, carry, True)
        l = acc[:, ONES_LANE:ONES_LANE + 1]
        o_ref[...] = acc / l
        lse_ref[...] = _as_row(m + jnp.log(l) * LOG2_E)

    blk = pl.BlockSpec((tq, LANES), lambda h, i: (i, h))
    full = pl.BlockSpec((S, LANES), lambda h, i: (0, h))
    return pl.pallas_call(
        body, name="attn_fwd", grid=(MLA_HEADS, S // tq), in_specs=[blk, full, full] + extra_specs,
        out_specs=[blk, pl.BlockSpec((None, None, 1, tq), lambda h, i: (h, i, 0, 0))],
        out_shape=[_sds((S, D), F32), _sds((MLA_HEADS, S // tq, 1, tq), F32)], compiler_params=_cp2())(
            qp, kp, vp, *extra)


def even_post(x, ypool, o, wo_pool, wo_att):
    S = x.shape[0]
    ts = _tile_rows(S, 512)

    def body(x_ref, yp_ref, o_ref, wp_ref, wa_ref, out_ref):
        out_ref[...] = x_ref[...] + _dot(yp_ref[...], wp_ref[...]) + _dot(o_ref[...], wa_ref[...])

    return pl.pallas_call(
        body, name="even_post", grid=(S // ts,),
        in_specs=[_row(ts, D), _row(ts, POOL_DIM), _row(ts, D), _const(wo_pool.shape), _const(wo_att.shape)],
        out_specs=_row(ts, D), out_shape=_sds((S, D), F32), compiler_params=_cp())(x, ypool, o, wo_pool, wo_att)


def mem_kv(mem, g, wkv):
    M = mem.shape[0]

    def body(mem_ref, g_ref, w_ref, mn_ref, k_ref, v_ref):
        mn, _ = _rms(mem_ref[...], g_ref[...])
        mn_ref[...] = mn.astype(BF16)
        k_ref[...] = _dot(mn, w_ref[:, :D]).astype(BF16)
        v_ref[...] = _dot(mn, w_ref[:, D:]).astype(BF16)

    return pl.pallas_call(
        body, name="mem_kv", grid=(1,), in_specs=[_acc(mem.shape), _acc(g.shape), _acc(wkv.shape)],
        out_specs=[_acc((M, D))] * 3, out_shape=[_sds((M, D), BF16)] * 3, compiler_params=_cp())(mem, g, wkv)


def _xattn_heads(hx, wq_ref, k_ref, v_ref):
    q = _dot(hx, wq_ref[...])
    scale = MEM_HEAD_DIM ** -0.5
    ps, os_ = [], []
    for h in range(MEM_HEADS):
        sl = slice(h * MEM_HEAD_DIM, (h + 1) * MEM_HEAD_DIM)
        s = _dot_nt(q[:, sl], k_ref[:, sl]) * scale
        e = jnp.exp(s - jnp.max(s, axis=1, keepdims=True))
        p = e / jnp.sum(e, axis=1, keepdims=True)
        ps.append(p)
        os_.append(_dot(p, v_ref[:, sl]))
    return q, ps, jnp.concatenate(os_, axis=1)


def xattn_fwd(x, g, wq, kmem, vmem, wo):
    S = x.shape[0]
    ts = _tile_rows(S, 512)

    def body(x_ref, g_ref, wq_ref, k_ref, v_ref, wo_ref, out_ref):
        x_ = x_ref[...]
        hx, _ = _rms(x_, g_ref[...])
        _, _, o = _xattn_heads(hx, wq_ref, k_ref, v_ref)
        out_ref[...] = x_ + _dot(o, wo_ref[...])

    ins = [x, g, wq, kmem, vmem, wo]
    return pl.pallas_call(
        body, name="xattn_fwd", grid=(S // ts,), in_specs=[_row(ts, D)] + [_const(v.shape) for v in ins[1:]],
        out_specs=_row(ts, D), out_shape=_sds((S, D), F32), compiler_params=_cp())(*ins)


def xattn_bwd(x, dy, g, wq, kmem, vmem, wo):
    S = x.shape[0]
    M = kmem.shape[0]
    ts = _tile_rows(S, 512)
    scale = MEM_HEAD_DIM ** -0.5

    def body(x_ref, dy_ref, g_ref, wq_ref, k_ref, v_ref, wo_ref,
             dx_ref, o_ref, dq_ref, hx_ref, dg_ref, dk_ref, dv_ref):
        i = pl.program_id(0)

        @pl.when(i == 0)
        def _():
            dg_ref[...] = jnp.zeros_like(dg_ref)
            dk_ref[...] = jnp.zeros_like(dk_ref)
            dv_ref[...] = jnp.zeros_like(dv_ref)

        x_, dy_ = x_ref[...], dy_ref[...]
        hx, rstd = _rms(x_, g_ref[...])
        q, ps, o = _xattn_heads(hx, wq_ref, k_ref, v_ref)
        hx_ref[...] = hx.astype(BF16)
        o_ref[...] = o.astype(BF16)
        do = _dot_nt(dy_, wo_ref[...])
        dqs = []
        for h in range(MEM_HEADS):
            sl = slice(h * MEM_HEAD_DIM, (h + 1) * MEM_HEAD_DIM)
            p, do_h = ps[h], do[:, sl]
            dp = _dot_nt(do_h, v_ref[:, sl])
            ds = p * (dp - jnp.sum(p * dp, axis=1, keepdims=True)) * scale
            dqs.append(_dot(ds, k_ref[:, sl]))
            dk_ref[:, sl] += _dot_tn(ds, q[:, sl])
            dv_ref[:, sl] += _dot_tn(p, do_h)
        dq = jnp.concatenate(dqs, axis=1).astype(BF16)
        dq_ref[...] = dq
        dxn, dgr = _rms_bwd(x_, g_ref[...], rstd, _dot_nt(dq, wq_ref[...]))
        dx_ref[...] = dy_ + dxn
        dg_ref[...] += _rowsum(dgr)

    ins = [x, dy, g, wq, kmem, vmem, wo]
    return pl.pallas_call(
        body, name="xattn_bwd", grid=(S // ts,),
        in_specs=[_row(ts, D), _row(ts, D)] + [_const(v.shape) for v in ins[2:]],
        out_specs=[_row(ts, D)] * 4 + [_acc((1, D)), _acc((M, D)), _acc((M, D))],
        out_shape=[_sds((S, D), F32)] + [_sds((S, D), BF16)] * 3 + [_sds((1, D), F32), _sds((M, D), F32),
                                                                    _sds((M, D), F32)],
        compiler_params=_cp())(*ins)


def mem_bwd(mem, g, dk, dv, wkv):
    M = mem.shape[0]

    def body(mem_ref, g_ref, dk_ref, dv_ref, w_ref, dkv_ref, dg_ref):
        dkv = jnp.concatenate([dk_ref[...], dv_ref[...]], axis=1)
        dkv_ref[...] = dkv.astype(BF16)
        _, rstd = _rms(mem_ref[...], g_ref[...])
        dg_ref[...] = _rowsum(_dot_nt(dkv, w_ref[...]) * (mem_ref[...] * rstd))

    ins = [mem, g, dk, dv, wkv]
    return pl.pallas_call(
        body, name="mem_bwd", grid=(1,), in_specs=[_acc(v.shape) for v in ins],
        out_specs=[_acc((M, 2 * D)), _acc((1, D))], out_shape=[_sds((M, 2 * D), BF16), _sds((1, D), F32)],
        compiler_params=_cp())(*ins)


FF_CHUNK = 2 * D_FF // N_DEV
FF_HALF = N_DEV // 2


def _layer_of(w, layer):
    return pl.BlockSpec((N_DEV, None) + w.shape[2:], lambda i: (0, layer, 0, 0), pipeline_mode=pl.Buffered(1))


def _ff_chunks(c, ts):
    return pl.BlockSpec((c, ts, FF_CHUNK), lambda i: (0, i, 0))


def _ffn(x_, g_ref, wgu_ref, wd_ref, hf_ref, gu_ref):
    hf = _rms(x_, g_ref[...])[0].astype(BF16)
    hf_ref[...] = hf
    out = x_
    for j in range(FF_HALF):
        gg, uu = _dot(hf, wgu_ref[j]), _dot(hf, wgu_ref[j + FF_HALF])
        gu_ref[j] = gg.astype(BF16)
        gu_ref[j + FF_HALF] = uu.astype(BF16)
        out = out + _dot(gg * jax.nn.sigmoid(gg) * uu, wd_ref[j])
    return out


def ffn_fwd(x, g, wgu, layer, wd):
    S = x.shape[0]
    ts = _tile_rows(S, 256)

    def body(x_ref, g_ref, wgu_ref, wd_ref, out_ref, hf_ref, gu_ref):
        out_ref[...] = _ffn(x_ref[...], g_ref, wgu_ref, wd_ref, hf_ref, gu_ref)

    return pl.pallas_call(
        body, name="ffn_fwd", grid=(S // ts,),
        in_specs=[_row(ts, D), _const(g.shape), _layer_of(wgu, layer), _const(wd.shape)],
        out_specs=[_row(ts, D), _row(ts, D), _ff_chunks(N_DEV, ts)],
        out_shape=[_sds((S, D), F32), _sds((S, D), BF16), _sds((N_DEV, S, FF_CHUNK), BF16)],
        compiler_params=_cp())(x, g, wgu, wd)


def ffn_fwd_loss(x, g, wgu, layer, wd, target, gf):
    S = x.shape[0]
    ts = _tile_rows(S, 256)

    def body(x_ref, g_ref, wgu_ref, wd_ref, t_ref, gf_ref, dx_ref, dgf_ref, loss_ref, hf_ref, gu_ref):
        @pl.when(pl.program_id(0) == 0)
        def _():
            dgf_ref[...] = jnp.zeros_like(dgf_ref)
            loss_ref[...] = jnp.zeros_like(loss_ref)

        out = _ffn(x_ref[...], g_ref, wgu_ref, wd_ref, hf_ref, gu_ref)
        y, rstd = _rms(out, gf_ref[...])
        err = y - t_ref[...]
        loss_ref[...] += 0.5 * _rowsum(jnp.mean(err * err, axis=1, keepdims=True))
        dxn, dgr = _rms_bwd(out, gf_ref[...], rstd, err * (1.0 / D))
        dx_ref[...] = dxn
        dgf_ref[...] += _rowsum(dgr)

    return pl.pallas_call(
        body, name="ffn_fwd_loss", grid=(S // ts,),
        in_specs=[_row(ts, D), _const(g.shape), _layer_of(wgu, layer), _const(wd.shape), _row(ts, D),
                  _const(gf.shape)],
        out_specs=[_row(ts, D), _acc((1, D)), _acc((1, 1)), _row(ts, D), _ff_chunks(N_DEV, ts)],
        out_shape=[_sds((S, D), F32), _sds((1, D), F32), _sds((1, 1), F32), _sds((S, D), BF16),
                   _sds((N_DEV, S, FF_CHUNK), BF16)],
        compiler_params=_cp())(x, g, wgu, wd, target, gf)


def ffn_bwd(x, dy, gu, g, wgu, layer, wd):
    S = x.shape[0]
    ts = _tile_rows(S, 256)

    def body(x_ref, dy_ref, gu_ref, g_ref, wgu_ref, wd_ref, dx_ref, dg_ref, act_ref, dgu_ref):
        @pl.when(pl.program_id(0) == 0)
        def _():
            dg_ref[...] = jnp.zeros_like(dg_ref)

        dy_ = dy_ref[...]
        dyb = dy_.astype(BF16)
        dh = jnp.zeros((ts, D), F32)
        dacts = [_dot_nt(dyb, wd_ref[j]) for j in range(FF_HALF)]
        for j in range(FF_HALF):
            gg, uu = gu_ref[j].astype(F32), gu_ref[j + FF_HALF].astype(F32)
            sg = jax.nn.sigmoid(gg)
            silu = gg * sg
            act_ref[j] = (silu * uu).astype(BF16)
            dact = dacts[j]
            dgate = (dact * uu * (sg * (1.0 + gg * (1.0 - sg)))).astype(BF16)
            dup = (dact * silu).astype(BF16)
            dgu_ref[j] = dgate
            dgu_ref[j + FF_HALF] = dup
            dh = dh + _dot_nt(dgate, wgu_ref[j]) + _dot_nt(dup, wgu_ref[j + FF_HALF])
        x_ = x_ref[...]
        _, rstd = _rms(x_, g_ref[...])
        dxn, dgr = _rms_bwd(x_, g_ref[...], rstd, dh)
        dx_ref[...] = dy_ + dxn
        dg_ref[...] += _rowsum(dgr)

    return pl.pallas_call(
        body, name="ffn_bwd", grid=(S // ts,),
        in_specs=[_row(ts, D), _row(ts, D), _ff_chunks(N_DEV, ts), _const(g.shape), _layer_of(wgu, layer),
                  _const(wd.shape)],
        out_specs=[_row(ts, D), _acc((1, D)), _ff_chunks(FF_HALF, ts), _ff_chunks(N_DEV, ts)],
        out_shape=[_sds((S, D), F32), _sds((1, D), F32), _sds((FF_HALF, S, FF_CHUNK), BF16),
                   _sds((N_DEV, S, FF_CHUNK), BF16)],
        compiler_params=_cp())(x, dy, gu, g, wgu, wd)


def _conv_fwd(xprev, xbp, cw_ref, cb):
    ext = jnp.concatenate([xprev, xbp], axis=0)
    acc = cb + cw_ref[3:4, :] * xbp
    for k in range(3):
        acc = acc + cw_ref[k:k + 1, :] * _roll(ext, 3 - k, 0)[CONV_HALO:]
    return acc


def _decay(r, lam):
    sp = _softplus(-lam)
    log_a = -LRU_C * r * sp
    return sp, jnp.exp(log_a), jnp.sqrt(jnp.maximum(-_expm1(2.0 * log_a), 0.0))


def odd_pre(x, keep, g, win, cw, cb, wr, br, wi, bi, lam):
    S = x.shape[0]
    ts = _tile_rows(S, 512)

    def body(x_ref, xp_ref, keep_ref, g_ref, win_ref, cw_ref, cb_ref, wr_ref, br_ref, wi_ref, bi_ref, lam_ref,
             z_ref, a_ref, b_ref, xb_ref, r_ref, ig_ref):
        i = pl.program_id(0)
        h, _ = _rms(x_ref[...], g_ref[...])
        z = _dot(h, win_ref[...])
        z_ref[...] = z
        hp, _ = _rms(xp_ref[...], g_ref[...])
        xprev = _dot(hp, win_ref[:, D:]) * (i > 0).astype(F32)
        xb = _conv_fwd(xprev, z[:, D:], cw_ref, cb_ref[...])
        xb_ref[...] = xb
        r = jax.nn.sigmoid(_blockdot(xb, wr_ref, LRU_HEADS, LRU_HEAD_DIM) + br_ref[...])
        ig = jax.nn.sigmoid(_blockdot(xb, wi_ref, LRU_HEADS, LRU_HEAD_DIM) + bi_ref[...])
        r_ref[...] = r
        ig_ref[...] = ig
        keep_ = keep_ref[...]
        _, a, mult = _decay(r, lam_ref[...])
        a_ref[...] = a * keep_
        b_ref[...] = jnp.where(keep_ > 0.0, mult, 1.0) * (ig * xb)

    ins = [x, x, keep, g, win, cw, cb, wr, br, wi, bi, lam]
    return pl.pallas_call(
        body, name="odd_pre", grid=(S // ts,),
        in_specs=[_row(ts, D), _prev(CONV_HALO, D, ts), _row(ts, 1)] + [_const(v.shape) for v in ins[3:]],
        out_specs=[_row(ts, 2 * D)] + [_row(ts, D)] * 5,
        out_shape=[_sds((S, 2 * D), F32)] + [_sds((S, D), F32)] * 5, compiler_params=_cp())(*ins)


def lru_scan(a, b, reverse=False):
    S = a.shape[0]
    ts = _tile_rows(S, 512)
    n = S // ts
    groups = ts // 8

    def body(a_ref, an_ref, b_ref, h_ref, carry_ref, ash_ref):
        i = pl.program_id(0)

        @pl.when(i == 0)
        def _():
            carry_ref[...] = jnp.zeros_like(carry_ref)

        rid = lax.broadcasted_iota(jnp.int32, (8, D), 0)
        if reverse:
            ext = jnp.concatenate([a_ref[...], an_ref[...] * (i > 0).astype(F32)], axis=0)
            ash_ref[...] = _roll(ext, -1, 0)[:ts]
        src = ash_ref if reverse else a_ref

        def group(j, carry):
            off = pl.multiple_of((groups - 1 - j if reverse else j) * 8, 8)
            a8, b8 = src[pl.ds(off, 8), :], b_ref[pl.ds(off, 8), :]
            for k in (1, 2, 4):
                inside = (rid < 8 - k) if reverse else (rid >= k)
                sh = -k if reverse else k
                a_sh = jnp.where(inside, _roll(a8, sh, 0), 1.0)
                b_sh = jnp.where(inside, _roll(b8, sh, 0), 0.0)
                b8 = a8 * b_sh + b8
                a8 = a8 * a_sh
            h8 = a8 * carry + b8
            h_ref[pl.ds(off, 8), :] = h8
            return h8[0:1, :] if reverse else h8[7:8, :]

        carry_ref[...] = lax.fori_loop(0, groups, group, carry_ref[...], unroll=4)

    if reverse:
        r = ts // 8
        tile = pl.BlockSpec((ts, D), lambda i: (n - 1 - i, 0))
        halo = pl.BlockSpec((8, D), lambda i: (jnp.minimum((n - i) * r, n * r - 1), 0))
    else:
        tile, halo = _row(ts, D), _prev(8, D, ts)
    return pl.pallas_call(
        body, name="lru_scan_rev" if reverse else "lru_scan", grid=(n,), in_specs=[tile, halo, tile],
        out_specs=tile, out_shape=_sds((S, D), F32),
        scratch_shapes=[pltpu.VMEM((1, D), F32), pltpu.VMEM((ts, D), F32)], compiler_params=_cp())(a, a, b)


def odd_post(x, z, hseq, wout):
    S = x.shape[0]
    ts = _tile_rows(S, 512)

    def body(x_ref, gate_ref, h_ref, w_ref, out_ref):
        gl, _ = _gelu(gate_ref[...])
        out_ref[...] = x_ref[...] + _dot(gl * h_ref[...], w_ref[...])

    return pl.pallas_call(
        body, name="odd_post", grid=(S // ts,),
        in_specs=[_row(ts, D), _row(ts, D), _row(ts, D), _const(wout.shape)],
        out_specs=_row(ts, D), out_shape=_sds((S, D), F32), compiler_params=_cp())(x, z, hseq, wout)


def _accumulate_tn(acc_ref, out_ref, a, b, steps):
    i = pl.program_id(0)

    @pl.when(i == 0)
    def _():
        acc_ref[...] = jnp.zeros_like(acc_ref)

    acc_ref[...] += _dot_tn(a, b)

    @pl.when(i == steps - 1)
    def _():
        out_ref[...] = acc_ref[...].astype(out_ref.dtype)


def odd_post_bwd(dy, z, hseq, wout):
    S = dy.shape[0]
    ts = _tile_rows(S, 512)
    n = S // ts

    def body(dy_ref, gate_ref, h_ref, w_ref, dgate_ref, dh_ref, dw_ref, acc_ref):
        gate, hs, dy_ = gate_ref[...], h_ref[...], dy_ref[...]
        gl, t = _gelu(gate)
        dyy = _dot_nt(dy_, w_ref[...])
        dgate_ref[...] = dyy * hs * _gelu_grad(gate, t)
        dh_ref[...] = dyy * gl
        _accumulate_tn(acc_ref, dw_ref, gl * hs, dy_, n)

    return pl.pallas_call(
        body, name="odd_post_bwd", grid=(n,),
        in_specs=[_row(ts, D), _row(ts, D), _row(ts, D), _const(wout.shape)],
        out_specs=[_row(ts, D), _row(ts, D), _acc((D, D))],
        out_shape=[_sds((S, D), F32), _sds((S, D), F32), _sds((D, D), BF16)],
        scratch_shapes=[pltpu.VMEM((D, D), F32)], compiler_params=_cp())(dy, z, hseq, wout)


def odd_gates_bwd(xb, r, ig, lam_grad, hseq, keep, wr, wi, lam):
    S = xb.shape[0]
    ts = _tile_rows(S, 512)

    def body(xb_ref, r_ref, ig_ref, lg_ref, h_ref, hp_ref, keep_ref, wr_ref, wi_ref, lam_ref,
             dxb_ref, dcb_ref, dbr_ref, dbi_ref, dlam_ref, dwr_ref, dwi_ref):
        i = pl.program_id(0)

        @pl.when(i == 0)
        def _():
            for ref in (dcb_ref, dbr_ref, dbi_ref, dlam_ref, dwr_ref, dwi_ref):
                ref[...] = jnp.zeros_like(ref)

        first = (i > 0).astype(F32)
        xb, r, ig = xb_ref[...], r_ref[...], ig_ref[...]
        keep_ = keep_ref[...]
        lam_ = lam_ref[...]
        sp, a, mult = _decay(r, lam_)
        hs = h_ref[...]
        hprev = _roll(jnp.concatenate([hp_ref[...] * first, hs], axis=0), 1, 0)[CONV_HALO:]
        lg = lg_ref[...]
        da = lg * hprev * keep_
        ixb = ig * xb
        dmult = lg * ixb * keep_
        dixb = lg * jnp.where(keep_ > 0.0, mult, 1.0)
        dlog_a = da * a - dmult * jnp.where(mult > 0.0, a * a / mult, 0.0)
        dr = dlog_a * (-LRU_C * sp)
        dlam_ref[...] += _rowsum(dlog_a * (-LRU_C * r)) * (-jax.nn.sigmoid(-lam_))
        dpr = dr * r * (1.0 - r)
        dpi = dixb * xb * ig * (1.0 - ig)
        dbr_ref[...] += _rowsum(dpr)
        dbi_ref[...] += _rowsum(dpi)
        dxb = dixb * ig
        parts = []
        for h in range(LRU_HEADS):
            sl = slice(h * LRU_HEAD_DIM, (h + 1) * LRU_HEAD_DIM)
            dwr_ref[h] += _dot_tn(xb[:, sl], dpr[:, sl])
            dwi_ref[h] += _dot_tn(xb[:, sl], dpi[:, sl])
            parts.append(_dot_nt(dpr[:, sl], wr_ref[h]) + _dot_nt(dpi[:, sl], wi_ref[h]))
        dxb = dxb + jnp.concatenate(parts, axis=1)
        dxb_ref[...] = dxb
        dcb_ref[...] += _rowsum(dxb)

    ins = [xb, r, ig, lam_grad, hseq, hseq, keep, wr, wi, lam]
    in_specs = [_row(ts, D)] * 5 + [_prev(CONV_HALO, D, ts), _row(ts, 1)] + [_const(v.shape) for v in ins[7:]]
    gshape = (LRU_HEADS, LRU_HEAD_DIM, LRU_HEAD_DIM)
    return pl.pallas_call(
        body, name="odd_gates_bwd", grid=(S // ts,), in_specs=in_specs,
        out_specs=[_row(ts, D)] + [_acc((1, D))] * 4 + [_acc(gshape)] * 2,
        out_shape=[_sds((S, D), F32)] + [_sds((1, D), F32)] * 4 + [_sds(gshape, F32)] * 2,
        compiler_params=_cp())(*ins)


def odd_pre_bwd(x, dy, z, dxb, dgate, g, cw, win):
    S = x.shape[0]
    ts = _tile_rows(S, 512)
    n = S // ts

    def body(x_ref, dy_ref, xbp_ref, xbpp_ref, dxb_ref, dxbn_ref, dgate_ref, g_ref, cw_ref, win_ref,
             dx_ref, dcw_ref, dg_ref, dwin_ref, acc_ref):
        i = pl.program_id(0)

        @pl.when(i == 0)
        def _():
            dcw_ref[...] = jnp.zeros_like(dcw_ref)
            dg_ref[...] = jnp.zeros_like(dg_ref)

        dxb = dxb_ref[...]
        extd = jnp.concatenate([dxb, dxbn_ref[...] * (i < n - 1).astype(F32)], axis=0)
        extx = jnp.concatenate([xbpp_ref[...] * (i > 0).astype(F32), xbp_ref[...]], axis=0)
        dxbp = cw_ref[3:4, :] * dxb
        dcw_ref[3:4, :] += _rowsum(dxb * xbp_ref[...])
        for k in range(3):
            dxbp = dxbp + cw_ref[k:k + 1, :] * _roll(extd, -(3 - k), 0)[:ts]
            dcw_ref[k:k + 1, :] += _rowsum(dxb * _roll(extx, 3 - k, 0)[CONV_HALO:])
        dz = jnp.concatenate([dgate_ref[...], dxbp], axis=1).astype(BF16)
        x_ = x_ref[...]
        h, rstd = _rms(x_, g_ref[...])
        dxn, dgr = _rms_bwd(x_, g_ref[...], rstd, _dot_nt(dz, win_ref[...]))
        dx_ref[...] = dy_ref[...] + dxn
        dg_ref[...] += _rowsum(dgr)
        _accumulate_tn(acc_ref, dwin_ref, h, dz, n)

    ins = [x, dy, z, z, dxb, dxb, dgate, g, cw, win]
    in_specs = [_row(ts, D), _row(ts, D), _row(ts, D, 1), _prev(CONV_HALO, D, ts, 1), _row(ts, D),
                _next(CONV_HALO, D, ts, n), _row(ts, D)] + [_const(v.shape) for v in ins[7:]]
    return pl.pallas_call(
        body, name="odd_pre_bwd", grid=(n,), in_specs=in_specs,
        out_specs=[_row(ts, D), _acc((4, D)), _acc((1, D)), _acc((D, 2 * D))],
        out_shape=[_sds((S, D), F32), _sds((4, D), F32), _sds((1, D), F32), _sds((D, 2 * D), BF16)],
        scratch_shapes=[pltpu.VMEM((D, 2 * D), F32)], compiler_params=_cp())(*ins)


def even_post_bwd(dy, ypool, o, wo_pool, wo_att):
    S = dy.shape[0]
    ts = _tile_rows(S, 512)
    n = S // ts

    def body(dy_ref, yp_ref, o_ref, wp_ref, wa_ref, dyp_ref, do_ref, delta_ref, dwp_ref, dwa_ref, accp_ref,
             acca_ref):
        dy_, o_ = dy_ref[...], o_ref[...]
        dyp_ref[...] = _dot_nt(dy_, wp_ref[...])
        do = _dot_nt(dy_, wa_ref[...])
        do_ref[...] = do.astype(BF16)
        prod = do * o_
        for h in range(MLA_HEADS):
            delta_ref[h] = _as_row(jnp.sum(prod[:, h * LANES:(h + 1) * LANES], axis=1, keepdims=True))
        _accumulate_tn(accp_ref, dwp_ref, yp_ref[...], dy_, n)
        _accumulate_tn(acca_ref, dwa_ref, o_, dy_, n)

    return pl.pallas_call(
        body, name="even_post_bwd", grid=(n,),
        in_specs=[_row(ts, D), _row(ts, POOL_DIM), _row(ts, D), _const(wo_pool.shape), _const(wo_att.shape)],
        out_specs=[_row(ts, POOL_DIM), _row(ts, D),
                   pl.BlockSpec((MLA_HEADS, None, 1, ts), lambda i: (0, i, 0, 0)), _acc((POOL_DIM, D)),
                   _acc((D, D))],
        out_shape=[_sds((S, POOL_DIM), F32), _sds((S, D), BF16), _sds((MLA_HEADS, n, 1, ts), F32),
                   _sds((POOL_DIM, D), BF16), _sds((D, D), BF16)],
        scratch_shapes=[pltpu.VMEM((POOL_DIM, D), F32), pltpu.VMEM((D, D), F32)],
        compiler_params=_cp())(dy, ypool, o, wo_pool, wo_att)


def attn_bwd(qp, kp, vp, do, lse_row, delta_row, token=None):
    S = qp.shape[0]
    tk = _tile_rows(S, 512)
    nq = S // tk
    extra, extra_specs = _after(token)

    def body(q_ref, k_ref, v_ref, do_ref, lse_ref, delta_ref, *rest):
        dq_ref, dk_ref, dv_ref = rest[-3:]
        kj = pl.program_id(1)

        @pl.when(kj == 0)
        def _():
            dq_ref[...] = jnp.zeros_like(dq_ref)

        k, v = k_ref[...], v_ref[...]

        def block(qi, carry, masked):
            dk, dv = carry
            off = pl.multiple_of(qi * tk, tk)
            q = q_ref[pl.ds(off, tk), :]
            do_ = do_ref[pl.ds(off, tk), :]
            st = _dot_nt(k, q)
            if masked:
                row = lax.broadcasted_iota(jnp.int32, (tk, tk), 0)
                col = lax.broadcasted_iota(jnp.int32, (tk, tk), 1)
                st = jnp.where(col >= row, st, -1e30)
            pt = _exp2(st - lse_ref[qi])
            dv = dv + _dot(pt, do_)
            dst = (pt * (_dot_nt(v, do_) - delta_ref[qi])).astype(BF16)
            dk = dk + _dot(dst, q)
            dq_ref[pl.ds(off, tk), :] += _dot_tn(dst, k)
            return dk, dv

        zero = jnp.zeros((tk, LANES), F32)
        carry = block(kj, (zero, zero), True)
        dk, dv = _pair_loop(kj + 1, nq, lambda qi, c: block(qi, c, False), carry, unrolls=(4, 2, 1))
        dk_ref[...] = dk * LN_2
        dv_ref[...] = dv

    blk = pl.BlockSpec((tk, LANES), lambda h, j: (j, h))
    full = pl.BlockSpec((S, LANES), lambda h, j: (0, h))
    rowv = pl.BlockSpec((None, nq, 1, tk), lambda h, j: (h, 0, 0, 0))
    return pl.pallas_call(
        body, name="attn_bwd", grid=(MLA_HEADS, nq), in_specs=[full, blk, blk, full, rowv, rowv] + extra_specs,
        out_specs=[full, blk, blk], out_shape=[_sds((S, D), F32)] * 3, compiler_params=_cp2())(
            qp, kp, vp, do, lse_row, delta_row, *extra)


def even_pre_bwd(x, dy, z, dq, dk, dv, dyp, tabs, g, win, pw, pscale, qg, wq, kvg, wk, wv):
    S = x.shape[0]
    ts = _tile_rows(S, 512)
    n = S // ts

    def body(x_ref, dy_ref, z_ref, up_ref, dq_ref, dk_ref, dv_ref, dyp_ref, dypn_ref, c_ref, a_ref, b_ref,
             g_ref, win_ref, pw_ref, ps_ref, qg_ref, wq_ref, kvg_ref, wk_ref, wv_ref,
             dx_ref, dg_ref, dpw_ref, dps_ref, dqg_ref, dwq_ref, dkvg_ref, dwk_ref, dwv_ref, dwin_ref, acc_ref):
        i = pl.program_id(0)

        @pl.when(i == 0)
        def _():
            for ref in (dg_ref, dpw_ref, dps_ref, dqg_ref, dwq_ref, dkvg_ref, dwk_ref, dwv_ref):
                ref[...] = jnp.zeros_like(ref)

        z = z_ref[...]
        c, a, b = c_ref[...], a_ref[...], b_ref[...]
        ps = ps_ref[...]
        u = z[:, :POOL_DIM]
        pooled = _pooled(up_ref[...] * (i > 0).astype(F32), u, i * ts)
        dyp_ = dyp_ref[...]
        dps_ref[...] += _rowsum(dyp_ * _blockdot(pooled, pw_ref, 4, LANES))
        ext = jnp.concatenate([dyp_, dypn_ref[...] * (i < n - 1).astype(F32)], axis=0) * ps
        for gidx in range(4):
            sl = slice(gidx * LANES, (gidx + 1) * LANES)
            dpw_ref[gidx] += _dot_tn(pooled[:, sl], ext[:ts, sl])
        dpooled = jnp.concatenate(
            [_dot_nt(ext[:, gidx * LANES:(gidx + 1) * LANES], pw_ref[gidx]) for gidx in range(4)], axis=1)
        dm = dpooled / _pool_cnt(i * ts, ts + POOL_HALO)
        du = _pool_windows(dm, -1)[:ts] - dpooled[:ts]
        cq = z[:, 512:768]
        cqn, rstd_q = _rms(cq, qg_ref[...])
        dqf = _rope_bwd(dq_ref[...] * ATTN_SCALE, c, a, b)
        dwq_ref[...] += _dot_tn(cqn, dqf)
        dcq, dqg_rows = _rms_bwd(cq, qg_ref[...], rstd_q, _dot_nt(dqf, wq_ref[...]))
        dqg_ref[...] += _rowsum(dqg_rows)
        ckv = z[:, 768:896]
        ckvn, rstd_kv = _rms(ckv, kvg_ref[...])
        dk_, dv_ = dk_ref[...], dv_ref[...]
        dwk_ref[...] += _dot_tn(ckvn, dk_)
        dwv_ref[...] += _dot_tn(ckvn, dv_)
        dckv, dkvg_rows = _rms_bwd(ckv, kvg_ref[...], rstd_kv,
                                   _dot_nt(dk_, wk_ref[...]) + _dot_nt(dv_, wv_ref[...]))
        dkvg_ref[...] += _rowsum(dkvg_rows)
        dkr = dk_[:, :LANES]
        for h in range(1, MLA_HEADS):
            dkr = dkr + dk_[:, h * LANES:(h + 1) * LANES]
        lane = lax.broadcasted_iota(jnp.int32, (ts, LANES), 1)
        dkr = jnp.where((lane >= 64) & (lane < 96), _rope_bwd(dkr, c, a, b), 0.0)
        dz = jnp.concatenate([du, dcq, dckv, dkr], axis=1).astype(BF16)
        x_ = x_ref[...]
        h, rstd = _rms(x_, g_ref[...])
        dxn, dgr = _rms_bwd(x_, g_ref[...], rstd, _dot_nt(dz, win_ref[...]))
        dx_ref[...] = dy_ref[...] + dxn
        dg_ref[...] += _rowsum(dgr)
        _accumulate_tn(acc_ref, dwin_ref, h, dz, n)

    ins = [x, dy, z, z, dq, dk, dv, dyp, dyp, *tabs, g, win, pw, pscale, qg, wq, kvg, wk, wv]
    in_specs = [_row(ts, D), _row(ts, D), _row(ts, D), _prev(POOL_HALO, POOL_DIM, ts), _row(ts, D), _row(ts, D),
                _row(ts, D), _row(ts, POOL_DIM), _next(POOL_HALO, POOL_DIM, ts, n), _row(ts, LANES),
                _row(ts, LANES), _row(ts, LANES)] + [_const(v.shape) for v in ins[12:]]
    acc_shapes = [(1, D), (4, LANES, LANES), (1, POOL_DIM), (1, Q_LORA), (Q_LORA, D), (1, KV_LORA), (KV_LORA, D),
                  (KV_LORA, D)]
    return pl.pallas_call(
        body, name="even_pre_bwd", grid=(n,), in_specs=in_specs,
        out_specs=[_row(ts, D)] + [_acc(s) for s in acc_shapes] + [_acc((D, D))],
        out_shape=[_sds((S, D), F32)] + [_sds(s, F32) for s in acc_shapes] + [_sds((D, D), BF16)],
        scratch_shapes=[pltpu.VMEM((D, D), F32)], compiler_params=_cp())(*ins)


def _pick(n, options):
    for o in options:
        if n % o == 0:
            return o
    return n


def matmul_tn(name, a, b):
    out_dtype = BF16
    S = a.shape[-2]
    ts = _tile_rows(S, 2048)
    steps = S // ts

    def body(a_ref, b_ref, o_ref, acc_ref):
        s = pl.program_id(2)

        @pl.when(s == 0)
        def _():
            acc_ref[...] = jnp.zeros_like(acc_ref)

        acc_ref[...] += _dot_tn(a_ref[...], b_ref[...])

        @pl.when(s == steps - 1)
        def _():
            o_ref[...] = acc_ref[...].astype(o_ref.dtype)

    if a.ndim == 3:
        C, _, K = a.shape
        N = b.shape[1]
        tn = _pick(N, (1024, 512, 256, 128))
        grid = (C, N // tn, S // ts)
        in_specs = [pl.BlockSpec((None, ts, K), lambda c, j, s: (c, s, 0)),
                    pl.BlockSpec((ts, tn), lambda c, j, s: (s, j))]
        out_spec, out_shape, tile = pl.BlockSpec((None, K, tn), lambda c, j, s: (c, 0, j)), (C, K, N), (K, tn)
    elif b.ndim == 3:
        C, _, N = b.shape
        K = a.shape[1]
        tk = _pick(K, (1024, 512, 256, 128))
        grid = (C, K // tk, S // ts)
        in_specs = [pl.BlockSpec((ts, tk), lambda c, i, s: (s, i)),
                    pl.BlockSpec((None, ts, N), lambda c, i, s: (c, s, 0))]
        out_spec, out_shape, tile = pl.BlockSpec((None, tk, N), lambda c, i, s: (c, i, 0)), (C, K, N), (tk, N)
    else:
        K, N = a.shape[1], b.shape[1]
        tk = _pick(K, (1024, 512, 256, 128))
        tn = _pick(N, (1024, 512, 256, 128))
        grid = (K // tk, N // tn, S // ts)
        in_specs = [pl.BlockSpec((ts, tk), lambda i, j, s: (s, i)), pl.BlockSpec((ts, tn), lambda i, j, s: (s, j))]
        out_spec, out_shape, tile = pl.BlockSpec((tk, tn), lambda i, j, s: (i, j)), (K, N), (tk, tn)
    return pl.pallas_call(
        body, name=name, grid=grid, in_specs=in_specs, out_specs=out_spec, out_shape=_sds(out_shape, out_dtype),
        scratch_shapes=[pltpu.VMEM(tile, F32)], compiler_params=pltpu.CompilerParams(dimension_semantics=("arbitrary",) * 3, vmem_limit_bytes=VMEM_LIMIT))(
            a, b)


def _my_id():
    return lax.axis_index("x") * 4 + lax.axis_index("y") * 2 + lax.axis_index("c")


def _peer(j):
    x, y, c = lax.axis_index("x"), lax.axis_index("y"), lax.axis_index("c")
    px = 1 - x if j & 4 else x
    py = 1 - y if j & 2 else y
    pc = 1 - c if j & 1 else c
    return (px, py, pc), px * 4 + py * 2 + pc


def all_gather(name, arrays):
    n = len(arrays)

    def body(*refs):
        ins, outs = refs[:n], refs[n:2 * n]
        send_sems, recv_sems, local_sems = refs[2 * n:]
        me = _my_id()
        local = [pltpu.make_async_copy(ins[k], outs[k].at[me], local_sems.at[k]) for k in range(n)]
        for cp in local:
            cp.start()
        sends = []
        for j in range(1, N_DEV):
            peer, _ = _peer(j)
            for k in range(n):
                cp = pltpu.make_async_remote_copy(
                    src_ref=ins[k], dst_ref=outs[k].at[me], send_sem=send_sems.at[k, j - 1],
                    recv_sem=recv_sems.at[k, j - 1], device_id=peer, device_id_type=pl.DeviceIdType.MESH)
                cp.start()
                sends.append(cp)
        for j in range(1, N_DEV):
            peer, pid = _peer(j)
            for k in range(n):
                pltpu.make_async_remote_copy(
                    src_ref=ins[k], dst_ref=outs[k].at[pid], send_sem=send_sems.at[k, j - 1],
                    recv_sem=recv_sems.at[k, j - 1], device_id=peer, device_id_type=pl.DeviceIdType.MESH).wait_recv()
        for cp in sends:
            cp.wait_send()
        for cp in local:
            cp.wait()

    any_spec = pl.BlockSpec(memory_space=pl.ANY)
    return pl.pallas_call(
        body, name=name, in_specs=[any_spec] * n, out_specs=[any_spec] * n,
        out_shape=[_sds((N_DEV,) + a.shape, a.dtype) for a in arrays],
        scratch_shapes=[pltpu.SemaphoreType.DMA((n, N_DEV - 1)), pltpu.SemaphoreType.DMA((n, N_DEV - 1)),
                        pltpu.SemaphoreType.DMA((n,))],
        compiler_params=pltpu.CompilerParams(has_side_effects=True))(*arrays)


def exchange(name, arrays, gathers=()):
    n_ex, n = len(arrays), len(arrays) + len(gathers)

    def body(*refs):
        ins, outs = refs[:n], refs[n:2 * n]
        send_sems, recv_sems, local_sems = refs[2 * n:]
        me = _my_id()

        def mine(k, slot):
            return ins[k].at[slot] if k < n_ex else ins[k]

        local = [pltpu.make_async_copy(mine(k, me), outs[k].at[me], local_sems.at[k]) for k in range(n)]
        for cp in local:
            cp.start()
        sends = []
        for j in range(1, N_DEV):
            peer, pid = _peer(j)
            for k in range(n):
                cp = pltpu.make_async_remote_copy(
                    src_ref=mine(k, pid), dst_ref=outs[k].at[me], send_sem=send_sems.at[k, j - 1],
                    recv_sem=recv_sems.at[k, j - 1], device_id=peer, device_id_type=pl.DeviceIdType.MESH)
                cp.start()
                sends.append(cp)
        for j in range(1, N_DEV):
            peer, pid = _peer(j)
            for k in range(n):
                pltpu.make_async_remote_copy(
                    src_ref=mine(k, me), dst_ref=outs[k].at[pid], send_sem=send_sems.at[k, j - 1],
                    recv_sem=recv_sems.at[k, j - 1], device_id=peer, device_id_type=pl.DeviceIdType.MESH).wait_recv()
        for cp in sends:
            cp.wait_send()
        for cp in local:
            cp.wait()

    any_spec = pl.BlockSpec(memory_space=pl.ANY)
    return pl.pallas_call(
        body, name=name, in_specs=[any_spec] * n, out_specs=[any_spec] * n,
        out_shape=[_sds(a.shape, a.dtype) for a in arrays] + [_sds((N_DEV,) + a.shape, a.dtype) for a in gathers],
        scratch_shapes=[pltpu.SemaphoreType.DMA((n, N_DEV - 1)), pltpu.SemaphoreType.DMA((n, N_DEV - 1)),
                        pltpu.SemaphoreType.DMA((n,))],
        compiler_params=pltpu.CompilerParams(has_side_effects=True))(*arrays, *gathers)


_HBM = pl.BlockSpec(memory_space=pltpu.HBM)
_SEM = pl.BlockSpec(memory_space=pltpu.SEMAPHORE)
_DATAFLOW = pltpu.SideEffectType.DATAFLOW_SIDE_EFFECTING


def _in_hbm(v):
    return pltpu.with_memory_space_constraint(v, pltpu.HBM)


N_PEERS = N_DEV - 1


def _split_copy(k, j, srcs, lands, send_sems, recv_sems, gather, slot):
    peer, pid = _peer(j)
    return pltpu.make_async_remote_copy(
        src_ref=srcs[k] if _flag(gather, k) else srcs[k].at[pid],
        dst_ref=lands[k].at[_my_id() if slot == "mine" else pid],
        send_sem=send_sems[j - 1], recv_sem=recv_sems[j - 1], device_id=peer, device_id_type=pl.DeviceIdType.MESH)


def _flag(gather, k):
    return gather[k] if isinstance(gather, tuple) else gather


def split_start(name, arrays, gather):
    n = len(arrays)
    lands = [lax.empty((N_DEV,) + a.shape if _flag(gather, k) else a.shape, a.dtype) for k, a in enumerate(arrays)]

    def body(*refs):
        srcs, lnds = refs[:n], refs[n:2 * n]
        sems = refs[4 * n:4 * n + 2 * N_PEERS]
        token = refs[-1]
        for j in range(1, N_DEV):
            for k in range(n):
                _split_copy(k, j, srcs, lnds, sems[:N_PEERS], sems[N_PEERS:], gather, "mine").start()
        token[...] = jnp.zeros_like(token)

    out = pl.pallas_call(
        body, name=name,
        out_shape=(*[pltpu.HBM(a.shape, a.dtype) for a in arrays], *[pltpu.HBM(l.shape, l.dtype) for l in lands],
                   *[pltpu.SemaphoreType.DMA(())] * (2 * N_PEERS), _sds((8, LANES), F32)),
        in_specs=[_HBM] * (2 * n),
        out_specs=(*[_HBM] * (2 * n), *[_SEM] * (2 * N_PEERS), pl.BlockSpec(memory_space=pltpu.VMEM)),
        input_output_aliases={k: k for k in range(2 * n)},
        compiler_params=pltpu.CompilerParams(has_side_effects=_DATAFLOW))(
            *[_in_hbm(a) for a in arrays], *[_in_hbm(l) for l in lands])
    sems = list(out[2 * n:2 * n + 2 * N_PEERS])
    return sems[:N_PEERS], sems[N_PEERS:], list(out[:n]), list(out[n:2 * n]), out[-1]


def split_wait(name, handle, after, gather):
    send_sems, recv_sems, srcs, lands, _ = handle
    n = len(srcs)

    def body(*refs):
        srcs_r, lnds_r = refs[:n], refs[n:2 * n]
        sems = refs[2 * n:2 * n + 2 * N_PEERS]
        for j in range(1, N_DEV):
            for k in range(n):
                cp = _split_copy(k, j, srcs_r, lnds_r, sems[:N_PEERS], sems[N_PEERS:], gather, "peer")
                cp.wait_send()
                cp.wait_recv()

    out = pl.pallas_call(
        body, name=name, out_shape=tuple(pltpu.HBM(a.shape, a.dtype) for a in srcs + lands),
        in_specs=[_HBM] * (2 * n) + [_SEM] * (2 * N_PEERS) + [pl.BlockSpec(memory_space=pl.ANY)],
        out_specs=tuple([_HBM] * (2 * n)), input_output_aliases={k: k for k in range(2 * n)},
        compiler_params=pltpu.CompilerParams(has_side_effects=_DATAFLOW))(
            *srcs, *lands, *send_sems, *recv_sems, after)
    return list(out[:n]), list(out[n:])


def _fill_own_slot(src, land, gather):
    me = _my_id()
    own = src[None] if gather else lax.dynamic_index_in_dim(src, me, 0, keepdims=True)
    return lax.dynamic_update_slice_in_dim(land, own, me, 0)


ADAMW_BLOCK_ELEMS = 128 * 1024


def adamw(name, parts, w, m, v, token=None):
    R, C = w.shape
    tr = _pick(R, [t for t in (512, 256, 128, 64, 32, 16, 8) if t * C <= ADAMW_BLOCK_ELEMS])
    c1 = 1.0 - ADAM_B1 ** ADAM_STEP
    c2 = 1.0 - ADAM_B2 ** ADAM_STEP
    extra, extra_specs = _after(token)

    def body(p_ref, w_ref, m_ref, v_ref, *rest):
        g_ref, d_ref, nm_ref, nv_ref = rest[-4:]
        g = p_ref[0].astype(F32)
        for s in range(1, N_DEV):
            g = g + p_ref[s].astype(F32)
        g_ref[...] = g
        m_ = ADAM_B1 * m_ref[...] + (1.0 - ADAM_B1) * g
        v_ = ADAM_B2 * v_ref[...] + (1.0 - ADAM_B2) * (g * g)
        nm_ref[...] = m_
        nv_ref[...] = v_
        d_ref[...] = -ADAM_LR * ((m_ / c1) / (jnp.sqrt(v_ / c2) + ADAM_EPS) + ADAM_WD * w_ref[...])

    row = pl.BlockSpec((tr, C), lambda i: (i, 0))
    return pl.pallas_call(
        body, name=name, grid=(R // tr,),
        in_specs=[pl.BlockSpec((N_DEV, tr, C), lambda i: (0, i, 0)), row, row, row] + extra_specs,
        out_specs=[row] * 4, out_shape=[_sds((R, C), F32)] * 4, compiler_params=_cp())(parts, w, m, v, *extra)


WEIGHTS = ['ev_norm', 'ev_w_in', 'ev_pool_w', 'ev_pool_scale', 'ev_q_norm', 'ev_w_q_up', 'ev_kv_norm', 'ev_w_kv_up',
           'ev_w_out', 'od_norm', 'od_w_in', 'od_conv_w', 'od_conv_b', 'od_w_rgate', 'od_b_rgate', 'od_w_igate',
           'od_b_igate', 'od_lambda', 'od_w_out', 'xa_norm_x', 'xa_norm_mem', 'xa_w_q', 'xa_w_kv', 'xa_w_o',
           'ffn_norm', 'ffn_w_gate_up', 'ffn_w_down', 'final_norm']
SHARD_AXIS = {'ev_w_in': 1, 'ev_w_q_up': 2, 'ev_w_kv_up': 2, 'ev_w_out': 1, 'od_norm': 1, 'od_w_in': 2,
              'od_conv_w': 2, 'od_conv_b': 1, 'od_w_rgate': 2, 'od_b_rgate': 1, 'od_w_igate': 2, 'od_b_igate': 1,
              'od_lambda': 1, 'od_w_out': 1, 'xa_w_q': 1, 'xa_w_kv': 2, 'xa_w_o': 1, 'ffn_w_gate_up': 2,
              'ffn_w_down': 1}
SMALL_F32 = ('od_norm', 'od_conv_w', 'od_conv_b', 'od_b_rgate', 'od_b_igate', 'od_lambda')
STACKED = ('ffn_w_gate_up', 'ffn_w_down')
SHARDED = [n for n in WEIGHTS if n in SHARD_AXIS]
REPLICATED = [n for n in WEIGHTS if n not in SHARD_AXIS]
ROW_ALIGN = 512


def _pack(flats, dtype):
    v = jnp.concatenate([f.reshape(-1).astype(dtype) for f in flats])
    pad = (-v.shape[0]) % (ROW_ALIGN * LANES)
    return jnp.pad(v, (0, pad)).reshape(-1, LANES)


def _rows8(n_elems):
    return -(-n_elems // (8 * LANES)) * 8


def _pack_rows(arrays, lead=False):
    out = []
    for a in arrays:
        r = a.reshape((N_DEV, -1, LANES) if lead else (-1, LANES))
        pad = _rows8(r.shape[-2] * LANES) - r.shape[-2]
        out.append(jnp.pad(r, [(0, 0)] * (r.ndim - 2) + [(0, pad), (0, 0)]))
    return jnp.concatenate(out, axis=-2)


def _unpack_rows(buf, shapes, lead=False):
    out, off = [], 0
    for s in shapes:
        n = 1
        for d in s:
            n *= d
        rows = buf[..., off:off + n // LANES, :]
        out.append(rows.reshape(((N_DEV,) if lead else ()) + tuple(s)))
        off += _rows8(n)
    return out


def _unpack(flat, shapes):
    out, off = [], 0
    v = flat.reshape(-1)
    for s in shapes:
        n = 1
        for d in s:
            n *= d
        out.append(v[off:off + n].reshape(s))
        off += n
    return out


def _to_full(stacked, axis):
    v = jnp.moveaxis(stacked, 0, axis)
    s = v.shape
    return v.reshape(s[:axis] + (s[axis] * s[axis + 1],) + s[axis + 2:])


def _to_shards(full, axis):
    s = full.shape
    v = full.reshape(s[:axis] + (N_DEV, s[axis] // N_DEV) + s[axis + 1:])
    return jnp.moveaxis(v, axis, 0)


def _pad_heads(w, nh, dh, lead):
    s = w.shape
    v = w.reshape(s[:-1] + (nh, dh))
    v = jnp.pad(v, [(0, 0)] * (len(s) - 1) + [(0, 0), (lead, LANES - dh - lead)])
    return v.reshape(s[:-1] + (nh * LANES,))


def _unpad_heads(w, nh, dh, lead):
    s = w.shape
    return w.reshape(s[:-1] + (nh, LANES))[..., lead:lead + dh].reshape(s[:-1] + (nh * dh,))


def _rope_tables(positions):
    inv_freq = 10000.0 ** (-jnp.arange(0, 32, 2, dtype=F32) / 32)
    ang = positions.astype(F32)[:, None] * inv_freq
    cos, sin = jnp.tile(jnp.cos(ang), (1, LANES // 16)), jnp.tile(jnp.sin(ang), (1, LANES // 16))
    lane = lax.broadcasted_iota(jnp.int32, cos.shape, 1)
    c = jnp.where((lane >= 64) & (lane < 96), cos, 1.0)
    a = jnp.where((lane >= 80) & (lane < 96), sin, 0.0)
    b = jnp.where((lane >= 64) & (lane < 80), -sin, 0.0)
    return c, a, b


def _t(w):
    return jnp.swapaxes(w, -1, -2)


def device_step(x, mem, positions, target, W, fwd_token=None, late_weights=None, ship_grads=None):
    S = x.shape[0]
    G = {}
    tabs = _rope_tables(positions)
    keep = (positions != 0).astype(F32)[:, None]
    row = lambda v: v.reshape(1, -1)

    w_in = W['ev_w_in'][0]
    ev_win = jnp.concatenate([w_in[:, :896], _pad_heads(w_in[:, 896:], 1, 32, 64)], axis=1)
    ev_wq = _pad_heads(W['ev_w_q_up'][0], MLA_HEADS, QK_DIM, 0)
    kvw = W['ev_w_kv_up'][0].reshape(KV_LORA, MLA_HEADS, 128)
    ev_wk = _pad_heads(kvw[:, :, :64].reshape(KV_LORA, 512), MLA_HEADS, 64, 0)
    ev_wv = _pad_heads(kvw[:, :, 64:].reshape(KV_LORA, 512), MLA_HEADS, 64, 0)
    ev_wo_pool = W['ev_w_out'][0][:POOL_DIM]
    ev_wo_att = _t(_pad_heads(_t(W['ev_w_out'][0][POOL_DIM:]), MLA_HEADS, 64, 0))
    pw = W['ev_pool_w'][0].astype(BF16)
    ev_g, ps, qg, kvg = row(W['ev_norm'][0]), row(W['ev_pool_scale'][0]), row(W['ev_q_norm'][0]), row(W['ev_kv_norm'][0])

    z0, qp, kp, vp, ypool = even_pre(x, tabs, ev_g, ev_win, pw, ps, qg, ev_wq, kvg, ev_wk, ev_wv)
    o_att, lse = attn_fwd(qp, kp, vp, fwd_token)
    if late_weights is not None:
        W = {**W, **late_weights(lse)}
    x1 = even_post(x, ypool, o_att, ev_wo_pool, ev_wo_att)

    def xa_ffn_fwd(xin, l, head=()):
        mn, km, vm = mem_kv(mem, row(W['xa_norm_mem'][l]), W['xa_w_kv'][l])
        xm = xattn_fwd(xin, row(W['xa_norm_x'][l]), W['xa_w_q'][l], km, vm, W['xa_w_o'][l])
        *xo, hf, gu = (ffn_fwd_loss if head else ffn_fwd)(
            xm, row(W['ffn_norm'][l]), W['ffn_w_gate_up'], l, W['ffn_w_down'][:, l].reshape(FF_HALF, FF_CHUNK, D),
            *head)
        return xm, (xo if head else xo[0]), (mn, km, vm, hf, gu)

    x2, x3, memkv0 = xa_ffn_fwd(x1, 0)

    od_g, lam = row(W['od_norm'][0]), row(W['od_lambda'][0])
    cw, cb = W['od_conv_w'][0], row(W['od_conv_b'][0])
    wr, wi = W['od_w_rgate'][0], W['od_w_igate'][0]
    br, bi = row(W['od_b_rgate'][0]), row(W['od_b_igate'][0])
    z1, a_t, b_t, xb1, r1, ig1 = odd_pre(x3, keep, od_g, W['od_w_in'][0], cw, cb, wr, br, wi, bi, lam)
    hseq = lru_scan(a_t, b_t)
    x4 = odd_post(x3, z1, hseq, W['od_w_out'][0])
    x5, (dx, g_final, loss), memkv1 = xa_ffn_fwd(x4, 1, (target, row(W['final_norm'])))
    G['final_norm'] = g_final.reshape(D)

    gnx, gnm, gwq, gwkv, gwo, gfn, gwgu, gwd = ([None, None] for _ in range(8))

    def xa_ffn_bwd(dy, xin, xm, memkv, l):
        mn, km, vm, hf, gu = memkv
        fg = row(W['ffn_norm'][l])
        dxm, dfg, act, dgu = ffn_bwd(xm, dy, gu, fg, W['ffn_w_gate_up'], l,
                                     W['ffn_w_down'][:, l].reshape(FF_HALF, FF_CHUNK, D))
        gwd[l] = matmul_tn("ffn_dwd", act, dy).reshape(N_DEV, D_FF // N_DEV, D)
        gwgu[l] = matmul_tn("ffn_dwgu", hf, dgu)
        gfn[l] = dfg[0]
        dxin, o, dq, hx, dgx, dk, dv = xattn_bwd(xin, dxm, row(W['xa_norm_x'][l]), W['xa_w_q'][l], km, vm,
                                                  W['xa_w_o'][l])
        gnx[l] = dgx[0]
        gwo[l] = matmul_tn("xa_dwo", o, dxm)
        gwq[l] = matmul_tn("xa_dwq", hx, dq)
        dkv, dgm = mem_bwd(mem, row(W['xa_norm_mem'][l]), dk, dv, W['xa_w_kv'][l])
        gnm[l] = dgm[0]
        gwkv[l] = matmul_tn("xa_dwkv", mn, dkv)
        return dxin

    dx4 = xa_ffn_bwd(dx, x4, x5, memkv1, 1)

    dgate, dhs, g_od_wout = odd_post_bwd(dx4, z1, hseq, W['od_w_out'][0])
    G['od_w_out'] = g_od_wout[None]
    lam_grad = lru_scan(a_t, dhs, reverse=True)
    dxb, dcb, dbr, dbi, dlam, dwr, dwi = odd_gates_bwd(xb1, r1, ig1, lam_grad, hseq, keep, wr, wi, lam)
    dx3, dcw, dg_od, g_od_win = odd_pre_bwd(x3, dx4, z1, dxb, dgate, od_g, cw, W['od_w_in'][0])
    G['od_w_in'] = g_od_win[None]
    G['od_norm'], G['od_conv_w'], G['od_conv_b'] = dg_od, dcw[None], dcb
    G['od_w_rgate'], G['od_b_rgate'], G['od_w_igate'], G['od_b_igate'], G['od_lambda'] = (
        dwr[None], dbr, dwi[None], dbi, dlam)

    dx1 = xa_ffn_bwd(dx3, x1, x2, memkv0, 0)
    G['xa_norm_x'], G['xa_norm_mem'], G['ffn_norm'] = jnp.stack(gnx), jnp.stack(gnm), jnp.stack(gfn)
    G['xa_w_q'], G['xa_w_kv'], G['xa_w_o'] = jnp.stack(gwq), jnp.stack(gwkv), jnp.stack(gwo)
    G['ffn_w_gate_up'], G['ffn_w_down'] = jnp.stack(gwgu, axis=1), jnp.stack(gwd, axis=1)
    bwd_token = ship_grads(G) if ship_grads is not None else None

    dyp, do_att, delta, g_wo_pool, g_wo_att = even_post_bwd(dx1, ypool, o_att, ev_wo_pool, ev_wo_att)
    G['ev_w_out'] = jnp.concatenate([g_wo_pool, _t(_unpad_heads(_t(g_wo_att), MLA_HEADS, 64, 0))], axis=0)[None]
    dq, dk, dv = attn_bwd(qp, kp, vp, do_att, lse, delta, bwd_token)
    (grad_x, dg_ev, dpw, dps, dqg, dwq, dkvg, dwk, dwv, g_win) = even_pre_bwd(
        x, dx1, z0, dq, dk, dv, dyp, tabs, ev_g, ev_win, pw, ps, qg, ev_wq, kvg, ev_wk, ev_wv)
    G['ev_w_in'] = jnp.concatenate([g_win[:, :896], _unpad_heads(g_win[:, 896:], 1, 32, 64)], axis=1)[None]
    G['ev_norm'], G['ev_pool_w'], G['ev_pool_scale'], G['ev_q_norm'], G['ev_kv_norm'] = (
        dg_ev, dpw[None], dps, dqg, dkvg)
    G['ev_w_q_up'] = _unpad_heads(dwq, MLA_HEADS, QK_DIM, 0)[None]
    gk = _unpad_heads(dwk, MLA_HEADS, 64, 0).reshape(KV_LORA, MLA_HEADS, 64)
    gv = _unpad_heads(dwv, MLA_HEADS, 64, 0).reshape(KV_LORA, MLA_HEADS, 64)
    G['ev_w_kv_up'] = jnp.concatenate([gk, gv], axis=2).reshape(1, KV_LORA, MLA_HEADS * 128)
    return loss[0, 0], grad_x, G


def kernel(x, mem, positions, ev_norm, ev_w_in, ev_pool_w, ev_pool_scale, ev_q_norm, ev_w_q_up, ev_kv_norm, ev_w_kv_up, ev_w_out, od_norm, od_w_in, od_conv_w, od_conv_b, od_w_rgate, od_b_rgate, od_w_igate, od_b_igate, od_lambda, od_w_out, xa_norm_x, xa_norm_mem, xa_w_q, xa_w_kv, xa_w_o, ffn_norm, ffn_w_gate_up, ffn_w_down, final_norm, loss_target, m_ev_norm, m_ev_w_in, m_ev_pool_w, m_ev_pool_scale, m_ev_q_norm, m_ev_w_q_up, m_ev_kv_norm, m_ev_w_kv_up, m_ev_w_out, m_od_norm, m_od_w_in, m_od_conv_w, m_od_conv_b, m_od_w_rgate, m_od_b_rgate, m_od_w_igate, m_od_b_igate, m_od_lambda, m_od_w_out, m_xa_norm_x, m_xa_norm_mem, m_xa_w_q, m_xa_w_kv, m_xa_w_o, m_ffn_norm, m_ffn_w_gate_up, m_ffn_w_down, m_final_norm, v_ev_norm, v_ev_w_in, v_ev_pool_w, v_ev_pool_scale, v_ev_q_norm, v_ev_w_q_up, v_ev_kv_norm, v_ev_w_kv_up, v_ev_w_out, v_od_norm, v_od_w_in, v_od_conv_w, v_od_conv_b, v_od_w_rgate, v_od_b_rgate, v_od_w_igate, v_od_b_igate, v_od_lambda, v_od_w_out, v_xa_norm_x, v_xa_norm_mem, v_xa_w_q, v_xa_w_kv, v_xa_w_o, v_ffn_norm, v_ffn_w_gate_up, v_ffn_w_down, v_final_norm):
    args = dict(locals())
    w = {n: args[n] for n in WEIGHTS}
    m = {n: args['m_' + n] for n in WEIGHTS}
    v = {n: args['v_' + n] for n in WEIGHTS}
    big = [n for n in SHARDED if n not in SMALL_F32]
    small = [n for n in SHARDED if n in SMALL_F32]

    small_shapes = [w[n].shape for n in small]
    first = [n for n in big if n.startswith('ev_')]
    late = [n for n in big if n not in first]

    def full(n, st):
        return st if n in STACKED else _to_full(st, SHARD_AXIS[n])

    W = {n: w[n] for n in REPLICATED}
    W.update((n, full(n, st)) for n, st in zip(first, all_gather("gather_ev_weights", [w[n].astype(BF16) for n in first])))
    gather = split_start("gather_start", [w[n].astype(BF16) for n in late] + [_pack_rows([w[n] for n in small])], True)

    def late_weights(after):
        srcs, lands = split_wait("gather_wait", gather, after, True)
        lands = [_fill_own_slot(s, l, True) for s, l in zip(srcs, lands)]
        out = {n: full(n, st) for n, st in zip(late, lands)}
        out.update((n, _to_full(st, SHARD_AXIS[n])) for n, st in zip(small, _unpack_rows(lands[-1], small_shapes, True)))
        return out

    def shards(G, n):
        return G[n] if n in STACKED else _to_shards(G[n], SHARD_AXIS[n])

    shipped = []

    def ship_grads(G):
        shipped.append(split_start("exchange_start", [shards(G, n).astype(BF16) for n in late] +
                                   [_pack_rows([shards(G, n) for n in small], lead=True)], False))
        return shipped[0][-1]

    loss, grad_x, G = device_step(x[0], mem[0], positions[0], loss_target[0], W, gather[-1], late_weights, ship_grads)
    outs = [{}, {}, {}, {}]

    last_flags = (False,) * len(first) + (True,)
    last = split_start("last_start", [shards(G, n).astype(BF16) for n in first] + [_pack(
        [G[n] for n in REPLICATED] + [jnp.broadcast_to(loss, (LANES,))], F32)], last_flags)

    two_d = lambda a: a.reshape(-1, a.shape[-1])

    def update(names, parts):
        prev = None
        for n, p in zip(names, parts):
            res = adamw("adamw_" + n, p.reshape((N_DEV,) + two_d(w[n]).shape), two_d(w[n]), two_d(m[n]),
                        two_d(v[n]), prev)
            prev = res[0]
            for k in range(4):
                outs[k][n] = res[k].reshape(w[n].shape)
        return prev

    srcs, lands = split_wait("exchange_wait", shipped[0], last[-1], False)
    late_parts = [_fill_own_slot(s, l, False) for s, l in zip(srcs, lands)]
    after = update(late, late_parts)
    res = adamw("adamw_small", late_parts[-1], *[_pack_rows([d[n] for n in small]) for d in (w, m, v)])
    for k in range(4):
        outs[k].update(zip(small, _unpack_rows(res[k], small_shapes)))

    srcs, lands = split_wait("last_wait", last, after, last_flags)
    *first_parts, rep_parts = [_fill_own_slot(s, l, f) for s, l, f in zip(srcs, lands, last_flags)]
    update(first, first_parts)

    rep_shapes = [w[n].shape for n in REPLICATED] + [(LANES,)]
    zero = jnp.zeros((LANES,), F32)
    rep = adamw("adamw_rep", rep_parts, *[_pack([d[n] for n in REPLICATED] + [zero], F32) for d in (w, m, v)])
    for k in range(4):
        outs[k].update(zip(REPLICATED + ['loss'], _unpack(rep[k], rep_shapes)))
    loss = outs[0]['loss'][0]

    return (loss, grad_x[None], *[outs[0][n] for n in WEIGHTS], *[outs[1][n] for n in WEIGHTS],
            *[outs[2][n] for n in WEIGHTS], *[outs[3][n] for n in WEIGHTS])
```

```python
import functools

import jax
import jax.numpy as jnp
from jax import lax
from jax.experimental import pallas as pl
from jax.experimental.pallas import tpu as pltpu

F32, BF16 = jnp.float32, jnp.bfloat16
N_DEV = 8
D = 1024
POOL_DIM = 512
POOL_WINDOWS = (2, 4, 8, 16)
MLA_HEADS = 8
QK_DIM = 96
Q_LORA, KV_LORA = 256, 128
LRU_HEADS, LRU_HEAD_DIM = 4, 256
LRU_C = 8.0
MEM_HEADS, MEM_HEAD_DIM = 4, 256
D_FF = 2816
RMS_EPS = 1e-6
ADAM_LR, ADAM_B1, ADAM_B2, ADAM_EPS, ADAM_WD, ADAM_STEP = 0.001, 0.9, 0.999, 1e-08, 0.01, 10
LANES = 128
POOL_HALO = 16
CONV_HALO = 8
VMEM_LIMIT = 60000 * 1024


def _cp():
    return pltpu.CompilerParams(dimension_semantics=("arbitrary",), vmem_limit_bytes=VMEM_LIMIT)


def _cp2():
    return pltpu.CompilerParams(dimension_semantics=("arbitrary", "arbitrary"), vmem_limit_bytes=VMEM_LIMIT)


def _row(ts, c, col=0):
    return pl.BlockSpec((ts, c), lambda i: (i, col))


def _prev(hr, c, ts, col=0):
    r = ts // hr
    return pl.BlockSpec((hr, c), lambda i: (jnp.maximum(i * r - 1, 0), col))


def _next(hr, c, ts, n, col=0):
    r = ts // hr
    return pl.BlockSpec((hr, c), lambda i: (jnp.minimum((i + 1) * r, n * r - 1), col))


def _const(shape):
    nd = len(shape)
    return pl.BlockSpec(tuple(shape), lambda i: (0,) * nd, pipeline_mode=pl.Buffered(1))


def _acc(shape):
    nd = len(shape)
    return pl.BlockSpec(tuple(shape), lambda i: (0,) * nd)


def _sds(shape, dt):
    return jax.ShapeDtypeStruct(tuple(shape), dt)


def _dot(a, b):
    return jnp.dot(a.astype(BF16), b.astype(BF16), preferred_element_type=F32)


def _dot_nt(a, b):
    return lax.dot_general(a.astype(BF16), b.astype(BF16), (((1,), (1,)), ((), ())), preferred_element_type=F32)


def _dot_tn(a, b):
    return lax.dot_general(a.astype(BF16), b.astype(BF16), (((0,), (0,)), ((), ())), preferred_element_type=F32)


def _rms(x, g):
    rstd = lax.rsqrt(jnp.mean(x * x, axis=-1, keepdims=True) + RMS_EPS)
    return x * rstd * g, rstd


def _rms_bwd(x, g, rstd, dy):
    xn = x * rstd
    dyg = dy * g
    dx = rstd * (dyg - xn * jnp.mean(dyg * xn, axis=-1, keepdims=True))
    return dx, dy * xn


def _rowsum(v):
    return jnp.sum(v, axis=0, keepdims=True)


def _roll(v, s, axis):
    n = v.shape[axis]
    return pltpu.roll(v, s % n, axis)


def _rope(t, c, a, b):
    k = t.shape[1] // LANES
    if k > 1:
        c, a, b = (jnp.tile(v, (1, k)) for v in (c, a, b))
    return t * c + _roll(t, 16, 1) * a + _roll(t, -16, 1) * b


def _rope_bwd(d, c, a, b):
    k = d.shape[1] // LANES
    if k > 1:
        c, a, b = (jnp.tile(v, (1, k)) for v in (c, a, b))
    return d * c + _roll(d * a, -16, 1) + _roll(d * b, 16, 1)


def _gelu(x):
    c = 0.7978845608028654
    t = jnp.tanh(c * (x + 0.044715 * x * x * x))
    return 0.5 * x * (1.0 + t), t


def _gelu_grad(x, t):
    c = 0.7978845608028654
    return 0.5 * (1.0 + t) + 0.5 * x * (1.0 - t * t) * c * (1.0 + 3.0 * 0.044715 * x * x)


def _blockdot(v, w_ref, nblk, width):
    return jnp.concatenate(
        [_dot(v[:, j * width:(j + 1) * width], w_ref[j]) for j in range(nblk)], axis=1)


def _pool_cnt(row0, rows):
    t = row0 + lax.broadcasted_iota(jnp.int32, (rows, POOL_DIM), 0)
    w = jnp.left_shift(2, lax.broadcasted_iota(jnp.int32, (rows, POOL_DIM), 1) // LANES)
    return jnp.minimum(t + 1, w).astype(F32)


def _pool_windows(ext, sign):
    s2 = ext + _roll(ext, sign * 1, 0)
    t = s2[:, LANES:]
    s4 = t + _roll(t, sign * 2, 0)
    t = s4[:, LANES:]
    s8 = t + _roll(t, sign * 4, 0)
    t = s8[:, LANES:]
    s16 = t + _roll(t, sign * 8, 0)
    return jnp.concatenate([s2[:, :LANES], s4[:, :LANES], s8[:, :LANES], s16], axis=1)


def _pooled(uprev, u, row0):
    ts = u.shape[0]
    ext = jnp.concatenate([uprev, u], axis=0)
    sums = _pool_windows(ext, 1)[POOL_HALO:]
    return sums / _pool_cnt(row0, ts) - u


def _expm1(x):
    return jnp.where(jnp.abs(x) < 0.01, x * (1.0 + 0.5 * x * (1.0 + x * (1.0 / 3.0))), jnp.exp(x) - 1.0)


def _softplus(z):
    return jnp.maximum(z, 0.0) + jnp.log1p(jnp.exp(-jnp.abs(z)))


def _tile_rows(s, want):
    while s % want:
        want //= 2
    return want


def even_pre(x, tabs, g, win, pw, pscale, qg, wq, kvg, wk, wv):
    S = x.shape[0]
    ts = _tile_rows(S, 512)

    def body(x_ref, xp_ref, c_ref, a_ref, b_ref, g_ref, win_ref, pw_ref, ps_ref, qg_ref, wq_ref, kvg_ref,
             wk_ref, wv_ref, z_ref, q_ref, k_ref, v_ref, yp_ref):
        i = pl.program_id(0)
        h, _ = _rms(x_ref[...], g_ref[...])
        z = _dot(h, win_ref[...])
        z_ref[...] = z
        hp, _ = _rms(xp_ref[...], g_ref[...])
        uprev = _dot(hp, win_ref[:, :POOL_DIM]) * (i > 0).astype(F32)
        u = z[:, :POOL_DIM]
        pooled = _pooled(uprev, u, i * ts)
        yp_ref[...] = (_blockdot(pooled, pw_ref, 4, LANES) * ps_ref[...]).astype(BF16)
        c, a, b = c_ref[...], a_ref[...], b_ref[...]
        cqn, _ = _rms(z[:, 512:768], qg_ref[...])
        q_ref[...] = (_rope(_dot(cqn, wq_ref[...]), c, a, b) * (ATTN_SCALE * LOG2_E)).astype(BF16)
        ckvn, _ = _rms(z[:, 768:896], kvg_ref[...])
        krr = _rope(z[:, 896:1024], c, a, b)
        k_ref[...] = (_dot(ckvn, wk_ref[...]) + jnp.tile(krr, (1, MLA_HEADS))).astype(BF16)
        lane = lax.broadcasted_iota(jnp.int32, (ts, D), 1) % LANES
        v_ref[...] = jnp.where(lane == ONES_LANE, 1.0, _dot(ckvn, wv_ref[...])).astype(BF16)

    ins = [x, x, *tabs, g, win, pw, pscale, qg, wq, kvg, wk, wv]
    in_specs = [_row(ts, D), _prev(POOL_HALO, D, ts), _row(ts, LANES), _row(ts, LANES), _row(ts, LANES)]
    in_specs += [_const(v.shape) for v in ins[5:]]
    return pl.pallas_call(
        body, name="even_pre", grid=(S // ts,), in_specs=in_specs,
        out_specs=[_row(ts, D)] * 4 + [_row(ts, POOL_DIM)],
        out_shape=[_sds((S, D), F32)] + [_sds((S, D), BF16)] * 3 + [_sds((S, POOL_DIM), BF16)],
        compiler_params=_cp())(*ins)


ATTN_SCALE = QK_DIM ** -0.5
LOG2_E = 1.4426950408889634
LN_2 = 0.6931471805599453
ONES_LANE = 64


def _exp2(x):
    return jnp.exp2(x)


def _pair_loop(lo, hi, step, init, unrolls=(2, 1)):
    carry = init
    for unroll in unrolls:
        groups = (hi - lo) // unroll

        def group(j, c, lo=lo, unroll=unroll):
            for u in range(unroll):
                c = step(lo + unroll * j + u, c)
            return c

        carry = lax.fori_loop(0, groups, group, carry)
        lo = lo + unroll * groups
    return carry


def _as_row(col):
    return jnp.transpose(jnp.broadcast_to(col, (col.shape[0], LANES)))[0:1, :]


def _after(token):
    return ([], []) if token is None else ([token], [pl.BlockSpec(memory_space=pl.ANY)])


def attn_fwd(qp, kp, vp, token=None):
    S = qp.shape[0]
    tq = _tile_rows(S, 512)
    extra, extra_specs = _after(token)

    def body(q_ref, k_ref, v_ref, *rest):
        o_ref, lse_ref = rest[-2:]
        qi = pl.program_id(1)
        q = q_ref[...]

        def block(ki, carry, masked):
            m, acc = carry
            off = pl.multiple_of(ki * tq, tq)
            s = _dot_nt(q, k_ref[pl.ds(off, tq), :])
            if masked:
                row = lax.broadcasted_iota(jnp.int32, (tq, tq), 0)
                col = lax.broadcasted_iota(jnp.int32, (tq, tq), 1)
                s = jnp.where(col <= row, s, -1e30)
            m_new = jnp.maximum(m, jnp.max(s, axis=1, keepdims=True))
            acc = _exp2(m - m_new) * acc + _dot(_exp2(s - m_new), v_ref[pl.ds(off, tq), :])
            return m_new, acc

        init = (jnp.full((tq, 1), -1e30, F32), jnp.zeros((tq, LANES), F32))
        carry = _pair_loop(0, qi, lambda ki, c: block(ki, c, False), init, unrolls=(8, 4, 2, 1))
        m, acc = block(qi, carry, True)
        l = acc[:, ONES_LANE:ONES_LANE + 1]
        o_ref[...] = acc / l
        lse_ref[...] = _as_row(m + jnp.log(l) * LOG2_E)

    blk = pl.BlockSpec((tq, LANES), lambda h, i: (i, h))
    full = pl.BlockSpec((S, LANES), lambda h, i: (0, h))
    return pl.pallas_call(
        body, name="attn_fwd", grid=(MLA_HEADS, S // tq), in_specs=[blk, full, full] + extra_specs,
        out_specs=[blk, pl.BlockSpec((None, None, 1, tq), lambda h, i: (h, i, 0, 0))],
        out_shape=[_sds((S, D), F32), _sds((MLA_HEADS, S // tq, 1, tq), F32)], compiler_params=_cp2())(
            qp, kp, vp, *extra)


def even_post(x, ypool, o, wo_pool, wo_att):
    S = x.shape[0]
    ts = _tile_rows(S, 512)

    def body(x_ref, yp_ref, o_ref, wp_ref, wa_ref, out_ref):
        out_ref[...] = x_ref[...] + _dot(yp_ref[...], wp_ref[...]) + _dot(o_ref[...], wa_ref[...])

    return pl.pallas_call(
        body, name="even_post", grid=(S // ts,),
        in_specs=[_row(ts, D), _row(ts, POOL_DIM), _row(ts, D), _const(wo_pool.shape), _const(wo_att.shape)],
        out_specs=_row(ts, D), out_shape=_sds((S, D), F32), compiler_params=_cp())(x, ypool, o, wo_pool, wo_att)


def mem_kv(mem, g, wkv):
    M = mem.shape[0]

    def body(mem_ref, g_ref, w_ref, mn_ref, k_ref, v_ref):
        mn, _ = _rms(mem_ref[...], g_ref[...])
        mn_ref[...] = mn.astype(BF16)
        k_ref[...] = _dot(mn, w_ref[:, :D]).astype(BF16)
        v_ref[...] = _dot(mn, w_ref[:, D:]).astype(BF16)

    return pl.pallas_call(
        body, name="mem_kv", grid=(1,), in_specs=[_acc(mem.shape), _acc(g.shape), _acc(wkv.shape)],
        out_specs=[_acc((M, D))] * 3, out_shape=[_sds((M, D), BF16)] * 3, compiler_params=_cp())(mem, g, wkv)


def _xattn_heads(hx, wq_ref, k_ref, v_ref):
    q = _dot(hx, wq_ref[...])
    scale = MEM_HEAD_DIM ** -0.5
    ps, os_ = [], []
    for h in range(MEM_HEADS):
        sl = slice(h * MEM_HEAD_DIM, (h + 1) * MEM_HEAD_DIM)
        s = _dot_nt(q[:, sl], k_ref[:, sl]) * scale
        e = jnp.exp(s - jnp.max(s, axis=1, keepdims=True))
        p = e / jnp.sum(e, axis=1, keepdims=True)
        ps.append(p)
        os_.append(_dot(p, v_ref[:, sl]))
    return q, ps, jnp.concatenate(os_, axis=1)


def xattn_fwd(x, g, wq, kmem, vmem, wo):
    S = x.shape[0]
    ts = _tile_rows(S, 512)

    def body(x_ref, g_ref, wq_ref, k_ref, v_ref, wo_ref, out_ref):
        x_ = x_ref[...]
        hx, _ = _rms(x_, g_ref[...])
        _, _, o = _xattn_heads(hx, wq_ref, k_ref, v_ref)
        out_ref[...] = x_ + _dot(o, wo_ref[...])

    ins = [x, g, wq, kmem, vmem, wo]
    return pl.pallas_call(
        body, name="xattn_fwd", grid=(S // ts,), in_specs=[_row(ts, D)] + [_const(v.shape) for v in ins[1:]],
        out_specs=_row(ts, D), out_shape=_sds((S, D), F32), compiler_params=_cp())(*ins)


def xattn_bwd(x, dy, g, wq, kmem, vmem, wo):
    S = x.shape[0]
    M = kmem.shape[0]
    ts = _tile_rows(S, 512)
    scale = MEM_HEAD_DIM ** -0.5

    def body(x_ref, dy_ref, g_ref, wq_ref, k_ref, v_ref, wo_ref,
             dx_ref, o_ref, dq_ref, hx_ref, dg_ref, dk_ref, dv_ref):
        i = pl.program_id(0)

        @pl.when(i == 0)
        def _():
            dg_ref[...] = jnp.zeros_like(dg_ref)
            dk_ref[...] = jnp.zeros_like(dk_ref)
            dv_ref[...] = jnp.zeros_like(dv_ref)

        x_, dy_ = x_ref[...], dy_ref[...]
        do = _dot_nt(dy_, wo_ref[...])
        hx, rstd = _rms(x_, g_ref[...])
        q, ps, o = _xattn_heads(hx, wq_ref, k_ref, v_ref)
        hx_ref[...] = hx.astype(BF16)
        o_ref[...] = o.astype(BF16)
        dqs = []
        for h in range(MEM_HEADS):
            sl = slice(h * MEM_HEAD_DIM, (h + 1) * MEM_HEAD_DIM)
            p, do_h = ps[h], do[:, sl]
            dp = _dot_nt(do_h, v_ref[:, sl])
            ds = p * (dp - jnp.sum(p * dp, axis=1, keepdims=True)) * scale
            dqs.append(_dot(ds, k_ref[:, sl]))
            dk_ref[:, sl] += _dot_tn(ds, q[:, sl])
            dv_ref[:, sl] += _dot_tn(p, do_h)
        dq = jnp.concatenate(dqs, axis=1).astype(BF16)
        dq_ref[...] = dq
        dxn, dgr = _rms_bwd(x_, g_ref[...], rstd, _dot_nt(dq, wq_ref[...]))
        dx_ref[...] = dy_ + dxn
        dg_ref[...] += _rowsum(dgr)

    ins = [x, dy, g, wq, kmem, vmem, wo]
    return pl.pallas_call(
        body, name="xattn_bwd", grid=(S // ts,),
        in_specs=[_row(ts, D), _row(ts, D)] + [_const(v.shape) for v in ins[2:]],
        out_specs=[_row(ts, D)] * 4 + [_acc((1, D)), _acc((M, D)), _acc((M, D))],
        out_shape=[_sds((S, D), F32)] + [_sds((S, D), BF16)] * 3 + [_sds((1, D), F32), _sds((M, D), F32),
                                                                    _sds((M, D), F32)],
        compiler_params=_cp())(*ins)


def mem_bwd(mem, g, dk, dv, wkv):
    M = mem.shape[0]

    def body(mem_ref, g_ref, dk_ref, dv_ref, w_ref, dkv_ref, dg_ref):
        dkv = jnp.concatenate([dk_ref[...], dv_ref[...]], axis=1)
        dkv_ref[...] = dkv.astype(BF16)
        _, rstd = _rms(mem_ref[...], g_ref[...])
        dg_ref[...] = _rowsum(_dot_nt(dkv, w_ref[...]) * (mem_ref[...] * rstd))

    ins = [mem, g, dk, dv, wkv]
    return pl.pallas_call(
        body, name="mem_bwd", grid=(1,), in_specs=[_acc(v.shape) for v in ins],
        out_specs=[_acc((M, 2 * D)), _acc((1, D))], out_shape=[_sds((M, 2 * D), BF16), _sds((1, D), F32)],
        compiler_params=_cp())(*ins)


FF_CHUNK = 2 * D_FF // N_DEV
FF_HALF = N_DEV // 2


def _layer_of(w, layer):
    return pl.BlockSpec((N_DEV, None) + w.shape[2:], lambda i: (0, layer, 0, 0), pipeline_mode=pl.Buffered(1))


def _ff_chunks(c, ts):
    return pl.BlockSpec((c, ts, FF_CHUNK), lambda i: (0, i, 0))


def _ffn(x_, g_ref, wgu_ref, wd_ref, hf_ref, gu_ref):
    hf = _rms(x_, g_ref[...])[0].astype(BF16)
    hf_ref[...] = hf
    out = x_
    for j in range(FF_HALF):
        gg, uu = _dot(hf, wgu_ref[j]), _dot(hf, wgu_ref[j + FF_HALF])
        gu_ref[j] = gg.astype(BF16)
        gu_ref[j + FF_HALF] = uu.astype(BF16)
        out = out + _dot(gg * jax.nn.sigmoid(gg) * uu, wd_ref[j])
    return out


def ffn_fwd(x, g, wgu, layer, wd):
    S = x.shape[0]
    ts = _tile_rows(S, 256)

    def body(x_ref, g_ref, wgu_ref, wd_ref, out_ref, hf_ref, gu_ref):
        out_ref[...] = _ffn(x_ref[...], g_ref, wgu_ref, wd_ref, hf_ref, gu_ref)

    return pl.pallas_call(
        body, name="ffn_fwd", grid=(S // ts,),
        in_specs=[_row(ts, D), _const(g.shape), _layer_of(wgu, layer), _const(wd.shape)],
        out_specs=[_row(ts, D), _row(ts, D), _ff_chunks(N_DEV, ts)],
        out_shape=[_sds((S, D), F32), _sds((S, D), BF16), _sds((N_DEV, S, FF_CHUNK), BF16)],
        compiler_params=_cp())(x, g, wgu, wd)


def ffn_fwd_loss(x, g, wgu, layer, wd, target, gf):
    S = x.shape[0]
    ts = _tile_rows(S, 256)

    def body(x_ref, g_ref, wgu_ref, wd_ref, t_ref, gf_ref, dx_ref, dgf_ref, loss_ref, hf_ref, gu_ref):
        @pl.when(pl.program_id(0) == 0)
        def _():
            dgf_ref[...] = jnp.zeros_like(dgf_ref)
            loss_ref[...] = jnp.zeros_like(loss_ref)

        out = _ffn(x_ref[...], g_ref, wgu_ref, wd_ref, hf_ref, gu_ref)
        y, rstd = _rms(out, gf_ref[...])
        err = y - t_ref[...]
        loss_ref[...] += 0.5 * _rowsum(jnp.mean(err * err, axis=1, keepdims=True))
        dxn, dgr = _rms_bwd(out, gf_ref[...], rstd, err * (1.0 / D))
        dx_ref[...] = dxn
        dgf_ref[...] += _rowsum(dgr)

    return pl.pallas_call(
        body, name="ffn_fwd_loss", grid=(S // ts,),
        in_specs=[_row(ts, D), _const(g.shape), _layer_of(wgu, layer), _const(wd.shape), _row(ts, D),
                  _const(gf.shape)],
        out_specs=[_row(ts, D), _acc((1, D)), _acc((1, 1)), _row(ts, D), _ff_chunks(N_DEV, ts)],
        out_shape=[_sds((S, D), F32), _sds((1, D), F32), _sds((1, 1), F32), _sds((S, D), BF16),
                   _sds((N_DEV, S, FF_CHUNK), BF16)],
        compiler_params=_cp())(x, g, wgu, wd, target, gf)


def ffn_bwd(x, dy, gu, g, wgu, layer, wd):
    S = x.shape[0]
    ts = _tile_rows(S, 256)

    def body(x_ref, dy_ref, gu_ref, g_ref, wgu_ref, wd_ref, dx_ref, dg_ref, act_ref, dgu_ref):
        @pl.when(pl.program_id(0) == 0)
        def _():
            dg_ref[...] = jnp.zeros_like(dg_ref)

        dy_ = dy_ref[...]
        dyb = dy_.astype(BF16)
        dh = jnp.zeros((ts, D), F32)
        dacts = [_dot_nt(dyb, wd_ref[j]) for j in range(FF_HALF)]
        for j in range(FF_HALF):
            gg, uu = gu_ref[j].astype(F32), gu_ref[j + FF_HALF].astype(F32)
            sg = jax.nn.sigmoid(gg)
            silu = gg * sg
            act_ref[j] = (silu * uu).astype(BF16)
            dact = dacts[j]
            dgate = (dact * uu * (sg * (1.0 + gg * (1.0 - sg)))).astype(BF16)
            dup = (dact * silu).astype(BF16)
            dgu_ref[j] = dgate
            dgu_ref[j + FF_HALF] = dup
            dh = dh + _dot_nt(dgate, wgu_ref[j]) + _dot_nt(dup, wgu_ref[j + FF_HALF])
        x_ = x_ref[...]
        _, rstd = _rms(x_, g_ref[...])
        dxn, dgr = _rms_bwd(x_, g_ref[...], rstd, dh)
        dx_ref[...] = dy_ + dxn
        dg_ref[...] += _rowsum(dgr)

    return pl.pallas_call(
        body, name="ffn_bwd", grid=(S // ts,),
        in_specs=[_row(ts, D), _row(ts, D), _ff_chunks(N_DEV, ts), _const(g.shape), _layer_of(wgu, layer),
                  _const(wd.shape)],
        out_specs=[_row(ts, D), _acc((1, D)), _ff_chunks(FF_HALF, ts), _ff_chunks(N_DEV, ts)],
        out_shape=[_sds((S, D), F32), _sds((1, D), F32), _sds((FF_HALF, S, FF_CHUNK), BF16),
                   _sds((N_DEV, S, FF_CHUNK), BF16)],
        compiler_params=_cp())(x, dy, gu, g, wgu, wd)


def _conv_fwd(xprev, xbp, cw_ref, cb):
    ext = jnp.concatenate([xprev, xbp], axis=0)
    acc = cb + cw_ref[3:4, :] * xbp
    for k in range(3):
        acc = acc + cw_ref[k:k + 1, :] * _roll(ext, 3 - k, 0)[CONV_HALO:]
    return acc


def _decay(r, lam):
    sp = _softplus(-lam)
    log_a = -LRU_C * r * sp
    return sp, jnp.exp(log_a), jnp.sqrt(jnp.maximum(-_expm1(2.0 * log_a), 0.0))


def odd_pre(x, keep, g, win, cw, cb, wr, br, wi, bi, lam):
    S = x.shape[0]
    ts = _tile_rows(S, 512)

    def body(x_ref, xp_ref, keep_ref, g_ref, win_ref, cw_ref, cb_ref, wr_ref, br_ref, wi_ref, bi_ref, lam_ref,
             z_ref, a_ref, b_ref, xb_ref, r_ref, ig_ref):
        i = pl.program_id(0)
        h, _ = _rms(x_ref[...], g_ref[...])
        z = _dot(h, win_ref[...])
        z_ref[...] = z
        hp, _ = _rms(xp_ref[...], g_ref[...])
        xprev = _dot(hp, win_ref[:, D:]) * (i > 0).astype(F32)
        xb = _conv_fwd(xprev, z[:, D:], cw_ref, cb_ref[...])
        xb_ref[...] = xb
        r = jax.nn.sigmoid(_blockdot(xb, wr_ref, LRU_HEADS, LRU_HEAD_DIM) + br_ref[...])
        ig = jax.nn.sigmoid(_blockdot(xb, wi_ref, LRU_HEADS, LRU_HEAD_DIM) + bi_ref[...])
        r_ref[...] = r
        ig_ref[...] = ig
        keep_ = keep_ref[...]
        _, a, mult = _decay(r, lam_ref[...])
        a_ref[...] = a * keep_
        b_ref[...] = jnp.where(keep_ > 0.0, mult, 1.0) * (ig * xb)

    ins = [x, x, keep, g, win, cw, cb, wr, br, wi, bi, lam]
    return pl.pallas_call(
        body, name="odd_pre", grid=(S // ts,),
        in_specs=[_row(ts, D), _prev(CONV_HALO, D, ts), _row(ts, 1)] + [_const(v.shape) for v in ins[3:]],
        out_specs=[_row(ts, 2 * D)] + [_row(ts, D)] * 5,
        out_shape=[_sds((S, 2 * D), F32)] + [_sds((S, D), F32)] * 5, compiler_params=_cp())(*ins)


def lru_scan(a, b, reverse=False):
    S = a.shape[0]
    ts = _tile_rows(S, 512)
    n = S // ts
    groups = ts // 8

    def body(a_ref, an_ref, b_ref, h_ref, carry_ref, ash_ref):
        i = pl.program_id(0)

        @pl.when(i == 0)
        def _():
            carry_ref[...] = jnp.zeros_like(carry_ref)

        rid = lax.broadcasted_iota(jnp.int32, (8, D), 0)
        if reverse:
            ext = jnp.concatenate([a_ref[...], an_ref[...] * (i > 0).astype(F32)], axis=0)
            ash_ref[...] = _roll(ext, -1, 0)[:ts]
        src = ash_ref if reverse else a_ref

        def group(j, carry):
            off = pl.multiple_of((groups - 1 - j if reverse else j) * 8, 8)
            a8, b8 = src[pl.ds(off, 8), :], b_ref[pl.ds(off, 8), :]
            for k in (1, 2, 4):
                inside = (rid < 8 - k) if reverse else (rid >= k)
                sh = -k if reverse else k
                a_sh = jnp.where(inside, _roll(a8, sh, 0), 1.0)
                b_sh = jnp.where(inside, _roll(b8, sh, 0), 0.0)
                b8 = a8 * b_sh + b8
                a8 = a8 * a_sh
            h8 = a8 * carry + b8
            h_ref[pl.ds(off, 8), :] = h8
            return h8[0:1, :] if reverse else h8[7:8, :]

        carry_ref[...] = lax.fori_loop(0, groups, group, carry_ref[...], unroll=4)

    if reverse:
        r = ts // 8
        tile = pl.BlockSpec((ts, D), lambda i: (n - 1 - i, 0))
        halo = pl.BlockSpec((8, D), lambda i: (jnp.minimum((n - i) * r, n * r - 1), 0))
    else:
        tile, halo = _row(ts, D), _prev(8, D, ts)
    return pl.pallas_call(
        body, name="lru_scan_rev" if reverse else "lru_scan", grid=(n,), in_specs=[tile, halo, tile],
        out_specs=tile, out_shape=_sds((S, D), F32),
        scratch_shapes=[pltpu.VMEM((1, D), F32), pltpu.VMEM((ts, D), F32)], compiler_params=_cp())(a, a, b)


def odd_post(x, z, hseq, wout):
    S = x.shape[0]
    ts = _tile_rows(S, 512)

    def body(x_ref, gate_ref, h_ref, w_ref, out_ref):
        gl, _ = _gelu(gate_ref[...])
        out_ref[...] = x_ref[...] + _dot(gl * h_ref[...], w_ref[...])

    return pl.pallas_call(
        body, name="odd_post", grid=(S // ts,),
        in_specs=[_row(ts, D), _row(ts, D), _row(ts, D), _const(wout.shape)],
        out_specs=_row(ts, D), out_shape=_sds((S, D), F32), compiler_params=_cp())(x, z, hseq, wout)


def _accumulate_tn(acc_ref, out_ref, a, b, steps):
    i = pl.program_id(0)

    @pl.when(i == 0)
    def _():
        acc_ref[...] = jnp.zeros_like(acc_ref)

    acc_ref[...] += _dot_tn(a, b)

    @pl.when(i == steps - 1)
    def _():
        out_ref[...] = acc_ref[...].astype(out_ref.dtype)


def odd_post_bwd(dy, z, hseq, wout):
    S = dy.shape[0]
    ts = _tile_rows(S, 512)
    n = S // ts

    def body(dy_ref, gate_ref, h_ref, w_ref, dgate_ref, dh_ref, dw_ref, acc_ref):
        gate, hs, dy_ = gate_ref[...], h_ref[...], dy_ref[...]
        gl, t = _gelu(gate)
        dyy = _dot_nt(dy_, w_ref[...])
        dgate_ref[...] = dyy * hs * _gelu_grad(gate, t)
        dh_ref[...] = dyy * gl
        _accumulate_tn(acc_ref, dw_ref, gl * hs, dy_, n)

    return pl.pallas_call(
        body, name="odd_post_bwd", grid=(n,),
        in_specs=[_row(ts, D), _row(ts, D), _row(ts, D), _const(wout.shape)],
        out_specs=[_row(ts, D), _row(ts, D), _acc((D, D))],
        out_shape=[_sds((S, D), F32), _sds((S, D), F32), _sds((D, D), BF16)],
        scratch_shapes=[pltpu.VMEM((D, D), F32)], compiler_params=_cp())(dy, z, hseq, wout)


def odd_gates_bwd(xb, r, ig, lam_grad, hseq, keep, wr, wi, lam):
    S = xb.shape[0]
    ts = _tile_rows(S, 512)

    def body(xb_ref, r_ref, ig_ref, lg_ref, h_ref, hp_ref, keep_ref, wr_ref, wi_ref, lam_ref,
             dxb_ref, dcb_ref, dbr_ref, dbi_ref, dlam_ref, dwr_ref, dwi_ref):
        i = pl.program_id(0)

        @pl.when(i == 0)
        def _():
            for ref in (dcb_ref, dbr_ref, dbi_ref, dlam_ref, dwr_ref, dwi_ref):
                ref[...] = jnp.zeros_like(ref)

        first = (i > 0).astype(F32)
        xb, r, ig = xb_ref[...], r_ref[...], ig_ref[...]
        keep_ = keep_ref[...]
        lam_ = lam_ref[...]
        sp, a, mult = _decay(r, lam_)
        hs = h_ref[...]
        hprev = _roll(jnp.concatenate([hp_ref[...] * first, hs], axis=0), 1, 0)[CONV_HALO:]
        lg = lg_ref[...]
        da = lg * hprev * keep_
        ixb = ig * xb
        dmult = lg * ixb * keep_
        dixb = lg * jnp.where(keep_ > 0.0, mult, 1.0)
        dlog_a = da * a - dmult * jnp.where(mult > 0.0, a * a / mult, 0.0)
        dr = dlog_a * (-LRU_C * sp)
        dlam_ref[...] += _rowsum(dlog_a * (-LRU_C * r)) * (-jax.nn.sigmoid(-lam_))
        dpr = dr * r * (1.0 - r)
        dpi = dixb * xb * ig * (1.0 - ig)
        dbr_ref[...] += _rowsum(dpr)
        dbi_ref[...] += _rowsum(dpi)
        dxb = dixb * ig
        parts = []
        for h in range(LRU_HEADS):
            sl = slice(h * LRU_HEAD_DIM, (h + 1) * LRU_HEAD_DIM)
            dwr_ref[h] += _dot_tn(xb[:, sl], dpr[:, sl])
            dwi_ref[h] += _dot_tn(xb[:, sl], dpi[:, sl])
            parts.append(_dot_nt(dpr[:, sl], wr_ref[h]) + _dot_nt(dpi[:, sl], wi_ref[h]))
        dxb = dxb + jnp.concatenate(parts, axis=1)
        dxb_ref[...] = dxb
        dcb_ref[...] += _rowsum(dxb)

    ins = [xb, r, ig, lam_grad, hseq, hseq, keep, wr, wi, lam]
    in_specs = [_row(ts, D)] * 5 + [_prev(CONV_HALO, D, ts), _row(ts, 1)] + [_const(v.shape) for v in ins[7:]]
    gshape = (LRU_HEADS, LRU_HEAD_DIM, LRU_HEAD_DIM)
    return pl.pallas_call(
        body, name="odd_gates_bwd", grid=(S // ts,), in_specs=in_specs,
        out_specs=[_row(ts, D)] + [_acc((1, D))] * 4 + [_acc(gshape)] * 2,
        out_shape=[_sds((S, D), F32)] + [_sds((1, D), F32)] * 4 + [_sds(gshape, F32)] * 2,
        compiler_params=_cp())(*ins)


def odd_pre_bwd(x, dy, z, dxb, dgate, g, cw, win):
    S = x.shape[0]
    ts = _tile_rows(S, 512)
    n = S // ts

    def body(x_ref, dy_ref, xbp_ref, xbpp_ref, dxb_ref, dxbn_ref, dgate_ref, g_ref, cw_ref, win_ref,
             dx_ref, dcw_ref, dg_ref, dwin_ref, acc_ref):
        i = pl.program_id(0)

        @pl.when(i == 0)
        def _():
            dcw_ref[...] = jnp.zeros_like(dcw_ref)
            dg_ref[...] = jnp.zeros_like(dg_ref)

        dxb = dxb_ref[...]
        extd = jnp.concatenate([dxb, dxbn_ref[...] * (i < n - 1).astype(F32)], axis=0)
        extx = jnp.concatenate([xbpp_ref[...] * (i > 0).astype(F32), xbp_ref[...]], axis=0)
        dxbp = cw_ref[3:4, :] * dxb
        dcw_ref[3:4, :] += _rowsum(dxb * xbp_ref[...])
        for k in range(3):
            dxbp = dxbp + cw_ref[k:k + 1, :] * _roll(extd, -(3 - k), 0)[:ts]
            dcw_ref[k:k + 1, :] += _rowsum(dxb * _roll(extx, 3 - k, 0)[CONV_HALO:])
        dz = jnp.concatenate([dgate_ref[...], dxbp], axis=1).astype(BF16)
        x_ = x_ref[...]
        h, rstd = _rms(x_, g_ref[...])
        dxn, dgr = _rms_bwd(x_, g_ref[...], rstd, _dot_nt(dz, win_ref[...]))
        dx_ref[...] = dy_ref[...] + dxn
        dg_ref[...] += _rowsum(dgr)
        _accumulate_tn(acc_ref, dwin_ref, h, dz, n)

    ins = [x, dy, z, z, dxb, dxb, dgate, g, cw, win]
    in_specs = [_row(ts, D), _row(ts, D), _row(ts, D, 1), _prev(CONV_HALO, D, ts, 1), _row(ts, D),
                _next(CONV_HALO, D, ts, n), _row(ts, D)] + [_const(v.shape) for v in ins[7:]]
    return pl.pallas_call(
        body, name="odd_pre_bwd", grid=(n,), in_specs=in_specs,
        out_specs=[_row(ts, D), _acc((4, D)), _acc((1, D)), _acc((D, 2 * D))],
        out_shape=[_sds((S, D), F32), _sds((4, D), F32), _sds((1, D), F32), _sds((D, 2 * D), BF16)],
        scratch_shapes=[pltpu.VMEM((D, 2 * D), F32)], compiler_params=_cp())(*ins)


def even_post_bwd(dy, ypool, o, wo_pool, wo_att):
    S = dy.shape[0]
    ts = _tile_rows(S, 512)
    n = S // ts

    def body(dy_ref, yp_ref, o_ref, wp_ref, wa_ref, dyp_ref, do_ref, delta_ref, dwp_ref, dwa_ref, accp_ref,
             acca_ref):
        dy_, o_ = dy_ref[...], o_ref[...]
        dyp_ref[...] = _dot_nt(dy_, wp_ref[...])
        do = _dot_nt(dy_, wa_ref[...])
        do_ref[...] = do.astype(BF16)
        prod = do * o_
        for h in range(MLA_HEADS):
            delta_ref[h] = _as_row(jnp.sum(prod[:, h * LANES:(h + 1) * LANES], axis=1, keepdims=True))
        _accumulate_tn(accp_ref, dwp_ref, yp_ref[...], dy_, n)
        _accumulate_tn(acca_ref, dwa_ref, o_, dy_, n)

    return pl.pallas_call(
        body, name="even_post_bwd", grid=(n,),
        in_specs=[_row(ts, D), _row(ts, POOL_DIM), _row(ts, D), _const(wo_pool.shape), _const(wo_att.shape)],
        out_specs=[_row(ts, POOL_DIM), _row(ts, D),
                   pl.BlockSpec((MLA_HEADS, None, 1, ts), lambda i: (0, i, 0, 0)), _acc((POOL_DIM, D)),
                   _acc((D, D))],
        out_shape=[_sds((S, POOL_DIM), F32), _sds((S, D), BF16), _sds((MLA_HEADS, n, 1, ts), F32),
                   _sds((POOL_DIM, D), BF16), _sds((D, D), BF16)],
        scratch_shapes=[pltpu.VMEM((POOL_DIM, D), F32), pltpu.VMEM((D, D), F32)],
        compiler_params=_cp())(dy, ypool, o, wo_pool, wo_att)


def attn_bwd(qp, kp, vp, do, lse_row, delta_row, token=None):
    S = qp.shape[0]
    tk = _tile_rows(S, 512)
    nq = S // tk
    extra, extra_specs = _after(token)

    def body(q_ref, k_ref, v_ref, do_ref, lse_ref, delta_ref, *rest):
        dq_ref, dk_ref, dv_ref = rest[-3:]
        kj = pl.program_id(1)

        @pl.when(kj == 0)
        def _():
            dq_ref[...] = jnp.zeros_like(dq_ref)

        k, v = k_ref[...], v_ref[...]

        def block(qi, carry, masked):
            dk, dv = carry
            off = pl.multiple_of(qi * tk, tk)
            q = q_ref[pl.ds(off, tk), :]
            do_ = do_ref[pl.ds(off, tk), :]
            st = _dot_nt(k, q)
            if masked:
                row = lax.broadcasted_iota(jnp.int32, (tk, tk), 0)
                col = lax.broadcasted_iota(jnp.int32, (tk, tk), 1)
                st = jnp.where(col >= row, st, -1e30)
            pt = _exp2(st - lse_ref[qi])
            dv = dv + _dot(pt, do_)
            dst = (pt * (_dot_nt(v, do_) - delta_ref[qi])).astype(BF16)
            dk = dk + _dot(dst, q)
            dq_ref[pl.ds(off, tk), :] += _dot_tn(dst, k)
            return dk, dv

        zero = jnp.zeros((tk, LANES), F32)
        carry = block(kj, (zero, zero), True)
        dk, dv = _pair_loop(kj + 1, nq, lambda qi, c: block(qi, c, False), carry, unrolls=(4, 2, 1))
        dk_ref[...] = dk * LN_2
        dv_ref[...] = dv

    blk = pl.BlockSpec((tk, LANES), lambda h, j: (j, h))
    full = pl.BlockSpec((S, LANES), lambda h, j: (0, h))
    rowv = pl.BlockSpec((None, nq, 1, tk), lambda h, j: (h, 0, 0, 0))
    return pl.pallas_call(
        body, name="attn_bwd", grid=(MLA_HEADS, nq), in_specs=[full, blk, blk, full, rowv, rowv] + extra_specs,
        out_specs=[full, blk, blk], out_shape=[_sds((S, D), F32)] * 3, compiler_params=_cp2())(
            qp, kp, vp, do, lse_row, delta_row, *extra)


def even_pre_bwd(x, dy, z, dq, dk, dv, dyp, tabs, g, win, pw, pscale, qg, wq, kvg, wk, wv):
    S = x.shape[0]
    ts = _tile_rows(S, 512)
    n = S // ts

    def body(x_ref, dy_ref, z_ref, up_ref, dq_ref, dk_ref, dv_ref, dyp_ref, dypn_ref, c_ref, a_ref, b_ref,
             g_ref, win_ref, pw_ref, ps_ref, qg_ref, wq_ref, kvg_ref, wk_ref, wv_ref,
             dx_ref, dg_ref, dpw_ref, dps_ref, dqg_ref, dwq_ref, dkvg_ref, dwk_ref, dwv_ref, dwin_ref, acc_ref):
        i = pl.program_id(0)

        @pl.when(i == 0)
        def _():
            for ref in (dg_ref, dpw_ref, dps_ref, dqg_ref, dwq_ref, dkvg_ref, dwk_ref, dwv_ref):
                ref[...] = jnp.zeros_like(ref)

        z = z_ref[...]
        c, a, b = c_ref[...], a_ref[...], b_ref[...]
        ps = ps_ref[...]
        u = z[:, :POOL_DIM]
        pooled = _pooled(up_ref[...] * (i > 0).astype(F32), u, i * ts)
        dyp_ = dyp_ref[...]
        dps_ref[...] += _rowsum(dyp_ * _blockdot(pooled, pw_ref, 4, LANES))
        ext = jnp.concatenate([dyp_, dypn_ref[...] * (i < n - 1).astype(F32)], axis=0) * ps
        for gidx in range(4):
            sl = slice(gidx * LANES, (gidx + 1) * LANES)
            dpw_ref[gidx] += _dot_tn(pooled[:, sl], ext[:ts, sl])
        dpooled = jnp.concatenate(
            [_dot_nt(ext[:, gidx * LANES:(gidx + 1) * LANES], pw_ref[gidx]) for gidx in range(4)], axis=1)
        dm = dpooled / _pool_cnt(i * ts, ts + POOL_HALO)
        du = _pool_windows(dm, -1)[:ts] - dpooled[:ts]
        cq = z[:, 512:768]
        cqn, rstd_q = _rms(cq, qg_ref[...])
        dqf = _rope_bwd(dq_ref[...] * ATTN_SCALE, c, a, b)
        dwq_ref[...] += _dot_tn(cqn, dqf)
        dcq, dqg_rows = _rms_bwd(cq, qg_ref[...], rstd_q, _dot_nt(dqf, wq_ref[...]))
        dqg_ref[...] += _rowsum(dqg_rows)
        ckv = z[:, 768:896]
        ckvn, rstd_kv = _rms(ckv, kvg_ref[...])
        dk_, dv_ = dk_ref[...], dv_ref[...]
        dwk_ref[...] += _dot_tn(ckvn, dk_)
        dwv_ref[...] += _dot_tn(ckvn, dv_)
        dckv, dkvg_rows = _rms_bwd(ckv, kvg_ref[...], rstd_kv,
                                   _dot_nt(dk_, wk_ref[...]) + _dot_nt(dv_, wv_ref[...]))
        dkvg_ref[...] += _rowsum(dkvg_rows)
        dkr = dk_[:, :LANES]
        for h in range(1, MLA_HEADS):
            dkr = dkr + dk_[:, h * LANES:(h + 1) * LANES]
        lane = lax.broadcasted_iota(jnp.int32, (ts, LANES), 1)
        dkr = jnp.where((lane >= 64) & (lane < 96), _rope_bwd(dkr, c, a, b), 0.0)
        dz = jnp.concatenate([du, dcq, dckv, dkr], axis=1).astype(BF16)
        x_ = x_ref[...]
        h, rstd = _rms(x_, g_ref[...])
        dxn, dgr = _rms_bwd(x_, g_ref[...], rstd, _dot_nt(dz, win_ref[...]))
        dx_ref[...] = dy_ref[...] + dxn
        dg_ref[...] += _rowsum(dgr)
        _accumulate_tn(acc_ref, dwin_ref, h, dz, n)

    ins = [x, dy, z, z, dq, dk, dv, dyp, dyp, *tabs, g, win, pw, pscale, qg, wq, kvg, wk, wv]
    in_specs = [_row(ts, D), _row(ts, D), _row(ts, D), _prev(POOL_HALO, POOL_DIM, ts), _row(ts, D), _row(ts, D),
                _row(ts, D), _row(ts, POOL_DIM), _next(POOL_HALO, POOL_DIM, ts, n), _row(ts, LANES),
                _row(ts, LANES), _row(ts, LANES)] + [_const(v.shape) for v in ins[12:]]
    acc_shapes = [(1, D), (4, LANES, LANES), (1, POOL_DIM), (1, Q_LORA), (Q_LORA, D), (1, KV_LORA), (KV_LORA, D),
                  (KV_LORA, D)]
    return pl.pallas_call(
        body, name="even_pre_bwd", grid=(n,), in_specs=in_specs,
        out_specs=[_row(ts, D)] + [_acc(s) for s in acc_shapes] + [_acc((D, D))],
        out_shape=[_sds((S, D), F32)] + [_sds(s, F32) for s in acc_shapes] + [_sds((D, D), BF16)],
        scratch_shapes=[pltpu.VMEM((D, D), F32)], compiler_params=_cp())(*ins)


def _pick(n, options):
    for o in options:
        if n % o == 0:
            return o
    return n


def matmul_tn(name, a, b):
    out_dtype = BF16
    S = a.shape[-2]
    ts = _tile_rows(S, 2048)
    steps = S // ts

    def body(a_ref, b_ref, o_ref, acc_ref):
        s = pl.program_id(2)

        @pl.when(s == 0)
        def _():
            acc_ref[...] = jnp.zeros_like(acc_ref)

        acc_ref[...] += _dot_tn(a_ref[...], b_ref[...])

        @pl.when(s == steps - 1)
        def _():
            o_ref[...] = acc_ref[...].astype(o_ref.dtype)

    if a.ndim == 3:
        C, _, K = a.shape
        N = b.shape[1]
        tn = _pick(N, (1024, 512, 256, 128))
        grid = (C, N // tn, S // ts)
        in_specs = [pl.BlockSpec((None, ts, K), lambda c, j, s: (c, s, 0)),
                    pl.BlockSpec((ts, tn), lambda c, j, s: (s, j))]
        out_spec, out_shape, tile = pl.BlockSpec((None, K, tn), lambda c, j, s: (c, 0, j)), (C, K, N), (K, tn)
    elif b.ndim == 3:
        C, _, N = b.shape
        K = a.shape[1]
        tk = _pick(K, (1024, 512, 256, 128))
        grid = (C, K // tk, S // ts)
        in_specs = [pl.BlockSpec((ts, tk), lambda c, i, s: (s, i)),
                    pl.BlockSpec((None, ts, N), lambda c, i, s: (c, s, 0))]
        out_spec, out_shape, tile = pl.BlockSpec((None, tk, N), lambda c, i, s: (c, i, 0)), (C, K, N), (tk, N)
    else:
        K, N = a.shape[1], b.shape[1]
        tk = _pick(K, (1024, 512, 256, 128))
        tn = _pick(N, (1024, 512, 256, 128))
        grid = (K // tk, N // tn, S // ts)
        in_specs = [pl.BlockSpec((ts, tk), lambda i, j, s: (s, i)), pl.BlockSpec((ts, tn), lambda i, j, s: (s, j))]
        out_spec, out_shape, tile = pl.BlockSpec((tk, tn), lambda i, j, s: (i, j)), (K, N), (tk, tn)
    return pl.pallas_call(
        body, name=name, grid=grid, in_specs=in_specs, out_specs=out_spec, out_shape=_sds(out_shape, out_dtype),
        scratch_shapes=[pltpu.VMEM(tile, F32)], compiler_params=pltpu.CompilerParams(dimension_semantics=("arbitrary",) * 3, vmem_limit_bytes=VMEM_LIMIT))(
            a, b)


def _my_id():
    return lax.axis_index("x") * 4 + lax.axis_index("y") * 2 + lax.axis_index("c")


def _peer(j):
    x, y, c = lax.axis_index("x"), lax.axis_index("y"), lax.axis_index("c")
    px = 1 - x if j & 4 else x
    py = 1 - y if j & 2 else y
    pc = 1 - c if j & 1 else c
    return (px, py, pc), px * 4 + py * 2 + pc


def all_gather(name, arrays):
    n = len(arrays)

    def body(*refs):
        ins, outs = refs[:n], refs[n:2 * n]
        send_sems, recv_sems, local_sems = refs[2 * n:]
        me = _my_id()
        local = [pltpu.make_async_copy(ins[k], outs[k].at[me], local_sems.at[k]) for k in range(n)]
        for cp in local:
            cp.start()
        sends = []
        for j in range(1, N_DEV):
            peer, _ = _peer(j)
            for k in range(n):
                cp = pltpu.make_async_remote_copy(
                    src_ref=ins[k], dst_ref=outs[k].at[me], send_sem=send_sems.at[k, j - 1],
                    recv_sem=recv_sems.at[k, j - 1], device_id=peer, device_id_type=pl.DeviceIdType.MESH)
                cp.start()
                sends.append(cp)
        for j in range(1, N_DEV):
            peer, pid = _peer(j)
            for k in range(n):
                pltpu.make_async_remote_copy(
                    src_ref=ins[k], dst_ref=outs[k].at[pid], send_sem=send_sems.at[k, j - 1],
                    recv_sem=recv_sems.at[k, j - 1], device_id=peer, device_id_type=pl.DeviceIdType.MESH).wait_recv()
        for cp in sends:
            cp.wait_send()
        for cp in local:
            cp.wait()

    any_spec = pl.BlockSpec(memory_space=pl.ANY)
    return pl.pallas_call(
        body, name=name, in_specs=[any_spec] * n, out_specs=[any_spec] * n,
        out_shape=[_sds((N_DEV,) + a.shape, a.dtype) for a in arrays],
        scratch_shapes=[pltpu.SemaphoreType.DMA((n, N_DEV - 1)), pltpu.SemaphoreType.DMA((n, N_DEV - 1)),
                        pltpu.SemaphoreType.DMA((n,))],
        compiler_params=pltpu.CompilerParams(has_side_effects=True))(*arrays)


def exchange(name, arrays, gathers=()):
    n_ex, n = len(arrays), len(arrays) + len(gathers)

    def body(*refs):
        ins, outs = refs[:n], refs[n:2 * n]
        send_sems, recv_sems, local_sems = refs[2 * n:]
        me = _my_id()

        def mine(k, slot):
            return ins[k].at[slot] if k < n_ex else ins[k]

        local = [pltpu.make_async_copy(mine(k, me), outs[k].at[me], local_sems.at[k]) for k in range(n)]
        for cp in local:
            cp.start()
        sends = []
        for j in range(1, N_DEV):
            peer, pid = _peer(j)
            for k in range(n):
                cp = pltpu.make_async_remote_copy(
                    src_ref=mine(k, pid), dst_ref=outs[k].at[me], send_sem=send_sems.at[k, j - 1],
                    recv_sem=recv_sems.at[k, j - 1], device_id=peer, device_id_type=pl.DeviceIdType.MESH)
                cp.start()
                sends.append(cp)
        for j in range(1, N_DEV):
            peer, pid = _peer(j)
            for k in range(n):
                pltpu.make_async_remote_copy(
                    src_ref=mine(k, me), dst_ref=outs[k].at[pid], send_sem=send_sems.at[k, j - 1],
                    recv_sem=recv_sems.at[k, j - 1], device_id=peer, device_id_type=pl.DeviceIdType.MESH).wait_recv()
        for cp in sends:
            cp.wait_send()
        for cp in local:
            cp.wait()

    any_spec = pl.BlockSpec(memory_space=pl.ANY)
    return pl.pallas_call(
        body, name=name, in_specs=[any_spec] * n, out_specs=[any_spec] * n,
        out_shape=[_sds(a.shape, a.dtype) for a in arrays] + [_sds((N_DEV,) + a.shape, a.dtype) for a in gathers],
        scratch_shapes=[pltpu.SemaphoreType.DMA((n, N_DEV - 1)), pltpu.SemaphoreType.DMA((n, N_DEV - 1)),
                        pltpu.SemaphoreType.DMA((n,))],
        compiler_params=pltpu.CompilerParams(has_side_effects=True))(*arrays, *gathers)


_HBM = pl.BlockSpec(memory_space=pltpu.HBM)
_SEM = pl.BlockSpec(memory_space=pltpu.SEMAPHORE)
_DATAFLOW = pltpu.SideEffectType.DATAFLOW_SIDE_EFFECTING


def _in_hbm(v):
    return pltpu.with_memory_space_constraint(v, pltpu.HBM)


N_PEERS = N_DEV - 1


def _split_copy(k, j, srcs, lands, send_sems, recv_sems, gather, slot):
    peer, pid = _peer(j)
    return pltpu.make_async_remote_copy(
        src_ref=srcs[k] if _flag(gather, k) else srcs[k].at[pid],
        dst_ref=lands[k].at[_my_id() if slot == "mine" else pid],
        send_sem=send_sems[j - 1], recv_sem=recv_sems[j - 1], device_id=peer, device_id_type=pl.DeviceIdType.MESH)


def _flag(gather, k):
    return gather[k] if isinstance(gather, tuple) else gather


def split_start(name, arrays, gather):
    n = len(arrays)
    lands = [lax.empty((N_DEV,) + a.shape if _flag(gather, k) else a.shape, a.dtype) for k, a in enumerate(arrays)]

    def body(*refs):
        srcs, lnds = refs[:n], refs[n:2 * n]
        sems = refs[4 * n:4 * n + 2 * N_PEERS]
        token = refs[-1]
        for j in range(1, N_DEV):
            for k in range(n):
                _split_copy(k, j, srcs, lnds, sems[:N_PEERS], sems[N_PEERS:], gather, "mine").start()
        token[...] = jnp.zeros_like(token)

    out = pl.pallas_call(
        body, name=name,
        out_shape=(*[pltpu.HBM(a.shape, a.dtype) for a in arrays], *[pltpu.HBM(l.shape, l.dtype) for l in lands],
                   *[pltpu.SemaphoreType.DMA(())] * (2 * N_PEERS), _sds((8, LANES), F32)),
        in_specs=[_HBM] * (2 * n),
        out_specs=(*[_HBM] * (2 * n), *[_SEM] * (2 * N_PEERS), pl.BlockSpec(memory_space=pltpu.VMEM)),
        input_output_aliases={k: k for k in range(2 * n)},
        compiler_params=pltpu.CompilerParams(has_side_effects=_DATAFLOW))(
            *[_in_hbm(a) for a in arrays], *[_in_hbm(l) for l in lands])
    sems = list(out[2 * n:2 * n + 2 * N_PEERS])
    return sems[:N_PEERS], sems[N_PEERS:], list(out[:n]), list(out[n:2 * n]), out[-1]


def split_wait(name, handle, after, gather):
    send_sems, recv_sems, srcs, lands, _ = handle
    n = len(srcs)

    def body(*refs):
        srcs_r, lnds_r = refs[:n], refs[n:2 * n]
        sems = refs[2 * n:2 * n + 2 * N_PEERS]
        for j in range(1, N_DEV):
            for k in range(n):
                cp = _split_copy(k, j, srcs_r, lnds_r, sems[:N_PEERS], sems[N_PEERS:], gather, "peer")
                cp.wait_send()
                cp.wait_recv()

    out = pl.pallas_call(
        body, name=name, out_shape=tuple(pltpu.HBM(a.shape, a.dtype) for a in srcs + lands),
        in_specs=[_HBM] * (2 * n) + [_SEM] * (2 * N_PEERS) + [pl.BlockSpec(memory_space=pl.ANY)],
        out_specs=tuple([_HBM] * (2 * n)), input_output_aliases={k: k for k in range(2 * n)},
        compiler_params=pltpu.CompilerParams(has_side_effects=_DATAFLOW))(
            *srcs, *lands, *send_sems, *recv_sems, after)
    return list(out[:n]), list(out[n:])


def _fill_own_slot(src, land, gather):
    me = _my_id()
    own = src[None] if gather else lax.dynamic_index_in_dim(src, me, 0, keepdims=True)
    return lax.dynamic_update_slice_in_dim(land, own, me, 0)


ADAMW_BLOCK_ELEMS = 128 * 1024


def adamw(name, parts, w, m, v, token=None):
    R, C = w.shape
    tr = _pick(R, [t for t in (512, 256, 128, 64, 32, 16, 8) if t * C <= ADAMW_BLOCK_ELEMS])
    c1 = 1.0 - ADAM_B1 ** ADAM_STEP
    c2 = 1.0 - ADAM_B2 ** ADAM_STEP
    extra, extra_specs = _after(token)

    def body(p_ref, w_ref, m_ref, v_ref, *rest):
        g_ref, d_ref, nm_ref, nv_ref = rest[-4:]
        g = p_ref[0].astype(F32)
        for s in range(1, N_DEV):
            g = g + p_ref[s].astype(F32)
        g_ref[...] = g
        m_ = ADAM_B1 * m_ref[...] + (1.0 - ADAM_B1) * g
        v_ = ADAM_B2 * v_ref[...] + (1.0 - ADAM_B2) * (g * g)
        nm_ref[...] = m_
        nv_ref[...] = v_
        d_ref[...] = -ADAM_LR * ((m_ / c1) / (jnp.sqrt(v_ / c2) + ADAM_EPS) + ADAM_WD * w_ref[...])

    row = pl.BlockSpec((tr, C), lambda i: (i, 0))
    return pl.pallas_call(
        body, name=name, grid=(R // tr,),
        in_specs=[pl.BlockSpec((N_DEV, tr, C), lambda i: (0, i, 0)), row, row, row] + extra_specs,
        out_specs=[row] * 4, out_shape=[_sds((R, C), F32)] * 4, compiler_params=_cp())(parts, w, m, v, *extra)


WEIGHTS = ['ev_norm', 'ev_w_in', 'ev_pool_w', 'ev_pool_scale', 'ev_q_norm', 'ev_w_q_up', 'ev_kv_norm', 'ev_w_kv_up',
           'ev_w_out', 'od_norm', 'od_w_in', 'od_conv_w', 'od_conv_b', 'od_w_rgate', 'od_b_rgate', 'od_w_igate',
           'od_b_igate', 'od_lambda', 'od_w_out', 'xa_norm_x', 'xa_norm_mem', 'xa_w_q', 'xa_w_kv', 'xa_w_o',
           'ffn_norm', 'ffn_w_gate_up', 'ffn_w_down', 'final_norm']
SHARD_AXIS = {'ev_w_in': 1, 'ev_w_q_up': 2, 'ev_w_kv_up': 2, 'ev_w_out': 1, 'od_norm': 1, 'od_w_in': 2,
              'od_conv_w': 2, 'od_conv_b': 1, 'od_w_rgate': 2, 'od_b_rgate': 1, 'od_w_igate': 2, 'od_b_igate': 1,
              'od_lambda': 1, 'od_w_out': 1, 'xa_w_q': 1, 'xa_w_kv': 2, 'xa_w_o': 1, 'ffn_w_gate_up': 2,
              'ffn_w_down': 1}
SMALL_F32 = ('od_norm', 'od_conv_w', 'od_conv_b', 'od_b_rgate', 'od_b_igate', 'od_lambda')
STACKED = ('ffn_w_gate_up', 'ffn_w_down')
SHARDED = [n for n in WEIGHTS if n in SHARD_AXIS]
REPLICATED = [n for n in WEIGHTS if n not in SHARD_AXIS]
ROW_ALIGN = 512


def _pack(flats, dtype):
    v = jnp.concatenate([f.reshape(-1).astype(dtype) for f in flats])
    pad = (-v.shape[0]) % (ROW_ALIGN * LANES)
    return jnp.pad(v, (0, pad)).reshape(-1, LANES)


def _rows8(n_elems):
    return -(-n_elems // (8 * LANES)) * 8


def _pack_rows(arrays, lead=False):
    out = []
    for a in arrays:
        r = a.reshape((N_DEV, -1, LANES) if lead else (-1, LANES))
        pad = _rows8(r.shape[-2] * LANES) - r.shape[-2]
        out.append(jnp.pad(r, [(0, 0)] * (r.ndim - 2) + [(0, pad), (0, 0)]))
    return jnp.concatenate(out, axis=-2)


def _unpack_rows(buf, shapes, lead=False):
    out, off = [], 0
    for s in shapes:
        n = 1
        for d in s:
            n *= d
        rows = buf[..., off:off + n // LANES, :]
        out.append(rows.reshape(((N_DEV,) if lead else ()) + tuple(s)))
        off += _rows8(n)
    return out


def _unpack(flat, shapes):
    out, off = [], 0
    v = flat.reshape(-1)
    for s in shapes:
        n = 1
        for d in s:
            n *= d
        out.append(v[off:off + n].reshape(s))
        off += n
    return out


def _to_full(stacked, axis):
    v = jnp.moveaxis(stacked, 0, axis)
    s = v.shape
    return v.reshape(s[:axis] + (s[axis] * s[axis + 1],) + s[axis + 2:])


def _to_shards(full, axis):
    s = full.shape
    v = full.reshape(s[:axis] + (N_DEV, s[axis] // N_DEV) + s[axis + 1:])
    return jnp.moveaxis(v, axis, 0)


def _pad_heads(w, nh, dh, lead):
    s = w.shape
    v = w.reshape(s[:-1] + (nh, dh))
    v = jnp.pad(v, [(0, 0)] * (len(s) - 1) + [(0, 0), (lead, LANES - dh - lead)])
    return v.reshape(s[:-1] + (nh * LANES,))


def _unpad_heads(w, nh, dh, lead):
    s = w.shape
    return w.reshape(s[:-1] + (nh, LANES))[..., lead:lead + dh].reshape(s[:-1] + (nh * dh,))


def _rope_tables(positions):
    inv_freq = 10000.0 ** (-jnp.arange(0, 32, 2, dtype=F32) / 32)
    ang = positions.astype(F32)[:, None] * inv_freq
    cos, sin = jnp.tile(jnp.cos(ang), (1, LANES // 16)), jnp.tile(jnp.sin(ang), (1, LANES // 16))
    lane = lax.broadcasted_iota(jnp.int32, cos.shape, 1)
    c = jnp.where((lane >= 64) & (lane < 96), cos, 1.0)
    a = jnp.where((lane >= 80) & (lane < 96), sin, 0.0)
    b = jnp.where((lane >= 64) & (lane < 80), -sin, 0.0)
    return c, a, b


def _t(w):
    return jnp.swapaxes(w, -1, -2)


def device_step(x, mem, positions, target, W, fwd_token=None, late_weights=None, ship_grads=None,
                first_weights=None):
    G = {}
    tabs = _rope_tables(positions)
    keep = (positions != 0).astype(F32)[:, None]
    row = lambda v: v.reshape(1, -1)
    if first_weights is not None:
        W = {**W, **first_weights(tabs[0])}

    w_in = W['ev_w_in'][0]
    ev_win = jnp.concatenate([w_in[:, :896], _pad_heads(w_in[:, 896:], 1, 32, 64)], axis=1)
    ev_wq = _pad_heads(W['ev_w_q_up'][0], MLA_HEADS, QK_DIM, 0)
    kvw = W['ev_w_kv_up'][0].reshape(KV_LORA, MLA_HEADS, 128)
    ev_wk = _pad_heads(kvw[:, :, :64].reshape(KV_LORA, 512), MLA_HEADS, 64, 0)
    ev_wv = _pad_heads(kvw[:, :, 64:].reshape(KV_LORA, 512), MLA_HEADS, 64, 0)
    ev_wo_pool = W['ev_w_out'][0][:POOL_DIM]
    ev_wo_att = _t(_pad_heads(_t(W['ev_w_out'][0][POOL_DIM:]), MLA_HEADS, 64, 0))
    pw = W['ev_pool_w'][0].astype(BF16)
    ev_g, ps, qg, kvg = row(W['ev_norm'][0]), row(W['ev_pool_scale'][0]), row(W['ev_q_norm'][0]), row(W['ev_kv_norm'][0])

    z0, qp, kp, vp, ypool = even_pre(x, tabs, ev_g, ev_win, pw, ps, qg, ev_wq, kvg, ev_wk, ev_wv)
    o_att, lse = attn_fwd(qp, kp, vp, fwd_token)
    if late_weights is not None:
        W = {**W, **late_weights(lse)}
    x1 = even_post(x, ypool, o_att, ev_wo_pool, ev_wo_att)

    def xa_ffn_fwd(xin, l, head=()):
        mn, km, vm = mem_kv(mem, row(W['xa_norm_mem'][l]), W['xa_w_kv'][l])
        xm = xattn_fwd(xin, row(W['xa_norm_x'][l]), W['xa_w_q'][l], km, vm, W['xa_w_o'][l])
        *xo, hf, gu = (ffn_fwd_loss if head else ffn_fwd)(
            xm, row(W['ffn_norm'][l]), W['ffn_w_gate_up'], l, W['ffn_w_down'][:, l].reshape(FF_HALF, FF_CHUNK, D),
            *head)
        return xm, (xo if head else xo[0]), (mn, km, vm, hf, gu)

    x2, x3, memkv0 = xa_ffn_fwd(x1, 0)

    od_g, lam = row(W['od_norm'][0]), row(W['od_lambda'][0])
    cw, cb = W['od_conv_w'][0], row(W['od_conv_b'][0])
    wr, wi = W['od_w_rgate'][0], W['od_w_igate'][0]
    br, bi = row(W['od_b_rgate'][0]), row(W['od_b_igate'][0])
    z1, a_t, b_t, xb1, r1, ig1 = odd_pre(x3, keep, od_g, W['od_w_in'][0], cw, cb, wr, br, wi, bi, lam)
    hseq = lru_scan(a_t, b_t)
    x4 = odd_post(x3, z1, hseq, W['od_w_out'][0])
    x5, (dx, g_final, loss), memkv1 = xa_ffn_fwd(x4, 1, (target, row(W['final_norm'])))
    G['final_norm'] = g_final.reshape(D)

    gnx, gnm, gwq, gwkv, gwo, gfn, gwgu, gwd = ([None, None] for _ in range(8))

    def xa_ffn_bwd(dy, xin, xm, memkv, l):
        mn, km, vm, hf, gu = memkv
        fg = row(W['ffn_norm'][l])
        dxm, dfg, act, dgu = ffn_bwd(xm, dy, gu, fg, W['ffn_w_gate_up'], l,
                                     W['ffn_w_down'][:, l].reshape(FF_HALF, FF_CHUNK, D))
        gwd[l] = matmul_tn("ffn_dwd", act, dy).reshape(N_DEV, D_FF // N_DEV, D)
        gwgu[l] = matmul_tn("ffn_dwgu", hf, dgu)
        gfn[l] = dfg[0]
        dxin, o, dq, hx, dgx, dk, dv = xattn_bwd(xin, dxm, row(W['xa_norm_x'][l]), W['xa_w_q'][l], km, vm,
                                                  W['xa_w_o'][l])
        gnx[l] = dgx[0]
        gwo[l] = matmul_tn("xa_dwo", o, dxm)
        gwq[l] = matmul_tn("xa_dwq", hx, dq)
        dkv, dgm = mem_bwd(mem, row(W['xa_norm_mem'][l]), dk, dv, W['xa_w_kv'][l])
        gnm[l] = dgm[0]
        gwkv[l] = matmul_tn("xa_dwkv", mn, dkv)
        return dxin

    dx4 = xa_ffn_bwd(dx, x4, x5, memkv1, 1)

    dgate, dhs, g_od_wout = odd_post_bwd(dx4, z1, hseq, W['od_w_out'][0])
    G['od_w_out'] = g_od_wout[None]
    lam_grad = lru_scan(a_t, dhs, reverse=True)
    dxb, dcb, dbr, dbi, dlam, dwr, dwi = odd_gates_bwd(xb1, r1, ig1, lam_grad, hseq, keep, wr, wi, lam)
    dx3, dcw, dg_od, g_od_win = odd_pre_bwd(x3, dx4, z1, dxb, dgate, od_g, cw, W['od_w_in'][0])
    G['od_w_in'] = g_od_win[None]
    G['od_norm'], G['od_conv_w'], G['od_conv_b'] = dg_od, dcw[None], dcb
    G['od_w_rgate'], G['od_b_rgate'], G['od_w_igate'], G['od_b_igate'], G['od_lambda'] = (
        dwr[None], dbr, dwi[None], dbi, dlam)

    dx1 = xa_ffn_bwd(dx3, x1, x2, memkv0, 0)
    G['xa_norm_x'], G['xa_norm_mem'], G['ffn_norm'] = jnp.stack(gnx), jnp.stack(gnm), jnp.stack(gfn)
    G['xa_w_q'], G['xa_w_kv'], G['xa_w_o'] = jnp.stack(gwq), jnp.stack(gwkv), jnp.stack(gwo)
    G['ffn_w_gate_up'], G['ffn_w_down'] = jnp.stack(gwgu, axis=1), jnp.stack(gwd, axis=1)
    bwd_token = ship_grads(G) if ship_grads is not None else None

    dyp, do_att, delta, g_wo_pool, g_wo_att = even_post_bwd(dx1, ypool, o_att, ev_wo_pool, ev_wo_att)
    G['ev_w_out'] = jnp.concatenate([g_wo_pool, _t(_unpad_heads(_t(g_wo_att), MLA_HEADS, 64, 0))], axis=0)[None]
    dq, dk, dv = attn_bwd(qp, kp, vp, do_att, lse, delta, bwd_token)
    (grad_x, dg_ev, dpw, dps, dqg, dwq, dkvg, dwk, dwv, g_win) = even_pre_bwd(
        x, dx1, z0, dq, dk, dv, dyp, tabs, ev_g, ev_win, pw, ps, qg, ev_wq, kvg, ev_wk, ev_wv)
    G['ev_w_in'] = jnp.concatenate([g_win[:, :896], _unpad_heads(g_win[:, 896:], 1, 32, 64)], axis=1)[None]
    G['ev_norm'], G['ev_pool_w'], G['ev_pool_scale'], G['ev_q_norm'], G['ev_kv_norm'] = (
        dg_ev, dpw[None], dps, dqg, dkvg)
    G['ev_w_q_up'] = _unpad_heads(dwq, MLA_HEADS, QK_DIM, 0)[None]
    gk = _unpad_heads(dwk, MLA_HEADS, 64, 0).reshape(KV_LORA, MLA_HEADS, 64)
    gv = _unpad_heads(dwv, MLA_HEADS, 64, 0).reshape(KV_LORA, MLA_HEADS, 64)
    G['ev_w_kv_up'] = jnp.concatenate([gk, gv], axis=2).reshape(1, KV_LORA, MLA_HEADS * 128)
    return loss[0, 0], grad_x, G


def kernel(x, mem, positions, ev_norm, ev_w_in, ev_pool_w, ev_pool_scale, ev_q_norm, ev_w_q_up, ev_kv_norm, ev_w_kv_up, ev_w_out, od_norm, od_w_in, od_conv_w, od_conv_b, od_w_rgate, od_b_rgate, od_w_igate, od_b_igate, od_lambda, od_w_out, xa_norm_x, xa_norm_mem, xa_w_q, xa_w_kv, xa_w_o, ffn_norm, ffn_w_gate_up, ffn_w_down, final_norm, loss_target, m_ev_norm, m_ev_w_in, m_ev_pool_w, m_ev_pool_scale, m_ev_q_norm, m_ev_w_q_up, m_ev_kv_norm, m_ev_w_kv_up, m_ev_w_out, m_od_norm, m_od_w_in, m_od_conv_w, m_od_conv_b, m_od_w_rgate, m_od_b_rgate, m_od_w_igate, m_od_b_igate, m_od_lambda, m_od_w_out, m_xa_norm_x, m_xa_norm_mem, m_xa_w_q, m_xa_w_kv, m_xa_w_o, m_ffn_norm, m_ffn_w_gate_up, m_ffn_w_down, m_final_norm, v_ev_norm, v_ev_w_in, v_ev_pool_w, v_ev_pool_scale, v_ev_q_norm, v_ev_w_q_up, v_ev_kv_norm, v_ev_w_kv_up, v_ev_w_out, v_od_norm, v_od_w_in, v_od_conv_w, v_od_conv_b, v_od_w_rgate, v_od_b_rgate, v_od_w_igate, v_od_b_igate, v_od_lambda, v_od_w_out, v_xa_norm_x, v_xa_norm_mem, v_xa_w_q, v_xa_w_kv, v_xa_w_o, v_ffn_norm, v_ffn_w_gate_up, v_ffn_w_down, v_final_norm):
    args = dict(locals())
    w = {n: args[n] for n in WEIGHTS}
    m = {n: args['m_' + n] for n in WEIGHTS}
    v = {n: args['v_' + n] for n in WEIGHTS}
    big = [n for n in SHARDED if n not in SMALL_F32]
    small = [n for n in SHARDED if n in SMALL_F32]

    small_shapes = [w[n].shape for n in small]
    first = [n for n in big if n.startswith('ev_')]
    late = [n for n in big if n not in first]

    def full(n, st):
        return st if n in STACKED else _to_full(st, SHARD_AXIS[n])

    W = {n: w[n] for n in REPLICATED}
    gather_first = split_start("first_start", [w[n].astype(BF16) for n in first], True)

    def first_weights(after):
        srcs, lands = split_wait("first_wait", gather_first, after, True)
        return {n: full(n, _fill_own_slot(s, l, True)) for n, s, l in zip(first, srcs, lands)}

    gather = split_start("gather_start", [w[n].astype(BF16) for n in late] + [_pack_rows([w[n] for n in small])], True)

    def late_weights(after):
        srcs, lands = split_wait("gather_wait", gather, after, True)
        lands = [_fill_own_slot(s, l, True) for s, l in zip(srcs, lands)]
        out = {n: full(n, st) for n, st in zip(late, lands)}
        out.update((n, _to_full(st, SHARD_AXIS[n])) for n, st in zip(small, _unpack_rows(lands[-1], small_shapes, True)))
        return out

    def shards(G, n):
        return G[n] if n in STACKED else _to_shards(G[n], SHARD_AXIS[n])

    shipped = []

    def ship_grads(G):
        shipped.append(split_start("exchange_start", [shards(G, n).astype(BF16) for n in late] +
                                   [_pack_rows([shards(G, n) for n in small], lead=True)], False))
        return shipped[0][-1]

    loss, grad_x, G = device_step(x[0], mem[0], positions[0], loss_target[0], W, gather[-1], late_weights, ship_grads,
                                  first_weights)
    outs = [{}, {}, {}, {}]

    last_flags = (False,) * len(first) + (True,)
    last = split_start("last_start", [shards(G, n).astype(BF16) for n in first] + [_pack(
        [G[n] for n in REPLICATED] + [jnp.broadcast_to(loss, (LANES,))], F32)], last_flags)

    two_d = lambda a: a.reshape(-1, a.shape[-1])

    def update(names, parts):
        prev = None
        for n, p in zip(names, parts):
            res = adamw("adamw_" + n, p.reshape((N_DEV,) + two_d(w[n]).shape), two_d(w[n]), two_d(m[n]),
                        two_d(v[n]), prev)
            prev = res[0]
            for k in range(4):
                outs[k][n] = res[k].reshape(w[n].shape)
        return prev

    srcs, lands = split_wait("exchange_wait", shipped[0], last[-1], False)
    late_parts = [_fill_own_slot(s, l, False) for s, l in zip(srcs, lands)]
    after = update(late, late_parts)
    res = adamw("adamw_small", late_parts[-1], *[_pack_rows([d[n] for n in small]) for d in (w, m, v)])
    for k in range(4):
        outs[k].update(zip(small, _unpack_rows(res[k], small_shapes)))

    srcs, lands = split_wait("last_wait", last, after, last_flags)
    *first_parts, rep_parts = [_fill_own_slot(s, l, f) for s, l, f in zip(srcs, lands, last_flags)]
    update(first, first_parts)

    rep_shapes = [w[n].shape for n in REPLICATED] + [(LANES,)]
    zero = jnp.zeros((LANES,), F32)
    rep = adamw("adamw_rep", rep_parts, *[_pack([d[n] for n in REPLICATED] + [zero], F32) for d in (w, m, v)])
    for k in range(4):
        outs[k].update(zip(REPLICATED + ['loss'], _unpack(rep[k], rep_shapes)))
    loss = outs[0]['loss'][0]

    return (loss, grad_x[None], *[outs[0][n] for n in WEIGHTS], *[outs[1][n] for n in WEIGHTS],
            *[outs[2][n] for n in WEIGHTS], *[outs[3][n] for n in WEIGHTS])
```

```python
import functools

import jax
import jax.numpy as jnp
from jax import lax
from jax.experimental import pallas as pl
from jax.experimental.pallas import tpu as pltpu

F32, BF16 = jnp.float32, jnp.bfloat16
N_DEV = 8
D = 1024
POOL_DIM = 512
POOL_WINDOWS = (2, 4, 8, 16)
MLA_HEADS = 8
QK_DIM = 96
Q_LORA, KV_LORA = 256, 128
LRU_HEADS, LRU_HEAD_DIM = 4, 256
LRU_C = 8.0
MEM_HEADS, MEM_HEAD_DIM = 4, 256
D_FF = 2816
RMS_EPS = 1e-6
ADAM_LR, ADAM_B1, ADAM_B2, ADAM_EPS, ADAM_WD, ADAM_STEP = 0.001, 0.9, 0.999, 1e-08, 0.01, 10
LANES = 128
POOL_HALO = 16
CONV_HALO = 8
VMEM_LIMIT = 60000 * 1024


def _cp():
    return pltpu.CompilerParams(dimension_semantics=("arbitrary",), vmem_limit_bytes=VMEM_LIMIT)


def _cp2():
    return pltpu.CompilerParams(dimension_semantics=("arbitrary", "arbitrary"), vmem_limit_bytes=VMEM_LIMIT)


def _row(ts, c, col=0):
    return pl.BlockSpec((ts, c), lambda i: (i, col))


def _prev(hr, c, ts, col=0):
    r = ts // hr
    return pl.BlockSpec((hr, c), lambda i: (jnp.maximum(i * r - 1, 0), col))


def _next(hr, c, ts, n, col=0):
    r = ts // hr
    return pl.BlockSpec((hr, c), lambda i: (jnp.minimum((i + 1) * r, n * r - 1), col))


def _const(shape):
    nd = len(shape)
    return pl.BlockSpec(tuple(shape), lambda i: (0,) * nd, pipeline_mode=pl.Buffered(1))


def _acc(shape):
    nd = len(shape)
    return pl.BlockSpec(tuple(shape), lambda i: (0,) * nd)


def _sds(shape, dt):
    return jax.ShapeDtypeStruct(tuple(shape), dt)


def _dot(a, b):
    return jnp.dot(a.astype(BF16), b.astype(BF16), preferred_element_type=F32)


def _dot_nt(a, b):
    return lax.dot_general(a.astype(BF16), b.astype(BF16), (((1,), (1,)), ((), ())), preferred_element_type=F32)


def _dot_tn(a, b):
    return lax.dot_general(a.astype(BF16), b.astype(BF16), (((0,), (0,)), ((), ())), preferred_element_type=F32)


def _rms(x, g):
    rstd = lax.rsqrt(jnp.mean(x * x, axis=-1, keepdims=True) + RMS_EPS)
    return x * rstd * g, rstd


def _rms_bwd(x, g, rstd, dy):
    xn = x * rstd
    dyg = dy * g
    dx = rstd * (dyg - xn * jnp.mean(dyg * xn, axis=-1, keepdims=True))
    return dx, dy * xn


def _rowsum(v):
    return jnp.sum(v, axis=0, keepdims=True)


def _roll(v, s, axis):
    n = v.shape[axis]
    return pltpu.roll(v, s % n, axis)


def _rope(t, c, a, b):
    k = t.shape[1] // LANES
    if k > 1:
        c, a, b = (jnp.tile(v, (1, k)) for v in (c, a, b))
    return t * c + _roll(t, 16, 1) * a + _roll(t, -16, 1) * b


def _rope_bwd(d, c, a, b):
    k = d.shape[1] // LANES
    if k > 1:
        c, a, b = (jnp.tile(v, (1, k)) for v in (c, a, b))
    return d * c + _roll(d * a, -16, 1) + _roll(d * b, 16, 1)


def _gelu(x):
    c = 0.7978845608028654
    t = jnp.tanh(c * (x + 0.044715 * x * x * x))
    return 0.5 * x * (1.0 + t), t


def _gelu_grad(x, t):
    c = 0.7978845608028654
    return 0.5 * (1.0 + t) + 0.5 * x * (1.0 - t * t) * c * (1.0 + 3.0 * 0.044715 * x * x)


def _blockdot(v, w_ref, nblk, width):
    return jnp.concatenate(
        [_dot(v[:, j * width:(j + 1) * width], w_ref[j]) for j in range(nblk)], axis=1)


def _pool_cnt(row0, rows):
    t = row0 + lax.broadcasted_iota(jnp.int32, (rows, POOL_DIM), 0)
    w = jnp.left_shift(2, lax.broadcasted_iota(jnp.int32, (rows, POOL_DIM), 1) // LANES)
    return jnp.minimum(t + 1, w).astype(F32)


def _pool_windows(ext, sign):
    s2 = ext + _roll(ext, sign * 1, 0)
    t = s2[:, LANES:]
    s4 = t + _roll(t, sign * 2, 0)
    t = s4[:, LANES:]
    s8 = t + _roll(t, sign * 4, 0)
    t = s8[:, LANES:]
    s16 = t + _roll(t, sign * 8, 0)
    return jnp.concatenate([s2[:, :LANES], s4[:, :LANES], s8[:, :LANES], s16], axis=1)


def _pooled(uprev, u, row0):
    ts = u.shape[0]
    ext = jnp.concatenate([uprev, u], axis=0)
    sums = _pool_windows(ext, 1)[POOL_HALO:]
    return sums / _pool_cnt(row0, ts) - u


def _expm1(x):
    return jnp.where(jnp.abs(x) < 0.01, x * (1.0 + 0.5 * x * (1.0 + x * (1.0 / 3.0))), jnp.exp(x) - 1.0)


def _softplus(z):
    return jnp.maximum(z, 0.0) + jnp.log1p(jnp.exp(-jnp.abs(z)))


def _tile_rows(s, want):
    while s % want:
        want //= 2
    return want


def even_pre(x, tabs, g, win, pw, pscale, qg, wq, kvg, wk, wv):
    S = x.shape[0]
    ts = _tile_rows(S, 512)

    def body(x_ref, xp_ref, c_ref, a_ref, b_ref, g_ref, win_ref, pw_ref, ps_ref, qg_ref, wq_ref, kvg_ref,
             wk_ref, wv_ref, z_ref, q_ref, k_ref, v_ref, yp_ref):
        i = pl.program_id(0)
        h, _ = _rms(x_ref[...], g_ref[...])
        z = _dot(h, win_ref[...])
        z_ref[...] = z
        hp, _ = _rms(xp_ref[...], g_ref[...])
        uprev = _dot(hp, win_ref[:, :POOL_DIM]) * (i > 0).astype(F32)
        u = z[:, :POOL_DIM]
        pooled = _pooled(uprev, u, i * ts)
        yp_ref[...] = (_blockdot(pooled, pw_ref, 4, LANES) * ps_ref[...]).astype(BF16)
        c, a, b = c_ref[...], a_ref[...], b_ref[...]
        cqn, _ = _rms(z[:, 512:768], qg_ref[...])
        q_ref[...] = (_rope(_dot(cqn, wq_ref[...]), c, a, b) * (ATTN_SCALE * LOG2_E)).astype(BF16)
        ckvn, _ = _rms(z[:, 768:896], kvg_ref[...])
        krr = _rope(z[:, 896:1024], c, a, b)
        k_ref[...] = (_dot(ckvn, wk_ref[...]) + jnp.tile(krr, (1, MLA_HEADS))).astype(BF16)
        lane = lax.broadcasted_iota(jnp.int32, (ts, D), 1) % LANES
        v_ref[...] = jnp.where(lane == ONES_LANE, 1.0, _dot(ckvn, wv_ref[...])).astype(BF16)

    ins = [x, x, *tabs, g, win, pw, pscale, qg, wq, kvg, wk, wv]
    in_specs = [_row(ts, D), _prev(POOL_HALO, D, ts), _row(ts, LANES), _row(ts, LANES), _row(ts, LANES)]
    in_specs += [_const(v.shape) for v in ins[5:]]
    return pl.pallas_call(
        body, name="even_pre", grid=(S // ts,), in_specs=in_specs,
        out_specs=[_row(ts, D)] * 4 + [_row(ts, POOL_DIM)],
        out_shape=[_sds((S, D), F32)] + [_sds((S, D), BF16)] * 3 + [_sds((S, POOL_DIM), BF16)],
        compiler_params=_cp())(*ins)


ATTN_SCALE = QK_DIM ** -0.5
LOG2_E = 1.4426950408889634
LN_2 = 0.6931471805599453
ONES_LANE = 64


def _exp2(x):
    return jnp.exp2(x)


def _pair_loop(lo, hi, step, init, unrolls=(2, 1)):
    carry = init
    for unroll in unrolls:
        groups = (hi - lo) // unroll

        def group(j, c, lo=lo, unroll=unroll):
            for u in range(unroll):
                c = step(lo + unroll * j + u, c)
            return c

        carry = lax.fori_loop(0, groups, group, carry)
        lo = lo + unroll * groups
    return carry


def _as_row(col):
    return jnp.transpose(jnp.broadcast_to(col, (col.shape[0], LANES)))[0:1, :]


def _after(token):
    return ([], []) if token is None else ([token], [pl.BlockSpec(memory_space=pl.ANY)])


def attn_fwd(qp, kp, vp, token=None):
    S = qp.shape[0]
    tq = _tile_rows(S, 512)
    extra, extra_specs = _after(token)

    def body(q_ref, k_ref, v_ref, *rest):
        o_ref, lse_ref = rest[-2:]
        qi = pl.program_id(1)
        q = q_ref[...]

        def block(ki, carry, masked):
            m, acc = carry
            off = pl.multiple_of(ki * tq, tq)
            s = _dot_nt(q, k_ref[pl.ds(off, tq), :])
            if masked:
                row = lax.broadcasted_iota(jnp.int32, (tq, tq), 0)
                col = lax.broadcasted_iota(jnp.int32, (tq, tq), 1)
                s = jnp.where(col <= row, s, -1e30)
            m_new = jnp.maximum(m, jnp.max(s, axis=1, keepdims=True))
            acc = _exp2(m - m_new) * acc + _dot(_exp2(s - m_new), v_ref[pl.ds(off, tq), :])
            return m_new, acc

        init = (jnp.full((tq, 1), -1e30, F32), jnp.zeros((tq, LANES), F32))
        carry = _pair_loop(0, qi, lambda ki, c: block(ki, c, False), init, unrolls=(8, 4, 2, 1))
        m, acc = block(qi, carry, True)
        l = acc[:, ONES_LANE:ONES_LANE + 1]
        o_ref[...] = acc / l
        lse_ref[...] = _as_row(m + jnp.log(l) * LOG2_E)

    blk = pl.BlockSpec((tq, LANES), lambda h, i: (i, h))
    full = pl.BlockSpec((S, LANES), lambda h, i: (0, h))
    return pl.pallas_call(
        body, name="attn_fwd", grid=(MLA_HEADS, S // tq), in_specs=[blk, full, full] + extra_specs,
        out_specs=[blk, pl.BlockSpec((None, None, 1, tq), lambda h, i: (h, i, 0, 0))],
        out_shape=[_sds((S, D), F32), _sds((MLA_HEADS, S // tq, 1, tq), F32)], compiler_params=_cp2())(
            qp, kp, vp, *extra)


def even_post(x, ypool, o, wo_pool, wo_att):
    S = x.shape[0]
    ts = _tile_rows(S, 512)

    def body(x_ref, yp_ref, o_ref, wp_ref, wa_ref, out_ref):
        out_ref[...] = x_ref[...] + _dot(yp_ref[...], wp_ref[...]) + _dot(o_ref[...], wa_ref[...])

    return pl.pallas_call(
        body, name="even_post", grid=(S // ts,),
        in_specs=[_row(ts, D), _row(ts, POOL_DIM), _row(ts, D), _const(wo_pool.shape), _const(wo_att.shape)],
        out_specs=_row(ts, D), out_shape=_sds((S, D), F32), compiler_params=_cp())(x, ypool, o, wo_pool, wo_att)


def mem_kv(mem, g, wkv):
    M = mem.shape[0]

    def body(mem_ref, g_ref, w_ref, mn_ref, k_ref, v_ref):
        mn, _ = _rms(mem_ref[...], g_ref[...])
        mn_ref[...] = mn.astype(BF16)
        k_ref[...] = _dot(mn, w_ref[:, :D]).astype(BF16)
        v_ref[...] = _dot(mn, w_ref[:, D:]).astype(BF16)

    return pl.pallas_call(
        body, name="mem_kv", grid=(1,), in_specs=[_acc(mem.shape), _acc(g.shape), _acc(wkv.shape)],
        out_specs=[_acc((M, D))] * 3, out_shape=[_sds((M, D), BF16)] * 3, compiler_params=_cp())(mem, g, wkv)


def _xattn_heads(hx, wq_ref, k_ref, v_ref):
    q = _dot(hx, wq_ref[...])
    scale = MEM_HEAD_DIM ** -0.5
    ps, os_ = [], []
    for h in range(MEM_HEADS):
        sl = slice(h * MEM_HEAD_DIM, (h + 1) * MEM_HEAD_DIM)
        s = _dot_nt(q[:, sl], k_ref[:, sl]) * scale
        e = jnp.exp(s - jnp.max(s, axis=1, keepdims=True))
        p = e / jnp.sum(e, axis=1, keepdims=True)
        ps.append(p)
        os_.append(_dot(p, v_ref[:, sl]))
    return q, ps, jnp.concatenate(os_, axis=1)


def xattn_fwd(x, g, wq, kmem, vmem, wo):
    S = x.shape[0]
    ts = _tile_rows(S, 512)

    def body(x_ref, g_ref, wq_ref, k_ref, v_ref, wo_ref, out_ref):
        x_ = x_ref[...]
        hx, _ = _rms(x_, g_ref[...])
        _, _, o = _xattn_heads(hx, wq_ref, k_ref, v_ref)
        out_ref[...] = x_ + _dot(o, wo_ref[...])

    ins = [x, g, wq, kmem, vmem, wo]
    return pl.pallas_call(
        body, name="xattn_fwd", grid=(S // ts,), in_specs=[_row(ts, D)] + [_const(v.shape) for v in ins[1:]],
        out_specs=_row(ts, D), out_shape=_sds((S, D), F32), compiler_params=_cp())(*ins)


def xattn_bwd(x, dy, g, wq, kmem, vmem, wo):
    S = x.shape[0]
    M = kmem.shape[0]
    ts = _tile_rows(S, 512)
    scale = MEM_HEAD_DIM ** -0.5

    def body(x_ref, dy_ref, g_ref, wq_ref, k_ref, v_ref, wo_ref,
             dx_ref, o_ref, dq_ref, hx_ref, dg_ref, dk_ref, dv_ref):
        i = pl.program_id(0)

        @pl.when(i == 0)
        def _():
            dg_ref[...] = jnp.zeros_like(dg_ref)
            dk_ref[...] = jnp.zeros_like(dk_ref)
            dv_ref[...] = jnp.zeros_like(dv_ref)

        x_, dy_ = x_ref[...], dy_ref[...]
        hx, rstd = _rms(x_, g_ref[...])
        q, ps, o = _xattn_heads(hx, wq_ref, k_ref, v_ref)
        hx_ref[...] = hx.astype(BF16)
        o_ref[...] = o.astype(BF16)
        do = _dot_nt(dy_, wo_ref[...])
        dqs = []
        for h in range(MEM_HEADS):
            sl = slice(h * MEM_HEAD_DIM, (h + 1) * MEM_HEAD_DIM)
            p, do_h = ps[h], do[:, sl]
            dp = _dot_nt(do_h, v_ref[:, sl])
            ds = p * (dp - jnp.sum(p * dp, axis=1, keepdims=True)) * scale
            dqs.append(_dot(ds, k_ref[:, sl]))
            dk_ref[:, sl] += _dot_tn(ds, q[:, sl])
            dv_ref[:, sl] += _dot_tn(p, do_h)
        dq = jnp.concatenate(dqs, axis=1).astype(BF16)
        dq_ref[...] = dq
        dxn, dgr = _rms_bwd(x_, g_ref[...], rstd, _dot_nt(dq, wq_ref[...]))
        dx_ref[...] = dy_ + dxn
        dg_ref[...] += _rowsum(dgr)

    ins = [x, dy, g, wq, kmem, vmem, wo]
    return pl.pallas_call(
        body, name="xattn_bwd", grid=(S // ts,),
        in_specs=[_row(ts, D), _row(ts, D)] + [_const(v.shape) for v in ins[2:]],
        out_specs=[_row(ts, D)] * 4 + [_acc((1, D)), _acc((M, D)), _acc((M, D))],
        out_shape=[_sds((S, D), F32)] + [_sds((S, D), BF16)] * 3 + [_sds((1, D), F32), _sds((M, D), F32),
                                                                    _sds((M, D), F32)],
        compiler_params=_cp())(*ins)


def mem_bwd(mem, g, dk, dv, wkv):
    M = mem.shape[0]

    def body(mem_ref, g_ref, dk_ref, dv_ref, w_ref, dkv_ref, dg_ref):
        dkv = jnp.concatenate([dk_ref[...], dv_ref[...]], axis=1)
        dkv_ref[...] = dkv.astype(BF16)
        _, rstd = _rms(mem_ref[...], g_ref[...])
        dg_ref[...] = _rowsum(_dot_nt(dkv, w_ref[...]) * (mem_ref[...] * rstd))

    ins = [mem, g, dk, dv, wkv]
    return pl.pallas_call(
        body, name="mem_bwd", grid=(1,), in_specs=[_acc(v.shape) for v in ins],
        out_specs=[_acc((M, 2 * D)), _acc((1, D))], out_shape=[_sds((M, 2 * D), BF16), _sds((1, D), F32)],
        compiler_params=_cp())(*ins)


FF_CHUNK = 2 * D_FF // N_DEV
FF_HALF = N_DEV // 2


def _layer_of(w, layer):
    return pl.BlockSpec((N_DEV, None) + w.shape[2:], lambda i: (0, layer, 0, 0), pipeline_mode=pl.Buffered(1))


def _ff_chunks(c, ts):
    return pl.BlockSpec((c, ts, FF_CHUNK), lambda i: (0, i, 0))


def _ffn(x_, g_ref, wgu_ref, wd_ref, hf_ref, gu_ref):
    hf = _rms(x_, g_ref[...])[0].astype(BF16)
    hf_ref[...] = hf
    out = x_
    for j in range(FF_HALF):
        gg, uu = _dot(hf, wgu_ref[j]), _dot(hf, wgu_ref[j + FF_HALF])
        gu_ref[j] = gg.astype(BF16)
        gu_ref[j + FF_HALF] = uu.astype(BF16)
        out = out + _dot(gg * jax.nn.sigmoid(gg) * uu, wd_ref[j])
    return out


def ffn_fwd(x, g, wgu, layer, wd):
    S = x.shape[0]
    ts = _tile_rows(S, 256)

    def body(x_ref, g_ref, wgu_ref, wd_ref, out_ref, hf_ref, gu_ref):
        out_ref[...] = _ffn(x_ref[...], g_ref, wgu_ref, wd_ref, hf_ref, gu_ref)

    return pl.pallas_call(
        body, name="ffn_fwd", grid=(S // ts,),
        in_specs=[_row(ts, D), _const(g.shape), _layer_of(wgu, layer), _const(wd.shape)],
        out_specs=[_row(ts, D), _row(ts, D), _ff_chunks(N_DEV, ts)],
        out_shape=[_sds((S, D), F32), _sds((S, D), BF16), _sds((N_DEV, S, FF_CHUNK), BF16)],
        compiler_params=_cp())(x, g, wgu, wd)


def ffn_fwd_loss(x, g, wgu, layer, wd, target, gf):
    S = x.shape[0]
    ts = _tile_rows(S, 256)

    def body(x_ref, g_ref, wgu_ref, wd_ref, t_ref, gf_ref, dx_ref, dgf_ref, loss_ref, hf_ref, gu_ref):
        @pl.when(pl.program_id(0) == 0)
        def _():
            dgf_ref[...] = jnp.zeros_like(dgf_ref)
            loss_ref[...] = jnp.zeros_like(loss_ref)

        out = _ffn(x_ref[...], g_ref, wgu_ref, wd_ref, hf_ref, gu_ref)
        y, rstd = _rms(out, gf_ref[...])
        err = y - t_ref[...]
        loss_ref[...] += 0.5 * _rowsum(jnp.mean(err * err, axis=1, keepdims=True))
        dxn, dgr = _rms_bwd(out, gf_ref[...], rstd, err * (1.0 / D))
        dx_ref[...] = dxn
        dgf_ref[...] += _rowsum(dgr)

    return pl.pallas_call(
        body, name="ffn_fwd_loss", grid=(S // ts,),
        in_specs=[_row(ts, D), _const(g.shape), _layer_of(wgu, layer), _const(wd.shape), _row(ts, D),
                  _const(gf.shape)],
        out_specs=[_row(ts, D), _acc((1, D)), _acc((1, 1)), _row(ts, D), _ff_chunks(N_DEV, ts)],
        out_shape=[_sds((S, D), F32), _sds((1, D), F32), _sds((1, 1), F32), _sds((S, D), BF16),
                   _sds((N_DEV, S, FF_CHUNK), BF16)],
        compiler_params=_cp())(x, g, wgu, wd, target, gf)


def ffn_bwd(x, dy, gu, g, wgu, layer, wd):
    S = x.shape[0]
    ts = _tile_rows(S, 256)

    def body(x_ref, dy_ref, gu_ref, g_ref, wgu_ref, wd_ref, dx_ref, dg_ref, act_ref, dgu_ref):
        @pl.when(pl.program_id(0) == 0)
        def _():
            dg_ref[...] = jnp.zeros_like(dg_ref)

        dy_ = dy_ref[...]
        dyb = dy_.astype(BF16)
        dh = jnp.zeros((ts, D), F32)
        dacts = [_dot_nt(dyb, wd_ref[j]) for j in range(FF_HALF)]
        for j in range(FF_HALF):
            gg, uu = gu_ref[j].astype(F32), gu_ref[j + FF_HALF].astype(F32)
            sg = jax.nn.sigmoid(gg)
            silu = gg * sg
            act_ref[j] = (silu * uu).astype(BF16)
            dact = dacts[j]
            dgate = (dact * uu * (sg * (1.0 + gg * (1.0 - sg)))).astype(BF16)
            dup = (dact * silu).astype(BF16)
            dgu_ref[j] = dgate
            dgu_ref[j + FF_HALF] = dup
            dh = dh + _dot_nt(dgate, wgu_ref[j]) + _dot_nt(dup, wgu_ref[j + FF_HALF])
        x_ = x_ref[...]
        _, rstd = _rms(x_, g_ref[...])
        dxn, dgr = _rms_bwd(x_, g_ref[...], rstd, dh)
        dx_ref[...] = dy_ + dxn
        dg_ref[...] += _rowsum(dgr)

    return pl.pallas_call(
        body, name="ffn_bwd", grid=(S // ts,),
        in_specs=[_row(ts, D), _row(ts, D), _ff_chunks(N_DEV, ts), _const(g.shape), _layer_of(wgu, layer),
                  _const(wd.shape)],
        out_specs=[_row(ts, D), _acc((1, D)), _ff_chunks(FF_HALF, ts), _ff_chunks(N_DEV, ts)],
        out_shape=[_sds((S, D), F32), _sds((1, D), F32), _sds((FF_HALF, S, FF_CHUNK), BF16),
                   _sds((N_DEV, S, FF_CHUNK), BF16)],
        compiler_params=_cp())(x, dy, gu, g, wgu, wd)


def _conv_fwd(xprev, xbp, cw_ref, cb):
    ext = jnp.concatenate([xprev, xbp], axis=0)
    acc = cb + cw_ref[3:4, :] * xbp
    for k in range(3):
        acc = acc + cw_ref[k:k + 1, :] * _roll(ext, 3 - k, 0)[CONV_HALO:]
    return acc


def _decay(r, lam):
    sp = _softplus(-lam)
    log_a = -LRU_C * r * sp
    return sp, jnp.exp(log_a), jnp.sqrt(jnp.maximum(-_expm1(2.0 * log_a), 0.0))


def odd_pre(x, keep, g, win, cw, cb, wr, br, wi, bi, lam):
    S = x.shape[0]
    ts = _tile_rows(S, 512)

    def body(x_ref, xp_ref, keep_ref, g_ref, win_ref, cw_ref, cb_ref, wr_ref, br_ref, wi_ref, bi_ref, lam_ref,
             z_ref, a_ref, b_ref, xb_ref, r_ref, ig_ref):
        i = pl.program_id(0)
        h, _ = _rms(x_ref[...], g_ref[...])
        z = _dot(h, win_ref[...])
        z_ref[...] = z
        hp, _ = _rms(xp_ref[...], g_ref[...])
        xprev = _dot(hp, win_ref[:, D:]) * (i > 0).astype(F32)
        xb = _conv_fwd(xprev, z[:, D:], cw_ref, cb_ref[...])
        xb_ref[...] = xb
        r = jax.nn.sigmoid(_blockdot(xb, wr_ref, LRU_HEADS, LRU_HEAD_DIM) + br_ref[...])
        ig = jax.nn.sigmoid(_blockdot(xb, wi_ref, LRU_HEADS, LRU_HEAD_DIM) + bi_ref[...])
        r_ref[...] = r
        ig_ref[...] = ig
        keep_ = keep_ref[...]
        _, a, mult = _decay(r, lam_ref[...])
        a_ref[...] = a * keep_
        b_ref[...] = jnp.where(keep_ > 0.0, mult, 1.0) * (ig * xb)

    ins = [x, x, keep, g, win, cw, cb, wr, br, wi, bi, lam]
    return pl.pallas_call(
        body, name="odd_pre", grid=(S // ts,),
        in_specs=[_row(ts, D), _prev(CONV_HALO, D, ts), _row(ts, 1)] + [_const(v.shape) for v in ins[3:]],
        out_specs=[_row(ts, 2 * D)] + [_row(ts, D)] * 5,
        out_shape=[_sds((S, 2 * D), F32)] + [_sds((S, D), F32)] * 5, compiler_params=_cp())(*ins)


def lru_scan(a, b, reverse=False):
    S = a.shape[0]
    ts = _tile_rows(S, 512)
    n = S // ts
    groups = ts // 8

    def body(a_ref, an_ref, b_ref, h_ref, carry_ref, ash_ref):
        i = pl.program_id(0)

        @pl.when(i == 0)
        def _():
            carry_ref[...] = jnp.zeros_like(carry_ref)

        rid = lax.broadcasted_iota(jnp.int32, (8, D), 0)
        if reverse:
            ext = jnp.concatenate([a_ref[...], an_ref[...] * (i > 0).astype(F32)], axis=0)
            ash_ref[...] = _roll(ext, -1, 0)[:ts]
        src = ash_ref if reverse else a_ref

        def group(j, carry):
            off = pl.multiple_of((groups - 1 - j if reverse else j) * 8, 8)
            a8, b8 = src[pl.ds(off, 8), :], b_ref[pl.ds(off, 8), :]
            for k in (1, 2, 4):
                inside = (rid < 8 - k) if reverse else (rid >= k)
                sh = -k if reverse else k
                a_sh = jnp.where(inside, _roll(a8, sh, 0), 1.0)
                b_sh = jnp.where(inside, _roll(b8, sh, 0), 0.0)
                b8 = a8 * b_sh + b8
                a8 = a8 * a_sh
            h8 = a8 * carry + b8
            h_ref[pl.ds(off, 8), :] = h8
            return h8[0:1, :] if reverse else h8[7:8, :]

        carry_ref[...] = lax.fori_loop(0, groups, group, carry_ref[...], unroll=4)

    if reverse:
        r = ts // 8
        tile = pl.BlockSpec((ts, D), lambda i: (n - 1 - i, 0))
        halo = pl.BlockSpec((8, D), lambda i: (jnp.minimum((n - i) * r, n * r - 1), 0))
    else:
        tile, halo = _row(ts, D), _prev(8, D, ts)
    return pl.pallas_call(
        body, name="lru_scan_rev" if reverse else "lru_scan", grid=(n,), in_specs=[tile, halo, tile],
        out_specs=tile, out_shape=_sds((S, D), F32),
        scratch_shapes=[pltpu.VMEM((1, D), F32), pltpu.VMEM((ts, D), F32)], compiler_params=_cp())(a, a, b)


def odd_post(x, z, hseq, wout):
    S = x.shape[0]
    ts = _tile_rows(S, 512)

    def body(x_ref, gate_ref, h_ref, w_ref, out_ref):
        gl, _ = _gelu(gate_ref[...])
        out_ref[...] = x_ref[...] + _dot(gl * h_ref[...], w_ref[...])

    return pl.pallas_call(
        body, name="odd_post", grid=(S // ts,),
        in_specs=[_row(ts, D), _row(ts, D), _row(ts, D), _const(wout.shape)],
        out_specs=_row(ts, D), out_shape=_sds((S, D), F32), compiler_params=_cp())(x, z, hseq, wout)


def _accumulate_tn(acc_ref, out_ref, a, b, steps):
    i = pl.program_id(0)

    @pl.when(i == 0)
    def _():
        acc_ref[...] = jnp.zeros_like(acc_ref)

    acc_ref[...] += _dot_tn(a, b)

    @pl.when(i == steps - 1)
    def _():
        out_ref[...] = acc_ref[...].astype(out_ref.dtype)


def odd_post_bwd(dy, z, hseq, wout):
    S = dy.shape[0]
    ts = _tile_rows(S, 512)
    n = S // ts

    def body(dy_ref, gate_ref, h_ref, w_ref, dgate_ref, dh_ref, dw_ref, acc_ref):
        gate, hs, dy_ = gate_ref[...], h_ref[...], dy_ref[...]
        gl, t = _gelu(gate)
        dyy = _dot_nt(dy_, w_ref[...])
        dgate_ref[...] = dyy * hs * _gelu_grad(gate, t)
        dh_ref[...] = dyy * gl
        _accumulate_tn(acc_ref, dw_ref, gl * hs, dy_, n)

    return pl.pallas_call(
        body, name="odd_post_bwd", grid=(n,),
        in_specs=[_row(ts, D), _row(ts, D), _row(ts, D), _const(wout.shape)],
        out_specs=[_row(ts, D), _row(ts, D), _acc((D, D))],
        out_shape=[_sds((S, D), F32), _sds((S, D), F32), _sds((D, D), BF16)],
        scratch_shapes=[pltpu.VMEM((D, D), F32)], compiler_params=_cp())(dy, z, hseq, wout)


def odd_gates_bwd(xb, r, ig, lam_grad, hseq, keep, wr, wi, lam):
    S = xb.shape[0]
    ts = _tile_rows(S, 512)

    def body(xb_ref, r_ref, ig_ref, lg_ref, h_ref, hp_ref, keep_ref, wr_ref, wi_ref, lam_ref,
             dxb_ref, dcb_ref, dbr_ref, dbi_ref, dlam_ref, dwr_ref, dwi_ref):
        i = pl.program_id(0)

        @pl.when(i == 0)
        def _():
            for ref in (dcb_ref, dbr_ref, dbi_ref, dlam_ref, dwr_ref, dwi_ref):
                ref[...] = jnp.zeros_like(ref)

        first = (i > 0).astype(F32)
        xb, r, ig = xb_ref[...], r_ref[...], ig_ref[...]
        keep_ = keep_ref[...]
        lam_ = lam_ref[...]
        sp, a, mult = _decay(r, lam_)
        hs = h_ref[...]
        hprev = _roll(jnp.concatenate([hp_ref[...] * first, hs], axis=0), 1, 0)[CONV_HALO:]
        lg = lg_ref[...]
        da = lg * hprev * keep_
        ixb = ig * xb
        dmult = lg * ixb * keep_
        dixb = lg * jnp.where(keep_ > 0.0, mult, 1.0)
        dlog_a = da * a - dmult * jnp.where(mult > 0.0, a * a / mult, 0.0)
        dr = dlog_a * (-LRU_C * sp)
        dlam_ref[...] += _rowsum(dlog_a * (-LRU_C * r)) * (-jax.nn.sigmoid(-lam_))
        dpr = dr * r * (1.0 - r)
        dpi = dixb * xb * ig * (1.0 - ig)
        dbr_ref[...] += _rowsum(dpr)
        dbi_ref[...] += _rowsum(dpi)
        dxb = dixb * ig
        parts = []
        for h in range(LRU_HEADS):
            sl = slice(h * LRU_HEAD_DIM, (h + 1) * LRU_HEAD_DIM)
            dwr_ref[h] += _dot_tn(xb[:, sl], dpr[:, sl])
            dwi_ref[h] += _dot_tn(xb[:, sl], dpi[:, sl])
            parts.append(_dot_nt(dpr[:, sl], wr_ref[h]) + _dot_nt(dpi[:, sl], wi_ref[h]))
        dxb = dxb + jnp.concatenate(parts, axis=1)
        dxb_ref[...] = dxb
        dcb_ref[...] += _rowsum(dxb)

    ins = [xb, r, ig, lam_grad, hseq, hseq, keep, wr, wi, lam]
    in_specs = [_row(ts, D)] * 5 + [_prev(CONV_HALO, D, ts), _row(ts, 1)] + [_const(v.shape) for v in ins[7:]]
    gshape = (LRU_HEADS, LRU_HEAD_DIM, LRU_HEAD_DIM)
    return pl.pallas_call(
        body, name="odd_gates_bwd", grid=(S // ts,), in_specs=in_specs,
        out_specs=[_row(ts, D)] + [_acc((1, D))] * 4 + [_acc(gshape)] * 2,
        out_shape=[_sds((S, D), F32)] + [_sds((1, D), F32)] * 4 + [_sds(gshape, F32)] * 2,
        compiler_params=_cp())(*ins)


def odd_pre_bwd(x, dy, z, dxb, dgate, g, cw, win):
    S = x.shape[0]
    ts = _tile_rows(S, 512)
    n = S // ts

    def body(x_ref, dy_ref, xbp_ref, xbpp_ref, dxb_ref, dxbn_ref, dgate_ref, g_ref, cw_ref, win_ref,
             dx_ref, dcw_ref, dg_ref, dwin_ref, acc_ref):
        i = pl.program_id(0)

        @pl.when(i == 0)
        def _():
            dcw_ref[...] = jnp.zeros_like(dcw_ref)
            dg_ref[...] = jnp.zeros_like(dg_ref)

        dxb = dxb_ref[...]
        extd = jnp.concatenate([dxb, dxbn_ref[...] * (i < n - 1).astype(F32)], axis=0)
        extx = jnp.concatenate([xbpp_ref[...] * (i > 0).astype(F32), xbp_ref[...]], axis=0)
        dxbp = cw_ref[3:4, :] * dxb
        dcw_ref[3:4, :] += _rowsum(dxb * xbp_ref[...])
        for k in range(3):
            dxbp = dxbp + cw_ref[k:k + 1, :] * _roll(extd, -(3 - k), 0)[:ts]
            dcw_ref[k:k + 1, :] += _rowsum(dxb * _roll(extx, 3 - k, 0)[CONV_HALO:])
        dz = jnp.concatenate([dgate_ref[...], dxbp], axis=1).astype(BF16)
        x_ = x_ref[...]
        h, rstd = _rms(x_, g_ref[...])
        dxn, dgr = _rms_bwd(x_, g_ref[...], rstd, _dot_nt(dz, win_ref[...]))
        dx_ref[...] = dy_ref[...] + dxn
        dg_ref[...] += _rowsum(dgr)
        _accumulate_tn(acc_ref, dwin_ref, h, dz, n)

    ins = [x, dy, z, z, dxb, dxb, dgate, g, cw, win]
    in_specs = [_row(ts, D), _row(ts, D), _row(ts, D, 1), _prev(CONV_HALO, D, ts, 1), _row(ts, D),
                _next(CONV_HALO, D, ts, n), _row(ts, D)] + [_const(v.shape) for v in ins[7:]]
    return pl.pallas_call(
        body, name="odd_pre_bwd", grid=(n,), in_specs=in_specs,
        out_specs=[_row(ts, D), _acc((4, D)), _acc((1, D)), _acc((D, 2 * D))],
        out_shape=[_sds((S, D), F32), _sds((4, D), F32), _sds((1, D), F32), _sds((D, 2 * D), BF16)],
        scratch_shapes=[pltpu.VMEM((D, 2 * D), F32)], compiler_params=_cp())(*ins)


def even_post_bwd(dy, ypool, o, wo_pool, wo_att):
    S = dy.shape[0]
    ts = _tile_rows(S, 512)
    n = S // ts

    def body(dy_ref, yp_ref, o_ref, wp_ref, wa_ref, dyp_ref, do_ref, delta_ref, dwp_ref, dwa_ref, accp_ref,
             acca_ref):
        dy_, o_ = dy_ref[...], o_ref[...]
        dyp_ref[...] = _dot_nt(dy_, wp_ref[...])
        do = _dot_nt(dy_, wa_ref[...])
        do_ref[...] = do.astype(BF16)
        prod = do * o_
        for h in range(MLA_HEADS):
            delta_ref[h] = _as_row(jnp.sum(prod[:, h * LANES:(h + 1) * LANES], axis=1, keepdims=True))
        _accumulate_tn(accp_ref, dwp_ref, yp_ref[...], dy_, n)
        _accumulate_tn(acca_ref, dwa_ref, o_, dy_, n)

    return pl.pallas_call(
        body, name="even_post_bwd", grid=(n,),
        in_specs=[_row(ts, D), _row(ts, POOL_DIM), _row(ts, D), _const(wo_pool.shape), _const(wo_att.shape)],
        out_specs=[_row(ts, POOL_DIM), _row(ts, D),
                   pl.BlockSpec((MLA_HEADS, None, 1, ts), lambda i: (0, i, 0, 0)), _acc((POOL_DIM, D)),
                   _acc((D, D))],
        out_shape=[_sds((S, POOL_DIM), F32), _sds((S, D), BF16), _sds((MLA_HEADS, n, 1, ts), F32),
                   _sds((POOL_DIM, D), BF16), _sds((D, D), BF16)],
        scratch_shapes=[pltpu.VMEM((POOL_DIM, D), F32), pltpu.VMEM((D, D), F32)],
        compiler_params=_cp())(dy, ypool, o, wo_pool, wo_att)


def attn_bwd(qp, kp, vp, do, lse_row, delta_row, token=None):
    S = qp.shape[0]
    tk = _tile_rows(S, 512)
    nq = S // tk
    extra, extra_specs = _after(token)

    def body(q_ref, k_ref, v_ref, do_ref, lse_ref, delta_ref, *rest):
        dq_ref, dk_ref, dv_ref = rest[-3:]
        kj = pl.program_id(1)

        @pl.when(kj == 0)
        def _():
            dq_ref[...] = jnp.zeros_like(dq_ref)

        k, v = k_ref[...], v_ref[...]

        def block(qi, carry, masked):
            dk, dv = carry
            off = pl.multiple_of(qi * tk, tk)
            q = q_ref[pl.ds(off, tk), :]
            do_ = do_ref[pl.ds(off, tk), :]
            st = _dot_nt(k, q)
            if masked:
                row = lax.broadcasted_iota(jnp.int32, (tk, tk), 0)
                col = lax.broadcasted_iota(jnp.int32, (tk, tk), 1)
                st = jnp.where(col >= row, st, -1e30)
            pt = _exp2(st - lse_ref[qi])
            dv = dv + _dot(pt, do_)
            dst = (pt * (_dot_nt(v, do_) - delta_ref[qi])).astype(BF16)
            dk = dk + _dot(dst, q)
            dq_ref[pl.ds(off, tk), :] += _dot_tn(dst, k)
            return dk, dv

        zero = jnp.zeros((tk, LANES), F32)
        carry = block(kj, (zero, zero), True)
        dk, dv = _pair_loop(kj + 1, nq, lambda qi, c: block(qi, c, False), carry, unrolls=(8, 4, 2, 1))
        dk_ref[...] = dk * LN_2
        dv_ref[...] = dv

    blk = pl.BlockSpec((tk, LANES), lambda h, j: (j, h))
    full = pl.BlockSpec((S, LANES), lambda h, j: (0, h))
    rowv = pl.BlockSpec((None, nq, 1, tk), lambda h, j: (h, 0, 0, 0))
    return pl.pallas_call(
        body, name="attn_bwd", grid=(MLA_HEADS, nq), in_specs=[full, blk, blk, full, rowv, rowv] + extra_specs,
        out_specs=[full, blk, blk], out_shape=[_sds((S, D), F32)] * 3, compiler_params=_cp2())(
            qp, kp, vp, do, lse_row, delta_row, *extra)


def even_pre_bwd(x, dy, z, dq, dk, dv, dyp, tabs, g, win, pw, pscale, qg, wq, kvg, wk, wv):
    S = x.shape[0]
    ts = _tile_rows(S, 512)
    n = S // ts

    def body(x_ref, dy_ref, z_ref, up_ref, dq_ref, dk_ref, dv_ref, dyp_ref, dypn_ref, c_ref, a_ref, b_ref,
             g_ref, win_ref, pw_ref, ps_ref, qg_ref, wq_ref, kvg_ref, wk_ref, wv_ref,
             dx_ref, dg_ref, dpw_ref, dps_ref, dqg_ref, dwq_ref, dkvg_ref, dwk_ref, dwv_ref, dwin_ref, acc_ref):
        i = pl.program_id(0)

        @pl.when(i == 0)
        def _():
            for ref in (dg_ref, dpw_ref, dps_ref, dqg_ref, dwq_ref, dkvg_ref, dwk_ref, dwv_ref):
                ref[...] = jnp.zeros_like(ref)

        z = z_ref[...]
        c, a, b = c_ref[...], a_ref[...], b_ref[...]
        ps = ps_ref[...]
        u = z[:, :POOL_DIM]
        pooled = _pooled(up_ref[...] * (i > 0).astype(F32), u, i * ts)
        dyp_ = dyp_ref[...]
        dps_ref[...] += _rowsum(dyp_ * _blockdot(pooled, pw_ref, 4, LANES))
        ext = jnp.concatenate([dyp_, dypn_ref[...] * (i < n - 1).astype(F32)], axis=0) * ps
        for gidx in range(4):
            sl = slice(gidx * LANES, (gidx + 1) * LANES)
            dpw_ref[gidx] += _dot_tn(pooled[:, sl], ext[:ts, sl])
        dpooled = jnp.concatenate(
            [_dot_nt(ext[:, gidx * LANES:(gidx + 1) * LANES], pw_ref[gidx]) for gidx in range(4)], axis=1)
        dm = dpooled / _pool_cnt(i * ts, ts + POOL_HALO)
        du = _pool_windows(dm, -1)[:ts] - dpooled[:ts]
        cq = z[:, 512:768]
        cqn, rstd_q = _rms(cq, qg_ref[...])
        dqf = _rope_bwd(dq_ref[...] * ATTN_SCALE, c, a, b)
        dwq_ref[...] += _dot_tn(cqn, dqf)
        dcq, dqg_rows = _rms_bwd(cq, qg_ref[...], rstd_q, _dot_nt(dqf, wq_ref[...]))
        dqg_ref[...] += _rowsum(dqg_rows)
        ckv = z[:, 768:896]
        ckvn, rstd_kv = _rms(ckv, kvg_ref[...])
        dk_, dv_ = dk_ref[...], dv_ref[...]
        dwk_ref[...] += _dot_tn(ckvn, dk_)
        dwv_ref[...] += _dot_tn(ckvn, dv_)
        dckv, dkvg_rows = _rms_bwd(ckv, kvg_ref[...], rstd_kv,
                                   _dot_nt(dk_, wk_ref[...]) + _dot_nt(dv_, wv_ref[...]))
        dkvg_ref[...] += _rowsum(dkvg_rows)
        dkr = dk_[:, :LANES]
        for h in range(1, MLA_HEADS):
            dkr = dkr + dk_[:, h * LANES:(h + 1) * LANES]
        lane = lax.broadcasted_iota(jnp.int32, (ts, LANES), 1)
        dkr = jnp.where((lane >= 64) & (lane < 96), _rope_bwd(dkr, c, a, b), 0.0)
        dz = jnp.concatenate([du, dcq, dckv, dkr], axis=1).astype(BF16)
        x_ = x_ref[...]
        h, rstd = _rms(x_, g_ref[...])
        dxn, dgr = _rms_bwd(x_, g_ref[...], rstd, _dot_nt(dz, win_ref[...]))
        dx_ref[...] = dy_ref[...] + dxn
        dg_ref[...] += _rowsum(dgr)
        _accumulate_tn(acc_ref, dwin_ref, h, dz, n)

    ins = [x, dy, z, z, dq, dk, dv, dyp, dyp, *tabs, g, win, pw, pscale, qg, wq, kvg, wk, wv]
    in_specs = [_row(ts, D), _row(ts, D), _row(ts, D), _prev(POOL_HALO, POOL_DIM, ts), _row(ts, D), _row(ts, D),
                _row(ts, D), _row(ts, POOL_DIM), _next(POOL_HALO, POOL_DIM, ts, n), _row(ts, LANES),
                _row(ts, LANES), _row(ts, LANES)] + [_const(v.shape) for v in ins[12:]]
    acc_shapes = [(1, D), (4, LANES, LANES), (1, POOL_DIM), (1, Q_LORA), (Q_LORA, D), (1, KV_LORA), (KV_LORA, D),
                  (KV_LORA, D)]
    return pl.pallas_call(
        body, name="even_pre_bwd", grid=(n,), in_specs=in_specs,
        out_specs=[_row(ts, D)] + [_acc(s) for s in acc_shapes] + [_acc((D, D))],
        out_shape=[_sds((S, D), F32)] + [_sds(s, F32) for s in acc_shapes] + [_sds((D, D), BF16)],
        scratch_shapes=[pltpu.VMEM((D, D), F32)], compiler_params=_cp())(*ins)


def _pick(n, options):
    for o in options:
        if n % o == 0:
            return o
    return n


def matmul_tn(name, a, b):
    out_dtype = BF16
    S = a.shape[-2]
    ts = _tile_rows(S, 2048)
    steps = S // ts

    def body(a_ref, b_ref, o_ref, acc_ref):
        s = pl.program_id(2)

        @pl.when(s == 0)
        def _():
            acc_ref[...] = jnp.zeros_like(acc_ref)

        acc_ref[...] += _dot_tn(a_ref[...], b_ref[...])

        @pl.when(s == steps - 1)
        def _():
            o_ref[...] = acc_ref[...].astype(o_ref.dtype)

    if a.ndim == 3:
        C, _, K = a.shape
        N = b.shape[1]
        tn = _pick(N, (1024, 512, 256, 128))
        grid = (C, N // tn, S // ts)
        in_specs = [pl.BlockSpec((None, ts, K), lambda c, j, s: (c, s, 0)),
                    pl.BlockSpec((ts, tn), lambda c, j, s: (s, j))]
        out_spec, out_shape, tile = pl.BlockSpec((None, K, tn), lambda c, j, s: (c, 0, j)), (C, K, N), (K, tn)
    elif b.ndim == 3:
        C, _, N = b.shape
        K = a.shape[1]
        tk = _pick(K, (1024, 512, 256, 128))
        grid = (C, K // tk, S // ts)
        in_specs = [pl.BlockSpec((ts, tk), lambda c, i, s: (s, i)),
                    pl.BlockSpec((None, ts, N), lambda c, i, s: (c, s, 0))]
        out_spec, out_shape, tile = pl.BlockSpec((None, tk, N), lambda c, i, s: (c, i, 0)), (C, K, N), (tk, N)
    else:
        K, N = a.shape[1], b.shape[1]
        tk = _pick(K, (1024, 512, 256, 128))
        tn = _pick(N, (1024, 512, 256, 128))
        grid = (K // tk, N // tn, S // ts)
        in_specs = [pl.BlockSpec((ts, tk), lambda i, j, s: (s, i)), pl.BlockSpec((ts, tn), lambda i, j, s: (s, j))]
        out_spec, out_shape, tile = pl.BlockSpec((tk, tn), lambda i, j, s: (i, j)), (K, N), (tk, tn)
    return pl.pallas_call(
        body, name=name, grid=grid, in_specs=in_specs, out_specs=out_spec, out_shape=_sds(out_shape, out_dtype),
        scratch_shapes=[pltpu.VMEM(tile, F32)], compiler_params=pltpu.CompilerParams(dimension_semantics=("arbitrary",) * 3, vmem_limit_bytes=VMEM_LIMIT))(
            a, b)


def _my_id():
    return lax.axis_index("x") * 4 + lax.axis_index("y") * 2 + lax.axis_index("c")


def _peer(j):
    x, y, c = lax.axis_index("x"), lax.axis_index("y"), lax.axis_index("c")
    px = 1 - x if j & 4 else x
    py = 1 - y if j & 2 else y
    pc = 1 - c if j & 1 else c
    return (px, py, pc), px * 4 + py * 2 + pc


def all_gather(name, arrays):
    n = len(arrays)

    def body(*refs):
        ins, outs = refs[:n], refs[n:2 * n]
        send_sems, recv_sems, local_sems = refs[2 * n:]
        me = _my_id()
        local = [pltpu.make_async_copy(ins[k], outs[k].at[me], local_sems.at[k]) for k in range(n)]
        for cp in local:
            cp.start()
        sends = []
        for j in range(1, N_DEV):
            peer, _ = _peer(j)
            for k in range(n):
                cp = pltpu.make_async_remote_copy(
                    src_ref=ins[k], dst_ref=outs[k].at[me], send_sem=send_sems.at[k, j - 1],
                    recv_sem=recv_sems.at[k, j - 1], device_id=peer, device_id_type=pl.DeviceIdType.MESH)
                cp.start()
                sends.append(cp)
        for j in range(1, N_DEV):
            peer, pid = _peer(j)
            for k in range(n):
                pltpu.make_async_remote_copy(
                    src_ref=ins[k], dst_ref=outs[k].at[pid], send_sem=send_sems.at[k, j - 1],
                    recv_sem=recv_sems.at[k, j - 1], device_id=peer, device_id_type=pl.DeviceIdType.MESH).wait_recv()
        for cp in sends:
            cp.wait_send()
        for cp in local:
            cp.wait()

    any_spec = pl.BlockSpec(memory_space=pl.ANY)
    return pl.pallas_call(
        body, name=name, in_specs=[any_spec] * n, out_specs=[any_spec] * n,
        out_shape=[_sds((N_DEV,) + a.shape, a.dtype) for a in arrays],
        scratch_shapes=[pltpu.SemaphoreType.DMA((n, N_DEV - 1)), pltpu.SemaphoreType.DMA((n, N_DEV - 1)),
                        pltpu.SemaphoreType.DMA((n,))],
        compiler_params=pltpu.CompilerParams(has_side_effects=True))(*arrays)


def exchange(name, arrays, gathers=()):
    n_ex, n = len(arrays), len(arrays) + len(gathers)

    def body(*refs):
        ins, outs = refs[:n], refs[n:2 * n]
        send_sems, recv_sems, local_sems = refs[2 * n:]
        me = _my_id()

        def mine(k, slot):
            return ins[k].at[slot] if k < n_ex else ins[k]

        local = [pltpu.make_async_copy(mine(k, me), outs[k].at[me], local_sems.at[k]) for k in range(n)]
        for cp in local:
            cp.start()
        sends = []
        for j in range(1, N_DEV):
            peer, pid = _peer(j)
            for k in range(n):
                cp = pltpu.make_async_remote_copy(
                    src_ref=mine(k, pid), dst_ref=outs[k].at[me], send_sem=send_sems.at[k, j - 1],
                    recv_sem=recv_sems.at[k, j - 1], device_id=peer, device_id_type=pl.DeviceIdType.MESH)
                cp.start()
                sends.append(cp)
        for j in range(1, N_DEV):
            peer, pid = _peer(j)
            for k in range(n):
                pltpu.make_async_remote_copy(
                    src_ref=mine(k, me), dst_ref=outs[k].at[pid], send_sem=send_sems.at[k, j - 1],
                    recv_sem=recv_sems.at[k, j - 1], device_id=peer, device_id_type=pl.DeviceIdType.MESH).wait_recv()
        for cp in sends:
            cp.wait_send()
        for cp in local:
            cp.wait()

    any_spec = pl.BlockSpec(memory_space=pl.ANY)
    return pl.pallas_call(
        body, name=name, in_specs=[any_spec] * n, out_specs=[any_spec] * n,
        out_shape=[_sds(a.shape, a.dtype) for a in arrays] + [_sds((N_DEV,) + a.shape, a.dtype) for a in gathers],
        scratch_shapes=[pltpu.SemaphoreType.DMA((n, N_DEV - 1)), pltpu.SemaphoreType.DMA((n, N_DEV - 1)),
                        pltpu.SemaphoreType.DMA((n,))],
        compiler_params=pltpu.CompilerParams(has_side_effects=True))(*arrays, *gathers)


_HBM = pl.BlockSpec(memory_space=pltpu.HBM)
_SEM = pl.BlockSpec(memory_space=pltpu.SEMAPHORE)
_DATAFLOW = pltpu.SideEffectType.DATAFLOW_SIDE_EFFECTING


def _in_hbm(v):
    return pltpu.with_memory_space_constraint(v, pltpu.HBM)


N_PEERS = N_DEV - 1


def _split_copy(k, j, srcs, lands, send_sems, recv_sems, gather, slot):
    peer, pid = _peer(j)
    return pltpu.make_async_remote_copy(
        src_ref=srcs[k] if _flag(gather, k) else srcs[k].at[pid],
        dst_ref=lands[k].at[_my_id() if slot == "mine" else pid],
        send_sem=send_sems[j - 1], recv_sem=recv_sems[j - 1], device_id=peer, device_id_type=pl.DeviceIdType.MESH)


def _flag(gather, k):
    return gather[k] if isinstance(gather, tuple) else gather


def split_start(name, arrays, gather):
    n = len(arrays)
    lands = [lax.empty((N_DEV,) + a.shape if _flag(gather, k) else a.shape, a.dtype) for k, a in enumerate(arrays)]

    def body(*refs):
        srcs, lnds = refs[:n], refs[n:2 * n]
        sems = refs[4 * n:4 * n + 2 * N_PEERS]
        token = refs[-1]
        for j in range(1, N_DEV):
            for k in range(n):
                _split_copy(k, j, srcs, lnds, sems[:N_PEERS], sems[N_PEERS:], gather, "mine").start()
        token[...] = jnp.zeros_like(token)

    out = pl.pallas_call(
        body, name=name,
        out_shape=(*[pltpu.HBM(a.shape, a.dtype) for a in arrays], *[pltpu.HBM(l.shape, l.dtype) for l in lands],
                   *[pltpu.SemaphoreType.DMA(())] * (2 * N_PEERS), _sds((8, LANES), F32)),
        in_specs=[_HBM] * (2 * n),
        out_specs=(*[_HBM] * (2 * n), *[_SEM] * (2 * N_PEERS), pl.BlockSpec(memory_space=pltpu.VMEM)),
        input_output_aliases={k: k for k in range(2 * n)},
        compiler_params=pltpu.CompilerParams(has_side_effects=_DATAFLOW))(
            *[_in_hbm(a) for a in arrays], *[_in_hbm(l) for l in lands])
    sems = list(out[2 * n:2 * n + 2 * N_PEERS])
    return sems[:N_PEERS], sems[N_PEERS:], list(out[:n]), list(out[n:2 * n]), out[-1]


def split_wait(name, handle, after, gather):
    send_sems, recv_sems, srcs, lands, _ = handle
    n = len(srcs)

    def body(*refs):
        srcs_r, lnds_r = refs[:n], refs[n:2 * n]
        sems = refs[2 * n:2 * n + 2 * N_PEERS]
        for j in range(1, N_DEV):
            for k in range(n):
                cp = _split_copy(k, j, srcs_r, lnds_r, sems[:N_PEERS], sems[N_PEERS:], gather, "peer")
                cp.wait_send()
                cp.wait_recv()

    out = pl.pallas_call(
        body, name=name, out_shape=tuple(pltpu.HBM(a.shape, a.dtype) for a in srcs + lands),
        in_specs=[_HBM] * (2 * n) + [_SEM] * (2 * N_PEERS) + [pl.BlockSpec(memory_space=pl.ANY)],
        out_specs=tuple([_HBM] * (2 * n)), input_output_aliases={k: k for k in range(2 * n)},
        compiler_params=pltpu.CompilerParams(has_side_effects=_DATAFLOW))(
            *srcs, *lands, *send_sems, *recv_sems, after)
    return list(out[:n]), list(out[n:])


def _fill_own_slot(src, land, gather):
    me = _my_id()
    own = src[None] if gather else lax.dynamic_index_in_dim(src, me, 0, keepdims=True)
    return lax.dynamic_update_slice_in_dim(land, own, me, 0)


ADAMW_BLOCK_ELEMS = 128 * 1024


def adamw(name, parts, w, m, v, token=None):
    R, C = w.shape
    tr = _pick(R, [t for t in (512, 256, 128, 64, 32, 16, 8) if t * C <= ADAMW_BLOCK_ELEMS])
    c1 = 1.0 - ADAM_B1 ** ADAM_STEP
    c2 = 1.0 - ADAM_B2 ** ADAM_STEP
    extra, extra_specs = _after(token)

    def body(p_ref, w_ref, m_ref, v_ref, *rest):
        g_ref, d_ref, nm_ref, nv_ref = rest[-4:]
        g = p_ref[0].astype(F32)
        for s in range(1, N_DEV):
            g = g + p_ref[s].astype(F32)
        g_ref[...] = g
        m_ = ADAM_B1 * m_ref[...] + (1.0 - ADAM_B1) * g
        v_ = ADAM_B2 * v_ref[...] + (1.0 - ADAM_B2) * (g * g)
        nm_ref[...] = m_
        nv_ref[...] = v_
        d_ref[...] = -ADAM_LR * ((m_ / c1) / (jnp.sqrt(v_ / c2) + ADAM_EPS) + ADAM_WD * w_ref[...])

    row = pl.BlockSpec((tr, C), lambda i: (i, 0))
    return pl.pallas_call(
        body, name=name, grid=(R // tr,),
        in_specs=[pl.BlockSpec((N_DEV, tr, C), lambda i: (0, i, 0)), row, row, row] + extra_specs,
        out_specs=[row] * 4, out_shape=[_sds((R, C), F32)] * 4, compiler_params=_cp())(parts, w, m, v, *extra)


WEIGHTS = ['ev_norm', 'ev_w_in', 'ev_pool_w', 'ev_pool_scale', 'ev_q_norm', 'ev_w_q_up', 'ev_kv_norm', 'ev_w_kv_up',
           'ev_w_out', 'od_norm', 'od_w_in', 'od_conv_w', 'od_conv_b', 'od_w_rgate', 'od_b_rgate', 'od_w_igate',
           'od_b_igate', 'od_lambda', 'od_w_out', 'xa_norm_x', 'xa_norm_mem', 'xa_w_q', 'xa_w_kv', 'xa_w_o',
           'ffn_norm', 'ffn_w_gate_up', 'ffn_w_down', 'final_norm']
SHARD_AXIS = {'ev_w_in': 1, 'ev_w_q_up': 2, 'ev_w_kv_up': 2, 'ev_w_out': 1, 'od_norm': 1, 'od_w_in': 2,
              'od_conv_w': 2, 'od_conv_b': 1, 'od_w_rgate': 2, 'od_b_rgate': 1, 'od_w_igate': 2, 'od_b_igate': 1,
              'od_lambda': 1, 'od_w_out': 1, 'xa_w_q': 1, 'xa_w_kv': 2, 'xa_w_o': 1, 'ffn_w_gate_up': 2,
              'ffn_w_down': 1}
SMALL_F32 = ('od_norm', 'od_conv_w', 'od_conv_b', 'od_b_rgate', 'od_b_igate', 'od_lambda')
STACKED = ('ffn_w_gate_up', 'ffn_w_down')
SHARDED = [n for n in WEIGHTS if n in SHARD_AXIS]
REPLICATED = [n for n in WEIGHTS if n not in SHARD_AXIS]
ROW_ALIGN = 512


def _pack(flats, dtype):
    v = jnp.concatenate([f.reshape(-1).astype(dtype) for f in flats])
    pad = (-v.shape[0]) % (ROW_ALIGN * LANES)
    return jnp.pad(v, (0, pad)).reshape(-1, LANES)


def _rows8(n_elems):
    return -(-n_elems // (8 * LANES)) * 8


def _pack_rows(arrays, lead=False):
    out = []
    for a in arrays:
        r = a.reshape((N_DEV, -1, LANES) if lead else (-1, LANES))
        pad = _rows8(r.shape[-2] * LANES) - r.shape[-2]
        out.append(jnp.pad(r, [(0, 0)] * (r.ndim - 2) + [(0, pad), (0, 0)]))
    return jnp.concatenate(out, axis=-2)


def _unpack_rows(buf, shapes, lead=False):
    out, off = [], 0
    for s in shapes:
        n = 1
        for d in s:
            n *= d
        rows = buf[..., off:off + n // LANES, :]
        out.append(rows.reshape(((N_DEV,) if lead else ()) + tuple(s)))
        off += _rows8(n)
    return out


def _unpack(flat, shapes):
    out, off = [], 0
    v = flat.reshape(-1)
    for s in shapes:
        n = 1
        for d in s:
            n *= d
        out.append(v[off:off + n].reshape(s))
        off += n
    return out


def _to_full(stacked, axis):
    v = jnp.moveaxis(stacked, 0, axis)
    s = v.shape
    return v.reshape(s[:axis] + (s[axis] * s[axis + 1],) + s[axis + 2:])


def _to_shards(full, axis):
    s = full.shape
    v = full.reshape(s[:axis] + (N_DEV, s[axis] // N_DEV) + s[axis + 1:])
    return jnp.moveaxis(v, axis, 0)


def _pad_heads(w, nh, dh, lead):
    s = w.shape
    v = w.reshape(s[:-1] + (nh, dh))
    v = jnp.pad(v, [(0, 0)] * (len(s) - 1) + [(0, 0), (lead, LANES - dh - lead)])
    return v.reshape(s[:-1] + (nh * LANES,))


def _unpad_heads(w, nh, dh, lead):
    s = w.shape
    return w.reshape(s[:-1] + (nh, LANES))[..., lead:lead + dh].reshape(s[:-1] + (nh * dh,))


def _rope_tables(positions):
    inv_freq = 10000.0 ** (-jnp.arange(0, 32, 2, dtype=F32) / 32)
    ang = positions.astype(F32)[:, None] * inv_freq
    cos, sin = jnp.tile(jnp.cos(ang), (1, LANES // 16)), jnp.tile(jnp.sin(ang), (1, LANES // 16))
    lane = lax.broadcasted_iota(jnp.int32, cos.shape, 1)
    c = jnp.where((lane >= 64) & (lane < 96), cos, 1.0)
    a = jnp.where((lane >= 80) & (lane < 96), sin, 0.0)
    b = jnp.where((lane >= 64) & (lane < 80), -sin, 0.0)
    return c, a, b


def _t(w):
    return jnp.swapaxes(w, -1, -2)


def device_step(x, mem, positions, target, W, fwd_token=None, late_weights=None, ship_grads=None,
                first_weights=None):
    G = {}
    tabs = _rope_tables(positions)
    keep = (positions != 0).astype(F32)[:, None]
    row = lambda v: v.reshape(1, -1)
    if first_weights is not None:
        W = {**W, **first_weights(tabs[0])}

    w_in = W['ev_w_in'][0]
    ev_win = jnp.concatenate([w_in[:, :896], _pad_heads(w_in[:, 896:], 1, 32, 64)], axis=1)
    ev_wq = _pad_heads(W['ev_w_q_up'][0], MLA_HEADS, QK_DIM, 0)
    kvw = W['ev_w_kv_up'][0].reshape(KV_LORA, MLA_HEADS, 128)
    ev_wk = _pad_heads(kvw[:, :, :64].reshape(KV_LORA, 512), MLA_HEADS, 64, 0)
    ev_wv = _pad_heads(kvw[:, :, 64:].reshape(KV_LORA, 512), MLA_HEADS, 64, 0)
    ev_wo_pool = W['ev_w_out'][0][:POOL_DIM]
    ev_wo_att = _t(_pad_heads(_t(W['ev_w_out'][0][POOL_DIM:]), MLA_HEADS, 64, 0))
    pw = W['ev_pool_w'][0].astype(BF16)
    ev_g, ps, qg, kvg = row(W['ev_norm'][0]), row(W['ev_pool_scale'][0]), row(W['ev_q_norm'][0]), row(W['ev_kv_norm'][0])

    z0, qp, kp, vp, ypool = even_pre(x, tabs, ev_g, ev_win, pw, ps, qg, ev_wq, kvg, ev_wk, ev_wv)
    o_att, lse = attn_fwd(qp, kp, vp, fwd_token)
    if late_weights is not None:
        W = {**W, **late_weights(lse)}
    x1 = even_post(x, ypool, o_att, ev_wo_pool, ev_wo_att)

    def xa_ffn_fwd(xin, l, head=()):
        mn, km, vm = mem_kv(mem, row(W['xa_norm_mem'][l]), W['xa_w_kv'][l])
        xm = xattn_fwd(xin, row(W['xa_norm_x'][l]), W['xa_w_q'][l], km, vm, W['xa_w_o'][l])
        *xo, hf, gu = (ffn_fwd_loss if head else ffn_fwd)(
            xm, row(W['ffn_norm'][l]), W['ffn_w_gate_up'], l, W['ffn_w_down'][:, l].reshape(FF_HALF, FF_CHUNK, D),
            *head)
        return xm, (xo if head else xo[0]), (mn, km, vm, hf, gu)

    x2, x3, memkv0 = xa_ffn_fwd(x1, 0)

    od_g, lam = row(W['od_norm'][0]), row(W['od_lambda'][0])
    cw, cb = W['od_conv_w'][0], row(W['od_conv_b'][0])
    wr, wi = W['od_w_rgate'][0], W['od_w_igate'][0]
    br, bi = row(W['od_b_rgate'][0]), row(W['od_b_igate'][0])
    z1, a_t, b_t, xb1, r1, ig1 = odd_pre(x3, keep, od_g, W['od_w_in'][0], cw, cb, wr, br, wi, bi, lam)
    hseq = lru_scan(a_t, b_t)
    x4 = odd_post(x3, z1, hseq, W['od_w_out'][0])
    x5, (dx, g_final, loss), memkv1 = xa_ffn_fwd(x4, 1, (target, row(W['final_norm'])))
    G['final_norm'] = g_final.reshape(D)

    gnx, gnm, gwq, gwkv, gwo, gfn, gwgu, gwd = ([None, None] for _ in range(8))

    def xa_ffn_bwd(dy, xin, xm, memkv, l):
        mn, km, vm, hf, gu = memkv
        fg = row(W['ffn_norm'][l])
        dxm, dfg, act, dgu = ffn_bwd(xm, dy, gu, fg, W['ffn_w_gate_up'], l,
                                     W['ffn_w_down'][:, l].reshape(FF_HALF, FF_CHUNK, D))
        gwd[l] = matmul_tn("ffn_dwd", act, dy).reshape(N_DEV, D_FF // N_DEV, D)
        gwgu[l] = matmul_tn("ffn_dwgu", hf, dgu)
        gfn[l] = dfg[0]
        dxin, o, dq, hx, dgx, dk, dv = xattn_bwd(xin, dxm, row(W['xa_norm_x'][l]), W['xa_w_q'][l], km, vm,
                                                  W['xa_w_o'][l])
        gnx[l] = dgx[0]
        gwo[l] = matmul_tn("xa_dwo", o, dxm)
        gwq[l] = matmul_tn("xa_dwq", hx, dq)
        dkv, dgm = mem_bwd(mem, row(W['xa_norm_mem'][l]), dk, dv, W['xa_w_kv'][l])
        gnm[l] = dgm[0]
        gwkv[l] = matmul_tn("xa_dwkv", mn, dkv)
        return dxin

    dx4 = xa_ffn_bwd(dx, x4, x5, memkv1, 1)

    dgate, dhs, g_od_wout = odd_post_bwd(dx4, z1, hseq, W['od_w_out'][0])
    G['od_w_out'] = g_od_wout[None]
    lam_grad = lru_scan(a_t, dhs, reverse=True)
    dxb, dcb, dbr, dbi, dlam, dwr, dwi = odd_gates_bwd(xb1, r1, ig1, lam_grad, hseq, keep, wr, wi, lam)
    dx3, dcw, dg_od, g_od_win = odd_pre_bwd(x3, dx4, z1, dxb, dgate, od_g, cw, W['od_w_in'][0])
    G['od_w_in'] = g_od_win[None]
    G['od_norm'], G['od_conv_w'], G['od_conv_b'] = dg_od, dcw[None], dcb
    G['od_w_rgate'], G['od_b_rgate'], G['od_w_igate'], G['od_b_igate'], G['od_lambda'] = (
        dwr[None], dbr, dwi[None], dbi, dlam)

    dx1 = xa_ffn_bwd(dx3, x1, x2, memkv0, 0)
    G['xa_norm_x'], G['xa_norm_mem'], G['ffn_norm'] = jnp.stack(gnx), jnp.stack(gnm), jnp.stack(gfn)
    G['xa_w_q'], G['xa_w_kv'], G['xa_w_o'] = jnp.stack(gwq), jnp.stack(gwkv), jnp.stack(gwo)
    G['ffn_w_gate_up'], G['ffn_w_down'] = jnp.stack(gwgu, axis=1), jnp.stack(gwd, axis=1)
    bwd_token = ship_grads(G) if ship_grads is not None else None

    dyp, do_att, delta, g_wo_pool, g_wo_att = even_post_bwd(dx1, ypool, o_att, ev_wo_pool, ev_wo_att)
    G['ev_w_out'] = jnp.concatenate([g_wo_pool, _t(_unpad_heads(_t(g_wo_att), MLA_HEADS, 64, 0))], axis=0)[None]
    dq, dk, dv = attn_bwd(qp, kp, vp, do_att, lse, delta, bwd_token)
    (grad_x, dg_ev, dpw, dps, dqg, dwq, dkvg, dwk, dwv, g_win) = even_pre_bwd(
        x, dx1, z0, dq, dk, dv, dyp, tabs, ev_g, ev_win, pw, ps, qg, ev_wq, kvg, ev_wk, ev_wv)
    G['ev_w_in'] = jnp.concatenate([g_win[:, :896], _unpad_heads(g_win[:, 896:], 1, 32, 64)], axis=1)[None]
    G['ev_norm'], G['ev_pool_w'], G['ev_pool_scale'], G['ev_q_norm'], G['ev_kv_norm'] = (
        dg_ev, dpw[None], dps, dqg, dkvg)
    G['ev_w_q_up'] = _unpad_heads(dwq, MLA_HEADS, QK_DIM, 0)[None]
    gk = _unpad_heads(dwk, MLA_HEADS, 64, 0).reshape(KV_LORA, MLA_HEADS, 64)
    gv = _unpad_heads(dwv, MLA_HEADS, 64, 0).reshape(KV_LORA, MLA_HEADS, 64)
    G['ev_w_kv_up'] = jnp.concatenate([gk, gv], axis=2).reshape(1, KV_LORA, MLA_HEADS * 128)
    return loss[0, 0], grad_x, G


def kernel(x, mem, positions, ev_norm, ev_w_in, ev_pool_w, ev_pool_scale, ev_q_norm, ev_w_q_up, ev_kv_norm, ev_w_kv_up, ev_w_out, od_norm, od_w_in, od_conv_w, od_conv_b, od_w_rgate, od_b_rgate, od_w_igate, od_b_igate, od_lambda, od_w_out, xa_norm_x, xa_norm_mem, xa_w_q, xa_w_kv, xa_w_o, ffn_norm, ffn_w_gate_up, ffn_w_down, final_norm, loss_target, m_ev_norm, m_ev_w_in, m_ev_pool_w, m_ev_pool_scale, m_ev_q_norm, m_ev_w_q_up, m_ev_kv_norm, m_ev_w_kv_up, m_ev_w_out, m_od_norm, m_od_w_in, m_od_conv_w, m_od_conv_b, m_od_w_rgate, m_od_b_rgate, m_od_w_igate, m_od_b_igate, m_od_lambda, m_od_w_out, m_xa_norm_x, m_xa_norm_mem, m_xa_w_q, m_xa_w_kv, m_xa_w_o, m_ffn_norm, m_ffn_w_gate_up, m_ffn_w_down, m_final_norm, v_ev_norm, v_ev_w_in, v_ev_pool_w, v_ev_pool_scale, v_ev_q_norm, v_ev_w_q_up, v_ev_kv_norm, v_ev_w_kv_up, v_ev_w_out, v_od_norm, v_od_w_in, v_od_conv_w, v_od_conv_b, v_od_w_rgate, v_od_b_rgate, v_od_w_igate, v_od_b_igate, v_od_lambda, v_od_w_out, v_xa_norm_x, v_xa_norm_mem, v_xa_w_q, v_xa_w_kv, v_xa_w_o, v_ffn_norm, v_ffn_w_gate_up, v_ffn_w_down, v_final_norm):
    args = dict(locals())
    w = {n: args[n] for n in WEIGHTS}
    m = {n: args['m_' + n] for n in WEIGHTS}
    v = {n: args['v_' + n] for n in WEIGHTS}
    big = [n for n in SHARDED if n not in SMALL_F32]
    small = [n for n in SHARDED if n in SMALL_F32]

    small_shapes = [w[n].shape for n in small]
    first = [n for n in big if n.startswith('ev_')]
    late = [n for n in big if n not in first]

    def full(n, st):
        return st if n in STACKED else _to_full(st, SHARD_AXIS[n])

    W = {n: w[n] for n in REPLICATED}
    gather_first = split_start("first_start", [w[n].astype(BF16) for n in first], True)

    def first_weights(after):
        srcs, lands = split_wait("first_wait", gather_first, after, True)
        return {n: full(n, _fill_own_slot(s, l, True)) for n, s, l in zip(first, srcs, lands)}

    gather = split_start("gather_start", [w[n].astype(BF16) for n in late] + [_pack_rows([w[n] for n in small])], True)

    def late_weights(after):
        srcs, lands = split_wait("gather_wait", gather, after, True)
        lands = [_fill_own_slot(s, l, True) for s, l in zip(srcs, lands)]
        out = {n: full(n, st) for n, st in zip(late, lands)}
        out.update((n, _to_full(st, SHARD_AXIS[n])) for n, st in zip(small, _unpack_rows(lands[-1], small_shapes, True)))
        return out

    def shards(G, n):
        return G[n] if n in STACKED else _to_shards(G[n], SHARD_AXIS[n])

    shipped = []

    def ship_grads(G):
        shipped.append(split_start("exchange_start", [shards(G, n).astype(BF16) for n in late] +
                                   [_pack_rows([shards(G, n) for n in small], lead=True)], False))
        return shipped[0][-1]

    loss, grad_x, G = device_step(x[0], mem[0], positions[0], loss_target[0], W, gather[-1], late_weights, ship_grads,
                                  first_weights)
    outs = [{}, {}, {}, {}]

    last_flags = (False,) * len(first) + (True,)
    last = split_start("last_start", [shards(G, n).astype(BF16) for n in first] + [_pack(
        [G[n] for n in REPLICATED] + [jnp.broadcast_to(loss, (LANES,))], F32)], last_flags)

    two_d = lambda a: a.reshape(-1, a.shape[-1])

    def update(names, parts):
        prev = None
        for n, p in zip(names, parts):
            res = adamw("adamw_" + n, p.reshape((N_DEV,) + two_d(w[n]).shape), two_d(w[n]), two_d(m[n]),
                        two_d(v[n]), prev)
            prev = res[0]
            for k in range(4):
                outs[k][n] = res[k].reshape(w[n].shape)
        return prev

    srcs, lands = split_wait("exchange_wait", shipped[0], last[-1], False)
    late_parts = [_fill_own_slot(s, l, False) for s, l in zip(srcs, lands)]
    after = update(late, late_parts)
    res = adamw("adamw_small", late_parts[-1], *[_pack_rows([d[n] for n in small]) for d in (w, m, v)])
    for k in range(4):
        outs[k].update(zip(small, _unpack_rows(res[k], small_shapes)))

    srcs, lands = split_wait("last_wait", last, after, last_flags)
    *first_parts, rep_parts = [_fill_own_slot(s, l, f) for s, l, f in zip(srcs, lands, last_flags)]
    update(first, first_parts)

    rep_shapes = [w[n].shape for n in REPLICATED] + [(LANES,)]
    zero = jnp.zeros((LANES,), F32)
    rep = adamw("adamw_rep", rep_parts, *[_pack([d[n] for n in REPLICATED] + [zero], F32) for d in (w, m, v)])
    for k in range(4):
        outs[k].update(zip(REPLICATED + ['loss'], _unpack(rep[k], rep_shapes)))
    loss = outs[0]['loss'][0]

    return (loss, grad_x[None], *[outs[0][n] for n in WEIGHTS], *[outs[1][n] for n in WEIGHTS],
            *[outs[2][n] for n in WEIGHTS], *[outs[3][n] for n in WEIGHTS])
```

```python
import functools

import jax
import jax.numpy as jnp
from jax import lax
from jax.experimental import pallas as pl
from jax.experimental.pallas import tpu as pltpu

F32, BF16 = jnp.float32, jnp.bfloat16
N_DEV = 8
D = 1024
POOL_DIM = 512
POOL_WINDOWS = (2, 4, 8, 16)
MLA_HEADS = 8
QK_DIM = 96
Q_LORA, KV_LORA = 256, 128
LRU_HEADS, LRU_HEAD_DIM = 4, 256
LRU_C = 8.0
MEM_HEADS, MEM_HEAD_DIM = 4, 256
D_FF = 2816
RMS_EPS = 1e-6
ADAM_LR, ADAM_B1, ADAM_B2, ADAM_EPS, ADAM_WD, ADAM_STEP = 0.001, 0.9, 0.999, 1e-08, 0.01, 10
LANES = 128
POOL_HALO = 16
CONV_HALO = 8
VMEM_LIMIT = 60000 * 1024


def _cp():
    return pltpu.CompilerParams(dimension_semantics=("arbitrary",), vmem_limit_bytes=VMEM_LIMIT)


def _cp2():
    return pltpu.CompilerParams(dimension_semantics=("arbitrary", "arbitrary"), vmem_limit_bytes=VMEM_LIMIT)


def _row(ts, c, col=0):
    return pl.BlockSpec((ts, c), lambda i: (i, col))


def _prev(hr, c, ts, col=0):
    r = ts // hr
    return pl.BlockSpec((hr, c), lambda i: (jnp.maximum(i * r - 1, 0), col))


def _next(hr, c, ts, n, col=0):
    r = ts // hr
    return pl.BlockSpec((hr, c), lambda i: (jnp.minimum((i + 1) * r, n * r - 1), col))


def _const(shape):
    nd = len(shape)
    return pl.BlockSpec(tuple(shape), lambda i: (0,) * nd, pipeline_mode=pl.Buffered(1))


def _acc(shape):
    nd = len(shape)
    return pl.BlockSpec(tuple(shape), lambda i: (0,) * nd)


def _sds(shape, dt):
    return jax.ShapeDtypeStruct(tuple(shape), dt)


def _dot(a, b):
    return jnp.dot(a.astype(BF16), b.astype(BF16), preferred_element_type=F32)


def _dot_nt(a, b):
    return lax.dot_general(a.astype(BF16), b.astype(BF16), (((1,), (1,)), ((), ())), preferred_element_type=F32)


def _dot_tn(a, b):
    return lax.dot_general(a.astype(BF16), b.astype(BF16), (((0,), (0,)), ((), ())), preferred_element_type=F32)


def _rms(x, g):
    rstd = lax.rsqrt(jnp.mean(x * x, axis=-1, keepdims=True) + RMS_EPS)
    return x * rstd * g, rstd


def _rms_bwd(x, g, rstd, dy):
    xn = x * rstd
    dyg = dy * g
    dx = rstd * (dyg - xn * jnp.mean(dyg * xn, axis=-1, keepdims=True))
    return dx, dy * xn


def _rowsum(v):
    return jnp.sum(v, axis=0, keepdims=True)


def _roll(v, s, axis):
    n = v.shape[axis]
    return pltpu.roll(v, s % n, axis)


def _rope(t, c, a, b):
    k = t.shape[1] // LANES
    if k > 1:
        c, a, b = (jnp.tile(v, (1, k)) for v in (c, a, b))
    return t * c + _roll(t, 16, 1) * a + _roll(t, -16, 1) * b


def _rope_bwd(d, c, a, b):
    k = d.shape[1] // LANES
    if k > 1:
        c, a, b = (jnp.tile(v, (1, k)) for v in (c, a, b))
    return d * c + _roll(d * a, -16, 1) + _roll(d * b, 16, 1)


def _gelu(x):
    c = 0.7978845608028654
    t = jnp.tanh(c * (x + 0.044715 * x * x * x))
    return 0.5 * x * (1.0 + t), t


def _gelu_grad(x, t):
    c = 0.7978845608028654
    return 0.5 * (1.0 + t) + 0.5 * x * (1.0 - t * t) * c * (1.0 + 3.0 * 0.044715 * x * x)


def _blockdot(v, w_ref, nblk, width):
    return jnp.concatenate(
        [_dot(v[:, j * width:(j + 1) * width], w_ref[j]) for j in range(nblk)], axis=1)


def _pool_cnt(row0, rows):
    t = row0 + lax.broadcasted_iota(jnp.int32, (rows, POOL_DIM), 0)
    w = jnp.left_shift(2, lax.broadcasted_iota(jnp.int32, (rows, POOL_DIM), 1) // LANES)
    return jnp.minimum(t + 1, w).astype(F32)


def _pool_windows(ext, sign):
    s2 = ext + _roll(ext, sign * 1, 0)
    t = s2[:, LANES:]
    s4 = t + _roll(t, sign * 2, 0)
    t = s4[:, LANES:]
    s8 = t + _roll(t, sign * 4, 0)
    t = s8[:, LANES:]
    s16 = t + _roll(t, sign * 8, 0)
    return jnp.concatenate([s2[:, :LANES], s4[:, :LANES], s8[:, :LANES], s16], axis=1)


def _pooled(uprev, u, row0):
    ts = u.shape[0]
    ext = jnp.concatenate([uprev, u], axis=0)
    sums = _pool_windows(ext, 1)[POOL_HALO:]
    return sums / _pool_cnt(row0, ts) - u


def _expm1(x):
    return jnp.where(jnp.abs(x) < 0.01, x * (1.0 + 0.5 * x * (1.0 + x * (1.0 / 3.0))), jnp.exp(x) - 1.0)


def _softplus(z):
    return jnp.maximum(z, 0.0) + jnp.log1p(jnp.exp(-jnp.abs(z)))


def _tile_rows(s, want):
    while s % want:
        want //= 2
    return want


def even_pre(x, tabs, g, win, pw, pscale, qg, wq, kvg, wk, wv):
    S = x.shape[0]
    ts = _tile_rows(S, 512)

    def body(x_ref, xp_ref, c_ref, a_ref, b_ref, g_ref, win_ref, pw_ref, ps_ref, qg_ref, wq_ref, kvg_ref,
             wk_ref, wv_ref, z_ref, q_ref, k_ref, v_ref, yp_ref):
        i = pl.program_id(0)
        h, _ = _rms(x_ref[...], g_ref[...])
        z = _dot(h, win_ref[...])
        z_ref[...] = z
        hp, _ = _rms(xp_ref[...], g_ref[...])
        uprev = _dot(hp, win_ref[:, :POOL_DIM]) * (i > 0).astype(F32)
        u = z[:, :POOL_DIM]
        pooled = _pooled(uprev, u, i * ts)
        yp_ref[...] = (_blockdot(pooled, pw_ref, 4, LANES) * ps_ref[...]).astype(BF16)
        c, a, b = c_ref[...], a_ref[...], b_ref[...]
        cqn, _ = _rms(z[:, 512:768], qg_ref[...])
        q_ref[...] = (_rope(_dot(cqn, wq_ref[...]), c, a, b) * (ATTN_SCALE * LOG2_E)).astype(BF16)
        ckvn, _ = _rms(z[:, 768:896], kvg_ref[...])
        krr = _rope(z[:, 896:1024], c, a, b)
        k_ref[...] = (_dot(ckvn, wk_ref[...]) + jnp.tile(krr, (1, MLA_HEADS))).astype(BF16)
        lane = lax.broadcasted_iota(jnp.int32, (ts, D), 1) % LANES
        v_ref[...] = jnp.where(lane == ONES_LANE, 1.0, _dot(ckvn, wv_ref[...])).astype(BF16)

    ins = [x, x, *tabs, g, win, pw, pscale, qg, wq, kvg, wk, wv]
    in_specs = [_row(ts, D), _prev(POOL_HALO, D, ts), _row(ts, LANES), _row(ts, LANES), _row(ts, LANES)]
    in_specs += [_const(v.shape) for v in ins[5:]]
    return pl.pallas_call(
        body, name="even_pre", grid=(S // ts,), in_specs=in_specs,
        out_specs=[_row(ts, D)] * 4 + [_row(ts, POOL_DIM)],
        out_shape=[_sds((S, D), F32)] + [_sds((S, D), BF16)] * 3 + [_sds((S, POOL_DIM), BF16)],
        compiler_params=_cp())(*ins)


ATTN_SCALE = QK_DIM ** -0.5
LOG2_E = 1.4426950408889634
LN_2 = 0.6931471805599453
ONES_LANE = 64


def _exp2(x):
    return jnp.exp2(x)


def _pair_loop(lo, hi, step, init, unrolls=(2, 1)):
    carry = init
    for unroll in unrolls:
        groups = (hi - lo) // unroll

        def group(j, c, lo=lo, unroll=unroll):
            for u in range(unroll):
                c = step(lo + unroll * j + u, c)
            return c

        carry = lax.fori_loop(0, groups, group, carry)
        lo = lo + unroll * groups
    return carry


def _as_row(col):
    return jnp.transpose(jnp.broadcast_to(col, (col.shape[0], LANES)))[0:1, :]


def _after(token):
    return ([], []) if token is None else ([token], [pl.BlockSpec(memory_space=pl.ANY)])


def attn_fwd(qp, kp, vp, token=None):
    S = qp.shape[0]
    tq = _tile_rows(S, 512)
    extra, extra_specs = _after(token)

    def body(q_ref, k_ref, v_ref, *rest):
        o_ref, lse_ref = rest[-2:]
        qi = pl.program_id(1)
        q = q_ref[...]

        def block(ki, carry, masked):
            m, acc = carry
            off = pl.multiple_of(ki * tq, tq)
            s = _dot_nt(q, k_ref[pl.ds(off, tq), :])
            if masked:
                row = lax.broadcasted_iota(jnp.int32, (tq, tq), 0)
                col = lax.broadcasted_iota(jnp.int32, (tq, tq), 1)
                s = jnp.where(col <= row, s, -1e30)
            m_new = jnp.maximum(m, jnp.max(s, axis=1, keepdims=True))
            acc = _exp2(m - m_new) * acc + _dot(_exp2(s - m_new), v_ref[pl.ds(off, tq), :])
            return m_new, acc

        init = (jnp.full((tq, 1), -1e30, F32), jnp.zeros((tq, LANES), F32))
        carry = _pair_loop(0, qi, lambda ki, c: block(ki, c, False), init, unrolls=(8, 4, 2, 1))
        m, acc = block(qi, carry, True)
        l = acc[:, ONES_LANE:ONES_LANE + 1]
        o_ref[...] = acc / l
        lse_ref[...] = _as_row(m + jnp.log(l) * LOG2_E)

    blk = pl.BlockSpec((tq, LANES), lambda h, i: (i, h))
    full = pl.BlockSpec((S, LANES), lambda h, i: (0, h))
    return pl.pallas_call(
        body, name="attn_fwd", grid=(MLA_HEADS, S // tq), in_specs=[blk, full, full] + extra_specs,
        out_specs=[blk, pl.BlockSpec((None, None, 1, tq), lambda h, i: (h, i, 0, 0))],
        out_shape=[_sds((S, D), F32), _sds((MLA_HEADS, S // tq, 1, tq), F32)], compiler_params=_cp2())(
            qp, kp, vp, *extra)


def even_post(x, ypool, o, wo_pool, wo_att):
    S = x.shape[0]
    ts = _tile_rows(S, 512)

    def body(x_ref, yp_ref, o_ref, wp_ref, wa_ref, out_ref):
        out_ref[...] = x_ref[...] + _dot(yp_ref[...], wp_ref[...]) + _dot(o_ref[...], wa_ref[...])

    return pl.pallas_call(
        body, name="even_post", grid=(S // ts,),
        in_specs=[_row(ts, D), _row(ts, POOL_DIM), _row(ts, D), _const(wo_pool.shape), _const(wo_att.shape)],
        out_specs=_row(ts, D), out_shape=_sds((S, D), F32), compiler_params=_cp())(x, ypool, o, wo_pool, wo_att)


def mem_kv(mem, g, wkv):
    M = mem.shape[0]

    def body(mem_ref, g_ref, w_ref, mn_ref, k_ref, v_ref):
        mn, _ = _rms(mem_ref[...], g_ref[...])
        mn_ref[...] = mn.astype(BF16)
        k_ref[...] = _dot(mn, w_ref[:, :D]).astype(BF16)
        v_ref[...] = _dot(mn, w_ref[:, D:]).astype(BF16)

    return pl.pallas_call(
        body, name="mem_kv", grid=(1,), in_specs=[_acc(mem.shape), _acc(g.shape), _acc(wkv.shape)],
        out_specs=[_acc((M, D))] * 3, out_shape=[_sds((M, D), BF16)] * 3, compiler_params=_cp())(mem, g, wkv)


def _xattn_heads(hx, wq_ref, k_ref, v_ref):
    q = _dot(hx, wq_ref[...])
    scale = MEM_HEAD_DIM ** -0.5
    ps, os_ = [], []
    for h in range(MEM_HEADS):
        sl = slice(h * MEM_HEAD_DIM, (h + 1) * MEM_HEAD_DIM)
        s = _dot_nt(q[:, sl], k_ref[:, sl]) * scale
        e = jnp.exp(s - jnp.max(s, axis=1, keepdims=True))
        p = e / jnp.sum(e, axis=1, keepdims=True)
        ps.append(p)
        os_.append(_dot(p, v_ref[:, sl]))
    return q, ps, jnp.concatenate(os_, axis=1)


def xattn_fwd(x, g, wq, kmem, vmem, wo):
    S = x.shape[0]
    ts = _tile_rows(S, 512)

    def body(x_ref, g_ref, wq_ref, k_ref, v_ref, wo_ref, out_ref):
        x_ = x_ref[...]
        hx, _ = _rms(x_, g_ref[...])
        _, _, o = _xattn_heads(hx, wq_ref, k_ref, v_ref)
        out_ref[...] = x_ + _dot(o, wo_ref[...])

    ins = [x, g, wq, kmem, vmem, wo]
    return pl.pallas_call(
        body, name="xattn_fwd", grid=(S // ts,), in_specs=[_row(ts, D)] + [_const(v.shape) for v in ins[1:]],
        out_specs=_row(ts, D), out_shape=_sds((S, D), F32), compiler_params=_cp())(*ins)


def xattn_bwd(x, dy, g, wq, kmem, vmem, wo):
    S = x.shape[0]
    M = kmem.shape[0]
    ts = _tile_rows(S, 512)
    scale = MEM_HEAD_DIM ** -0.5

    def body(x_ref, dy_ref, g_ref, wq_ref, k_ref, v_ref, wo_ref,
             dx_ref, o_ref, dq_ref, hx_ref, dg_ref, dk_ref, dv_ref):
        i = pl.program_id(0)

        @pl.when(i == 0)
        def _():
            dg_ref[...] = jnp.zeros_like(dg_ref)
            dk_ref[...] = jnp.zeros_like(dk_ref)
            dv_ref[...] = jnp.zeros_like(dv_ref)

        x_, dy_ = x_ref[...], dy_ref[...]
        hx, rstd = _rms(x_, g_ref[...])
        q, ps, o = _xattn_heads(hx, wq_ref, k_ref, v_ref)
        hx_ref[...] = hx.astype(BF16)
        o_ref[...] = o.astype(BF16)
        do = _dot_nt(dy_, wo_ref[...])
        dqs = []
        for h in range(MEM_HEADS):
            sl = slice(h * MEM_HEAD_DIM, (h + 1) * MEM_HEAD_DIM)
            p, do_h = ps[h], do[:, sl]
            dp = _dot_nt(do_h, v_ref[:, sl])
            ds = p * (dp - jnp.sum(p * dp, axis=1, keepdims=True)) * scale
            dqs.append(_dot(ds, k_ref[:, sl]))
            dk_ref[:, sl] += _dot_tn(ds, q[:, sl])
            dv_ref[:, sl] += _dot_tn(p, do_h)
        dq = jnp.concatenate(dqs, axis=1).astype(BF16)
        dq_ref[...] = dq
        dxn, dgr = _rms_bwd(x_, g_ref[...], rstd, _dot_nt(dq, wq_ref[...]))
        dx_ref[...] = dy_ + dxn
        dg_ref[...] += _rowsum(dgr)

    ins = [x, dy, g, wq, kmem, vmem, wo]
    return pl.pallas_call(
        body, name="xattn_bwd", grid=(S // ts,),
        in_specs=[_row(ts, D), _row(ts, D)] + [_const(v.shape) for v in ins[2:]],
        out_specs=[_row(ts, D)] * 4 + [_acc((1, D)), _acc((M, D)), _acc((M, D))],
        out_shape=[_sds((S, D), F32)] + [_sds((S, D), BF16)] * 3 + [_sds((1, D), F32), _sds((M, D), F32),
                                                                    _sds((M, D), F32)],
        compiler_params=_cp())(*ins)


def mem_bwd(mem, g, dk, dv, wkv):
    M = mem.shape[0]

    def body(mem_ref, g_ref, dk_ref, dv_ref, w_ref, dkv_ref, dg_ref):
        dkv = jnp.concatenate([dk_ref[...], dv_ref[...]], axis=1)
        dkv_ref[...] = dkv.astype(BF16)
        _, rstd = _rms(mem_ref[...], g_ref[...])
        dg_ref[...] = _rowsum(_dot_nt(dkv, w_ref[...]) * (mem_ref[...] * rstd))

    ins = [mem, g, dk, dv, wkv]
    return pl.pallas_call(
        body, name="mem_bwd", grid=(1,), in_specs=[_acc(v.shape) for v in ins],
        out_specs=[_acc((M, 2 * D)), _acc((1, D))], out_shape=[_sds((M, 2 * D), BF16), _sds((1, D), F32)],
        compiler_params=_cp())(*ins)


FF_CHUNK = 2 * D_FF // N_DEV
FF_HALF = N_DEV // 2


def _layer_of(w, layer):
    return pl.BlockSpec((N_DEV, None) + w.shape[2:], lambda i: (0, layer, 0, 0), pipeline_mode=pl.Buffered(1))


def _ff_chunks(c, ts):
    return pl.BlockSpec((c, ts, FF_CHUNK), lambda i: (0, i, 0))


def _ffn(x_, g_ref, wgu_ref, wd_ref, hf_ref, gu_ref):
    hf = _rms(x_, g_ref[...])[0].astype(BF16)
    hf_ref[...] = hf
    out = x_
    for j in range(FF_HALF):
        gg, uu = _dot(hf, wgu_ref[j]), _dot(hf, wgu_ref[j + FF_HALF])
        gu_ref[j] = gg.astype(BF16)
        gu_ref[j + FF_HALF] = uu.astype(BF16)
        out = out + _dot(gg * jax.nn.sigmoid(gg) * uu, wd_ref[j])
    return out


def ffn_fwd(x, g, wgu, layer, wd):
    S = x.shape[0]
    ts = _tile_rows(S, 256)

    def body(x_ref, g_ref, wgu_ref, wd_ref, out_ref, hf_ref, gu_ref):
        out_ref[...] = _ffn(x_ref[...], g_ref, wgu_ref, wd_ref, hf_ref, gu_ref)

    return pl.pallas_call(
        body, name="ffn_fwd", grid=(S // ts,),
        in_specs=[_row(ts, D), _const(g.shape), _layer_of(wgu, layer), _const(wd.shape)],
        out_specs=[_row(ts, D), _row(ts, D), _ff_chunks(N_DEV, ts)],
        out_shape=[_sds((S, D), F32), _sds((S, D), BF16), _sds((N_DEV, S, FF_CHUNK), BF16)],
        compiler_params=_cp())(x, g, wgu, wd)


def ffn_fwd_loss(x, g, wgu, layer, wd, target, gf):
    S = x.shape[0]
    ts = _tile_rows(S, 256)

    def body(x_ref, g_ref, wgu_ref, wd_ref, t_ref, gf_ref, dx_ref, dgf_ref, loss_ref, hf_ref, gu_ref):
        @pl.when(pl.program_id(0) == 0)
        def _():
            dgf_ref[...] = jnp.zeros_like(dgf_ref)
            loss_ref[...] = jnp.zeros_like(loss_ref)

        out = _ffn(x_ref[...], g_ref, wgu_ref, wd_ref, hf_ref, gu_ref)
        y, rstd = _rms(out, gf_ref[...])
        err = y - t_ref[...]
        loss_ref[...] += 0.5 * _rowsum(jnp.mean(err * err, axis=1, keepdims=True))
        dxn, dgr = _rms_bwd(out, gf_ref[...], rstd, err * (1.0 / D))
        dx_ref[...] = dxn
        dgf_ref[...] += _rowsum(dgr)

    return pl.pallas_call(
        body, name="ffn_fwd_loss", grid=(S // ts,),
        in_specs=[_row(ts, D), _const(g.shape), _layer_of(wgu, layer), _const(wd.shape), _row(ts, D),
                  _const(gf.shape)],
        out_specs=[_row(ts, D), _acc((1, D)), _acc((1, 1)), _row(ts, D), _ff_chunks(N_DEV, ts)],
        out_shape=[_sds((S, D), F32), _sds((1, D), F32), _sds((1, 1), F32), _sds((S, D), BF16),
                   _sds((N_DEV, S, FF_CHUNK), BF16)],
        compiler_params=_cp())(x, g, wgu, wd, target, gf)


def ffn_bwd(x, dy, gu, g, wgu, layer, wd):
    S = x.shape[0]
    ts = _tile_rows(S, 256)

    def body(x_ref, dy_ref, gu_ref, g_ref, wgu_ref, wd_ref, dx_ref, dg_ref, act_ref, dgu_ref):
        @pl.when(pl.program_id(0) == 0)
        def _():
            dg_ref[...] = jnp.zeros_like(dg_ref)

        dy_ = dy_ref[...]
        dyb = dy_.astype(BF16)
        dh = jnp.zeros((ts, D), F32)
        dacts = [_dot_nt(dyb, wd_ref[j]) for j in range(FF_HALF)]
        for j in range(FF_HALF):
            gg, uu = gu_ref[j].astype(F32), gu_ref[j + FF_HALF].astype(F32)
            sg = jax.nn.sigmoid(gg)
            silu = gg * sg
            act_ref[j] = (silu * uu).astype(BF16)
            dact = dacts[j]
            dgate = (dact * uu * (sg * (1.0 + gg * (1.0 - sg)))).astype(BF16)
            dup = (dact * silu).astype(BF16)
            dgu_ref[j] = dgate
            dgu_ref[j + FF_HALF] = dup
            dh = dh + _dot_nt(dgate, wgu_ref[j]) + _dot_nt(dup, wgu_ref[j + FF_HALF])
        x_ = x_ref[...]
        _, rstd = _rms(x_, g_ref[...])
        dxn, dgr = _rms_bwd(x_, g_ref[...], rstd, dh)
        dx_ref[...] = dy_ + dxn
        dg_ref[...] += _rowsum(dgr)

    return pl.pallas_call(
        body, name="ffn_bwd", grid=(S // ts,),
        in_specs=[_row(ts, D), _row(ts, D), _ff_chunks(N_DEV, ts), _const(g.shape), _layer_of(wgu, layer),
                  _const(wd.shape)],
        out_specs=[_row(ts, D), _acc((1, D)), _ff_chunks(FF_HALF, ts), _ff_chunks(N_DEV, ts)],
        out_shape=[_sds((S, D), F32), _sds((1, D), F32), _sds((FF_HALF, S, FF_CHUNK), BF16),
                   _sds((N_DEV, S, FF_CHUNK), BF16)],
        compiler_params=_cp())(x, dy, gu, g, wgu, wd)


def _conv_fwd(xprev, xbp, cw_ref, cb):
    ext = jnp.concatenate([xprev, xbp], axis=0)
    acc = cb + cw_ref[3:4, :] * xbp
    for k in range(3):
        acc = acc + cw_ref[k:k + 1, :] * _roll(ext, 3 - k, 0)[CONV_HALO:]
    return acc


def _decay(r, lam):
    sp = _softplus(-lam)
    log_a = -LRU_C * r * sp
    return sp, jnp.exp(log_a), jnp.sqrt(jnp.maximum(-_expm1(2.0 * log_a), 0.0))


def odd_pre(x, keep, g, win, cw, cb, wr, br, wi, bi, lam):
    S = x.shape[0]
    ts = _tile_rows(S, 512)

    def body(x_ref, xp_ref, keep_ref, g_ref, win_ref, cw_ref, cb_ref, wr_ref, br_ref, wi_ref, bi_ref, lam_ref,
             z_ref, a_ref, b_ref, xb_ref, r_ref, ig_ref):
        i = pl.program_id(0)
        h, _ = _rms(x_ref[...], g_ref[...])
        z = _dot(h, win_ref[...])
        z_ref[...] = z
        hp, _ = _rms(xp_ref[...], g_ref[...])
        xprev = _dot(hp, win_ref[:, D:]) * (i > 0).astype(F32)
        xb = _conv_fwd(xprev, z[:, D:], cw_ref, cb_ref[...])
        xb_ref[...] = xb
        r = jax.nn.sigmoid(_blockdot(xb, wr_ref, LRU_HEADS, LRU_HEAD_DIM) + br_ref[...])
        ig = jax.nn.sigmoid(_blockdot(xb, wi_ref, LRU_HEADS, LRU_HEAD_DIM) + bi_ref[...])
        r_ref[...] = r
        ig_ref[...] = ig
        keep_ = keep_ref[...]
        _, a, mult = _decay(r, lam_ref[...])
        a_ref[...] = a * keep_
        b_ref[...] = jnp.where(keep_ > 0.0, mult, 1.0) * (ig * xb)

    ins = [x, x, keep, g, win, cw, cb, wr, br, wi, bi, lam]
    return pl.pallas_call(
        body, name="odd_pre", grid=(S // ts,),
        in_specs=[_row(ts, D), _prev(CONV_HALO, D, ts), _row(ts, 1)] + [_const(v.shape) for v in ins[3:]],
        out_specs=[_row(ts, 2 * D)] + [_row(ts, D)] * 5,
        out_shape=[_sds((S, 2 * D), F32)] + [_sds((S, D), F32)] * 5, compiler_params=_cp())(*ins)


def lru_scan(a, b, reverse=False):
    S = a.shape[0]
    ts = _tile_rows(S, 512)
    n = S // ts
    groups = ts // 8

    def body(a_ref, an_ref, b_ref, h_ref, carry_ref, ash_ref):
        i = pl.program_id(0)

        @pl.when(i == 0)
        def _():
            carry_ref[...] = jnp.zeros_like(carry_ref)

        rid = lax.broadcasted_iota(jnp.int32, (8, D), 0)
        if reverse:
            ext = jnp.concatenate([a_ref[...], an_ref[...] * (i > 0).astype(F32)], axis=0)
            ash_ref[...] = _roll(ext, -1, 0)[:ts]
        src = ash_ref if reverse else a_ref

        def group(j, carry):
            off = pl.multiple_of((groups - 1 - j if reverse else j) * 8, 8)
            a8, b8 = src[pl.ds(off, 8), :], b_ref[pl.ds(off, 8), :]
            for k in (1, 2, 4):
                inside = (rid < 8 - k) if reverse else (rid >= k)
                sh = -k if reverse else k
                a_sh = jnp.where(inside, _roll(a8, sh, 0), 1.0)
                b_sh = jnp.where(inside, _roll(b8, sh, 0), 0.0)
                b8 = a8 * b_sh + b8
                a8 = a8 * a_sh
            h8 = a8 * carry + b8
            h_ref[pl.ds(off, 8), :] = h8
            return h8[0:1, :] if reverse else h8[7:8, :]

        carry_ref[...] = lax.fori_loop(0, groups, group, carry_ref[...], unroll=4)

    if reverse:
        r = ts // 8
        tile = pl.BlockSpec((ts, D), lambda i: (n - 1 - i, 0))
        halo = pl.BlockSpec((8, D), lambda i: (jnp.minimum((n - i) * r, n * r - 1), 0))
    else:
        tile, halo = _row(ts, D), _prev(8, D, ts)
    return pl.pallas_call(
        body, name="lru_scan_rev" if reverse else "lru_scan", grid=(n,), in_specs=[tile, halo, tile],
        out_specs=tile, out_shape=_sds((S, D), F32),
        scratch_shapes=[pltpu.VMEM((1, D), F32), pltpu.VMEM((ts, D), F32)], compiler_params=_cp())(a, a, b)


def odd_post(x, z, hseq, wout):
    S = x.shape[0]
    ts = _tile_rows(S, 512)

    def body(x_ref, gate_ref, h_ref, w_ref, out_ref):
        gl, _ = _gelu(gate_ref[...])
        out_ref[...] = x_ref[...] + _dot(gl * h_ref[...], w_ref[...])

    return pl.pallas_call(
        body, name="odd_post", grid=(S // ts,),
        in_specs=[_row(ts, D), _row(ts, D), _row(ts, D), _const(wout.shape)],
        out_specs=_row(ts, D), out_shape=_sds((S, D), F32), compiler_params=_cp())(x, z, hseq, wout)


def _accumulate_tn(acc_ref, out_ref, a, b, steps):
    i = pl.program_id(0)

    @pl.when(i == 0)
    def _():
        acc_ref[...] = jnp.zeros_like(acc_ref)

    acc_ref[...] += _dot_tn(a, b)

    @pl.when(i == steps - 1)
    def _():
        out_ref[...] = acc_ref[...].astype(out_ref.dtype)


def odd_post_bwd(dy, z, hseq, wout):
    S = dy.shape[0]
    ts = _tile_rows(S, 512)
    n = S // ts

    def body(dy_ref, gate_ref, h_ref, w_ref, dgate_ref, dh_ref, dw_ref, acc_ref):
        gate, hs, dy_ = gate_ref[...], h_ref[...], dy_ref[...]
        gl, t = _gelu(gate)
        dyy = _dot_nt(dy_, w_ref[...])
        dgate_ref[...] = dyy * hs * _gelu_grad(gate, t)
        dh_ref[...] = dyy * gl
        _accumulate_tn(acc_ref, dw_ref, gl * hs, dy_, n)

    return pl.pallas_call(
        body, name="odd_post_bwd", grid=(n,),
        in_specs=[_row(ts, D), _row(ts, D), _row(ts, D), _const(wout.shape)],
        out_specs=[_row(ts, D), _row(ts, D), _acc((D, D))],
        out_shape=[_sds((S, D), F32), _sds((S, D), F32), _sds((D, D), BF16)],
        scratch_shapes=[pltpu.VMEM((D, D), F32)], compiler_params=_cp())(dy, z, hseq, wout)


def odd_gates_bwd(xb, r, ig, lam_grad, hseq, keep, wr, wi, lam):
    S = xb.shape[0]
    ts = _tile_rows(S, 512)

    def body(xb_ref, r_ref, ig_ref, lg_ref, h_ref, hp_ref, keep_ref, wr_ref, wi_ref, lam_ref,
             dxb_ref, dcb_ref, dbr_ref, dbi_ref, dlam_ref, dwr_ref, dwi_ref):
        i = pl.program_id(0)

        @pl.when(i == 0)
        def _():
            for ref in (dcb_ref, dbr_ref, dbi_ref, dlam_ref, dwr_ref, dwi_ref):
                ref[...] = jnp.zeros_like(ref)

        first = (i > 0).astype(F32)
        xb, r, ig = xb_ref[...], r_ref[...], ig_ref[...]
        keep_ = keep_ref[...]
        lam_ = lam_ref[...]
        sp, a, mult = _decay(r, lam_)
        hs = h_ref[...]
        hprev = _roll(jnp.concatenate([hp_ref[...] * first, hs], axis=0), 1, 0)[CONV_HALO:]
        lg = lg_ref[...]
        da = lg * hprev * keep_
        ixb = ig * xb
        dmult = lg * ixb * keep_
        dixb = lg * jnp.where(keep_ > 0.0, mult, 1.0)
        dlog_a = da * a - dmult * jnp.where(mult > 0.0, a * a / mult, 0.0)
        dr = dlog_a * (-LRU_C * sp)
        dlam_ref[...] += _rowsum(dlog_a * (-LRU_C * r)) * (-jax.nn.sigmoid(-lam_))
        dpr = dr * r * (1.0 - r)
        dpi = dixb * xb * ig * (1.0 - ig)
        dbr_ref[...] += _rowsum(dpr)
        dbi_ref[...] += _rowsum(dpi)
        dxb = dixb * ig
        parts = []
        for h in range(LRU_HEADS):
            sl = slice(h * LRU_HEAD_DIM, (h + 1) * LRU_HEAD_DIM)
            dwr_ref[h] += _dot_tn(xb[:, sl], dpr[:, sl])
            dwi_ref[h] += _dot_tn(xb[:, sl], dpi[:, sl])
            parts.append(_dot_nt(dpr[:, sl], wr_ref[h]) + _dot_nt(dpi[:, sl], wi_ref[h]))
        dxb = dxb + jnp.concatenate(parts, axis=1)
        dxb_ref[...] = dxb
        dcb_ref[...] += _rowsum(dxb)

    ins = [xb, r, ig, lam_grad, hseq, hseq, keep, wr, wi, lam]
    in_specs = [_row(ts, D)] * 5 + [_prev(CONV_HALO, D, ts), _row(ts, 1)] + [_const(v.shape) for v in ins[7:]]
    gshape = (LRU_HEADS, LRU_HEAD_DIM, LRU_HEAD_DIM)
    return pl.pallas_call(
        body, name="odd_gates_bwd", grid=(S // ts,), in_specs=in_specs,
        out_specs=[_row(ts, D)] + [_acc((1, D))] * 4 + [_acc(gshape)] * 2,
        out_shape=[_sds((S, D), F32)] + [_sds((1, D), F32)] * 4 + [_sds(gshape, F32)] * 2,
        compiler_params=_cp())(*ins)


def odd_pre_bwd(x, dy, z, dxb, dgate, g, cw, win):
    S = x.shape[0]
    ts = _tile_rows(S, 512)
    n = S // ts

    def body(x_ref, dy_ref, xbp_ref, xbpp_ref, dxb_ref, dxbn_ref, dgate_ref, g_ref, cw_ref, win_ref,
             dx_ref, dcw_ref, dg_ref, dwin_ref, acc_ref):
        i = pl.program_id(0)

        @pl.when(i == 0)
        def _():
            dcw_ref[...] = jnp.zeros_like(dcw_ref)
            dg_ref[...] = jnp.zeros_like(dg_ref)

        dxb = dxb_ref[...]
        extd = jnp.concatenate([dxb, dxbn_ref[...] * (i < n - 1).astype(F32)], axis=0)
        extx = jnp.concatenate([xbpp_ref[...] * (i > 0).astype(F32), xbp_ref[...]], axis=0)
        dxbp = cw_ref[3:4, :] * dxb
        dcw_ref[3:4, :] += _rowsum(dxb * xbp_ref[...])
        for k in range(3):
            dxbp = dxbp + cw_ref[k:k + 1, :] * _roll(extd, -(3 - k), 0)[:ts]
            dcw_ref[k:k + 1, :] += _rowsum(dxb * _roll(extx, 3 - k, 0)[CONV_HALO:])
        dz = jnp.concatenate([dgate_ref[...], dxbp], axis=1).astype(BF16)
        x_ = x_ref[...]
        h, rstd = _rms(x_, g_ref[...])
        dxn, dgr = _rms_bwd(x_, g_ref[...], rstd, _dot_nt(dz, win_ref[...]))
        dx_ref[...] = dy_ref[...] + dxn
        dg_ref[...] += _rowsum(dgr)
        _accumulate_tn(acc_ref, dwin_ref, h, dz, n)

    ins = [x, dy, z, z, dxb, dxb, dgate, g, cw, win]
    in_specs = [_row(ts, D), _row(ts, D), _row(ts, D, 1), _prev(CONV_HALO, D, ts, 1), _row(ts, D),
                _next(CONV_HALO, D, ts, n), _row(ts, D)] + [_const(v.shape) for v in ins[7:]]
    return pl.pallas_call(
        body, name="odd_pre_bwd", grid=(n,), in_specs=in_specs,
        out_specs=[_row(ts, D), _acc((4, D)), _acc((1, D)), _acc((D, 2 * D))],
        out_shape=[_sds((S, D), F32), _sds((4, D), F32), _sds((1, D), F32), _sds((D, 2 * D), BF16)],
        scratch_shapes=[pltpu.VMEM((D, 2 * D), F32)], compiler_params=_cp())(*ins)


def even_post_bwd(dy, ypool, o, wo_pool, wo_att):
    S = dy.shape[0]
    ts = _tile_rows(S, 512)
    n = S // ts

    def body(dy_ref, yp_ref, o_ref, wp_ref, wa_ref, dyp_ref, do_ref, delta_ref, dwp_ref, dwa_ref, accp_ref,
             acca_ref):
        dy_, o_ = dy_ref[...], o_ref[...]
        dyp_ref[...] = _dot_nt(dy_, wp_ref[...])
        do = _dot_nt(dy_, wa_ref[...])
        do_ref[...] = do.astype(BF16)
        prod = do * o_
        for h in range(MLA_HEADS):
            delta_ref[h] = _as_row(jnp.sum(prod[:, h * LANES:(h + 1) * LANES], axis=1, keepdims=True))
        _accumulate_tn(accp_ref, dwp_ref, yp_ref[...], dy_, n)
        _accumulate_tn(acca_ref, dwa_ref, o_, dy_, n)

    return pl.pallas_call(
        body, name="even_post_bwd", grid=(n,),
        in_specs=[_row(ts, D), _row(ts, POOL_DIM), _row(ts, D), _const(wo_pool.shape), _const(wo_att.shape)],
        out_specs=[_row(ts, POOL_DIM), _row(ts, D),
                   pl.BlockSpec((MLA_HEADS, None, 1, ts), lambda i: (0, i, 0, 0)), _acc((POOL_DIM, D)),
                   _acc((D, D))],
        out_shape=[_sds((S, POOL_DIM), F32), _sds((S, D), BF16), _sds((MLA_HEADS, n, 1, ts), F32),
                   _sds((POOL_DIM, D), BF16), _sds((D, D), BF16)],
        scratch_shapes=[pltpu.VMEM((POOL_DIM, D), F32), pltpu.VMEM((D, D), F32)],
        compiler_params=_cp())(dy, ypool, o, wo_pool, wo_att)


def attn_bwd(qp, kp, vp, do, lse_row, delta_row, token=None):
    S = qp.shape[0]
    tk = _tile_rows(S, 512)
    nq = S // tk
    extra, extra_specs = _after(token)

    def body(q_ref, k_ref, v_ref, do_ref, lse_ref, delta_ref, *rest):
        dq_ref, dk_ref, dv_ref = rest[-3:]
        kj = pl.program_id(1)

        @pl.when(kj == 0)
        def _():
            dq_ref[...] = jnp.zeros_like(dq_ref)

        k, v = k_ref[...], v_ref[...]

        def block(qi, carry, masked):
            dk, dv = carry
            off = pl.multiple_of(qi * tk, tk)
            q = q_ref[pl.ds(off, tk), :]
            do_ = do_ref[pl.ds(off, tk), :]
            st = _dot_nt(k, q)
            if masked:
                row = lax.broadcasted_iota(jnp.int32, (tk, tk), 0)
                col = lax.broadcasted_iota(jnp.int32, (tk, tk), 1)
                st = jnp.where(col >= row, st, -1e30)
            pt = _exp2(st - lse_ref[qi])
            dv = dv + _dot(pt, do_)
            dst = (pt * (_dot_nt(v, do_) - delta_ref[qi])).astype(BF16)
            dk = dk + _dot(dst, q)
            dq_ref[pl.ds(off, tk), :] += _dot_tn(dst, k)
            return dk, dv

        zero = jnp.zeros((tk, LANES), F32)
        carry = block(kj, (zero, zero), True)
        dk, dv = _pair_loop(kj + 1, nq, lambda qi, c: block(qi, c, False), carry, unrolls=(8, 4, 2, 1))
        dk_ref[...] = dk * LN_2
        dv_ref[...] = dv

    blk = pl.BlockSpec((tk, LANES), lambda h, j: (j, h))
    full = pl.BlockSpec((S, LANES), lambda h, j: (0, h))
    rowv = pl.BlockSpec((None, nq, 1, tk), lambda h, j: (h, 0, 0, 0))
    return pl.pallas_call(
        body, name="attn_bwd", grid=(MLA_HEADS, nq), in_specs=[full, blk, blk, full, rowv, rowv] + extra_specs,
        out_specs=[full, blk, blk], out_shape=[_sds((S, D), F32)] * 3, compiler_params=_cp2())(
            qp, kp, vp, do, lse_row, delta_row, *extra)


def even_pre_bwd(x, dy, z, dq, dk, dv, dyp, tabs, g, win, pw, pscale, qg, wq, kvg, wk, wv):
    S = x.shape[0]
    ts = _tile_rows(S, 512)
    n = S // ts

    def body(x_ref, dy_ref, z_ref, up_ref, dq_ref, dk_ref, dv_ref, dyp_ref, dypn_ref, c_ref, a_ref, b_ref,
             g_ref, win_ref, pw_ref, ps_ref, qg_ref, wq_ref, kvg_ref, wk_ref, wv_ref,
             dx_ref, dg_ref, dpw_ref, dps_ref, dqg_ref, dwq_ref, dkvg_ref, dwk_ref, dwv_ref, dwin_ref, acc_ref):
        i = pl.program_id(0)

        @pl.when(i == 0)
        def _():
            for ref in (dg_ref, dpw_ref, dps_ref, dqg_ref, dwq_ref, dkvg_ref, dwk_ref, dwv_ref):
                ref[...] = jnp.zeros_like(ref)

        z = z_ref[...]
        c, a, b = c_ref[...], a_ref[...], b_ref[...]
        ps = ps_ref[...]
        u = z[:, :POOL_DIM]
        pooled = _pooled(up_ref[...] * (i > 0).astype(F32), u, i * ts)
        dyp_ = dyp_ref[...]
        dps_ref[...] += _rowsum(dyp_ * _blockdot(pooled, pw_ref, 4, LANES))
        ext = jnp.concatenate([dyp_, dypn_ref[...] * (i < n - 1).astype(F32)], axis=0) * ps
        for gidx in range(4):
            sl = slice(gidx * LANES, (gidx + 1) * LANES)
            dpw_ref[gidx] += _dot_tn(pooled[:, sl], ext[:ts, sl])
        dpooled = jnp.concatenate(
            [_dot_nt(ext[:, gidx * LANES:(gidx + 1) * LANES], pw_ref[gidx]) for gidx in range(4)], axis=1)
        dm = dpooled / _pool_cnt(i * ts, ts + POOL_HALO)
        du = _pool_windows(dm, -1)[:ts] - dpooled[:ts]
        cq = z[:, 512:768]
        cqn, rstd_q = _rms(cq, qg_ref[...])
        dqf = _rope_bwd(dq_ref[...] * ATTN_SCALE, c, a, b)
        dwq_ref[...] += _dot_tn(cqn, dqf)
        dcq, dqg_rows = _rms_bwd(cq, qg_ref[...], rstd_q, _dot_nt(dqf, wq_ref[...]))
        dqg_ref[...] += _rowsum(dqg_rows)
        ckv = z[:, 768:896]
        ckvn, rstd_kv = _rms(ckv, kvg_ref[...])
        dk_, dv_ = dk_ref[...], dv_ref[...]
        dwk_ref[...] += _dot_tn(ckvn, dk_)
        dwv_ref[...] += _dot_tn(ckvn, dv_)
        dckv, dkvg_rows = _rms_bwd(ckv, kvg_ref[...], rstd_kv,
                                   _dot_nt(dk_, wk_ref[...]) + _dot_nt(dv_, wv_ref[...]))
        dkvg_ref[...] += _rowsum(dkvg_rows)
        dkr = dk_[:, :LANES]
        for h in range(1, MLA_HEADS):
            dkr = dkr + dk_[:, h * LANES:(h + 1) * LANES]
        lane = lax.broadcasted_iota(jnp.int32, (ts, LANES), 1)
        dkr = jnp.where((lane >= 64) & (lane < 96), _rope_bwd(dkr, c, a, b), 0.0)
        dz = jnp.concatenate([du, dcq, dckv, dkr], axis=1).astype(BF16)
        x_ = x_ref[...]
        h, rstd = _rms(x_, g_ref[...])
        dxn, dgr = _rms_bwd(x_, g_ref[...], rstd, _dot_nt(dz, win_ref[...]))
        dx_ref[...] = dy_ref[...] + dxn
        dg_ref[...] += _rowsum(dgr)
        _accumulate_tn(acc_ref, dwin_ref, h, dz, n)

    ins = [x, dy, z, z, dq, dk, dv, dyp, dyp, *tabs, g, win, pw, pscale, qg, wq, kvg, wk, wv]
    in_specs = [_row(ts, D), _row(ts, D), _row(ts, D), _prev(POOL_HALO, POOL_DIM, ts), _row(ts, D), _row(ts, D),
                _row(ts, D), _row(ts, POOL_DIM), _next(POOL_HALO, POOL_DIM, ts, n), _row(ts, LANES),
                _row(ts, LANES), _row(ts, LANES)] + [_const(v.shape) for v in ins[12:]]
    acc_shapes = [(1, D), (4, LANES, LANES), (1, POOL_DIM), (1, Q_LORA), (Q_LORA, D), (1, KV_LORA), (KV_LORA, D),
                  (KV_LORA, D)]
    return pl.pallas_call(
        body, name="even_pre_bwd", grid=(n,), in_specs=in_specs,
        out_specs=[_row(ts, D)] + [_acc(s) for s in acc_shapes] + [_acc((D, D))],
        out_shape=[_sds((S, D), F32)] + [_sds(s, F32) for s in acc_shapes] + [_sds((D, D), BF16)],
        scratch_shapes=[pltpu.VMEM((D, D), F32)], compiler_params=_cp())(*ins)


def _pick(n, options):
    for o in options:
        if n % o == 0:
            return o
    return n


def matmul_tn(name, a, b):
    out_dtype = BF16
    S = a.shape[-2]
    ts = _tile_rows(S, 4096 if a.dtype.itemsize == 2 and b.dtype.itemsize == 2 else 2048)
    steps = S // ts

    def body(a_ref, b_ref, o_ref, acc_ref):
        s = pl.program_id(2)

        @pl.when(s == 0)
        def _():
            acc_ref[...] = jnp.zeros_like(acc_ref)

        acc_ref[...] += _dot_tn(a_ref[...], b_ref[...])

        @pl.when(s == steps - 1)
        def _():
            o_ref[...] = acc_ref[...].astype(o_ref.dtype)

    if a.ndim == 3:
        C, _, K = a.shape
        N = b.shape[1]
        tn = _pick(N, (1024, 512, 256, 128))
        grid = (C, N // tn, S // ts)
        in_specs = [pl.BlockSpec((None, ts, K), lambda c, j, s: (c, s, 0)),
                    pl.BlockSpec((ts, tn), lambda c, j, s: (s, j))]
        out_spec, out_shape, tile = pl.BlockSpec((None, K, tn), lambda c, j, s: (c, 0, j)), (C, K, N), (K, tn)
    elif b.ndim == 3:
        C, _, N = b.shape
        K = a.shape[1]
        tk = _pick(K, (1024, 512, 256, 128))
        grid = (C, K // tk, S // ts)
        in_specs = [pl.BlockSpec((ts, tk), lambda c, i, s: (s, i)),
                    pl.BlockSpec((None, ts, N), lambda c, i, s: (c, s, 0))]
        out_spec, out_shape, tile = pl.BlockSpec((None, tk, N), lambda c, i, s: (c, i, 0)), (C, K, N), (tk, N)
    else:
        K, N = a.shape[1], b.shape[1]
        tk = _pick(K, (1024, 512, 256, 128))
        tn = _pick(N, (1024, 512, 256, 128))
        grid = (K // tk, N // tn, S // ts)
        in_specs = [pl.BlockSpec((ts, tk), lambda i, j, s: (s, i)), pl.BlockSpec((ts, tn), lambda i, j, s: (s, j))]
        out_spec, out_shape, tile = pl.BlockSpec((tk, tn), lambda i, j, s: (i, j)), (K, N), (tk, tn)
    return pl.pallas_call(
        body, name=name, grid=grid, in_specs=in_specs, out_specs=out_spec, out_shape=_sds(out_shape, out_dtype),
        scratch_shapes=[pltpu.VMEM(tile, F32)], compiler_params=pltpu.CompilerParams(dimension_semantics=("arbitrary",) * 3, vmem_limit_bytes=VMEM_LIMIT))(
            a, b)


def _my_id():
    return lax.axis_index("x") * 4 + lax.axis_index("y") * 2 + lax.axis_index("c")


def _peer(j):
    x, y, c = lax.axis_index("x"), lax.axis_index("y"), lax.axis_index("c")
    px = 1 - x if j & 4 else x
    py = 1 - y if j & 2 else y
    pc = 1 - c if j & 1 else c
    return (px, py, pc), px * 4 + py * 2 + pc


def all_gather(name, arrays):
    n = len(arrays)

    def body(*refs):
        ins, outs = refs[:n], refs[n:2 * n]
        send_sems, recv_sems, local_sems = refs[2 * n:]
        me = _my_id()
        local = [pltpu.make_async_copy(ins[k], outs[k].at[me], local_sems.at[k]) for k in range(n)]
        for cp in local:
            cp.start()
        sends = []
        for j in range(1, N_DEV):
            peer, _ = _peer(j)
            for k in range(n):
                cp = pltpu.make_async_remote_copy(
                    src_ref=ins[k], dst_ref=outs[k].at[me], send_sem=send_sems.at[k, j - 1],
                    recv_sem=recv_sems.at[k, j - 1], device_id=peer, device_id_type=pl.DeviceIdType.MESH)
                cp.start()
                sends.append(cp)
        for j in range(1, N_DEV):
            peer, pid = _peer(j)
            for k in range(n):
                pltpu.make_async_remote_copy(
                    src_ref=ins[k], dst_ref=outs[k].at[pid], send_sem=send_sems.at[k, j - 1],
                    recv_sem=recv_sems.at[k, j - 1], device_id=peer, device_id_type=pl.DeviceIdType.MESH).wait_recv()
        for cp in sends:
            cp.wait_send()
        for cp in local:
            cp.wait()

    any_spec = pl.BlockSpec(memory_space=pl.ANY)
    return pl.pallas_call(
        body, name=name, in_specs=[any_spec] * n, out_specs=[any_spec] * n,
        out_shape=[_sds((N_DEV,) + a.shape, a.dtype) for a in arrays],
        scratch_shapes=[pltpu.SemaphoreType.DMA((n, N_DEV - 1)), pltpu.SemaphoreType.DMA((n, N_DEV - 1)),
                        pltpu.SemaphoreType.DMA((n,))],
        compiler_params=pltpu.CompilerParams(has_side_effects=True))(*arrays)


def exchange(name, arrays, gathers=()):
    n_ex, n = len(arrays), len(arrays) + len(gathers)

    def body(*refs):
        ins, outs = refs[:n], refs[n:2 * n]
        send_sems, recv_sems, local_sems = refs[2 * n:]
        me = _my_id()

        def mine(k, slot):
            return ins[k].at[slot] if k < n_ex else ins[k]

        local = [pltpu.make_async_copy(mine(k, me), outs[k].at[me], local_sems.at[k]) for k in range(n)]
        for cp in local:
            cp.start()
        sends = []
        for j in range(1, N_DEV):
            peer, pid = _peer(j)
            for k in range(n):
                cp = pltpu.make_async_remote_copy(
                    src_ref=mine(k, pid), dst_ref=outs[k].at[me], send_sem=send_sems.at[k, j - 1],
                    recv_sem=recv_sems.at[k, j - 1], device_id=peer, device_id_type=pl.DeviceIdType.MESH)
                cp.start()
                sends.append(cp)
        for j in range(1, N_DEV):
            peer, pid = _peer(j)
            for k in range(n):
                pltpu.make_async_remote_copy(
                    src_ref=mine(k, me), dst_ref=outs[k].at[pid], send_sem=send_sems.at[k, j - 1],
                    recv_sem=recv_sems.at[k, j - 1], device_id=peer, device_id_type=pl.DeviceIdType.MESH).wait_recv()
        for cp in sends:
            cp.wait_send()
        for cp in local:
            cp.wait()

    any_spec = pl.BlockSpec(memory_space=pl.ANY)
    return pl.pallas_call(
        body, name=name, in_specs=[any_spec] * n, out_specs=[any_spec] * n,
        out_shape=[_sds(a.shape, a.dtype) for a in arrays] + [_sds((N_DEV,) + a.shape, a.dtype) for a in gathers],
        scratch_shapes=[pltpu.SemaphoreType.DMA((n, N_DEV - 1)), pltpu.SemaphoreType.DMA((n, N_DEV - 1)),
                        pltpu.SemaphoreType.DMA((n,))],
        compiler_params=pltpu.CompilerParams(has_side_effects=True))(*arrays, *gathers)


_HBM = pl.BlockSpec(memory_space=pltpu.HBM)
_SEM = pl.BlockSpec(memory_space=pltpu.SEMAPHORE)
_DATAFLOW = pltpu.SideEffectType.DATAFLOW_SIDE_EFFECTING


def _in_hbm(v):
    return pltpu.with_memory_space_constraint(v, pltpu.HBM)


N_PEERS = N_DEV - 1


def _split_copy(k, j, srcs, lands, send_sems, recv_sems, gather, slot):
    peer, pid = _peer(j)
    return pltpu.make_async_remote_copy(
        src_ref=srcs[k] if _flag(gather, k) else srcs[k].at[pid],
        dst_ref=lands[k].at[_my_id() if slot == "mine" else pid],
        send_sem=send_sems[j - 1], recv_sem=recv_sems[j - 1], device_id=peer, device_id_type=pl.DeviceIdType.MESH)


def _flag(gather, k):
    return gather[k] if isinstance(gather, tuple) else gather


def split_start(name, arrays, gather):
    n = len(arrays)
    lands = [lax.empty((N_DEV,) + a.shape if _flag(gather, k) else a.shape, a.dtype) for k, a in enumerate(arrays)]

    def body(*refs):
        srcs, lnds = refs[:n], refs[n:2 * n]
        sems = refs[4 * n:4 * n + 2 * N_PEERS]
        token = refs[-1]
        for j in range(1, N_DEV):
            for k in range(n):
                _split_copy(k, j, srcs, lnds, sems[:N_PEERS], sems[N_PEERS:], gather, "mine").start()
        token[...] = jnp.zeros_like(token)

    out = pl.pallas_call(
        body, name=name,
        out_shape=(*[pltpu.HBM(a.shape, a.dtype) for a in arrays], *[pltpu.HBM(l.shape, l.dtype) for l in lands],
                   *[pltpu.SemaphoreType.DMA(())] * (2 * N_PEERS), _sds((8, LANES), F32)),
        in_specs=[_HBM] * (2 * n),
        out_specs=(*[_HBM] * (2 * n), *[_SEM] * (2 * N_PEERS), pl.BlockSpec(memory_space=pltpu.VMEM)),
        input_output_aliases={k: k for k in range(2 * n)},
        compiler_params=pltpu.CompilerParams(has_side_effects=_DATAFLOW))(
            *[_in_hbm(a) for a in arrays], *[_in_hbm(l) for l in lands])
    sems = list(out[2 * n:2 * n + 2 * N_PEERS])
    return sems[:N_PEERS], sems[N_PEERS:], list(out[:n]), list(out[n:2 * n]), out[-1]


def split_wait(name, handle, after, gather):
    send_sems, recv_sems, srcs, lands, _ = handle
    n = len(srcs)

    def body(*refs):
        srcs_r, lnds_r = refs[:n], refs[n:2 * n]
        sems = refs[2 * n:2 * n + 2 * N_PEERS]
        for j in range(1, N_DEV):
            for k in range(n):
                cp = _split_copy(k, j, srcs_r, lnds_r, sems[:N_PEERS], sems[N_PEERS:], gather, "peer")
                cp.wait_send()
                cp.wait_recv()

    out = pl.pallas_call(
        body, name=name, out_shape=tuple(pltpu.HBM(a.shape, a.dtype) for a in srcs + lands),
        in_specs=[_HBM] * (2 * n) + [_SEM] * (2 * N_PEERS) + [pl.BlockSpec(memory_space=pl.ANY)],
        out_specs=tuple([_HBM] * (2 * n)), input_output_aliases={k: k for k in range(2 * n)},
        compiler_params=pltpu.CompilerParams(has_side_effects=_DATAFLOW))(
            *srcs, *lands, *send_sems, *recv_sems, after)
    return list(out[:n]), list(out[n:])


def _fill_own_slot(src, land, gather):
    me = _my_id()
    own = src[None] if gather else lax.dynamic_index_in_dim(src, me, 0, keepdims=True)
    return lax.dynamic_update_slice_in_dim(land, own, me, 0)


ADAMW_BLOCK_ELEMS = 128 * 1024


def adamw(name, parts, w, m, v, token=None):
    R, C = w.shape
    tr = _pick(R, [t for t in (512, 256, 128, 64, 32, 16, 8) if t * C <= ADAMW_BLOCK_ELEMS])
    c1 = 1.0 - ADAM_B1 ** ADAM_STEP
    c2 = 1.0 - ADAM_B2 ** ADAM_STEP
    extra, extra_specs = _after(token)

    def body(p_ref, w_ref, m_ref, v_ref, *rest):
        g_ref, d_ref, nm_ref, nv_ref = rest[-4:]
        g = p_ref[0].astype(F32)
        for s in range(1, N_DEV):
            g = g + p_ref[s].astype(F32)
        g_ref[...] = g
        m_ = ADAM_B1 * m_ref[...] + (1.0 - ADAM_B1) * g
        v_ = ADAM_B2 * v_ref[...] + (1.0 - ADAM_B2) * (g * g)
        nm_ref[...] = m_
        nv_ref[...] = v_
        d_ref[...] = -ADAM_LR * ((m_ / c1) / (jnp.sqrt(v_ / c2) + ADAM_EPS) + ADAM_WD * w_ref[...])

    row = pl.BlockSpec((tr, C), lambda i: (i, 0))
    return pl.pallas_call(
        body, name=name, grid=(R // tr,),
        in_specs=[pl.BlockSpec((N_DEV, tr, C), lambda i: (0, i, 0)), row, row, row] + extra_specs,
        out_specs=[row] * 4, out_shape=[_sds((R, C), F32)] * 4, compiler_params=_cp())(parts, w, m, v, *extra)


WEIGHTS = ['ev_norm', 'ev_w_in', 'ev_pool_w', 'ev_pool_scale', 'ev_q_norm', 'ev_w_q_up', 'ev_kv_norm', 'ev_w_kv_up',
           'ev_w_out', 'od_norm', 'od_w_in', 'od_conv_w', 'od_conv_b', 'od_w_rgate', 'od_b_rgate', 'od_w_igate',
           'od_b_igate', 'od_lambda', 'od_w_out', 'xa_norm_x', 'xa_norm_mem', 'xa_w_q', 'xa_w_kv', 'xa_w_o',
           'ffn_norm', 'ffn_w_gate_up', 'ffn_w_down', 'final_norm']
SHARD_AXIS = {'ev_w_in': 1, 'ev_w_q_up': 2, 'ev_w_kv_up': 2, 'ev_w_out': 1, 'od_norm': 1, 'od_w_in': 2,
              'od_conv_w': 2, 'od_conv_b': 1, 'od_w_rgate': 2, 'od_b_rgate': 1, 'od_w_igate': 2, 'od_b_igate': 1,
              'od_lambda': 1, 'od_w_out': 1, 'xa_w_q': 1, 'xa_w_kv': 2, 'xa_w_o': 1, 'ffn_w_gate_up': 2,
              'ffn_w_down': 1}
SMALL_F32 = ('od_norm', 'od_conv_w', 'od_conv_b', 'od_b_rgate', 'od_b_igate', 'od_lambda')
STACKED = ('ffn_w_gate_up', 'ffn_w_down')
SHARDED = [n for n in WEIGHTS if n in SHARD_AXIS]
REPLICATED = [n for n in WEIGHTS if n not in SHARD_AXIS]
ROW_ALIGN = 512


def _pack(flats, dtype):
    v = jnp.concatenate([f.reshape(-1).astype(dtype) for f in flats])
    pad = (-v.shape[0]) % (ROW_ALIGN * LANES)
    return jnp.pad(v, (0, pad)).reshape(-1, LANES)


def _rows8(n_elems):
    return -(-n_elems // (8 * LANES)) * 8


def _pack_rows(arrays, lead=False):
    out = []
    for a in arrays:
        r = a.reshape((N_DEV, -1, LANES) if lead else (-1, LANES))
        pad = _rows8(r.shape[-2] * LANES) - r.shape[-2]
        out.append(jnp.pad(r, [(0, 0)] * (r.ndim - 2) + [(0, pad), (0, 0)]))
    return jnp.concatenate(out, axis=-2)


def _unpack_rows(buf, shapes, lead=False):
    out, off = [], 0
    for s in shapes:
        n = 1
        for d in s:
            n *= d
        rows = buf[..., off:off + n // LANES, :]
        out.append(rows.reshape(((N_DEV,) if lead else ()) + tuple(s)))
        off += _rows8(n)
    return out


def _unpack(flat, shapes):
    out, off = [], 0
    v = flat.reshape(-1)
    for s in shapes:
        n = 1
        for d in s:
            n *= d
        out.append(v[off:off + n].reshape(s))
        off += n
    return out


def _to_full(stacked, axis):
    v = jnp.moveaxis(stacked, 0, axis)
    s = v.shape
    return v.reshape(s[:axis] + (s[axis] * s[axis + 1],) + s[axis + 2:])


def _to_shards(full, axis):
    s = full.shape
    v = full.reshape(s[:axis] + (N_DEV, s[axis] // N_DEV) + s[axis + 1:])
    return jnp.moveaxis(v, axis, 0)


def _pad_heads(w, nh, dh, lead):
    s = w.shape
    v = w.reshape(s[:-1] + (nh, dh))
    v = jnp.pad(v, [(0, 0)] * (len(s) - 1) + [(0, 0), (lead, LANES - dh - lead)])
    return v.reshape(s[:-1] + (nh * LANES,))


def _unpad_heads(w, nh, dh, lead):
    s = w.shape
    return w.reshape(s[:-1] + (nh, LANES))[..., lead:lead + dh].reshape(s[:-1] + (nh * dh,))


def _rope_tables(positions):
    inv_freq = 10000.0 ** (-jnp.arange(0, 32, 2, dtype=F32) / 32)
    ang = positions.astype(F32)[:, None] * inv_freq
    cos, sin = jnp.tile(jnp.cos(ang), (1, LANES // 16)), jnp.tile(jnp.sin(ang), (1, LANES // 16))
    lane = lax.broadcasted_iota(jnp.int32, cos.shape, 1)
    c = jnp.where((lane >= 64) & (lane < 96), cos, 1.0)
    a = jnp.where((lane >= 80) & (lane < 96), sin, 0.0)
    b = jnp.where((lane >= 64) & (lane < 80), -sin, 0.0)
    return c, a, b


def _t(w):
    return jnp.swapaxes(w, -1, -2)


def device_step(x, mem, positions, target, W, fwd_token=None, late_weights=None, ship_grads=None,
                first_weights=None):
    G = {}
    tabs = _rope_tables(positions)
    keep = (positions != 0).astype(F32)[:, None]
    row = lambda v: v.reshape(1, -1)
    if first_weights is not None:
        W = {**W, **first_weights(tabs[0])}

    w_in = W['ev_w_in'][0]
    ev_win = jnp.concatenate([w_in[:, :896], _pad_heads(w_in[:, 896:], 1, 32, 64)], axis=1)
    ev_wq = _pad_heads(W['ev_w_q_up'][0], MLA_HEADS, QK_DIM, 0)
    kvw = W['ev_w_kv_up'][0].reshape(KV_LORA, MLA_HEADS, 128)
    ev_wk = _pad_heads(kvw[:, :, :64].reshape(KV_LORA, 512), MLA_HEADS, 64, 0)
    ev_wv = _pad_heads(kvw[:, :, 64:].reshape(KV_LORA, 512), MLA_HEADS, 64, 0)
    ev_wo_pool = W['ev_w_out'][0][:POOL_DIM]
    ev_wo_att = _t(_pad_heads(_t(W['ev_w_out'][0][POOL_DIM:]), MLA_HEADS, 64, 0))
    pw = W['ev_pool_w'][0].astype(BF16)
    ev_g, ps, qg, kvg = row(W['ev_norm'][0]), row(W['ev_pool_scale'][0]), row(W['ev_q_norm'][0]), row(W['ev_kv_norm'][0])

    z0, qp, kp, vp, ypool = even_pre(x, tabs, ev_g, ev_win, pw, ps, qg, ev_wq, kvg, ev_wk, ev_wv)
    o_att, lse = attn_fwd(qp, kp, vp, fwd_token)
    if late_weights is not None:
        W = {**W, **late_weights(lse)}
    x1 = even_post(x, ypool, o_att, ev_wo_pool, ev_wo_att)

    def xa_ffn_fwd(xin, l, head=()):
        mn, km, vm = mem_kv(mem, row(W['xa_norm_mem'][l]), W['xa_w_kv'][l])
        xm = xattn_fwd(xin, row(W['xa_norm_x'][l]), W['xa_w_q'][l], km, vm, W['xa_w_o'][l])
        *xo, hf, gu = (ffn_fwd_loss if head else ffn_fwd)(
            xm, row(W['ffn_norm'][l]), W['ffn_w_gate_up'], l, W['ffn_w_down'][:, l].reshape(FF_HALF, FF_CHUNK, D),
            *head)
        return xm, (xo if head else xo[0]), (mn, km, vm, hf, gu)

    x2, x3, memkv0 = xa_ffn_fwd(x1, 0)

    od_g, lam = row(W['od_norm'][0]), row(W['od_lambda'][0])
    cw, cb = W['od_conv_w'][0], row(W['od_conv_b'][0])
    wr, wi = W['od_w_rgate'][0], W['od_w_igate'][0]
    br, bi = row(W['od_b_rgate'][0]), row(W['od_b_igate'][0])
    z1, a_t, b_t, xb1, r1, ig1 = odd_pre(x3, keep, od_g, W['od_w_in'][0], cw, cb, wr, br, wi, bi, lam)
    hseq = lru_scan(a_t, b_t)
    x4 = odd_post(x3, z1, hseq, W['od_w_out'][0])
    x5, (dx, g_final, loss), memkv1 = xa_ffn_fwd(x4, 1, (target, row(W['final_norm'])))
    G['final_norm'] = g_final.reshape(D)

    gnx, gnm, gwq, gwkv, gwo, gfn, gwgu, gwd = ([None, None] for _ in range(8))

    def xa_ffn_bwd(dy, xin, xm, memkv, l):
        mn, km, vm, hf, gu = memkv
        fg = row(W['ffn_norm'][l])
        dxm, dfg, act, dgu = ffn_bwd(xm, dy, gu, fg, W['ffn_w_gate_up'], l,
                                     W['ffn_w_down'][:, l].reshape(FF_HALF, FF_CHUNK, D))
        gwd[l] = matmul_tn("ffn_dwd", act, dy).reshape(N_DEV, D_FF // N_DEV, D)
        gwgu[l] = matmul_tn("ffn_dwgu", hf, dgu)
        gfn[l] = dfg[0]
        dxin, o, dq, hx, dgx, dk, dv = xattn_bwd(xin, dxm, row(W['xa_norm_x'][l]), W['xa_w_q'][l], km, vm,
                                                  W['xa_w_o'][l])
        gnx[l] = dgx[0]
        gwo[l] = matmul_tn("xa_dwo", o, dxm)
        gwq[l] = matmul_tn("xa_dwq", hx, dq)
        dkv, dgm = mem_bwd(mem, row(W['xa_norm_mem'][l]), dk, dv, W['xa_w_kv'][l])
        gnm[l] = dgm[0]
        gwkv[l] = matmul_tn("xa_dwkv", mn, dkv)
        return dxin

    dx4 = xa_ffn_bwd(dx, x4, x5, memkv1, 1)

    dgate, dhs, g_od_wout = odd_post_bwd(dx4, z1, hseq, W['od_w_out'][0])
    G['od_w_out'] = g_od_wout[None]
    lam_grad = lru_scan(a_t, dhs, reverse=True)
    dxb, dcb, dbr, dbi, dlam, dwr, dwi = odd_gates_bwd(xb1, r1, ig1, lam_grad, hseq, keep, wr, wi, lam)
    dx3, dcw, dg_od, g_od_win = odd_pre_bwd(x3, dx4, z1, dxb, dgate, od_g, cw, W['od_w_in'][0])
    G['od_w_in'] = g_od_win[None]
    G['od_norm'], G['od_conv_w'], G['od_conv_b'] = dg_od, dcw[None], dcb
    G['od_w_rgate'], G['od_b_rgate'], G['od_w_igate'], G['od_b_igate'], G['od_lambda'] = (
        dwr[None], dbr, dwi[None], dbi, dlam)

    dx1 = xa_ffn_bwd(dx3, x1, x2, memkv0, 0)
    G['xa_norm_x'], G['xa_norm_mem'], G['ffn_norm'] = jnp.stack(gnx), jnp.stack(gnm), jnp.stack(gfn)
    G['xa_w_q'], G['xa_w_kv'], G['xa_w_o'] = jnp.stack(gwq), jnp.stack(gwkv), jnp.stack(gwo)
    G['ffn_w_gate_up'], G['ffn_w_down'] = jnp.stack(gwgu, axis=1), jnp.stack(gwd, axis=1)
    bwd_token = ship_grads(G) if ship_grads is not None else None

    dyp, do_att, delta, g_wo_pool, g_wo_att = even_post_bwd(dx1, ypool, o_att, ev_wo_pool, ev_wo_att)
    G['ev_w_out'] = jnp.concatenate([g_wo_pool, _t(_unpad_heads(_t(g_wo_att), MLA_HEADS, 64, 0))], axis=0)[None]
    dq, dk, dv = attn_bwd(qp, kp, vp, do_att, lse, delta, bwd_token)
    (grad_x, dg_ev, dpw, dps, dqg, dwq, dkvg, dwk, dwv, g_win) = even_pre_bwd(
        x, dx1, z0, dq, dk, dv, dyp, tabs, ev_g, ev_win, pw, ps, qg, ev_wq, kvg, ev_wk, ev_wv)
    G['ev_w_in'] = jnp.concatenate([g_win[:, :896], _unpad_heads(g_win[:, 896:], 1, 32, 64)], axis=1)[None]
    G['ev_norm'], G['ev_pool_w'], G['ev_pool_scale'], G['ev_q_norm'], G['ev_kv_norm'] = (
        dg_ev, dpw[None], dps, dqg, dkvg)
    G['ev_w_q_up'] = _unpad_heads(dwq, MLA_HEADS, QK_DIM, 0)[None]
    gk = _unpad_heads(dwk, MLA_HEADS, 64, 0).reshape(KV_LORA, MLA_HEADS, 64)
    gv = _unpad_heads(dwv, MLA_HEADS, 64, 0).reshape(KV_LORA, MLA_HEADS, 64)
    G['ev_w_kv_up'] = jnp.concatenate([gk, gv], axis=2).reshape(1, KV_LORA, MLA_HEADS * 128)
    return loss[0, 0], grad_x, G


def kernel(x, mem, positions, ev_norm, ev_w_in, ev_pool_w, ev_pool_scale, ev_q_norm, ev_w_q_up, ev_kv_norm, ev_w_kv_up, ev_w_out, od_norm, od_w_in, od_conv_w, od_conv_b, od_w_rgate, od_b_rgate, od_w_igate, od_b_igate, od_lambda, od_w_out, xa_norm_x, xa_norm_mem, xa_w_q, xa_w_kv, xa_w_o, ffn_norm, ffn_w_gate_up, ffn_w_down, final_norm, loss_target, m_ev_norm, m_ev_w_in, m_ev_pool_w, m_ev_pool_scale, m_ev_q_norm, m_ev_w_q_up, m_ev_kv_norm, m_ev_w_kv_up, m_ev_w_out, m_od_norm, m_od_w_in, m_od_conv_w, m_od_conv_b, m_od_w_rgate, m_od_b_rgate, m_od_w_igate, m_od_b_igate, m_od_lambda, m_od_w_out, m_xa_norm_x, m_xa_norm_mem, m_xa_w_q, m_xa_w_kv, m_xa_w_o, m_ffn_norm, m_ffn_w_gate_up, m_ffn_w_down, m_final_norm, v_ev_norm, v_ev_w_in, v_ev_pool_w, v_ev_pool_scale, v_ev_q_norm, v_ev_w_q_up, v_ev_kv_norm, v_ev_w_kv_up, v_ev_w_out, v_od_norm, v_od_w_in, v_od_conv_w, v_od_conv_b, v_od_w_rgate, v_od_b_rgate, v_od_w_igate, v_od_b_igate, v_od_lambda, v_od_w_out, v_xa_norm_x, v_xa_norm_mem, v_xa_w_q, v_xa_w_kv, v_xa_w_o, v_ffn_norm, v_ffn_w_gate_up, v_ffn_w_down, v_final_norm):
    args = dict(locals())
    w = {n: args[n] for n in WEIGHTS}
    m = {n: args['m_' + n] for n in WEIGHTS}
    v = {n: args['v_' + n] for n in WEIGHTS}
    big = [n for n in SHARDED if n not in SMALL_F32]
    small = [n for n in SHARDED if n in SMALL_F32]

    small_shapes = [w[n].shape for n in small]
    first = [n for n in big if n.startswith('ev_')]
    late = [n for n in big if n not in first]

    def full(n, st):
        return st if n in STACKED else _to_full(st, SHARD_AXIS[n])

    W = {n: w[n] for n in REPLICATED}
    gather_first = split_start("first_start", [w[n].astype(BF16) for n in first], True)

    def first_weights(after):
        srcs, lands = split_wait("first_wait", gather_first, after, True)
        return {n: full(n, _fill_own_slot(s, l, True)) for n, s, l in zip(first, srcs, lands)}

    gather = split_start("gather_start", [w[n].astype(BF16) for n in late] + [_pack_rows([w[n] for n in small])], True)

    def late_weights(after):
        srcs, lands = split_wait("gather_wait", gather, after, True)
        lands = [_fill_own_slot(s, l, True) for s, l in zip(srcs, lands)]
        out = {n: full(n, st) for n, st in zip(late, lands)}
        out.update((n, _to_full(st, SHARD_AXIS[n])) for n, st in zip(small, _unpack_rows(lands[-1], small_shapes, True)))
        return out

    def shards(G, n):
        return G[n] if n in STACKED else _to_shards(G[n], SHARD_AXIS[n])

    shipped = []

    def ship_grads(G):
        shipped.append(split_start("exchange_start", [shards(G, n).astype(BF16) for n in late] +
                                   [_pack_rows([shards(G, n) for n in small], lead=True)], False))
        return shipped[0][-1]

    loss, grad_x, G = device_step(x[0], mem[0], positions[0], loss_target[0], W, gather[-1], late_weights, ship_grads,
                                  first_weights)
    outs = [{}, {}, {}, {}]

    last_flags = (False,) * len(first) + (True,)
    last = split_start("last_start", [shards(G, n).astype(BF16) for n in first] + [_pack(
        [G[n] for n in REPLICATED] + [jnp.broadcast_to(loss, (LANES,))], F32)], last_flags)

    two_d = lambda a: a.reshape(-1, a.shape[-1])

    def update(names, parts):
        prev = None
        for n, p in zip(names, parts):
            res = adamw("adamw_" + n, p.reshape((N_DEV,) + two_d(w[n]).shape), two_d(w[n]), two_d(m[n]),
                        two_d(v[n]), prev)
            prev = res[0]
            for k in range(4):
                outs[k][n] = res[k].reshape(w[n].shape)
        return prev

    srcs, lands = split_wait("exchange_wait", shipped[0], last[-1], False)
    late_parts = [_fill_own_slot(s, l, False) for s, l in zip(srcs, lands)]
    after = update(late, late_parts)
    res = adamw("adamw_small", late_parts[-1], *[_pack_rows([d[n] for n in small]) for d in (w, m, v)])
    for k in range(4):
        outs[k].update(zip(small, _unpack_rows(res[k], small_shapes)))

    srcs, lands = split_wait("last_wait", last, after, last_flags)
    *first_parts, rep_parts = [_fill_own_slot(s, l, f) for s, l, f in zip(srcs, lands, last_flags)]
    update(first, first_parts)

    rep_shapes = [w[n].shape for n in REPLICATED] + [(LANES,)]
    zero = jnp.zeros((LANES,), F32)
    rep = adamw("adamw_rep", rep_parts, *[_pack([d[n] for n in REPLICATED] + [zero], F32) for d in (w, m, v)])
    for k in range(4):
        outs[k].update(zip(REPLICATED + ['loss'], _unpack(rep[k], rep_shapes)))
    loss = outs[0]['loss'][0]

    return (loss, grad_x[None], *[outs[0][n] for n in WEIGHTS], *[outs[1][n] for n in WEIGHTS],
            *[outs[2][n] for n in WEIGHTS], *[outs[3][n] for n in WEIGHTS])
```

```python
import functools

import jax
import jax.numpy as jnp
from jax import lax
from jax.experimental import pallas as pl
from jax.experimental.pallas import tpu as pltpu

F32, BF16 = jnp.float32, jnp.bfloat16
N_DEV = 8
D = 1024
POOL_DIM = 512
POOL_WINDOWS = (2, 4, 8, 16)
MLA_HEADS = 8
QK_DIM = 96
Q_LORA, KV_LORA = 256, 128
LRU_HEADS, LRU_HEAD_DIM = 4, 256
LRU_C = 8.0
MEM_HEADS, MEM_HEAD_DIM = 4, 256
D_FF = 2816
RMS_EPS = 1e-6
ADAM_LR, ADAM_B1, ADAM_B2, ADAM_EPS, ADAM_WD, ADAM_STEP = 0.001, 0.9, 0.999, 1e-08, 0.01, 10
LANES = 128
POOL_HALO = 16
CONV_HALO = 8
VMEM_LIMIT = 60000 * 1024


def _cp():
    return pltpu.CompilerParams(dimension_semantics=("arbitrary",), vmem_limit_bytes=VMEM_LIMIT)


def _cp2():
    return pltpu.CompilerParams(dimension_semantics=("arbitrary", "arbitrary"), vmem_limit_bytes=VMEM_LIMIT)


def _row(ts, c, col=0):
    return pl.BlockSpec((ts, c), lambda i: (i, col))


def _prev(hr, c, ts, col=0):
    r = ts // hr
    return pl.BlockSpec((hr, c), lambda i: (jnp.maximum(i * r - 1, 0), col))


def _next(hr, c, ts, n, col=0):
    r = ts // hr
    return pl.BlockSpec((hr, c), lambda i: (jnp.minimum((i + 1) * r, n * r - 1), col))


def _const(shape):
    nd = len(shape)
    return pl.BlockSpec(tuple(shape), lambda i: (0,) * nd, pipeline_mode=pl.Buffered(1))


def _acc(shape):
    nd = len(shape)
    return pl.BlockSpec(tuple(shape), lambda i: (0,) * nd)


def _sds(shape, dt):
    return jax.ShapeDtypeStruct(tuple(shape), dt)


def _dot(a, b):
    return jnp.dot(a.astype(BF16), b.astype(BF16), preferred_element_type=F32)


def _dot_nt(a, b):
    return lax.dot_general(a.astype(BF16), b.astype(BF16), (((1,), (1,)), ((), ())), preferred_element_type=F32)


def _dot_tn(a, b):
    return lax.dot_general(a.astype(BF16), b.astype(BF16), (((0,), (0,)), ((), ())), preferred_element_type=F32)


def _rms(x, g):
    rstd = lax.rsqrt(jnp.mean(x * x, axis=-1, keepdims=True) + RMS_EPS)
    return x * rstd * g, rstd


def _rms_bwd(x, g, rstd, dy):
    xn = x * rstd
    dyg = dy * g
    dx = rstd * (dyg - xn * jnp.mean(dyg * xn, axis=-1, keepdims=True))
    return dx, dy * xn


def _rowsum(v):
    return jnp.sum(v, axis=0, keepdims=True)


def _roll(v, s, axis):
    n = v.shape[axis]
    return pltpu.roll(v, s % n, axis)


def _rope(t, c, a, b):
    k = t.shape[1] // LANES
    if k > 1:
        c, a, b = (jnp.tile(v, (1, k)) for v in (c, a, b))
    return t * c + _roll(t, 16, 1) * a + _roll(t, -16, 1) * b


def _rope_bwd(d, c, a, b):
    k = d.shape[1] // LANES
    if k > 1:
        c, a, b = (jnp.tile(v, (1, k)) for v in (c, a, b))
    return d * c + _roll(d * a, -16, 1) + _roll(d * b, 16, 1)


def _gelu(x):
    c = 0.7978845608028654
    t = jnp.tanh(c * (x + 0.044715 * x * x * x))
    return 0.5 * x * (1.0 + t), t


def _gelu_grad(x, t):
    c = 0.7978845608028654
    return 0.5 * (1.0 + t) + 0.5 * x * (1.0 - t * t) * c * (1.0 + 3.0 * 0.044715 * x * x)


def _blockdot(v, w_ref, nblk, width):
    return jnp.concatenate(
        [_dot(v[:, j * width:(j + 1) * width], w_ref[j]) for j in range(nblk)], axis=1)


def _pool_cnt(row0, rows):
    t = row0 + lax.broadcasted_iota(jnp.int32, (rows, POOL_DIM), 0)
    w = jnp.left_shift(2, lax.broadcasted_iota(jnp.int32, (rows, POOL_DIM), 1) // LANES)
    return jnp.minimum(t + 1, w).astype(F32)


def _pool_windows(ext, sign):
    s2 = ext + _roll(ext, sign * 1, 0)
    t = s2[:, LANES:]
    s4 = t + _roll(t, sign * 2, 0)
    t = s4[:, LANES:]
    s8 = t + _roll(t, sign * 4, 0)
    t = s8[:, LANES:]
    s16 = t + _roll(t, sign * 8, 0)
    return jnp.concatenate([s2[:, :LANES], s4[:, :LANES], s8[:, :LANES], s16], axis=1)


def _pooled(uprev, u, row0):
    ts = u.shape[0]
    ext = jnp.concatenate([uprev, u], axis=0)
    sums = _pool_windows(ext, 1)[POOL_HALO:]
    return sums / _pool_cnt(row0, ts) - u


def _expm1(x):
    return jnp.where(jnp.abs(x) < 0.01, x * (1.0 + 0.5 * x * (1.0 + x * (1.0 / 3.0))), jnp.exp(x) - 1.0)


def _softplus(z):
    return jnp.maximum(z, 0.0) + jnp.log1p(jnp.exp(-jnp.abs(z)))


def _tile_rows(s, want):
    while s % want:
        want //= 2
    return want


def even_pre(x, tabs, g, win, pw, pscale, qg, wq, kvg, wk, wv):
    S = x.shape[0]
    ts = _tile_rows(S, 512)

    def body(x_ref, xp_ref, c_ref, a_ref, b_ref, g_ref, win_ref, pw_ref, ps_ref, qg_ref, wq_ref, kvg_ref,
             wk_ref, wv_ref, z_ref, q_ref, k_ref, v_ref, yp_ref):
        i = pl.program_id(0)
        h, _ = _rms(x_ref[...], g_ref[...])
        z = _dot(h, win_ref[...])
        z_ref[...] = z
        hp, _ = _rms(xp_ref[...], g_ref[...])
        uprev = _dot(hp, win_ref[:, :POOL_DIM]) * (i > 0).astype(F32)
        u = z[:, :POOL_DIM]
        pooled = _pooled(uprev, u, i * ts)
        yp_ref[...] = (_blockdot(pooled, pw_ref, 4, LANES) * ps_ref[...]).astype(BF16)
        c, a, b = c_ref[...], a_ref[...], b_ref[...]
        cqn, _ = _rms(z[:, 512:768], qg_ref[...])
        q_ref[...] = (_rope(_dot(cqn, wq_ref[...]), c, a, b) * (ATTN_SCALE * LOG2_E)).astype(BF16)
        ckvn, _ = _rms(z[:, 768:896], kvg_ref[...])
        krr = _rope(z[:, 896:1024], c, a, b)
        k_ref[...] = (_dot(ckvn, wk_ref[...]) + jnp.tile(krr, (1, MLA_HEADS))).astype(BF16)
        lane = lax.broadcasted_iota(jnp.int32, (ts, D), 1) % LANES
        v_ref[...] = jnp.where(lane == ONES_LANE, 1.0, _dot(ckvn, wv_ref[...])).astype(BF16)

    ins = [x, x, *tabs, g, win, pw, pscale, qg, wq, kvg, wk, wv]
    in_specs = [_row(ts, D), _prev(POOL_HALO, D, ts), _row(ts, LANES), _row(ts, LANES), _row(ts, LANES)]
    in_specs += [_const(v.shape) for v in ins[5:]]
    return pl.pallas_call(
        body, name="even_pre", grid=(S // ts,), in_specs=in_specs,
        out_specs=[_row(ts, D)] * 4 + [_row(ts, POOL_DIM)],
        out_shape=[_sds((S, D), F32)] + [_sds((S, D), BF16)] * 3 + [_sds((S, POOL_DIM), BF16)],
        compiler_params=_cp())(*ins)


ATTN_SCALE = QK_DIM ** -0.5
LOG2_E = 1.4426950408889634
LN_2 = 0.6931471805599453
ONES_LANE = 64


def _exp2(x):
    return jnp.exp2(x)


def _pair_loop(lo, hi, step, init, unrolls=(2, 1)):
    carry = init
    for unroll in unrolls:
        groups = (hi - lo) // unroll

        def group(j, c, lo=lo, unroll=unroll):
            for u in range(unroll):
                c = step(lo + unroll * j + u, c)
            return c

        carry = lax.fori_loop(0, groups, group, carry)
        lo = lo + unroll * groups
    return carry


def _as_row(col):
    return jnp.transpose(jnp.broadcast_to(col, (col.shape[0], LANES)))[0:1, :]


def _after(token):
    return ([], []) if token is None else ([token], [pl.BlockSpec(memory_space=pl.ANY)])


def attn_fwd(qp, kp, vp, token=None):
    S = qp.shape[0]
    tq = _tile_rows(S, 512)
    extra, extra_specs = _after(token)

    def body(q_ref, k_ref, v_ref, *rest):
        o_ref, lse_ref = rest[-2:]
        qi = pl.program_id(1)
        q = q_ref[...]

        def block(ki, carry, masked):
            m, acc = carry
            off = pl.multiple_of(ki * tq, tq)
            s = _dot_nt(q, k_ref[pl.ds(off, tq), :])
            if masked:
                row = lax.broadcasted_iota(jnp.int32, (tq, tq), 0)
                col = lax.broadcasted_iota(jnp.int32, (tq, tq), 1)
                s = jnp.where(col <= row, s, -1e30)
            m_new = jnp.maximum(m, jnp.max(s, axis=1, keepdims=True))
            acc = _exp2(m - m_new) * acc + _dot(_exp2(s - m_new), v_ref[pl.ds(off, tq), :])
            return m_new, acc

        init = (jnp.full((tq, 1), -1e30, F32), jnp.zeros((tq, LANES), F32))
        carry = _pair_loop(0, qi, lambda ki, c: block(ki, c, False), init, unrolls=(16, 8, 4, 2, 1))
        m, acc = block(qi, carry, True)
        l = acc[:, ONES_LANE:ONES_LANE + 1]
        o_ref[...] = acc / l
        lse_ref[...] = _as_row(m + jnp.log(l) * LOG2_E)

    blk = pl.BlockSpec((tq, LANES), lambda h, i: (i, h))
    full = pl.BlockSpec((S, LANES), lambda h, i: (0, h))
    return pl.pallas_call(
        body, name="attn_fwd", grid=(MLA_HEADS, S // tq), in_specs=[blk, full, full] + extra_specs,
        out_specs=[blk, pl.BlockSpec((None, None, 1, tq), lambda h, i: (h, i, 0, 0))],
        out_shape=[_sds((S, D), F32), _sds((MLA_HEADS, S // tq, 1, tq), F32)], compiler_params=_cp2())(
            qp, kp, vp, *extra)


def even_post(x, ypool, o, wo_pool, wo_att):
    S = x.shape[0]
    ts = _tile_rows(S, 512)

    def body(x_ref, yp_ref, o_ref, wp_ref, wa_ref, out_ref):
        out_ref[...] = x_ref[...] + _dot(yp_ref[...], wp_ref[...]) + _dot(o_ref[...], wa_ref[...])

    return pl.pallas_call(
        body, name="even_post", grid=(S // ts,),
        in_specs=[_row(ts, D), _row(ts, POOL_DIM), _row(ts, D), _const(wo_pool.shape), _const(wo_att.shape)],
        out_specs=_row(ts, D), out_shape=_sds((S, D), F32), compiler_params=_cp())(x, ypool, o, wo_pool, wo_att)


def mem_kv(mem, g, wkv):
    M = mem.shape[0]

    def body(mem_ref, g_ref, w_ref, mn_ref, k_ref, v_ref):
        mn, _ = _rms(mem_ref[...], g_ref[...])
        mn_ref[...] = mn.astype(BF16)
        k_ref[...] = _dot(mn, w_ref[:, :D]).astype(BF16)
        v_ref[...] = _dot(mn, w_ref[:, D:]).astype(BF16)

    return pl.pallas_call(
        body, name="mem_kv", grid=(1,), in_specs=[_acc(mem.shape), _acc(g.shape), _acc(wkv.shape)],
        out_specs=[_acc((M, D))] * 3, out_shape=[_sds((M, D), BF16)] * 3, compiler_params=_cp())(mem, g, wkv)


def _xattn_heads(hx, wq_ref, k_ref, v_ref):
    q = _dot(hx, wq_ref[...])
    scale = MEM_HEAD_DIM ** -0.5
    ps, os_ = [], []
    for h in range(MEM_HEADS):
        sl = slice(h * MEM_HEAD_DIM, (h + 1) * MEM_HEAD_DIM)
        s = _dot_nt(q[:, sl], k_ref[:, sl]) * scale
        e = jnp.exp(s - jnp.max(s, axis=1, keepdims=True))
        p = e / jnp.sum(e, axis=1, keepdims=True)
        ps.append(p)
        os_.append(_dot(p, v_ref[:, sl]))
    return q, ps, jnp.concatenate(os_, axis=1)


def xattn_fwd(x, g, wq, kmem, vmem, wo):
    S = x.shape[0]
    ts = _tile_rows(S, 512)

    def body(x_ref, g_ref, wq_ref, k_ref, v_ref, wo_ref, out_ref):
        x_ = x_ref[...]
        hx, _ = _rms(x_, g_ref[...])
        _, _, o = _xattn_heads(hx, wq_ref, k_ref, v_ref)
        out_ref[...] = x_ + _dot(o, wo_ref[...])

    ins = [x, g, wq, kmem, vmem, wo]
    return pl.pallas_call(
        body, name="xattn_fwd", grid=(S // ts,), in_specs=[_row(ts, D)] + [_const(v.shape) for v in ins[1:]],
        out_specs=_row(ts, D), out_shape=_sds((S, D), F32), compiler_params=_cp())(*ins)


def xattn_bwd(x, dy, g, wq, kmem, vmem, wo):
    S = x.shape[0]
    M = kmem.shape[0]
    ts = _tile_rows(S, 512)
    scale = MEM_HEAD_DIM ** -0.5

    def body(x_ref, dy_ref, g_ref, wq_ref, k_ref, v_ref, wo_ref,
             dx_ref, o_ref, dq_ref, hx_ref, dg_ref, dk_ref, dv_ref):
        i = pl.program_id(0)

        @pl.when(i == 0)
        def _():
            dg_ref[...] = jnp.zeros_like(dg_ref)
            dk_ref[...] = jnp.zeros_like(dk_ref)
            dv_ref[...] = jnp.zeros_like(dv_ref)

        x_, dy_ = x_ref[...], dy_ref[...]
        hx, rstd = _rms(x_, g_ref[...])
        q, ps, o = _xattn_heads(hx, wq_ref, k_ref, v_ref)
        hx_ref[...] = hx.astype(BF16)
        o_ref[...] = o.astype(BF16)
        do = _dot_nt(dy_, wo_ref[...])
        dqs = []
        for h in range(MEM_HEADS):
            sl = slice(h * MEM_HEAD_DIM, (h + 1) * MEM_HEAD_DIM)
            p, do_h = ps[h], do[:, sl]
            dp = _dot_nt(do_h, v_ref[:, sl])
            ds = p * (dp - jnp.sum(p * dp, axis=1, keepdims=True)) * scale
            dqs.append(_dot(ds, k_ref[:, sl]))
            dk_ref[:, sl] += _dot_tn(ds, q[:, sl])
            dv_ref[:, sl] += _dot_tn(p, do_h)
        dq = jnp.concatenate(dqs, axis=1).astype(BF16)
        dq_ref[...] = dq
        dxn, dgr = _rms_bwd(x_, g_ref[...], rstd, _dot_nt(dq, wq_ref[...]))
        dx_ref[...] = dy_ + dxn
        dg_ref[...] += _rowsum(dgr)

    ins = [x, dy, g, wq, kmem, vmem, wo]
    return pl.pallas_call(
        body, name="xattn_bwd", grid=(S // ts,),
        in_specs=[_row(ts, D), _row(ts, D)] + [_const(v.shape) for v in ins[2:]],
        out_specs=[_row(ts, D)] * 4 + [_acc((1, D)), _acc((M, D)), _acc((M, D))],
        out_shape=[_sds((S, D), F32)] + [_sds((S, D), BF16)] * 3 + [_sds((1, D), F32), _sds((M, D), F32),
                                                                    _sds((M, D), F32)],
        compiler_params=_cp())(*ins)


def mem_bwd(mem, g, dk, dv, wkv):
    M = mem.shape[0]

    def body(mem_ref, g_ref, dk_ref, dv_ref, w_ref, dkv_ref, dg_ref):
        dkv = jnp.concatenate([dk_ref[...], dv_ref[...]], axis=1)
        dkv_ref[...] = dkv.astype(BF16)
        _, rstd = _rms(mem_ref[...], g_ref[...])
        dg_ref[...] = _rowsum(_dot_nt(dkv, w_ref[...]) * (mem_ref[...] * rstd))

    ins = [mem, g, dk, dv, wkv]
    return pl.pallas_call(
        body, name="mem_bwd", grid=(1,), in_specs=[_acc(v.shape) for v in ins],
        out_specs=[_acc((M, 2 * D)), _acc((1, D))], out_shape=[_sds((M, 2 * D), BF16), _sds((1, D), F32)],
        compiler_params=_cp())(*ins)


FF_CHUNK = 2 * D_FF // N_DEV
FF_HALF = N_DEV // 2


def _layer_of(w, layer):
    return pl.BlockSpec((N_DEV, None) + w.shape[2:], lambda i: (0, layer, 0, 0), pipeline_mode=pl.Buffered(1))


def _ff_chunks(c, ts):
    return pl.BlockSpec((c, ts, FF_CHUNK), lambda i: (0, i, 0))


def _ffn(x_, g_ref, wgu_ref, wd_ref, hf_ref, gu_ref):
    hf = _rms(x_, g_ref[...])[0].astype(BF16)
    hf_ref[...] = hf
    out = x_
    for j in range(FF_HALF):
        gg, uu = _dot(hf, wgu_ref[j]), _dot(hf, wgu_ref[j + FF_HALF])
        gu_ref[j] = gg.astype(BF16)
        gu_ref[j + FF_HALF] = uu.astype(BF16)
        out = out + _dot(gg * jax.nn.sigmoid(gg) * uu, wd_ref[j])
    return out


def ffn_fwd(x, g, wgu, layer, wd):
    S = x.shape[0]
    ts = _tile_rows(S, 256)

    def body(x_ref, g_ref, wgu_ref, wd_ref, out_ref, hf_ref, gu_ref):
        out_ref[...] = _ffn(x_ref[...], g_ref, wgu_ref, wd_ref, hf_ref, gu_ref)

    return pl.pallas_call(
        body, name="ffn_fwd", grid=(S // ts,),
        in_specs=[_row(ts, D), _const(g.shape), _layer_of(wgu, layer), _const(wd.shape)],
        out_specs=[_row(ts, D), _row(ts, D), _ff_chunks(N_DEV, ts)],
        out_shape=[_sds((S, D), F32), _sds((S, D), BF16), _sds((N_DEV, S, FF_CHUNK), BF16)],
        compiler_params=_cp())(x, g, wgu, wd)


def ffn_fwd_loss(x, g, wgu, layer, wd, target, gf):
    S = x.shape[0]
    ts = _tile_rows(S, 256)

    def body(x_ref, g_ref, wgu_ref, wd_ref, t_ref, gf_ref, dx_ref, dgf_ref, loss_ref, hf_ref, gu_ref):
        @pl.when(pl.program_id(0) == 0)
        def _():
            dgf_ref[...] = jnp.zeros_like(dgf_ref)
            loss_ref[...] = jnp.zeros_like(loss_ref)

        out = _ffn(x_ref[...], g_ref, wgu_ref, wd_ref, hf_ref, gu_ref)
        y, rstd = _rms(out, gf_ref[...])
        err = y - t_ref[...]
        loss_ref[...] += 0.5 * _rowsum(jnp.mean(err * err, axis=1, keepdims=True))
        dxn, dgr = _rms_bwd(out, gf_ref[...], rstd, err * (1.0 / D))
        dx_ref[...] = dxn
        dgf_ref[...] += _rowsum(dgr)

    return pl.pallas_call(
        body, name="ffn_fwd_loss", grid=(S // ts,),
        in_specs=[_row(ts, D), _const(g.shape), _layer_of(wgu, layer), _const(wd.shape), _row(ts, D),
                  _const(gf.shape)],
        out_specs=[_row(ts, D), _acc((1, D)), _acc((1, 1)), _row(ts, D), _ff_chunks(N_DEV, ts)],
        out_shape=[_sds((S, D), F32), _sds((1, D), F32), _sds((1, 1), F32), _sds((S, D), BF16),
                   _sds((N_DEV, S, FF_CHUNK), BF16)],
        compiler_params=_cp())(x, g, wgu, wd, target, gf)


def ffn_bwd(x, dy, gu, g, wgu, layer, wd):
    S = x.shape[0]
    ts = _tile_rows(S, 256)

    def body(x_ref, dy_ref, gu_ref, g_ref, wgu_ref, wd_ref, dx_ref, dg_ref, act_ref, dgu_ref):
        @pl.when(pl.program_id(0) == 0)
        def _():
            dg_ref[...] = jnp.zeros_like(dg_ref)

        dy_ = dy_ref[...]
        dyb = dy_.astype(BF16)
        dh = jnp.zeros((ts, D), F32)
        dacts = [_dot_nt(dyb, wd_ref[j]) for j in range(FF_HALF)]
        for j in range(FF_HALF):
            gg, uu = gu_ref[j].astype(F32), gu_ref[j + FF_HALF].astype(F32)
            sg = jax.nn.sigmoid(gg)
            silu = gg * sg
            act_ref[j] = (silu * uu).astype(BF16)
            dact = dacts[j]
            dgate = (dact * uu * (sg * (1.0 + gg * (1.0 - sg)))).astype(BF16)
            dup = (dact * silu).astype(BF16)
            dgu_ref[j] = dgate
            dgu_ref[j + FF_HALF] = dup
            dh = dh + _dot_nt(dgate, wgu_ref[j]) + _dot_nt(dup, wgu_ref[j + FF_HALF])
        x_ = x_ref[...]
        _, rstd = _rms(x_, g_ref[...])
        dxn, dgr = _rms_bwd(x_, g_ref[...], rstd, dh)
        dx_ref[...] = dy_ + dxn
        dg_ref[...] += _rowsum(dgr)

    return pl.pallas_call(
        body, name="ffn_bwd", grid=(S // ts,),
        in_specs=[_row(ts, D), _row(ts, D), _ff_chunks(N_DEV, ts), _const(g.shape), _layer_of(wgu, layer),
                  _const(wd.shape)],
        out_specs=[_row(ts, D), _acc((1, D)), _ff_chunks(FF_HALF, ts), _ff_chunks(N_DEV, ts)],
        out_shape=[_sds((S, D), F32), _sds((1, D), F32), _sds((FF_HALF, S, FF_CHUNK), BF16),
                   _sds((N_DEV, S, FF_CHUNK), BF16)],
        compiler_params=_cp())(x, dy, gu, g, wgu, wd)


def _conv_fwd(xprev, xbp, cw_ref, cb):
    ext = jnp.concatenate([xprev, xbp], axis=0)
    acc = cb + cw_ref[3:4, :] * xbp
    for k in range(3):
        acc = acc + cw_ref[k:k + 1, :] * _roll(ext, 3 - k, 0)[CONV_HALO:]
    return acc


def _decay(r, lam):
    sp = _softplus(-lam)
    log_a = -LRU_C * r * sp
    return sp, jnp.exp(log_a), jnp.sqrt(jnp.maximum(-_expm1(2.0 * log_a), 0.0))


def odd_pre(x, keep, g, win, cw, cb, wr, br, wi, bi, lam):
    S = x.shape[0]
    ts = _tile_rows(S, 512)

    def body(x_ref, xp_ref, keep_ref, g_ref, win_ref, cw_ref, cb_ref, wr_ref, br_ref, wi_ref, bi_ref, lam_ref,
             z_ref, a_ref, b_ref, xb_ref, r_ref, ig_ref):
        i = pl.program_id(0)
        h, _ = _rms(x_ref[...], g_ref[...])
        z = _dot(h, win_ref[...])
        z_ref[...] = z
        hp, _ = _rms(xp_ref[...], g_ref[...])
        xprev = _dot(hp, win_ref[:, D:]) * (i > 0).astype(F32)
        xb = _conv_fwd(xprev, z[:, D:], cw_ref, cb_ref[...])
        xb_ref[...] = xb
        r = jax.nn.sigmoid(_blockdot(xb, wr_ref, LRU_HEADS, LRU_HEAD_DIM) + br_ref[...])
        ig = jax.nn.sigmoid(_blockdot(xb, wi_ref, LRU_HEADS, LRU_HEAD_DIM) + bi_ref[...])
        r_ref[...] = r
        ig_ref[...] = ig
        keep_ = keep_ref[...]
        _, a, mult = _decay(r, lam_ref[...])
        a_ref[...] = a * keep_
        b_ref[...] = jnp.where(keep_ > 0.0, mult, 1.0) * (ig * xb)

    ins = [x, x, keep, g, win, cw, cb, wr, br, wi, bi, lam]
    return pl.pallas_call(
        body, name="odd_pre", grid=(S // ts,),
        in_specs=[_row(ts, D), _prev(CONV_HALO, D, ts), _row(ts, 1)] + [_const(v.shape) for v in ins[3:]],
        out_specs=[_row(ts, 2 * D)] + [_row(ts, D)] * 5,
        out_shape=[_sds((S, 2 * D), F32)] + [_sds((S, D), F32)] * 5, compiler_params=_cp())(*ins)


def lru_scan(a, b, reverse=False):
    S = a.shape[0]
    ts = _tile_rows(S, 512)
    n = S // ts
    groups = ts // 8

    def body(a_ref, an_ref, b_ref, h_ref, carry_ref, ash_ref):
        i = pl.program_id(0)

        @pl.when(i == 0)
        def _():
            carry_ref[...] = jnp.zeros_like(carry_ref)

        rid = lax.broadcasted_iota(jnp.int32, (8, D), 0)
        if reverse:
            ext = jnp.concatenate([a_ref[...], an_ref[...] * (i > 0).astype(F32)], axis=0)
            ash_ref[...] = _roll(ext, -1, 0)[:ts]
        src = ash_ref if reverse else a_ref

        def group(j, carry):
            off = pl.multiple_of((groups - 1 - j if reverse else j) * 8, 8)
            a8, b8 = src[pl.ds(off, 8), :], b_ref[pl.ds(off, 8), :]
            for k in (1, 2, 4):
                inside = (rid < 8 - k) if reverse else (rid >= k)
                sh = -k if reverse else k
                a_sh = jnp.where(inside, _roll(a8, sh, 0), 1.0)
                b_sh = jnp.where(inside, _roll(b8, sh, 0), 0.0)
                b8 = a8 * b_sh + b8
                a8 = a8 * a_sh
            h8 = a8 * carry + b8
            h_ref[pl.ds(off, 8), :] = h8
            return h8[0:1, :] if reverse else h8[7:8, :]

        carry_ref[...] = lax.fori_loop(0, groups, group, carry_ref[...], unroll=4)

    if reverse:
        r = ts // 8
        tile = pl.BlockSpec((ts, D), lambda i: (n - 1 - i, 0))
        halo = pl.BlockSpec((8, D), lambda i: (jnp.minimum((n - i) * r, n * r - 1), 0))
    else:
        tile, halo = _row(ts, D), _prev(8, D, ts)
    return pl.pallas_call(
        body, name="lru_scan_rev" if reverse else "lru_scan", grid=(n,), in_specs=[tile, halo, tile],
        out_specs=tile, out_shape=_sds((S, D), F32),
        scratch_shapes=[pltpu.VMEM((1, D), F32), pltpu.VMEM((ts, D), F32)], compiler_params=_cp())(a, a, b)


def odd_post(x, z, hseq, wout):
    S = x.shape[0]
    ts = _tile_rows(S, 512)

    def body(x_ref, gate_ref, h_ref, w_ref, out_ref):
        gl, _ = _gelu(gate_ref[...])
        out_ref[...] = x_ref[...] + _dot(gl * h_ref[...], w_ref[...])

    return pl.pallas_call(
        body, name="odd_post", grid=(S // ts,),
        in_specs=[_row(ts, D), _row(ts, D), _row(ts, D), _const(wout.shape)],
        out_specs=_row(ts, D), out_shape=_sds((S, D), F32), compiler_params=_cp())(x, z, hseq, wout)


def _accumulate_tn(acc_ref, out_ref, a, b, steps):
    i = pl.program_id(0)

    @pl.when(i == 0)
    def _():
        acc_ref[...] = jnp.zeros_like(acc_ref)

    acc_ref[...] += _dot_tn(a, b)

    @pl.when(i == steps - 1)
    def _():
        out_ref[...] = acc_ref[...].astype(out_ref.dtype)


def odd_post_bwd(dy, z, hseq, wout):
    S = dy.shape[0]
    ts = _tile_rows(S, 512)
    n = S // ts

    def body(dy_ref, gate_ref, h_ref, w_ref, dgate_ref, dh_ref, dw_ref, acc_ref):
        gate, hs, dy_ = gate_ref[...], h_ref[...], dy_ref[...]
        gl, t = _gelu(gate)
        dyy = _dot_nt(dy_, w_ref[...])
        dgate_ref[...] = dyy * hs * _gelu_grad(gate, t)
        dh_ref[...] = dyy * gl
        _accumulate_tn(acc_ref, dw_ref, gl * hs, dy_, n)

    return pl.pallas_call(
        body, name="odd_post_bwd", grid=(n,),
        in_specs=[_row(ts, D), _row(ts, D), _row(ts, D), _const(wout.shape)],
        out_specs=[_row(ts, D), _row(ts, D), _acc((D, D))],
        out_shape=[_sds((S, D), F32), _sds((S, D), F32), _sds((D, D), BF16)],
        scratch_shapes=[pltpu.VMEM((D, D), F32)], compiler_params=_cp())(dy, z, hseq, wout)


def odd_gates_bwd(xb, r, ig, lam_grad, hseq, keep, wr, wi, lam):
    S = xb.shape[0]
    ts = _tile_rows(S, 512)

    def body(xb_ref, r_ref, ig_ref, lg_ref, h_ref, hp_ref, keep_ref, wr_ref, wi_ref, lam_ref,
             dxb_ref, dcb_ref, dbr_ref, dbi_ref, dlam_ref, dwr_ref, dwi_ref):
        i = pl.program_id(0)

        @pl.when(i == 0)
        def _():
            for ref in (dcb_ref, dbr_ref, dbi_ref, dlam_ref, dwr_ref, dwi_ref):
                ref[...] = jnp.zeros_like(ref)

        first = (i > 0).astype(F32)
        xb, r, ig = xb_ref[...], r_ref[...], ig_ref[...]
        keep_ = keep_ref[...]
        lam_ = lam_ref[...]
        sp, a, mult = _decay(r, lam_)
        hs = h_ref[...]
        hprev = _roll(jnp.concatenate([hp_ref[...] * first, hs], axis=0), 1, 0)[CONV_HALO:]
        lg = lg_ref[...]
        da = lg * hprev * keep_
        ixb = ig * xb
        dmult = lg * ixb * keep_
        dixb = lg * jnp.where(keep_ > 0.0, mult, 1.0)
        dlog_a = da * a - dmult * jnp.where(mult > 0.0, a * a / mult, 0.0)
        dr = dlog_a * (-LRU_C * sp)
        dlam_ref[...] += _rowsum(dlog_a * (-LRU_C * r)) * (-jax.nn.sigmoid(-lam_))
        dpr = dr * r * (1.0 - r)
        dpi = dixb * xb * ig * (1.0 - ig)
        dbr_ref[...] += _rowsum(dpr)
        dbi_ref[...] += _rowsum(dpi)
        dxb = dixb * ig
        parts = []
        for h in range(LRU_HEADS):
            sl = slice(h * LRU_HEAD_DIM, (h + 1) * LRU_HEAD_DIM)
            dwr_ref[h] += _dot_tn(xb[:, sl], dpr[:, sl])
            dwi_ref[h] += _dot_tn(xb[:, sl], dpi[:, sl])
            parts.append(_dot_nt(dpr[:, sl], wr_ref[h]) + _dot_nt(dpi[:, sl], wi_ref[h]))
        dxb = dxb + jnp.concatenate(parts, axis=1)
        dxb_ref[...] = dxb
        dcb_ref[...] += _rowsum(dxb)

    ins = [xb, r, ig, lam_grad, hseq, hseq, keep, wr, wi, lam]
    in_specs = [_row(ts, D)] * 5 + [_prev(CONV_HALO, D, ts), _row(ts, 1)] + [_const(v.shape) for v in ins[7:]]
    gshape = (LRU_HEADS, LRU_HEAD_DIM, LRU_HEAD_DIM)
    return pl.pallas_call(
        body, name="odd_gates_bwd", grid=(S // ts,), in_specs=in_specs,
        out_specs=[_row(ts, D)] + [_acc((1, D))] * 4 + [_acc(gshape)] * 2,
        out_shape=[_sds((S, D), F32)] + [_sds((1, D), F32)] * 4 + [_sds(gshape, F32)] * 2,
        compiler_params=_cp())(*ins)


def odd_pre_bwd(x, dy, z, dxb, dgate, g, cw, win):
    S = x.shape[0]
    ts = _tile_rows(S, 512)
    n = S // ts

    def body(x_ref, dy_ref, xbp_ref, xbpp_ref, dxb_ref, dxbn_ref, dgate_ref, g_ref, cw_ref, win_ref,
             dx_ref, dcw_ref, dg_ref, dwin_ref, acc_ref):
        i = pl.program_id(0)

        @pl.when(i == 0)
        def _():
            dcw_ref[...] = jnp.zeros_like(dcw_ref)
            dg_ref[...] = jnp.zeros_like(dg_ref)

        dxb = dxb_ref[...]
        extd = jnp.concatenate([dxb, dxbn_ref[...] * (i < n - 1).astype(F32)], axis=0)
        extx = jnp.concatenate([xbpp_ref[...] * (i > 0).astype(F32), xbp_ref[...]], axis=0)
        dxbp = cw_ref[3:4, :] * dxb
        dcw_ref[3:4, :] += _rowsum(dxb * xbp_ref[...])
        for k in range(3):
            dxbp = dxbp + cw_ref[k:k + 1, :] * _roll(extd, -(3 - k), 0)[:ts]
            dcw_ref[k:k + 1, :] += _rowsum(dxb * _roll(extx, 3 - k, 0)[CONV_HALO:])
        dz = jnp.concatenate([dgate_ref[...], dxbp], axis=1).astype(BF16)
        x_ = x_ref[...]
        h, rstd = _rms(x_, g_ref[...])
        dxn, dgr = _rms_bwd(x_, g_ref[...], rstd, _dot_nt(dz, win_ref[...]))
        dx_ref[...] = dy_ref[...] + dxn
        dg_ref[...] += _rowsum(dgr)
        _accumulate_tn(acc_ref, dwin_ref, h, dz, n)

    ins = [x, dy, z, z, dxb, dxb, dgate, g, cw, win]
    in_specs = [_row(ts, D), _row(ts, D), _row(ts, D, 1), _prev(CONV_HALO, D, ts, 1), _row(ts, D),
                _next(CONV_HALO, D, ts, n), _row(ts, D)] + [_const(v.shape) for v in ins[7:]]
    return pl.pallas_call(
        body, name="odd_pre_bwd", grid=(n,), in_specs=in_specs,
        out_specs=[_row(ts, D), _acc((4, D)), _acc((1, D)), _acc((D, 2 * D))],
        out_shape=[_sds((S, D), F32), _sds((4, D), F32), _sds((1, D), F32), _sds((D, 2 * D), BF16)],
        scratch_shapes=[pltpu.VMEM((D, 2 * D), F32)], compiler_params=_cp())(*ins)


def even_post_bwd(dy, ypool, o, wo_pool, wo_att):
    S = dy.shape[0]
    ts = _tile_rows(S, 512)
    n = S // ts

    def body(dy_ref, yp_ref, o_ref, wp_ref, wa_ref, dyp_ref, do_ref, delta_ref, dwp_ref, dwa_ref, accp_ref,
             acca_ref):
        dy_, o_ = dy_ref[...], o_ref[...]
        dyp_ref[...] = _dot_nt(dy_, wp_ref[...])
        do = _dot_nt(dy_, wa_ref[...])
        do_ref[...] = do.astype(BF16)
        prod = do * o_
        for h in range(MLA_HEADS):
            delta_ref[h] = _as_row(jnp.sum(prod[:, h * LANES:(h + 1) * LANES], axis=1, keepdims=True))
        _accumulate_tn(accp_ref, dwp_ref, yp_ref[...], dy_, n)
        _accumulate_tn(acca_ref, dwa_ref, o_, dy_, n)

    return pl.pallas_call(
        body, name="even_post_bwd", grid=(n,),
        in_specs=[_row(ts, D), _row(ts, POOL_DIM), _row(ts, D), _const(wo_pool.shape), _const(wo_att.shape)],
        out_specs=[_row(ts, POOL_DIM), _row(ts, D),
                   pl.BlockSpec((MLA_HEADS, None, 1, ts), lambda i: (0, i, 0, 0)), _acc((POOL_DIM, D)),
                   _acc((D, D))],
        out_shape=[_sds((S, POOL_DIM), F32), _sds((S, D), BF16), _sds((MLA_HEADS, n, 1, ts), F32),
                   _sds((POOL_DIM, D), BF16), _sds((D, D), BF16)],
        scratch_shapes=[pltpu.VMEM((POOL_DIM, D), F32), pltpu.VMEM((D, D), F32)],
        compiler_params=_cp())(dy, ypool, o, wo_pool, wo_att)


def attn_bwd(qp, kp, vp, do, lse_row, delta_row, token=None):
    S = qp.shape[0]
    tk = _tile_rows(S, 512)
    nq = S // tk
    extra, extra_specs = _after(token)

    def body(q_ref, k_ref, v_ref, do_ref, lse_ref, delta_ref, *rest):
        dq_ref, dk_ref, dv_ref = rest[-3:]
        kj = pl.program_id(1)

        @pl.when(kj == 0)
        def _():
            dq_ref[...] = jnp.zeros_like(dq_ref)

        k, v = k_ref[...], v_ref[...]

        def block(qi, carry, masked):
            dk, dv = carry
            off = pl.multiple_of(qi * tk, tk)
            q = q_ref[pl.ds(off, tk), :]
            do_ = do_ref[pl.ds(off, tk), :]
            st = _dot_nt(k, q)
            if masked:
                row = lax.broadcasted_iota(jnp.int32, (tk, tk), 0)
                col = lax.broadcasted_iota(jnp.int32, (tk, tk), 1)
                st = jnp.where(col >= row, st, -1e30)
            pt = _exp2(st - lse_ref[qi])
            dv = dv + _dot(pt, do_)
            dst = (pt * (_dot_nt(v, do_) - delta_ref[qi])).astype(BF16)
            dk = dk + _dot(dst, q)
            dq_ref[pl.ds(off, tk), :] += _dot_tn(dst, k)
            return dk, dv

        zero = jnp.zeros((tk, LANES), F32)
        carry = block(kj, (zero, zero), True)
        dk, dv = _pair_loop(kj + 1, nq, lambda qi, c: block(qi, c, False), carry, unrolls=(8, 4, 2, 1))
        dk_ref[...] = dk * LN_2
        dv_ref[...] = dv

    blk = pl.BlockSpec((tk, LANES), lambda h, j: (j, h))
    full = pl.BlockSpec((S, LANES), lambda h, j: (0, h))
    rowv = pl.BlockSpec((None, nq, 1, tk), lambda h, j: (h, 0, 0, 0))
    return pl.pallas_call(
        body, name="attn_bwd", grid=(MLA_HEADS, nq), in_specs=[full, blk, blk, full, rowv, rowv] + extra_specs,
        out_specs=[full, blk, blk], out_shape=[_sds((S, D), F32)] * 3, compiler_params=_cp2())(
            qp, kp, vp, do, lse_row, delta_row, *extra)


def even_pre_bwd(x, dy, z, dq, dk, dv, dyp, tabs, g, win, pw, pscale, qg, wq, kvg, wk, wv):
    S = x.shape[0]
    ts = _tile_rows(S, 512)
    n = S // ts

    def body(x_ref, dy_ref, z_ref, up_ref, dq_ref, dk_ref, dv_ref, dyp_ref, dypn_ref, c_ref, a_ref, b_ref,
             g_ref, win_ref, pw_ref, ps_ref, qg_ref, wq_ref, kvg_ref, wk_ref, wv_ref,
             dx_ref, dg_ref, dpw_ref, dps_ref, dqg_ref, dwq_ref, dkvg_ref, dwk_ref, dwv_ref, dwin_ref, acc_ref):
        i = pl.program_id(0)

        @pl.when(i == 0)
        def _():
            for ref in (dg_ref, dpw_ref, dps_ref, dqg_ref, dwq_ref, dkvg_ref, dwk_ref, dwv_ref):
                ref[...] = jnp.zeros_like(ref)

        z = z_ref[...]
        c, a, b = c_ref[...], a_ref[...], b_ref[...]
        ps = ps_ref[...]
        u = z[:, :POOL_DIM]
        pooled = _pooled(up_ref[...] * (i > 0).astype(F32), u, i * ts)
        dyp_ = dyp_ref[...]
        dps_ref[...] += _rowsum(dyp_ * _blockdot(pooled, pw_ref, 4, LANES))
        ext = jnp.concatenate([dyp_, dypn_ref[...] * (i < n - 1).astype(F32)], axis=0) * ps
        for gidx in range(4):
            sl = slice(gidx * LANES, (gidx + 1) * LANES)
            dpw_ref[gidx] += _dot_tn(pooled[:, sl], ext[:ts, sl])
        dpooled = jnp.concatenate(
            [_dot_nt(ext[:, gidx * LANES:(gidx + 1) * LANES], pw_ref[gidx]) for gidx in range(4)], axis=1)
        dm = dpooled / _pool_cnt(i * ts, ts + POOL_HALO)
        du = _pool_windows(dm, -1)[:ts] - dpooled[:ts]
        cq = z[:, 512:768]
        cqn, rstd_q = _rms(cq, qg_ref[...])
        dqf = _rope_bwd(dq_ref[...] * ATTN_SCALE, c, a, b)
        dwq_ref[...] += _dot_tn(cqn, dqf)
        dcq, dqg_rows = _rms_bwd(cq, qg_ref[...], rstd_q, _dot_nt(dqf, wq_ref[...]))
        dqg_ref[...] += _rowsum(dqg_rows)
        ckv = z[:, 768:896]
        ckvn, rstd_kv = _rms(ckv, kvg_ref[...])
        dk_, dv_ = dk_ref[...], dv_ref[...]
        dwk_ref[...] += _dot_tn(ckvn, dk_)
        dwv_ref[...] += _dot_tn(ckvn, dv_)
        dckv, dkvg_rows = _rms_bwd(ckv, kvg_ref[...], rstd_kv,
                                   _dot_nt(dk_, wk_ref[...]) + _dot_nt(dv_, wv_ref[...]))
        dkvg_ref[...] += _rowsum(dkvg_rows)
        dkr = dk_[:, :LANES]
        for h in range(1, MLA_HEADS):
            dkr = dkr + dk_[:, h * LANES:(h + 1) * LANES]
        lane = lax.broadcasted_iota(jnp.int32, (ts, LANES), 1)
        dkr = jnp.where((lane >= 64) & (lane < 96), _rope_bwd(dkr, c, a, b), 0.0)
        dz = jnp.concatenate([du, dcq, dckv, dkr], axis=1).astype(BF16)
        x_ = x_ref[...]
        h, rstd = _rms(x_, g_ref[...])
        dxn, dgr = _rms_bwd(x_, g_ref[...], rstd, _dot_nt(dz, win_ref[...]))
        dx_ref[...] = dy_ref[...] + dxn
        dg_ref[...] += _rowsum(dgr)
        _accumulate_tn(acc_ref, dwin_ref, h, dz, n)

    ins = [x, dy, z, z, dq, dk, dv, dyp, dyp, *tabs, g, win, pw, pscale, qg, wq, kvg, wk, wv]
    in_specs = [_row(ts, D), _row(ts, D), _row(ts, D), _prev(POOL_HALO, POOL_DIM, ts), _row(ts, D), _row(ts, D),
                _row(ts, D), _row(ts, POOL_DIM), _next(POOL_HALO, POOL_DIM, ts, n), _row(ts, LANES),
                _row(ts, LANES), _row(ts, LANES)] + [_const(v.shape) for v in ins[12:]]
    acc_shapes = [(1, D), (4, LANES, LANES), (1, POOL_DIM), (1, Q_LORA), (Q_LORA, D), (1, KV_LORA), (KV_LORA, D),
                  (KV_LORA, D)]
    return pl.pallas_call(
        body, name="even_pre_bwd", grid=(n,), in_specs=in_specs,
        out_specs=[_row(ts, D)] + [_acc(s) for s in acc_shapes] + [_acc((D, D))],
        out_shape=[_sds((S, D), F32)] + [_sds(s, F32) for s in acc_shapes] + [_sds((D, D), BF16)],
        scratch_shapes=[pltpu.VMEM((D, D), F32)], compiler_params=_cp())(*ins)


def _pick(n, options):
    for o in options:
        if n % o == 0:
            return o
    return n


def matmul_tn(name, a, b):
    out_dtype = BF16
    S = a.shape[-2]
    ts = _tile_rows(S, 4096 if a.dtype.itemsize == 2 and b.dtype.itemsize == 2 else 2048)
    steps = S // ts

    def body(a_ref, b_ref, o_ref, acc_ref):
        s = pl.program_id(2)

        @pl.when(s == 0)
        def _():
            acc_ref[...] = jnp.zeros_like(acc_ref)

        acc_ref[...] += _dot_tn(a_ref[...], b_ref[...])

        @pl.when(s == steps - 1)
        def _():
            o_ref[...] = acc_ref[...].astype(o_ref.dtype)

    if a.ndim == 3:
        C, _, K = a.shape
        N = b.shape[1]
        tn = _pick(N, (1024, 512, 256, 128))
        grid = (C, N // tn, S // ts)
        in_specs = [pl.BlockSpec((None, ts, K), lambda c, j, s: (c, s, 0)),
                    pl.BlockSpec((ts, tn), lambda c, j, s: (s, j))]
        out_spec, out_shape, tile = pl.BlockSpec((None, K, tn), lambda c, j, s: (c, 0, j)), (C, K, N), (K, tn)
    elif b.ndim == 3:
        C, _, N = b.shape
        K = a.shape[1]
        tk = _pick(K, (1024, 512, 256, 128))
        grid = (C, K // tk, S // ts)
        in_specs = [pl.BlockSpec((ts, tk), lambda c, i, s: (s, i)),
                    pl.BlockSpec((None, ts, N), lambda c, i, s: (c, s, 0))]
        out_spec, out_shape, tile = pl.BlockSpec((None, tk, N), lambda c, i, s: (c, i, 0)), (C, K, N), (tk, N)
    else:
        K, N = a.shape[1], b.shape[1]
        tk = _pick(K, (1024, 512, 256, 128))
        tn = _pick(N, (1024, 512, 256, 128))
        grid = (K // tk, N // tn, S // ts)
        in_specs = [pl.BlockSpec((ts, tk), lambda i, j, s: (s, i)), pl.BlockSpec((ts, tn), lambda i, j, s: (s, j))]
        out_spec, out_shape, tile = pl.BlockSpec((tk, tn), lambda i, j, s: (i, j)), (K, N), (tk, tn)
    return pl.pallas_call(
        body, name=name, grid=grid, in_specs=in_specs, out_specs=out_spec, out_shape=_sds(out_shape, out_dtype),
        scratch_shapes=[pltpu.VMEM(tile, F32)], compiler_params=pltpu.CompilerParams(dimension_semantics=("arbitrary",) * 3, vmem_limit_bytes=VMEM_LIMIT))(
            a, b)


def _my_id():
    return lax.axis_index("x") * 4 + lax.axis_index("y") * 2 + lax.axis_index("c")


def _peer(j):
    x, y, c = lax.axis_index("x"), lax.axis_index("y"), lax.axis_index("c")
    px = 1 - x if j & 4 else x
    py = 1 - y if j & 2 else y
    pc = 1 - c if j & 1 else c
    return (px, py, pc), px * 4 + py * 2 + pc


_HBM = pl.BlockSpec(memory_space=pltpu.HBM)
_SEM = pl.BlockSpec(memory_space=pltpu.SEMAPHORE)
_DATAFLOW = pltpu.SideEffectType.DATAFLOW_SIDE_EFFECTING


def _in_hbm(v):
    return pltpu.with_memory_space_constraint(v, pltpu.HBM)


N_PEERS = N_DEV - 1


def _split_copy(k, j, srcs, lands, send_sems, recv_sems, gather, slot):
    peer, pid = _peer(j)
    return pltpu.make_async_remote_copy(
        src_ref=srcs[k] if _flag(gather, k) else srcs[k].at[pid],
        dst_ref=lands[k].at[_my_id() if slot == "mine" else pid],
        send_sem=send_sems[j - 1], recv_sem=recv_sems[j - 1], device_id=peer, device_id_type=pl.DeviceIdType.MESH)


def _flag(gather, k):
    return gather[k] if isinstance(gather, tuple) else gather


def split_start(name, arrays, gather):
    n = len(arrays)
    lands = [lax.empty((N_DEV,) + a.shape if _flag(gather, k) else a.shape, a.dtype) for k, a in enumerate(arrays)]

    def body(*refs):
        srcs, lnds = refs[:n], refs[n:2 * n]
        sems = refs[4 * n:4 * n + 2 * N_PEERS]
        token = refs[-1]
        for j in range(1, N_DEV):
            for k in range(n):
                _split_copy(k, j, srcs, lnds, sems[:N_PEERS], sems[N_PEERS:], gather, "mine").start()
        token[...] = jnp.zeros_like(token)

    out = pl.pallas_call(
        body, name=name,
        out_shape=(*[pltpu.HBM(a.shape, a.dtype) for a in arrays], *[pltpu.HBM(l.shape, l.dtype) for l in lands],
                   *[pltpu.SemaphoreType.DMA(())] * (2 * N_PEERS), _sds((8, LANES), F32)),
        in_specs=[_HBM] * (2 * n),
        out_specs=(*[_HBM] * (2 * n), *[_SEM] * (2 * N_PEERS), pl.BlockSpec(memory_space=pltpu.VMEM)),
        input_output_aliases={k: k for k in range(2 * n)},
        compiler_params=pltpu.CompilerParams(has_side_effects=_DATAFLOW))(
            *[_in_hbm(a) for a in arrays], *[_in_hbm(l) for l in lands])
    sems = list(out[2 * n:2 * n + 2 * N_PEERS])
    return sems[:N_PEERS], sems[N_PEERS:], list(out[:n]), list(out[n:2 * n]), out[-1]


def split_wait(name, handle, after, gather):
    send_sems, recv_sems, srcs, lands, _ = handle
    n = len(srcs)

    def body(*refs):
        srcs_r, lnds_r = refs[:n], refs[n:2 * n]
        sems = refs[2 * n:2 * n + 2 * N_PEERS]
        for j in range(1, N_DEV):
            for k in range(n):
                cp = _split_copy(k, j, srcs_r, lnds_r, sems[:N_PEERS], sems[N_PEERS:], gather, "peer")
                cp.wait_send()
                cp.wait_recv()

    out = pl.pallas_call(
        body, name=name, out_shape=tuple(pltpu.HBM(a.shape, a.dtype) for a in srcs + lands),
        in_specs=[_HBM] * (2 * n) + [_SEM] * (2 * N_PEERS) + [pl.BlockSpec(memory_space=pl.ANY)],
        out_specs=tuple([_HBM] * (2 * n)), input_output_aliases={k: k for k in range(2 * n)},
        compiler_params=pltpu.CompilerParams(has_side_effects=_DATAFLOW))(
            *srcs, *lands, *send_sems, *recv_sems, after)
    return list(out[:n]), list(out[n:])


def _fill_own_slot(src, land, gather):
    me = _my_id()
    own = src[None] if gather else lax.dynamic_index_in_dim(src, me, 0, keepdims=True)
    return lax.dynamic_update_slice_in_dim(land, own, me, 0)


ADAMW_BLOCK_ELEMS = 128 * 1024


def adamw(name, parts, w, m, v, token=None):
    R, C = w.shape
    tr = _pick(R, [t for t in (512, 256, 128, 64, 32, 16, 8) if t * C <= ADAMW_BLOCK_ELEMS])
    c1 = 1.0 - ADAM_B1 ** ADAM_STEP
    c2 = 1.0 - ADAM_B2 ** ADAM_STEP
    extra, extra_specs = _after(token)

    def body(p_ref, w_ref, m_ref, v_ref, *rest):
        g_ref, d_ref, nm_ref, nv_ref = rest[-4:]
        g = p_ref[0].astype(F32)
        for s in range(1, N_DEV):
            g = g + p_ref[s].astype(F32)
        g_ref[...] = g
        m_ = ADAM_B1 * m_ref[...] + (1.0 - ADAM_B1) * g
        v_ = ADAM_B2 * v_ref[...] + (1.0 - ADAM_B2) * (g * g)
        nm_ref[...] = m_
        nv_ref[...] = v_
        d_ref[...] = -ADAM_LR * ((m_ / c1) / (jnp.sqrt(v_ / c2) + ADAM_EPS) + ADAM_WD * w_ref[...])

    row = pl.BlockSpec((tr, C), lambda i: (i, 0))
    return pl.pallas_call(
        body, name=name, grid=(R // tr,),
        in_specs=[pl.BlockSpec((N_DEV, tr, C), lambda i: (0, i, 0)), row, row, row] + extra_specs,
        out_specs=[row] * 4, out_shape=[_sds((R, C), F32)] * 4, compiler_params=_cp())(parts, w, m, v, *extra)


WEIGHTS = ['ev_norm', 'ev_w_in', 'ev_pool_w', 'ev_pool_scale', 'ev_q_norm', 'ev_w_q_up', 'ev_kv_norm', 'ev_w_kv_up',
           'ev_w_out', 'od_norm', 'od_w_in', 'od_conv_w', 'od_conv_b', 'od_w_rgate', 'od_b_rgate', 'od_w_igate',
           'od_b_igate', 'od_lambda', 'od_w_out', 'xa_norm_x', 'xa_norm_mem', 'xa_w_q', 'xa_w_kv', 'xa_w_o',
           'ffn_norm', 'ffn_w_gate_up', 'ffn_w_down', 'final_norm']
SHARD_AXIS = {'ev_w_in': 1, 'ev_w_q_up': 2, 'ev_w_kv_up': 2, 'ev_w_out': 1, 'od_norm': 1, 'od_w_in': 2,
              'od_conv_w': 2, 'od_conv_b': 1, 'od_w_rgate': 2, 'od_b_rgate': 1, 'od_w_igate': 2, 'od_b_igate': 1,
              'od_lambda': 1, 'od_w_out': 1, 'xa_w_q': 1, 'xa_w_kv': 2, 'xa_w_o': 1, 'ffn_w_gate_up': 2,
              'ffn_w_down': 1}
SMALL_F32 = ('od_norm', 'od_conv_w', 'od_conv_b', 'od_b_rgate', 'od_b_igate', 'od_lambda')
STACKED = ('ffn_w_gate_up', 'ffn_w_down')
SHARDED = [n for n in WEIGHTS if n in SHARD_AXIS]
REPLICATED = [n for n in WEIGHTS if n not in SHARD_AXIS]
ROW_ALIGN = 512


def _pack(flats, dtype):
    v = jnp.concatenate([f.reshape(-1).astype(dtype) for f in flats])
    pad = (-v.shape[0]) % (ROW_ALIGN * LANES)
    return jnp.pad(v, (0, pad)).reshape(-1, LANES)


def _rows8(n_elems):
    return -(-n_elems // (8 * LANES)) * 8


def _pack_rows(arrays, lead=False):
    out = []
    for a in arrays:
        r = a.reshape((N_DEV, -1, LANES) if lead else (-1, LANES))
        pad = _rows8(r.shape[-2] * LANES) - r.shape[-2]
        out.append(jnp.pad(r, [(0, 0)] * (r.ndim - 2) + [(0, pad), (0, 0)]))
    return jnp.concatenate(out, axis=-2)


def _unpack_rows(buf, shapes, lead=False):
    out, off = [], 0
    for s in shapes:
        n = 1
        for d in s:
            n *= d
        rows = buf[..., off:off + n // LANES, :]
        out.append(rows.reshape(((N_DEV,) if lead else ()) + tuple(s)))
        off += _rows8(n)
    return out


def _unpack(flat, shapes):
    out, off = [], 0
    v = flat.reshape(-1)
    for s in shapes:
        n = 1
        for d in s:
            n *= d
        out.append(v[off:off + n].reshape(s))
        off += n
    return out


def _to_full(stacked, axis):
    v = jnp.moveaxis(stacked, 0, axis)
    s = v.shape
    return v.reshape(s[:axis] + (s[axis] * s[axis + 1],) + s[axis + 2:])


def _to_shards(full, axis):
    s = full.shape
    v = full.reshape(s[:axis] + (N_DEV, s[axis] // N_DEV) + s[axis + 1:])
    return jnp.moveaxis(v, axis, 0)


def _pad_heads(w, nh, dh, lead):
    s = w.shape
    v = w.reshape(s[:-1] + (nh, dh))
    v = jnp.pad(v, [(0, 0)] * (len(s) - 1) + [(0, 0), (lead, LANES - dh - lead)])
    return v.reshape(s[:-1] + (nh * LANES,))


def _unpad_heads(w, nh, dh, lead):
    s = w.shape
    return w.reshape(s[:-1] + (nh, LANES))[..., lead:lead + dh].reshape(s[:-1] + (nh * dh,))


def _rope_tables(positions):
    inv_freq = 10000.0 ** (-jnp.arange(0, 32, 2, dtype=F32) / 32)
    ang = positions.astype(F32)[:, None] * inv_freq
    cos, sin = jnp.tile(jnp.cos(ang), (1, LANES // 16)), jnp.tile(jnp.sin(ang), (1, LANES // 16))
    lane = lax.broadcasted_iota(jnp.int32, cos.shape, 1)
    c = jnp.where((lane >= 64) & (lane < 96), cos, 1.0)
    a = jnp.where((lane >= 80) & (lane < 96), sin, 0.0)
    b = jnp.where((lane >= 64) & (lane < 80), -sin, 0.0)
    return c, a, b


def _t(w):
    return jnp.swapaxes(w, -1, -2)


def device_step(x, mem, positions, target, W, fwd_token=None, late_weights=None, ship_grads=None,
                first_weights=None):
    G = {}
    tabs = _rope_tables(positions)
    keep = (positions != 0).astype(F32)[:, None]
    row = lambda v: v.reshape(1, -1)
    if first_weights is not None:
        W = {**W, **first_weights(tabs[0])}

    w_in = W['ev_w_in'][0]
    ev_win = jnp.concatenate([w_in[:, :896], _pad_heads(w_in[:, 896:], 1, 32, 64)], axis=1)
    ev_wq = _pad_heads(W['ev_w_q_up'][0], MLA_HEADS, QK_DIM, 0)
    kvw = W['ev_w_kv_up'][0].reshape(KV_LORA, MLA_HEADS, 128)
    ev_wk = _pad_heads(kvw[:, :, :64].reshape(KV_LORA, 512), MLA_HEADS, 64, 0)
    ev_wv = _pad_heads(kvw[:, :, 64:].reshape(KV_LORA, 512), MLA_HEADS, 64, 0)
    ev_wo_pool = W['ev_w_out'][0][:POOL_DIM]
    ev_wo_att = _t(_pad_heads(_t(W['ev_w_out'][0][POOL_DIM:]), MLA_HEADS, 64, 0))
    pw = W['ev_pool_w'][0].astype(BF16)
    ev_g, ps, qg, kvg = row(W['ev_norm'][0]), row(W['ev_pool_scale'][0]), row(W['ev_q_norm'][0]), row(W['ev_kv_norm'][0])

    z0, qp, kp, vp, ypool = even_pre(x, tabs, ev_g, ev_win, pw, ps, qg, ev_wq, kvg, ev_wk, ev_wv)
    o_att, lse = attn_fwd(qp, kp, vp, fwd_token)
    if late_weights is not None:
        W = {**W, **late_weights(lse)}
    x1 = even_post(x, ypool, o_att, ev_wo_pool, ev_wo_att)

    def xa_ffn_fwd(xin, l, head=()):
        mn, km, vm = mem_kv(mem, row(W['xa_norm_mem'][l]), W['xa_w_kv'][l])
        xm = xattn_fwd(xin, row(W['xa_norm_x'][l]), W['xa_w_q'][l], km, vm, W['xa_w_o'][l])
        *xo, hf, gu = (ffn_fwd_loss if head else ffn_fwd)(
            xm, row(W['ffn_norm'][l]), W['ffn_w_gate_up'], l, W['ffn_w_down'][:, l].reshape(FF_HALF, FF_CHUNK, D),
            *head)
        return xm, (xo if head else xo[0]), (mn, km, vm, hf, gu)

    x2, x3, memkv0 = xa_ffn_fwd(x1, 0)

    od_g, lam = row(W['od_norm'][0]), row(W['od_lambda'][0])
    cw, cb = W['od_conv_w'][0], row(W['od_conv_b'][0])
    wr, wi = W['od_w_rgate'][0], W['od_w_igate'][0]
    br, bi = row(W['od_b_rgate'][0]), row(W['od_b_igate'][0])
    z1, a_t, b_t, xb1, r1, ig1 = odd_pre(x3, keep, od_g, W['od_w_in'][0], cw, cb, wr, br, wi, bi, lam)
    hseq = lru_scan(a_t, b_t)
    x4 = odd_post(x3, z1, hseq, W['od_w_out'][0])
    x5, (dx, g_final, loss), memkv1 = xa_ffn_fwd(x4, 1, (target, row(W['final_norm'])))
    G['final_norm'] = g_final.reshape(D)

    gnx, gnm, gwq, gwkv, gwo, gfn, gwgu, gwd = ([None, None] for _ in range(8))

    def xa_ffn_bwd(dy, xin, xm, memkv, l):
        mn, km, vm, hf, gu = memkv
        fg = row(W['ffn_norm'][l])
        dxm, dfg, act, dgu = ffn_bwd(xm, dy, gu, fg, W['ffn_w_gate_up'], l,
                                     W['ffn_w_down'][:, l].reshape(FF_HALF, FF_CHUNK, D))
        gwd[l] = matmul_tn("ffn_dwd", act, dy).reshape(N_DEV, D_FF // N_DEV, D)
        gwgu[l] = matmul_tn("ffn_dwgu", hf, dgu)
        gfn[l] = dfg[0]
        dxin, o, dq, hx, dgx, dk, dv = xattn_bwd(xin, dxm, row(W['xa_norm_x'][l]), W['xa_w_q'][l], km, vm,
                                                  W['xa_w_o'][l])
        gnx[l] = dgx[0]
        gwo[l] = matmul_tn("xa_dwo", o, dxm)
        gwq[l] = matmul_tn("xa_dwq", hx, dq)
        dkv, dgm = mem_bwd(mem, row(W['xa_norm_mem'][l]), dk, dv, W['xa_w_kv'][l])
        gnm[l] = dgm[0]
        gwkv[l] = matmul_tn("xa_dwkv", mn, dkv)
        return dxin

    dx4 = xa_ffn_bwd(dx, x4, x5, memkv1, 1)

    dgate, dhs, g_od_wout = odd_post_bwd(dx4, z1, hseq, W['od_w_out'][0])
    G['od_w_out'] = g_od_wout[None]
    lam_grad = lru_scan(a_t, dhs, reverse=True)
    dxb, dcb, dbr, dbi, dlam, dwr, dwi = odd_gates_bwd(xb1, r1, ig1, lam_grad, hseq, keep, wr, wi, lam)
    dx3, dcw, dg_od, g_od_win = odd_pre_bwd(x3, dx4, z1, dxb, dgate, od_g, cw, W['od_w_in'][0])
    G['od_w_in'] = g_od_win[None]
    G['od_norm'], G['od_conv_w'], G['od_conv_b'] = dg_od, dcw[None], dcb
    G['od_w_rgate'], G['od_b_rgate'], G['od_w_igate'], G['od_b_igate'], G['od_lambda'] = (
        dwr[None], dbr, dwi[None], dbi, dlam)

    dx1 = xa_ffn_bwd(dx3, x1, x2, memkv0, 0)
    G['xa_norm_x'], G['xa_norm_mem'], G['ffn_norm'] = jnp.stack(gnx), jnp.stack(gnm), jnp.stack(gfn)
    G['xa_w_q'], G['xa_w_kv'], G['xa_w_o'] = jnp.stack(gwq), jnp.stack(gwkv), jnp.stack(gwo)
    G['ffn_w_gate_up'], G['ffn_w_down'] = jnp.stack(gwgu, axis=1), jnp.stack(gwd, axis=1)
    bwd_token = ship_grads(G) if ship_grads is not None else None

    dyp, do_att, delta, g_wo_pool, g_wo_att = even_post_bwd(dx1, ypool, o_att, ev_wo_pool, ev_wo_att)
    G['ev_w_out'] = jnp.concatenate([g_wo_pool, _t(_unpad_heads(_t(g_wo_att), MLA_HEADS, 64, 0))], axis=0)[None]
    dq, dk, dv = attn_bwd(qp, kp, vp, do_att, lse, delta, bwd_token)
    (grad_x, dg_ev, dpw, dps, dqg, dwq, dkvg, dwk, dwv, g_win) = even_pre_bwd(
        x, dx1, z0, dq, dk, dv, dyp, tabs, ev_g, ev_win, pw, ps, qg, ev_wq, kvg, ev_wk, ev_wv)
    G['ev_w_in'] = jnp.concatenate([g_win[:, :896], _unpad_heads(g_win[:, 896:], 1, 32, 64)], axis=1)[None]
    G['ev_norm'], G['ev_pool_w'], G['ev_pool_scale'], G['ev_q_norm'], G['ev_kv_norm'] = (
        dg_ev, dpw[None], dps, dqg, dkvg)
    G['ev_w_q_up'] = _unpad_heads(dwq, MLA_HEADS, QK_DIM, 0)[None]
    gk = _unpad_heads(dwk, MLA_HEADS, 64, 0).reshape(KV_LORA, MLA_HEADS, 64)
    gv = _unpad_heads(dwv, MLA_HEADS, 64, 0).reshape(KV_LORA, MLA_HEADS, 64)
    G['ev_w_kv_up'] = jnp.concatenate([gk, gv], axis=2).reshape(1, KV_LORA, MLA_HEADS * 128)
    return loss[0, 0], grad_x, G


def kernel(x, mem, positions, ev_norm, ev_w_in, ev_pool_w, ev_pool_scale, ev_q_norm, ev_w_q_up, ev_kv_norm, ev_w_kv_up, ev_w_out, od_norm, od_w_in, od_conv_w, od_conv_b, od_w_rgate, od_b_rgate, od_w_igate, od_b_igate, od_lambda, od_w_out, xa_norm_x, xa_norm_mem, xa_w_q, xa_w_kv, xa_w_o, ffn_norm, ffn_w_gate_up, ffn_w_down, final_norm, loss_target, m_ev_norm, m_ev_w_in, m_ev_pool_w, m_ev_pool_scale, m_ev_q_norm, m_ev_w_q_up, m_ev_kv_norm, m_ev_w_kv_up, m_ev_w_out, m_od_norm, m_od_w_in, m_od_conv_w, m_od_conv_b, m_od_w_rgate, m_od_b_rgate, m_od_w_igate, m_od_b_igate, m_od_lambda, m_od_w_out, m_xa_norm_x, m_xa_norm_mem, m_xa_w_q, m_xa_w_kv, m_xa_w_o, m_ffn_norm, m_ffn_w_gate_up, m_ffn_w_down, m_final_norm, v_ev_norm, v_ev_w_in, v_ev_pool_w, v_ev_pool_scale, v_ev_q_norm, v_ev_w_q_up, v_ev_kv_norm, v_ev_w_kv_up, v_ev_w_out, v_od_norm, v_od_w_in, v_od_conv_w, v_od_conv_b, v_od_w_rgate, v_od_b_rgate, v_od_w_igate, v_od_b_igate, v_od_lambda, v_od_w_out, v_xa_norm_x, v_xa_norm_mem, v_xa_w_q, v_xa_w_kv, v_xa_w_o, v_ffn_norm, v_ffn_w_gate_up, v_ffn_w_down, v_final_norm):
    args = dict(locals())
    w = {n: args[n] for n in WEIGHTS}
    m = {n: args['m_' + n] for n in WEIGHTS}
    v = {n: args['v_' + n] for n in WEIGHTS}
    big = [n for n in SHARDED if n not in SMALL_F32]
    small = [n for n in SHARDED if n in SMALL_F32]

    small_shapes = [w[n].shape for n in small]
    first = [n for n in big if n.startswith('ev_')]
    late = [n for n in big if n not in first]

    def full(n, st):
        return st if n in STACKED else _to_full(st, SHARD_AXIS[n])

    W = {n: w[n] for n in REPLICATED}
    gather_first = split_start("first_start", [w[n].astype(BF16) for n in first], True)

    def first_weights(after):
        srcs, lands = split_wait("first_wait", gather_first, after, True)
        return {n: full(n, _fill_own_slot(s, l, True)) for n, s, l in zip(first, srcs, lands)}

    gather = split_start("gather_start", [w[n].astype(BF16) for n in late] + [_pack_rows([w[n] for n in small])], True)

    def late_weights(after):
        srcs, lands = split_wait("gather_wait", gather, after, True)
        lands = [_fill_own_slot(s, l, True) for s, l in zip(srcs, lands)]
        out = {n: full(n, st) for n, st in zip(late, lands)}
        out.update((n, _to_full(st, SHARD_AXIS[n])) for n, st in zip(small, _unpack_rows(lands[-1], small_shapes, True)))
        return out

    def shards(G, n):
        return G[n] if n in STACKED else _to_shards(G[n], SHARD_AXIS[n])

    shipped = []

    def ship_grads(G):
        shipped.append(split_start("exchange_start", [shards(G, n).astype(BF16) for n in late] +
                                   [_pack_rows([shards(G, n) for n in small], lead=True)], False))
        return shipped[0][-1]

    loss, grad_x, G = device_step(x[0], mem[0], positions[0], loss_target[0], W, gather[-1], late_weights, ship_grads,
                                  first_weights)
    outs = [{}, {}, {}, {}]

    last_flags = (False,) * len(first) + (True,)
    last = split_start("last_start", [shards(G, n).astype(BF16) for n in first] + [_pack(
        [G[n] for n in REPLICATED] + [jnp.broadcast_to(loss, (LANES,))], F32)], last_flags)

    two_d = lambda a: a.reshape(-1, a.shape[-1])

    def update(names, parts):
        prev = None
        for n, p in zip(names, parts):
            res = adamw("adamw_" + n, p.reshape((N_DEV,) + two_d(w[n]).shape), two_d(w[n]), two_d(m[n]),
                        two_d(v[n]), prev)
            prev = res[0]
            for k in range(4):
                outs[k][n] = res[k].reshape(w[n].shape)
        return prev

    srcs, lands = split_wait("exchange_wait", shipped[0], last[-1], False)
    late_parts = [_fill_own_slot(s, l, False) for s, l in zip(srcs, lands)]
    after = update(late, late_parts)
    res = adamw("adamw_small", late_parts[-1], *[_pack_rows([d[n] for n in small]) for d in (w, m, v)])
    for k in range(4):
        outs[k].update(zip(small, _unpack_rows(res[k], small_shapes)))

    srcs, lands = split_wait("last_wait", last, after, last_flags)
    *first_parts, rep_parts = [_fill_own_slot(s, l, f) for s, l, f in zip(srcs, lands, last_flags)]
    update(first, first_parts)

    rep_shapes = [w[n].shape for n in REPLICATED] + [(LANES,)]
    zero = jnp.zeros((LANES,), F32)
    rep = adamw("adamw_rep", rep_parts, *[_pack([d[n] for n in REPLICATED] + [zero], F32) for d in (w, m, v)])
    for k in range(4):
        outs[k].update(zip(REPLICATED + ['loss'], _unpack(rep[k], rep_shapes)))
    loss = outs[0]['loss'][0]

    return (loss, grad_x[None], *[outs[0][n] for n in WEIGHTS], *[outs[1][n] for n in WEIGHTS],
            *[outs[2][n] for n in WEIGHTS], *[outs[3][n] for n in WEIGHTS])
```

```python
import functools

import jax
import jax.numpy as jnp
from jax import lax
from jax.experimental import pallas as pl
from jax.experimental.pallas import tpu as pltpu

F32, BF16 = jnp.float32, jnp.bfloat16
N_DEV = 8
D = 1024
POOL_DIM = 512
POOL_WINDOWS = (2, 4, 8, 16)
MLA_HEADS = 8
QK_DIM = 96
Q_LORA, KV_LORA = 256, 128
LRU_HEADS, LRU_HEAD_DIM = 4, 256
LRU_C = 8.0
MEM_HEADS, MEM_HEAD_DIM = 4, 256
D_FF = 2816
RMS_EPS = 1e-6
ADAM_LR, ADAM_B1, ADAM_B2, ADAM_EPS, ADAM_WD, ADAM_STEP = 0.001, 0.9, 0.999, 1e-08, 0.01, 10
LANES = 128
POOL_HALO = 16
CONV_HALO = 8
VMEM_LIMIT = 60000 * 1024


def _cp():
    return pltpu.CompilerParams(dimension_semantics=("arbitrary",), vmem_limit_bytes=VMEM_LIMIT)


def _cp2():
    return pltpu.CompilerParams(dimension_semantics=("arbitrary", "arbitrary"), vmem_limit_bytes=VMEM_LIMIT)


def _row(ts, c, col=0):
    return pl.BlockSpec((ts, c), lambda i: (i, col))


def _prev(hr, c, ts, col=0):
    r = ts // hr
    return pl.BlockSpec((hr, c), lambda i: (jnp.maximum(i * r - 1, 0), col))


def _next(hr, c, ts, n, col=0):
    r = ts // hr
    return pl.BlockSpec((hr, c), lambda i: (jnp.minimum((i + 1) * r, n * r - 1), col))


def _const(shape):
    nd = len(shape)
    return pl.BlockSpec(tuple(shape), lambda i: (0,) * nd, pipeline_mode=pl.Buffered(1))


def _acc(shape):
    nd = len(shape)
    return pl.BlockSpec(tuple(shape), lambda i: (0,) * nd)


def _sds(shape, dt):
    return jax.ShapeDtypeStruct(tuple(shape), dt)


def _dot(a, b):
    return jnp.dot(a.astype(BF16), b.astype(BF16), preferred_element_type=F32)


def _dot_nt(a, b):
    return lax.dot_general(a.astype(BF16), b.astype(BF16), (((1,), (1,)), ((), ())), preferred_element_type=F32)


def _dot_tn(a, b):
    return lax.dot_general(a.astype(BF16), b.astype(BF16), (((0,), (0,)), ((), ())), preferred_element_type=F32)


def _rms(x, g):
    rstd = lax.rsqrt(jnp.mean(x * x, axis=-1, keepdims=True) + RMS_EPS)
    return x * rstd * g, rstd


def _rms_bwd(x, g, rstd, dy):
    xn = x * rstd
    dyg = dy * g
    dx = rstd * (dyg - xn * jnp.mean(dyg * xn, axis=-1, keepdims=True))
    return dx, dy * xn


def _rowsum(v):
    return jnp.sum(v, axis=0, keepdims=True)


def _roll(v, s, axis):
    n = v.shape[axis]
    return pltpu.roll(v, s % n, axis)


def _rope(t, c, a, b):
    k = t.shape[1] // LANES
    if k > 1:
        c, a, b = (jnp.tile(v, (1, k)) for v in (c, a, b))
    return t * c + _roll(t, 16, 1) * a + _roll(t, -16, 1) * b


def _rope_bwd(d, c, a, b):
    k = d.shape[1] // LANES
    if k > 1:
        c, a, b = (jnp.tile(v, (1, k)) for v in (c, a, b))
    return d * c + _roll(d * a, -16, 1) + _roll(d * b, 16, 1)


def _gelu(x):
    c = 0.7978845608028654
    t = jnp.tanh(c * (x + 0.044715 * x * x * x))
    return 0.5 * x * (1.0 + t), t


def _gelu_grad(x, t):
    c = 0.7978845608028654
    return 0.5 * (1.0 + t) + 0.5 * x * (1.0 - t * t) * c * (1.0 + 3.0 * 0.044715 * x * x)


def _blockdot(v, w_ref, nblk, width):
    return jnp.concatenate(
        [_dot(v[:, j * width:(j + 1) * width], w_ref[j]) for j in range(nblk)], axis=1)


def _pool_cnt(row0, rows):
    t = row0 + lax.broadcasted_iota(jnp.int32, (rows, POOL_DIM), 0)
    w = jnp.left_shift(2, lax.broadcasted_iota(jnp.int32, (rows, POOL_DIM), 1) // LANES)
    return jnp.minimum(t + 1, w).astype(F32)


def _pool_windows(ext, sign):
    s2 = ext + _roll(ext, sign * 1, 0)
    t = s2[:, LANES:]
    s4 = t + _roll(t, sign * 2, 0)
    t = s4[:, LANES:]
    s8 = t + _roll(t, sign * 4, 0)
    t = s8[:, LANES:]
    s16 = t + _roll(t, sign * 8, 0)
    return jnp.concatenate([s2[:, :LANES], s4[:, :LANES], s8[:, :LANES], s16], axis=1)


def _pooled(uprev, u, row0):
    ts = u.shape[0]
    ext = jnp.concatenate([uprev, u], axis=0)
    sums = _pool_windows(ext, 1)[POOL_HALO:]
    return sums / _pool_cnt(row0, ts) - u


def _expm1(x):
    return jnp.where(jnp.abs(x) < 0.01, x * (1.0 + 0.5 * x * (1.0 + x * (1.0 / 3.0))), jnp.exp(x) - 1.0)


def _softplus(z):
    return jnp.maximum(z, 0.0) + jnp.log1p(jnp.exp(-jnp.abs(z)))


def _tile_rows(s, want):
    while s % want:
        want //= 2
    return want


def even_pre(x, tabs, g, win, pw, pscale, qg, wq, kvg, wk, wv):
    S = x.shape[0]
    ts = _tile_rows(S, 512)

    def body(x_ref, xp_ref, c_ref, a_ref, b_ref, g_ref, win_ref, pw_ref, ps_ref, qg_ref, wq_ref, kvg_ref,
             wk_ref, wv_ref, z_ref, q_ref, k_ref, v_ref, yp_ref):
        i = pl.program_id(0)
        h, _ = _rms(x_ref[...], g_ref[...])
        z = _dot(h, win_ref[...])
        z_ref[...] = z
        hp, _ = _rms(xp_ref[...], g_ref[...])
        uprev = _dot(hp, win_ref[:, :POOL_DIM]) * (i > 0).astype(F32)
        u = z[:, :POOL_DIM]
        pooled = _pooled(uprev, u, i * ts)
        yp_ref[...] = (_blockdot(pooled, pw_ref, 4, LANES) * ps_ref[...]).astype(BF16)
        c, a, b = c_ref[...], a_ref[...], b_ref[...]
        cqn, _ = _rms(z[:, 512:768], qg_ref[...])
        q_ref[...] = (_rope(_dot(cqn, wq_ref[...]), c, a, b) * (ATTN_SCALE * LOG2_E)).astype(BF16)
        ckvn, _ = _rms(z[:, 768:896], kvg_ref[...])
        krr = _rope(z[:, 896:1024], c, a, b)
        k_ref[...] = (_dot(ckvn, wk_ref[...]) + jnp.tile(krr, (1, MLA_HEADS))).astype(BF16)
        lane = lax.broadcasted_iota(jnp.int32, (ts, D), 1) % LANES
        v_ref[...] = jnp.where(lane == ONES_LANE, 1.0, _dot(ckvn, wv_ref[...])).astype(BF16)

    ins = [x, x, *tabs, g, win, pw, pscale, qg, wq, kvg, wk, wv]
    in_specs = [_row(ts, D), _prev(POOL_HALO, D, ts), _row(ts, LANES), _row(ts, LANES), _row(ts, LANES)]
    in_specs += [_const(v.shape) for v in ins[5:]]
    return pl.pallas_call(
        body, name="even_pre", grid=(S // ts,), in_specs=in_specs,
        out_specs=[_row(ts, D)] * 4 + [_row(ts, POOL_DIM)],
        out_shape=[_sds((S, D), F32)] + [_sds((S, D), BF16)] * 3 + [_sds((S, POOL_DIM), BF16)],
        compiler_params=_cp())(*ins)


ATTN_SCALE = QK_DIM ** -0.5
LOG2_E = 1.4426950408889634
LN_2 = 0.6931471805599453
ONES_LANE = 64


def _exp2(x):
    return jnp.exp2(x)


def _pair_loop(lo, hi, step, init, unrolls=(2, 1)):
    carry = init
    for unroll in unrolls:
        groups = (hi - lo) // unroll

        def group(j, c, lo=lo, unroll=unroll):
            for u in range(unroll):
                c = step(lo + unroll * j + u, c)
            return c

        carry = lax.fori_loop(0, groups, group, carry)
        lo = lo + unroll * groups
    return carry


def _as_row(col):
    return jnp.transpose(jnp.broadcast_to(col, (col.shape[0], LANES)))[0:1, :]


def _after(token):
    return ([], []) if token is None else ([token], [pl.BlockSpec(memory_space=pl.ANY)])


def attn_fwd(qp, kp, vp, token=None):
    S = qp.shape[0]
    tq = _tile_rows(S, 512)
    extra, extra_specs = _after(token)

    def body(q_ref, k_ref, v_ref, *rest):
        o_ref, lse_ref = rest[-2:]
        qi = pl.program_id(1)
        q = q_ref[...]

        def block(ki, carry, masked):
            m, acc = carry
            off = pl.multiple_of(ki * tq, tq)
            s = _dot_nt(q, k_ref[pl.ds(off, tq), :])
            if masked:
                row = lax.broadcasted_iota(jnp.int32, (tq, tq), 0)
                col = lax.broadcasted_iota(jnp.int32, (tq, tq), 1)
                s = jnp.where(col <= row, s, -1e30)
            m_new = jnp.maximum(m, jnp.max(s, axis=1, keepdims=True))
            acc = _exp2(m - m_new) * acc + _dot(_exp2(s - m_new), v_ref[pl.ds(off, tq), :])
            return m_new, acc

        init = (jnp.full((tq, 1), -1e30, F32), jnp.zeros((tq, LANES), F32))
        carry = _pair_loop(0, qi, lambda ki, c: block(ki, c, False), init, unrolls=(16, 8, 4, 2, 1))
        m, acc = block(qi, carry, True)
        l = acc[:, ONES_LANE:ONES_LANE + 1]
        o_ref[...] = acc / l
        lse_ref[...] = _as_row(m + jnp.log(l) * LOG2_E)

    blk = pl.BlockSpec((tq, LANES), lambda h, i: (i, h))
    full = pl.BlockSpec((S, LANES), lambda h, i: (0, h))
    return pl.pallas_call(
        body, name="attn_fwd", grid=(MLA_HEADS, S // tq), in_specs=[blk, full, full] + extra_specs,
        out_specs=[blk, pl.BlockSpec((None, None, 1, tq), lambda h, i: (h, i, 0, 0))],
        out_shape=[_sds((S, D), F32), _sds((MLA_HEADS, S // tq, 1, tq), F32)], compiler_params=_cp2())(
            qp, kp, vp, *extra)


def even_post(x, ypool, o, wo_pool, wo_att):
    S = x.shape[0]
    ts = _tile_rows(S, 512)

    def body(x_ref, yp_ref, o_ref, wp_ref, wa_ref, out_ref):
        out_ref[...] = x_ref[...] + _dot(yp_ref[...], wp_ref[...]) + _dot(o_ref[...], wa_ref[...])

    return pl.pallas_call(
        body, name="even_post", grid=(S // ts,),
        in_specs=[_row(ts, D), _row(ts, POOL_DIM), _row(ts, D), _const(wo_pool.shape), _const(wo_att.shape)],
        out_specs=_row(ts, D), out_shape=_sds((S, D), F32), compiler_params=_cp())(x, ypool, o, wo_pool, wo_att)


def mem_kv(mem, g, wkv):
    M = mem.shape[0]

    def body(mem_ref, g_ref, w_ref, mn_ref, k_ref, v_ref):
        mn, _ = _rms(mem_ref[...], g_ref[...])
        mn_ref[...] = mn.astype(BF16)
        k_ref[...] = _dot(mn, w_ref[:, :D]).astype(BF16)
        v_ref[...] = _dot(mn, w_ref[:, D:]).astype(BF16)

    return pl.pallas_call(
        body, name="mem_kv", grid=(1,), in_specs=[_acc(mem.shape), _acc(g.shape), _acc(wkv.shape)],
        out_specs=[_acc((M, D))] * 3, out_shape=[_sds((M, D), BF16)] * 3, compiler_params=_cp())(mem, g, wkv)


def _xattn_heads(hx, wq_ref, k_ref, v_ref):
    q = _dot(hx, wq_ref[...])
    scale = MEM_HEAD_DIM ** -0.5
    ps, os_ = [], []
    for h in range(MEM_HEADS):
        sl = slice(h * MEM_HEAD_DIM, (h + 1) * MEM_HEAD_DIM)
        s = _dot_nt(q[:, sl], k_ref[:, sl]) * scale
        e = jnp.exp(s - jnp.max(s, axis=1, keepdims=True))
        p = e / jnp.sum(e, axis=1, keepdims=True)
        ps.append(p)
        os_.append(_dot(p, v_ref[:, sl]))
    return q, ps, jnp.concatenate(os_, axis=1)


def xattn_fwd(x, g, wq, kmem, vmem, wo):
    S = x.shape[0]
    ts = _tile_rows(S, 512)

    def body(x_ref, g_ref, wq_ref, k_ref, v_ref, wo_ref, out_ref):
        x_ = x_ref[...]
        hx, _ = _rms(x_, g_ref[...])
        _, _, o = _xattn_heads(hx, wq_ref, k_ref, v_ref)
        out_ref[...] = x_ + _dot(o, wo_ref[...])

    ins = [x, g, wq, kmem, vmem, wo]
    return pl.pallas_call(
        body, name="xattn_fwd", grid=(S // ts,), in_specs=[_row(ts, D)] + [_const(v.shape) for v in ins[1:]],
        out_specs=_row(ts, D), out_shape=_sds((S, D), F32), compiler_params=_cp())(*ins)


def xattn_bwd(x, dy, g, wq, kmem, vmem, wo):
    S = x.shape[0]
    M = kmem.shape[0]
    ts = _tile_rows(S, 512)
    scale = MEM_HEAD_DIM ** -0.5

    def body(x_ref, dy_ref, g_ref, wq_ref, k_ref, v_ref, wo_ref,
             dx_ref, o_ref, dq_ref, hx_ref, dg_ref, dk_ref, dv_ref):
        i = pl.program_id(0)

        @pl.when(i == 0)
        def _():
            dg_ref[...] = jnp.zeros_like(dg_ref)
            dk_ref[...] = jnp.zeros_like(dk_ref)
            dv_ref[...] = jnp.zeros_like(dv_ref)

        x_, dy_ = x_ref[...], dy_ref[...]
        hx, rstd = _rms(x_, g_ref[...])
        q, ps, o = _xattn_heads(hx, wq_ref, k_ref, v_ref)
        hx_ref[...] = hx.astype(BF16)
        o_ref[...] = o.astype(BF16)
        do = _dot_nt(dy_, wo_ref[...])
        dqs = []
        for h in range(MEM_HEADS):
            sl = slice(h * MEM_HEAD_DIM, (h + 1) * MEM_HEAD_DIM)
            p, do_h = ps[h], do[:, sl]
            dp = _dot_nt(do_h, v_ref[:, sl])
            ds = p * (dp - jnp.sum(p * dp, axis=1, keepdims=True)) * scale
            dqs.append(_dot(ds, k_ref[:, sl]))
            dk_ref[:, sl] += _dot_tn(ds, q[:, sl])
            dv_ref[:, sl] += _dot_tn(p, do_h)
        dq = jnp.concatenate(dqs, axis=1).astype(BF16)
        dq_ref[...] = dq
        dxn, dgr = _rms_bwd(x_, g_ref[...], rstd, _dot_nt(dq, wq_ref[...]))
        dx_ref[...] = dy_ + dxn
        dg_ref[...] += _rowsum(dgr)

    ins = [x, dy, g, wq, kmem, vmem, wo]
    return pl.pallas_call(
        body, name="xattn_bwd", grid=(S // ts,),
        in_specs=[_row(ts, D), _row(ts, D)] + [_const(v.shape) for v in ins[2:]],
        out_specs=[_row(ts, D)] * 4 + [_acc((1, D)), _acc((M, D)), _acc((M, D))],
        out_shape=[_sds((S, D), F32)] + [_sds((S, D), BF16)] * 3 + [_sds((1, D), F32), _sds((M, D), F32),
                                                                    _sds((M, D), F32)],
        compiler_params=_cp())(*ins)


def mem_bwd(mem, g, dk, dv, wkv):
    M = mem.shape[0]

    def body(mem_ref, g_ref, dk_ref, dv_ref, w_ref, dkv_ref, dg_ref):
        dkv = jnp.concatenate([dk_ref[...], dv_ref[...]], axis=1)
        dkv_ref[...] = dkv.astype(BF16)
        _, rstd = _rms(mem_ref[...], g_ref[...])
        dg_ref[...] = _rowsum(_dot_nt(dkv, w_ref[...]) * (mem_ref[...] * rstd))

    ins = [mem, g, dk, dv, wkv]
    return pl.pallas_call(
        body, name="mem_bwd", grid=(1,), in_specs=[_acc(v.shape) for v in ins],
        out_specs=[_acc((M, 2 * D)), _acc((1, D))], out_shape=[_sds((M, 2 * D), BF16), _sds((1, D), F32)],
        compiler_params=_cp())(*ins)


FF_CHUNK = 2 * D_FF // N_DEV
FF_HALF = N_DEV // 2


def _layer_of(w, layer):
    return pl.BlockSpec((N_DEV, None) + w.shape[2:], lambda i: (0, layer, 0, 0), pipeline_mode=pl.Buffered(1))


def _ff_chunks(c, ts):
    return pl.BlockSpec((c, ts, FF_CHUNK), lambda i: (0, i, 0))


def _ffn(x_, g_ref, wgu_ref, wd_ref, hf_ref, gu_ref):
    hf = _rms(x_, g_ref[...])[0].astype(BF16)
    hf_ref[...] = hf
    out = x_
    for j in range(FF_HALF):
        gg, uu = _dot(hf, wgu_ref[j]), _dot(hf, wgu_ref[j + FF_HALF])
        gu_ref[j] = gg.astype(BF16)
        gu_ref[j + FF_HALF] = uu.astype(BF16)
        out = out + _dot(gg * jax.nn.sigmoid(gg) * uu, wd_ref[j])
    return out


def ffn_fwd(x, g, wgu, layer, wd):
    S = x.shape[0]
    ts = _tile_rows(S, 256)

    def body(x_ref, g_ref, wgu_ref, wd_ref, out_ref, hf_ref, gu_ref):
        out_ref[...] = _ffn(x_ref[...], g_ref, wgu_ref, wd_ref, hf_ref, gu_ref)

    return pl.pallas_call(
        body, name="ffn_fwd", grid=(S // ts,),
        in_specs=[_row(ts, D), _const(g.shape), _layer_of(wgu, layer), _const(wd.shape)],
        out_specs=[_row(ts, D), _row(ts, D), _ff_chunks(N_DEV, ts)],
        out_shape=[_sds((S, D), F32), _sds((S, D), BF16), _sds((N_DEV, S, FF_CHUNK), BF16)],
        compiler_params=_cp())(x, g, wgu, wd)


def ffn_fwd_loss(x, g, wgu, layer, wd, target, gf):
    S = x.shape[0]
    ts = _tile_rows(S, 256)

    def body(x_ref, g_ref, wgu_ref, wd_ref, t_ref, gf_ref, dx_ref, dgf_ref, loss_ref, hf_ref, gu_ref):
        @pl.when(pl.program_id(0) == 0)
        def _():
            dgf_ref[...] = jnp.zeros_like(dgf_ref)
            loss_ref[...] = jnp.zeros_like(loss_ref)

        out = _ffn(x_ref[...], g_ref, wgu_ref, wd_ref, hf_ref, gu_ref)
        y, rstd = _rms(out, gf_ref[...])
        err = y - t_ref[...]
        loss_ref[...] += 0.5 * _rowsum(jnp.mean(err * err, axis=1, keepdims=True))
        dxn, dgr = _rms_bwd(out, gf_ref[...], rstd, err * (1.0 / D))
        dx_ref[...] = dxn
        dgf_ref[...] += _rowsum(dgr)

    return pl.pallas_call(
        body, name="ffn_fwd_loss", grid=(S // ts,),
        in_specs=[_row(ts, D), _const(g.shape), _layer_of(wgu, layer), _const(wd.shape), _row(ts, D),
                  _const(gf.shape)],
        out_specs=[_row(ts, D), _acc((1, D)), _acc((1, 1)), _row(ts, D), _ff_chunks(N_DEV, ts)],
        out_shape=[_sds((S, D), F32), _sds((1, D), F32), _sds((1, 1), F32), _sds((S, D), BF16),
                   _sds((N_DEV, S, FF_CHUNK), BF16)],
        compiler_params=_cp())(x, g, wgu, wd, target, gf)


def ffn_bwd(x, dy, gu, g, wgu, layer, wd):
    S = x.shape[0]
    ts = _tile_rows(S, 256)

    def body(x_ref, dy_ref, gu_ref, g_ref, wgu_ref, wd_ref, dx_ref, dg_ref, act_ref, dgu_ref):
        @pl.when(pl.program_id(0) == 0)
        def _():
            dg_ref[...] = jnp.zeros_like(dg_ref)

        dy_ = dy_ref[...]
        dyb = dy_.astype(BF16)
        dh = jnp.zeros((ts, D), F32)
        dacts = [_dot_nt(dyb, wd_ref[j]) for j in range(FF_HALF)]
        for j in range(FF_HALF):
            gg, uu = gu_ref[j].astype(F32), gu_ref[j + FF_HALF].astype(F32)
            sg = jax.nn.sigmoid(gg)
            silu = gg * sg
            act_ref[j] = (silu * uu).astype(BF16)
            dact = dacts[j]
            dgate = (dact * uu * (sg * (1.0 + gg * (1.0 - sg)))).astype(BF16)
            dup = (dact * silu).astype(BF16)
            dgu_ref[j] = dgate
            dgu_ref[j + FF_HALF] = dup
            dh = dh + _dot_nt(dgate, wgu_ref[j]) + _dot_nt(dup, wgu_ref[j + FF_HALF])
        x_ = x_ref[...]
        _, rstd = _rms(x_, g_ref[...])
        dxn, dgr = _rms_bwd(x_, g_ref[...], rstd, dh)
        dx_ref[...] = dy_ + dxn
        dg_ref[...] += _rowsum(dgr)

    return pl.pallas_call(
        body, name="ffn_bwd", grid=(S // ts,),
        in_specs=[_row(ts, D), _row(ts, D), _ff_chunks(N_DEV, ts), _const(g.shape), _layer_of(wgu, layer),
                  _const(wd.shape)],
        out_specs=[_row(ts, D), _acc((1, D)), _ff_chunks(FF_HALF, ts), _ff_chunks(N_DEV, ts)],
        out_shape=[_sds((S, D), F32), _sds((1, D), F32), _sds((FF_HALF, S, FF_CHUNK), BF16),
                   _sds((N_DEV, S, FF_CHUNK), BF16)],
        compiler_params=_cp())(x, dy, gu, g, wgu, wd)


def _conv_fwd(xprev, xbp, cw_ref, cb):
    ext = jnp.concatenate([xprev, xbp], axis=0)
    acc = cb + cw_ref[3:4, :] * xbp
    for k in range(3):
        acc = acc + cw_ref[k:k + 1, :] * _roll(ext, 3 - k, 0)[CONV_HALO:]
    return acc


def _decay(r, lam):
    sp = _softplus(-lam)
    log_a = -LRU_C * r * sp
    return sp, jnp.exp(log_a), jnp.sqrt(jnp.maximum(-_expm1(2.0 * log_a), 0.0))


def odd_pre(x, keep, g, win, cw, cb, wr, br, wi, bi, lam):
    S = x.shape[0]
    ts = _tile_rows(S, 512)

    def body(x_ref, xp_ref, keep_ref, g_ref, win_ref, cw_ref, cb_ref, wr_ref, br_ref, wi_ref, bi_ref, lam_ref,
             z_ref, a_ref, b_ref, xb_ref, r_ref, ig_ref):
        i = pl.program_id(0)
        h, _ = _rms(x_ref[...], g_ref[...])
        z = _dot(h, win_ref[...])
        z_ref[...] = z
        hp, _ = _rms(xp_ref[...], g_ref[...])
        xprev = _dot(hp, win_ref[:, D:]) * (i > 0).astype(F32)
        xb = _conv_fwd(xprev, z[:, D:], cw_ref, cb_ref[...])
        xb_ref[...] = xb
        r = jax.nn.sigmoid(_blockdot(xb, wr_ref, LRU_HEADS, LRU_HEAD_DIM) + br_ref[...])
        ig = jax.nn.sigmoid(_blockdot(xb, wi_ref, LRU_HEADS, LRU_HEAD_DIM) + bi_ref[...])
        r_ref[...] = r
        ig_ref[...] = ig
        keep_ = keep_ref[...]
        _, a, mult = _decay(r, lam_ref[...])
        a_ref[...] = a * keep_
        b_ref[...] = jnp.where(keep_ > 0.0, mult, 1.0) * (ig * xb)

    ins = [x, x, keep, g, win, cw, cb, wr, br, wi, bi, lam]
    return pl.pallas_call(
        body, name="odd_pre", grid=(S // ts,),
        in_specs=[_row(ts, D), _prev(CONV_HALO, D, ts), _row(ts, 1)] + [_const(v.shape) for v in ins[3:]],
        out_specs=[_row(ts, 2 * D)] + [_row(ts, D)] * 5,
        out_shape=[_sds((S, 2 * D), F32)] + [_sds((S, D), F32)] * 5, compiler_params=_cp())(*ins)


def lru_scan(a, b, reverse=False):
    S = a.shape[0]
    ts = _tile_rows(S, 512)
    n = S // ts
    groups = ts // 8

    def body(a_ref, an_ref, b_ref, h_ref, carry_ref, ash_ref):
        i = pl.program_id(0)

        @pl.when(i == 0)
        def _():
            carry_ref[...] = jnp.zeros_like(carry_ref)

        rid = lax.broadcasted_iota(jnp.int32, (8, D), 0)
        if reverse:
            ext = jnp.concatenate([a_ref[...], an_ref[...] * (i > 0).astype(F32)], axis=0)
            ash_ref[...] = _roll(ext, -1, 0)[:ts]
        src = ash_ref if reverse else a_ref

        def group(j, carry):
            off = pl.multiple_of((groups - 1 - j if reverse else j) * 8, 8)
            a8, b8 = src[pl.ds(off, 8), :], b_ref[pl.ds(off, 8), :]
            for k in (1, 2, 4):
                inside = (rid < 8 - k) if reverse else (rid >= k)
                sh = -k if reverse else k
                a_sh = jnp.where(inside, _roll(a8, sh, 0), 1.0)
                b_sh = jnp.where(inside, _roll(b8, sh, 0), 0.0)
                b8 = a8 * b_sh + b8
                a8 = a8 * a_sh
            h8 = a8 * carry + b8
            h_ref[pl.ds(off, 8), :] = h8
            return h8[0:1, :] if reverse else h8[7:8, :]

        carry_ref[...] = lax.fori_loop(0, groups, group, carry_ref[...], unroll=4)

    if reverse:
        r = ts // 8
        tile = pl.BlockSpec((ts, D), lambda i: (n - 1 - i, 0))
        halo = pl.BlockSpec((8, D), lambda i: (jnp.minimum((n - i) * r, n * r - 1), 0))
    else:
        tile, halo = _row(ts, D), _prev(8, D, ts)
    return pl.pallas_call(
        body, name="lru_scan_rev" if reverse else "lru_scan", grid=(n,), in_specs=[tile, halo, tile],
        out_specs=tile, out_shape=_sds((S, D), F32),
        scratch_shapes=[pltpu.VMEM((1, D), F32), pltpu.VMEM((ts, D), F32)], compiler_params=_cp())(a, a, b)


def odd_post(x, z, hseq, wout):
    S = x.shape[0]
    ts = _tile_rows(S, 512)

    def body(x_ref, gate_ref, h_ref, w_ref, out_ref):
        gl, _ = _gelu(gate_ref[...])
        out_ref[...] = x_ref[...] + _dot(gl * h_ref[...], w_ref[...])

    return pl.pallas_call(
        body, name="odd_post", grid=(S // ts,),
        in_specs=[_row(ts, D), _row(ts, D), _row(ts, D), _const(wout.shape)],
        out_specs=_row(ts, D), out_shape=_sds((S, D), F32), compiler_params=_cp())(x, z, hseq, wout)


def _accumulate_tn(acc_ref, out_ref, a, b, steps):
    i = pl.program_id(0)

    @pl.when(i == 0)
    def _():
        acc_ref[...] = jnp.zeros_like(acc_ref)

    acc_ref[...] += _dot_tn(a, b)

    @pl.when(i == steps - 1)
    def _():
        out_ref[...] = acc_ref[...].astype(out_ref.dtype)


def odd_post_bwd(dy, z, hseq, wout):
    S = dy.shape[0]
    ts = _tile_rows(S, 512)
    n = S // ts

    def body(dy_ref, gate_ref, h_ref, w_ref, dgate_ref, dh_ref, dw_ref, acc_ref):
        gate, hs, dy_ = gate_ref[...], h_ref[...], dy_ref[...]
        gl, t = _gelu(gate)
        dyy = _dot_nt(dy_, w_ref[...])
        dgate_ref[...] = dyy * hs * _gelu_grad(gate, t)
        dh_ref[...] = dyy * gl
        _accumulate_tn(acc_ref, dw_ref, gl * hs, dy_, n)

    return pl.pallas_call(
        body, name="odd_post_bwd", grid=(n,),
        in_specs=[_row(ts, D), _row(ts, D), _row(ts, D), _const(wout.shape)],
        out_specs=[_row(ts, D), _row(ts, D), _acc((D, D))],
        out_shape=[_sds((S, D), F32), _sds((S, D), F32), _sds((D, D), BF16)],
        scratch_shapes=[pltpu.VMEM((D, D), F32)], compiler_params=_cp())(dy, z, hseq, wout)


def odd_gates_bwd(xb, r, ig, lam_grad, hseq, keep, wr, wi, lam):
    S = xb.shape[0]
    ts = _tile_rows(S, 512)

    def body(xb_ref, r_ref, ig_ref, lg_ref, h_ref, hp_ref, keep_ref, wr_ref, wi_ref, lam_ref,
             dxb_ref, dcb_ref, dbr_ref, dbi_ref, dlam_ref, dwr_ref, dwi_ref):
        i = pl.program_id(0)

        @pl.when(i == 0)
        def _():
            for ref in (dcb_ref, dbr_ref, dbi_ref, dlam_ref, dwr_ref, dwi_ref):
                ref[...] = jnp.zeros_like(ref)

        first = (i > 0).astype(F32)
        xb, r, ig = xb_ref[...], r_ref[...], ig_ref[...]
        keep_ = keep_ref[...]
        lam_ = lam_ref[...]
        sp, a, mult = _decay(r, lam_)
        hs = h_ref[...]
        hprev = _roll(jnp.concatenate([hp_ref[...] * first, hs], axis=0), 1, 0)[CONV_HALO:]
        lg = lg_ref[...]
        da = lg * hprev * keep_
        ixb = ig * xb
        dmult = lg * ixb * keep_
        dixb = lg * jnp.where(keep_ > 0.0, mult, 1.0)
        dlog_a = da * a - dmult * jnp.where(mult > 0.0, a * a / mult, 0.0)
        dr = dlog_a * (-LRU_C * sp)
        dlam_ref[...] += _rowsum(dlog_a * (-LRU_C * r)) * (-jax.nn.sigmoid(-lam_))
        dpr = dr * r * (1.0 - r)
        dpi = dixb * xb * ig * (1.0 - ig)
        dbr_ref[...] += _rowsum(dpr)
        dbi_ref[...] += _rowsum(dpi)
        dxb = dixb * ig
        parts = []
        for h in range(LRU_HEADS):
            sl = slice(h * LRU_HEAD_DIM, (h + 1) * LRU_HEAD_DIM)
            dwr_ref[h] += _dot_tn(xb[:, sl], dpr[:, sl])
            dwi_ref[h] += _dot_tn(xb[:, sl], dpi[:, sl])
            parts.append(_dot_nt(dpr[:, sl], wr_ref[h]) + _dot_nt(dpi[:, sl], wi_ref[h]))
        dxb = dxb + jnp.concatenate(parts, axis=1)
        dxb_ref[...] = dxb
        dcb_ref[...] += _rowsum(dxb)

    ins = [xb, r, ig, lam_grad, hseq, hseq, keep, wr, wi, lam]
    in_specs = [_row(ts, D)] * 5 + [_prev(CONV_HALO, D, ts), _row(ts, 1)] + [_const(v.shape) for v in ins[7:]]
    gshape = (LRU_HEADS, LRU_HEAD_DIM, LRU_HEAD_DIM)
    return pl.pallas_call(
        body, name="odd_gates_bwd", grid=(S // ts,), in_specs=in_specs,
        out_specs=[_row(ts, D)] + [_acc((1, D))] * 4 + [_acc(gshape)] * 2,
        out_shape=[_sds((S, D), F32)] + [_sds((1, D), F32)] * 4 + [_sds(gshape, F32)] * 2,
        compiler_params=_cp())(*ins)


def odd_pre_bwd(x, dy, z, dxb, dgate, g, cw, win):
    S = x.shape[0]
    ts = _tile_rows(S, 512)
    n = S // ts

    def body(x_ref, dy_ref, xbp_ref, xbpp_ref, dxb_ref, dxbn_ref, dgate_ref, g_ref, cw_ref, win_ref,
             dx_ref, dcw_ref, dg_ref, dwin_ref, acc_ref):
        i = pl.program_id(0)

        @pl.when(i == 0)
        def _():
            dcw_ref[...] = jnp.zeros_like(dcw_ref)
            dg_ref[...] = jnp.zeros_like(dg_ref)

        dxb = dxb_ref[...]
        extd = jnp.concatenate([dxb, dxbn_ref[...] * (i < n - 1).astype(F32)], axis=0)
        extx = jnp.concatenate([xbpp_ref[...] * (i > 0).astype(F32), xbp_ref[...]], axis=0)
        dxbp = cw_ref[3:4, :] * dxb
        dcw_ref[3:4, :] += _rowsum(dxb * xbp_ref[...])
        for k in range(3):
            dxbp = dxbp + cw_ref[k:k + 1, :] * _roll(extd, -(3 - k), 0)[:ts]
            dcw_ref[k:k + 1, :] += _rowsum(dxb * _roll(extx, 3 - k, 0)[CONV_HALO:])
        dz = jnp.concatenate([dgate_ref[...], dxbp], axis=1).astype(BF16)
        x_ = x_ref[...]
        h, rstd = _rms(x_, g_ref[...])
        dxn, dgr = _rms_bwd(x_, g_ref[...], rstd, _dot_nt(dz, win_ref[...]))
        dx_ref[...] = dy_ref[...] + dxn
        dg_ref[...] += _rowsum(dgr)
        _accumulate_tn(acc_ref, dwin_ref, h, dz, n)

    ins = [x, dy, z, z, dxb, dxb, dgate, g, cw, win]
    in_specs = [_row(ts, D), _row(ts, D), _row(ts, D, 1), _prev(CONV_HALO, D, ts, 1), _row(ts, D),
                _next(CONV_HALO, D, ts, n), _row(ts, D)] + [_const(v.shape) for v in ins[7:]]
    return pl.pallas_call(
        body, name="odd_pre_bwd", grid=(n,), in_specs=in_specs,
        out_specs=[_row(ts, D), _acc((4, D)), _acc((1, D)), _acc((D, 2 * D))],
        out_shape=[_sds((S, D), F32), _sds((4, D), F32), _sds((1, D), F32), _sds((D, 2 * D), BF16)],
        scratch_shapes=[pltpu.VMEM((D, 2 * D), F32)], compiler_params=_cp())(*ins)


def even_post_bwd(dy, ypool, o, wo_pool, wo_att):
    S = dy.shape[0]
    ts = _tile_rows(S, 512)
    n = S // ts

    def body(dy_ref, yp_ref, o_ref, wp_ref, wa_ref, dyp_ref, do_ref, delta_ref, dwp_ref, dwa_ref, accp_ref,
             acca_ref):
        dy_, o_ = dy_ref[...], o_ref[...]
        dyp_ref[...] = _dot_nt(dy_, wp_ref[...])
        do = _dot_nt(dy_, wa_ref[...])
        do_ref[...] = do.astype(BF16)
        prod = do * o_
        for h in range(MLA_HEADS):
            delta_ref[h] = _as_row(jnp.sum(prod[:, h * LANES:(h + 1) * LANES], axis=1, keepdims=True))
        _accumulate_tn(accp_ref, dwp_ref, yp_ref[...], dy_, n)
        _accumulate_tn(acca_ref, dwa_ref, o_, dy_, n)

    return pl.pallas_call(
        body, name="even_post_bwd", grid=(n,),
        in_specs=[_row(ts, D), _row(ts, POOL_DIM), _row(ts, D), _const(wo_pool.shape), _const(wo_att.shape)],
        out_specs=[_row(ts, POOL_DIM), _row(ts, D),
                   pl.BlockSpec((MLA_HEADS, None, 1, ts), lambda i: (0, i, 0, 0)), _acc((POOL_DIM, D)),
                   _acc((D, D))],
        out_shape=[_sds((S, POOL_DIM), F32), _sds((S, D), BF16), _sds((MLA_HEADS, n, 1, ts), F32),
                   _sds((POOL_DIM, D), BF16), _sds((D, D), BF16)],
        scratch_shapes=[pltpu.VMEM((POOL_DIM, D), F32), pltpu.VMEM((D, D), F32)],
        compiler_params=_cp())(dy, ypool, o, wo_pool, wo_att)


def attn_bwd(qp, kp, vp, do, lse_row, delta_row, token=None):
    S = qp.shape[0]
    tk = _tile_rows(S, 512)
    nq = S // tk
    extra, extra_specs = _after(token)

    def body(q_ref, k_ref, v_ref, do_ref, lse_ref, delta_ref, *rest):
        dq_ref, dk_ref, dv_ref = rest[-3:]
        kj = pl.program_id(1)

        @pl.when(kj == 0)
        def _():
            dq_ref[...] = jnp.zeros_like(dq_ref)

        k, v = k_ref[...], v_ref[...]

        def block(qi, carry, masked):
            dk, dv = carry
            off = pl.multiple_of(qi * tk, tk)
            q = q_ref[pl.ds(off, tk), :]
            do_ = do_ref[pl.ds(off, tk), :]
            st = _dot_nt(k, q)
            if masked:
                row = lax.broadcasted_iota(jnp.int32, (tk, tk), 0)
                col = lax.broadcasted_iota(jnp.int32, (tk, tk), 1)
                st = jnp.where(col >= row, st, -1e30)
            pt = _exp2(st - lse_ref[qi])
            dv = dv + _dot(pt, do_)
            dst = (pt * (_dot_nt(v, do_) - delta_ref[qi])).astype(BF16)
            dk = dk + _dot(dst, q)
            dq_ref[pl.ds(off, tk), :] += _dot_tn(dst, k)
            return dk, dv

        zero = jnp.zeros((tk, LANES), F32)
        carry = block(kj, (zero, zero), True)
        dk, dv = _pair_loop(kj + 1, nq, lambda qi, c: block(qi, c, False), carry, unrolls=(16, 8, 4, 2, 1))
        dk_ref[...] = dk * LN_2
        dv_ref[...] = dv

    blk = pl.BlockSpec((tk, LANES), lambda h, j: (j, h))
    full = pl.BlockSpec((S, LANES), lambda h, j: (0, h))
    rowv = pl.BlockSpec((None, nq, 1, tk), lambda h, j: (h, 0, 0, 0))
    return pl.pallas_call(
        body, name="attn_bwd", grid=(MLA_HEADS, nq), in_specs=[full, blk, blk, full, rowv, rowv] + extra_specs,
        out_specs=[full, blk, blk], out_shape=[_sds((S, D), F32)] * 3, compiler_params=_cp2())(
            qp, kp, vp, do, lse_row, delta_row, *extra)


def even_pre_bwd(x, dy, z, dq, dk, dv, dyp, tabs, g, win, pw, pscale, qg, wq, kvg, wk, wv):
    S = x.shape[0]
    ts = _tile_rows(S, 512)
    n = S // ts

    def body(x_ref, dy_ref, z_ref, up_ref, dq_ref, dk_ref, dv_ref, dyp_ref, dypn_ref, c_ref, a_ref, b_ref,
             g_ref, win_ref, pw_ref, ps_ref, qg_ref, wq_ref, kvg_ref, wk_ref, wv_ref,
             dx_ref, dg_ref, dpw_ref, dps_ref, dqg_ref, dwq_ref, dkvg_ref, dwk_ref, dwv_ref, dwin_ref, acc_ref):
        i = pl.program_id(0)

        @pl.when(i == 0)
        def _():
            for ref in (dg_ref, dpw_ref, dps_ref, dqg_ref, dwq_ref, dkvg_ref, dwk_ref, dwv_ref):
                ref[...] = jnp.zeros_like(ref)

        z = z_ref[...]
        c, a, b = c_ref[...], a_ref[...], b_ref[...]
        ps = ps_ref[...]
        u = z[:, :POOL_DIM]
        pooled = _pooled(up_ref[...] * (i > 0).astype(F32), u, i * ts)
        dyp_ = dyp_ref[...]
        dps_ref[...] += _rowsum(dyp_ * _blockdot(pooled, pw_ref, 4, LANES))
        ext = jnp.concatenate([dyp_, dypn_ref[...] * (i < n - 1).astype(F32)], axis=0) * ps
        for gidx in range(4):
            sl = slice(gidx * LANES, (gidx + 1) * LANES)
            dpw_ref[gidx] += _dot_tn(pooled[:, sl], ext[:ts, sl])
        dpooled = jnp.concatenate(
            [_dot_nt(ext[:, gidx * LANES:(gidx + 1) * LANES], pw_ref[gidx]) for gidx in range(4)], axis=1)
        dm = dpooled / _pool_cnt(i * ts, ts + POOL_HALO)
        du = _pool_windows(dm, -1)[:ts] - dpooled[:ts]
        cq = z[:, 512:768]
        cqn, rstd_q = _rms(cq, qg_ref[...])
        dqf = _rope_bwd(dq_ref[...] * ATTN_SCALE, c, a, b)
        dwq_ref[...] += _dot_tn(cqn, dqf)
        dcq, dqg_rows = _rms_bwd(cq, qg_ref[...], rstd_q, _dot_nt(dqf, wq_ref[...]))
        dqg_ref[...] += _rowsum(dqg_rows)
        ckv = z[:, 768:896]
        ckvn, rstd_kv = _rms(ckv, kvg_ref[...])
        dk_, dv_ = dk_ref[...], dv_ref[...]
        dwk_ref[...] += _dot_tn(ckvn, dk_)
        dwv_ref[...] += _dot_tn(ckvn, dv_)
        dckv, dkvg_rows = _rms_bwd(ckv, kvg_ref[...], rstd_kv,
                                   _dot_nt(dk_, wk_ref[...]) + _dot_nt(dv_, wv_ref[...]))
        dkvg_ref[...] += _rowsum(dkvg_rows)
        dkr = dk_[:, :LANES]
        for h in range(1, MLA_HEADS):
            dkr = dkr + dk_[:, h * LANES:(h + 1) * LANES]
        lane = lax.broadcasted_iota(jnp.int32, (ts, LANES), 1)
        dkr = jnp.where((lane >= 64) & (lane < 96), _rope_bwd(dkr, c, a, b), 0.0)
        dz = jnp.concatenate([du, dcq, dckv, dkr], axis=1).astype(BF16)
        x_ = x_ref[...]
        h, rstd = _rms(x_, g_ref[...])
        dxn, dgr = _rms_bwd(x_, g_ref[...], rstd, _dot_nt(dz, win_ref[...]))
        dx_ref[...] = dy_ref[...] + dxn
        dg_ref[...] += _rowsum(dgr)
        _accumulate_tn(acc_ref, dwin_ref, h, dz, n)

    ins = [x, dy, z, z, dq, dk, dv, dyp, dyp, *tabs, g, win, pw, pscale, qg, wq, kvg, wk, wv]
    in_specs = [_row(ts, D), _row(ts, D), _row(ts, D), _prev(POOL_HALO, POOL_DIM, ts), _row(ts, D), _row(ts, D),
                _row(ts, D), _row(ts, POOL_DIM), _next(POOL_HALO, POOL_DIM, ts, n), _row(ts, LANES),
                _row(ts, LANES), _row(ts, LANES)] + [_const(v.shape) for v in ins[12:]]
    acc_shapes = [(1, D), (4, LANES, LANES), (1, POOL_DIM), (1, Q_LORA), (Q_LORA, D), (1, KV_LORA), (KV_LORA, D),
                  (KV_LORA, D)]
    return pl.pallas_call(
        body, name="even_pre_bwd", grid=(n,), in_specs=in_specs,
        out_specs=[_row(ts, D)] + [_acc(s) for s in acc_shapes] + [_acc((D, D))],
        out_shape=[_sds((S, D), F32)] + [_sds(s, F32) for s in acc_shapes] + [_sds((D, D), BF16)],
        scratch_shapes=[pltpu.VMEM((D, D), F32)], compiler_params=_cp())(*ins)


def _pick(n, options):
    for o in options:
        if n % o == 0:
            return o
    return n


def matmul_tn(name, a, b):
    out_dtype = BF16
    S = a.shape[-2]
    ts = _tile_rows(S, 4096 if a.dtype.itemsize == 2 and b.dtype.itemsize == 2 else 2048)
    steps = S // ts

    def body(a_ref, b_ref, o_ref, acc_ref):
        s = pl.program_id(2)

        @pl.when(s == 0)
        def _():
            acc_ref[...] = jnp.zeros_like(acc_ref)

        acc_ref[...] += _dot_tn(a_ref[...], b_ref[...])

        @pl.when(s == steps - 1)
        def _():
            o_ref[...] = acc_ref[...].astype(o_ref.dtype)

    if a.ndim == 3:
        C, _, K = a.shape
        N = b.shape[1]
        tn = _pick(N, (1024, 512, 256, 128))
        grid = (C, N // tn, S // ts)
        in_specs = [pl.BlockSpec((None, ts, K), lambda c, j, s: (c, s, 0)),
                    pl.BlockSpec((ts, tn), lambda c, j, s: (s, j))]
        out_spec, out_shape, tile = pl.BlockSpec((None, K, tn), lambda c, j, s: (c, 0, j)), (C, K, N), (K, tn)
    elif b.ndim == 3:
        C, _, N = b.shape
        K = a.shape[1]
        tk = _pick(K, (1024, 512, 256, 128))
        grid = (C, K // tk, S // ts)
        in_specs = [pl.BlockSpec((ts, tk), lambda c, i, s: (s, i)),
                    pl.BlockSpec((None, ts, N), lambda c, i, s: (c, s, 0))]
        out_spec, out_shape, tile = pl.BlockSpec((None, tk, N), lambda c, i, s: (c, i, 0)), (C, K, N), (tk, N)
    else:
        K, N = a.shape[1], b.shape[1]
        tk = _pick(K, (1024, 512, 256, 128))
        tn = _pick(N, (1024, 512, 256, 128))
        grid = (K // tk, N // tn, S // ts)
        in_specs = [pl.BlockSpec((ts, tk), lambda i, j, s: (s, i)), pl.BlockSpec((ts, tn), lambda i, j, s: (s, j))]
        out_spec, out_shape, tile = pl.BlockSpec((tk, tn), lambda i, j, s: (i, j)), (K, N), (tk, tn)
    return pl.pallas_call(
        body, name=name, grid=grid, in_specs=in_specs, out_specs=out_spec, out_shape=_sds(out_shape, out_dtype),
        scratch_shapes=[pltpu.VMEM(tile, F32)], compiler_params=pltpu.CompilerParams(dimension_semantics=("arbitrary",) * 3, vmem_limit_bytes=VMEM_LIMIT))(
            a, b)


def _my_id():
    return lax.axis_index("x") * 4 + lax.axis_index("y") * 2 + lax.axis_index("c")


def _peer(j):
    x, y, c = lax.axis_index("x"), lax.axis_index("y"), lax.axis_index("c")
    px = 1 - x if j & 4 else x
    py = 1 - y if j & 2 else y
    pc = 1 - c if j & 1 else c
    return (px, py, pc), px * 4 + py * 2 + pc


_HBM = pl.BlockSpec(memory_space=pltpu.HBM)
_SEM = pl.BlockSpec(memory_space=pltpu.SEMAPHORE)
_DATAFLOW = pltpu.SideEffectType.DATAFLOW_SIDE_EFFECTING


def _in_hbm(v):
    return pltpu.with_memory_space_constraint(v, pltpu.HBM)


N_PEERS = N_DEV - 1


def _split_copy(k, j, srcs, lands, send_sems, recv_sems, gather, slot):
    peer, pid = _peer(j)
    return pltpu.make_async_remote_copy(
        src_ref=srcs[k] if _flag(gather, k) else srcs[k].at[pid],
        dst_ref=lands[k].at[_my_id() if slot == "mine" else pid],
        send_sem=send_sems[j - 1], recv_sem=recv_sems[j - 1], device_id=peer, device_id_type=pl.DeviceIdType.MESH)


def _flag(gather, k):
    return gather[k] if isinstance(gather, tuple) else gather


def split_start(name, arrays, gather):
    n = len(arrays)
    lands = [lax.empty((N_DEV,) + a.shape if _flag(gather, k) else a.shape, a.dtype) for k, a in enumerate(arrays)]

    def body(*refs):
        srcs, lnds = refs[:n], refs[n:2 * n]
        sems = refs[4 * n:4 * n + 2 * N_PEERS]
        token = refs[-1]
        for j in range(1, N_DEV):
            for k in range(n):
                _split_copy(k, j, srcs, lnds, sems[:N_PEERS], sems[N_PEERS:], gather, "mine").start()
        token[...] = jnp.zeros_like(token)

    out = pl.pallas_call(
        body, name=name,
        out_shape=(*[pltpu.HBM(a.shape, a.dtype) for a in arrays], *[pltpu.HBM(l.shape, l.dtype) for l in lands],
                   *[pltpu.SemaphoreType.DMA(())] * (2 * N_PEERS), _sds((8, LANES), F32)),
        in_specs=[_HBM] * (2 * n),
        out_specs=(*[_HBM] * (2 * n), *[_SEM] * (2 * N_PEERS), pl.BlockSpec(memory_space=pltpu.VMEM)),
        input_output_aliases={k: k for k in range(2 * n)},
        compiler_params=pltpu.CompilerParams(has_side_effects=_DATAFLOW))(
            *[_in_hbm(a) for a in arrays], *[_in_hbm(l) for l in lands])
    sems = list(out[2 * n:2 * n + 2 * N_PEERS])
    return sems[:N_PEERS], sems[N_PEERS:], list(out[:n]), list(out[n:2 * n]), out[-1]


def split_wait(name, handle, after, gather):
    send_sems, recv_sems, srcs, lands, _ = handle
    n = len(srcs)

    def body(*refs):
        srcs_r, lnds_r = refs[:n], refs[n:2 * n]
        sems = refs[2 * n:2 * n + 2 * N_PEERS]
        for j in range(1, N_DEV):
            for k in range(n):
                cp = _split_copy(k, j, srcs_r, lnds_r, sems[:N_PEERS], sems[N_PEERS:], gather, "peer")
                cp.wait_send()
                cp.wait_recv()

    out = pl.pallas_call(
        body, name=name, out_shape=tuple(pltpu.HBM(a.shape, a.dtype) for a in srcs + lands),
        in_specs=[_HBM] * (2 * n) + [_SEM] * (2 * N_PEERS) + [pl.BlockSpec(memory_space=pl.ANY)],
        out_specs=tuple([_HBM] * (2 * n)), input_output_aliases={k: k for k in range(2 * n)},
        compiler_params=pltpu.CompilerParams(has_side_effects=_DATAFLOW))(
            *srcs, *lands, *send_sems, *recv_sems, after)
    return list(out[:n]), list(out[n:])


def _fill_own_slot(src, land, gather):
    me = _my_id()
    own = src[None] if gather else lax.dynamic_index_in_dim(src, me, 0, keepdims=True)
    return lax.dynamic_update_slice_in_dim(land, own, me, 0)


ADAMW_BLOCK_ELEMS = 128 * 1024


def adamw(name, parts, w, m, v, token=None):
    R, C = w.shape
    tr = _pick(R, [t for t in (512, 256, 128, 64, 32, 16, 8) if t * C <= ADAMW_BLOCK_ELEMS])
    c1 = 1.0 - ADAM_B1 ** ADAM_STEP
    c2 = 1.0 - ADAM_B2 ** ADAM_STEP
    extra, extra_specs = _after(token)

    def body(p_ref, w_ref, m_ref, v_ref, *rest):
        g_ref, d_ref, nm_ref, nv_ref = rest[-4:]
        g = p_ref[0].astype(F32)
        for s in range(1, N_DEV):
            g = g + p_ref[s].astype(F32)
        g_ref[...] = g
        m_ = ADAM_B1 * m_ref[...] + (1.0 - ADAM_B1) * g
        v_ = ADAM_B2 * v_ref[...] + (1.0 - ADAM_B2) * (g * g)
        nm_ref[...] = m_
        nv_ref[...] = v_
        d_ref[...] = -ADAM_LR * ((m_ / c1) / (jnp.sqrt(v_ / c2) + ADAM_EPS) + ADAM_WD * w_ref[...])

    row = pl.BlockSpec((tr, C), lambda i: (i, 0))
    return pl.pallas_call(
        body, name=name, grid=(R // tr,),
        in_specs=[pl.BlockSpec((N_DEV, tr, C), lambda i: (0, i, 0)), row, row, row] + extra_specs,
        out_specs=[row] * 4, out_shape=[_sds((R, C), F32)] * 4, compiler_params=_cp())(parts, w, m, v, *extra)


WEIGHTS = ['ev_norm', 'ev_w_in', 'ev_pool_w', 'ev_pool_scale', 'ev_q_norm', 'ev_w_q_up', 'ev_kv_norm', 'ev_w_kv_up',
           'ev_w_out', 'od_norm', 'od_w_in', 'od_conv_w', 'od_conv_b', 'od_w_rgate', 'od_b_rgate', 'od_w_igate',
           'od_b_igate', 'od_lambda', 'od_w_out', 'xa_norm_x', 'xa_norm_mem', 'xa_w_q', 'xa_w_kv', 'xa_w_o',
           'ffn_norm', 'ffn_w_gate_up', 'ffn_w_down', 'final_norm']
SHARD_AXIS = {'ev_w_in': 1, 'ev_w_q_up': 2, 'ev_w_kv_up': 2, 'ev_w_out': 1, 'od_norm': 1, 'od_w_in': 2,
              'od_conv_w': 2, 'od_conv_b': 1, 'od_w_rgate': 2, 'od_b_rgate': 1, 'od_w_igate': 2, 'od_b_igate': 1,
              'od_lambda': 1, 'od_w_out': 1, 'xa_w_q': 1, 'xa_w_kv': 2, 'xa_w_o': 1, 'ffn_w_gate_up': 2,
              'ffn_w_down': 1}
SMALL_F32 = ('od_norm', 'od_conv_w', 'od_conv_b', 'od_b_rgate', 'od_b_igate', 'od_lambda')
STACKED = ('ffn_w_gate_up', 'ffn_w_down')
SHARDED = [n for n in WEIGHTS if n in SHARD_AXIS]
REPLICATED = [n for n in WEIGHTS if n not in SHARD_AXIS]
ROW_ALIGN = 512


def _pack(flats, dtype):
    v = jnp.concatenate([f.reshape(-1).astype(dtype) for f in flats])
    pad = (-v.shape[0]) % (ROW_ALIGN * LANES)
    return jnp.pad(v, (0, pad)).reshape(-1, LANES)


def _rows8(n_elems):
    return -(-n_elems // (8 * LANES)) * 8


def _pack_rows(arrays, lead=False):
    out = []
    for a in arrays:
        r = a.reshape((N_DEV, -1, LANES) if lead else (-1, LANES))
        pad = _rows8(r.shape[-2] * LANES) - r.shape[-2]
        out.append(jnp.pad(r, [(0, 0)] * (r.ndim - 2) + [(0, pad), (0, 0)]))
    return jnp.concatenate(out, axis=-2)


def _unpack_rows(buf, shapes, lead=False):
    out, off = [], 0
    for s in shapes:
        n = 1
        for d in s:
            n *= d
        rows = buf[..., off:off + n // LANES, :]
        out.append(rows.reshape(((N_DEV,) if lead else ()) + tuple(s)))
        off += _rows8(n)
    return out


def _unpack(flat, shapes):
    out, off = [], 0
    v = flat.reshape(-1)
    for s in shapes:
        n = 1
        for d in s:
            n *= d
        out.append(v[off:off + n].reshape(s))
        off += n
    return out


def _to_full(stacked, axis):
    v = jnp.moveaxis(stacked, 0, axis)
    s = v.shape
    return v.reshape(s[:axis] + (s[axis] * s[axis + 1],) + s[axis + 2:])


def _to_shards(full, axis):
    s = full.shape
    v = full.reshape(s[:axis] + (N_DEV, s[axis] // N_DEV) + s[axis + 1:])
    return jnp.moveaxis(v, axis, 0)


def _pad_heads(w, nh, dh, lead):
    s = w.shape
    v = w.reshape(s[:-1] + (nh, dh))
    v = jnp.pad(v, [(0, 0)] * (len(s) - 1) + [(0, 0), (lead, LANES - dh - lead)])
    return v.reshape(s[:-1] + (nh * LANES,))


def _unpad_heads(w, nh, dh, lead):
    s = w.shape
    return w.reshape(s[:-1] + (nh, LANES))[..., lead:lead + dh].reshape(s[:-1] + (nh * dh,))


def _rope_tables(positions):
    inv_freq = 10000.0 ** (-jnp.arange(0, 32, 2, dtype=F32) / 32)
    ang = positions.astype(F32)[:, None] * inv_freq
    cos, sin = jnp.tile(jnp.cos(ang), (1, LANES // 16)), jnp.tile(jnp.sin(ang), (1, LANES // 16))
    lane = lax.broadcasted_iota(jnp.int32, cos.shape, 1)
    c = jnp.where((lane >= 64) & (lane < 96), cos, 1.0)
    a = jnp.where((lane >= 80) & (lane < 96), sin, 0.0)
    b = jnp.where((lane >= 64) & (lane < 80), -sin, 0.0)
    return c, a, b


def _t(w):
    return jnp.swapaxes(w, -1, -2)


def device_step(x, mem, positions, target, W, fwd_token=None, late_weights=None, ship_grads=None,
                first_weights=None):
    G = {}
    tabs = _rope_tables(positions)
    keep = (positions != 0).astype(F32)[:, None]
    row = lambda v: v.reshape(1, -1)
    if first_weights is not None:
        W = {**W, **first_weights(tabs[0])}

    w_in = W['ev_w_in'][0]
    ev_win = jnp.concatenate([w_in[:, :896], _pad_heads(w_in[:, 896:], 1, 32, 64)], axis=1)
    ev_wq = _pad_heads(W['ev_w_q_up'][0], MLA_HEADS, QK_DIM, 0)
    kvw = W['ev_w_kv_up'][0].reshape(KV_LORA, MLA_HEADS, 128)
    ev_wk = _pad_heads(kvw[:, :, :64].reshape(KV_LORA, 512), MLA_HEADS, 64, 0)
    ev_wv = _pad_heads(kvw[:, :, 64:].reshape(KV_LORA, 512), MLA_HEADS, 64, 0)
    ev_wo_pool = W['ev_w_out'][0][:POOL_DIM]
    ev_wo_att = _t(_pad_heads(_t(W['ev_w_out'][0][POOL_DIM:]), MLA_HEADS, 64, 0))
    pw = W['ev_pool_w'][0].astype(BF16)
    ev_g, ps, qg, kvg = row(W['ev_norm'][0]), row(W['ev_pool_scale'][0]), row(W['ev_q_norm'][0]), row(W['ev_kv_norm'][0])

    z0, qp, kp, vp, ypool = even_pre(x, tabs, ev_g, ev_win, pw, ps, qg, ev_wq, kvg, ev_wk, ev_wv)
    o_att, lse = attn_fwd(qp, kp, vp, fwd_token)
    if late_weights is not None:
        W = {**W, **late_weights(lse)}
    x1 = even_post(x, ypool, o_att, ev_wo_pool, ev_wo_att)

    def xa_ffn_fwd(xin, l, head=()):
        mn, km, vm = mem_kv(mem, row(W['xa_norm_mem'][l]), W['xa_w_kv'][l])
        xm = xattn_fwd(xin, row(W['xa_norm_x'][l]), W['xa_w_q'][l], km, vm, W['xa_w_o'][l])
        *xo, hf, gu = (ffn_fwd_loss if head else ffn_fwd)(
            xm, row(W['ffn_norm'][l]), W['ffn_w_gate_up'], l, W['ffn_w_down'][:, l].reshape(FF_HALF, FF_CHUNK, D),
            *head)
        return xm, (xo if head else xo[0]), (mn, km, vm, hf, gu)

    x2, x3, memkv0 = xa_ffn_fwd(x1, 0)

    od_g, lam = row(W['od_norm'][0]), row(W['od_lambda'][0])
    cw, cb = W['od_conv_w'][0], row(W['od_conv_b'][0])
    wr, wi = W['od_w_rgate'][0], W['od_w_igate'][0]
    br, bi = row(W['od_b_rgate'][0]), row(W['od_b_igate'][0])
    z1, a_t, b_t, xb1, r1, ig1 = odd_pre(x3, keep, od_g, W['od_w_in'][0], cw, cb, wr, br, wi, bi, lam)
    hseq = lru_scan(a_t, b_t)
    x4 = odd_post(x3, z1, hseq, W['od_w_out'][0])
    x5, (dx, g_final, loss), memkv1 = xa_ffn_fwd(x4, 1, (target, row(W['final_norm'])))
    G['final_norm'] = g_final.reshape(D)

    gnx, gnm, gwq, gwkv, gwo, gfn, gwgu, gwd = ([None, None] for _ in range(8))

    def xa_ffn_bwd(dy, xin, xm, memkv, l):
        mn, km, vm, hf, gu = memkv
        fg = row(W['ffn_norm'][l])
        dxm, dfg, act, dgu = ffn_bwd(xm, dy, gu, fg, W['ffn_w_gate_up'], l,
                                     W['ffn_w_down'][:, l].reshape(FF_HALF, FF_CHUNK, D))
        gwd[l] = matmul_tn("ffn_dwd", act, dy).reshape(N_DEV, D_FF // N_DEV, D)
        gwgu[l] = matmul_tn("ffn_dwgu", hf, dgu)
        gfn[l] = dfg[0]
        dxin, o, dq, hx, dgx, dk, dv = xattn_bwd(xin, dxm, row(W['xa_norm_x'][l]), W['xa_w_q'][l], km, vm,
                                                  W['xa_w_o'][l])
        gnx[l] = dgx[0]
        gwo[l] = matmul_tn("xa_dwo", o, dxm)
        gwq[l] = matmul_tn("xa_dwq", hx, dq)
        dkv, dgm = mem_bwd(mem, row(W['xa_norm_mem'][l]), dk, dv, W['xa_w_kv'][l])
        gnm[l] = dgm[0]
        gwkv[l] = matmul_tn("xa_dwkv", mn, dkv)
        return dxin

    dx4 = xa_ffn_bwd(dx, x4, x5, memkv1, 1)

    dgate, dhs, g_od_wout = odd_post_bwd(dx4, z1, hseq, W['od_w_out'][0])
    G['od_w_out'] = g_od_wout[None]
    lam_grad = lru_scan(a_t, dhs, reverse=True)
    dxb, dcb, dbr, dbi, dlam, dwr, dwi = odd_gates_bwd(xb1, r1, ig1, lam_grad, hseq, keep, wr, wi, lam)
    dx3, dcw, dg_od, g_od_win = odd_pre_bwd(x3, dx4, z1, dxb, dgate, od_g, cw, W['od_w_in'][0])
    G['od_w_in'] = g_od_win[None]
    G['od_norm'], G['od_conv_w'], G['od_conv_b'] = dg_od, dcw[None], dcb
    G['od_w_rgate'], G['od_b_rgate'], G['od_w_igate'], G['od_b_igate'], G['od_lambda'] = (
        dwr[None], dbr, dwi[None], dbi, dlam)

    dx1 = xa_ffn_bwd(dx3, x1, x2, memkv0, 0)
    G['xa_norm_x'], G['xa_norm_mem'], G['ffn_norm'] = jnp.stack(gnx), jnp.stack(gnm), jnp.stack(gfn)
    G['xa_w_q'], G['xa_w_kv'], G['xa_w_o'] = jnp.stack(gwq), jnp.stack(gwkv), jnp.stack(gwo)
    G['ffn_w_gate_up'], G['ffn_w_down'] = jnp.stack(gwgu, axis=1), jnp.stack(gwd, axis=1)
    bwd_token = ship_grads(G) if ship_grads is not None else None

    dyp, do_att, delta, g_wo_pool, g_wo_att = even_post_bwd(dx1, ypool, o_att, ev_wo_pool, ev_wo_att)
    G['ev_w_out'] = jnp.concatenate([g_wo_pool, _t(_unpad_heads(_t(g_wo_att), MLA_HEADS, 64, 0))], axis=0)[None]
    dq, dk, dv = attn_bwd(qp, kp, vp, do_att, lse, delta, bwd_token)
    (grad_x, dg_ev, dpw, dps, dqg, dwq, dkvg, dwk, dwv, g_win) = even_pre_bwd(
        x, dx1, z0, dq, dk, dv, dyp, tabs, ev_g, ev_win, pw, ps, qg, ev_wq, kvg, ev_wk, ev_wv)
    G['ev_w_in'] = jnp.concatenate([g_win[:, :896], _unpad_heads(g_win[:, 896:], 1, 32, 64)], axis=1)[None]
    G['ev_norm'], G['ev_pool_w'], G['ev_pool_scale'], G['ev_q_norm'], G['ev_kv_norm'] = (
        dg_ev, dpw[None], dps, dqg, dkvg)
    G['ev_w_q_up'] = _unpad_heads(dwq, MLA_HEADS, QK_DIM, 0)[None]
    gk = _unpad_heads(dwk, MLA_HEADS, 64, 0).reshape(KV_LORA, MLA_HEADS, 64)
    gv = _unpad_heads(dwv, MLA_HEADS, 64, 0).reshape(KV_LORA, MLA_HEADS, 64)
    G['ev_w_kv_up'] = jnp.concatenate([gk, gv], axis=2).reshape(1, KV_LORA, MLA_HEADS * 128)
    return loss[0, 0], grad_x, G


def kernel(x, mem, positions, ev_norm, ev_w_in, ev_pool_w, ev_pool_scale, ev_q_norm, ev_w_q_up, ev_kv_norm, ev_w_kv_up, ev_w_out, od_norm, od_w_in, od_conv_w, od_conv_b, od_w_rgate, od_b_rgate, od_w_igate, od_b_igate, od_lambda, od_w_out, xa_norm_x, xa_norm_mem, xa_w_q, xa_w_kv, xa_w_o, ffn_norm, ffn_w_gate_up, ffn_w_down, final_norm, loss_target, m_ev_norm, m_ev_w_in, m_ev_pool_w, m_ev_pool_scale, m_ev_q_norm, m_ev_w_q_up, m_ev_kv_norm, m_ev_w_kv_up, m_ev_w_out, m_od_norm, m_od_w_in, m_od_conv_w, m_od_conv_b, m_od_w_rgate, m_od_b_rgate, m_od_w_igate, m_od_b_igate, m_od_lambda, m_od_w_out, m_xa_norm_x, m_xa_norm_mem, m_xa_w_q, m_xa_w_kv, m_xa_w_o, m_ffn_norm, m_ffn_w_gate_up, m_ffn_w_down, m_final_norm, v_ev_norm, v_ev_w_in, v_ev_pool_w, v_ev_pool_scale, v_ev_q_norm, v_ev_w_q_up, v_ev_kv_norm, v_ev_w_kv_up, v_ev_w_out, v_od_norm, v_od_w_in, v_od_conv_w, v_od_conv_b, v_od_w_rgate, v_od_b_rgate, v_od_w_igate, v_od_b_igate, v_od_lambda, v_od_w_out, v_xa_norm_x, v_xa_norm_mem, v_xa_w_q, v_xa_w_kv, v_xa_w_o, v_ffn_norm, v_ffn_w_gate_up, v_ffn_w_down, v_final_norm):
    args = dict(locals())
    w = {n: args[n] for n in WEIGHTS}
    m = {n: args['m_' + n] for n in WEIGHTS}
    v = {n: args['v_' + n] for n in WEIGHTS}
    big = [n for n in SHARDED if n not in SMALL_F32]
    small = [n for n in SHARDED if n in SMALL_F32]

    small_shapes = [w[n].shape for n in small]
    first = [n for n in big if n.startswith('ev_')]
    late = [n for n in big if n not in first]

    def full(n, st):
        return st if n in STACKED else _to_full(st, SHARD_AXIS[n])

    W = {n: w[n] for n in REPLICATED}
    gather_first = split_start("first_start", [w[n].astype(BF16) for n in first], True)

    def first_weights(after):
        srcs, lands = split_wait("first_wait", gather_first, after, True)
        return {n: full(n, _fill_own_slot(s, l, True)) for n, s, l in zip(first, srcs, lands)}

    gather = split_start("gather_start", [w[n].astype(BF16) for n in late] + [_pack_rows([w[n] for n in small])], True)

    def late_weights(after):
        srcs, lands = split_wait("gather_wait", gather, after, True)
        lands = [_fill_own_slot(s, l, True) for s, l in zip(srcs, lands)]
        out = {n: full(n, st) for n, st in zip(late, lands)}
        out.update((n, _to_full(st, SHARD_AXIS[n])) for n, st in zip(small, _unpack_rows(lands[-1], small_shapes, True)))
        return out

    def shards(G, n):
        return G[n] if n in STACKED else _to_shards(G[n], SHARD_AXIS[n])

    shipped = []

    def ship_grads(G):
        shipped.append(split_start("exchange_start", [shards(G, n).astype(BF16) for n in late] +
                                   [_pack_rows([shards(G, n) for n in small], lead=True)], False))
        return shipped[0][-1]

    loss, grad_x, G = device_step(x[0], mem[0], positions[0], loss_target[0], W, gather[-1], late_weights, ship_grads,
                                  first_weights)
    outs = [{}, {}, {}, {}]

    last_flags = (False,) * len(first) + (True,)
    last = split_start("last_start", [shards(G, n).astype(BF16) for n in first] + [_pack(
        [G[n] for n in REPLICATED] + [jnp.broadcast_to(loss, (LANES,))], F32)], last_flags)

    two_d = lambda a: a.reshape(-1, a.shape[-1])

    def update(names, parts):
        prev = None
        for n, p in zip(names, parts):
            res = adamw("adamw_" + n, p.reshape((N_DEV,) + two_d(w[n]).shape), two_d(w[n]), two_d(m[n]),
                        two_d(v[n]), prev)
            prev = res[0]
            for k in range(4):
                outs[k][n] = res[k].reshape(w[n].shape)
        return prev

    srcs, lands = split_wait("exchange_wait", shipped[0], last[-1], False)
    late_parts = [_fill_own_slot(s, l, False) for s, l in zip(srcs, lands)]
    after = update(late, late_parts)
    res = adamw("adamw_small", late_parts[-1], *[_pack_rows([d[n] for n in small]) for d in (w, m, v)])
    for k in range(4):
        outs[k].update(zip(small, _unpack_rows(res[k], small_shapes)))

    srcs, lands = split_wait("last_wait", last, after, last_flags)
    *first_parts, rep_parts = [_fill_own_slot(s, l, f) for s, l, f in zip(srcs, lands, last_flags)]
    update(first, first_parts)

    rep_shapes = [w[n].shape for n in REPLICATED] + [(LANES,)]
    zero = jnp.zeros((LANES,), F32)
    rep = adamw("adamw_rep", rep_parts, *[_pack([d[n] for n in REPLICATED] + [zero], F32) for d in (w, m, v)])
    for k in range(4):
        outs[k].update(zip(REPLICATED + ['loss'], _unpack(rep[k], rep_shapes)))
    loss = outs[0]['loss'][0]

    return (loss, grad_x[None], *[outs[0][n] for n in WEIGHTS], *[outs[1][n] for n in WEIGHTS],
            *[outs[2][n] for n in WEIGHTS], *[outs[3][n] for n in WEIGHTS])
```
